```python
import jax, jax.numpy as jnp
from jax import lax
import numpy as np

D_MODEL = 1024
BATCH = 4
SEQ = 4096
DEPTH = 1

D_MIX = D_MODEL
LRU_WIDTH = 512
LRU_BLOCKS = 8
LRU_BLOCK = LRU_WIDTH // LRU_BLOCKS
LRU_C = 8.0
CONV_WIDTH = 4
GDN_HEADS = 4
GDN_DK = 128
GDN_DV = 128
GDN_CHUNK = 64
GDN_QK = GDN_HEADS * GDN_DK
GDN_V = GDN_HEADS * GDN_DV
IN_SPLITS = (LRU_WIDTH, LRU_WIDTH, GDN_QK, GDN_QK, GDN_V, GDN_V, GDN_HEADS, GDN_HEADS)
D_IN = LRU_WIDTH * 2 + GDN_QK * 2 + GDN_V * 2 + GDN_HEADS * 2
IN_OFFSETS = (LRU_WIDTH, 2 * LRU_WIDTH, 2 * LRU_WIDTH + GDN_QK, 2 * LRU_WIDTH + 2 * GDN_QK,
              2 * LRU_WIDTH + 2 * GDN_QK + GDN_V, 2 * LRU_WIDTH + 2 * GDN_QK + 2 * GDN_V,
              2 * LRU_WIDTH + 2 * GDN_QK + 2 * GDN_V + GDN_HEADS)
N_EXPERTS = 256
TOP_K = 8
N_GROUPS = 8
TOPK_GROUPS = 4
D_EXPERT = 256
D_SHARED = 256
ROUTED_SCALE = 2.5
MOE_BLOCK = 128
LN_EPS = 1e-5
NORM_EPS = 1e-6
DEEPNORM_ALPHA = (2.0 * DEPTH) ** 0.25
DEEPNORM_BETA = (8.0 * DEPTH) ** -0.25

kernel_name = 'hybrid_rglru_gdn_moe_deepnorm'


def layer_norm(x, g, b):
    xf = x.astype(jnp.float32)
    mu = jnp.mean(xf, -1, keepdims=True)
    var = jnp.mean(jnp.square(xf - mu), -1, keepdims=True)
    return ((xf - mu) * lax.rsqrt(var + LN_EPS) * g + b).astype(x.dtype)


def rms_norm(x, g):
    xf = x.astype(jnp.float32)
    return xf * lax.rsqrt(jnp.mean(jnp.square(xf), -1, keepdims=True) + NORM_EPS) * g


def l2_normalize(x):
    return x * lax.rsqrt(jnp.sum(jnp.square(x), -1, keepdims=True) + NORM_EPS)


def causal_depthwise_conv(x, w):
    seq = x.shape[1]
    xp = jnp.pad(x, ((0, 0), (CONV_WIDTH - 1, 0), (0, 0)))
    return sum(xp[:, j:j + seq] * w[j] for j in range(CONV_WIDTH))


def _lru_combine(c1, c2):
    a1, b1 = c1
    a2, b2 = c2
    return a1 * a2, a2 * b1 + b2


def rg_lru_group(u, gate, conv_w, conv_b, w_rg, b_rg, w_ig, b_ig, lam, out_g):
    bsz, seq = u.shape[0], u.shape[1]
    f32 = jnp.float32
    xc = causal_depthwise_conv(u.astype(f32), conv_w.astype(f32)) + conv_b
    xb = xc.reshape(bsz, seq, LRU_BLOCKS, LRU_BLOCK)
    r = jax.nn.sigmoid(jnp.einsum('bsgi,gij->bsgj', xb, w_rg).reshape(bsz, seq, LRU_WIDTH) + b_rg)
    i = jax.nn.sigmoid(jnp.einsum('bsgi,gij->bsgj', xb, w_ig).reshape(bsz, seq, LRU_WIDTH) + b_ig)
    log_a = -LRU_C * r * jax.nn.softplus(-lam.astype(f32))
    a = jnp.exp(log_a)
    mult = jnp.sqrt(-jnp.expm1(2.0 * log_a))
    _, h = lax.associative_scan(_lru_combine, (a, mult * (i * xc)), axis=1)
    y = h * jax.nn.gelu(gate.astype(f32))
    return rms_norm(y, out_g)


def gated_deltanet_group(q, k, v, z, b_logit, a_logit, conv_w, a_log, dt_bias, norm_w):
    bsz, seq = q.shape[0], q.shape[1]
    n_chunks = seq // GDN_CHUNK
    f32 = jnp.float32
    qkv = jnp.concatenate([q, k, v], -1).astype(f32)
    qkv = jax.nn.silu(causal_depthwise_conv(qkv, conv_w.astype(f32)))
    q, k, v = jnp.split(qkv, [GDN_QK, 2 * GDN_QK], -1)
    q = l2_normalize(q.reshape(bsz, seq, GDN_HEADS, GDN_DK)) * (GDN_DK ** -0.5)
    k = l2_normalize(k.reshape(bsz, seq, GDN_HEADS, GDN_DK))
    v = v.reshape(bsz, seq, GDN_HEADS, GDN_DV)
    beta = jax.nn.sigmoid(b_logit.astype(f32))
    g = -jnp.exp(a_log.astype(f32)) * jax.nn.softplus(a_logit.astype(f32) + dt_bias)

    def to_chunks(t):
        t = t.reshape((bsz, n_chunks, GDN_CHUNK) + t.shape[2:])
        return jnp.moveaxis(t, 3, 1)

    qc, kc, vc, bc, gc = to_chunks(q), to_chunks(k), to_chunks(v), to_chunks(beta), to_chunks(g)
    gc = jnp.cumsum(gc, -1)
    causal = jnp.tril(jnp.ones((GDN_CHUNK, GDN_CHUNK), bool))
    strict = jnp.tril(jnp.ones((GDN_CHUNK, GDN_CHUNK), bool), -1)
    decay = jnp.exp(jnp.where(causal, gc[..., :, None] - gc[..., None, :], -jnp.inf))
    kb = kc * bc[..., None]
    vb = vc * bc[..., None]
    a_mat = jnp.where(strict, jnp.einsum('bhnid,bhnjd->bhnij', kb, kc) * decay, 0.0)
    eye = jnp.eye(GDN_CHUNK, dtype=f32)
    rhs = jnp.concatenate([vb, kb * jnp.exp(gc)[..., None]], -1)
    sol = lax.linalg.triangular_solve(a_mat + eye, rhs, left_side=True, lower=True, unit_diagonal=True)
    u_val, w_key = sol[..., :GDN_DV], sol[..., GDN_DV:]
    qk = jnp.where(causal, jnp.einsum('bhnid,bhnjd->bhnij', qc, kc) * decay, 0.0)
    q_dec = qc * jnp.exp(gc)[..., None]
    g_last = gc[..., -1]
    k_dec = kc * jnp.exp(g_last[..., None] - gc)[..., None]
    xs = tuple(jnp.moveaxis(t, 2, 0) for t in (u_val, w_key, qk, q_dec, k_dec, jnp.exp(g_last)))

    def chunk_step(state, inp):
        u_n, w_n, qk_n, qd_n, kd_n, gl_n = inp
        v_new = u_n - jnp.einsum('bhcd,bhde->bhce', w_n, state)
        o = jnp.einsum('bhcd,bhde->bhce', qd_n, state) + jnp.einsum('bhij,bhje->bhie', qk_n, v_new)
        state = state * gl_n[..., None, None] + jnp.einsum('bhcd,bhce->bhde', kd_n, v_new)
        return state, o

    state0 = jnp.zeros((bsz, GDN_HEADS, GDN_DK, GDN_DV), f32)
    _, o = lax.scan(chunk_step, state0, xs)
    o = jnp.transpose(o, (1, 0, 3, 2, 4)).reshape(bsz, seq, GDN_HEADS, GDN_DV)
    zh = z.astype(f32).reshape(bsz, seq, GDN_HEADS, GDN_DV)
    return (rms_norm(o, norm_w) * jax.nn.silu(zh)).reshape(bsz, seq, GDN_V)


def moe_ffn(x, w_router, router_bias, w_gate, w_up, w_down, ws_gate, ws_up, ws_down):
    bsz, seq, d = x.shape
    n_tok = bsz * seq
    f32 = jnp.float32
    xf = x.reshape(n_tok, d)
    scores = jax.nn.sigmoid((xf @ w_router).astype(f32))
    choice = scores + router_bias
    grp_score = lax.top_k(choice.reshape(n_tok, N_GROUPS, N_EXPERTS // N_GROUPS), 2)[0].sum(-1)
    _, top_grp = lax.top_k(grp_score, TOPK_GROUPS)
    grp_mask = jnp.any(top_grp[..., None] == jnp.arange(N_GROUPS), axis=1)
    masked = jnp.where(jnp.repeat(grp_mask, N_EXPERTS // N_GROUPS, axis=1), choice, -jnp.inf)
    _, top_e = lax.top_k(masked, TOP_K)
    wts = jnp.take_along_axis(scores, top_e, 1)
    wts = wts / (jnp.sum(wts, -1, keepdims=True) + 1e-20) * ROUTED_SCALE

    n_assign = n_tok * TOP_K
    flat_e = top_e.reshape(n_assign)
    flat_t = jnp.repeat(jnp.arange(n_tok, dtype=jnp.int32), TOP_K)
    flat_w = wts.reshape(n_assign)
    order = jnp.argsort(flat_e)
    se, st, sw = flat_e[order], flat_t[order], flat_w[order]
    counts = jnp.bincount(flat_e, length=N_EXPERTS)
    padded = (counts + MOE_BLOCK - 1) // MOE_BLOCK * MOE_BLOCK
    pad_end = jnp.cumsum(padded)
    pad_start = pad_end - padded
    sort_start = jnp.cumsum(counts) - counts
    dest = pad_start[se] + (jnp.arange(n_assign) - sort_start[se])
    n_blocks = -(-n_assign // MOE_BLOCK) + N_EXPERTS
    n_rows = n_blocks * MOE_BLOCK
    buf_t = jnp.full((n_rows,), n_tok, jnp.int32).at[dest].set(st)
    buf_w = jnp.zeros((n_rows,), f32).at[dest].set(sw)
    blk_e = jnp.minimum(jnp.searchsorted(pad_end, jnp.arange(n_blocks) * MOE_BLOCK, side='right'),
                        N_EXPERTS - 1)
    x_pad = jnp.concatenate([xf, jnp.zeros((1, d), xf.dtype)], 0)

    def expert_block(args):
        tok, e = args
        xb = x_pad[tok]
        h = jax.nn.silu(xb @ w_gate[e]) * (xb @ w_up[e])
        return h @ w_down[e]

    y = lax.map(expert_block, (buf_t.reshape(n_blocks, MOE_BLOCK), blk_e))
    y = y.reshape(n_rows, d).astype(f32) * buf_w[:, None]
    routed = jnp.zeros((n_tok + 1, d), f32).at[buf_t].add(y)[:n_tok]
    shared = (jax.nn.silu(xf @ ws_gate) * (xf @ ws_up)) @ ws_down
    return (routed + shared.astype(f32)).reshape(bsz, seq, d).astype(x.dtype)


def setup_inputs(seed: int = 0) -> dict:
    key = jax.random.key(seed)
    ks = jax.random.split(key, 32)
    f32 = jnp.float32
    L = DEPTH

    def nrm(k, shape, scale):
        return jax.random.normal(k, shape, f32) * scale

    def gain(k, shape):
        return 1.0 + 0.02 * jax.random.normal(k, shape, f32)

    col_scale = jnp.concatenate([
        jnp.full((LRU_WIDTH,), DEEPNORM_BETA, f32), jnp.ones((LRU_WIDTH + 2 * GDN_QK,), f32),
        jnp.full((GDN_V,), DEEPNORM_BETA, f32), jnp.ones((GDN_V + 2 * GDN_HEADS,), f32)])
    u_lam = jax.random.uniform(ks[9], (L, LRU_WIDTH), f32, 0.9, 0.999)
    a_base = u_lam ** (1.0 / LRU_C)
    dt = jnp.exp(jax.random.uniform(ks[13], (L, GDN_HEADS), f32, np.log(1e-3), np.log(1e-1)))
    return {
        'x': jax.random.normal(ks[0], (BATCH, SEQ, D_MODEL), f32),
        'ln_in_g': gain(ks[1], (D_MODEL,)),
        'ln_in_b': nrm(ks[2], (D_MODEL,), 0.02),
        'w_in': nrm(ks[3], (L, D_MODEL, D_IN), D_MODEL ** -0.5) * col_scale,
        'lru_conv_w': nrm(ks[4], (L, CONV_WIDTH, LRU_WIDTH), CONV_WIDTH ** -0.5),
        'lru_conv_b': nrm(ks[5], (L, LRU_WIDTH), 0.02),
        'lru_w_rg': nrm(ks[6], (L, LRU_BLOCKS, LRU_BLOCK, LRU_BLOCK), LRU_BLOCK ** -0.5),
        'lru_b_rg': nrm(ks[7], (L, LRU_WIDTH), 0.02),
        'lru_w_ig': nrm(ks[8], (L, LRU_BLOCKS, LRU_BLOCK, LRU_BLOCK), LRU_BLOCK ** -0.5),
        'lru_b_ig': nrm(ks[10], (L, LRU_WIDTH), 0.02),
        'lru_lambda': jnp.log(a_base) - jnp.log1p(-a_base),
        'lru_out_g': gain(ks[11], (L, LRU_WIDTH)),
        'gdn_conv_w': nrm(ks[12], (L, CONV_WIDTH, 2 * GDN_QK + GDN_V), CONV_WIDTH ** -0.5),
        'gdn_a_log': jnp.log(jax.random.uniform(ks[14], (L, GDN_HEADS), f32, 1.0, 16.0)),
        'gdn_dt_bias': dt + jnp.log(-jnp.expm1(-dt)),
        'gdn_norm_w': gain(ks[15], (L, GDN_DV)),
        'w_out': nrm(ks[16], (L, D_MIX, D_MODEL), D_MIX ** -0.5 * DEEPNORM_BETA),
        'ln1_g': gain(ks[17], (L, D_MODEL)),
        'ln1_b': nrm(ks[18], (L, D_MODEL), 0.02),
        'w_router': nrm(ks[19], (L, D_MODEL, N_EXPERTS), D_MODEL ** -0.5),
        'router_bias': nrm(ks[20], (L, N_EXPERTS), 0.01),
        'w_gate': nrm(ks[21], (L, N_EXPERTS, D_MODEL, D_EXPERT), D_MODEL ** -0.5),
        'w_up': nrm(ks[22], (L, N_EXPERTS, D_MODEL, D_EXPERT), D_MODEL ** -0.5),
        'w_down': nrm(ks[23], (L, N_EXPERTS, D_EXPERT, D_MODEL), D_EXPERT ** -0.5 * DEEPNORM_BETA),
        'ws_gate': nrm(ks[24], (L, D_MODEL, D_SHARED), D_MODEL ** -0.5),
        'ws_up': nrm(ks[25], (L, D_MODEL, D_SHARED), D_MODEL ** -0.5),
        'ws_down': nrm(ks[26], (L, D_SHARED, D_MODEL), D_SHARED ** -0.5 * DEEPNORM_BETA),
        'ln2_g': gain(ks[27], (L, D_MODEL)),
        'ln2_b': nrm(ks[28], (L, D_MODEL), 0.02),
    }


def reference(x, ln_in_g, ln_in_b, w_in, lru_conv_w, lru_conv_b, lru_w_rg, lru_b_rg, lru_w_ig,
              lru_b_ig, lru_lambda, lru_out_g, gdn_conv_w, gdn_a_log, gdn_dt_bias, gdn_norm_w,
              w_out, ln1_g, ln1_b, w_router, router_bias, w_gate, w_up, w_down, ws_gate, ws_up,
              ws_down, ln2_g, ln2_b):
    h = layer_norm(x, ln_in_g, ln_in_b)
    for l in range(DEPTH):
        proj = h @ w_in[l]
        u, gate, q, k, v, z, b_logit, a_logit = jnp.split(proj, IN_OFFSETS, -1)
        y_lru = rg_lru_group(u, gate, lru_conv_w[l], lru_conv_b[l], lru_w_rg[l], lru_b_rg[l],
                             lru_w_ig[l], lru_b_ig[l], lru_lambda[l], lru_out_g[l])
        y_gdn = gated_deltanet_group(q, k, v, z, b_logit, a_logit, gdn_conv_w[l], gdn_a_log[l],
                                     gdn_dt_bias[l], gdn_norm_w[l])
        mix = jnp.concatenate([y_lru, y_gdn], -1).astype(h.dtype) @ w_out[l]
        h = layer_norm(DEEPNORM_ALPHA * h + mix, ln1_g[l], ln1_b[l])
        ffn = moe_ffn(h, w_router[l], router_bias[l], w_gate[l], w_up[l], w_down[l],
                      ws_gate[l], ws_up[l], ws_down[l])
        h = layer_norm(DEEPNORM_ALPHA * h + ffn, ln2_g[l], ln2_b[l])
    return h
```

```python
import functools

import jax
import jax.numpy as jnp
from jax import lax
from jax.experimental import pallas as pl
from jax.experimental.pallas import tpu as pltpu

F32 = jnp.float32
BF16 = jnp.bfloat16

D_MODEL = 1024
LRU_WIDTH = 512
LRU_BLOCKS = 8
LRU_C = 8.0
CONV_WIDTH = 4
GDN_HEADS = 4
GDN_DK = 128
GDN_DV = 128
GDN_CHUNK = 64
GDN_QK = GDN_HEADS * GDN_DK
GDN_V = GDN_HEADS * GDN_DV
N_MAIN = 2 * LRU_WIDTH + 2 * GDN_QK + 2 * GDN_V
N_EXPERTS = 256
TOP_K = 8
N_GROUPS = 8
GROUP_SIZE = N_EXPERTS // N_GROUPS
TOPK_GROUPS = 4
D_EXPERT = 256
D_SHARED = 256
ROUTED_SCALE = 2.5
MOE_BLOCK = 128
LN_EPS = 1e-5
NORM_EPS = 1e-6
DEPTH = 1
DEEPNORM_ALPHA = (2.0 * DEPTH) ** 0.25

HALO = 8
LANES = 128
VMEM_LIMIT = 56 * 1024 * 1024

NN = (((1,), (0,)), ((), ()))
NT = (((1,), (1,)), ((), ()))
TN = (((0,), (0,)), ((), ()))


def _dot(a, b, dims=NN):
    return lax.dot_general(a, b, dims, preferred_element_type=F32)


def _split(a):
    hi = a.astype(BF16)
    lo = (a - hi.astype(F32)).astype(BF16)
    return hi, lo


def _dot3(a, b, dims=NN):
    ah, al = _split(a)
    bh, bl = _split(b)
    return _dot(ah, bh, dims) + (_dot(ah, bl, dims) + _dot(al, bh, dims))


def _layer_norm(x, g, b):
    mu = jnp.mean(x, -1, keepdims=True)
    xc = x - mu
    var = jnp.mean(xc * xc, -1, keepdims=True)
    return xc * lax.rsqrt(var + LN_EPS) * g + b


def _sigmoid(x):
    return 1.0 / (1.0 + jnp.exp(-x))


def _silu(x):
    return x * _sigmoid(x)


def _softplus(x):
    return jnp.maximum(x, 0.0) + jnp.log1p(jnp.exp(-jnp.abs(x)))


def _gelu_tanh(x):
    c = 0.7978845608028654
    return x * (0.5 * (1.0 + jnp.tanh(c * (x + 0.044715 * (x * x * x)))))


def _params(sem, **kw):
    return pltpu.CompilerParams(dimension_semantics=sem, vmem_limit_bytes=VMEM_LIMIT, **kw)


def _inproj_kernel(x_ref, g_ref, b_ref, w_ref, ws_ref, wst_ref, h_ref, proj_ref, small_ref, smallt_ref):
    h = _layer_norm(x_ref[...], g_ref[...], b_ref[...])
    h_ref[...] = h
    hb = h.astype(BF16)
    proj_ref[...] = _dot(hb, w_ref[...])
    small_ref[...] = _dot3(h, ws_ref[...])
    smallt_ref[...] = _dot3(wst_ref[...], h, NT)


def _inproj(x2d, g, b, w_main, w_small, w_small_t, tm):
    t = x2d.shape[0]
    return pl.pallas_call(
        _inproj_kernel,
        grid=(t // tm,),
        in_specs=[
            pl.BlockSpec((tm, D_MODEL), lambda i: (i, 0)),
            pl.BlockSpec((1, D_MODEL), lambda i: (0, 0)),
            pl.BlockSpec((1, D_MODEL), lambda i: (0, 0)),
            pl.BlockSpec((D_MODEL, N_MAIN), lambda i: (0, 0)),
            pl.BlockSpec((D_MODEL, LANES), lambda i: (0, 0)),
            pl.BlockSpec((8, D_MODEL), lambda i: (0, 0)),
        ],
        out_specs=[
            pl.BlockSpec((tm, D_MODEL), lambda i: (i, 0)),
            pl.BlockSpec((tm, N_MAIN), lambda i: (i, 0)),
            pl.BlockSpec((tm, LANES), lambda i: (i, 0)),
            pl.BlockSpec((8, tm), lambda i: (0, i)),
        ],
        out_shape=[
            jax.ShapeDtypeStruct((t, D_MODEL), F32),
            jax.ShapeDtypeStruct((t, N_MAIN), F32),
            jax.ShapeDtypeStruct((t, LANES), F32),
            jax.ShapeDtypeStruct((8, t), F32),
        ],
        compiler_params=_params(("arbitrary",)),
        name="ln_inproj",
    )(x2d, g, b, w_main, w_small, w_small_t)


def _causal_conv(buf_ref, x, w_ref, first, rows):
    @pl.when(first)
    def _():
        buf_ref[0:HALO, :] = jnp.zeros((HALO, buf_ref.shape[1]), F32)

    buf_ref[HALO:HALO + rows, :] = x
    acc = None
    for j in range(CONV_WIDTH):
        off = HALO - (CONV_WIDTH - 1) + j
        term = buf_ref[off:off + rows, :] * w_ref[j:j + 1, :]
        acc = term if acc is None else acc + term
    buf_ref[0:HALO, :] = buf_ref[rows:rows + HALO, :]
    return acc


def _shift_rows(x, d, fill):
    rows = x.shape[0]
    if d % 8 == 0:
        pad = jnp.full((d, x.shape[1]), fill, x.dtype)
        return jnp.concatenate([pad, x[:rows - d]], axis=0)
    rolled = pltpu.roll(x, d, 0)
    row = lax.broadcasted_iota(jnp.int32, x.shape, 0)
    return jnp.where(row < d, fill, rolled)


def _lru_kernel(u_ref, gate_ref, cw_ref, cb_ref, wg_ref, bg_ref, lam_ref, og_ref,
                y_ref, ubuf, hcarry):
    s = pl.program_id(1)
    rows = u_ref.shape[0]

    @pl.when(s == 0)
    def _():
        hcarry[...] = jnp.zeros_like(hcarry)

    xc = _causal_conv(ubuf, u_ref[...], cw_ref, s == 0, rows) + cb_ref[...]
    gates = _dot(xc.astype(BF16), wg_ref[...]) + bg_ref[...]
    r = _sigmoid(gates[:, :LRU_WIDTH])
    i = _sigmoid(gates[:, LRU_WIDTH:])
    log_a = (-LRU_C) * r * _softplus(-lam_ref[...])
    a = jnp.exp(log_a)
    mult = jnp.sqrt(-jnp.tanh(log_a) * (a * a + 1.0))
    bv = mult * (i * xc)
    d = 1
    while d < rows:
        a_sh = _shift_rows(a, d, 1.0)
        b_sh = _shift_rows(bv, d, 0.0)
        bv = a * b_sh + bv
        a = a * a_sh
        d *= 2
    h = a * hcarry[...] + bv
    hcarry[...] = h[rows - 1:rows, :]
    y = h * _gelu_tanh(gate_ref[...])
    ms = jnp.mean(y * y, -1, keepdims=True)
    y_ref[...] = y * lax.rsqrt(ms + NORM_EPS) * og_ref[...]


def _lru(proj3, conv_w, conv_b, w_gates, b_gates, lam, out_g, ts):
    bsz, seq, _ = proj3.shape
    row = lambda n: pl.BlockSpec((1, n), lambda b, s: (0, 0))
    return pl.pallas_call(
        _lru_kernel,
        grid=(bsz, seq // ts),
        in_specs=[
            pl.BlockSpec((None, ts, LRU_WIDTH), lambda b, s: (b, s, 0)),
            pl.BlockSpec((None, ts, LRU_WIDTH), lambda b, s: (b, s, 1)),
            pl.BlockSpec((CONV_WIDTH, LRU_WIDTH), lambda b, s: (0, 0)),
            row(LRU_WIDTH),
            pl.BlockSpec((LRU_WIDTH, 2 * LRU_WIDTH), lambda b, s: (0, 0)),
            row(2 * LRU_WIDTH),
            row(LRU_WIDTH),
            row(LRU_WIDTH),
        ],
        out_specs=pl.BlockSpec((None, ts, LRU_WIDTH), lambda b, s: (b, s, 0)),
        out_shape=jax.ShapeDtypeStruct((bsz, seq, LRU_WIDTH), F32),
        scratch_shapes=[
            pltpu.VMEM((HALO + ts, LRU_WIDTH), F32),
            pltpu.VMEM((1, LRU_WIDTH), F32),
        ],
        compiler_params=_params(("arbitrary", "arbitrary")),
        name="rg_lru",
    )(proj3, proj3, conv_w, conv_b, w_gates, b_gates, lam, out_g)


def _gdn_kernel(q_ref, k_ref, v_ref, z_ref, sm_ref, smt_ref, cwq_ref, cwk_ref, cwv_ref,
                alr_ref, dtr_ref, alc_ref, dtc_ref, nw_ref, y_ref, qbuf, kbuf, vbuf, state):
    n = pl.program_id(1)
    c = GDN_CHUNK
    first = n == 0

    @pl.when(first)
    def _():
        state[...] = jnp.zeros_like(state)

    q_all = _silu(_causal_conv(qbuf, q_ref[...], cwq_ref, first, c))
    k_all = _silu(_causal_conv(kbuf, k_ref[...], cwk_ref, first, c))
    v_all = _silu(_causal_conv(vbuf, v_ref[...], cwv_ref, first, c))

    sm = sm_ref[...]
    beta_all = _sigmoid(sm)
    g_cols = -jnp.exp(alr_ref[...]) * _softplus(sm + dtr_ref[...])
    g_rows = -jnp.exp(alc_ref[...]) * _softplus(smt_ref[...] + dtc_ref[...])

    ri = lax.broadcasted_iota(jnp.int32, (c, c), 0)
    ci = lax.broadcasted_iota(jnp.int32, (c, c), 1)
    causal = ri >= ci
    strict = ri > ci

    for hd in range(GDN_HEADS):
        sl = slice(hd * GDN_DK, (hd + 1) * GDN_DK)
        q = q_all[:, sl]
        k = k_all[:, sl]
        v = v_all[:, sl]
        q = q * lax.rsqrt(jnp.sum(q * q, -1, keepdims=True) + NORM_EPS) * (GDN_DK ** -0.5)
        k = k * lax.rsqrt(jnp.sum(k * k, -1, keepdims=True) + NORM_EPS)
        beta = beta_all[:, hd:hd + 1]
        g_col = g_cols[:, GDN_HEADS + hd:GDN_HEADS + hd + 1]
        g_row = g_rows[GDN_HEADS + hd:GDN_HEADS + hd + 1, :]
        gc_col = jnp.sum(jnp.where(causal, g_row, 0.0), axis=1, keepdims=True)
        gc_row = jnp.sum(jnp.where(ri <= ci, g_col, 0.0), axis=0, keepdims=True)
        decay = jnp.exp(jnp.where(causal, gc_col - gc_row, -jnp.inf))
        kb = k * beta
        vb = v * beta
        a_mat = jnp.where(strict, _dot3(kb, k, NT) * decay, 0.0)
        rhs = jnp.concatenate([vb, kb * jnp.exp(gc_col)], axis=1)
        sol = rhs - _dot3(a_mat, rhs)
        p = a_mat
        for _ in range(5):
            p = _dot3(p, p)
            sol = sol + _dot3(p, sol)
        u_val = sol[:, :GDN_DV]
        w_key = sol[:, GDN_DV:]
        qk = jnp.where(causal, _dot3(q, k, NT) * decay, 0.0)
        q_dec = q * jnp.exp(gc_col)
        g_last = gc_col[c - 1:c, :]
        k_dec = k * jnp.exp(g_last - gc_col)
        st = state[hd]
        v_new = u_val - _dot3(w_key, st)
        o = _dot3(q_dec, st) + _dot3(qk, v_new)
        state[hd] = st * jnp.exp(g_last) + _dot3(k_dec, v_new, TN)
        ms = jnp.mean(o * o, -1, keepdims=True)
        o = o * lax.rsqrt(ms + NORM_EPS) * nw_ref[...]
        y_ref[:, sl] = o * _silu(z_ref[:, sl])


def _gdn(proj3, small3, smallt3, cwq, cwk, cwv, alr, dtr, alc, dtc, norm_w):
    bsz, seq, _ = proj3.shape
    c = GDN_CHUNK
    nch = seq // c
    col = lambda j: pl.BlockSpec((None, c, GDN_QK), lambda b, n: (b, n, j))
    const = lambda shape: pl.BlockSpec(shape, lambda b, n: (0,) * len(shape))
    return pl.pallas_call(
        _gdn_kernel,
        grid=(bsz, nch),
        in_specs=[
            col(2), col(3), col(4), col(5),
            pl.BlockSpec((None, c, LANES), lambda b, n: (b, n, 0)),
            pl.BlockSpec((None, None, 8, c), lambda b, n: (b, n, 0, 0)),
            const((CONV_WIDTH, GDN_QK)), const((CONV_WIDTH, GDN_QK)), const((CONV_WIDTH, GDN_V)),
            const((1, LANES)), const((1, LANES)), const((8, 1)), const((8, 1)),
            const((1, GDN_DV)),
        ],
        out_specs=pl.BlockSpec((None, c, GDN_V), lambda b, n: (b, n, 0)),
        out_shape=jax.ShapeDtypeStruct((bsz, seq, GDN_V), F32),
        scratch_shapes=[
            pltpu.VMEM((HALO + c, GDN_QK), F32),
            pltpu.VMEM((HALO + c, GDN_QK), F32),
            pltpu.VMEM((HALO + c, GDN_V), F32),
            pltpu.VMEM((GDN_HEADS, GDN_DK, GDN_DV), F32),
        ],
        compiler_params=_params(("arbitrary", "arbitrary")),
        name="gated_deltanet",
    )(proj3, proj3, proj3, proj3, small3, smallt3, cwq, cwk, cwv, alr, dtr, alc, dtc, norm_w)


def _router_kernel(yl_ref, yg_ref, h0_ref, wo1_ref, wo2_ref, g_ref, b_ref, wrt_ref, rb_ref,
                   h1_ref, e_ref, w_ref, rank_ref, cnt_ref, carry):
    i = pl.program_id(0)
    tm = h0_ref.shape[0]

    @pl.when(i == 0)
    def _():
        carry[...] = jnp.zeros_like(carry)

    mix = _dot(yl_ref[...].astype(BF16), wo1_ref[...]) + _dot(yg_ref[...].astype(BF16), wo2_ref[...])
    h1 = _layer_norm(DEEPNORM_ALPHA * h0_ref[...] + mix, g_ref[...], b_ref[...])
    h1_ref[...] = h1

    scores = _sigmoid(_dot3(wrt_ref[...], h1, NT))
    choice = scores + rb_ref[...]
    neg = -jnp.inf
    gs_rows = []
    sub = lax.broadcasted_iota(jnp.int32, (GROUP_SIZE, tm), 0).astype(F32)
    for g in range(N_GROUPS):
        cg = choice[g * GROUP_SIZE:(g + 1) * GROUP_SIZE, :]
        m1 = jnp.max(cg, axis=0, keepdims=True)
        i1 = jnp.min(jnp.where(cg == m1, sub, float(GROUP_SIZE)), axis=0, keepdims=True)
        m2 = jnp.max(jnp.where(sub == i1, neg, cg), axis=0, keepdims=True)
        gs_rows.append(m1 + m2)
    gs = jnp.concatenate(gs_rows, axis=0)
    gi = lax.broadcasted_iota(jnp.int32, (N_GROUPS, tm), 0).astype(F32)
    gsel = jnp.zeros((N_GROUPS, tm), jnp.bool_)
    for _ in range(TOPK_GROUPS):
        m = jnp.max(gs, axis=0, keepdims=True)
        idx = jnp.min(jnp.where(gs == m, gi, float(N_GROUPS)), axis=0, keepdims=True)
        hit = gi == idx
        gsel = jnp.logical_or(gsel, hit)
        gs = jnp.where(hit, neg, gs)
    masked = jnp.concatenate(
        [jnp.where(gsel[g:g + 1, :], choice[g * GROUP_SIZE:(g + 1) * GROUP_SIZE, :], neg)
         for g in range(N_GROUPS)], axis=0)

    ei = lax.broadcasted_iota(jnp.int32, (N_EXPERTS, tm), 0).astype(F32)
    hits = []
    e_rows, w_rows = [], []
    multi = jnp.zeros((N_EXPERTS, tm), F32)
    for _ in range(TOP_K):
        m = jnp.max(masked, axis=0, keepdims=True)
        idx = jnp.min(jnp.where(masked == m, ei, float(N_EXPERTS)), axis=0, keepdims=True)
        hit = ei == idx
        hits.append(hit)
        e_rows.append(idx)
        w_rows.append(jnp.sum(jnp.where(hit, scores, 0.0), axis=0, keepdims=True))
        multi = multi + hit.astype(F32)
        masked = jnp.where(hit, neg, masked)
    wts = jnp.concatenate(w_rows, axis=0)
    wts = wts / (jnp.sum(wts, axis=0, keepdims=True) + 1e-20) * ROUTED_SCALE
    ti = lax.broadcasted_iota(jnp.int32, (tm, tm), 0)
    tj = lax.broadcasted_iota(jnp.int32, (tm, tm), 1)
    before = (ti < tj).astype(BF16)
    cum = _dot(multi.astype(BF16), before) + carry[...]
    r_rows = [jnp.sum(jnp.where(hit, cum, 0.0), axis=0, keepdims=True) for hit in hits]
    carry[...] = carry[...] + jnp.sum(multi, axis=1, keepdims=True)
    e_ref[...] = jnp.concatenate(e_rows, axis=0).astype(jnp.int32)
    w_ref[...] = wts
    rank_ref[...] = jnp.concatenate(r_rows, axis=0).astype(jnp.int32)
    cnt_ref[...] = carry[...].astype(jnp.int32)


def _router(y_lru, y_gdn, h0, wo1, wo2, g, b, w_router_t, rbias, tm):
    t = h0.shape[0]
    const = lambda shape: pl.BlockSpec(shape, lambda i: (0,) * len(shape))
    return pl.pallas_call(
        _router_kernel,
        grid=(t // tm,),
        in_specs=[
            pl.BlockSpec((tm, LRU_WIDTH), lambda i: (i, 0)),
            pl.BlockSpec((tm, GDN_V), lambda i: (i, 0)),
            pl.BlockSpec((tm, D_MODEL), lambda i: (i, 0)),
            const((LRU_WIDTH, D_MODEL)), const((GDN_V, D_MODEL)),
            const((1, D_MODEL)), const((1, D_MODEL)),
            const((N_EXPERTS, D_MODEL)), const((N_EXPERTS, 1)),
        ],
        out_specs=[
            pl.BlockSpec((tm, D_MODEL), lambda i: (i, 0)),
            pl.BlockSpec((TOP_K, tm), lambda i: (0, i)),
            pl.BlockSpec((TOP_K, tm), lambda i: (0, i)),
            pl.BlockSpec((TOP_K, tm), lambda i: (0, i)),
            const((N_EXPERTS, 1)),
        ],
        out_shape=[
            jax.ShapeDtypeStruct((t, D_MODEL), F32),
            jax.ShapeDtypeStruct((TOP_K, t), jnp.int32),
            jax.ShapeDtypeStruct((TOP_K, t), F32),
            jax.ShapeDtypeStruct((TOP_K, t), jnp.int32),
            jax.ShapeDtypeStruct((N_EXPERTS, 1), jnp.int32),
        ],
        scratch_shapes=[pltpu.VMEM((N_EXPERTS, 1), F32)],
        compiler_params=_params(("arbitrary",)),
        name="outproj_router",
    )(y_lru, y_gdn, h0, wo1, wo2, g, b, w_router_t, rbias)


def _row_copy(src_hbm, dst_hbm, src_row, dst_row, sem):
    return pltpu.make_async_copy(src_hbm.at[pl.ds(src_row, 1), :], dst_hbm.at[pl.ds(dst_row, 1), :], sem)


def _dispatch_kernel(dest_ref, x_hbm, init_hbm, xs_hbm, sem):
    del init_hbm
    i = pl.program_id(0)
    tm = dest_ref.shape[1]
    base = i * tm

    def issue(t, carry):
        for k in range(TOP_K):
            _row_copy(x_hbm, xs_hbm, base + t, dest_ref[k, t], sem).start()
        return carry

    lax.fori_loop(0, tm, issue, 0)

    def drain(t, carry):
        for k in range(TOP_K):
            _row_copy(x_hbm, xs_hbm, base + t, dest_ref[k, t], sem).wait()
        return carry

    lax.fori_loop(0, tm, drain, 0)


def _dispatch(dest, x2d, n_rows, tm):
    t = x2d.shape[0]
    init = jnp.zeros((n_rows, D_MODEL), F32)
    return pl.pallas_call(
        _dispatch_kernel,
        grid=(t // tm,),
        in_specs=[
            pl.BlockSpec((TOP_K, tm), lambda i: (0, i), memory_space=pltpu.SMEM),
            pl.BlockSpec(memory_space=pl.ANY),
            pl.BlockSpec(memory_space=pl.ANY),
        ],
        out_specs=pl.BlockSpec(memory_space=pl.ANY),
        out_shape=jax.ShapeDtypeStruct((n_rows, D_MODEL), F32),
        scratch_shapes=[pltpu.SemaphoreType.DMA],
        input_output_aliases={2: 0},
        compiler_params=_params(("arbitrary",), has_side_effects=True),
        name="moe_dispatch",
    )(dest, x2d, init)


def _expert_kernel(be_ref, nu_ref, xs_ref, wg_ref, wu_ref, wd_ref, ys_ref, wgu_b, wd_b):
    i = pl.program_id(0)

    @pl.when(i < nu_ref[0])
    def _():
        prev = be_ref[jnp.maximum(i - 1, 0)]

        @pl.when(jnp.logical_or(i == 0, be_ref[i] != prev))
        def _():
            wgu_b[:, :D_EXPERT] = wg_ref[...].astype(BF16)
            wgu_b[:, D_EXPERT:] = wu_ref[...].astype(BF16)
            wd_b[...] = wd_ref[...].astype(BF16)

        gu = _dot(xs_ref[...].astype(BF16), wgu_b[...])
        h = _silu(gu[:, :D_EXPERT]) * gu[:, D_EXPERT:]
        ys_ref[...] = _dot(h.astype(BF16), wd_b[...])


def _experts(blk_e, n_used, xs, w_gate, w_up, w_down):
    n_rows = xs.shape[0]
    n_blocks = n_rows // MOE_BLOCK
    blk = lambda i, be, nu: (jnp.minimum(i, nu[0] - 1), 0)
    exp = lambda i, be, nu: (be[i], 0, 0)
    return pl.pallas_call(
        _expert_kernel,
        grid_spec=pltpu.PrefetchScalarGridSpec(
            num_scalar_prefetch=2,
            grid=(n_blocks,),
            in_specs=[
                pl.BlockSpec((MOE_BLOCK, D_MODEL), blk),
                pl.BlockSpec((None, D_MODEL, D_EXPERT), exp),
                pl.BlockSpec((None, D_MODEL, D_EXPERT), exp),
                pl.BlockSpec((None, D_EXPERT, D_MODEL), exp),
            ],
            out_specs=pl.BlockSpec((MOE_BLOCK, D_MODEL), blk),
            scratch_shapes=[
                pltpu.VMEM((D_MODEL, 2 * D_EXPERT), BF16),
                pltpu.VMEM((D_EXPERT, D_MODEL), BF16),
            ],
        ),
        out_shape=jax.ShapeDtypeStruct((n_rows, D_MODEL), F32),
        compiler_params=_params(("arbitrary",)),
        name="moe_experts",
    )(blk_e, n_used, xs, w_gate, w_up, w_down)


def _combine_kernel(dest_ref, h1_ref, wts_ref, wsgu_ref, wsd_ref, g_ref, b_ref, ys_hbm, out_ref, buf, sem):
    tm = h1_ref.shape[0]

    def issue(t, carry):
        for k in range(TOP_K):
            pltpu.make_async_copy(ys_hbm.at[pl.ds(dest_ref[k, t], 1), :],
                                  buf.at[k, pl.ds(t, 1), :], sem).start()
        return carry

    lax.fori_loop(0, tm, issue, 0)

    h1 = h1_ref[...]
    gu = _dot(h1.astype(BF16), wsgu_ref[...])
    hs = _silu(gu[:, :D_SHARED]) * gu[:, D_SHARED:]
    acc = DEEPNORM_ALPHA * h1 + _dot(hs.astype(BF16), wsd_ref[...])

    def drain(t, carry):
        for k in range(TOP_K):
            pltpu.make_async_copy(ys_hbm.at[pl.ds(dest_ref[k, t], 1), :],
                                  buf.at[k, pl.ds(t, 1), :], sem).wait()
        return carry

    lax.fori_loop(0, tm, drain, 0)

    wts = wts_ref[...]
    for k in range(TOP_K):
        acc = acc + buf[k] * wts[:, k:k + 1]
    out_ref[...] = _layer_norm(acc, g_ref[...], b_ref[...])


def _combine(dest, h1, wts_t, ws_gu, ws_down, g, b, ys, tm):
    t = h1.shape[0]
    const = lambda shape: pl.BlockSpec(shape, lambda i: (0,) * len(shape))
    return pl.pallas_call(
        _combine_kernel,
        grid=(t // tm,),
        in_specs=[
            pl.BlockSpec((TOP_K, tm), lambda i: (0, i), memory_space=pltpu.SMEM),
            pl.BlockSpec((tm, D_MODEL), lambda i: (i, 0)),
            pl.BlockSpec((tm, TOP_K), lambda i: (i, 0)),
            const((D_MODEL, 2 * D_SHARED)), const((D_SHARED, D_MODEL)),
            const((1, D_MODEL)), const((1, D_MODEL)),
            pl.BlockSpec(memory_space=pl.ANY),
        ],
        out_specs=pl.BlockSpec((tm, D_MODEL), lambda i: (i, 0)),
        out_shape=jax.ShapeDtypeStruct((t, D_MODEL), F32),
        scratch_shapes=[
            pltpu.VMEM((TOP_K, tm, D_MODEL), F32),
            pltpu.SemaphoreType.DMA,
        ],
        compiler_params=_params(("arbitrary",)),
        name="moe_combine",
    )(dest, h1, wts_t, ws_gu, ws_down, g, b, ys)


def _block_diag(w):
    nb, bi, bo = w.shape
    eye = jnp.eye(nb, dtype=w.dtype)
    return (eye[:, None, :, None] * w[:, :, None, :]).reshape(nb * bi, nb * bo)


def _pad_lanes(v, offset, width):
    return jnp.zeros((1, width), F32).at[0, offset:offset + v.shape[0]].set(v)


def _layer(h_in_x, l, p, tiles):
    bsz, seq, _ = h_in_x.shape
    t = bsz * seq
    row = lambda v: v.reshape(1, -1)

    w_in = p['w_in'][l]
    w_main = w_in[:, :N_MAIN].astype(BF16)
    w_small = jnp.zeros((D_MODEL, LANES), F32).at[:, :2 * GDN_HEADS].set(w_in[:, N_MAIN:])
    w_small_t = w_in[:, N_MAIN:].T
    h0, proj, small, small_t = _inproj(h_in_x.reshape(t, D_MODEL), row(p['ln_g']), row(p['ln_b']),
                                       w_main, w_small, w_small_t, tiles['inproj'])
    proj3 = proj.reshape(bsz, seq, N_MAIN)

    w_gates = jnp.concatenate([_block_diag(p['lru_w_rg'][l]), _block_diag(p['lru_w_ig'][l])], 1).astype(BF16)
    b_gates = jnp.concatenate([p['lru_b_rg'][l], p['lru_b_ig'][l]]).reshape(1, -1)
    y_lru = _lru(proj3, p['lru_conv_w'][l], row(p['lru_conv_b'][l]), w_gates, b_gates,
                 row(p['lru_lambda'][l]), row(p['lru_out_g'][l]), tiles['lru'])

    nch = seq // GDN_CHUNK
    small3 = small.reshape(bsz, seq, LANES)
    smallt3 = small_t.reshape(8, bsz, nch, GDN_CHUNK).transpose(1, 2, 0, 3)
    cw = p['gdn_conv_w'][l]
    a_log, dt_bias = p['gdn_a_log'][l], p['gdn_dt_bias'][l]
    alr = _pad_lanes(a_log, GDN_HEADS, LANES)
    dtr = _pad_lanes(dt_bias, GDN_HEADS, LANES)
    alc = _pad_lanes(a_log, GDN_HEADS, 8).reshape(8, 1)
    dtc = _pad_lanes(dt_bias, GDN_HEADS, 8).reshape(8, 1)
    y_gdn = _gdn(proj3, small3, smallt3, cw[:, :GDN_QK], cw[:, GDN_QK:2 * GDN_QK], cw[:, 2 * GDN_QK:],
                 alr, dtr, alc, dtc, row(p['gdn_norm_w'][l]))

    w_out = p['w_out'][l].astype(BF16)
    h1, top_e, wts, rank, counts = _router(
        y_lru.reshape(t, LRU_WIDTH), y_gdn.reshape(t, GDN_V), h0, w_out[:LRU_WIDTH], w_out[LRU_WIDTH:],
        row(p['ln1_g'][l]), row(p['ln1_b'][l]), p['w_router'][l].T, p['router_bias'][l].reshape(-1, 1),
        tiles['router'])

    counts = counts[:, 0]
    padded = (counts + MOE_BLOCK - 1) // MOE_BLOCK * MOE_BLOCK
    pad_end = jnp.cumsum(padded)
    pad_start = pad_end - padded
    n_blocks = (t * TOP_K) // MOE_BLOCK + N_EXPERTS
    n_rows = n_blocks * MOE_BLOCK
    dest = pad_start[top_e] + rank
    n_used = (pad_end[-1] // MOE_BLOCK).astype(jnp.int32)
    blk_ids = jnp.minimum(jnp.arange(n_blocks, dtype=jnp.int32), n_used - 1)
    blk_e = jnp.minimum(jnp.searchsorted(pad_end, blk_ids * MOE_BLOCK, side='right'),
                        N_EXPERTS - 1).astype(jnp.int32)

    xs = _dispatch(dest, h1, n_rows, tiles['dispatch'])
    ys = _experts(blk_e, n_used.reshape(1), xs, p['w_gate'][l], p['w_up'][l], p['w_down'][l])
    ws_gu = jnp.concatenate([p['ws_gate'][l], p['ws_up'][l]], 1).astype(BF16)
    out = _combine(dest, h1, wts.T, ws_gu, p['ws_down'][l].astype(BF16),
                   row(p['ln2_g'][l]), row(p['ln2_b'][l]), ys, tiles['combine'])
    return out.reshape(bsz, seq, D_MODEL)


def _tiles(bsz, seq):
    t = bsz * seq
    return {
        'inproj': min(256, t),
        'lru': min(256, seq),
        'router': min(256, t),
        'dispatch': min(512, t),
        'combine': min(128, t),
    }


def kernel(x, ln_in_g, ln_in_b, w_in, lru_conv_w, lru_conv_b, lru_w_rg, lru_b_rg, lru_w_ig, lru_b_ig,
           lru_lambda, lru_out_g, gdn_conv_w, gdn_a_log, gdn_dt_bias, gdn_norm_w, w_out, ln1_g, ln1_b,
           w_router, router_bias, w_gate, w_up, w_down, ws_gate, ws_up, ws_down, ln2_g, ln2_b):
    assert w_in.shape[0] == DEPTH == 1
    p = dict(ln_g=ln_in_g, ln_b=ln_in_b, w_in=w_in, lru_conv_w=lru_conv_w, lru_conv_b=lru_conv_b,
             lru_w_rg=lru_w_rg, lru_b_rg=lru_b_rg, lru_w_ig=lru_w_ig, lru_b_ig=lru_b_ig,
             lru_lambda=lru_lambda, lru_out_g=lru_out_g, gdn_conv_w=gdn_conv_w, gdn_a_log=gdn_a_log,
             gdn_dt_bias=gdn_dt_bias, gdn_norm_w=gdn_norm_w, w_out=w_out, ln1_g=ln1_g, ln1_b=ln1_b,
             w_router=w_router, router_bias=router_bias, w_gate=w_gate, w_up=w_up, w_down=w_down,
             ws_gate=ws_gate, ws_up=ws_up, ws_down=ws_down, ln2_g=ln2_g, ln2_b=ln2_b)
    bsz, seq, _ = x.shape
    return _layer(x, 0, p, _tiles(bsz, seq))
```

```python
import functools

import jax
import jax.numpy as jnp
from jax import lax
from jax.experimental import pallas as pl
from jax.experimental.pallas import tpu as pltpu

F32 = jnp.float32
BF16 = jnp.bfloat16

D_MODEL = 1024
LRU_WIDTH = 512
LRU_BLOCKS = 8
LRU_C = 8.0
CONV_WIDTH = 4
GDN_HEADS = 4
GDN_DK = 128
GDN_DV = 128
GDN_CHUNK = 64
GDN_QK = GDN_HEADS * GDN_DK
GDN_V = GDN_HEADS * GDN_DV
N_MAIN = 2 * LRU_WIDTH + 2 * GDN_QK + 2 * GDN_V
N_EXPERTS = 256
TOP_K = 8
N_GROUPS = 8
GROUP_SIZE = N_EXPERTS // N_GROUPS
TOPK_GROUPS = 4
D_EXPERT = 256
D_SHARED = 256
ROUTED_SCALE = 2.5
MOE_BLOCK = 128
LN_EPS = 1e-5
NORM_EPS = 1e-6
DEPTH = 1
DEEPNORM_ALPHA = (2.0 * DEPTH) ** 0.25

HALO = 8
LANES = 128
VMEM_LIMIT = 56 * 1024 * 1024

NN = (((1,), (0,)), ((), ()))
NT = (((1,), (1,)), ((), ()))
TN = (((0,), (0,)), ((), ()))


def _dot(a, b, dims=NN):
    return lax.dot_general(a, b, dims, preferred_element_type=F32)


def _split(a):
    hi = a.astype(BF16)
    lo = (a - hi.astype(F32)).astype(BF16)
    return hi, lo


def _dot3(a, b, dims=NN):
    ah, al = _split(a)
    bh, bl = _split(b)
    return _dot(ah, bh, dims) + (_dot(ah, bl, dims) + _dot(al, bh, dims))


def _layer_norm(x, g, b):
    mu = jnp.mean(x, -1, keepdims=True)
    xc = x - mu
    var = jnp.mean(xc * xc, -1, keepdims=True)
    return xc * lax.rsqrt(var + LN_EPS) * g + b


def _sigmoid(x):
    return 1.0 / (1.0 + jnp.exp(-x))


def _silu(x):
    return x * _sigmoid(x)


def _softplus(x):
    return jnp.maximum(x, 0.0) + jnp.log1p(jnp.exp(-jnp.abs(x)))


def _gelu_tanh(x):
    c = 0.7978845608028654
    return x * (0.5 * (1.0 + jnp.tanh(c * (x + 0.044715 * (x * x * x)))))


def _params(sem, **kw):
    return pltpu.CompilerParams(dimension_semantics=sem, vmem_limit_bytes=VMEM_LIMIT, **kw)


def _inproj_kernel(x_ref, g_ref, b_ref, w_ref, ws_ref, wst_ref, h_ref, proj_ref, small_ref, smallt_ref):
    h = _layer_norm(x_ref[...], g_ref[...], b_ref[...])
    h_ref[...] = h
    hb = h.astype(BF16)
    proj_ref[...] = _dot(hb, w_ref[...])
    small_ref[...] = _dot3(h, ws_ref[...])
    smallt_ref[...] = _dot3(wst_ref[...], h, NT)


def _inproj(x2d, g, b, w_main, w_small, w_small_t, tm):
    t = x2d.shape[0]
    return pl.pallas_call(
        _inproj_kernel,
        grid=(t // tm,),
        in_specs=[
            pl.BlockSpec((tm, D_MODEL), lambda i: (i, 0)),
            pl.BlockSpec((1, D_MODEL), lambda i: (0, 0)),
            pl.BlockSpec((1, D_MODEL), lambda i: (0, 0)),
            pl.BlockSpec((D_MODEL, N_MAIN), lambda i: (0, 0)),
            pl.BlockSpec((D_MODEL, LANES), lambda i: (0, 0)),
            pl.BlockSpec((8, D_MODEL), lambda i: (0, 0)),
        ],
        out_specs=[
            pl.BlockSpec((tm, D_MODEL), lambda i: (i, 0)),
            pl.BlockSpec((tm, N_MAIN), lambda i: (i, 0)),
            pl.BlockSpec((tm, LANES), lambda i: (i, 0)),
            pl.BlockSpec((8, tm), lambda i: (0, i)),
        ],
        out_shape=[
            jax.ShapeDtypeStruct((t, D_MODEL), F32),
            jax.ShapeDtypeStruct((t, N_MAIN), F32),
            jax.ShapeDtypeStruct((t, LANES), F32),
            jax.ShapeDtypeStruct((8, t), F32),
        ],
        compiler_params=_params(("arbitrary",)),
        name="ln_inproj",
    )(x2d, g, b, w_main, w_small, w_small_t)


def _causal_conv(buf_ref, x, w_ref, first, rows):
    @pl.when(first)
    def _():
        buf_ref[0:HALO, :] = jnp.zeros((HALO, buf_ref.shape[1]), F32)

    buf_ref[HALO:HALO + rows, :] = x
    acc = None
    for j in range(CONV_WIDTH):
        off = HALO - (CONV_WIDTH - 1) + j
        term = buf_ref[off:off + rows, :] * w_ref[j:j + 1, :]
        acc = term if acc is None else acc + term
    buf_ref[0:HALO, :] = buf_ref[rows:rows + HALO, :]
    return acc


def _shift_rows(x, d, fill):
    rows = x.shape[0]
    if d % 8 == 0:
        pad = jnp.full((d, x.shape[1]), fill, x.dtype)
        return jnp.concatenate([pad, x[:rows - d]], axis=0)
    rolled = pltpu.roll(x, d, 0)
    row = lax.broadcasted_iota(jnp.int32, x.shape, 0)
    return jnp.where(row < d, fill, rolled)


def _lru_kernel(u_ref, gate_ref, cw_ref, cb_ref, wg_ref, bg_ref, lam_ref, og_ref,
                y_ref, ubuf, hcarry):
    s = pl.program_id(1)
    rows = u_ref.shape[0]

    @pl.when(s == 0)
    def _():
        hcarry[...] = jnp.zeros_like(hcarry)

    xc = _causal_conv(ubuf, u_ref[...], cw_ref, s == 0, rows) + cb_ref[...]
    gates = _dot(xc.astype(BF16), wg_ref[...]) + bg_ref[...]
    r = _sigmoid(gates[:, :LRU_WIDTH])
    i = _sigmoid(gates[:, LRU_WIDTH:])
    log_a = (-LRU_C) * r * _softplus(-lam_ref[...])
    a = jnp.exp(log_a)
    mult = jnp.sqrt(-jnp.tanh(log_a) * (a * a + 1.0))
    bv = mult * (i * xc)
    d = 1
    while d < rows:
        a_sh = _shift_rows(a, d, 1.0)
        b_sh = _shift_rows(bv, d, 0.0)
        bv = a * b_sh + bv
        a = a * a_sh
        d *= 2
    h = a * hcarry[...] + bv
    hcarry[...] = h[rows - 1:rows, :]
    y = h * _gelu_tanh(gate_ref[...])
    ms = jnp.mean(y * y, -1, keepdims=True)
    y_ref[...] = y * lax.rsqrt(ms + NORM_EPS) * og_ref[...]


def _lru(proj3, conv_w, conv_b, w_gates, b_gates, lam, out_g, ts):
    bsz, seq, _ = proj3.shape
    row = lambda n: pl.BlockSpec((1, n), lambda b, s: (0, 0))
    return pl.pallas_call(
        _lru_kernel,
        grid=(bsz, seq // ts),
        in_specs=[
            pl.BlockSpec((None, ts, LRU_WIDTH), lambda b, s: (b, s, 0)),
            pl.BlockSpec((None, ts, LRU_WIDTH), lambda b, s: (b, s, 1)),
            pl.BlockSpec((CONV_WIDTH, LRU_WIDTH), lambda b, s: (0, 0)),
            row(LRU_WIDTH),
            pl.BlockSpec((LRU_WIDTH, 2 * LRU_WIDTH), lambda b, s: (0, 0)),
            row(2 * LRU_WIDTH),
            row(LRU_WIDTH),
            row(LRU_WIDTH),
        ],
        out_specs=pl.BlockSpec((None, ts, LRU_WIDTH), lambda b, s: (b, s, 0)),
        out_shape=jax.ShapeDtypeStruct((bsz, seq, LRU_WIDTH), F32),
        scratch_shapes=[
            pltpu.VMEM((HALO + ts, LRU_WIDTH), F32),
            pltpu.VMEM((1, LRU_WIDTH), F32),
        ],
        compiler_params=_params(("arbitrary", "arbitrary")),
        name="rg_lru",
    )(proj3, proj3, conv_w, conv_b, w_gates, b_gates, lam, out_g)


def _bdot(a, b, dims=NN):
    return _dot(a.astype(BF16), b.astype(BF16), dims)


def _gdn_heads(args, norm_w, causal, strict, upper):
    c = GDN_CHUNK
    each = lambda f, *ls: [f(*xs) for xs in zip(*ls)]
    q, k, v, z, beta, g_col, g_row, st = [list(x) for x in zip(*args)]
    q = each(lambda x: x * lax.rsqrt(jnp.sum(x * x, -1, keepdims=True) + NORM_EPS) * (GDN_DK ** -0.5), q)
    k = each(lambda x: x * lax.rsqrt(jnp.sum(x * x, -1, keepdims=True) + NORM_EPS), k)
    gc_col = each(lambda g: jnp.sum(jnp.where(causal, g, 0.0), axis=1, keepdims=True), g_row)
    gc_row = each(lambda g: jnp.sum(jnp.where(upper, g, 0.0), axis=0, keepdims=True), g_col)
    decay = each(lambda gc, gr: jnp.exp(jnp.where(causal, gc - gr, -jnp.inf)), gc_col, gc_row)
    kb = each(lambda x, bt: x * bt, k, beta)
    vb = each(lambda x, bt: x * bt, v, beta)
    kk = each(lambda x, y: _bdot(x, y, NT), kb, k)
    a_mat = each(lambda m, d: jnp.where(strict, m * d, 0.0), kk, decay)
    e_col = each(jnp.exp, gc_col)
    rhs = each(lambda x, y, e: jnp.concatenate([x, y * e], axis=1), vb, kb, e_col)
    sol = each(lambda r, a: r - _bdot(a, r), rhs, a_mat)
    p = a_mat
    for _ in range(5):
        p = each(lambda x: _bdot(x, x), p)
        sol = each(lambda x, y: y + _bdot(x, y), p, sol)
    qk = each(lambda x, y: _bdot(x, y, NT), q, k)
    qk = each(lambda m, d: jnp.where(causal, m * d, 0.0), qk, decay)
    q_dec = each(lambda x, e: x * e, q, e_col)
    g_last = each(lambda gc: gc[c - 1:c, :], gc_col)
    k_dec = each(lambda x, gl, gc: x * jnp.exp(gl - gc), k, g_last, gc_col)
    ws = each(lambda x, s: _bdot(x[:, GDN_DV:], s), sol, st)
    qs = each(lambda x, s: _bdot(x, s), q_dec, st)
    v_new = each(lambda x, w: x[:, :GDN_DV] - w, sol, ws)
    o = each(lambda a, m, vn: a + _bdot(m, vn), qs, qk, v_new)
    kv = each(lambda x, vn: _bdot(x, vn, TN), k_dec, v_new)
    st_new = each(lambda s, gl, d: s * jnp.exp(gl) + d, st, g_last, kv)
    o = each(lambda x: x * lax.rsqrt(jnp.mean(x * x, -1, keepdims=True) + NORM_EPS) * norm_w, o)
    o = each(lambda x, zz: x * _silu(zz), o, z)
    return list(zip(o, st_new))


def _gdn_kernel(q_ref, k_ref, v_ref, z_ref, sm_ref, smt_ref, cwq_ref, cwk_ref, cwv_ref,
                alr_ref, dtr_ref, alc_ref, dtc_ref, nw_ref, y_ref, qbuf, kbuf, vbuf, state):
    n = pl.program_id(1)
    c = GDN_CHUNK
    nb = q_ref.shape[0]
    first = n == 0

    @pl.when(first)
    def _():
        state[...] = jnp.zeros_like(state)

    ri = lax.broadcasted_iota(jnp.int32, (c, c), 0)
    ci = lax.broadcasted_iota(jnp.int32, (c, c), 1)
    causal = ri >= ci
    strict = ri > ci
    upper = ri <= ci
    norm_w = nw_ref[...]

    args = []
    for b in range(nb):
        q_all = _silu(_causal_conv(qbuf.at[b], q_ref[b], cwq_ref, first, c))
        k_all = _silu(_causal_conv(kbuf.at[b], k_ref[b], cwk_ref, first, c))
        v_all = _silu(_causal_conv(vbuf.at[b], v_ref[b], cwv_ref, first, c))
        z_all = z_ref[b]
        sm = sm_ref[b]
        beta_all = _sigmoid(sm)
        g_cols = -jnp.exp(alr_ref[...]) * _softplus(sm + dtr_ref[...])
        g_rows = -jnp.exp(alc_ref[...]) * _softplus(smt_ref[b] + dtc_ref[...])
        for hd in range(GDN_HEADS):
            sl = slice(hd * GDN_DK, (hd + 1) * GDN_DK)
            args.append((q_all[:, sl], k_all[:, sl], v_all[:, sl], z_all[:, sl],
                         beta_all[:, hd:hd + 1],
                         g_cols[:, GDN_HEADS + hd:GDN_HEADS + hd + 1],
                         g_rows[GDN_HEADS + hd:GDN_HEADS + hd + 1, :],
                         state[b, hd]))
    outs = _gdn_heads(args, norm_w, causal, strict, upper)
    for b in range(nb):
        for hd in range(GDN_HEADS):
            o, st_new = outs[b * GDN_HEADS + hd]
            state[b, hd] = st_new
            y_ref[b, :, hd * GDN_DK:(hd + 1) * GDN_DK] = o


def _gdn(proj3, small3, smallt3, cwq, cwk, cwv, alr, dtr, alc, dtc, norm_w, nb):
    bsz, seq, _ = proj3.shape
    c = GDN_CHUNK
    nch = seq // c
    col = lambda j: pl.BlockSpec((nb, c, GDN_QK), lambda b, n: (b, n, j))
    const = lambda shape: pl.BlockSpec(shape, lambda b, n: (0,) * len(shape))
    return pl.pallas_call(
        _gdn_kernel,
        grid=(bsz // nb, nch),
        in_specs=[
            col(2), col(3), col(4), col(5),
            pl.BlockSpec((nb, c, LANES), lambda b, n: (b, n, 0)),
            pl.BlockSpec((nb, None, 8, c), lambda b, n: (b, n, 0, 0)),
            const((CONV_WIDTH, GDN_QK)), const((CONV_WIDTH, GDN_QK)), const((CONV_WIDTH, GDN_V)),
            const((1, LANES)), const((1, LANES)), const((8, 1)), const((8, 1)),
            const((1, GDN_DV)),
        ],
        out_specs=pl.BlockSpec((nb, c, GDN_V), lambda b, n: (b, n, 0)),
        out_shape=jax.ShapeDtypeStruct((bsz, seq, GDN_V), F32),
        scratch_shapes=[
            pltpu.VMEM((nb, HALO + c, GDN_QK), F32),
            pltpu.VMEM((nb, HALO + c, GDN_QK), F32),
            pltpu.VMEM((nb, HALO + c, GDN_V), F32),
            pltpu.VMEM((nb, GDN_HEADS, GDN_DK, GDN_DV), F32),
        ],
        compiler_params=_params(("arbitrary", "arbitrary")),
        name="gated_deltanet",
    )(proj3, proj3, proj3, proj3, small3, smallt3, cwq, cwk, cwv, alr, dtr, alc, dtc, norm_w)


def _router_kernel(yl_ref, yg_ref, h0_ref, wo1_ref, wo2_ref, g_ref, b_ref, wrt_ref, rb_ref,
                   h1_ref, e_ref, w_ref, rank_ref, cnt_ref, carry):
    i = pl.program_id(0)
    tm = h0_ref.shape[0]

    @pl.when(i == 0)
    def _():
        carry[...] = jnp.zeros_like(carry)

    mix = _dot(yl_ref[...].astype(BF16), wo1_ref[...]) + _dot(yg_ref[...].astype(BF16), wo2_ref[...])
    h1 = _layer_norm(DEEPNORM_ALPHA * h0_ref[...] + mix, g_ref[...], b_ref[...])
    h1_ref[...] = h1

    scores = _sigmoid(_dot3(wrt_ref[...], h1, NT))
    choice = scores + rb_ref[...]
    neg = -jnp.inf
    gs_rows = []
    sub = lax.broadcasted_iota(jnp.int32, (GROUP_SIZE, tm), 0).astype(F32)
    for g in range(N_GROUPS):
        cg = choice[g * GROUP_SIZE:(g + 1) * GROUP_SIZE, :]
        m1 = jnp.max(cg, axis=0, keepdims=True)
        i1 = jnp.min(jnp.where(cg == m1, sub, float(GROUP_SIZE)), axis=0, keepdims=True)
        m2 = jnp.max(jnp.where(sub == i1, neg, cg), axis=0, keepdims=True)
        gs_rows.append(m1 + m2)
    gs = jnp.concatenate(gs_rows, axis=0)
    gi = lax.broadcasted_iota(jnp.int32, (N_GROUPS, tm), 0).astype(F32)
    gsel = jnp.zeros((N_GROUPS, tm), jnp.bool_)
    for _ in range(TOPK_GROUPS):
        m = jnp.max(gs, axis=0, keepdims=True)
        idx = jnp.min(jnp.where(gs == m, gi, float(N_GROUPS)), axis=0, keepdims=True)
        hit = gi == idx
        gsel = jnp.logical_or(gsel, hit)
        gs = jnp.where(hit, neg, gs)
    masked = jnp.concatenate(
        [jnp.where(gsel[g:g + 1, :], choice[g * GROUP_SIZE:(g + 1) * GROUP_SIZE, :], neg)
         for g in range(N_GROUPS)], axis=0)

    ei = lax.broadcasted_iota(jnp.int32, (N_EXPERTS, tm), 0).astype(F32)
    hits = []
    e_rows, w_rows = [], []
    multi = jnp.zeros((N_EXPERTS, tm), F32)
    for _ in range(TOP_K):
        m = jnp.max(masked, axis=0, keepdims=True)
        idx = jnp.min(jnp.where(masked == m, ei, float(N_EXPERTS)), axis=0, keepdims=True)
        hit = ei == idx
        hits.append(hit)
        e_rows.append(idx)
        w_rows.append(jnp.sum(jnp.where(hit, scores, 0.0), axis=0, keepdims=True))
        multi = multi + hit.astype(F32)
        masked = jnp.where(hit, neg, masked)
    wts = jnp.concatenate(w_rows, axis=0)
    wts = wts / (jnp.sum(wts, axis=0, keepdims=True) + 1e-20) * ROUTED_SCALE
    ti = lax.broadcasted_iota(jnp.int32, (tm, tm), 0)
    tj = lax.broadcasted_iota(jnp.int32, (tm, tm), 1)
    before = (ti < tj).astype(BF16)
    cum = _dot(multi.astype(BF16), before) + carry[...]
    r_rows = [jnp.sum(jnp.where(hit, cum, 0.0), axis=0, keepdims=True) for hit in hits]
    carry[...] = carry[...] + jnp.sum(multi, axis=1, keepdims=True)
    e_ref[...] = jnp.concatenate(e_rows, axis=0).astype(jnp.int32)
    w_ref[...] = wts
    rank_ref[...] = jnp.concatenate(r_rows, axis=0).astype(jnp.int32)
    cnt_ref[...] = carry[...].astype(jnp.int32)


def _router(y_lru, y_gdn, h0, wo1, wo2, g, b, w_router_t, rbias, tm):
    t = h0.shape[0]
    const = lambda shape: pl.BlockSpec(shape, lambda i: (0,) * len(shape))
    return pl.pallas_call(
        _router_kernel,
        grid=(t // tm,),
        in_specs=[
            pl.BlockSpec((tm, LRU_WIDTH), lambda i: (i, 0)),
            pl.BlockSpec((tm, GDN_V), lambda i: (i, 0)),
            pl.BlockSpec((tm, D_MODEL), lambda i: (i, 0)),
            const((LRU_WIDTH, D_MODEL)), const((GDN_V, D_MODEL)),
            const((1, D_MODEL)), const((1, D_MODEL)),
            const((N_EXPERTS, D_MODEL)), const((N_EXPERTS, 1)),
        ],
        out_specs=[
            pl.BlockSpec((tm, D_MODEL), lambda i: (i, 0)),
            pl.BlockSpec((TOP_K, tm), lambda i: (0, i)),
            pl.BlockSpec((TOP_K, tm), lambda i: (0, i)),
            pl.BlockSpec((TOP_K, tm), lambda i: (0, i)),
            const((N_EXPERTS, 1)),
        ],
        out_shape=[
            jax.ShapeDtypeStruct((t, D_MODEL), F32),
            jax.ShapeDtypeStruct((TOP_K, t), jnp.int32),
            jax.ShapeDtypeStruct((TOP_K, t), F32),
            jax.ShapeDtypeStruct((TOP_K, t), jnp.int32),
            jax.ShapeDtypeStruct((N_EXPERTS, 1), jnp.int32),
        ],
        scratch_shapes=[pltpu.VMEM((N_EXPERTS, 1), F32)],
        compiler_params=_params(("arbitrary",)),
        name="outproj_router",
    )(y_lru, y_gdn, h0, wo1, wo2, g, b, w_router_t, rbias)


def _dest_kernel(e_ref, r_ref, ps_ref, d_ref):
    tm = e_ref.shape[1]
    ei = lax.broadcasted_iota(jnp.int32, (N_EXPERTS, tm), 0)
    rows = []
    for k in range(TOP_K):
        hit = ei == e_ref[k:k + 1, :]
        rows.append(jnp.sum(jnp.where(hit, ps_ref[...], 0), axis=0, keepdims=True))
    d_ref[...] = jnp.concatenate(rows, axis=0) + r_ref[...]


def _dest(top_e, rank, pad_start, tm):
    t = top_e.shape[1]
    blk = pl.BlockSpec((TOP_K, tm), lambda i: (0, i))
    return pl.pallas_call(
        _dest_kernel,
        grid=(t // tm,),
        in_specs=[blk, blk, pl.BlockSpec((N_EXPERTS, 1), lambda i: (0, 0))],
        out_specs=blk,
        out_shape=jax.ShapeDtypeStruct((TOP_K, t), jnp.int32),
        compiler_params=_params(("arbitrary",)),
        name="moe_dest",
    )(top_e, rank, pad_start)


def _dispatch_kernel(fs_ref, fn_ref, dest_ref, x_ref, xs_hbm, zrow, sem):
    i = pl.program_id(0)
    tm = x_ref.shape[0]

    def row_copy(t, k):
        return pltpu.make_async_copy(x_ref.at[pl.ds(t, 1), :], xs_hbm.at[pl.ds(dest_ref[k, t], 1), :], sem)

    def zero_copy(row):
        return pltpu.make_async_copy(zrow.at[pl.ds(0, 1), :], xs_hbm.at[pl.ds(row, 1), :], sem)

    def for_rows(fn):
        def body(t, carry):
            for k in range(TOP_K):
                fn(row_copy(t, k))
            return carry
        lax.fori_loop(0, tm, body, 0)

    def for_padding(fn):
        def per_expert(e, carry):
            def body(j, c):
                fn(zero_copy(fs_ref[e] + j))
                return c
            return lax.fori_loop(0, fn_ref[e], body, carry)
        lax.fori_loop(0, N_EXPERTS, per_expert, 0)

    for_rows(lambda cp: cp.start())

    @pl.when(i == 0)
    def _():
        zrow[...] = jnp.zeros_like(zrow)
        for_padding(lambda cp: cp.start())
        for_padding(lambda cp: cp.wait())

    for_rows(lambda cp: cp.wait())


def _dispatch(fill_start, fill_n, dest, x2d, n_rows, tm):
    t = x2d.shape[0]
    return pl.pallas_call(
        _dispatch_kernel,
        grid_spec=pltpu.PrefetchScalarGridSpec(
            num_scalar_prefetch=2,
            grid=(t // tm,),
            in_specs=[
                pl.BlockSpec((TOP_K, tm), lambda i, fs, fn: (0, i), memory_space=pltpu.SMEM),
                pl.BlockSpec((tm, D_MODEL), lambda i, fs, fn: (i, 0)),
            ],
            out_specs=pl.BlockSpec(memory_space=pl.ANY),
            scratch_shapes=[pltpu.VMEM((8, D_MODEL), F32), pltpu.SemaphoreType.DMA],
        ),
        out_shape=jax.ShapeDtypeStruct((n_rows, D_MODEL), F32),
        compiler_params=_params(("arbitrary",), has_side_effects=True),
        name="moe_dispatch",
    )(fill_start, fill_n, dest, x2d)


def _expert_kernel(be_ref, nu_ref, xs_ref, wg_ref, wu_ref, wd_ref, ys_ref, wgu_b, wd_b):
    i = pl.program_id(0)

    @pl.when(i < nu_ref[0])
    def _():
        prev = be_ref[jnp.maximum(i - 1, 0)]

        @pl.when(jnp.logical_or(i == 0, be_ref[i] != prev))
        def _():
            wgu_b[:, :D_EXPERT] = wg_ref[...].astype(BF16)
            wgu_b[:, D_EXPERT:] = wu_ref[...].astype(BF16)
            wd_b[...] = wd_ref[...].astype(BF16)

        gu = _dot(xs_ref[...].astype(BF16), wgu_b[...])
        h = _silu(gu[:, :D_EXPERT]) * gu[:, D_EXPERT:]
        ys_ref[...] = _dot(h.astype(BF16), wd_b[...])


def _experts(blk_e, n_used, xs, w_gate, w_up, w_down):
    n_rows = xs.shape[0]
    n_blocks = n_rows // MOE_BLOCK
    blk = lambda i, be, nu: (jnp.minimum(i, nu[0] - 1), 0)
    exp = lambda i, be, nu: (be[i], 0, 0)
    return pl.pallas_call(
        _expert_kernel,
        grid_spec=pltpu.PrefetchScalarGridSpec(
            num_scalar_prefetch=2,
            grid=(n_blocks,),
            in_specs=[
                pl.BlockSpec((MOE_BLOCK, D_MODEL), blk),
                pl.BlockSpec((None, D_MODEL, D_EXPERT), exp),
                pl.BlockSpec((None, D_MODEL, D_EXPERT), exp),
                pl.BlockSpec((None, D_EXPERT, D_MODEL), exp),
            ],
            out_specs=pl.BlockSpec((MOE_BLOCK, D_MODEL), blk),
            scratch_shapes=[
                pltpu.VMEM((D_MODEL, 2 * D_EXPERT), BF16),
                pltpu.VMEM((D_EXPERT, D_MODEL), BF16),
            ],
        ),
        out_shape=jax.ShapeDtypeStruct((n_rows, D_MODEL), F32),
        compiler_params=_params(("arbitrary",)),
        name="moe_experts",
    )(blk_e, n_used, xs, w_gate, w_up, w_down)


def _combine_kernel(dest_ref, h1_ref, wts_ref, wsgu_ref, wsd_ref, g_ref, b_ref, ys_hbm, out_ref, buf, sem):
    tm = h1_ref.shape[0]

    def issue(t, carry):
        for k in range(TOP_K):
            pltpu.make_async_copy(ys_hbm.at[pl.ds(dest_ref[k, t], 1), :],
                                  buf.at[k, pl.ds(t, 1), :], sem).start()
        return carry

    lax.fori_loop(0, tm, issue, 0)

    h1 = h1_ref[...]
    gu = _dot(h1.astype(BF16), wsgu_ref[...])
    hs = _silu(gu[:, :D_SHARED]) * gu[:, D_SHARED:]
    acc = DEEPNORM_ALPHA * h1 + _dot(hs.astype(BF16), wsd_ref[...])

    def drain(t, carry):
        for k in range(TOP_K):
            pltpu.make_async_copy(ys_hbm.at[pl.ds(dest_ref[k, t], 1), :],
                                  buf.at[k, pl.ds(t, 1), :], sem).wait()
        return carry

    lax.fori_loop(0, tm, drain, 0)

    wts = wts_ref[...]
    for k in range(TOP_K):
        acc = acc + buf[k] * wts[:, k:k + 1]
    out_ref[...] = _layer_norm(acc, g_ref[...], b_ref[...])


def _combine(dest, h1, wts_t, ws_gu, ws_down, g, b, ys, tm):
    t = h1.shape[0]
    const = lambda shape: pl.BlockSpec(shape, lambda i: (0,) * len(shape))
    return pl.pallas_call(
        _combine_kernel,
        grid=(t // tm,),
        in_specs=[
            pl.BlockSpec((TOP_K, tm), lambda i: (0, i), memory_space=pltpu.SMEM),
            pl.BlockSpec((tm, D_MODEL), lambda i: (i, 0)),
            pl.BlockSpec((tm, TOP_K), lambda i: (i, 0)),
            const((D_MODEL, 2 * D_SHARED)), const((D_SHARED, D_MODEL)),
            const((1, D_MODEL)), const((1, D_MODEL)),
            pl.BlockSpec(memory_space=pl.ANY),
        ],
        out_specs=pl.BlockSpec((tm, D_MODEL), lambda i: (i, 0)),
        out_shape=jax.ShapeDtypeStruct((t, D_MODEL), F32),
        scratch_shapes=[
            pltpu.VMEM((TOP_K, tm, D_MODEL), F32),
            pltpu.SemaphoreType.DMA,
        ],
        compiler_params=_params(("arbitrary",)),
        name="moe_combine",
    )(dest, h1, wts_t, ws_gu, ws_down, g, b, ys)


def _block_diag(w):
    nb, bi, bo = w.shape
    eye = jnp.eye(nb, dtype=w.dtype)
    return (eye[:, None, :, None] * w[:, :, None, :]).reshape(nb * bi, nb * bo)


def _pad_lanes(v, offset, width):
    return jnp.zeros((1, width), F32).at[0, offset:offset + v.shape[0]].set(v)


def _layer(h_in_x, l, p, tiles):
    bsz, seq, _ = h_in_x.shape
    t = bsz * seq
    row = lambda v: v.reshape(1, -1)

    w_in = p['w_in'][l]
    w_main = w_in[:, :N_MAIN].astype(BF16)
    w_small = jnp.zeros((D_MODEL, LANES), F32).at[:, :2 * GDN_HEADS].set(w_in[:, N_MAIN:])
    w_small_t = w_in[:, N_MAIN:].T
    h0, proj, small, small_t = _inproj(h_in_x.reshape(t, D_MODEL), row(p['ln_g']), row(p['ln_b']),
                                       w_main, w_small, w_small_t, tiles['inproj'])
    proj3 = proj.reshape(bsz, seq, N_MAIN)

    w_gates = jnp.concatenate([_block_diag(p['lru_w_rg'][l]), _block_diag(p['lru_w_ig'][l])], 1).astype(BF16)
    b_gates = jnp.concatenate([p['lru_b_rg'][l], p['lru_b_ig'][l]]).reshape(1, -1)
    y_lru = _lru(proj3, p['lru_conv_w'][l], row(p['lru_conv_b'][l]), w_gates, b_gates,
                 row(p['lru_lambda'][l]), row(p['lru_out_g'][l]), tiles['lru'])

    nch = seq // GDN_CHUNK
    small3 = small.reshape(bsz, seq, LANES)
    smallt3 = small_t.reshape(8, bsz, nch, GDN_CHUNK).transpose(1, 2, 0, 3)
    cw = p['gdn_conv_w'][l]
    a_log, dt_bias = p['gdn_a_log'][l], p['gdn_dt_bias'][l]
    alr = _pad_lanes(a_log, GDN_HEADS, LANES)
    dtr = _pad_lanes(dt_bias, GDN_HEADS, LANES)
    alc = _pad_lanes(a_log, GDN_HEADS, 8).reshape(8, 1)
    dtc = _pad_lanes(dt_bias, GDN_HEADS, 8).reshape(8, 1)
    y_gdn = _gdn(proj3, small3, smallt3, cw[:, :GDN_QK], cw[:, GDN_QK:2 * GDN_QK], cw[:, 2 * GDN_QK:],
                 alr, dtr, alc, dtc, row(p['gdn_norm_w'][l]), tiles['gdn_nb'])

    w_out = p['w_out'][l].astype(BF16)
    h1, top_e, wts, rank, counts = _router(
        y_lru.reshape(t, LRU_WIDTH), y_gdn.reshape(t, GDN_V), h0, w_out[:LRU_WIDTH], w_out[LRU_WIDTH:],
        row(p['ln1_g'][l]), row(p['ln1_b'][l]), p['w_router'][l].T, p['router_bias'][l].reshape(-1, 1),
        tiles['router'])

    counts = counts[:, 0]
    padded = (counts + MOE_BLOCK - 1) // MOE_BLOCK * MOE_BLOCK
    pad_end = jnp.cumsum(padded)
    pad_start = pad_end - padded
    n_blocks = (t * TOP_K) // MOE_BLOCK + N_EXPERTS
    n_rows = n_blocks * MOE_BLOCK
    n_used = (pad_end[-1] // MOE_BLOCK).astype(jnp.int32)
    blk_ids = jnp.minimum(jnp.arange(n_blocks, dtype=jnp.int32), n_used - 1)
    blk_e = jnp.minimum(jnp.sum(pad_end[None, :] <= (blk_ids * MOE_BLOCK)[:, None], axis=1),
                        N_EXPERTS - 1).astype(jnp.int32)

    dest = _dest(top_e, rank, pad_start.reshape(-1, 1), tiles['dest'])
    xs = _dispatch(pad_start + counts, padded - counts, dest, h1, n_rows, tiles['dispatch'])
    ys = _experts(blk_e, n_used.reshape(1), xs, p['w_gate'][l], p['w_up'][l], p['w_down'][l])
    ws_gu = jnp.concatenate([p['ws_gate'][l], p['ws_up'][l]], 1).astype(BF16)
    out = _combine(dest, h1, wts.T, ws_gu, p['ws_down'][l].astype(BF16),
                   row(p['ln2_g'][l]), row(p['ln2_b'][l]), ys, tiles['combine'])
    return out.reshape(bsz, seq, D_MODEL)


def _tiles(bsz, seq):
    t = bsz * seq
    return {
        'inproj': min(256, t),
        'lru': min(256, seq),
        'gdn_nb': bsz,
        'router': min(256, t),
        'dest': min(512, t),
        'dispatch': min(512, t),
        'combine': min(128, t),
    }


def kernel(x, ln_in_g, ln_in_b, w_in, lru_conv_w, lru_conv_b, lru_w_rg, lru_b_rg, lru_w_ig, lru_b_ig,
           lru_lambda, lru_out_g, gdn_conv_w, gdn_a_log, gdn_dt_bias, gdn_norm_w, w_out, ln1_g, ln1_b,
           w_router, router_bias, w_gate, w_up, w_down, ws_gate, ws_up, ws_down, ln2_g, ln2_b):
    assert w_in.shape[0] == DEPTH == 1
    p = dict(ln_g=ln_in_g, ln_b=ln_in_b, w_in=w_in, lru_conv_w=lru_conv_w, lru_conv_b=lru_conv_b,
             lru_w_rg=lru_w_rg, lru_b_rg=lru_b_rg, lru_w_ig=lru_w_ig, lru_b_ig=lru_b_ig,
             lru_lambda=lru_lambda, lru_out_g=lru_out_g, gdn_conv_w=gdn_conv_w, gdn_a_log=gdn_a_log,
             gdn_dt_bias=gdn_dt_bias, gdn_norm_w=gdn_norm_w, w_out=w_out, ln1_g=ln1_g, ln1_b=ln1_b,
             w_router=w_router, router_bias=router_bias, w_gate=w_gate, w_up=w_up, w_down=w_down,
             ws_gate=ws_gate, ws_up=ws_up, ws_down=ws_down, ln2_g=ln2_g, ln2_b=ln2_b)
    bsz, seq, _ = x.shape
    return _layer(x, 0, p, _tiles(bsz, seq))
```

```python
import functools

import jax
import jax.numpy as jnp
from jax import lax
from jax.experimental import pallas as pl
from jax.experimental.pallas import tpu as pltpu

F32 = jnp.float32
BF16 = jnp.bfloat16

D_MODEL = 1024
LRU_WIDTH = 512
LRU_BLOCKS = 8
LRU_C = 8.0
CONV_WIDTH = 4
GDN_HEADS = 4
GDN_DK = 128
GDN_DV = 128
GDN_CHUNK = 64
GDN_QK = GDN_HEADS * GDN_DK
GDN_V = GDN_HEADS * GDN_DV
N_MAIN = 2 * LRU_WIDTH + 2 * GDN_QK + 2 * GDN_V
N_EXPERTS = 256
TOP_K = 8
N_GROUPS = 8
GROUP_SIZE = N_EXPERTS // N_GROUPS
TOPK_GROUPS = 4
D_EXPERT = 256
D_SHARED = 256
ROUTED_SCALE = 2.5
MOE_BLOCK = 256
D_PACK = D_MODEL // 2
LN_EPS = 1e-5
NORM_EPS = 1e-6
DEPTH = 1
DEEPNORM_ALPHA = (2.0 * DEPTH) ** 0.25

HALO = 8
LANES = 128
VMEM_LIMIT = 56 * 1024 * 1024

NN = (((1,), (0,)), ((), ()))
NT = (((1,), (1,)), ((), ()))
TN = (((0,), (0,)), ((), ()))


def _dot(a, b, dims=NN):
    return lax.dot_general(a, b, dims, preferred_element_type=F32)


def _split(a):
    hi = a.astype(BF16)
    lo = (a - hi.astype(F32)).astype(BF16)
    return hi, lo


def _dot3(a, b, dims=NN):
    ah, al = _split(a)
    bh, bl = _split(b)
    return _dot(ah, bh, dims) + (_dot(ah, bl, dims) + _dot(al, bh, dims))


def _layer_norm(x, g, b):
    mu = jnp.mean(x, -1, keepdims=True)
    xc = x - mu
    var = jnp.mean(xc * xc, -1, keepdims=True)
    return xc * lax.rsqrt(var + LN_EPS) * g + b


def _sigmoid(x):
    return 1.0 / (1.0 + jnp.exp(-x))


def _silu(x):
    return x * _sigmoid(x)


def _softplus(x):
    return jnp.maximum(x, 0.0) + jnp.log1p(jnp.exp(-jnp.abs(x)))


def _gelu_tanh(x):
    c = 0.7978845608028654
    return x * (0.5 * (1.0 + jnp.tanh(c * (x + 0.044715 * (x * x * x)))))


def _pack_rows(x):
    hi = lax.bitcast_convert_type(x[:, :D_PACK].astype(BF16).astype(F32), jnp.uint32)
    lo = lax.bitcast_convert_type(x[:, D_PACK:].astype(BF16).astype(F32), jnp.uint32)
    return (hi & jnp.uint32(0xFFFF0000)) | (lo >> 16)


def _unpack_rows(w):
    hi = lax.bitcast_convert_type(w & jnp.uint32(0xFFFF0000), F32)
    lo = lax.bitcast_convert_type(w << 16, F32)
    return hi, lo


def _params(sem, **kw):
    return pltpu.CompilerParams(dimension_semantics=sem, vmem_limit_bytes=VMEM_LIMIT, **kw)


def _inproj_kernel(x_ref, g_ref, b_ref, w_ref, ws_ref, wst_ref, h_ref, proj_ref, small_ref, smallt_ref):
    h = _layer_norm(x_ref[...], g_ref[...], b_ref[...])
    h_ref[...] = h
    hb = h.astype(BF16)
    proj_ref[...] = _dot(hb, w_ref[...])
    small_ref[...] = _dot3(h, ws_ref[...])
    smallt_ref[...] = _dot3(wst_ref[...], h, NT)


def _inproj(x2d, g, b, w_main, w_small, w_small_t, tm):
    t = x2d.shape[0]
    return pl.pallas_call(
        _inproj_kernel,
        grid=(t // tm,),
        in_specs=[
            pl.BlockSpec((tm, D_MODEL), lambda i: (i, 0)),
            pl.BlockSpec((1, D_MODEL), lambda i: (0, 0)),
            pl.BlockSpec((1, D_MODEL), lambda i: (0, 0)),
            pl.BlockSpec((D_MODEL, N_MAIN), lambda i: (0, 0)),
            pl.BlockSpec((D_MODEL, LANES), lambda i: (0, 0)),
            pl.BlockSpec((8, D_MODEL), lambda i: (0, 0)),
        ],
        out_specs=[
            pl.BlockSpec((tm, D_MODEL), lambda i: (i, 0)),
            pl.BlockSpec((tm, N_MAIN), lambda i: (i, 0)),
            pl.BlockSpec((tm, LANES), lambda i: (i, 0)),
            pl.BlockSpec((8, tm), lambda i: (0, i)),
        ],
        out_shape=[
            jax.ShapeDtypeStruct((t, D_MODEL), F32),
            jax.ShapeDtypeStruct((t, N_MAIN), F32),
            jax.ShapeDtypeStruct((t, LANES), F32),
            jax.ShapeDtypeStruct((8, t), F32),
        ],
        compiler_params=_params(("arbitrary",)),
        name="ln_inproj",
    )(x2d, g, b, w_main, w_small, w_small_t)


def _causal_conv(buf_ref, x, w_ref, first, rows):
    @pl.when(first)
    def _():
        buf_ref[0:HALO, :] = jnp.zeros((HALO, buf_ref.shape[1]), F32)

    buf_ref[HALO:HALO + rows, :] = x
    acc = None
    for j in range(CONV_WIDTH):
        off = HALO - (CONV_WIDTH - 1) + j
        term = buf_ref[off:off + rows, :] * w_ref[j:j + 1, :]
        acc = term if acc is None else acc + term
    buf_ref[0:HALO, :] = buf_ref[rows:rows + HALO, :]
    return acc


def _shift_rows(x, d, fill):
    rows = x.shape[0]
    if d % 8 == 0:
        pad = jnp.full((d, x.shape[1]), fill, x.dtype)
        return jnp.concatenate([pad, x[:rows - d]], axis=0)
    rolled = pltpu.roll(x, d, 0)
    row = lax.broadcasted_iota(jnp.int32, x.shape, 0)
    return jnp.where(row < d, fill, rolled)


def _lru_kernel(u_ref, gate_ref, cw_ref, cb_ref, wg_ref, bg_ref, lam_ref, og_ref,
                y_ref, ubuf, hcarry):
    s = pl.program_id(1)
    rows = u_ref.shape[0]

    @pl.when(s == 0)
    def _():
        hcarry[...] = jnp.zeros_like(hcarry)

    xc = _causal_conv(ubuf, u_ref[...], cw_ref, s == 0, rows) + cb_ref[...]
    gates = _dot(xc.astype(BF16), wg_ref[...]) + bg_ref[...]
    r = _sigmoid(gates[:, :LRU_WIDTH])
    i = _sigmoid(gates[:, LRU_WIDTH:])
    log_a = (-LRU_C) * r * _softplus(-lam_ref[...])
    a = jnp.exp(log_a)
    mult = jnp.sqrt(-jnp.tanh(log_a) * (a * a + 1.0))
    bv = mult * (i * xc)
    d = 1
    while d < rows:
        a_sh = _shift_rows(a, d, 1.0)
        b_sh = _shift_rows(bv, d, 0.0)
        bv = a * b_sh + bv
        a = a * a_sh
        d *= 2
    h = a * hcarry[...] + bv
    hcarry[...] = h[rows - 1:rows, :]
    y = h * _gelu_tanh(gate_ref[...])
    ms = jnp.mean(y * y, -1, keepdims=True)
    y_ref[...] = y * lax.rsqrt(ms + NORM_EPS) * og_ref[...]


def _lru(proj3, conv_w, conv_b, w_gates, b_gates, lam, out_g, ts):
    bsz, seq, _ = proj3.shape
    row = lambda n: pl.BlockSpec((1, n), lambda b, s: (0, 0))
    return pl.pallas_call(
        _lru_kernel,
        grid=(bsz, seq // ts),
        in_specs=[
            pl.BlockSpec((None, ts, LRU_WIDTH), lambda b, s: (b, s, 0)),
            pl.BlockSpec((None, ts, LRU_WIDTH), lambda b, s: (b, s, 1)),
            pl.BlockSpec((CONV_WIDTH, LRU_WIDTH), lambda b, s: (0, 0)),
            row(LRU_WIDTH),
            pl.BlockSpec((LRU_WIDTH, 2 * LRU_WIDTH), lambda b, s: (0, 0)),
            row(2 * LRU_WIDTH),
            row(LRU_WIDTH),
            row(LRU_WIDTH),
        ],
        out_specs=pl.BlockSpec((None, ts, LRU_WIDTH), lambda b, s: (b, s, 0)),
        out_shape=jax.ShapeDtypeStruct((bsz, seq, LRU_WIDTH), F32),
        scratch_shapes=[
            pltpu.VMEM((HALO + ts, LRU_WIDTH), F32),
            pltpu.VMEM((1, LRU_WIDTH), F32),
        ],
        compiler_params=_params(("arbitrary", "arbitrary")),
        name="rg_lru",
    )(proj3, proj3, conv_w, conv_b, w_gates, b_gates, lam, out_g)


def _bdot(a, b, dims=NN):
    return _dot(a.astype(BF16), b.astype(BF16), dims)


def _gdn_heads(args, norm_w, causal, strict, upper):
    c = GDN_CHUNK
    each = lambda f, *ls: [f(*xs) for xs in zip(*ls)]
    q, k, v, z, beta, g_col, g_row, st = [list(x) for x in zip(*args)]
    q = each(lambda x: x * lax.rsqrt(jnp.sum(x * x, -1, keepdims=True) + NORM_EPS) * (GDN_DK ** -0.5), q)
    k = each(lambda x: x * lax.rsqrt(jnp.sum(x * x, -1, keepdims=True) + NORM_EPS), k)
    gc_col = each(lambda g: jnp.sum(jnp.where(causal, g, 0.0), axis=1, keepdims=True), g_row)
    gc_row = each(lambda g: jnp.sum(jnp.where(upper, g, 0.0), axis=0, keepdims=True), g_col)
    decay = each(lambda gc, gr: jnp.exp(jnp.where(causal, gc - gr, -jnp.inf)), gc_col, gc_row)
    kb = each(lambda x, bt: x * bt, k, beta)
    vb = each(lambda x, bt: x * bt, v, beta)
    kk = each(lambda x, y: _bdot(x, y, NT), kb, k)
    a_mat = each(lambda m, d: jnp.where(strict, m * d, 0.0), kk, decay)
    e_col = each(jnp.exp, gc_col)
    rhs = each(lambda x, y, e: jnp.concatenate([x, y * e], axis=1), vb, kb, e_col)
    sol = each(lambda r, a: r - _bdot(a, r), rhs, a_mat)
    p = a_mat
    for _ in range(5):
        p = each(lambda x: _bdot(x, x), p)
        sol = each(lambda x, y: y + _bdot(x, y), p, sol)
    qk = each(lambda x, y: _bdot(x, y, NT), q, k)
    qk = each(lambda m, d: jnp.where(causal, m * d, 0.0), qk, decay)
    q_dec = each(lambda x, e: x * e, q, e_col)
    g_last = each(lambda gc: gc[c - 1:c, :], gc_col)
    k_dec = each(lambda x, gl, gc: x * jnp.exp(gl - gc), k, g_last, gc_col)
    ws = each(lambda x, s: _bdot(x[:, GDN_DV:], s), sol, st)
    qs = each(lambda x, s: _bdot(x, s), q_dec, st)
    v_new = each(lambda x, w: x[:, :GDN_DV] - w, sol, ws)
    o = each(lambda a, m, vn: a + _bdot(m, vn), qs, qk, v_new)
    kv = each(lambda x, vn: _bdot(x, vn, TN), k_dec, v_new)
    st_new = each(lambda s, gl, d: s * jnp.exp(gl) + d, st, g_last, kv)
    o = each(lambda x: x * lax.rsqrt(jnp.mean(x * x, -1, keepdims=True) + NORM_EPS) * norm_w, o)
    o = each(lambda x, zz: x * _silu(zz), o, z)
    return list(zip(o, st_new))


def _gdn_kernel(q_ref, k_ref, v_ref, z_ref, sm_ref, smt_ref, cwq_ref, cwk_ref, cwv_ref,
                alr_ref, dtr_ref, alc_ref, dtc_ref, nw_ref, y_ref, qbuf, kbuf, vbuf, state):
    n = pl.program_id(1)
    c = GDN_CHUNK
    nb = q_ref.shape[0]
    first = n == 0

    @pl.when(first)
    def _():
        state[...] = jnp.zeros_like(state)

    ri = lax.broadcasted_iota(jnp.int32, (c, c), 0)
    ci = lax.broadcasted_iota(jnp.int32, (c, c), 1)
    causal = ri >= ci
    strict = ri > ci
    upper = ri <= ci
    norm_w = nw_ref[...]

    args = []
    for b in range(nb):
        q_all = _silu(_causal_conv(qbuf.at[b], q_ref[b], cwq_ref, first, c))
        k_all = _silu(_causal_conv(kbuf.at[b], k_ref[b], cwk_ref, first, c))
        v_all = _silu(_causal_conv(vbuf.at[b], v_ref[b], cwv_ref, first, c))
        z_all = z_ref[b]
        sm = sm_ref[b]
        beta_all = _sigmoid(sm)
        g_cols = -jnp.exp(alr_ref[...]) * _softplus(sm + dtr_ref[...])
        g_rows = -jnp.exp(alc_ref[...]) * _softplus(smt_ref[b] + dtc_ref[...])
        for hd in range(GDN_HEADS):
            sl = slice(hd * GDN_DK, (hd + 1) * GDN_DK)
            args.append((q_all[:, sl], k_all[:, sl], v_all[:, sl], z_all[:, sl],
                         beta_all[:, hd:hd + 1],
                         g_cols[:, GDN_HEADS + hd:GDN_HEADS + hd + 1],
                         g_rows[GDN_HEADS + hd:GDN_HEADS + hd + 1, :],
                         state[b, hd]))
    outs = _gdn_heads(args, norm_w, causal, strict, upper)
    for b in range(nb):
        for hd in range(GDN_HEADS):
            o, st_new = outs[b * GDN_HEADS + hd]
            state[b, hd] = st_new
            y_ref[b, :, hd * GDN_DK:(hd + 1) * GDN_DK] = o


def _gdn(proj3, small3, smallt3, cwq, cwk, cwv, alr, dtr, alc, dtc, norm_w, nb):
    bsz, seq, _ = proj3.shape
    c = GDN_CHUNK
    nch = seq // c
    col = lambda j: pl.BlockSpec((nb, c, GDN_QK), lambda b, n: (b, n, j))
    const = lambda shape: pl.BlockSpec(shape, lambda b, n: (0,) * len(shape))
    return pl.pallas_call(
        _gdn_kernel,
        grid=(bsz // nb, nch),
        in_specs=[
            col(2), col(3), col(4), col(5),
            pl.BlockSpec((nb, c, LANES), lambda b, n: (b, n, 0)),
            pl.BlockSpec((nb, None, 8, c), lambda b, n: (b, n, 0, 0)),
            const((CONV_WIDTH, GDN_QK)), const((CONV_WIDTH, GDN_QK)), const((CONV_WIDTH, GDN_V)),
            const((1, LANES)), const((1, LANES)), const((8, 1)), const((8, 1)),
            const((1, GDN_DV)),
        ],
        out_specs=pl.BlockSpec((nb, c, GDN_V), lambda b, n: (b, n, 0)),
        out_shape=jax.ShapeDtypeStruct((bsz, seq, GDN_V), F32),
        scratch_shapes=[
            pltpu.VMEM((nb, HALO + c, GDN_QK), F32),
            pltpu.VMEM((nb, HALO + c, GDN_QK), F32),
            pltpu.VMEM((nb, HALO + c, GDN_V), F32),
            pltpu.VMEM((nb, GDN_HEADS, GDN_DK, GDN_DV), F32),
        ],
        compiler_params=_params(("arbitrary", "arbitrary")),
        name="gated_deltanet",
    )(proj3, proj3, proj3, proj3, small3, smallt3, cwq, cwk, cwv, alr, dtr, alc, dtc, norm_w)


def _router_kernel(yl_ref, yg_ref, h0_ref, wo1_ref, wo2_ref, g_ref, b_ref, wrt_ref, rb_ref,
                   h1_ref, h1p_ref, e_ref, w_ref, rank_ref, cnt_ref, carry):
    i = pl.program_id(0)
    tm = h0_ref.shape[0]

    @pl.when(i == 0)
    def _():
        carry[...] = jnp.zeros_like(carry)

    mix = _dot(yl_ref[...].astype(BF16), wo1_ref[...]) + _dot(yg_ref[...].astype(BF16), wo2_ref[...])
    h1 = _layer_norm(DEEPNORM_ALPHA * h0_ref[...] + mix, g_ref[...], b_ref[...])
    h1_ref[...] = h1
    h1p_ref[...] = _pack_rows(h1)

    scores = _sigmoid(_dot3(wrt_ref[...], h1, NT))
    choice = scores + rb_ref[...]
    neg = -jnp.inf
    gs_rows = []
    sub = lax.broadcasted_iota(jnp.int32, (GROUP_SIZE, tm), 0).astype(F32)
    for g in range(N_GROUPS):
        cg = choice[g * GROUP_SIZE:(g + 1) * GROUP_SIZE, :]
        m1 = jnp.max(cg, axis=0, keepdims=True)
        i1 = jnp.min(jnp.where(cg == m1, sub, float(GROUP_SIZE)), axis=0, keepdims=True)
        m2 = jnp.max(jnp.where(sub == i1, neg, cg), axis=0, keepdims=True)
        gs_rows.append(m1 + m2)
    gs = jnp.concatenate(gs_rows, axis=0)
    gi = lax.broadcasted_iota(jnp.int32, (N_GROUPS, tm), 0).astype(F32)
    gsel = jnp.zeros((N_GROUPS, tm), jnp.bool_)
    for _ in range(TOPK_GROUPS):
        m = jnp.max(gs, axis=0, keepdims=True)
        idx = jnp.min(jnp.where(gs == m, gi, float(N_GROUPS)), axis=0, keepdims=True)
        hit = gi == idx
        gsel = jnp.logical_or(gsel, hit)
        gs = jnp.where(hit, neg, gs)
    masked = jnp.concatenate(
        [jnp.where(gsel[g:g + 1, :], choice[g * GROUP_SIZE:(g + 1) * GROUP_SIZE, :], neg)
         for g in range(N_GROUPS)], axis=0)

    ei = lax.broadcasted_iota(jnp.int32, (N_EXPERTS, tm), 0).astype(F32)
    hits = []
    e_rows, w_rows = [], []
    multi = jnp.zeros((N_EXPERTS, tm), F32)
    for _ in range(TOP_K):
        m = jnp.max(masked, axis=0, keepdims=True)
        idx = jnp.min(jnp.where(masked == m, ei, float(N_EXPERTS)), axis=0, keepdims=True)
        hit = ei == idx
        hits.append(hit)
        e_rows.append(idx)
        w_rows.append(jnp.sum(jnp.where(hit, scores, 0.0), axis=0, keepdims=True))
        multi = multi + hit.astype(F32)
        masked = jnp.where(hit, neg, masked)
    wts = jnp.concatenate(w_rows, axis=0)
    wts = wts / (jnp.sum(wts, axis=0, keepdims=True) + 1e-20) * ROUTED_SCALE
    ti = lax.broadcasted_iota(jnp.int32, (tm, tm), 0)
    tj = lax.broadcasted_iota(jnp.int32, (tm, tm), 1)
    before = (ti < tj).astype(BF16)
    cum = _dot(multi.astype(BF16), before) + carry[...]
    r_rows = [jnp.sum(jnp.where(hit, cum, 0.0), axis=0, keepdims=True) for hit in hits]
    carry[...] = carry[...] + jnp.sum(multi, axis=1, keepdims=True)
    e_ref[...] = jnp.concatenate(e_rows, axis=0).astype(jnp.int32)
    w_ref[...] = wts
    rank_ref[...] = jnp.concatenate(r_rows, axis=0).astype(jnp.int32)
    cnt_ref[...] = carry[...].astype(jnp.int32)


def _router(y_lru, y_gdn, h0, wo1, wo2, g, b, w_router_t, rbias, tm):
    t = h0.shape[0]
    const = lambda shape: pl.BlockSpec(shape, lambda i: (0,) * len(shape))
    return pl.pallas_call(
        _router_kernel,
        grid=(t // tm,),
        in_specs=[
            pl.BlockSpec((tm, LRU_WIDTH), lambda i: (i, 0)),
            pl.BlockSpec((tm, GDN_V), lambda i: (i, 0)),
            pl.BlockSpec((tm, D_MODEL), lambda i: (i, 0)),
            const((LRU_WIDTH, D_MODEL)), const((GDN_V, D_MODEL)),
            const((1, D_MODEL)), const((1, D_MODEL)),
            const((N_EXPERTS, D_MODEL)), const((N_EXPERTS, 1)),
        ],
        out_specs=[
            pl.BlockSpec((tm, D_MODEL), lambda i: (i, 0)),
            pl.BlockSpec((tm, D_PACK), lambda i: (i, 0)),
            pl.BlockSpec((TOP_K, tm), lambda i: (0, i)),
            pl.BlockSpec((TOP_K, tm), lambda i: (0, i)),
            pl.BlockSpec((TOP_K, tm), lambda i: (0, i)),
            const((N_EXPERTS, 1)),
        ],
        out_shape=[
            jax.ShapeDtypeStruct((t, D_MODEL), F32),
            jax.ShapeDtypeStruct((t, D_PACK), jnp.uint32),
            jax.ShapeDtypeStruct((TOP_K, t), jnp.int32),
            jax.ShapeDtypeStruct((TOP_K, t), F32),
            jax.ShapeDtypeStruct((TOP_K, t), jnp.int32),
            jax.ShapeDtypeStruct((N_EXPERTS, 1), jnp.int32),
        ],
        scratch_shapes=[pltpu.VMEM((N_EXPERTS, 1), F32)],
        compiler_params=_params(("arbitrary",)),
        name="outproj_router",
    )(y_lru, y_gdn, h0, wo1, wo2, g, b, w_router_t, rbias)


def _dest_kernel(e_ref, r_ref, ps_ref, d_ref):
    tm = e_ref.shape[1]
    ei = lax.broadcasted_iota(jnp.int32, (N_EXPERTS, tm), 0)
    rows = []
    for k in range(TOP_K):
        hit = ei == e_ref[k:k + 1, :]
        rows.append(jnp.sum(jnp.where(hit, ps_ref[...], 0), axis=0, keepdims=True))
    d_ref[...] = jnp.concatenate(rows, axis=0) + r_ref[...]


def _dest(top_e, rank, pad_start, tm):
    t = top_e.shape[1]
    blk = pl.BlockSpec((TOP_K, tm), lambda i: (0, i))
    return pl.pallas_call(
        _dest_kernel,
        grid=(t // tm,),
        in_specs=[blk, blk, pl.BlockSpec((N_EXPERTS, 1), lambda i: (0, 0))],
        out_specs=blk,
        out_shape=jax.ShapeDtypeStruct((TOP_K, t), jnp.int32),
        compiler_params=_params(("arbitrary",)),
        name="moe_dest",
    )(top_e, rank, pad_start)


def _dispatch_kernel(fs_ref, fn_ref, dest_ref, x_ref, xs_hbm, zrow, sem):
    i = pl.program_id(0)
    tm = x_ref.shape[0]

    def row_copy(t, k):
        return pltpu.make_async_copy(x_ref.at[pl.ds(t, 1), :], xs_hbm.at[pl.ds(dest_ref[k, t], 1), :], sem)

    def zero_copy(row):
        return pltpu.make_async_copy(zrow.at[pl.ds(0, 1), :], xs_hbm.at[pl.ds(row, 1), :], sem)

    def for_rows(fn):
        def body(t, carry):
            for k in range(TOP_K):
                fn(row_copy(t, k))
            return carry
        lax.fori_loop(0, tm, body, 0)

    def for_padding(fn):
        def per_expert(e, carry):
            def body(j, c):
                fn(zero_copy(fs_ref[e] + j))
                return c
            return lax.fori_loop(0, fn_ref[e], body, carry)
        lax.fori_loop(0, N_EXPERTS, per_expert, 0)

    for_rows(lambda cp: cp.start())

    @pl.when(i == 0)
    def _():
        zrow[...] = jnp.zeros_like(zrow)
        for_padding(lambda cp: cp.start())
        for_padding(lambda cp: cp.wait())

    for_rows(lambda cp: cp.wait())


def _dispatch(fill_start, fill_n, dest, x2d, n_rows, tm):
    t = x2d.shape[0]
    return pl.pallas_call(
        _dispatch_kernel,
        grid_spec=pltpu.PrefetchScalarGridSpec(
            num_scalar_prefetch=2,
            grid=(t // tm,),
            in_specs=[
                pl.BlockSpec((TOP_K, tm), lambda i, fs, fn: (0, i), memory_space=pltpu.SMEM),
                pl.BlockSpec((tm, D_PACK), lambda i, fs, fn: (i, 0)),
            ],
            out_specs=pl.BlockSpec(memory_space=pl.ANY),
            scratch_shapes=[pltpu.VMEM((8, D_PACK), jnp.uint32), pltpu.SemaphoreType.DMA],
        ),
        out_shape=jax.ShapeDtypeStruct((n_rows, D_PACK), jnp.uint32),
        compiler_params=_params(("arbitrary",), has_side_effects=True),
        name="moe_dispatch",
    )(fill_start, fill_n, dest, x2d)


def _expert_kernel(be_ref, nu_ref, xs_ref, wg_ref, wu_ref, wd_ref, ys_ref, wgu_b, wd_b):
    i = pl.program_id(0)

    @pl.when(i < nu_ref[0])
    def _():
        prev = be_ref[jnp.maximum(i - 1, 0)]

        @pl.when(jnp.logical_or(i == 0, be_ref[i] != prev))
        def _():
            wgu_b[:, :D_EXPERT] = wg_ref[...].astype(BF16)
            wgu_b[:, D_EXPERT:] = wu_ref[...].astype(BF16)
            wd_b[...] = wd_ref[...].astype(BF16)

        x_hi, x_lo = _unpack_rows(xs_ref[...])
        gu = _dot(x_hi.astype(BF16), wgu_b[:D_PACK, :]) + _dot(x_lo.astype(BF16), wgu_b[D_PACK:, :])
        h = _silu(gu[:, :D_EXPERT]) * gu[:, D_EXPERT:]
        ys_ref[...] = _pack_rows(_dot(h.astype(BF16), wd_b[...]))


def _experts(blk_e, n_used, xs, w_gate, w_up, w_down):
    n_rows = xs.shape[0]
    n_blocks = n_rows // MOE_BLOCK
    blk = lambda i, be, nu: (jnp.minimum(i, nu[0] - 1), 0)
    exp = lambda i, be, nu: (be[i], 0, 0)
    return pl.pallas_call(
        _expert_kernel,
        grid_spec=pltpu.PrefetchScalarGridSpec(
            num_scalar_prefetch=2,
            grid=(n_blocks,),
            in_specs=[
                pl.BlockSpec((MOE_BLOCK, D_PACK), blk),
                pl.BlockSpec((None, D_MODEL, D_EXPERT), exp),
                pl.BlockSpec((None, D_MODEL, D_EXPERT), exp),
                pl.BlockSpec((None, D_EXPERT, D_MODEL), exp),
            ],
            out_specs=pl.BlockSpec((MOE_BLOCK, D_PACK), blk),
            scratch_shapes=[
                pltpu.VMEM((D_MODEL, 2 * D_EXPERT), BF16),
                pltpu.VMEM((D_EXPERT, D_MODEL), BF16),
            ],
        ),
        out_shape=jax.ShapeDtypeStruct((n_rows, D_PACK), jnp.uint32),
        compiler_params=_params(("arbitrary",)),
        name="moe_experts",
    )(blk_e, n_used, xs, w_gate, w_up, w_down)


def _combine_kernel(dest_ref, h1_ref, wts_ref, wsgu_ref, wsd_ref, g_ref, b_ref, ys_hbm, out_ref, buf, sem):
    tm = h1_ref.shape[0]

    def issue(t, carry):
        for k in range(TOP_K):
            pltpu.make_async_copy(ys_hbm.at[pl.ds(dest_ref[k, t], 1), :],
                                  buf.at[k, pl.ds(t, 1), :], sem).start()
        return carry

    lax.fori_loop(0, tm, issue, 0)

    h1 = h1_ref[...]
    gu = _dot(h1.astype(BF16), wsgu_ref[...])
    hs = _silu(gu[:, :D_SHARED]) * gu[:, D_SHARED:]
    acc = DEEPNORM_ALPHA * h1 + _dot(hs.astype(BF16), wsd_ref[...])

    def drain(t, carry):
        for k in range(TOP_K):
            pltpu.make_async_copy(ys_hbm.at[pl.ds(dest_ref[k, t], 1), :],
                                  buf.at[k, pl.ds(t, 1), :], sem).wait()
        return carry

    lax.fori_loop(0, tm, drain, 0)

    wts = wts_ref[...]
    acc_hi = acc[:, :D_PACK]
    acc_lo = acc[:, D_PACK:]
    for k in range(TOP_K):
        y_hi, y_lo = _unpack_rows(buf[k])
        acc_hi = acc_hi + y_hi * wts[:, k:k + 1]
        acc_lo = acc_lo + y_lo * wts[:, k:k + 1]
    out_ref[...] = _layer_norm(jnp.concatenate([acc_hi, acc_lo], axis=1), g_ref[...], b_ref[...])


def _combine(dest, h1, wts_t, ws_gu, ws_down, g, b, ys, tm):
    t = h1.shape[0]
    const = lambda shape: pl.BlockSpec(shape, lambda i: (0,) * len(shape))
    return pl.pallas_call(
        _combine_kernel,
        grid=(t // tm,),
        in_specs=[
            pl.BlockSpec((TOP_K, tm), lambda i: (0, i), memory_space=pltpu.SMEM),
            pl.BlockSpec((tm, D_MODEL), lambda i: (i, 0)),
            pl.BlockSpec((tm, TOP_K), lambda i: (i, 0)),
            const((D_MODEL, 2 * D_SHARED)), const((D_SHARED, D_MODEL)),
            const((1, D_MODEL)), const((1, D_MODEL)),
            pl.BlockSpec(memory_space=pl.ANY),
        ],
        out_specs=pl.BlockSpec((tm, D_MODEL), lambda i: (i, 0)),
        out_shape=jax.ShapeDtypeStruct((t, D_MODEL), F32),
        scratch_shapes=[
            pltpu.VMEM((TOP_K, tm, D_PACK), jnp.uint32),
            pltpu.SemaphoreType.DMA,
        ],
        compiler_params=_params(("arbitrary",)),
        name="moe_combine",
    )(dest, h1, wts_t, ws_gu, ws_down, g, b, ys)


def _block_diag(w):
    nb, bi, bo = w.shape
    eye = jnp.eye(nb, dtype=w.dtype)
    return (eye[:, None, :, None] * w[:, :, None, :]).reshape(nb * bi, nb * bo)


def _pad_lanes(v, offset, width):
    return jnp.zeros((1, width), F32).at[0, offset:offset + v.shape[0]].set(v)


def _layer(h_in_x, l, p, tiles):
    bsz, seq, _ = h_in_x.shape
    t = bsz * seq
    row = lambda v: v.reshape(1, -1)

    w_in = p['w_in'][l]
    w_main = w_in[:, :N_MAIN].astype(BF16)
    w_small = jnp.zeros((D_MODEL, LANES), F32).at[:, :2 * GDN_HEADS].set(w_in[:, N_MAIN:])
    w_small_t = w_in[:, N_MAIN:].T
    h0, proj, small, small_t = _inproj(h_in_x.reshape(t, D_MODEL), row(p['ln_g']), row(p['ln_b']),
                                       w_main, w_small, w_small_t, tiles['inproj'])
    proj3 = proj.reshape(bsz, seq, N_MAIN)

    w_gates = jnp.concatenate([_block_diag(p['lru_w_rg'][l]), _block_diag(p['lru_w_ig'][l])], 1).astype(BF16)
    b_gates = jnp.concatenate([p['lru_b_rg'][l], p['lru_b_ig'][l]]).reshape(1, -1)
    y_lru = _lru(proj3, p['lru_conv_w'][l], row(p['lru_conv_b'][l]), w_gates, b_gates,
                 row(p['lru_lambda'][l]), row(p['lru_out_g'][l]), tiles['lru'])

    nch = seq // GDN_CHUNK
    small3 = small.reshape(bsz, seq, LANES)
    smallt3 = small_t.reshape(8, bsz, nch, GDN_CHUNK).transpose(1, 2, 0, 3)
    cw = p['gdn_conv_w'][l]
    a_log, dt_bias = p['gdn_a_log'][l], p['gdn_dt_bias'][l]
    alr = _pad_lanes(a_log, GDN_HEADS, LANES)
    dtr = _pad_lanes(dt_bias, GDN_HEADS, LANES)
    alc = _pad_lanes(a_log, GDN_HEADS, 8).reshape(8, 1)
    dtc = _pad_lanes(dt_bias, GDN_HEADS, 8).reshape(8, 1)
    y_gdn = _gdn(proj3, small3, smallt3, cw[:, :GDN_QK], cw[:, GDN_QK:2 * GDN_QK], cw[:, 2 * GDN_QK:],
                 alr, dtr, alc, dtc, row(p['gdn_norm_w'][l]), tiles['gdn_nb'])

    w_out = p['w_out'][l].astype(BF16)
    h1, h1p, top_e, wts, rank, counts = _router(
        y_lru.reshape(t, LRU_WIDTH), y_gdn.reshape(t, GDN_V), h0, w_out[:LRU_WIDTH], w_out[LRU_WIDTH:],
        row(p['ln1_g'][l]), row(p['ln1_b'][l]), p['w_router'][l].T, p['router_bias'][l].reshape(-1, 1),
        tiles['router'])

    counts = counts[:, 0]
    padded = (counts + MOE_BLOCK - 1) // MOE_BLOCK * MOE_BLOCK
    pad_end = jnp.cumsum(padded)
    pad_start = pad_end - padded
    n_blocks = (t * TOP_K) // MOE_BLOCK + N_EXPERTS
    n_rows = n_blocks * MOE_BLOCK
    n_used = (pad_end[-1] // MOE_BLOCK).astype(jnp.int32)
    blk_ids = jnp.minimum(jnp.arange(n_blocks, dtype=jnp.int32), n_used - 1)
    blk_e = jnp.minimum(jnp.sum(pad_end[None, :] <= (blk_ids * MOE_BLOCK)[:, None], axis=1),
                        N_EXPERTS - 1).astype(jnp.int32)

    dest = _dest(top_e, rank, pad_start.reshape(-1, 1), tiles['dest'])
    xs = _dispatch(pad_start + counts, padded - counts, dest, h1p, n_rows, tiles['dispatch'])
    ys = _experts(blk_e, n_used.reshape(1), xs, p['w_gate'][l], p['w_up'][l], p['w_down'][l])
    ws_gu = jnp.concatenate([p['ws_gate'][l], p['ws_up'][l]], 1).astype(BF16)
    out = _combine(dest, h1, wts.T, ws_gu, p['ws_down'][l].astype(BF16),
                   row(p['ln2_g'][l]), row(p['ln2_b'][l]), ys, tiles['combine'])
    return out.reshape(bsz, seq, D_MODEL)


def _tiles(bsz, seq):
    t = bsz * seq
    return {
        'inproj': min(256, t),
        'lru': min(256, seq),
        'gdn_nb': bsz,
        'router': min(256, t),
        'dest': min(512, t),
        'dispatch': min(512, t),
        'combine': min(128, t),
    }


def kernel(x, ln_in_g, ln_in_b, w_in, lru_conv_w, lru_conv_b, lru_w_rg, lru_b_rg, lru_w_ig, lru_b_ig,
           lru_lambda, lru_out_g, gdn_conv_w, gdn_a_log, gdn_dt_bias, gdn_norm_w, w_out, ln1_g, ln1_b,
           w_router, router_bias, w_gate, w_up, w_down, ws_gate, ws_up, ws_down, ln2_g, ln2_b):
    assert w_in.shape[0] == DEPTH == 1
    p = dict(ln_g=ln_in_g, ln_b=ln_in_b, w_in=w_in, lru_conv_w=lru_conv_w, lru_conv_b=lru_conv_b,
             lru_w_rg=lru_w_rg, lru_b_rg=lru_b_rg, lru_w_ig=lru_w_ig, lru_b_ig=lru_b_ig,
             lru_lambda=lru_lambda, lru_out_g=lru_out_g, gdn_conv_w=gdn_conv_w, gdn_a_log=gdn_a_log,
             gdn_dt_bias=gdn_dt_bias, gdn_norm_w=gdn_norm_w, w_out=w_out, ln1_g=ln1_g, ln1_b=ln1_b,
             w_router=w_router, router_bias=router_bias, w_gate=w_gate, w_up=w_up, w_down=w_down,
             ws_gate=ws_gate, ws_up=ws_up, ws_down=ws_down, ln2_g=ln2_g, ln2_b=ln2_b)
    bsz, seq, _ = x.shape
    return _layer(x, 0, p, _tiles(bsz, seq))
```

```python
import functools

import jax
import jax.numpy as jnp
from jax import lax
from jax.experimental import pallas as pl
from jax.experimental.pallas import tpu as pltpu
from jax.experimental.pallas import tpu_sc as plsc

F32 = jnp.float32
BF16 = jnp.bfloat16

D_MODEL = 1024
LRU_WIDTH = 512
LRU_BLOCKS = 8
LRU_C = 8.0
CONV_WIDTH = 4
GDN_HEADS = 4
GDN_DK = 128
GDN_DV = 128
GDN_CHUNK = 64
GDN_QK = GDN_HEADS * GDN_DK
GDN_V = GDN_HEADS * GDN_DV
N_MAIN = 2 * LRU_WIDTH + 2 * GDN_QK + 2 * GDN_V
N_EXPERTS = 256
TOP_K = 8
N_GROUPS = 8
GROUP_SIZE = N_EXPERTS // N_GROUPS
TOPK_GROUPS = 4
D_EXPERT = 256
D_SHARED = 256
ROUTED_SCALE = 2.5
MOE_BLOCK = 256
D_PACK = D_MODEL // 2
LN_EPS = 1e-5
NORM_EPS = 1e-6
DEPTH = 1
DEEPNORM_ALPHA = (2.0 * DEPTH) ** 0.25

HALO = 8
LANES = 128
VMEM_LIMIT = 56 * 1024 * 1024
SC_CORES = 2
SC_WORKERS = 32
SC_CHUNK = 64

NN = (((1,), (0,)), ((), ()))
NT = (((1,), (1,)), ((), ()))
TN = (((0,), (0,)), ((), ()))


def _dot(a, b, dims=NN):
    return lax.dot_general(a, b, dims, preferred_element_type=F32)


def _split(a):
    hi = a.astype(BF16)
    lo = (a - hi.astype(F32)).astype(BF16)
    return hi, lo


def _dot3(a, b, dims=NN):
    ah, al = _split(a)
    bh, bl = _split(b)
    return _dot(ah, bh, dims) + (_dot(ah, bl, dims) + _dot(al, bh, dims))


def _layer_norm(x, g, b):
    mu = jnp.mean(x, -1, keepdims=True)
    xc = x - mu
    var = jnp.mean(xc * xc, -1, keepdims=True)
    return xc * lax.rsqrt(var + LN_EPS) * g + b


def _sigmoid(x):
    return 1.0 / (1.0 + jnp.exp(-x))


def _silu(x):
    return x * _sigmoid(x)


def _softplus(x):
    return jnp.maximum(x, 0.0) + jnp.log1p(jnp.exp(-jnp.abs(x)))


def _gelu_tanh(x):
    c = 0.7978845608028654
    return x * (0.5 * (1.0 + jnp.tanh(c * (x + 0.044715 * (x * x * x)))))


def _pack_rows(x):
    hi = lax.bitcast_convert_type(x[:, :D_PACK].astype(BF16).astype(F32), jnp.uint32)
    lo = lax.bitcast_convert_type(x[:, D_PACK:].astype(BF16).astype(F32), jnp.uint32)
    return (hi & jnp.uint32(0xFFFF0000)) | (lo >> 16)


def _unpack_rows(w):
    hi = lax.bitcast_convert_type(w & jnp.uint32(0xFFFF0000), F32)
    lo = lax.bitcast_convert_type(w << 16, F32)
    return hi, lo


def _params(sem, **kw):
    return pltpu.CompilerParams(dimension_semantics=sem, vmem_limit_bytes=VMEM_LIMIT, **kw)


def _inproj_kernel(x_ref, g_ref, b_ref, w_ref, ws_ref, wst_ref, h_ref, proj_ref, small_ref, smallt_ref):
    h = _layer_norm(x_ref[...], g_ref[...], b_ref[...])
    h_ref[...] = h
    hb = h.astype(BF16)
    proj_ref[...] = _dot(hb, w_ref[...])
    small_ref[...] = _dot3(h, ws_ref[...])
    smallt_ref[...] = _dot3(wst_ref[...], h, NT)


def _inproj(x2d, g, b, w_main, w_small, w_small_t, tm):
    t = x2d.shape[0]
    return pl.pallas_call(
        _inproj_kernel,
        grid=(t // tm,),
        in_specs=[
            pl.BlockSpec((tm, D_MODEL), lambda i: (i, 0)),
            pl.BlockSpec((1, D_MODEL), lambda i: (0, 0)),
            pl.BlockSpec((1, D_MODEL), lambda i: (0, 0)),
            pl.BlockSpec((D_MODEL, N_MAIN), lambda i: (0, 0)),
            pl.BlockSpec((D_MODEL, LANES), lambda i: (0, 0)),
            pl.BlockSpec((8, D_MODEL), lambda i: (0, 0)),
        ],
        out_specs=[
            pl.BlockSpec((tm, D_MODEL), lambda i: (i, 0)),
            pl.BlockSpec((tm, N_MAIN), lambda i: (i, 0)),
            pl.BlockSpec((tm, LANES), lambda i: (i, 0)),
            pl.BlockSpec((8, tm), lambda i: (0, i)),
        ],
        out_shape=[
            jax.ShapeDtypeStruct((t, D_MODEL), F32),
            jax.ShapeDtypeStruct((t, N_MAIN), F32),
            jax.ShapeDtypeStruct((t, LANES), F32),
            jax.ShapeDtypeStruct((8, t), F32),
        ],
        compiler_params=_params(("arbitrary",)),
        name="ln_inproj",
    )(x2d, g, b, w_main, w_small, w_small_t)


def _causal_conv(buf_ref, x, w_ref, first, rows):
    @pl.when(first)
    def _():
        buf_ref[0:HALO, :] = jnp.zeros((HALO, buf_ref.shape[1]), F32)

    buf_ref[HALO:HALO + rows, :] = x
    acc = None
    for j in range(CONV_WIDTH):
        off = HALO - (CONV_WIDTH - 1) + j
        term = buf_ref[off:off + rows, :] * w_ref[j:j + 1, :]
        acc = term if acc is None else acc + term
    buf_ref[0:HALO, :] = buf_ref[rows:rows + HALO, :]
    return acc


def _shift_rows(x, d, fill):
    rows = x.shape[0]
    if d % 8 == 0:
        pad = jnp.full((d, x.shape[1]), fill, x.dtype)
        return jnp.concatenate([pad, x[:rows - d]], axis=0)
    rolled = pltpu.roll(x, d, 0)
    row = lax.broadcasted_iota(jnp.int32, x.shape, 0)
    return jnp.where(row < d, fill, rolled)


def _lru_kernel(u_ref, gate_ref, cw_ref, cb_ref, wg_ref, bg_ref, lam_ref, og_ref,
                y_ref, ubuf, hcarry):
    s = pl.program_id(1)
    rows = u_ref.shape[0]

    @pl.when(s == 0)
    def _():
        hcarry[...] = jnp.zeros_like(hcarry)

    xc = _causal_conv(ubuf, u_ref[...], cw_ref, s == 0, rows) + cb_ref[...]
    gates = _dot(xc.astype(BF16), wg_ref[...]) + bg_ref[...]
    r = _sigmoid(gates[:, :LRU_WIDTH])
    i = _sigmoid(gates[:, LRU_WIDTH:])
    log_a = (-LRU_C) * r * _softplus(-lam_ref[...])
    a = jnp.exp(log_a)
    mult = jnp.sqrt(-jnp.tanh(log_a) * (a * a + 1.0))
    bv = mult * (i * xc)
    d = 1
    while d < rows:
        a_sh = _shift_rows(a, d, 1.0)
        b_sh = _shift_rows(bv, d, 0.0)
        bv = a * b_sh + bv
        a = a * a_sh
        d *= 2
    h = a * hcarry[...] + bv
    hcarry[...] = h[rows - 1:rows, :]
    y = h * _gelu_tanh(gate_ref[...])
    ms = jnp.mean(y * y, -1, keepdims=True)
    y_ref[...] = y * lax.rsqrt(ms + NORM_EPS) * og_ref[...]


def _lru(proj3, conv_w, conv_b, w_gates, b_gates, lam, out_g, ts):
    bsz, seq, _ = proj3.shape
    row = lambda n: pl.BlockSpec((1, n), lambda b, s: (0, 0))
    return pl.pallas_call(
        _lru_kernel,
        grid=(bsz, seq // ts),
        in_specs=[
            pl.BlockSpec((None, ts, LRU_WIDTH), lambda b, s: (b, s, 0)),
            pl.BlockSpec((None, ts, LRU_WIDTH), lambda b, s: (b, s, 1)),
            pl.BlockSpec((CONV_WIDTH, LRU_WIDTH), lambda b, s: (0, 0)),
            row(LRU_WIDTH),
            pl.BlockSpec((LRU_WIDTH, 2 * LRU_WIDTH), lambda b, s: (0, 0)),
            row(2 * LRU_WIDTH),
            row(LRU_WIDTH),
            row(LRU_WIDTH),
        ],
        out_specs=pl.BlockSpec((None, ts, LRU_WIDTH), lambda b, s: (b, s, 0)),
        out_shape=jax.ShapeDtypeStruct((bsz, seq, LRU_WIDTH), F32),
        scratch_shapes=[
            pltpu.VMEM((HALO + ts, LRU_WIDTH), F32),
            pltpu.VMEM((1, LRU_WIDTH), F32),
        ],
        compiler_params=_params(("arbitrary", "arbitrary")),
        name="rg_lru",
    )(proj3, proj3, conv_w, conv_b, w_gates, b_gates, lam, out_g)


def _bdot(a, b, dims=NN):
    return _dot(a.astype(BF16), b.astype(BF16), dims)


def _gdn_heads(args, norm_w, causal, strict, upper):
    c = GDN_CHUNK
    each = lambda f, *ls: [f(*xs) for xs in zip(*ls)]
    q, k, v, z, beta, g_col, g_row, st = [list(x) for x in zip(*args)]
    q = each(lambda x: x * lax.rsqrt(jnp.sum(x * x, -1, keepdims=True) + NORM_EPS) * (GDN_DK ** -0.5), q)
    k = each(lambda x: x * lax.rsqrt(jnp.sum(x * x, -1, keepdims=True) + NORM_EPS), k)
    gc_col = each(lambda g: jnp.sum(jnp.where(causal, g, 0.0), axis=1, keepdims=True), g_row)
    gc_row = each(lambda g: jnp.sum(jnp.where(upper, g, 0.0), axis=0, keepdims=True), g_col)
    decay = each(lambda gc, gr: jnp.exp(jnp.where(causal, gc - gr, -jnp.inf)), gc_col, gc_row)
    kb = each(lambda x, bt: x * bt, k, beta)
    vb = each(lambda x, bt: x * bt, v, beta)
    kk = each(lambda x, y: _bdot(x, y, NT), kb, k)
    a_mat = each(lambda m, d: jnp.where(strict, m * d, 0.0), kk, decay)
    e_col = each(jnp.exp, gc_col)
    rhs = each(lambda x, y, e: jnp.concatenate([x, y * e], axis=1), vb, kb, e_col)
    sol = each(lambda r, a: r - _bdot(a, r), rhs, a_mat)
    p = a_mat
    for _ in range(5):
        p = each(lambda x: _bdot(x, x), p)
        sol = each(lambda x, y: y + _bdot(x, y), p, sol)
    qk = each(lambda x, y: _bdot(x, y, NT), q, k)
    qk = each(lambda m, d: jnp.where(causal, m * d, 0.0), qk, decay)
    q_dec = each(lambda x, e: x * e, q, e_col)
    g_last = each(lambda gc: gc[c - 1:c, :], gc_col)
    k_dec = each(lambda x, gl, gc: x * jnp.exp(gl - gc), k, g_last, gc_col)
    ws = each(lambda x, s: _bdot(x[:, GDN_DV:], s), sol, st)
    qs = each(lambda x, s: _bdot(x, s), q_dec, st)
    v_new = each(lambda x, w: x[:, :GDN_DV] - w, sol, ws)
    o = each(lambda a, m, vn: a + _bdot(m, vn), qs, qk, v_new)
    kv = each(lambda x, vn: _bdot(x, vn, TN), k_dec, v_new)
    st_new = each(lambda s, gl, d: s * jnp.exp(gl) + d, st, g_last, kv)
    o = each(lambda x: x * lax.rsqrt(jnp.mean(x * x, -1, keepdims=True) + NORM_EPS) * norm_w, o)
    o = each(lambda x, zz: x * _silu(zz), o, z)
    return list(zip(o, st_new))


def _gdn_kernel(q_ref, k_ref, v_ref, z_ref, sm_ref, smt_ref, cwq_ref, cwk_ref, cwv_ref,
                alr_ref, dtr_ref, alc_ref, dtc_ref, nw_ref, y_ref, qbuf, kbuf, vbuf, state):
    n = pl.program_id(1)
    c = GDN_CHUNK
    nb = q_ref.shape[0]
    first = n == 0

    @pl.when(first)
    def _():
        state[...] = jnp.zeros_like(state)

    ri = lax.broadcasted_iota(jnp.int32, (c, c), 0)
    ci = lax.broadcasted_iota(jnp.int32, (c, c), 1)
    causal = ri >= ci
    strict = ri > ci
    upper = ri <= ci
    norm_w = nw_ref[...]

    args = []
    for b in range(nb):
        q_all = _silu(_causal_conv(qbuf.at[b], q_ref[b], cwq_ref, first, c))
        k_all = _silu(_causal_conv(kbuf.at[b], k_ref[b], cwk_ref, first, c))
        v_all = _silu(_causal_conv(vbuf.at[b], v_ref[b], cwv_ref, first, c))
        z_all = z_ref[b]
        sm = sm_ref[b]
        beta_all = _sigmoid(sm)
        g_cols = -jnp.exp(alr_ref[...]) * _softplus(sm + dtr_ref[...])
        g_rows = -jnp.exp(alc_ref[...]) * _softplus(smt_ref[b] + dtc_ref[...])
        for hd in range(GDN_HEADS):
            sl = slice(hd * GDN_DK, (hd + 1) * GDN_DK)
            args.append((q_all[:, sl], k_all[:, sl], v_all[:, sl], z_all[:, sl],
                         beta_all[:, hd:hd + 1],
                         g_cols[:, GDN_HEADS + hd:GDN_HEADS + hd + 1],
                         g_rows[GDN_HEADS + hd:GDN_HEADS + hd + 1, :],
                         state[b, hd]))
    outs = _gdn_heads(args, norm_w, causal, strict, upper)
    for b in range(nb):
        for hd in range(GDN_HEADS):
            o, st_new = outs[b * GDN_HEADS + hd]
            state[b, hd] = st_new
            y_ref[b, :, hd * GDN_DK:(hd + 1) * GDN_DK] = o


def _gdn(proj3, small3, smallt3, cwq, cwk, cwv, alr, dtr, alc, dtc, norm_w, nb):
    bsz, seq, _ = proj3.shape
    c = GDN_CHUNK
    nch = seq // c
    col = lambda j: pl.BlockSpec((nb, c, GDN_QK), lambda b, n: (b, n, j))
    const = lambda shape: pl.BlockSpec(shape, lambda b, n: (0,) * len(shape))
    return pl.pallas_call(
        _gdn_kernel,
        grid=(bsz // nb, nch),
        in_specs=[
            col(2), col(3), col(4), col(5),
            pl.BlockSpec((nb, c, LANES), lambda b, n: (b, n, 0)),
            pl.BlockSpec((nb, None, 8, c), lambda b, n: (b, n, 0, 0)),
            const((CONV_WIDTH, GDN_QK)), const((CONV_WIDTH, GDN_QK)), const((CONV_WIDTH, GDN_V)),
            const((1, LANES)), const((1, LANES)), const((8, 1)), const((8, 1)),
            const((1, GDN_DV)),
        ],
        out_specs=pl.BlockSpec((nb, c, GDN_V), lambda b, n: (b, n, 0)),
        out_shape=jax.ShapeDtypeStruct((bsz, seq, GDN_V), F32),
        scratch_shapes=[
            pltpu.VMEM((nb, HALO + c, GDN_QK), F32),
            pltpu.VMEM((nb, HALO + c, GDN_QK), F32),
            pltpu.VMEM((nb, HALO + c, GDN_V), F32),
            pltpu.VMEM((nb, GDN_HEADS, GDN_DK, GDN_DV), F32),
        ],
        compiler_params=_params(("arbitrary", "arbitrary")),
        name="gated_deltanet",
    )(proj3, proj3, proj3, proj3, small3, smallt3, cwq, cwk, cwv, alr, dtr, alc, dtc, norm_w)


def _router_kernel(yl_ref, yg_ref, h0_ref, wo1_ref, wo2_ref, g_ref, b_ref, wrt_ref, rb_ref,
                   h1_ref, h1p_ref, e_ref, w_ref, rank_ref, cnt_ref, carry):
    i = pl.program_id(0)
    tm = h0_ref.shape[0]

    @pl.when(i == 0)
    def _():
        carry[...] = jnp.zeros_like(carry)

    mix = _dot(yl_ref[...].astype(BF16), wo1_ref[...]) + _dot(yg_ref[...].astype(BF16), wo2_ref[...])
    h1 = _layer_norm(DEEPNORM_ALPHA * h0_ref[...] + mix, g_ref[...], b_ref[...])
    h1_ref[...] = h1
    h1p_ref[...] = _pack_rows(h1)

    scores = _sigmoid(_dot3(wrt_ref[...], h1, NT))
    choice = scores + rb_ref[...]
    neg = -jnp.inf
    gs_rows = []
    sub = lax.broadcasted_iota(jnp.int32, (GROUP_SIZE, tm), 0).astype(F32)
    for g in range(N_GROUPS):
        cg = choice[g * GROUP_SIZE:(g + 1) * GROUP_SIZE, :]
        m1 = jnp.max(cg, axis=0, keepdims=True)
        i1 = jnp.min(jnp.where(cg == m1, sub, float(GROUP_SIZE)), axis=0, keepdims=True)
        m2 = jnp.max(jnp.where(sub == i1, neg, cg), axis=0, keepdims=True)
        gs_rows.append(m1 + m2)
    gs = jnp.concatenate(gs_rows, axis=0)
    gi = lax.broadcasted_iota(jnp.int32, (N_GROUPS, tm), 0).astype(F32)
    gsel = jnp.zeros((N_GROUPS, tm), jnp.bool_)
    for _ in range(TOPK_GROUPS):
        m = jnp.max(gs, axis=0, keepdims=True)
        idx = jnp.min(jnp.where(gs == m, gi, float(N_GROUPS)), axis=0, keepdims=True)
        hit = gi == idx
        gsel = jnp.logical_or(gsel, hit)
        gs = jnp.where(hit, neg, gs)
    masked = jnp.concatenate(
        [jnp.where(gsel[g:g + 1, :], choice[g * GROUP_SIZE:(g + 1) * GROUP_SIZE, :], neg)
         for g in range(N_GROUPS)], axis=0)

    ei = lax.broadcasted_iota(jnp.int32, (N_EXPERTS, tm), 0).astype(F32)
    hits = []
    e_rows, w_rows = [], []
    multi = jnp.zeros((N_EXPERTS, tm), F32)
    for _ in range(TOP_K):
        m = jnp.max(masked, axis=0, keepdims=True)
        idx = jnp.min(jnp.where(masked == m, ei, float(N_EXPERTS)), axis=0, keepdims=True)
        hit = ei == idx
        hits.append(hit)
        e_rows.append(idx)
        w_rows.append(jnp.sum(jnp.where(hit, scores, 0.0), axis=0, keepdims=True))
        multi = multi + hit.astype(F32)
        masked = jnp.where(hit, neg, masked)
    wts = jnp.concatenate(w_rows, axis=0)
    wts = wts / (jnp.sum(wts, axis=0, keepdims=True) + 1e-20) * ROUTED_SCALE
    ti = lax.broadcasted_iota(jnp.int32, (tm, tm), 0)
    tj = lax.broadcasted_iota(jnp.int32, (tm, tm), 1)
    before = (ti < tj).astype(BF16)
    cum = _dot(multi.astype(BF16), before) + carry[...]
    r_rows = [jnp.sum(jnp.where(hit, cum, 0.0), axis=0, keepdims=True) for hit in hits]
    carry[...] = carry[...] + jnp.sum(multi, axis=1, keepdims=True)
    e_ref[...] = jnp.concatenate(e_rows, axis=0).astype(jnp.int32)
    w_ref[...] = wts
    rank_ref[...] = jnp.concatenate(r_rows, axis=0).astype(jnp.int32)
    cnt_ref[...] = carry[...].astype(jnp.int32)


def _router(y_lru, y_gdn, h0, wo1, wo2, g, b, w_router_t, rbias, tm):
    t = h0.shape[0]
    const = lambda shape: pl.BlockSpec(shape, lambda i: (0,) * len(shape))
    return pl.pallas_call(
        _router_kernel,
        grid=(t // tm,),
        in_specs=[
            pl.BlockSpec((tm, LRU_WIDTH), lambda i: (i, 0)),
            pl.BlockSpec((tm, GDN_V), lambda i: (i, 0)),
            pl.BlockSpec((tm, D_MODEL), lambda i: (i, 0)),
            const((LRU_WIDTH, D_MODEL)), const((GDN_V, D_MODEL)),
            const((1, D_MODEL)), const((1, D_MODEL)),
            const((N_EXPERTS, D_MODEL)), const((N_EXPERTS, 1)),
        ],
        out_specs=[
            pl.BlockSpec((tm, D_MODEL), lambda i: (i, 0)),
            pl.BlockSpec((tm, D_PACK), lambda i: (i, 0)),
            pl.BlockSpec((TOP_K, tm), lambda i: (0, i)),
            pl.BlockSpec((TOP_K, tm), lambda i: (0, i)),
            pl.BlockSpec((TOP_K, tm), lambda i: (0, i)),
            const((N_EXPERTS, 1)),
        ],
        out_shape=[
            jax.ShapeDtypeStruct((t, D_MODEL), F32),
            jax.ShapeDtypeStruct((t, D_PACK), jnp.uint32),
            jax.ShapeDtypeStruct((TOP_K, t), jnp.int32),
            jax.ShapeDtypeStruct((TOP_K, t), F32),
            jax.ShapeDtypeStruct((TOP_K, t), jnp.int32),
            jax.ShapeDtypeStruct((N_EXPERTS, 1), jnp.int32),
        ],
        scratch_shapes=[pltpu.VMEM((N_EXPERTS, 1), F32)],
        compiler_params=_params(("arbitrary",)),
        name="outproj_router",
    )(y_lru, y_gdn, h0, wo1, wo2, g, b, w_router_t, rbias)


def _dest_kernel(e_ref, r_ref, ps_ref, d_ref):
    tm = e_ref.shape[1]
    ei = lax.broadcasted_iota(jnp.int32, (N_EXPERTS, tm), 0)
    rows = []
    for k in range(TOP_K):
        hit = ei == e_ref[k:k + 1, :]
        rows.append(jnp.sum(jnp.where(hit, ps_ref[...], 0), axis=0, keepdims=True))
    d_ref[...] = jnp.concatenate(rows, axis=0) + r_ref[...]


def _dest(top_e, rank, pad_start, tm):
    t = top_e.shape[1]
    blk = pl.BlockSpec((TOP_K, tm), lambda i: (0, i))
    return pl.pallas_call(
        _dest_kernel,
        grid=(t // tm,),
        in_specs=[blk, blk, pl.BlockSpec((N_EXPERTS, 1), lambda i: (0, 0))],
        out_specs=blk,
        out_shape=jax.ShapeDtypeStruct((TOP_K, t), jnp.int32),
        compiler_params=_params(("arbitrary",)),
        name="moe_dest",
    )(top_e, rank, pad_start)


def _dispatch_kernel(fs_ref, fn_ref, dest_ref, x_ref, xs_hbm, zrow, sem):
    i = pl.program_id(0)
    tm = x_ref.shape[0]

    def row_copy(t, k):
        return pltpu.make_async_copy(x_ref.at[pl.ds(t, 1), :], xs_hbm.at[pl.ds(dest_ref[k, t], 1), :], sem)

    def zero_copy(row):
        return pltpu.make_async_copy(zrow.at[pl.ds(0, 1), :], xs_hbm.at[pl.ds(row, 1), :], sem)

    def for_rows(fn):
        def body(t, carry):
            for k in range(TOP_K):
                fn(row_copy(t, k))
            return carry
        lax.fori_loop(0, tm, body, 0)

    def for_padding(fn):
        def per_expert(e, carry):
            def body(j, c):
                fn(zero_copy(fs_ref[e] + j))
                return c
            return lax.fori_loop(0, fn_ref[e], body, carry)
        lax.fori_loop(0, N_EXPERTS, per_expert, 0)

    for_rows(lambda cp: cp.start())

    @pl.when(i == 0)
    def _():
        zrow[...] = jnp.zeros_like(zrow)
        for_padding(lambda cp: cp.start())
        for_padding(lambda cp: cp.wait())

    for_rows(lambda cp: cp.wait())


def _dispatch(fill_start, fill_n, dest, x2d, n_rows, tm):
    t = x2d.shape[0]
    return pl.pallas_call(
        _dispatch_kernel,
        grid_spec=pltpu.PrefetchScalarGridSpec(
            num_scalar_prefetch=2,
            grid=(t // tm,),
            in_specs=[
                pl.BlockSpec((TOP_K, tm), lambda i, fs, fn: (0, i), memory_space=pltpu.SMEM),
                pl.BlockSpec((tm, D_PACK), lambda i, fs, fn: (i, 0)),
            ],
            out_specs=pl.BlockSpec(memory_space=pl.ANY),
            scratch_shapes=[pltpu.VMEM((8, D_PACK), jnp.uint32), pltpu.SemaphoreType.DMA],
        ),
        out_shape=jax.ShapeDtypeStruct((n_rows, D_PACK), jnp.uint32),
        compiler_params=_params(("arbitrary",), has_side_effects=True),
        name="moe_dispatch",
    )(fill_start, fill_n, dest, x2d)


def _expert_kernel(be_ref, nu_ref, xs_ref, wg_ref, wu_ref, wd_ref, ys_ref, wgu_b, wd_b):
    i = pl.program_id(0)

    @pl.when(i < nu_ref[0])
    def _():
        prev = be_ref[jnp.maximum(i - 1, 0)]

        @pl.when(jnp.logical_or(i == 0, be_ref[i] != prev))
        def _():
            wgu_b[:, :D_EXPERT] = wg_ref[...].astype(BF16)
            wgu_b[:, D_EXPERT:] = wu_ref[...].astype(BF16)
            wd_b[...] = wd_ref[...].astype(BF16)

        x_hi, x_lo = _unpack_rows(xs_ref[...])
        gu = _dot(x_hi.astype(BF16), wgu_b[:D_PACK, :]) + _dot(x_lo.astype(BF16), wgu_b[D_PACK:, :])
        h = _silu(gu[:, :D_EXPERT]) * gu[:, D_EXPERT:]
        ys_ref[...] = _pack_rows(_dot(h.astype(BF16), wd_b[...]))


def _experts(blk_e, n_used, xs, w_gate, w_up, w_down):
    n_rows = xs.shape[0]
    n_blocks = n_rows // MOE_BLOCK
    blk = lambda i, be, nu: (jnp.minimum(i, nu[0] - 1), 0)
    exp = lambda i, be, nu: (be[i], 0, 0)
    return pl.pallas_call(
        _expert_kernel,
        grid_spec=pltpu.PrefetchScalarGridSpec(
            num_scalar_prefetch=2,
            grid=(n_blocks,),
            in_specs=[
                pl.BlockSpec((MOE_BLOCK, D_PACK), blk),
                pl.BlockSpec((None, D_MODEL, D_EXPERT), exp),
                pl.BlockSpec((None, D_MODEL, D_EXPERT), exp),
                pl.BlockSpec((None, D_EXPERT, D_MODEL), exp),
            ],
            out_specs=pl.BlockSpec((MOE_BLOCK, D_PACK), blk),
            scratch_shapes=[
                pltpu.VMEM((D_MODEL, 2 * D_EXPERT), BF16),
                pltpu.VMEM((D_EXPERT, D_MODEL), BF16),
            ],
        ),
        out_shape=jax.ShapeDtypeStruct((n_rows, D_PACK), jnp.uint32),
        compiler_params=_params(("arbitrary",)),
        name="moe_experts",
    )(blk_e, n_used, xs, w_gate, w_up, w_down)


def _sc_gather_rows(table, idx, chunk):
    n_idx = idx.shape[0]
    d = table.shape[1]
    per_worker = n_idx // SC_WORKERS
    n_chunks = per_worker // chunk
    mesh = plsc.VectorSubcoreMesh(core_axis_name="c", subcore_axis_name="s")

    @functools.partial(
        pl.kernel, mesh=mesh,
        out_type=jax.ShapeDtypeStruct((n_idx, d), table.dtype),
        scratch_types=[
            pltpu.VMEM((chunk,), jnp.int32),
            pltpu.VMEM((chunk, d), table.dtype),
            pltpu.SemaphoreType.DMA,
        ],
    )
    def gather(table_hbm, idx_hbm, out_hbm, idx_v, rows_v, sem):
        wid = lax.axis_index("s") * SC_CORES + lax.axis_index("c")
        base = wid * per_worker

        @pl.loop(0, n_chunks)
        def _(j):
            off = base + j * chunk
            pltpu.sync_copy(idx_hbm.at[pl.ds(off, chunk)], idx_v)
            pltpu.async_copy(table_hbm.at[idx_v], rows_v, sem).wait()
            pltpu.sync_copy(rows_v, out_hbm.at[pl.ds(off, chunk)])

    return gather(table, idx)


def _combine_kernel(h1_ref, wts_ref, wsgu_ref, wsd_ref, g_ref, b_ref, yg_ref, out_ref):
    h1 = h1_ref[...]
    gu = _dot(h1.astype(BF16), wsgu_ref[...])
    hs = _silu(gu[:, :D_SHARED]) * gu[:, D_SHARED:]
    acc = DEEPNORM_ALPHA * h1 + _dot(hs.astype(BF16), wsd_ref[...])
    wts = wts_ref[...]
    acc_hi = acc[:, :D_PACK]
    acc_lo = acc[:, D_PACK:]
    for k in range(TOP_K):
        y_hi, y_lo = _unpack_rows(yg_ref[k])
        acc_hi = acc_hi + y_hi * wts[:, k:k + 1]
        acc_lo = acc_lo + y_lo * wts[:, k:k + 1]
    out_ref[...] = _layer_norm(jnp.concatenate([acc_hi, acc_lo], axis=1), g_ref[...], b_ref[...])


def _combine(h1, wts_t, ws_gu, ws_down, g, b, yg, tm):
    t = h1.shape[0]
    const = lambda shape: pl.BlockSpec(shape, lambda i: (0,) * len(shape))
    return pl.pallas_call(
        _combine_kernel,
        grid=(t // tm,),
        in_specs=[
            pl.BlockSpec((tm, D_MODEL), lambda i: (i, 0)),
            pl.BlockSpec((tm, TOP_K), lambda i: (i, 0)),
            const((D_MODEL, 2 * D_SHARED)), const((D_SHARED, D_MODEL)),
            const((1, D_MODEL)), const((1, D_MODEL)),
            pl.BlockSpec((TOP_K, tm, D_PACK), lambda i: (0, i, 0)),
        ],
        out_specs=pl.BlockSpec((tm, D_MODEL), lambda i: (i, 0)),
        out_shape=jax.ShapeDtypeStruct((t, D_MODEL), F32),
        compiler_params=_params(("arbitrary",)),
        name="moe_combine",
    )(h1, wts_t, ws_gu, ws_down, g, b, yg)


def _block_diag(w):
    nb, bi, bo = w.shape
    eye = jnp.eye(nb, dtype=w.dtype)
    return (eye[:, None, :, None] * w[:, :, None, :]).reshape(nb * bi, nb * bo)


def _pad_lanes(v, offset, width):
    return jnp.zeros((1, width), F32).at[0, offset:offset + v.shape[0]].set(v)


def _layer(h_in_x, l, p, tiles):
    bsz, seq, _ = h_in_x.shape
    t = bsz * seq
    row = lambda v: v.reshape(1, -1)

    w_in = p['w_in'][l]
    w_main = w_in[:, :N_MAIN].astype(BF16)
    w_small = jnp.zeros((D_MODEL, LANES), F32).at[:, :2 * GDN_HEADS].set(w_in[:, N_MAIN:])
    w_small_t = w_in[:, N_MAIN:].T
    h0, proj, small, small_t = _inproj(h_in_x.reshape(t, D_MODEL), row(p['ln_g']), row(p['ln_b']),
                                       w_main, w_small, w_small_t, tiles['inproj'])
    proj3 = proj.reshape(bsz, seq, N_MAIN)

    w_gates = jnp.concatenate([_block_diag(p['lru_w_rg'][l]), _block_diag(p['lru_w_ig'][l])], 1).astype(BF16)
    b_gates = jnp.concatenate([p['lru_b_rg'][l], p['lru_b_ig'][l]]).reshape(1, -1)
    y_lru = _lru(proj3, p['lru_conv_w'][l], row(p['lru_conv_b'][l]), w_gates, b_gates,
                 row(p['lru_lambda'][l]), row(p['lru_out_g'][l]), tiles['lru'])

    nch = seq // GDN_CHUNK
    small3 = small.reshape(bsz, seq, LANES)
    smallt3 = small_t.reshape(8, bsz, nch, GDN_CHUNK).transpose(1, 2, 0, 3)
    cw = p['gdn_conv_w'][l]
    a_log, dt_bias = p['gdn_a_log'][l], p['gdn_dt_bias'][l]
    alr = _pad_lanes(a_log, GDN_HEADS, LANES)
    dtr = _pad_lanes(dt_bias, GDN_HEADS, LANES)
    alc = _pad_lanes(a_log, GDN_HEADS, 8).reshape(8, 1)
    dtc = _pad_lanes(dt_bias, GDN_HEADS, 8).reshape(8, 1)
    y_gdn = _gdn(proj3, small3, smallt3, cw[:, :GDN_QK], cw[:, GDN_QK:2 * GDN_QK], cw[:, 2 * GDN_QK:],
                 alr, dtr, alc, dtc, row(p['gdn_norm_w'][l]), tiles['gdn_nb'])

    w_out = p['w_out'][l].astype(BF16)
    h1, h1p, top_e, wts, rank, counts = _router(
        y_lru.reshape(t, LRU_WIDTH), y_gdn.reshape(t, GDN_V), h0, w_out[:LRU_WIDTH], w_out[LRU_WIDTH:],
        row(p['ln1_g'][l]), row(p['ln1_b'][l]), p['w_router'][l].T, p['router_bias'][l].reshape(-1, 1),
        tiles['router'])

    counts = counts[:, 0]
    padded = (counts + MOE_BLOCK - 1) // MOE_BLOCK * MOE_BLOCK
    pad_end = jnp.cumsum(padded)
    pad_start = pad_end - padded
    n_blocks = (t * TOP_K) // MOE_BLOCK + N_EXPERTS
    n_rows = n_blocks * MOE_BLOCK
    n_used = (pad_end[-1] // MOE_BLOCK).astype(jnp.int32)
    blk_ids = jnp.minimum(jnp.arange(n_blocks, dtype=jnp.int32), n_used - 1)
    blk_e = jnp.minimum(jnp.sum(pad_end[None, :] <= (blk_ids * MOE_BLOCK)[:, None], axis=1),
                        N_EXPERTS - 1).astype(jnp.int32)

    dest = _dest(top_e, rank, pad_start.reshape(-1, 1), tiles['dest'])
    xs = _dispatch(pad_start + counts, padded - counts, dest, h1p, n_rows, tiles['dispatch'])
    ys = _experts(blk_e, n_used.reshape(1), xs, p['w_gate'][l], p['w_up'][l], p['w_down'][l])
    ws_gu = jnp.concatenate([p['ws_gate'][l], p['ws_up'][l]], 1).astype(BF16)
    yg = _sc_gather_rows(ys, dest.reshape(TOP_K * t), SC_CHUNK).reshape(TOP_K, t, D_PACK)
    out = _combine(h1, wts.T, ws_gu, p['ws_down'][l].astype(BF16),
                   row(p['ln2_g'][l]), row(p['ln2_b'][l]), yg, tiles['combine'])
    return out.reshape(bsz, seq, D_MODEL)


def _tiles(bsz, seq):
    t = bsz * seq
    return {
        'inproj': min(256, t),
        'lru': min(256, seq),
        'gdn_nb': bsz,
        'router': min(256, t),
        'dest': min(512, t),
        'dispatch': min(512, t),
        'combine': min(256, t),
    }


def kernel(x, ln_in_g, ln_in_b, w_in, lru_conv_w, lru_conv_b, lru_w_rg, lru_b_rg, lru_w_ig, lru_b_ig,
           lru_lambda, lru_out_g, gdn_conv_w, gdn_a_log, gdn_dt_bias, gdn_norm_w, w_out, ln1_g, ln1_b,
           w_router, router_bias, w_gate, w_up, w_down, ws_gate, ws_up, ws_down, ln2_g, ln2_b):
    assert w_in.shape[0] == DEPTH == 1
    p = dict(ln_g=ln_in_g, ln_b=ln_in_b, w_in=w_in, lru_conv_w=lru_conv_w, lru_conv_b=lru_conv_b,
             lru_w_rg=lru_w_rg, lru_b_rg=lru_b_rg, lru_w_ig=lru_w_ig, lru_b_ig=lru_b_ig,
             lru_lambda=lru_lambda, lru_out_g=lru_out_g, gdn_conv_w=gdn_conv_w, gdn_a_log=gdn_a_log,
             gdn_dt_bias=gdn_dt_bias, gdn_norm_w=gdn_norm_w, w_out=w_out, ln1_g=ln1_g, ln1_b=ln1_b,
             w_router=w_router, router_bias=router_bias, w_gate=w_gate, w_up=w_up, w_down=w_down,
             ws_gate=ws_gate, ws_up=ws_up, ws_down=ws_down, ln2_g=ln2_g, ln2_b=ln2_b)
    bsz, seq, _ = x.shape
    return _layer(x, 0, p, _tiles(bsz, seq))
```

```python
import functools

import jax
import jax.numpy as jnp
from jax import lax
from jax.experimental import pallas as pl
from jax.experimental.pallas import tpu as pltpu
from jax.experimental.pallas import tpu_sc as plsc

F32 = jnp.float32
BF16 = jnp.bfloat16

D_MODEL = 1024
LRU_WIDTH = 512
LRU_BLOCKS = 8
LRU_C = 8.0
CONV_WIDTH = 4
GDN_HEADS = 4
GDN_DK = 128
GDN_DV = 128
GDN_CHUNK = 64
GDN_QK = GDN_HEADS * GDN_DK
GDN_V = GDN_HEADS * GDN_DV
N_MAIN = 2 * LRU_WIDTH + 2 * GDN_QK + 2 * GDN_V
N_EXPERTS = 256
TOP_K = 8
N_GROUPS = 8
GROUP_SIZE = N_EXPERTS // N_GROUPS
TOPK_GROUPS = 4
D_EXPERT = 256
D_SHARED = 256
ROUTED_SCALE = 2.5
MOE_BLOCK = 256
D_PACK = D_MODEL // 2
LN_EPS = 1e-5
NORM_EPS = 1e-6
DEPTH = 1
DEEPNORM_ALPHA = (2.0 * DEPTH) ** 0.25

HALO = 8
LANES = 128
VMEM_LIMIT = 56 * 1024 * 1024
SC_CORES = 2
SC_WORKERS = 32
SC_CHUNK = 64

NN = (((1,), (0,)), ((), ()))
NT = (((1,), (1,)), ((), ()))
TN = (((0,), (0,)), ((), ()))


def _dot(a, b, dims=NN):
    return lax.dot_general(a, b, dims, preferred_element_type=F32)


def _split(a):
    hi = a.astype(BF16)
    lo = (a - hi.astype(F32)).astype(BF16)
    return hi, lo


def _dot3(a, b, dims=NN):
    ah, al = _split(a)
    bh, bl = _split(b)
    return _dot(ah, bh, dims) + (_dot(ah, bl, dims) + _dot(al, bh, dims))


def _layer_norm(x, g, b):
    mu = jnp.mean(x, -1, keepdims=True)
    xc = x - mu
    var = jnp.mean(xc * xc, -1, keepdims=True)
    return xc * lax.rsqrt(var + LN_EPS) * g + b


def _sigmoid(x):
    return 1.0 / (1.0 + jnp.exp(-x))


def _silu(x):
    return x * _sigmoid(x)


def _softplus(x):
    return jnp.maximum(x, 0.0) + jnp.log1p(jnp.exp(-jnp.abs(x)))


def _gelu_tanh(x):
    c = 0.7978845608028654
    return x * (0.5 * (1.0 + jnp.tanh(c * (x + 0.044715 * (x * x * x)))))


def _pack_rows(x):
    hi = lax.bitcast_convert_type(x[:, :D_PACK].astype(BF16).astype(F32), jnp.uint32)
    lo = lax.bitcast_convert_type(x[:, D_PACK:].astype(BF16).astype(F32), jnp.uint32)
    return (hi & jnp.uint32(0xFFFF0000)) | (lo >> 16)


def _unpack_rows(w):
    hi = lax.bitcast_convert_type(w & jnp.uint32(0xFFFF0000), F32)
    lo = lax.bitcast_convert_type(w << 16, F32)
    return hi, lo


def _params(sem, **kw):
    return pltpu.CompilerParams(dimension_semantics=sem, vmem_limit_bytes=VMEM_LIMIT, **kw)


def _inproj_kernel(x_ref, g_ref, b_ref, w_ref, ws_ref, wst_ref, h_ref, proj_ref, small_ref, smallt_ref):
    h = _layer_norm(x_ref[...], g_ref[...], b_ref[...])
    h_ref[...] = h
    hb = h.astype(BF16)
    proj_ref[...] = _dot(hb, w_ref[...])
    small_ref[...] = _dot3(h, ws_ref[...])
    smallt_ref[...] = _dot3(wst_ref[...], h, NT)


def _inproj(x2d, g, b, w_main, w_small, w_small_t, tm):
    t = x2d.shape[0]
    return pl.pallas_call(
        _inproj_kernel,
        grid=(t // tm,),
        in_specs=[
            pl.BlockSpec((tm, D_MODEL), lambda i: (i, 0)),
            pl.BlockSpec((1, D_MODEL), lambda i: (0, 0)),
            pl.BlockSpec((1, D_MODEL), lambda i: (0, 0)),
            pl.BlockSpec((D_MODEL, N_MAIN), lambda i: (0, 0)),
            pl.BlockSpec((D_MODEL, LANES), lambda i: (0, 0)),
            pl.BlockSpec((8, D_MODEL), lambda i: (0, 0)),
        ],
        out_specs=[
            pl.BlockSpec((tm, D_MODEL), lambda i: (i, 0)),
            pl.BlockSpec((tm, N_MAIN), lambda i: (i, 0)),
            pl.BlockSpec((tm, LANES), lambda i: (i, 0)),
            pl.BlockSpec((8, tm), lambda i: (0, i)),
        ],
        out_shape=[
            jax.ShapeDtypeStruct((t, D_MODEL), F32),
            jax.ShapeDtypeStruct((t, N_MAIN), F32),
            jax.ShapeDtypeStruct((t, LANES), F32),
            jax.ShapeDtypeStruct((8, t), F32),
        ],
        compiler_params=_params(("arbitrary",)),
        name="ln_inproj",
    )(x2d, g, b, w_main, w_small, w_small_t)


def _causal_conv(buf_ref, x, w_ref, first, rows):
    @pl.when(first)
    def _():
        buf_ref[0:HALO, :] = jnp.zeros((HALO, buf_ref.shape[1]), F32)

    buf_ref[HALO:HALO + rows, :] = x
    acc = None
    for j in range(CONV_WIDTH):
        off = HALO - (CONV_WIDTH - 1) + j
        term = buf_ref[off:off + rows, :] * w_ref[j:j + 1, :]
        acc = term if acc is None else acc + term
    buf_ref[0:HALO, :] = buf_ref[rows:rows + HALO, :]
    return acc


def _shift_rows(x, d, fill):
    rows = x.shape[0]
    if d % 8 == 0:
        pad = jnp.full((d, x.shape[1]), fill, x.dtype)
        return jnp.concatenate([pad, x[:rows - d]], axis=0)
    rolled = pltpu.roll(x, d, 0)
    row = lax.broadcasted_iota(jnp.int32, x.shape, 0)
    return jnp.where(row < d, fill, rolled)


def _lru_kernel(u_ref, gate_ref, cw_ref, cb_ref, wg_ref, bg_ref, lam_ref, og_ref,
                y_ref, ubuf, hcarry):
    s = pl.program_id(1)
    rows = u_ref.shape[0]

    @pl.when(s == 0)
    def _():
        hcarry[...] = jnp.zeros_like(hcarry)

    xc = _causal_conv(ubuf, u_ref[...], cw_ref, s == 0, rows) + cb_ref[...]
    gates = _dot(xc.astype(BF16), wg_ref[...]) + bg_ref[...]
    r = _sigmoid(gates[:, :LRU_WIDTH])
    i = _sigmoid(gates[:, LRU_WIDTH:])
    log_a = (-LRU_C) * r * _softplus(-lam_ref[...])
    a = jnp.exp(log_a)
    mult = jnp.sqrt(-jnp.tanh(log_a) * (a * a + 1.0))
    bv = mult * (i * xc)
    d = 1
    while d < rows:
        a_sh = _shift_rows(a, d, 1.0)
        b_sh = _shift_rows(bv, d, 0.0)
        bv = a * b_sh + bv
        a = a * a_sh
        d *= 2
    h = a * hcarry[...] + bv
    hcarry[...] = h[rows - 1:rows, :]
    y = h * _gelu_tanh(gate_ref[...])
    ms = jnp.mean(y * y, -1, keepdims=True)
    y_ref[...] = y * lax.rsqrt(ms + NORM_EPS) * og_ref[...]


def _lru(proj3, conv_w, conv_b, w_gates, b_gates, lam, out_g, ts):
    bsz, seq, _ = proj3.shape
    row = lambda n: pl.BlockSpec((1, n), lambda b, s: (0, 0))
    return pl.pallas_call(
        _lru_kernel,
        grid=(bsz, seq // ts),
        in_specs=[
            pl.BlockSpec((None, ts, LRU_WIDTH), lambda b, s: (b, s, 0)),
            pl.BlockSpec((None, ts, LRU_WIDTH), lambda b, s: (b, s, 1)),
            pl.BlockSpec((CONV_WIDTH, LRU_WIDTH), lambda b, s: (0, 0)),
            row(LRU_WIDTH),
            pl.BlockSpec((LRU_WIDTH, 2 * LRU_WIDTH), lambda b, s: (0, 0)),
            row(2 * LRU_WIDTH),
            row(LRU_WIDTH),
            row(LRU_WIDTH),
        ],
        out_specs=pl.BlockSpec((None, ts, LRU_WIDTH), lambda b, s: (b, s, 0)),
        out_shape=jax.ShapeDtypeStruct((bsz, seq, LRU_WIDTH), F32),
        scratch_shapes=[
            pltpu.VMEM((HALO + ts, LRU_WIDTH), F32),
            pltpu.VMEM((1, LRU_WIDTH), F32),
        ],
        compiler_params=_params(("arbitrary", "arbitrary")),
        name="rg_lru",
    )(proj3, proj3, conv_w, conv_b, w_gates, b_gates, lam, out_g)


def _bdot(a, b, dims=NN):
    return _dot(a.astype(BF16), b.astype(BF16), dims)


def _gdn_heads(args, norm_w, causal, strict, upper):
    c = GDN_CHUNK
    each = lambda f, *ls: [f(*xs) for xs in zip(*ls)]
    q, k, v, z, beta, g_col, g_row, st = [list(x) for x in zip(*args)]
    q = each(lambda x: x * lax.rsqrt(jnp.sum(x * x, -1, keepdims=True) + NORM_EPS) * (GDN_DK ** -0.5), q)
    k = each(lambda x: x * lax.rsqrt(jnp.sum(x * x, -1, keepdims=True) + NORM_EPS), k)
    gc_col = each(lambda g: jnp.sum(jnp.where(causal, g, 0.0), axis=1, keepdims=True), g_row)
    gc_row = each(lambda g: jnp.sum(jnp.where(upper, g, 0.0), axis=0, keepdims=True), g_col)
    decay = each(lambda gc, gr: jnp.exp(jnp.where(causal, gc - gr, -jnp.inf)), gc_col, gc_row)
    kb = each(lambda x, bt: x * bt, k, beta)
    vb = each(lambda x, bt: x * bt, v, beta)
    kk = each(lambda x, y: _bdot(x, y, NT), kb, k)
    a_mat = each(lambda m, d: jnp.where(strict, m * d, 0.0), kk, decay)
    e_col = each(jnp.exp, gc_col)
    rhs = each(lambda x, y, e: jnp.concatenate([x, y * e], axis=1), vb, kb, e_col)
    sol = each(lambda r, a: r - _bdot(a, r), rhs, a_mat)
    p = a_mat
    for _ in range(5):
        p = each(lambda x: _bdot(x, x), p)
        sol = each(lambda x, y: y + _bdot(x, y), p, sol)
    qk = each(lambda x, y: _bdot(x, y, NT), q, k)
    qk = each(lambda m, d: jnp.where(causal, m * d, 0.0), qk, decay)
    q_dec = each(lambda x, e: x * e, q, e_col)
    g_last = each(lambda gc: gc[c - 1:c, :], gc_col)
    k_dec = each(lambda x, gl, gc: x * jnp.exp(gl - gc), k, g_last, gc_col)
    ws = each(lambda x, s: _bdot(x[:, GDN_DV:], s), sol, st)
    qs = each(lambda x, s: _bdot(x, s), q_dec, st)
    v_new = each(lambda x, w: x[:, :GDN_DV] - w, sol, ws)
    o = each(lambda a, m, vn: a + _bdot(m, vn), qs, qk, v_new)
    kv = each(lambda x, vn: _bdot(x, vn, TN), k_dec, v_new)
    st_new = each(lambda s, gl, d: s * jnp.exp(gl) + d, st, g_last, kv)
    o = each(lambda x: x * lax.rsqrt(jnp.mean(x * x, -1, keepdims=True) + NORM_EPS) * norm_w, o)
    o = each(lambda x, zz: x * _silu(zz), o, z)
    return list(zip(o, st_new))


def _gdn_kernel(q_ref, k_ref, v_ref, z_ref, sm_ref, smt_ref, cwq_ref, cwk_ref, cwv_ref,
                alr_ref, dtr_ref, alc_ref, dtc_ref, nw_ref, y_ref, qbuf, kbuf, vbuf, state):
    n = pl.program_id(1)
    c = GDN_CHUNK
    nb = q_ref.shape[0]
    first = n == 0

    @pl.when(first)
    def _():
        state[...] = jnp.zeros_like(state)

    ri = lax.broadcasted_iota(jnp.int32, (c, c), 0)
    ci = lax.broadcasted_iota(jnp.int32, (c, c), 1)
    causal = ri >= ci
    strict = ri > ci
    upper = ri <= ci
    norm_w = nw_ref[...]

    args = []
    for b in range(nb):
        q_all = _silu(_causal_conv(qbuf.at[b], q_ref[b], cwq_ref, first, c))
        k_all = _silu(_causal_conv(kbuf.at[b], k_ref[b], cwk_ref, first, c))
        v_all = _silu(_causal_conv(vbuf.at[b], v_ref[b], cwv_ref, first, c))
        z_all = z_ref[b]
        sm = sm_ref[b]
        beta_all = _sigmoid(sm)
        g_cols = -jnp.exp(alr_ref[...]) * _softplus(sm + dtr_ref[...])
        g_rows = -jnp.exp(alc_ref[...]) * _softplus(smt_ref[b] + dtc_ref[...])
        for hd in range(GDN_HEADS):
            sl = slice(hd * GDN_DK, (hd + 1) * GDN_DK)
            args.append((q_all[:, sl], k_all[:, sl], v_all[:, sl], z_all[:, sl],
                         beta_all[:, hd:hd + 1],
                         g_cols[:, GDN_HEADS + hd:GDN_HEADS + hd + 1],
                         g_rows[GDN_HEADS + hd:GDN_HEADS + hd + 1, :],
                         state[b, hd]))
    outs = _gdn_heads(args, norm_w, causal, strict, upper)
    for b in range(nb):
        for hd in range(GDN_HEADS):
            o, st_new = outs[b * GDN_HEADS + hd]
            state[b, hd] = st_new
            y_ref[b, :, hd * GDN_DK:(hd + 1) * GDN_DK] = o


def _gdn(proj3, small3, smallt3, cwq, cwk, cwv, alr, dtr, alc, dtc, norm_w, nb):
    bsz, seq, _ = proj3.shape
    c = GDN_CHUNK
    nch = seq // c
    col = lambda j: pl.BlockSpec((nb, c, GDN_QK), lambda b, n: (b, n, j))
    const = lambda shape: pl.BlockSpec(shape, lambda b, n: (0,) * len(shape))
    return pl.pallas_call(
        _gdn_kernel,
        grid=(bsz // nb, nch),
        in_specs=[
            col(2), col(3), col(4), col(5),
            pl.BlockSpec((nb, c, LANES), lambda b, n: (b, n, 0)),
            pl.BlockSpec((nb, None, 8, c), lambda b, n: (b, n, 0, 0)),
            const((CONV_WIDTH, GDN_QK)), const((CONV_WIDTH, GDN_QK)), const((CONV_WIDTH, GDN_V)),
            const((1, LANES)), const((1, LANES)), const((8, 1)), const((8, 1)),
            const((1, GDN_DV)),
        ],
        out_specs=pl.BlockSpec((nb, c, GDN_V), lambda b, n: (b, n, 0)),
        out_shape=jax.ShapeDtypeStruct((bsz, seq, GDN_V), F32),
        scratch_shapes=[
            pltpu.VMEM((nb, HALO + c, GDN_QK), F32),
            pltpu.VMEM((nb, HALO + c, GDN_QK), F32),
            pltpu.VMEM((nb, HALO + c, GDN_V), F32),
            pltpu.VMEM((nb, GDN_HEADS, GDN_DK, GDN_DV), F32),
        ],
        compiler_params=_params(("arbitrary", "arbitrary")),
        name="gated_deltanet",
    )(proj3, proj3, proj3, proj3, small3, smallt3, cwq, cwk, cwv, alr, dtr, alc, dtc, norm_w)


def _router_kernel(yl_ref, yg_ref, h0_ref, wo1_ref, wo2_ref, g_ref, b_ref, wrt_ref, rb_ref,
                   h1_ref, h1p_ref, e_ref, w_ref, rank_ref, cnt_ref, carry):
    i = pl.program_id(0)
    tm = h0_ref.shape[0]

    @pl.when(i == 0)
    def _():
        carry[...] = jnp.zeros_like(carry)

    mix = _dot(yl_ref[...].astype(BF16), wo1_ref[...]) + _dot(yg_ref[...].astype(BF16), wo2_ref[...])
    h1 = _layer_norm(DEEPNORM_ALPHA * h0_ref[...] + mix, g_ref[...], b_ref[...])
    h1_ref[...] = h1
    h1p_ref[...] = _pack_rows(h1)

    scores = _sigmoid(_dot3(wrt_ref[...], h1, NT))
    choice = scores + rb_ref[...]
    neg = -jnp.inf
    gs_rows = []
    sub = lax.broadcasted_iota(jnp.int32, (GROUP_SIZE, tm), 0).astype(F32)
    for g in range(N_GROUPS):
        cg = choice[g * GROUP_SIZE:(g + 1) * GROUP_SIZE, :]
        m1 = jnp.max(cg, axis=0, keepdims=True)
        i1 = jnp.min(jnp.where(cg == m1, sub, float(GROUP_SIZE)), axis=0, keepdims=True)
        m2 = jnp.max(jnp.where(sub == i1, neg, cg), axis=0, keepdims=True)
        gs_rows.append(m1 + m2)
    gs = jnp.concatenate(gs_rows, axis=0)
    gi = lax.broadcasted_iota(jnp.int32, (N_GROUPS, tm), 0).astype(F32)
    gsel = jnp.zeros((N_GROUPS, tm), jnp.bool_)
    for _ in range(TOPK_GROUPS):
        m = jnp.max(gs, axis=0, keepdims=True)
        idx = jnp.min(jnp.where(gs == m, gi, float(N_GROUPS)), axis=0, keepdims=True)
        hit = gi == idx
        gsel = jnp.logical_or(gsel, hit)
        gs = jnp.where(hit, neg, gs)
    masked = jnp.concatenate(
        [jnp.where(gsel[g:g + 1, :], choice[g * GROUP_SIZE:(g + 1) * GROUP_SIZE, :], neg)
         for g in range(N_GROUPS)], axis=0)

    ei = lax.broadcasted_iota(jnp.int32, (N_EXPERTS, tm), 0).astype(F32)
    hits = []
    e_rows, w_rows = [], []
    multi = jnp.zeros((N_EXPERTS, tm), F32)
    for _ in range(TOP_K):
        m = jnp.max(masked, axis=0, keepdims=True)
        idx = jnp.min(jnp.where(masked == m, ei, float(N_EXPERTS)), axis=0, keepdims=True)
        hit = ei == idx
        hits.append(hit)
        e_rows.append(idx)
        w_rows.append(jnp.sum(jnp.where(hit, scores, 0.0), axis=0, keepdims=True))
        multi = multi + hit.astype(F32)
        masked = jnp.where(hit, neg, masked)
    wts = jnp.concatenate(w_rows, axis=0)
    wts = wts / (jnp.sum(wts, axis=0, keepdims=True) + 1e-20) * ROUTED_SCALE
    ti = lax.broadcasted_iota(jnp.int32, (tm, tm), 0)
    tj = lax.broadcasted_iota(jnp.int32, (tm, tm), 1)
    before = (ti < tj).astype(BF16)
    cum = _dot(multi.astype(BF16), before) + carry[...]
    r_rows = [jnp.sum(jnp.where(hit, cum, 0.0), axis=0, keepdims=True) for hit in hits]
    carry[...] = carry[...] + jnp.sum(multi, axis=1, keepdims=True)
    e_ref[...] = jnp.concatenate(e_rows, axis=0).astype(jnp.int32)
    w_ref[...] = wts
    rank_ref[...] = jnp.concatenate(r_rows, axis=0).astype(jnp.int32)
    cnt_ref[...] = carry[...].astype(jnp.int32)


def _router(y_lru, y_gdn, h0, wo1, wo2, g, b, w_router_t, rbias, tm):
    t = h0.shape[0]
    const = lambda shape: pl.BlockSpec(shape, lambda i: (0,) * len(shape))
    return pl.pallas_call(
        _router_kernel,
        grid=(t // tm,),
        in_specs=[
            pl.BlockSpec((tm, LRU_WIDTH), lambda i: (i, 0)),
            pl.BlockSpec((tm, GDN_V), lambda i: (i, 0)),
            pl.BlockSpec((tm, D_MODEL), lambda i: (i, 0)),
            const((LRU_WIDTH, D_MODEL)), const((GDN_V, D_MODEL)),
            const((1, D_MODEL)), const((1, D_MODEL)),
            const((N_EXPERTS, D_MODEL)), const((N_EXPERTS, 1)),
        ],
        out_specs=[
            pl.BlockSpec((tm, D_MODEL), lambda i: (i, 0)),
            pl.BlockSpec((tm, D_PACK), lambda i: (i, 0)),
            pl.BlockSpec((TOP_K, tm), lambda i: (0, i)),
            pl.BlockSpec((TOP_K, tm), lambda i: (0, i)),
            pl.BlockSpec((TOP_K, tm), lambda i: (0, i)),
            const((N_EXPERTS, 1)),
        ],
        out_shape=[
            jax.ShapeDtypeStruct((t, D_MODEL), F32),
            jax.ShapeDtypeStruct((t, D_PACK), jnp.uint32),
            jax.ShapeDtypeStruct((TOP_K, t), jnp.int32),
            jax.ShapeDtypeStruct((TOP_K, t), F32),
            jax.ShapeDtypeStruct((TOP_K, t), jnp.int32),
            jax.ShapeDtypeStruct((N_EXPERTS, 1), jnp.int32),
        ],
        scratch_shapes=[pltpu.VMEM((N_EXPERTS, 1), F32)],
        compiler_params=_params(("arbitrary",)),
        name="outproj_router",
    )(y_lru, y_gdn, h0, wo1, wo2, g, b, w_router_t, rbias)


def _dest_kernel(e_ref, r_ref, ps_ref, d_ref):
    tm = e_ref.shape[1]
    ei = lax.broadcasted_iota(jnp.int32, (N_EXPERTS, tm), 0)
    rows = []
    for k in range(TOP_K):
        hit = ei == e_ref[k:k + 1, :]
        rows.append(jnp.sum(jnp.where(hit, ps_ref[...], 0), axis=0, keepdims=True))
    d_ref[...] = jnp.concatenate(rows, axis=0) + r_ref[...]


def _dest(top_e, rank, pad_start, tm):
    t = top_e.shape[1]
    blk = pl.BlockSpec((TOP_K, tm), lambda i: (0, i))
    return pl.pallas_call(
        _dest_kernel,
        grid=(t // tm,),
        in_specs=[blk, blk, pl.BlockSpec((N_EXPERTS, 1), lambda i: (0, 0))],
        out_specs=blk,
        out_shape=jax.ShapeDtypeStruct((TOP_K, t), jnp.int32),
        compiler_params=_params(("arbitrary",)),
        name="moe_dest",
    )(top_e, rank, pad_start)


def _sc_scatter_rows(rows, idx, n_out, chunk):
    n_copies, t = idx.shape
    d = rows.shape[1]
    per_worker = t // SC_WORKERS
    n_chunks = per_worker // chunk
    mesh = plsc.VectorSubcoreMesh(core_axis_name="c", subcore_axis_name="s")
    idx_flat = idx.reshape(n_copies * t)

    @functools.partial(
        pl.kernel, mesh=mesh,
        out_type=jax.ShapeDtypeStruct((n_out, d), rows.dtype),
        scratch_types=[pltpu.VMEM((chunk,), jnp.int32) for _ in range(n_copies)] + [
            pltpu.VMEM((chunk, d), rows.dtype),
            pltpu.SemaphoreType.DMA,
        ],
    )
    def scatter(rows_hbm, idx_hbm, out_hbm, *scratch):
        idx_v = scratch[:n_copies]
        rows_v, sem = scratch[n_copies:]
        wid = lax.axis_index("s") * SC_CORES + lax.axis_index("c")
        base = wid * per_worker

        @pl.loop(0, n_chunks)
        def _(j):
            off = base + j * chunk
            for k in range(n_copies):
                pltpu.sync_copy(idx_hbm.at[pl.ds(k * t + off, chunk)], idx_v[k])
            pltpu.sync_copy(rows_hbm.at[pl.ds(off, chunk)], rows_v)
            copies = [pltpu.async_copy(rows_v, out_hbm.at[idx_v[k]], sem) for k in range(n_copies)]
            for cp in copies:
                cp.wait()

    return scatter(rows, idx_flat)


def _expert_kernel(be_ref, nu_ref, nv_ref, xs_ref, wg_ref, wu_ref, wd_ref, ys_ref, wgu_b, wd_b):
    i = pl.program_id(0)

    @pl.when(i < nu_ref[0])
    def _():
        prev = be_ref[jnp.maximum(i - 1, 0)]

        @pl.when(jnp.logical_or(i == 0, be_ref[i] != prev))
        def _():
            wgu_b[:, :D_EXPERT] = wg_ref[...].astype(BF16)
            wgu_b[:, D_EXPERT:] = wu_ref[...].astype(BF16)
            wd_b[...] = wd_ref[...].astype(BF16)

        row = lax.broadcasted_iota(jnp.int32, xs_ref.shape, 0)
        x_hi, x_lo = _unpack_rows(jnp.where(row < nv_ref[i], xs_ref[...], jnp.uint32(0)))
        gu = _dot(x_hi.astype(BF16), wgu_b[:D_PACK, :]) + _dot(x_lo.astype(BF16), wgu_b[D_PACK:, :])
        h = _silu(gu[:, :D_EXPERT]) * gu[:, D_EXPERT:]
        ys_ref[...] = _pack_rows(_dot(h.astype(BF16), wd_b[...]))


def _experts(blk_e, n_used, n_valid, xs, w_gate, w_up, w_down):
    n_rows = xs.shape[0]
    n_blocks = n_rows // MOE_BLOCK
    blk = lambda i, be, nu, nv: (jnp.minimum(i, nu[0] - 1), 0)
    exp = lambda i, be, nu, nv: (be[i], 0, 0)
    return pl.pallas_call(
        _expert_kernel,
        grid_spec=pltpu.PrefetchScalarGridSpec(
            num_scalar_prefetch=3,
            grid=(n_blocks,),
            in_specs=[
                pl.BlockSpec((MOE_BLOCK, D_PACK), blk),
                pl.BlockSpec((None, D_MODEL, D_EXPERT), exp),
                pl.BlockSpec((None, D_MODEL, D_EXPERT), exp),
                pl.BlockSpec((None, D_EXPERT, D_MODEL), exp),
            ],
            out_specs=pl.BlockSpec((MOE_BLOCK, D_PACK), blk),
            scratch_shapes=[
                pltpu.VMEM((D_MODEL, 2 * D_EXPERT), BF16),
                pltpu.VMEM((D_EXPERT, D_MODEL), BF16),
            ],
        ),
        out_shape=jax.ShapeDtypeStruct((n_rows, D_PACK), jnp.uint32),
        compiler_params=_params(("arbitrary",)),
        name="moe_experts",
    )(blk_e, n_used, n_valid, xs, w_gate, w_up, w_down)


def _sc_gather_rows(table, idx, chunk):
    n_idx = idx.shape[0]
    d = table.shape[1]
    per_worker = n_idx // SC_WORKERS
    n_chunks = per_worker // chunk
    mesh = plsc.VectorSubcoreMesh(core_axis_name="c", subcore_axis_name="s")

    @functools.partial(
        pl.kernel, mesh=mesh,
        out_type=jax.ShapeDtypeStruct((n_idx, d), table.dtype),
        scratch_types=[
            pltpu.VMEM((chunk,), jnp.int32),
            pltpu.VMEM((chunk, d), table.dtype),
            pltpu.SemaphoreType.DMA,
        ],
    )
    def gather(table_hbm, idx_hbm, out_hbm, idx_v, rows_v, sem):
        wid = lax.axis_index("s") * SC_CORES + lax.axis_index("c")
        base = wid * per_worker

        @pl.loop(0, n_chunks)
        def _(j):
            off = base + j * chunk
            pltpu.sync_copy(idx_hbm.at[pl.ds(off, chunk)], idx_v)
            pltpu.async_copy(table_hbm.at[idx_v], rows_v, sem).wait()
            pltpu.sync_copy(rows_v, out_hbm.at[pl.ds(off, chunk)])

    return gather(table, idx)


def _combine_kernel(h1_ref, wts_ref, wsgu_ref, wsd_ref, g_ref, b_ref, yg_ref, out_ref):
    h1 = h1_ref[...]
    gu = _dot(h1.astype(BF16), wsgu_ref[...])
    hs = _silu(gu[:, :D_SHARED]) * gu[:, D_SHARED:]
    acc = DEEPNORM_ALPHA * h1 + _dot(hs.astype(BF16), wsd_ref[...])
    wts = wts_ref[...]
    acc_hi = acc[:, :D_PACK]
    acc_lo = acc[:, D_PACK:]
    for k in range(TOP_K):
        y_hi, y_lo = _unpack_rows(yg_ref[k])
        acc_hi = acc_hi + y_hi * wts[:, k:k + 1]
        acc_lo = acc_lo + y_lo * wts[:, k:k + 1]
    out_ref[...] = _layer_norm(jnp.concatenate([acc_hi, acc_lo], axis=1), g_ref[...], b_ref[...])


def _combine(h1, wts_t, ws_gu, ws_down, g, b, yg, tm):
    t = h1.shape[0]
    const = lambda shape: pl.BlockSpec(shape, lambda i: (0,) * len(shape))
    return pl.pallas_call(
        _combine_kernel,
        grid=(t // tm,),
        in_specs=[
            pl.BlockSpec((tm, D_MODEL), lambda i: (i, 0)),
            pl.BlockSpec((tm, TOP_K), lambda i: (i, 0)),
            const((D_MODEL, 2 * D_SHARED)), const((D_SHARED, D_MODEL)),
            const((1, D_MODEL)), const((1, D_MODEL)),
            pl.BlockSpec((TOP_K, tm, D_PACK), lambda i: (0, i, 0)),
        ],
        out_specs=pl.BlockSpec((tm, D_MODEL), lambda i: (i, 0)),
        out_shape=jax.ShapeDtypeStruct((t, D_MODEL), F32),
        compiler_params=_params(("arbitrary",)),
        name="moe_combine",
    )(h1, wts_t, ws_gu, ws_down, g, b, yg)


def _block_diag(w):
    nb, bi, bo = w.shape
    eye = jnp.eye(nb, dtype=w.dtype)
    return (eye[:, None, :, None] * w[:, :, None, :]).reshape(nb * bi, nb * bo)


def _pad_lanes(v, offset, width):
    return jnp.zeros((1, width), F32).at[0, offset:offset + v.shape[0]].set(v)


def _layer(h_in_x, l, p, tiles):
    bsz, seq, _ = h_in_x.shape
    t = bsz * seq
    row = lambda v: v.reshape(1, -1)

    w_in = p['w_in'][l]
    w_main = w_in[:, :N_MAIN].astype(BF16)
    w_small = jnp.zeros((D_MODEL, LANES), F32).at[:, :2 * GDN_HEADS].set(w_in[:, N_MAIN:])
    w_small_t = w_in[:, N_MAIN:].T
    h0, proj, small, small_t = _inproj(h_in_x.reshape(t, D_MODEL), row(p['ln_g']), row(p['ln_b']),
                                       w_main, w_small, w_small_t, tiles['inproj'])
    proj3 = proj.reshape(bsz, seq, N_MAIN)

    w_gates = jnp.concatenate([_block_diag(p['lru_w_rg'][l]), _block_diag(p['lru_w_ig'][l])], 1).astype(BF16)
    b_gates = jnp.concatenate([p['lru_b_rg'][l], p['lru_b_ig'][l]]).reshape(1, -1)
    y_lru = _lru(proj3, p['lru_conv_w'][l], row(p['lru_conv_b'][l]), w_gates, b_gates,
                 row(p['lru_lambda'][l]), row(p['lru_out_g'][l]), tiles['lru'])

    nch = seq // GDN_CHUNK
    small3 = small.reshape(bsz, seq, LANES)
    smallt3 = small_t.reshape(8, bsz, nch, GDN_CHUNK).transpose(1, 2, 0, 3)
    cw = p['gdn_conv_w'][l]
    a_log, dt_bias = p['gdn_a_log'][l], p['gdn_dt_bias'][l]
    alr = _pad_lanes(a_log, GDN_HEADS, LANES)
    dtr = _pad_lanes(dt_bias, GDN_HEADS, LANES)
    alc = _pad_lanes(a_log, GDN_HEADS, 8).reshape(8, 1)
    dtc = _pad_lanes(dt_bias, GDN_HEADS, 8).reshape(8, 1)
    y_gdn = _gdn(proj3, small3, smallt3, cw[:, :GDN_QK], cw[:, GDN_QK:2 * GDN_QK], cw[:, 2 * GDN_QK:],
                 alr, dtr, alc, dtc, row(p['gdn_norm_w'][l]), tiles['gdn_nb'])

    w_out = p['w_out'][l].astype(BF16)
    h1, h1p, top_e, wts, rank, counts = _router(
        y_lru.reshape(t, LRU_WIDTH), y_gdn.reshape(t, GDN_V), h0, w_out[:LRU_WIDTH], w_out[LRU_WIDTH:],
        row(p['ln1_g'][l]), row(p['ln1_b'][l]), p['w_router'][l].T, p['router_bias'][l].reshape(-1, 1),
        tiles['router'])

    counts = counts[:, 0]
    padded = (counts + MOE_BLOCK - 1) // MOE_BLOCK * MOE_BLOCK
    pad_end = jnp.cumsum(padded)
    pad_start = pad_end - padded
    n_blocks = (t * TOP_K) // MOE_BLOCK + N_EXPERTS
    n_rows = n_blocks * MOE_BLOCK
    n_used = (pad_end[-1] // MOE_BLOCK).astype(jnp.int32)
    blk_ids = jnp.minimum(jnp.arange(n_blocks, dtype=jnp.int32), n_used - 1)
    blk_e = jnp.minimum(jnp.sum(pad_end[None, :] <= (blk_ids * MOE_BLOCK)[:, None], axis=1),
                        N_EXPERTS - 1).astype(jnp.int32)

    dest = _dest(top_e, rank, pad_start.reshape(-1, 1), tiles['dest'])
    n_valid = jnp.clip(counts[blk_e] - (blk_ids * MOE_BLOCK - pad_start[blk_e]), 0, MOE_BLOCK).astype(jnp.int32)
    xs = _sc_scatter_rows(h1p, dest, n_rows, SC_CHUNK)
    ys = _experts(blk_e, n_used.reshape(1), n_valid, xs, p['w_gate'][l], p['w_up'][l], p['w_down'][l])
    ws_gu = jnp.concatenate([p['ws_gate'][l], p['ws_up'][l]], 1).astype(BF16)
    yg = _sc_gather_rows(ys, dest.reshape(TOP_K * t), SC_CHUNK).reshape(TOP_K, t, D_PACK)
    out = _combine(h1, wts.T, ws_gu, p['ws_down'][l].astype(BF16),
                   row(p['ln2_g'][l]), row(p['ln2_b'][l]), yg, tiles['combine'])
    return out.reshape(bsz, seq, D_MODEL)


def _tiles(bsz, seq):
    t = bsz * seq
    return {
        'inproj': min(256, t),
        'lru': min(256, seq),
        'gdn_nb': bsz,
        'router': min(256, t),
        'dest': min(512, t),
        'combine': min(256, t),
    }


def kernel(x, ln_in_g, ln_in_b, w_in, lru_conv_w, lru_conv_b, lru_w_rg, lru_b_rg, lru_w_ig, lru_b_ig,
           lru_lambda, lru_out_g, gdn_conv_w, gdn_a_log, gdn_dt_bias, gdn_norm_w, w_out, ln1_g, ln1_b,
           w_router, router_bias, w_gate, w_up, w_down, ws_gate, ws_up, ws_down, ln2_g, ln2_b):
    assert w_in.shape[0] == DEPTH == 1
    p = dict(ln_g=ln_in_g, ln_b=ln_in_b, w_in=w_in, lru_conv_w=lru_conv_w, lru_conv_b=lru_conv_b,
             lru_w_rg=lru_w_rg, lru_b_rg=lru_b_rg, lru_w_ig=lru_w_ig, lru_b_ig=lru_b_ig,
             lru_lambda=lru_lambda, lru_out_g=lru_out_g, gdn_conv_w=gdn_conv_w, gdn_a_log=gdn_a_log,
             gdn_dt_bias=gdn_dt_bias, gdn_norm_w=gdn_norm_w, w_out=w_out, ln1_g=ln1_g, ln1_b=ln1_b,
             w_router=w_router, router_bias=router_bias, w_gate=w_gate, w_up=w_up, w_down=w_down,
             ws_gate=ws_gate, ws_up=ws_up, ws_down=ws_down, ln2_g=ln2_g, ln2_b=ln2_b)
    bsz, seq, _ = x.shape
    return _layer(x, 0, p, _tiles(bsz, seq))
```

```python
import functools

import jax
import jax.numpy as jnp
from jax import lax
from jax.experimental import pallas as pl
from jax.experimental.pallas import tpu as pltpu
from jax.experimental.pallas import tpu_sc as plsc

F32 = jnp.float32
BF16 = jnp.bfloat16

D_MODEL = 1024
LRU_WIDTH = 512
LRU_BLOCKS = 8
LRU_C = 8.0
CONV_WIDTH = 4
GDN_HEADS = 4
GDN_DK = 128
GDN_DV = 128
GDN_CHUNK = 64
GDN_QK = GDN_HEADS * GDN_DK
GDN_V = GDN_HEADS * GDN_DV
N_MAIN = 2 * LRU_WIDTH + 2 * GDN_QK + 2 * GDN_V
N_EXPERTS = 256
TOP_K = 8
N_GROUPS = 8
GROUP_SIZE = N_EXPERTS // N_GROUPS
TOPK_GROUPS = 4
D_EXPERT = 256
D_SHARED = 256
ROUTED_SCALE = 2.5
MOE_BLOCK = 256
D_PACK = D_MODEL // 2
LN_EPS = 1e-5
NORM_EPS = 1e-6
DEPTH = 1
DEEPNORM_ALPHA = (2.0 * DEPTH) ** 0.25

HALO = 8
LANES = 128
VMEM_LIMIT = 56 * 1024 * 1024
SC_CORES = 2
SC_WORKERS = 32
SC_CHUNK = 64

NN = (((1,), (0,)), ((), ()))
NT = (((1,), (1,)), ((), ()))
TN = (((0,), (0,)), ((), ()))


def _dot(a, b, dims=NN):
    return lax.dot_general(a, b, dims, preferred_element_type=F32)


def _split(a):
    hi = a.astype(BF16)
    lo = (a - hi.astype(F32)).astype(BF16)
    return hi, lo


def _dot3(a, b, dims=NN):
    ah, al = _split(a)
    bh, bl = _split(b)
    return _dot(ah, bh, dims) + (_dot(ah, bl, dims) + _dot(al, bh, dims))


def _layer_norm(x, g, b):
    mu = jnp.mean(x, -1, keepdims=True)
    xc = x - mu
    var = jnp.mean(xc * xc, -1, keepdims=True)
    return xc * lax.rsqrt(var + LN_EPS) * g + b


def _sigmoid(x):
    return 1.0 / (1.0 + jnp.exp(-x))


def _silu(x):
    return x * _sigmoid(x)


def _softplus(x):
    return jnp.maximum(x, 0.0) + jnp.log1p(jnp.exp(-jnp.abs(x)))


def _gelu_tanh(x):
    c = 0.7978845608028654
    return x * (0.5 * (1.0 + jnp.tanh(c * (x + 0.044715 * (x * x * x)))))


def _pack_rows(x):
    hi = lax.bitcast_convert_type(x[:, :D_PACK].astype(BF16).astype(F32), jnp.uint32)
    lo = lax.bitcast_convert_type(x[:, D_PACK:].astype(BF16).astype(F32), jnp.uint32)
    return (hi & jnp.uint32(0xFFFF0000)) | (lo >> 16)


def _unpack_rows(w):
    hi = lax.bitcast_convert_type(w & jnp.uint32(0xFFFF0000), F32)
    lo = lax.bitcast_convert_type(w << 16, F32)
    return hi, lo


def _params(sem, **kw):
    return pltpu.CompilerParams(dimension_semantics=sem, vmem_limit_bytes=VMEM_LIMIT, **kw)


def _inproj_kernel(x_ref, g_ref, b_ref, w_ref, ws_ref, h_ref, proj_ref, small_ref, smallt_ref):
    h = _layer_norm(x_ref[...], g_ref[...], b_ref[...])
    h_ref[...] = h
    hb = h.astype(BF16)
    proj_ref[...] = _dot(hb, w_ref[...])
    small = _dot3(h, ws_ref[...])
    small_ref[...] = small
    smallt_ref[...] = small.T[:smallt_ref.shape[0], :]


def _inproj(x2d, g, b, w_main, w_small, tm):
    t = x2d.shape[0]
    return pl.pallas_call(
        _inproj_kernel,
        grid=(t // tm,),
        in_specs=[
            pl.BlockSpec((tm, D_MODEL), lambda i: (i, 0)),
            pl.BlockSpec((1, D_MODEL), lambda i: (0, 0)),
            pl.BlockSpec((1, D_MODEL), lambda i: (0, 0)),
            pl.BlockSpec((D_MODEL, N_MAIN), lambda i: (0, 0)),
            pl.BlockSpec((D_MODEL, LANES), lambda i: (0, 0)),
        ],
        out_specs=[
            pl.BlockSpec((tm, D_MODEL), lambda i: (i, 0)),
            pl.BlockSpec((tm, N_MAIN), lambda i: (i, 0)),
            pl.BlockSpec((tm, LANES), lambda i: (i, 0)),
            pl.BlockSpec((8, tm), lambda i: (0, i)),
        ],
        out_shape=[
            jax.ShapeDtypeStruct((t, D_MODEL), F32),
            jax.ShapeDtypeStruct((t, N_MAIN), F32),
            jax.ShapeDtypeStruct((t, LANES), F32),
            jax.ShapeDtypeStruct((8, t), F32),
        ],
        compiler_params=_params(("arbitrary",)),
        name="ln_inproj",
    )(x2d, g, b, w_main, w_small)


def _causal_conv(buf_ref, x, w_ref, first, rows):
    @pl.when(first)
    def _():
        buf_ref[0:HALO, :] = jnp.zeros((HALO, buf_ref.shape[1]), F32)

    buf_ref[HALO:HALO + rows, :] = x
    acc = None
    for j in range(CONV_WIDTH):
        off = HALO - (CONV_WIDTH - 1) + j
        term = buf_ref[off:off + rows, :] * w_ref[j:j + 1, :]
        acc = term if acc is None else acc + term
    buf_ref[0:HALO, :] = buf_ref[rows:rows + HALO, :]
    return acc


def _shift_rows(x, d, fill):
    rows = x.shape[0]
    if d % 8 == 0:
        pad = jnp.full((d, x.shape[1]), fill, x.dtype)
        return jnp.concatenate([pad, x[:rows - d]], axis=0)
    rolled = pltpu.roll(x, d, 0)
    row = lax.broadcasted_iota(jnp.int32, x.shape, 0)
    return jnp.where(row < d, fill, rolled)


def _lru_kernel(u_ref, gate_ref, cw_ref, cb_ref, wg_ref, bg_ref, lam_ref, og_ref,
                y_ref, ubuf, hcarry):
    s = pl.program_id(1)
    rows = u_ref.shape[0]

    @pl.when(s == 0)
    def _():
        hcarry[...] = jnp.zeros_like(hcarry)

    xc = _causal_conv(ubuf, u_ref[...], cw_ref, s == 0, rows) + cb_ref[...]
    gates = _dot(xc.astype(BF16), wg_ref[...]) + bg_ref[...]
    r = _sigmoid(gates[:, :LRU_WIDTH])
    i = _sigmoid(gates[:, LRU_WIDTH:])
    log_a = (-LRU_C) * r * _softplus(-lam_ref[...])
    a = jnp.exp(log_a)
    mult = jnp.sqrt(-jnp.tanh(log_a) * (a * a + 1.0))
    bv = mult * (i * xc)
    d = 1
    while d < rows:
        a_sh = _shift_rows(a, d, 1.0)
        b_sh = _shift_rows(bv, d, 0.0)
        bv = a * b_sh + bv
        a = a * a_sh
        d *= 2
    h = a * hcarry[...] + bv
    hcarry[...] = h[rows - 1:rows, :]
    y = h * _gelu_tanh(gate_ref[...])
    ms = jnp.mean(y * y, -1, keepdims=True)
    y_ref[...] = y * lax.rsqrt(ms + NORM_EPS) * og_ref[...]


def _lru(proj3, conv_w, conv_b, w_gates, b_gates, lam, out_g, ts):
    bsz, seq, _ = proj3.shape
    row = lambda n: pl.BlockSpec((1, n), lambda b, s: (0, 0))
    return pl.pallas_call(
        _lru_kernel,
        grid=(bsz, seq // ts),
        in_specs=[
            pl.BlockSpec((None, ts, LRU_WIDTH), lambda b, s: (b, s, 0)),
            pl.BlockSpec((None, ts, LRU_WIDTH), lambda b, s: (b, s, 1)),
            pl.BlockSpec((CONV_WIDTH, LRU_WIDTH), lambda b, s: (0, 0)),
            row(LRU_WIDTH),
            pl.BlockSpec((LRU_WIDTH, 2 * LRU_WIDTH), lambda b, s: (0, 0)),
            row(2 * LRU_WIDTH),
            row(LRU_WIDTH),
            row(LRU_WIDTH),
        ],
        out_specs=pl.BlockSpec((None, ts, LRU_WIDTH), lambda b, s: (b, s, 0)),
        out_shape=jax.ShapeDtypeStruct((bsz, seq, LRU_WIDTH), F32),
        scratch_shapes=[
            pltpu.VMEM((HALO + ts, LRU_WIDTH), F32),
            pltpu.VMEM((1, LRU_WIDTH), F32),
        ],
        compiler_params=_params(("arbitrary", "arbitrary")),
        name="rg_lru",
    )(proj3, proj3, conv_w, conv_b, w_gates, b_gates, lam, out_g)


def _bdot(a, b, dims=NN):
    return _dot(a.astype(BF16), b.astype(BF16), dims)


def _gdn_heads(args, norm_w, causal, strict, upper):
    c = GDN_CHUNK
    each = lambda f, *ls: [f(*xs) for xs in zip(*ls)]
    q, k, v, z, beta, g_col, g_row, st = [list(x) for x in zip(*args)]
    q = each(lambda x: x * lax.rsqrt(jnp.sum(x * x, -1, keepdims=True) + NORM_EPS) * (GDN_DK ** -0.5), q)
    k = each(lambda x: x * lax.rsqrt(jnp.sum(x * x, -1, keepdims=True) + NORM_EPS), k)
    gc_col = each(lambda g: jnp.sum(jnp.where(causal, g, 0.0), axis=1, keepdims=True), g_row)
    gc_row = each(lambda g: jnp.sum(jnp.where(upper, g, 0.0), axis=0, keepdims=True), g_col)
    decay = each(lambda gc, gr: jnp.exp(jnp.where(causal, gc - gr, -jnp.inf)), gc_col, gc_row)
    kb = each(lambda x, bt: x * bt, k, beta)
    vb = each(lambda x, bt: x * bt, v, beta)
    kk = each(lambda x, y: _bdot(x, y, NT), kb, k)
    a_mat = each(lambda m, d: jnp.where(strict, m * d, 0.0), kk, decay)
    e_col = each(jnp.exp, gc_col)
    rhs = each(lambda x, y, e: jnp.concatenate([x, y * e], axis=1), vb, kb, e_col)
    sol = each(lambda r, a: r - _bdot(a, r), rhs, a_mat)
    p = a_mat
    for _ in range(5):
        p = each(lambda x: _bdot(x, x), p)
        sol = each(lambda x, y: y + _bdot(x, y), p, sol)
    qk = each(lambda x, y: _bdot(x, y, NT), q, k)
    qk = each(lambda m, d: jnp.where(causal, m * d, 0.0), qk, decay)
    q_dec = each(lambda x, e: x * e, q, e_col)
    g_last = each(lambda gc: gc[c - 1:c, :], gc_col)
    k_dec = each(lambda x, gl, gc: x * jnp.exp(gl - gc), k, g_last, gc_col)
    ws = each(lambda x, s: _bdot(x[:, GDN_DV:], s), sol, st)
    qs = each(lambda x, s: _bdot(x, s), q_dec, st)
    v_new = each(lambda x, w: x[:, :GDN_DV] - w, sol, ws)
    o = each(lambda a, m, vn: a + _bdot(m, vn), qs, qk, v_new)
    kv = each(lambda x, vn: _bdot(x, vn, TN), k_dec, v_new)
    st_new = each(lambda s, gl, d: s * jnp.exp(gl) + d, st, g_last, kv)
    o = each(lambda x: x * lax.rsqrt(jnp.mean(x * x, -1, keepdims=True) + NORM_EPS) * norm_w, o)
    o = each(lambda x, zz: x * _silu(zz), o, z)
    return list(zip(o, st_new))


def _gdn_kernel(q_ref, k_ref, v_ref, z_ref, sm_ref, smt_ref, cwq_ref, cwk_ref, cwv_ref,
                alr_ref, dtr_ref, alc_ref, dtc_ref, nw_ref, y_ref, qbuf, kbuf, vbuf, state):
    n = pl.program_id(1)
    c = GDN_CHUNK
    nb = q_ref.shape[0]
    first = n == 0

    @pl.when(first)
    def _():
        state[...] = jnp.zeros_like(state)

    ri = lax.broadcasted_iota(jnp.int32, (c, c), 0)
    ci = lax.broadcasted_iota(jnp.int32, (c, c), 1)
    causal = ri >= ci
    strict = ri > ci
    upper = ri <= ci
    norm_w = nw_ref[...]

    args = []
    for b in range(nb):
        q_all = _silu(_causal_conv(qbuf.at[b], q_ref[b], cwq_ref, first, c))
        k_all = _silu(_causal_conv(kbuf.at[b], k_ref[b], cwk_ref, first, c))
        v_all = _silu(_causal_conv(vbuf.at[b], v_ref[b], cwv_ref, first, c))
        z_all = z_ref[b]
        sm = sm_ref[b]
        beta_all = _sigmoid(sm)
        g_cols = -jnp.exp(alr_ref[...]) * _softplus(sm + dtr_ref[...])
        g_rows = -jnp.exp(alc_ref[...]) * _softplus(smt_ref[b] + dtc_ref[...])
        for hd in range(GDN_HEADS):
            sl = slice(hd * GDN_DK, (hd + 1) * GDN_DK)
            args.append((q_all[:, sl], k_all[:, sl], v_all[:, sl], z_all[:, sl],
                         beta_all[:, hd:hd + 1],
                         g_cols[:, GDN_HEADS + hd:GDN_HEADS + hd + 1],
                         g_rows[GDN_HEADS + hd:GDN_HEADS + hd + 1, :],
                         state[b, hd]))
    outs = _gdn_heads(args, norm_w, causal, strict, upper)
    for b in range(nb):
        for hd in range(GDN_HEADS):
            o, st_new = outs[b * GDN_HEADS + hd]
            state[b, hd] = st_new
            y_ref[b, :, hd * GDN_DK:(hd + 1) * GDN_DK] = o


def _gdn(proj3, small3, smallt3, cwq, cwk, cwv, alr, dtr, alc, dtc, norm_w, nb):
    bsz, seq, _ = proj3.shape
    c = GDN_CHUNK
    nch = seq // c
    col = lambda j: pl.BlockSpec((nb, c, GDN_QK), lambda b, n: (b, n, j))
    const = lambda shape: pl.BlockSpec(shape, lambda b, n: (0,) * len(shape))
    return pl.pallas_call(
        _gdn_kernel,
        grid=(bsz // nb, nch),
        in_specs=[
            col(2), col(3), col(4), col(5),
            pl.BlockSpec((nb, c, LANES), lambda b, n: (b, n, 0)),
            pl.BlockSpec((nb, None, 8, c), lambda b, n: (b, n, 0, 0)),
            const((CONV_WIDTH, GDN_QK)), const((CONV_WIDTH, GDN_QK)), const((CONV_WIDTH, GDN_V)),
            const((1, LANES)), const((1, LANES)), const((8, 1)), const((8, 1)),
            const((1, GDN_DV)),
        ],
        out_specs=pl.BlockSpec((nb, c, GDN_V), lambda b, n: (b, n, 0)),
        out_shape=jax.ShapeDtypeStruct((bsz, seq, GDN_V), F32),
        scratch_shapes=[
            pltpu.VMEM((nb, HALO + c, GDN_QK), F32),
            pltpu.VMEM((nb, HALO + c, GDN_QK), F32),
            pltpu.VMEM((nb, HALO + c, GDN_V), F32),
            pltpu.VMEM((nb, GDN_HEADS, GDN_DK, GDN_DV), F32),
        ],
        compiler_params=_params(("arbitrary", "arbitrary")),
        name="gated_deltanet",
    )(proj3, proj3, proj3, proj3, small3, smallt3, cwq, cwk, cwv, alr, dtr, alc, dtc, norm_w)


def _router_kernel(yl_ref, yg_ref, h0_ref, wo1_ref, wo2_ref, g_ref, b_ref, wrt_ref, rb_ref,
                   h1_ref, h1p_ref, e_ref, w_ref, rank_ref, cnt_ref, carry):
    i = pl.program_id(0)
    tm = h0_ref.shape[0]

    @pl.when(i == 0)
    def _():
        carry[...] = jnp.zeros_like(carry)

    mix = _dot(yl_ref[...].astype(BF16), wo1_ref[...]) + _dot(yg_ref[...].astype(BF16), wo2_ref[...])
    h1 = _layer_norm(DEEPNORM_ALPHA * h0_ref[...] + mix, g_ref[...], b_ref[...])
    h1_ref[...] = h1
    h1p_ref[...] = _pack_rows(h1)

    scores = _sigmoid(_dot3(wrt_ref[...], h1, NT))
    choice = scores + rb_ref[...]
    neg = -jnp.inf
    gs_rows = []
    sub = lax.broadcasted_iota(jnp.int32, (GROUP_SIZE, tm), 0).astype(F32)
    for g in range(N_GROUPS):
        cg = choice[g * GROUP_SIZE:(g + 1) * GROUP_SIZE, :]
        m1 = jnp.max(cg, axis=0, keepdims=True)
        i1 = jnp.min(jnp.where(cg == m1, sub, float(GROUP_SIZE)), axis=0, keepdims=True)
        m2 = jnp.max(jnp.where(sub == i1, neg, cg), axis=0, keepdims=True)
        gs_rows.append(m1 + m2)
    gs = jnp.concatenate(gs_rows, axis=0)
    gi = lax.broadcasted_iota(jnp.int32, (N_GROUPS, tm), 0).astype(F32)
    gsel = jnp.zeros((N_GROUPS, tm), jnp.bool_)
    for _ in range(TOPK_GROUPS):
        m = jnp.max(gs, axis=0, keepdims=True)
        idx = jnp.min(jnp.where(gs == m, gi, float(N_GROUPS)), axis=0, keepdims=True)
        hit = gi == idx
        gsel = jnp.logical_or(gsel, hit)
        gs = jnp.where(hit, neg, gs)
    masked = jnp.concatenate(
        [jnp.where(gsel[g:g + 1, :], choice[g * GROUP_SIZE:(g + 1) * GROUP_SIZE, :], neg)
         for g in range(N_GROUPS)], axis=0)

    ei = lax.broadcasted_iota(jnp.int32, (N_EXPERTS, tm), 0).astype(F32)
    hits = []
    e_rows, w_rows = [], []
    multi = jnp.zeros((N_EXPERTS, tm), F32)
    for _ in range(TOP_K):
        m = jnp.max(masked, axis=0, keepdims=True)
        idx = jnp.min(jnp.where(masked == m, ei, float(N_EXPERTS)), axis=0, keepdims=True)
        hit = ei == idx
        hits.append(hit)
        e_rows.append(idx)
        w_rows.append(jnp.sum(jnp.where(hit, scores, 0.0), axis=0, keepdims=True))
        multi = multi + hit.astype(F32)
        masked = jnp.where(hit, neg, masked)
    wts = jnp.concatenate(w_rows, axis=0)
    wts = wts / (jnp.sum(wts, axis=0, keepdims=True) + 1e-20) * ROUTED_SCALE
    ti = lax.broadcasted_iota(jnp.int32, (tm, tm), 0)
    tj = lax.broadcasted_iota(jnp.int32, (tm, tm), 1)
    before = (ti < tj).astype(BF16)
    cum = _dot(multi.astype(BF16), before) + carry[...]
    r_rows = [jnp.sum(jnp.where(hit, cum, 0.0), axis=0, keepdims=True) for hit in hits]
    carry[...] = carry[...] + jnp.sum(multi, axis=1, keepdims=True)
    e_ref[...] = jnp.concatenate(e_rows, axis=0).astype(jnp.int32)
    w_ref[...] = wts
    rank_ref[...] = jnp.concatenate(r_rows, axis=0).astype(jnp.int32)
    cnt_ref[...] = carry[...].astype(jnp.int32)


def _router(y_lru, y_gdn, h0, wo1, wo2, g, b, w_router_t, rbias, tm):
    t = h0.shape[0]
    const = lambda shape: pl.BlockSpec(shape, lambda i: (0,) * len(shape))
    return pl.pallas_call(
        _router_kernel,
        grid=(t // tm,),
        in_specs=[
            pl.BlockSpec((tm, LRU_WIDTH), lambda i: (i, 0)),
            pl.BlockSpec((tm, GDN_V), lambda i: (i, 0)),
            pl.BlockSpec((tm, D_MODEL), lambda i: (i, 0)),
            const((LRU_WIDTH, D_MODEL)), const((GDN_V, D_MODEL)),
            const((1, D_MODEL)), const((1, D_MODEL)),
            const((N_EXPERTS, D_MODEL)), const((N_EXPERTS, 1)),
        ],
        out_specs=[
            pl.BlockSpec((tm, D_MODEL), lambda i: (i, 0)),
            pl.BlockSpec((tm, D_PACK), lambda i: (i, 0)),
            pl.BlockSpec((TOP_K, tm), lambda i: (0, i)),
            pl.BlockSpec((TOP_K, tm), lambda i: (0, i)),
            pl.BlockSpec((TOP_K, tm), lambda i: (0, i)),
            const((N_EXPERTS, 1)),
        ],
        out_shape=[
            jax.ShapeDtypeStruct((t, D_MODEL), F32),
            jax.ShapeDtypeStruct((t, D_PACK), jnp.uint32),
            jax.ShapeDtypeStruct((TOP_K, t), jnp.int32),
            jax.ShapeDtypeStruct((TOP_K, t), F32),
            jax.ShapeDtypeStruct((TOP_K, t), jnp.int32),
            jax.ShapeDtypeStruct((N_EXPERTS, 1), jnp.int32),
        ],
        scratch_shapes=[pltpu.VMEM((N_EXPERTS, 1), F32)],
        compiler_params=_params(("arbitrary",)),
        name="outproj_router",
    )(y_lru, y_gdn, h0, wo1, wo2, g, b, w_router_t, rbias)


def _dest_kernel(e_ref, r_ref, ps_ref, d_ref):
    tm = e_ref.shape[1]
    ei = lax.broadcasted_iota(jnp.int32, (N_EXPERTS, tm), 0)
    rows = []
    for k in range(TOP_K):
        hit = ei == e_ref[k:k + 1, :]
        rows.append(jnp.sum(jnp.where(hit, ps_ref[...], 0), axis=0, keepdims=True))
    d_ref[...] = jnp.concatenate(rows, axis=0) + r_ref[...]


def _dest(top_e, rank, pad_start, tm):
    t = top_e.shape[1]
    blk = pl.BlockSpec((TOP_K, tm), lambda i: (0, i))
    return pl.pallas_call(
        _dest_kernel,
        grid=(t // tm,),
        in_specs=[blk, blk, pl.BlockSpec((N_EXPERTS, 1), lambda i: (0, 0))],
        out_specs=blk,
        out_shape=jax.ShapeDtypeStruct((TOP_K, t), jnp.int32),
        compiler_params=_params(("arbitrary",)),
        name="moe_dest",
    )(top_e, rank, pad_start)


def _sc_scatter_rows(rows, idx, n_out, chunk):
    n_copies, t = idx.shape
    d = rows.shape[1]
    per_worker = t // SC_WORKERS
    n_chunks = per_worker // chunk
    mesh = plsc.VectorSubcoreMesh(core_axis_name="c", subcore_axis_name="s")
    idx_flat = idx.reshape(n_copies * t)

    @functools.partial(
        pl.kernel, mesh=mesh,
        out_type=jax.ShapeDtypeStruct((n_out, d), rows.dtype),
        scratch_types=[pltpu.VMEM((chunk,), jnp.int32) for _ in range(n_copies)] + [
            pltpu.VMEM((chunk, d), rows.dtype),
            pltpu.SemaphoreType.DMA,
        ],
    )
    def scatter(rows_hbm, idx_hbm, out_hbm, *scratch):
        idx_v = scratch[:n_copies]
        rows_v, sem = scratch[n_copies:]
        wid = lax.axis_index("s") * SC_CORES + lax.axis_index("c")
        base = wid * per_worker

        @pl.loop(0, n_chunks)
        def _(j):
            off = base + j * chunk
            for k in range(n_copies):
                pltpu.sync_copy(idx_hbm.at[pl.ds(k * t + off, chunk)], idx_v[k])
            pltpu.sync_copy(rows_hbm.at[pl.ds(off, chunk)], rows_v)
            copies = [pltpu.async_copy(rows_v, out_hbm.at[idx_v[k]], sem) for k in range(n_copies)]
            for cp in copies:
                cp.wait()

    return scatter(rows, idx_flat)


def _expert_kernel(be_ref, nu_ref, nv_ref, xs_ref, wg_ref, wu_ref, wd_ref, ys_ref, wgu_b, wd_b):
    i = pl.program_id(0)

    @pl.when(i < nu_ref[0])
    def _():
        prev = be_ref[jnp.maximum(i - 1, 0)]

        @pl.when(jnp.logical_or(i == 0, be_ref[i] != prev))
        def _():
            wgu_b[:, :D_EXPERT] = wg_ref[...].astype(BF16)
            wgu_b[:, D_EXPERT:] = wu_ref[...].astype(BF16)
            wd_b[...] = wd_ref[...].astype(BF16)

        row = lax.broadcasted_iota(jnp.int32, xs_ref.shape, 0)
        x_hi, x_lo = _unpack_rows(jnp.where(row < nv_ref[i], xs_ref[...], jnp.uint32(0)))
        gu = _dot(x_hi.astype(BF16), wgu_b[:D_PACK, :]) + _dot(x_lo.astype(BF16), wgu_b[D_PACK:, :])
        h = _silu(gu[:, :D_EXPERT]) * gu[:, D_EXPERT:]
        ys_ref[...] = _pack_rows(_dot(h.astype(BF16), wd_b[...]))


def _experts(blk_e, n_used, n_valid, xs, w_gate, w_up, w_down):
    n_rows = xs.shape[0]
    n_blocks = n_rows // MOE_BLOCK
    blk = lambda i, be, nu, nv: (jnp.minimum(i, nu[0] - 1), 0)
    exp = lambda i, be, nu, nv: (be[i], 0, 0)
    return pl.pallas_call(
        _expert_kernel,
        grid_spec=pltpu.PrefetchScalarGridSpec(
            num_scalar_prefetch=3,
            grid=(n_blocks,),
            in_specs=[
                pl.BlockSpec((MOE_BLOCK, D_PACK), blk),
                pl.BlockSpec((None, D_MODEL, D_EXPERT), exp),
                pl.BlockSpec((None, D_MODEL, D_EXPERT), exp),
                pl.BlockSpec((None, D_EXPERT, D_MODEL), exp),
            ],
            out_specs=pl.BlockSpec((MOE_BLOCK, D_PACK), blk),
            scratch_shapes=[
                pltpu.VMEM((D_MODEL, 2 * D_EXPERT), BF16),
                pltpu.VMEM((D_EXPERT, D_MODEL), BF16),
            ],
        ),
        out_shape=jax.ShapeDtypeStruct((n_rows, D_PACK), jnp.uint32),
        compiler_params=_params(("arbitrary",)),
        name="moe_experts",
    )(blk_e, n_used, n_valid, xs, w_gate, w_up, w_down)


def _sc_gather_rows(table, idx, chunk):
    n_idx = idx.shape[0]
    d = table.shape[1]
    per_worker = n_idx // SC_WORKERS
    n_chunks = per_worker // chunk
    assert n_chunks % 2 == 0 and n_chunks * chunk * SC_WORKERS == n_idx
    mesh = plsc.VectorSubcoreMesh(core_axis_name="c", subcore_axis_name="s")

    @functools.partial(
        pl.kernel, mesh=mesh,
        out_type=jax.ShapeDtypeStruct((n_idx, d), table.dtype),
        scratch_types=[
            pltpu.VMEM((chunk,), jnp.int32), pltpu.VMEM((chunk,), jnp.int32),
            pltpu.VMEM((chunk, d), table.dtype), pltpu.VMEM((chunk, d), table.dtype),
            pltpu.SemaphoreType.DMA, pltpu.SemaphoreType.DMA, pltpu.SemaphoreType.DMA, pltpu.SemaphoreType.DMA,
        ],
    )
    def gather(table_hbm, idx_hbm, out_hbm, idx_v0, idx_v1, rows_v0, rows_v1, gsem0, gsem1, osem0, osem1):
        idx_v, rows_v, gsem, osem = (idx_v0, idx_v1), (rows_v0, rows_v1), (gsem0, gsem1), (osem0, osem1)
        wid = lax.axis_index("s") * SC_CORES + lax.axis_index("c")
        base = wid * per_worker

        def gather_copy(slot):
            return pltpu.make_async_copy(table_hbm.at[idx_v[slot]], rows_v[slot], gsem[slot])

        def out_copy(c, slot):
            return pltpu.make_async_copy(rows_v[slot], out_hbm.at[pl.ds(base + c * chunk, chunk)], osem[slot])

        def start_gather(c, slot):
            pltpu.sync_copy(idx_hbm.at[pl.ds(base + c * chunk, chunk)], idx_v[slot])
            gather_copy(slot).start()

        start_gather(0, 0)

        @pl.loop(0, n_chunks, step=2)
        def _(j):
            for b in range(2):
                c = j + b
                cur, other = b, 1 - b

                @pl.when(c >= 1)
                def _():
                    out_copy(c - 1, other).wait()

                @pl.when(c + 1 < n_chunks)
                def _():
                    start_gather(c + 1, other)

                gather_copy(cur).wait()
                out_copy(c, cur).start()

        out_copy(n_chunks - 1, 1).wait()

    return gather(table, idx)


def _combine_kernel(h1_ref, wts_ref, wsgu_ref, wsd_ref, g_ref, b_ref, yg_ref, out_ref):
    h1 = h1_ref[...]
    gu = _dot(h1.astype(BF16), wsgu_ref[...])
    hs = _silu(gu[:, :D_SHARED]) * gu[:, D_SHARED:]
    acc = DEEPNORM_ALPHA * h1 + _dot(hs.astype(BF16), wsd_ref[...])
    wts = wts_ref[...]
    acc_hi = acc[:, :D_PACK]
    acc_lo = acc[:, D_PACK:]
    for k in range(TOP_K):
        y_hi, y_lo = _unpack_rows(yg_ref[k])
        acc_hi = acc_hi + y_hi * wts[:, k:k + 1]
        acc_lo = acc_lo + y_lo * wts[:, k:k + 1]
    out_ref[...] = _layer_norm(jnp.concatenate([acc_hi, acc_lo], axis=1), g_ref[...], b_ref[...])


def _combine(h1, wts_t, ws_gu, ws_down, g, b, yg, tm):
    t = h1.shape[0]
    const = lambda shape: pl.BlockSpec(shape, lambda i: (0,) * len(shape))
    return pl.pallas_call(
        _combine_kernel,
        grid=(t // tm,),
        in_specs=[
            pl.BlockSpec((tm, D_MODEL), lambda i: (i, 0)),
            pl.BlockSpec((tm, TOP_K), lambda i: (i, 0)),
            const((D_MODEL, 2 * D_SHARED)), const((D_SHARED, D_MODEL)),
            const((1, D_MODEL)), const((1, D_MODEL)),
            pl.BlockSpec((TOP_K, tm, D_PACK), lambda i: (0, i, 0)),
        ],
        out_specs=pl.BlockSpec((tm, D_MODEL), lambda i: (i, 0)),
        out_shape=jax.ShapeDtypeStruct((t, D_MODEL), F32),
        compiler_params=_params(("arbitrary",)),
        name="moe_combine",
    )(h1, wts_t, ws_gu, ws_down, g, b, yg)


def _block_diag(w):
    nb, bi, bo = w.shape
    eye = jnp.eye(nb, dtype=w.dtype)
    return (eye[:, None, :, None] * w[:, :, None, :]).reshape(nb * bi, nb * bo)


def _pad_lanes(v, offset, width):
    return jnp.zeros((1, width), F32).at[0, offset:offset + v.shape[0]].set(v)


def _layer(h_in_x, l, p, tiles):
    bsz, seq, _ = h_in_x.shape
    t = bsz * seq
    row = lambda v: v.reshape(1, -1)

    w_in = p['w_in'][l]
    w_main = w_in[:, :N_MAIN].astype(BF16)
    w_small = jnp.zeros((D_MODEL, LANES), F32).at[:, :2 * GDN_HEADS].set(w_in[:, N_MAIN:])
    h0, proj, small, small_t = _inproj(h_in_x.reshape(t, D_MODEL), row(p['ln_g']), row(p['ln_b']),
                                       w_main, w_small, tiles['inproj'])
    proj3 = proj.reshape(bsz, seq, N_MAIN)

    w_gates = jnp.concatenate([_block_diag(p['lru_w_rg'][l]), _block_diag(p['lru_w_ig'][l])], 1).astype(BF16)
    b_gates = jnp.concatenate([p['lru_b_rg'][l], p['lru_b_ig'][l]]).reshape(1, -1)
    y_lru = _lru(proj3, p['lru_conv_w'][l], row(p['lru_conv_b'][l]), w_gates, b_gates,
                 row(p['lru_lambda'][l]), row(p['lru_out_g'][l]), tiles['lru'])

    nch = seq // GDN_CHUNK
    small3 = small.reshape(bsz, seq, LANES)
    smallt3 = small_t.reshape(8, bsz, nch, GDN_CHUNK).transpose(1, 2, 0, 3)
    cw = p['gdn_conv_w'][l]
    a_log, dt_bias = p['gdn_a_log'][l], p['gdn_dt_bias'][l]
    alr = _pad_lanes(a_log, GDN_HEADS, LANES)
    dtr = _pad_lanes(dt_bias, GDN_HEADS, LANES)
    alc = _pad_lanes(a_log, GDN_HEADS, 8).reshape(8, 1)
    dtc = _pad_lanes(dt_bias, GDN_HEADS, 8).reshape(8, 1)
    y_gdn = _gdn(proj3, small3, smallt3, cw[:, :GDN_QK], cw[:, GDN_QK:2 * GDN_QK], cw[:, 2 * GDN_QK:],
                 alr, dtr, alc, dtc, row(p['gdn_norm_w'][l]), tiles['gdn_nb'])

    w_out = p['w_out'][l].astype(BF16)
    h1, h1p, top_e, wts, rank, counts = _router(
        y_lru.reshape(t, LRU_WIDTH), y_gdn.reshape(t, GDN_V), h0, w_out[:LRU_WIDTH], w_out[LRU_WIDTH:],
        row(p['ln1_g'][l]), row(p['ln1_b'][l]), p['w_router'][l].T, p['router_bias'][l].reshape(-1, 1),
        tiles['router'])

    counts = counts[:, 0]
    padded = (counts + MOE_BLOCK - 1) // MOE_BLOCK * MOE_BLOCK
    pad_end = jnp.cumsum(padded)
    pad_start = pad_end - padded
    n_blocks = (t * TOP_K) // MOE_BLOCK + N_EXPERTS
    n_rows = n_blocks * MOE_BLOCK
    n_used = (pad_end[-1] // MOE_BLOCK).astype(jnp.int32)
    blk_ids = jnp.minimum(jnp.arange(n_blocks, dtype=jnp.int32), n_used - 1)
    blk_e = jnp.minimum(jnp.sum(pad_end[None, :] <= (blk_ids * MOE_BLOCK)[:, None], axis=1),
                        N_EXPERTS - 1).astype(jnp.int32)

    dest = _dest(top_e, rank, pad_start.reshape(-1, 1), tiles['dest'])
    n_valid = jnp.clip(counts[blk_e] - (blk_ids * MOE_BLOCK - pad_start[blk_e]), 0, MOE_BLOCK).astype(jnp.int32)
    xs = _sc_scatter_rows(h1p, dest, n_rows, SC_CHUNK)
    ys = _experts(blk_e, n_used.reshape(1), n_valid, xs, p['w_gate'][l], p['w_up'][l], p['w_down'][l])
    ws_gu = jnp.concatenate([p['ws_gate'][l], p['ws_up'][l]], 1).astype(BF16)
    yg = _sc_gather_rows(ys, dest.reshape(TOP_K * t), SC_CHUNK).reshape(TOP_K, t, D_PACK)
    out = _combine(h1, wts.T, ws_gu, p['ws_down'][l].astype(BF16),
                   row(p['ln2_g'][l]), row(p['ln2_b'][l]), yg, tiles['combine'])
    return out.reshape(bsz, seq, D_MODEL)


def _tiles(bsz, seq):
    t = bsz * seq
    return {
        'inproj': min(512, t),
        'lru': min(256, seq),
        'gdn_nb': bsz,
        'router': min(256, t),
        'dest': min(512, t),
        'combine': min(256, t),
    }


def kernel(x, ln_in_g, ln_in_b, w_in, lru_conv_w, lru_conv_b, lru_w_rg, lru_b_rg, lru_w_ig, lru_b_ig,
           lru_lambda, lru_out_g, gdn_conv_w, gdn_a_log, gdn_dt_bias, gdn_norm_w, w_out, ln1_g, ln1_b,
           w_router, router_bias, w_gate, w_up, w_down, ws_gate, ws_up, ws_down, ln2_g, ln2_b):
    assert w_in.shape[0] == DEPTH == 1
    p = dict(ln_g=ln_in_g, ln_b=ln_in_b, w_in=w_in, lru_conv_w=lru_conv_w, lru_conv_b=lru_conv_b,
             lru_w_rg=lru_w_rg, lru_b_rg=lru_b_rg, lru_w_ig=lru_w_ig, lru_b_ig=lru_b_ig,
             lru_lambda=lru_lambda, lru_out_g=lru_out_g, gdn_conv_w=gdn_conv_w, gdn_a_log=gdn_a_log,
             gdn_dt_bias=gdn_dt_bias, gdn_norm_w=gdn_norm_w, w_out=w_out, ln1_g=ln1_g, ln1_b=ln1_b,
             w_router=w_router, router_bias=router_bias, w_gate=w_gate, w_up=w_up, w_down=w_down,
             ws_gate=ws_gate, ws_up=ws_up, ws_down=ws_down, ln2_g=ln2_g, ln2_b=ln2_b)
    bsz, seq, _ = x.shape
    return _layer(x, 0, p, _tiles(bsz, seq))
```

```python
import functools

import jax
import jax.numpy as jnp
from jax import lax
from jax.experimental import pallas as pl
from jax.experimental.pallas import tpu as pltpu
from jax.experimental.pallas import tpu_sc as plsc

F32 = jnp.float32
BF16 = jnp.bfloat16

D_MODEL = 1024
LRU_WIDTH = 512
LRU_BLOCKS = 8
LRU_C = 8.0
CONV_WIDTH = 4
GDN_HEADS = 4
GDN_DK = 128
GDN_DV = 128
GDN_CHUNK = 64
GDN_QK = GDN_HEADS * GDN_DK
GDN_V = GDN_HEADS * GDN_DV
N_MAIN = 2 * LRU_WIDTH + 2 * GDN_QK + 2 * GDN_V
N_EXPERTS = 256
TOP_K = 8
N_GROUPS = 8
GROUP_SIZE = N_EXPERTS // N_GROUPS
TOPK_GROUPS = 4
D_EXPERT = 256
D_SHARED = 256
ROUTED_SCALE = 2.5
MOE_BLOCK = 256
D_PACK = D_MODEL // 2
LN_EPS = 1e-5
NORM_EPS = 1e-6
DEPTH = 1
DEEPNORM_ALPHA = (2.0 * DEPTH) ** 0.25

HALO = 8
LANES = 128
VMEM_LIMIT = 56 * 1024 * 1024
SC_CORES = 2
SC_WORKERS = 32
SC_CHUNK = 64

NN = (((1,), (0,)), ((), ()))
NT = (((1,), (1,)), ((), ()))
TN = (((0,), (0,)), ((), ()))


def _dot(a, b, dims=NN):
    return lax.dot_general(a, b, dims, preferred_element_type=F32)


def _split(a):
    hi = a.astype(BF16)
    lo = (a - hi.astype(F32)).astype(BF16)
    return hi, lo


def _dot3(a, b, dims=NN):
    ah, al = _split(a)
    bh, bl = _split(b)
    return _dot(ah, bh, dims) + (_dot(ah, bl, dims) + _dot(al, bh, dims))


def _layer_norm(x, g, b):
    mu = jnp.mean(x, -1, keepdims=True)
    xc = x - mu
    var = jnp.mean(xc * xc, -1, keepdims=True)
    return xc * lax.rsqrt(var + LN_EPS) * g + b


def _sigmoid(x):
    return 1.0 / (1.0 + jnp.exp(-x))


def _silu(x):
    return x * _sigmoid(x)


def _softplus(x):
    return jnp.maximum(x, 0.0) + jnp.log1p(jnp.exp(-jnp.abs(x)))


def _gelu_tanh(x):
    c = 0.7978845608028654
    return x * (0.5 * (1.0 + jnp.tanh(c * (x + 0.044715 * (x * x * x)))))


def _pack_rows(x):
    hi = lax.bitcast_convert_type(x[:, :D_PACK].astype(BF16).astype(F32), jnp.uint32)
    lo = lax.bitcast_convert_type(x[:, D_PACK:].astype(BF16).astype(F32), jnp.uint32)
    return (hi & jnp.uint32(0xFFFF0000)) | (lo >> 16)


def _unpack_rows(w):
    hi = lax.bitcast_convert_type(w & jnp.uint32(0xFFFF0000), F32)
    lo = lax.bitcast_convert_type(w << 16, F32)
    return hi, lo


def _params(sem, **kw):
    return pltpu.CompilerParams(dimension_semantics=sem, vmem_limit_bytes=VMEM_LIMIT, **kw)


def _inproj_kernel(x_ref, g_ref, b_ref, w_ref, ws_ref, h_ref, proj_ref, small_ref, smallt_ref):
    h = _layer_norm(x_ref[...], g_ref[...], b_ref[...])
    h_ref[...] = h
    hb = h.astype(BF16)
    proj_ref[...] = _dot(hb, w_ref[...])
    small = _dot3(h, ws_ref[...])
    small_ref[...] = small
    smallt_ref[...] = small.T[:smallt_ref.shape[0], :]


def _inproj(x2d, g, b, w_main, w_small, tm):
    t = x2d.shape[0]
    return pl.pallas_call(
        _inproj_kernel,
        grid=(t // tm,),
        in_specs=[
            pl.BlockSpec((tm, D_MODEL), lambda i: (i, 0)),
            pl.BlockSpec((1, D_MODEL), lambda i: (0, 0)),
            pl.BlockSpec((1, D_MODEL), lambda i: (0, 0)),
            pl.BlockSpec((D_MODEL, N_MAIN), lambda i: (0, 0)),
            pl.BlockSpec((D_MODEL, LANES), lambda i: (0, 0)),
        ],
        out_specs=[
            pl.BlockSpec((tm, D_MODEL), lambda i: (i, 0)),
            pl.BlockSpec((tm, N_MAIN), lambda i: (i, 0)),
            pl.BlockSpec((tm, LANES), lambda i: (i, 0)),
            pl.BlockSpec((8, tm), lambda i: (0, i)),
        ],
        out_shape=[
            jax.ShapeDtypeStruct((t, D_MODEL), F32),
            jax.ShapeDtypeStruct((t, N_MAIN), F32),
            jax.ShapeDtypeStruct((t, LANES), F32),
            jax.ShapeDtypeStruct((8, t), F32),
        ],
        compiler_params=_params(("arbitrary",)),
        name="ln_inproj",
    )(x2d, g, b, w_main, w_small)


def _causal_conv(buf_ref, x, w_ref, first, rows):
    @pl.when(first)
    def _():
        buf_ref[0:HALO, :] = jnp.zeros((HALO, buf_ref.shape[1]), F32)

    buf_ref[HALO:HALO + rows, :] = x
    acc = None
    for j in range(CONV_WIDTH):
        off = HALO - (CONV_WIDTH - 1) + j
        term = buf_ref[off:off + rows, :] * w_ref[j:j + 1, :]
        acc = term if acc is None else acc + term
    buf_ref[0:HALO, :] = buf_ref[rows:rows + HALO, :]
    return acc


def _shift_rows(x, d, fill):
    rows = x.shape[0]
    if d % 8 == 0:
        pad = jnp.full((d, x.shape[1]), fill, x.dtype)
        return jnp.concatenate([pad, x[:rows - d]], axis=0)
    rolled = pltpu.roll(x, d, 0)
    row = lax.broadcasted_iota(jnp.int32, x.shape, 0)
    return jnp.where(row < d, fill, rolled)


def _lru_kernel(u_ref, gate_ref, cw_ref, cb_ref, wg_ref, bg_ref, lam_ref, og_ref,
                y_ref, ubuf, hcarry):
    s = pl.program_id(1)
    rows = u_ref.shape[0]

    @pl.when(s == 0)
    def _():
        hcarry[...] = jnp.zeros_like(hcarry)

    xc = _causal_conv(ubuf, u_ref[...], cw_ref, s == 0, rows) + cb_ref[...]
    gates = _dot(xc.astype(BF16), wg_ref[...]) + bg_ref[...]
    r = _sigmoid(gates[:, :LRU_WIDTH])
    i = _sigmoid(gates[:, LRU_WIDTH:])
    log_a = (-LRU_C) * r * _softplus(-lam_ref[...])
    a = jnp.exp(log_a)
    mult = jnp.sqrt(-jnp.tanh(log_a) * (a * a + 1.0))
    bv = mult * (i * xc)
    d = 1
    while d < rows:
        a_sh = _shift_rows(a, d, 1.0)
        b_sh = _shift_rows(bv, d, 0.0)
        bv = a * b_sh + bv
        a = a * a_sh
        d *= 2
    h = a * hcarry[...] + bv
    hcarry[...] = h[rows - 1:rows, :]
    y = h * _gelu_tanh(gate_ref[...])
    ms = jnp.mean(y * y, -1, keepdims=True)
    y_ref[...] = y * lax.rsqrt(ms + NORM_EPS) * og_ref[...]


def _lru(proj3, conv_w, conv_b, w_gates, b_gates, lam, out_g, ts):
    bsz, seq, _ = proj3.shape
    row = lambda n: pl.BlockSpec((1, n), lambda b, s: (0, 0))
    return pl.pallas_call(
        _lru_kernel,
        grid=(bsz, seq // ts),
        in_specs=[
            pl.BlockSpec((None, ts, LRU_WIDTH), lambda b, s: (b, s, 0)),
            pl.BlockSpec((None, ts, LRU_WIDTH), lambda b, s: (b, s, 1)),
            pl.BlockSpec((CONV_WIDTH, LRU_WIDTH), lambda b, s: (0, 0)),
            row(LRU_WIDTH),
            pl.BlockSpec((LRU_WIDTH, 2 * LRU_WIDTH), lambda b, s: (0, 0)),
            row(2 * LRU_WIDTH),
            row(LRU_WIDTH),
            row(LRU_WIDTH),
        ],
        out_specs=pl.BlockSpec((None, ts, LRU_WIDTH), lambda b, s: (b, s, 0)),
        out_shape=jax.ShapeDtypeStruct((bsz, seq, LRU_WIDTH), F32),
        scratch_shapes=[
            pltpu.VMEM((HALO + ts, LRU_WIDTH), F32),
            pltpu.VMEM((1, LRU_WIDTH), F32),
        ],
        compiler_params=_params(("arbitrary", "arbitrary")),
        name="rg_lru",
    )(proj3, proj3, conv_w, conv_b, w_gates, b_gates, lam, out_g)


def _bdot(a, b, dims=NN):
    return _dot(a.astype(BF16), b.astype(BF16), dims)


def _gdn_heads(args, norm_w, causal, strict, upper):
    c = GDN_CHUNK
    each = lambda f, *ls: [f(*xs) for xs in zip(*ls)]
    q, k, v, z, beta, g_col, g_row, st = [list(x) for x in zip(*args)]
    q = each(lambda x: x * lax.rsqrt(jnp.sum(x * x, -1, keepdims=True) + NORM_EPS) * (GDN_DK ** -0.5), q)
    k = each(lambda x: x * lax.rsqrt(jnp.sum(x * x, -1, keepdims=True) + NORM_EPS), k)
    gc_col = each(lambda g: jnp.sum(jnp.where(causal, g, 0.0), axis=1, keepdims=True), g_row)
    gc_row = each(lambda g: jnp.sum(jnp.where(upper, g, 0.0), axis=0, keepdims=True), g_col)
    decay = each(lambda gc, gr: jnp.exp(jnp.where(causal, gc - gr, -jnp.inf)), gc_col, gc_row)
    kb = each(lambda x, bt: x * bt, k, beta)
    vb = each(lambda x, bt: x * bt, v, beta)
    kk = each(lambda x, y: _bdot(x, y, NT), kb, k)
    a_mat = each(lambda m, d: jnp.where(strict, m * d, 0.0), kk, decay)
    e_col = each(jnp.exp, gc_col)
    rhs = each(lambda x, y, e: jnp.concatenate([x, y * e], axis=1), vb, kb, e_col)
    sol = each(lambda r, a: r - _bdot(a, r), rhs, a_mat)
    p = a_mat
    for _ in range(5):
        p = each(lambda x: _bdot(x, x), p)
        sol = each(lambda x, y: y + _bdot(x, y), p, sol)
    qk = each(lambda x, y: _bdot(x, y, NT), q, k)
    qk = each(lambda m, d: jnp.where(causal, m * d, 0.0), qk, decay)
    q_dec = each(lambda x, e: x * e, q, e_col)
    g_last = each(lambda gc: gc[c - 1:c, :], gc_col)
    k_dec = each(lambda x, gl, gc: x * jnp.exp(gl - gc), k, g_last, gc_col)
    ws = each(lambda x, s: _bdot(x[:, GDN_DV:], s), sol, st)
    qs = each(lambda x, s: _bdot(x, s), q_dec, st)
    v_new = each(lambda x, w: x[:, :GDN_DV] - w, sol, ws)
    o = each(lambda a, m, vn: a + _bdot(m, vn), qs, qk, v_new)
    kv = each(lambda x, vn: _bdot(x, vn, TN), k_dec, v_new)
    st_new = each(lambda s, gl, d: s * jnp.exp(gl) + d, st, g_last, kv)
    o = each(lambda x: x * lax.rsqrt(jnp.mean(x * x, -1, keepdims=True) + NORM_EPS) * norm_w, o)
    o = each(lambda x, zz: x * _silu(zz), o, z)
    return list(zip(o, st_new))


def _gdn_kernel(q_ref, k_ref, v_ref, z_ref, sm_ref, smt_ref, cwq_ref, cwk_ref, cwv_ref,
                alr_ref, dtr_ref, alc_ref, dtc_ref, nw_ref, y_ref, qbuf, kbuf, vbuf, state):
    n = pl.program_id(1)
    c = GDN_CHUNK
    nb = q_ref.shape[0]
    first = n == 0

    @pl.when(first)
    def _():
        state[...] = jnp.zeros_like(state)

    ri = lax.broadcasted_iota(jnp.int32, (c, c), 0)
    ci = lax.broadcasted_iota(jnp.int32, (c, c), 1)
    causal = ri >= ci
    strict = ri > ci
    upper = ri <= ci
    norm_w = nw_ref[...]

    args = []
    for b in range(nb):
        q_all = _silu(_causal_conv(qbuf.at[b], q_ref[b], cwq_ref, first, c))
        k_all = _silu(_causal_conv(kbuf.at[b], k_ref[b], cwk_ref, first, c))
        v_all = _silu(_causal_conv(vbuf.at[b], v_ref[b], cwv_ref, first, c))
        z_all = z_ref[b]
        sm = sm_ref[b]
        beta_all = _sigmoid(sm)
        g_cols = -jnp.exp(alr_ref[...]) * _softplus(sm + dtr_ref[...])
        g_rows = -jnp.exp(alc_ref[...]) * _softplus(smt_ref[b] + dtc_ref[...])
        for hd in range(GDN_HEADS):
            sl = slice(hd * GDN_DK, (hd + 1) * GDN_DK)
            args.append((q_all[:, sl], k_all[:, sl], v_all[:, sl], z_all[:, sl],
                         beta_all[:, hd:hd + 1],
                         g_cols[:, GDN_HEADS + hd:GDN_HEADS + hd + 1],
                         g_rows[GDN_HEADS + hd:GDN_HEADS + hd + 1, :],
                         state[b, hd]))
    outs = _gdn_heads(args, norm_w, causal, strict, upper)
    for b in range(nb):
        for hd in range(GDN_HEADS):
            o, st_new = outs[b * GDN_HEADS + hd]
            state[b, hd] = st_new
            y_ref[b, :, hd * GDN_DK:(hd + 1) * GDN_DK] = o


def _gdn(proj3, small3, smallt3, cwq, cwk, cwv, alr, dtr, alc, dtc, norm_w, nb):
    bsz, seq, _ = proj3.shape
    c = GDN_CHUNK
    nch = seq // c
    col = lambda j: pl.BlockSpec((nb, c, GDN_QK), lambda b, n: (b, n, j))
    const = lambda shape: pl.BlockSpec(shape, lambda b, n: (0,) * len(shape))
    return pl.pallas_call(
        _gdn_kernel,
        grid=(bsz // nb, nch),
        in_specs=[
            col(2), col(3), col(4), col(5),
            pl.BlockSpec((nb, c, LANES), lambda b, n: (b, n, 0)),
            pl.BlockSpec((nb, None, 8, c), lambda b, n: (b, n, 0, 0)),
            const((CONV_WIDTH, GDN_QK)), const((CONV_WIDTH, GDN_QK)), const((CONV_WIDTH, GDN_V)),
            const((1, LANES)), const((1, LANES)), const((8, 1)), const((8, 1)),
            const((1, GDN_DV)),
        ],
        out_specs=pl.BlockSpec((nb, c, GDN_V), lambda b, n: (b, n, 0)),
        out_shape=jax.ShapeDtypeStruct((bsz, seq, GDN_V), F32),
        scratch_shapes=[
            pltpu.VMEM((nb, HALO + c, GDN_QK), F32),
            pltpu.VMEM((nb, HALO + c, GDN_QK), F32),
            pltpu.VMEM((nb, HALO + c, GDN_V), F32),
            pltpu.VMEM((nb, GDN_HEADS, GDN_DK, GDN_DV), F32),
        ],
        compiler_params=_params(("arbitrary", "arbitrary")),
        name="gated_deltanet",
    )(proj3, proj3, proj3, proj3, small3, smallt3, cwq, cwk, cwv, alr, dtr, alc, dtc, norm_w)


def _router_kernel(yl_ref, yg_ref, h0_ref, wo1_ref, wo2_ref, g_ref, b_ref, wrt_ref, rb_ref,
                   h1_ref, h1p_ref, e_ref, w_ref, rank_ref, cnt_ref, carry):
    i = pl.program_id(0)
    tm = h0_ref.shape[0]

    @pl.when(i == 0)
    def _():
        carry[...] = jnp.zeros_like(carry)

    mix = _dot(yl_ref[...].astype(BF16), wo1_ref[...]) + _dot(yg_ref[...].astype(BF16), wo2_ref[...])
    h1 = _layer_norm(DEEPNORM_ALPHA * h0_ref[...] + mix, g_ref[...], b_ref[...])
    h1_ref[...] = h1
    h1p_ref[...] = _pack_rows(h1)

    scores = _sigmoid(_dot3(wrt_ref[...], h1, NT))
    choice = scores + rb_ref[...]
    neg = -jnp.inf
    gs_rows = []
    sub = lax.broadcasted_iota(jnp.int32, (GROUP_SIZE, tm), 0).astype(F32)
    for g in range(N_GROUPS):
        cg = choice[g * GROUP_SIZE:(g + 1) * GROUP_SIZE, :]
        m1 = jnp.max(cg, axis=0, keepdims=True)
        i1 = jnp.min(jnp.where(cg == m1, sub, float(GROUP_SIZE)), axis=0, keepdims=True)
        m2 = jnp.max(jnp.where(sub == i1, neg, cg), axis=0, keepdims=True)
        gs_rows.append(m1 + m2)
    gs = jnp.concatenate(gs_rows, axis=0)
    gi = lax.broadcasted_iota(jnp.int32, (N_GROUPS, tm), 0).astype(F32)
    gsel = jnp.zeros((N_GROUPS, tm), jnp.bool_)
    for _ in range(TOPK_GROUPS):
        m = jnp.max(gs, axis=0, keepdims=True)
        idx = jnp.min(jnp.where(gs == m, gi, float(N_GROUPS)), axis=0, keepdims=True)
        hit = gi == idx
        gsel = jnp.logical_or(gsel, hit)
        gs = jnp.where(hit, neg, gs)
    masked = jnp.concatenate(
        [jnp.where(gsel[g:g + 1, :], choice[g * GROUP_SIZE:(g + 1) * GROUP_SIZE, :], neg)
         for g in range(N_GROUPS)], axis=0)

    ei = lax.broadcasted_iota(jnp.int32, (N_EXPERTS, tm), 0).astype(F32)
    hits = []
    e_rows, w_rows = [], []
    multi = jnp.zeros((N_EXPERTS, tm), F32)
    for _ in range(TOP_K):
        m = jnp.max(masked, axis=0, keepdims=True)
        idx = jnp.min(jnp.where(masked == m, ei, float(N_EXPERTS)), axis=0, keepdims=True)
        hit = ei == idx
        hits.append(hit)
        e_rows.append(idx)
        w_rows.append(jnp.sum(jnp.where(hit, scores, 0.0), axis=0, keepdims=True))
        multi = multi + hit.astype(F32)
        masked = jnp.where(hit, neg, masked)
    wts = jnp.concatenate(w_rows, axis=0)
    wts = wts / (jnp.sum(wts, axis=0, keepdims=True) + 1e-20) * ROUTED_SCALE
    ti = lax.broadcasted_iota(jnp.int32, (tm, tm), 0)
    tj = lax.broadcasted_iota(jnp.int32, (tm, tm), 1)
    before = (ti < tj).astype(BF16)
    cum = _dot(multi.astype(BF16), before) + carry[...]
    r_rows = [jnp.sum(jnp.where(hit, cum, 0.0), axis=0, keepdims=True) for hit in hits]
    carry[...] = carry[...] + jnp.sum(multi, axis=1, keepdims=True)
    e_ref[...] = jnp.concatenate(e_rows, axis=0).astype(jnp.int32)
    w_ref[...] = wts
    rank_ref[...] = jnp.concatenate(r_rows, axis=0).astype(jnp.int32)
    cnt_ref[...] = carry[...].astype(jnp.int32)


def _router(y_lru, y_gdn, h0, wo1, wo2, g, b, w_router_t, rbias, tm):
    t = h0.shape[0]
    const = lambda shape: pl.BlockSpec(shape, lambda i: (0,) * len(shape))
    return pl.pallas_call(
        _router_kernel,
        grid=(t // tm,),
        in_specs=[
            pl.BlockSpec((tm, LRU_WIDTH), lambda i: (i, 0)),
            pl.BlockSpec((tm, GDN_V), lambda i: (i, 0)),
            pl.BlockSpec((tm, D_MODEL), lambda i: (i, 0)),
            const((LRU_WIDTH, D_MODEL)), const((GDN_V, D_MODEL)),
            const((1, D_MODEL)), const((1, D_MODEL)),
            const((N_EXPERTS, D_MODEL)), const((N_EXPERTS, 1)),
        ],
        out_specs=[
            pl.BlockSpec((tm, D_MODEL), lambda i: (i, 0)),
            pl.BlockSpec((tm, D_PACK), lambda i: (i, 0)),
            pl.BlockSpec((TOP_K, tm), lambda i: (0, i)),
            pl.BlockSpec((TOP_K, tm), lambda i: (0, i)),
            pl.BlockSpec((TOP_K, tm), lambda i: (0, i)),
            const((N_EXPERTS, 1)),
        ],
        out_shape=[
            jax.ShapeDtypeStruct((t, D_MODEL), F32),
            jax.ShapeDtypeStruct((t, D_PACK), jnp.uint32),
            jax.ShapeDtypeStruct((TOP_K, t), jnp.int32),
            jax.ShapeDtypeStruct((TOP_K, t), F32),
            jax.ShapeDtypeStruct((TOP_K, t), jnp.int32),
            jax.ShapeDtypeStruct((N_EXPERTS, 1), jnp.int32),
        ],
        scratch_shapes=[pltpu.VMEM((N_EXPERTS, 1), F32)],
        compiler_params=_params(("arbitrary",)),
        name="outproj_router",
    )(y_lru, y_gdn, h0, wo1, wo2, g, b, w_router_t, rbias)


def _dest_kernel(e_ref, r_ref, ps_ref, d_ref):
    tm = e_ref.shape[1]
    ei = lax.broadcasted_iota(jnp.int32, (N_EXPERTS, tm), 0)
    rows = []
    for k in range(TOP_K):
        hit = ei == e_ref[k:k + 1, :]
        rows.append(jnp.sum(jnp.where(hit, ps_ref[...], 0), axis=0, keepdims=True))
    d_ref[...] = jnp.concatenate(rows, axis=0) + r_ref[...]


def _dest(top_e, rank, pad_start, tm):
    t = top_e.shape[1]
    blk = pl.BlockSpec((TOP_K, tm), lambda i: (0, i))
    return pl.pallas_call(
        _dest_kernel,
        grid=(t // tm,),
        in_specs=[blk, blk, pl.BlockSpec((N_EXPERTS, 1), lambda i: (0, 0))],
        out_specs=blk,
        out_shape=jax.ShapeDtypeStruct((TOP_K, t), jnp.int32),
        compiler_params=_params(("arbitrary",)),
        name="moe_dest",
    )(top_e, rank, pad_start)


def _sc_scatter_rows(rows, idx, n_out, chunk):
    n_copies, t = idx.shape
    d = rows.shape[1]
    per_worker = t // SC_WORKERS
    n_chunks = per_worker // chunk
    mesh = plsc.VectorSubcoreMesh(core_axis_name="c", subcore_axis_name="s")
    idx_flat = idx.reshape(n_copies * t)

    @functools.partial(
        pl.kernel, mesh=mesh,
        out_type=jax.ShapeDtypeStruct((n_out, d), rows.dtype),
        scratch_types=[pltpu.VMEM((chunk,), jnp.int32) for _ in range(n_copies)] + [
            pltpu.VMEM((chunk, d), rows.dtype),
            pltpu.SemaphoreType.DMA,
        ],
    )
    def scatter(rows_hbm, idx_hbm, out_hbm, *scratch):
        idx_v = scratch[:n_copies]
        rows_v, sem = scratch[n_copies:]
        wid = lax.axis_index("s") * SC_CORES + lax.axis_index("c")
        base = wid * per_worker

        @pl.loop(0, n_chunks)
        def _(j):
            off = base + j * chunk
            for k in range(n_copies):
                pltpu.sync_copy(idx_hbm.at[pl.ds(k * t + off, chunk)], idx_v[k])
            pltpu.sync_copy(rows_hbm.at[pl.ds(off, chunk)], rows_v)
            copies = [pltpu.async_copy(rows_v, out_hbm.at[idx_v[k]], sem) for k in range(n_copies)]
            for cp in copies:
                cp.wait()

    return scatter(rows, idx_flat)


def _expert_kernel(be_ref, nv_ref, first_ref, slot_ref, next_ref, nu_ref,
                   xs_ref, wg_hbm, wu_hbm, wd_hbm, ys_ref, wg_f, wu_f, wd_f, wgu_b, wd_b, sem):
    i = pl.program_id(0)

    def fetch(e, slot):
        return (pltpu.make_async_copy(wg_hbm.at[e], wg_f.at[slot], sem.at[slot]),
                pltpu.make_async_copy(wu_hbm.at[e], wu_f.at[slot], sem.at[slot]),
                pltpu.make_async_copy(wd_hbm.at[e], wd_f.at[slot], sem.at[slot]))

    @pl.when(i < nu_ref[0])
    def _():
        e = be_ref[i]
        slot = slot_ref[i]

        @pl.when(first_ref[i] == 1)
        def _():
            @pl.when(i == 0)
            def _():
                for cp in fetch(e, slot):
                    cp.start()

            for cp in fetch(e, slot):
                cp.wait()

            @pl.when(next_ref[i] >= 0)
            def _():
                for cp in fetch(next_ref[i], 1 - slot):
                    cp.start()

            wgu_b[:, :D_EXPERT] = wg_f[slot].astype(BF16)
            wgu_b[:, D_EXPERT:] = wu_f[slot].astype(BF16)
            wd_b[...] = wd_f[slot].astype(BF16)

        row = lax.broadcasted_iota(jnp.int32, xs_ref.shape, 0)
        x_hi, x_lo = _unpack_rows(jnp.where(row < nv_ref[i], xs_ref[...], jnp.uint32(0)))
        gu = _dot(x_hi.astype(BF16), wgu_b[:D_PACK, :]) + _dot(x_lo.astype(BF16), wgu_b[D_PACK:, :])
        h = _silu(gu[:, :D_EXPERT]) * gu[:, D_EXPERT:]
        ys_ref[...] = _pack_rows(_dot(h.astype(BF16), wd_b[...]))


def _experts(blk_e, n_valid, first, slot, next_e, n_used, xs, w_gate, w_up, w_down):
    n_rows = xs.shape[0]
    n_blocks = n_rows // MOE_BLOCK
    blk = lambda i, be, nv, fi, sl, nx, nu: (jnp.minimum(i, nu[0] - 1), 0)
    return pl.pallas_call(
        _expert_kernel,
        grid_spec=pltpu.PrefetchScalarGridSpec(
            num_scalar_prefetch=6,
            grid=(n_blocks,),
            in_specs=[
                pl.BlockSpec((MOE_BLOCK, D_PACK), blk),
                pl.BlockSpec(memory_space=pl.ANY),
                pl.BlockSpec(memory_space=pl.ANY),
                pl.BlockSpec(memory_space=pl.ANY),
            ],
            out_specs=pl.BlockSpec((MOE_BLOCK, D_PACK), blk),
            scratch_shapes=[
                pltpu.VMEM((2, D_MODEL, D_EXPERT), F32),
                pltpu.VMEM((2, D_MODEL, D_EXPERT), F32),
                pltpu.VMEM((2, D_EXPERT, D_MODEL), F32),
                pltpu.VMEM((D_MODEL, 2 * D_EXPERT), BF16),
                pltpu.VMEM((D_EXPERT, D_MODEL), BF16),
                pltpu.SemaphoreType.DMA((2,)),
            ],
        ),
        out_shape=jax.ShapeDtypeStruct((n_rows, D_PACK), jnp.uint32),
        compiler_params=_params(("arbitrary",)),
        name="moe_experts",
    )(blk_e, n_valid, first, slot, next_e, n_used, xs, w_gate, w_up, w_down)


def _sc_gather_rows(table, idx, chunk):
    n_idx = idx.shape[0]
    d = table.shape[1]
    per_worker = n_idx // SC_WORKERS
    n_chunks = per_worker // chunk
    assert n_chunks % 2 == 0 and n_chunks * chunk * SC_WORKERS == n_idx
    mesh = plsc.VectorSubcoreMesh(core_axis_name="c", subcore_axis_name="s")

    @functools.partial(
        pl.kernel, mesh=mesh,
        out_type=jax.ShapeDtypeStruct((n_idx, d), table.dtype),
        scratch_types=[
            pltpu.VMEM((chunk,), jnp.int32), pltpu.VMEM((chunk,), jnp.int32),
            pltpu.VMEM((chunk, d), table.dtype), pltpu.VMEM((chunk, d), table.dtype),
            pltpu.SemaphoreType.DMA, pltpu.SemaphoreType.DMA, pltpu.SemaphoreType.DMA, pltpu.SemaphoreType.DMA,
        ],
    )
    def gather(table_hbm, idx_hbm, out_hbm, idx_v0, idx_v1, rows_v0, rows_v1, gsem0, gsem1, osem0, osem1):
        idx_v, rows_v, gsem, osem = (idx_v0, idx_v1), (rows_v0, rows_v1), (gsem0, gsem1), (osem0, osem1)
        wid = lax.axis_index("s") * SC_CORES + lax.axis_index("c")
        base = wid * per_worker

        def gather_copy(slot):
            return pltpu.make_async_copy(table_hbm.at[idx_v[slot]], rows_v[slot], gsem[slot])

        def out_copy(c, slot):
            return pltpu.make_async_copy(rows_v[slot], out_hbm.at[pl.ds(base + c * chunk, chunk)], osem[slot])

        def start_gather(c, slot):
            pltpu.sync_copy(idx_hbm.at[pl.ds(base + c * chunk, chunk)], idx_v[slot])
            gather_copy(slot).start()

        start_gather(0, 0)

        @pl.loop(0, n_chunks, step=2)
        def _(j):
            for b in range(2):
                c = j + b
                cur, other = b, 1 - b

                @pl.when(c >= 1)
                def _():
                    out_copy(c - 1, other).wait()

                @pl.when(c + 1 < n_chunks)
                def _():
                    start_gather(c + 1, other)

                gather_copy(cur).wait()
                out_copy(c, cur).start()

        out_copy(n_chunks - 1, 1).wait()

    return gather(table, idx)


def _combine_kernel(h1_ref, wts_ref, wsgu_ref, wsd_ref, g_ref, b_ref, yg_ref, out_ref):
    h1 = h1_ref[...]
    gu = _dot(h1.astype(BF16), wsgu_ref[...])
    hs = _silu(gu[:, :D_SHARED]) * gu[:, D_SHARED:]
    acc = DEEPNORM_ALPHA * h1 + _dot(hs.astype(BF16), wsd_ref[...])
    wts = wts_ref[...]
    acc_hi = acc[:, :D_PACK]
    acc_lo = acc[:, D_PACK:]
    for k in range(TOP_K):
        y_hi, y_lo = _unpack_rows(yg_ref[k])
        acc_hi = acc_hi + y_hi * wts[:, k:k + 1]
        acc_lo = acc_lo + y_lo * wts[:, k:k + 1]
    out_ref[...] = _layer_norm(jnp.concatenate([acc_hi, acc_lo], axis=1), g_ref[...], b_ref[...])


def _combine(h1, wts_t, ws_gu, ws_down, g, b, yg, tm):
    t = h1.shape[0]
    const = lambda shape: pl.BlockSpec(shape, lambda i: (0,) * len(shape))
    return pl.pallas_call(
        _combine_kernel,
        grid=(t // tm,),
        in_specs=[
            pl.BlockSpec((tm, D_MODEL), lambda i: (i, 0)),
            pl.BlockSpec((tm, TOP_K), lambda i: (i, 0)),
            const((D_MODEL, 2 * D_SHARED)), const((D_SHARED, D_MODEL)),
            const((1, D_MODEL)), const((1, D_MODEL)),
            pl.BlockSpec((TOP_K, tm, D_PACK), lambda i: (0, i, 0)),
        ],
        out_specs=pl.BlockSpec((tm, D_MODEL), lambda i: (i, 0)),
        out_shape=jax.ShapeDtypeStruct((t, D_MODEL), F32),
        compiler_params=_params(("arbitrary",)),
        name="moe_combine",
    )(h1, wts_t, ws_gu, ws_down, g, b, yg)


def _block_diag(w):
    nb, bi, bo = w.shape
    eye = jnp.eye(nb, dtype=w.dtype)
    return (eye[:, None, :, None] * w[:, :, None, :]).reshape(nb * bi, nb * bo)


def _pad_lanes(v, offset, width):
    return jnp.zeros((1, width), F32).at[0, offset:offset + v.shape[0]].set(v)


def _layer(h_in_x, l, p, tiles):
    bsz, seq, _ = h_in_x.shape
    t = bsz * seq
    row = lambda v: v.reshape(1, -1)

    w_in = p['w_in'][l]
    w_main = w_in[:, :N_MAIN].astype(BF16)
    w_small = jnp.zeros((D_MODEL, LANES), F32).at[:, :2 * GDN_HEADS].set(w_in[:, N_MAIN:])
    h0, proj, small, small_t = _inproj(h_in_x.reshape(t, D_MODEL), row(p['ln_g']), row(p['ln_b']),
                                       w_main, w_small, tiles['inproj'])
    proj3 = proj.reshape(bsz, seq, N_MAIN)

    w_gates = jnp.concatenate([_block_diag(p['lru_w_rg'][l]), _block_diag(p['lru_w_ig'][l])], 1).astype(BF16)
    b_gates = jnp.concatenate([p['lru_b_rg'][l], p['lru_b_ig'][l]]).reshape(1, -1)
    y_lru = _lru(proj3, p['lru_conv_w'][l], row(p['lru_conv_b'][l]), w_gates, b_gates,
                 row(p['lru_lambda'][l]), row(p['lru_out_g'][l]), tiles['lru'])

    nch = seq // GDN_CHUNK
    small3 = small.reshape(bsz, seq, LANES)
    smallt3 = small_t.reshape(8, bsz, nch, GDN_CHUNK).transpose(1, 2, 0, 3)
    cw = p['gdn_conv_w'][l]
    a_log, dt_bias = p['gdn_a_log'][l], p['gdn_dt_bias'][l]
    alr = _pad_lanes(a_log, GDN_HEADS, LANES)
    dtr = _pad_lanes(dt_bias, GDN_HEADS, LANES)
    alc = _pad_lanes(a_log, GDN_HEADS, 8).reshape(8, 1)
    dtc = _pad_lanes(dt_bias, GDN_HEADS, 8).reshape(8, 1)
    y_gdn = _gdn(proj3, small3, smallt3, cw[:, :GDN_QK], cw[:, GDN_QK:2 * GDN_QK], cw[:, 2 * GDN_QK:],
                 alr, dtr, alc, dtc, row(p['gdn_norm_w'][l]), tiles['gdn_nb'])

    w_out = p['w_out'][l].astype(BF16)
    h1, h1p, top_e, wts, rank, counts = _router(
        y_lru.reshape(t, LRU_WIDTH), y_gdn.reshape(t, GDN_V), h0, w_out[:LRU_WIDTH], w_out[LRU_WIDTH:],
        row(p['ln1_g'][l]), row(p['ln1_b'][l]), p['w_router'][l].T, p['router_bias'][l].reshape(-1, 1),
        tiles['router'])

    counts = counts[:, 0]
    padded = (counts + MOE_BLOCK - 1) // MOE_BLOCK * MOE_BLOCK
    pad_end = jnp.cumsum(padded)
    pad_start = pad_end - padded
    n_blocks = (t * TOP_K) // MOE_BLOCK + N_EXPERTS
    n_rows = n_blocks * MOE_BLOCK
    n_used = (pad_end[-1] // MOE_BLOCK).astype(jnp.int32)
    blk_ids = jnp.minimum(jnp.arange(n_blocks, dtype=jnp.int32), n_used - 1)
    blk_e = jnp.minimum(jnp.sum(pad_end[None, :] <= (blk_ids * MOE_BLOCK)[:, None], axis=1),
                        N_EXPERTS - 1).astype(jnp.int32)

    dest = _dest(top_e, rank, pad_start.reshape(-1, 1), tiles['dest'])
    n_valid = jnp.clip(counts[blk_e] - (blk_ids * MOE_BLOCK - pad_start[blk_e]), 0, MOE_BLOCK).astype(jnp.int32)
    xs = _sc_scatter_rows(h1p, dest, n_rows, SC_CHUNK)
    active = jnp.arange(n_blocks, dtype=jnp.int32) < n_used
    first = (active & jnp.concatenate([jnp.ones((1,), bool), blk_e[1:] != blk_e[:-1]])).astype(jnp.int32)
    slot = ((jnp.cumsum(first) - 1) % 2).astype(jnp.int32)
    used = counts > 0
    later = jnp.where(used[None, :] & (jnp.arange(N_EXPERTS)[None, :] > jnp.arange(N_EXPERTS)[:, None]),
                      jnp.arange(N_EXPERTS, dtype=jnp.int32)[None, :], N_EXPERTS)
    next_used = jnp.min(later, axis=1)
    next_e = jnp.where(next_used < N_EXPERTS, next_used, -1)[blk_e].astype(jnp.int32)
    ys = _experts(blk_e, n_valid, first, slot, next_e, n_used.reshape(1), xs,
                  p['w_gate'][l], p['w_up'][l], p['w_down'][l])
    ws_gu = jnp.concatenate([p['ws_gate'][l], p['ws_up'][l]], 1).astype(BF16)
    yg = _sc_gather_rows(ys, dest.reshape(TOP_K * t), SC_CHUNK).reshape(TOP_K, t, D_PACK)
    out = _combine(h1, wts.T, ws_gu, p['ws_down'][l].astype(BF16),
                   row(p['ln2_g'][l]), row(p['ln2_b'][l]), yg, tiles['combine'])
    return out.reshape(bsz, seq, D_MODEL)


def _tiles(bsz, seq):
    t = bsz * seq
    return {
        'inproj': min(512, t),
        'lru': min(256, seq),
        'gdn_nb': bsz,
        'router': min(256, t),
        'dest': min(512, t),
        'combine': min(256, t),
    }


def kernel(x, ln_in_g, ln_in_b, w_in, lru_conv_w, lru_conv_b, lru_w_rg, lru_b_rg, lru_w_ig, lru_b_ig,
           lru_lambda, lru_out_g, gdn_conv_w, gdn_a_log, gdn_dt_bias, gdn_norm_w, w_out, ln1_g, ln1_b,
           w_router, router_bias, w_gate, w_up, w_down, ws_gate, ws_up, ws_down, ln2_g, ln2_b):
    assert w_in.shape[0] == DEPTH == 1
    p = dict(ln_g=ln_in_g, ln_b=ln_in_b, w_in=w_in, lru_conv_w=lru_conv_w, lru_conv_b=lru_conv_b,
             lru_w_rg=lru_w_rg, lru_b_rg=lru_b_rg, lru_w_ig=lru_w_ig, lru_b_ig=lru_b_ig,
             lru_lambda=lru_lambda, lru_out_g=lru_out_g, gdn_conv_w=gdn_conv_w, gdn_a_log=gdn_a_log,
             gdn_dt_bias=gdn_dt_bias, gdn_norm_w=gdn_norm_w, w_out=w_out, ln1_g=ln1_g, ln1_b=ln1_b,
             w_router=w_router, router_bias=router_bias, w_gate=w_gate, w_up=w_up, w_down=w_down,
             ws_gate=ws_gate, ws_up=ws_up, ws_down=ws_down, ln2_g=ln2_g, ln2_b=ln2_b)
    bsz, seq, _ = x.shape
    return _layer(x, 0, p, _tiles(bsz, seq))
```

```python
import functools

import jax
import jax.numpy as jnp
from jax import lax
from jax.experimental import pallas as pl
from jax.experimental.pallas import tpu as pltpu
from jax.experimental.pallas import tpu_sc as plsc

F32 = jnp.float32
BF16 = jnp.bfloat16

D_MODEL = 1024
LRU_WIDTH = 512
LRU_BLOCKS = 8
LRU_C = 8.0
CONV_WIDTH = 4
GDN_HEADS = 4
GDN_DK = 128
GDN_DV = 128
GDN_CHUNK = 64
GDN_QK = GDN_HEADS * GDN_DK
GDN_V = GDN_HEADS * GDN_DV
N_MAIN = 2 * LRU_WIDTH + 2 * GDN_QK + 2 * GDN_V
N_EXPERTS = 256
TOP_K = 8
N_GROUPS = 8
GROUP_SIZE = N_EXPERTS // N_GROUPS
TOPK_GROUPS = 4
D_EXPERT = 256
D_SHARED = 256
ROUTED_SCALE = 2.5
MOE_BLOCK = 256
D_PACK = D_MODEL // 2
LN_EPS = 1e-5
NORM_EPS = 1e-6
DEPTH = 1
DEEPNORM_ALPHA = (2.0 * DEPTH) ** 0.25

HALO = 8
LANES = 128
VMEM_LIMIT = 56 * 1024 * 1024
WEIGHT_DMA_BANDS = 4
SC_CORES = 2
SC_WORKERS = 32
SC_CHUNK = 64

NN = (((1,), (0,)), ((), ()))
NT = (((1,), (1,)), ((), ()))
TN = (((0,), (0,)), ((), ()))


def _dot(a, b, dims=NN):
    return lax.dot_general(a, b, dims, preferred_element_type=F32)


def _split(a):
    hi = a.astype(BF16)
    lo = (a - hi.astype(F32)).astype(BF16)
    return hi, lo


def _dot3(a, b, dims=NN):
    ah, al = _split(a)
    bh, bl = _split(b)
    return _dot(ah, bh, dims) + (_dot(ah, bl, dims) + _dot(al, bh, dims))


def _layer_norm(x, g, b):
    mu = jnp.mean(x, -1, keepdims=True)
    xc = x - mu
    var = jnp.mean(xc * xc, -1, keepdims=True)
    return xc * lax.rsqrt(var + LN_EPS) * g + b


def _sigmoid(x):
    return 1.0 / (1.0 + jnp.exp(-x))


def _silu(x):
    return x * _sigmoid(x)


def _softplus(x):
    return jnp.maximum(x, 0.0) + jnp.log1p(jnp.exp(-jnp.abs(x)))


def _gelu_tanh(x):
    c = 0.7978845608028654
    return x * (0.5 * (1.0 + jnp.tanh(c * (x + 0.044715 * (x * x * x)))))


def _pack_rows(x):
    hi = lax.bitcast_convert_type(x[:, :D_PACK].astype(BF16).astype(F32), jnp.uint32)
    lo = lax.bitcast_convert_type(x[:, D_PACK:].astype(BF16).astype(F32), jnp.uint32)
    return (hi & jnp.uint32(0xFFFF0000)) | (lo >> 16)


def _unpack_rows(w):
    hi = lax.bitcast_convert_type(w & jnp.uint32(0xFFFF0000), F32)
    lo = lax.bitcast_convert_type(w << 16, F32)
    return hi, lo


def _params(sem, **kw):
    return pltpu.CompilerParams(dimension_semantics=sem, vmem_limit_bytes=VMEM_LIMIT, **kw)


def _inproj_kernel(x_ref, g_ref, b_ref, w_ref, ws_ref, h_ref, proj_ref, small_ref, smallt_ref):
    h = _layer_norm(x_ref[...], g_ref[...], b_ref[...])
    h_ref[...] = h
    hb = h.astype(BF16)
    proj_ref[...] = _dot(hb, w_ref[...])
    small = _dot3(h, ws_ref[...])
    small_ref[...] = small
    smallt_ref[...] = small.T[:smallt_ref.shape[0], :]


def _inproj(x2d, g, b, w_main, w_small, tm):
    t = x2d.shape[0]
    return pl.pallas_call(
        _inproj_kernel,
        grid=(t // tm,),
        in_specs=[
            pl.BlockSpec((tm, D_MODEL), lambda i: (i, 0)),
            pl.BlockSpec((1, D_MODEL), lambda i: (0, 0)),
            pl.BlockSpec((1, D_MODEL), lambda i: (0, 0)),
            pl.BlockSpec((D_MODEL, N_MAIN), lambda i: (0, 0)),
            pl.BlockSpec((D_MODEL, LANES), lambda i: (0, 0)),
        ],
        out_specs=[
            pl.BlockSpec((tm, D_MODEL), lambda i: (i, 0)),
            pl.BlockSpec((tm, N_MAIN), lambda i: (i, 0)),
            pl.BlockSpec((tm, LANES), lambda i: (i, 0)),
            pl.BlockSpec((8, tm), lambda i: (0, i)),
        ],
        out_shape=[
            jax.ShapeDtypeStruct((t, D_MODEL), F32),
            jax.ShapeDtypeStruct((t, N_MAIN), F32),
            jax.ShapeDtypeStruct((t, LANES), F32),
            jax.ShapeDtypeStruct((8, t), F32),
        ],
        compiler_params=_params(("arbitrary",)),
        name="ln_inproj",
    )(x2d, g, b, w_main, w_small)


def _causal_conv(buf_ref, x, w_ref, first, rows):
    @pl.when(first)
    def _():
        buf_ref[0:HALO, :] = jnp.zeros((HALO, buf_ref.shape[1]), F32)

    buf_ref[HALO:HALO + rows, :] = x
    acc = None
    for j in range(CONV_WIDTH):
        off = HALO - (CONV_WIDTH - 1) + j
        term = buf_ref[off:off + rows, :] * w_ref[j:j + 1, :]
        acc = term if acc is None else acc + term
    buf_ref[0:HALO, :] = buf_ref[rows:rows + HALO, :]
    return acc


def _shift_rows(x, d, fill):
    rows = x.shape[0]
    if d % 8 == 0:
        pad = jnp.full((d, x.shape[1]), fill, x.dtype)
        return jnp.concatenate([pad, x[:rows - d]], axis=0)
    rolled = pltpu.roll(x, d, 0)
    row = lax.broadcasted_iota(jnp.int32, x.shape, 0)
    return jnp.where(row < d, fill, rolled)


def _lru_kernel(u_ref, gate_ref, cw_ref, cb_ref, wg_ref, bg_ref, lam_ref, og_ref,
                y_ref, ubuf, hcarry):
    s = pl.program_id(1)
    rows = u_ref.shape[0]

    @pl.when(s == 0)
    def _():
        hcarry[...] = jnp.zeros_like(hcarry)

    xc = _causal_conv(ubuf, u_ref[...], cw_ref, s == 0, rows) + cb_ref[...]
    gates = _dot(xc.astype(BF16), wg_ref[...]) + bg_ref[...]
    r = _sigmoid(gates[:, :LRU_WIDTH])
    i = _sigmoid(gates[:, LRU_WIDTH:])
    log_a = (-LRU_C) * r * _softplus(-lam_ref[...])
    a = jnp.exp(log_a)
    mult = jnp.sqrt(-jnp.tanh(log_a) * (a * a + 1.0))
    bv = mult * (i * xc)
    d = 1
    while d < rows:
        a_sh = _shift_rows(a, d, 1.0)
        b_sh = _shift_rows(bv, d, 0.0)
        bv = a * b_sh + bv
        a = a * a_sh
        d *= 2
    h = a * hcarry[...] + bv
    hcarry[...] = h[rows - 1:rows, :]
    y = h * _gelu_tanh(gate_ref[...])
    ms = jnp.mean(y * y, -1, keepdims=True)
    y_ref[...] = y * lax.rsqrt(ms + NORM_EPS) * og_ref[...]


def _lru(proj3, conv_w, conv_b, w_gates, b_gates, lam, out_g, ts):
    bsz, seq, _ = proj3.shape
    row = lambda n: pl.BlockSpec((1, n), lambda b, s: (0, 0))
    return pl.pallas_call(
        _lru_kernel,
        grid=(bsz, seq // ts),
        in_specs=[
            pl.BlockSpec((None, ts, LRU_WIDTH), lambda b, s: (b, s, 0)),
            pl.BlockSpec((None, ts, LRU_WIDTH), lambda b, s: (b, s, 1)),
            pl.BlockSpec((CONV_WIDTH, LRU_WIDTH), lambda b, s: (0, 0)),
            row(LRU_WIDTH),
            pl.BlockSpec((LRU_WIDTH, 2 * LRU_WIDTH), lambda b, s: (0, 0)),
            row(2 * LRU_WIDTH),
            row(LRU_WIDTH),
            row(LRU_WIDTH),
        ],
        out_specs=pl.BlockSpec((None, ts, LRU_WIDTH), lambda b, s: (b, s, 0)),
        out_shape=jax.ShapeDtypeStruct((bsz, seq, LRU_WIDTH), F32),
        scratch_shapes=[
            pltpu.VMEM((HALO + ts, LRU_WIDTH), F32),
            pltpu.VMEM((1, LRU_WIDTH), F32),
        ],
        compiler_params=_params(("arbitrary", "arbitrary")),
        name="rg_lru",
    )(proj3, proj3, conv_w, conv_b, w_gates, b_gates, lam, out_g)


def _bdot(a, b, dims=NN):
    return _dot(a.astype(BF16), b.astype(BF16), dims)


def _gdn_heads(args, norm_w, causal, strict, upper):
    c = GDN_CHUNK
    each = lambda f, *ls: [f(*xs) for xs in zip(*ls)]
    q, k, v, z, beta, g_col, g_row, st = [list(x) for x in zip(*args)]
    q = each(lambda x: x * lax.rsqrt(jnp.sum(x * x, -1, keepdims=True) + NORM_EPS) * (GDN_DK ** -0.5), q)
    k = each(lambda x: x * lax.rsqrt(jnp.sum(x * x, -1, keepdims=True) + NORM_EPS), k)
    gc_col = each(lambda g: jnp.sum(jnp.where(causal, g, 0.0), axis=1, keepdims=True), g_row)
    gc_row = each(lambda g: jnp.sum(jnp.where(upper, g, 0.0), axis=0, keepdims=True), g_col)
    decay = each(lambda gc, gr: jnp.exp(jnp.where(causal, gc - gr, -jnp.inf)), gc_col, gc_row)
    kb = each(lambda x, bt: x * bt, k, beta)
    vb = each(lambda x, bt: x * bt, v, beta)
    kk = each(lambda x, y: _bdot(x, y, NT), kb, k)
    a_mat = each(lambda m, d: jnp.where(strict, m * d, 0.0), kk, decay)
    e_col = each(jnp.exp, gc_col)
    rhs = each(lambda x, y, e: jnp.concatenate([x, y * e], axis=1), vb, kb, e_col)
    sol = each(lambda r, a: r - _bdot(a, r), rhs, a_mat)
    p = a_mat
    for _ in range(5):
        p = each(lambda x: _bdot(x, x), p)
        sol = each(lambda x, y: y + _bdot(x, y), p, sol)
    qk = each(lambda x, y: _bdot(x, y, NT), q, k)
    qk = each(lambda m, d: jnp.where(causal, m * d, 0.0), qk, decay)
    q_dec = each(lambda x, e: x * e, q, e_col)
    g_last = each(lambda gc: gc[c - 1:c, :], gc_col)
    k_dec = each(lambda x, gl, gc: x * jnp.exp(gl - gc), k, g_last, gc_col)
    ws = each(lambda x, s: _bdot(x[:, GDN_DV:], s), sol, st)
    qs = each(lambda x, s: _bdot(x, s), q_dec, st)
    v_new = each(lambda x, w: x[:, :GDN_DV] - w, sol, ws)
    o = each(lambda a, m, vn: a + _bdot(m, vn), qs, qk, v_new)
    kv = each(lambda x, vn: _bdot(x, vn, TN), k_dec, v_new)
    st_new = each(lambda s, gl, d: s * jnp.exp(gl) + d, st, g_last, kv)
    o = each(lambda x: x * lax.rsqrt(jnp.mean(x * x, -1, keepdims=True) + NORM_EPS) * norm_w, o)
    o = each(lambda x, zz: x * _silu(zz), o, z)
    return list(zip(o, st_new))


def _gdn_kernel(q_ref, k_ref, v_ref, z_ref, sm_ref, smt_ref, cwq_ref, cwk_ref, cwv_ref,
                alr_ref, dtr_ref, alc_ref, dtc_ref, nw_ref, y_ref, qbuf, kbuf, vbuf, state):
    n = pl.program_id(1)
    c = GDN_CHUNK
    nb = q_ref.shape[0]
    first = n == 0

    @pl.when(first)
    def _():
        state[...] = jnp.zeros_like(state)

    ri = lax.broadcasted_iota(jnp.int32, (c, c), 0)
    ci = lax.broadcasted_iota(jnp.int32, (c, c), 1)
    causal = ri >= ci
    strict = ri > ci
    upper = ri <= ci
    norm_w = nw_ref[...]

    args = []
    for b in range(nb):
        q_all = _silu(_causal_conv(qbuf.at[b], q_ref[b], cwq_ref, first, c))
        k_all = _silu(_causal_conv(kbuf.at[b], k_ref[b], cwk_ref, first, c))
        v_all = _silu(_causal_conv(vbuf.at[b], v_ref[b], cwv_ref, first, c))
        z_all = z_ref[b]
        sm = sm_ref[b]
        beta_all = _sigmoid(sm)
        g_cols = -jnp.exp(alr_ref[...]) * _softplus(sm + dtr_ref[...])
        g_rows = -jnp.exp(alc_ref[...]) * _softplus(smt_ref[b] + dtc_ref[...])
        for hd in range(GDN_HEADS):
            sl = slice(hd * GDN_DK, (hd + 1) * GDN_DK)
            args.append((q_all[:, sl], k_all[:, sl], v_all[:, sl], z_all[:, sl],
                         beta_all[:, hd:hd + 1],
                         g_cols[:, GDN_HEADS + hd:GDN_HEADS + hd + 1],
                         g_rows[GDN_HEADS + hd:GDN_HEADS + hd + 1, :],
                         state[b, hd]))
    outs = _gdn_heads(args, norm_w, causal, strict, upper)
    for b in range(nb):
        for hd in range(GDN_HEADS):
            o, st_new = outs[b * GDN_HEADS + hd]
            state[b, hd] = st_new
            y_ref[b, :, hd * GDN_DK:(hd + 1) * GDN_DK] = o


def _gdn(proj3, small3, smallt3, cwq, cwk, cwv, alr, dtr, alc, dtc, norm_w, nb):
    bsz, seq, _ = proj3.shape
    c = GDN_CHUNK
    nch = seq // c
    col = lambda j: pl.BlockSpec((nb, c, GDN_QK), lambda b, n: (b, n, j))
    const = lambda shape: pl.BlockSpec(shape, lambda b, n: (0,) * len(shape))
    return pl.pallas_call(
        _gdn_kernel,
        grid=(bsz // nb, nch),
        in_specs=[
            col(2), col(3), col(4), col(5),
            pl.BlockSpec((nb, c, LANES), lambda b, n: (b, n, 0)),
            pl.BlockSpec((nb, None, 8, c), lambda b, n: (b, n, 0, 0)),
            const((CONV_WIDTH, GDN_QK)), const((CONV_WIDTH, GDN_QK)), const((CONV_WIDTH, GDN_V)),
            const((1, LANES)), const((1, LANES)), const((8, 1)), const((8, 1)),
            const((1, GDN_DV)),
        ],
        out_specs=pl.BlockSpec((nb, c, GDN_V), lambda b, n: (b, n, 0)),
        out_shape=jax.ShapeDtypeStruct((bsz, seq, GDN_V), F32),
        scratch_shapes=[
            pltpu.VMEM((nb, HALO + c, GDN_QK), F32),
            pltpu.VMEM((nb, HALO + c, GDN_QK), F32),
            pltpu.VMEM((nb, HALO + c, GDN_V), F32),
            pltpu.VMEM((nb, GDN_HEADS, GDN_DK, GDN_DV), F32),
        ],
        compiler_params=_params(("arbitrary", "arbitrary")),
        name="gated_deltanet",
    )(proj3, proj3, proj3, proj3, small3, smallt3, cwq, cwk, cwv, alr, dtr, alc, dtc, norm_w)


def _router_kernel(yl_ref, yg_ref, h0_ref, wo1_ref, wo2_ref, g_ref, b_ref, wrt_ref, rb_ref,
                   h1_ref, h1p_ref, e_ref, w_ref, rank_ref, cnt_ref, carry):
    i = pl.program_id(0)
    tm = h0_ref.shape[0]

    @pl.when(i == 0)
    def _():
        carry[...] = jnp.zeros_like(carry)

    mix = _dot(yl_ref[...].astype(BF16), wo1_ref[...]) + _dot(yg_ref[...].astype(BF16), wo2_ref[...])
    h1 = _layer_norm(DEEPNORM_ALPHA * h0_ref[...] + mix, g_ref[...], b_ref[...])
    h1_ref[...] = h1
    h1p_ref[...] = _pack_rows(h1)

    scores = _sigmoid(_dot3(wrt_ref[...], h1, NT))
    choice = scores + rb_ref[...]
    neg = -jnp.inf
    gs_rows = []
    sub = lax.broadcasted_iota(jnp.int32, (GROUP_SIZE, tm), 0).astype(F32)
    for g in range(N_GROUPS):
        cg = choice[g * GROUP_SIZE:(g + 1) * GROUP_SIZE, :]
        m1 = jnp.max(cg, axis=0, keepdims=True)
        i1 = jnp.min(jnp.where(cg == m1, sub, float(GROUP_SIZE)), axis=0, keepdims=True)
        m2 = jnp.max(jnp.where(sub == i1, neg, cg), axis=0, keepdims=True)
        gs_rows.append(m1 + m2)
    gs = jnp.concatenate(gs_rows, axis=0)
    gi = lax.broadcasted_iota(jnp.int32, (N_GROUPS, tm), 0).astype(F32)
    gsel = jnp.zeros((N_GROUPS, tm), jnp.bool_)
    for _ in range(TOPK_GROUPS):
        m = jnp.max(gs, axis=0, keepdims=True)
        idx = jnp.min(jnp.where(gs == m, gi, float(N_GROUPS)), axis=0, keepdims=True)
        hit = gi == idx
        gsel = jnp.logical_or(gsel, hit)
        gs = jnp.where(hit, neg, gs)
    masked = jnp.concatenate(
        [jnp.where(gsel[g:g + 1, :], choice[g * GROUP_SIZE:(g + 1) * GROUP_SIZE, :], neg)
         for g in range(N_GROUPS)], axis=0)

    ei = lax.broadcasted_iota(jnp.int32, (N_EXPERTS, tm), 0).astype(F32)
    hits = []
    e_rows, w_rows = [], []
    multi = jnp.zeros((N_EXPERTS, tm), F32)
    for _ in range(TOP_K):
        m = jnp.max(masked, axis=0, keepdims=True)
        idx = jnp.min(jnp.where(masked == m, ei, float(N_EXPERTS)), axis=0, keepdims=True)
        hit = ei == idx
        hits.append(hit)
        e_rows.append(idx)
        w_rows.append(jnp.sum(jnp.where(hit, scores, 0.0), axis=0, keepdims=True))
        multi = multi + hit.astype(F32)
        masked = jnp.where(hit, neg, masked)
    wts = jnp.concatenate(w_rows, axis=0)
    wts = wts / (jnp.sum(wts, axis=0, keepdims=True) + 1e-20) * ROUTED_SCALE
    ti = lax.broadcasted_iota(jnp.int32, (tm, tm), 0)
    tj = lax.broadcasted_iota(jnp.int32, (tm, tm), 1)
    before = (ti < tj).astype(BF16)
    cum = _dot(multi.astype(BF16), before) + carry[...]
    r_rows = [jnp.sum(jnp.where(hit, cum, 0.0), axis=0, keepdims=True) for hit in hits]
    carry[...] = carry[...] + jnp.sum(multi, axis=1, keepdims=True)
    e_ref[...] = jnp.concatenate(e_rows, axis=0).astype(jnp.int32)
    w_ref[...] = wts
    rank_ref[...] = jnp.concatenate(r_rows, axis=0).astype(jnp.int32)
    cnt_ref[...] = carry[...].astype(jnp.int32)


def _router(y_lru, y_gdn, h0, wo1, wo2, g, b, w_router_t, rbias, tm):
    t = h0.shape[0]
    const = lambda shape: pl.BlockSpec(shape, lambda i: (0,) * len(shape))
    return pl.pallas_call(
        _router_kernel,
        grid=(t // tm,),
        in_specs=[
            pl.BlockSpec((tm, LRU_WIDTH), lambda i: (i, 0)),
            pl.BlockSpec((tm, GDN_V), lambda i: (i, 0)),
            pl.BlockSpec((tm, D_MODEL), lambda i: (i, 0)),
            const((LRU_WIDTH, D_MODEL)), const((GDN_V, D_MODEL)),
            const((1, D_MODEL)), const((1, D_MODEL)),
            const((N_EXPERTS, D_MODEL)), const((N_EXPERTS, 1)),
        ],
        out_specs=[
            pl.BlockSpec((tm, D_MODEL), lambda i: (i, 0)),
            pl.BlockSpec((tm, D_PACK), lambda i: (i, 0)),
            pl.BlockSpec((TOP_K, tm), lambda i: (0, i)),
            pl.BlockSpec((TOP_K, tm), lambda i: (0, i)),
            pl.BlockSpec((TOP_K, tm), lambda i: (0, i)),
            const((N_EXPERTS, 1)),
        ],
        out_shape=[
            jax.ShapeDtypeStruct((t, D_MODEL), F32),
            jax.ShapeDtypeStruct((t, D_PACK), jnp.uint32),
            jax.ShapeDtypeStruct((TOP_K, t), jnp.int32),
            jax.ShapeDtypeStruct((TOP_K, t), F32),
            jax.ShapeDtypeStruct((TOP_K, t), jnp.int32),
            jax.ShapeDtypeStruct((N_EXPERTS, 1), jnp.int32),
        ],
        scratch_shapes=[pltpu.VMEM((N_EXPERTS, 1), F32)],
        compiler_params=_params(("arbitrary",)),
        name="outproj_router",
    )(y_lru, y_gdn, h0, wo1, wo2, g, b, w_router_t, rbias)


def _dest_kernel(e_ref, r_ref, ps_ref, d_ref):
    tm = e_ref.shape[1]
    ei = lax.broadcasted_iota(jnp.int32, (N_EXPERTS, tm), 0)
    rows = []
    for k in range(TOP_K):
        hit = ei == e_ref[k:k + 1, :]
        rows.append(jnp.sum(jnp.where(hit, ps_ref[...], 0), axis=0, keepdims=True))
    d_ref[...] = jnp.concatenate(rows, axis=0) + r_ref[...]


def _dest(top_e, rank, pad_start, tm):
    t = top_e.shape[1]
    blk = pl.BlockSpec((TOP_K, tm), lambda i: (0, i))
    return pl.pallas_call(
        _dest_kernel,
        grid=(t // tm,),
        in_specs=[blk, blk, pl.BlockSpec((N_EXPERTS, 1), lambda i: (0, 0))],
        out_specs=blk,
        out_shape=jax.ShapeDtypeStruct((TOP_K, t), jnp.int32),
        compiler_params=_params(("arbitrary",)),
        name="moe_dest",
    )(top_e, rank, pad_start)


def _sc_scatter_rows(rows, idx, n_out, chunk):
    n_copies, t = idx.shape
    d = rows.shape[1]
    per_worker = t // SC_WORKERS
    n_chunks = per_worker // chunk
    mesh = plsc.VectorSubcoreMesh(core_axis_name="c", subcore_axis_name="s")
    idx_flat = idx.reshape(n_copies * t)

    @functools.partial(
        pl.kernel, mesh=mesh,
        out_type=jax.ShapeDtypeStruct((n_out, d), rows.dtype),
        scratch_types=[pltpu.VMEM((chunk,), jnp.int32) for _ in range(n_copies)] + [
            pltpu.VMEM((chunk, d), rows.dtype),
            pltpu.SemaphoreType.DMA,
        ],
    )
    def scatter(rows_hbm, idx_hbm, out_hbm, *scratch):
        idx_v = scratch[:n_copies]
        rows_v, sem = scratch[n_copies:]
        wid = lax.axis_index("s") * SC_CORES + lax.axis_index("c")
        base = wid * per_worker

        @pl.loop(0, n_chunks)
        def _(j):
            off = base + j * chunk
            for k in range(n_copies):
                pltpu.sync_copy(idx_hbm.at[pl.ds(k * t + off, chunk)], idx_v[k])
            pltpu.sync_copy(rows_hbm.at[pl.ds(off, chunk)], rows_v)
            copies = [pltpu.async_copy(rows_v, out_hbm.at[idx_v[k]], sem) for k in range(n_copies)]
            for cp in copies:
                cp.wait()

    return scatter(rows, idx_flat)


def _expert_kernel(be_ref, nv_ref, first_ref, slot_ref, next_ref, nu_ref,
                   xs_ref, wg_hbm, wu_hbm, wd_hbm, ys_ref, wg_f, wu_f, wd_f, wgu_b, wd_b, sem):
    i = pl.program_id(0)

    def fetch(e, slot):
        copies = []
        for src, dst in ((wg_hbm, wg_f), (wu_hbm, wu_f), (wd_hbm, wd_f)):
            band = src.shape[1] // WEIGHT_DMA_BANDS
            for j in range(WEIGHT_DMA_BANDS):
                rows = pl.ds(j * band, band)
                copies.append(pltpu.make_async_copy(src.at[e, rows], dst.at[slot, rows], sem.at[slot]))
        return copies

    @pl.when(i < nu_ref[0])
    def _():
        e = be_ref[i]
        slot = slot_ref[i]

        @pl.when(first_ref[i] == 1)
        def _():
            @pl.when(i == 0)
            def _():
                for cp in fetch(e, slot):
                    cp.start()

            for cp in fetch(e, slot):
                cp.wait()

            @pl.when(next_ref[i] >= 0)
            def _():
                for cp in fetch(next_ref[i], 1 - slot):
                    cp.start()

            wgu_b[:, :D_EXPERT] = wg_f[slot].astype(BF16)
            wgu_b[:, D_EXPERT:] = wu_f[slot].astype(BF16)
            wd_b[...] = wd_f[slot].astype(BF16)

        row = lax.broadcasted_iota(jnp.int32, xs_ref.shape, 0)
        x_hi, x_lo = _unpack_rows(jnp.where(row < nv_ref[i], xs_ref[...], jnp.uint32(0)))
        gu = _dot(x_hi.astype(BF16), wgu_b[:D_PACK, :]) + _dot(x_lo.astype(BF16), wgu_b[D_PACK:, :])
        h = _silu(gu[:, :D_EXPERT]) * gu[:, D_EXPERT:]
        ys_ref[...] = _pack_rows(_dot(h.astype(BF16), wd_b[...]))


def _experts(blk_e, n_valid, first, slot, next_e, n_used, xs, w_gate, w_up, w_down):
    n_rows = xs.shape[0]
    n_blocks = n_rows // MOE_BLOCK
    blk = lambda i, be, nv, fi, sl, nx, nu: (jnp.minimum(i, nu[0] - 1), 0)
    return pl.pallas_call(
        _expert_kernel,
        grid_spec=pltpu.PrefetchScalarGridSpec(
            num_scalar_prefetch=6,
            grid=(n_blocks,),
            in_specs=[
                pl.BlockSpec((MOE_BLOCK, D_PACK), blk),
                pl.BlockSpec(memory_space=pl.ANY),
                pl.BlockSpec(memory_space=pl.ANY),
                pl.BlockSpec(memory_space=pl.ANY),
            ],
            out_specs=pl.BlockSpec((MOE_BLOCK, D_PACK), blk),
            scratch_shapes=[
                pltpu.VMEM((2, D_MODEL, D_EXPERT), F32),
                pltpu.VMEM((2, D_MODEL, D_EXPERT), F32),
                pltpu.VMEM((2, D_EXPERT, D_MODEL), F32),
                pltpu.VMEM((D_MODEL, 2 * D_EXPERT), BF16),
                pltpu.VMEM((D_EXPERT, D_MODEL), BF16),
                pltpu.SemaphoreType.DMA((2,)),
            ],
        ),
        out_shape=jax.ShapeDtypeStruct((n_rows, D_PACK), jnp.uint32),
        compiler_params=_params(("arbitrary",)),
        name="moe_experts",
    )(blk_e, n_valid, first, slot, next_e, n_used, xs, w_gate, w_up, w_down)


def _sc_gather_rows(table, idx, chunk):
    n_idx = idx.shape[0]
    d = table.shape[1]
    per_worker = n_idx // SC_WORKERS
    n_chunks = per_worker // chunk
    assert n_chunks % 2 == 0 and n_chunks * chunk * SC_WORKERS == n_idx
    mesh = plsc.VectorSubcoreMesh(core_axis_name="c", subcore_axis_name="s")

    @functools.partial(
        pl.kernel, mesh=mesh,
        out_type=jax.ShapeDtypeStruct((n_idx, d), table.dtype),
        scratch_types=[
            pltpu.VMEM((chunk,), jnp.int32), pltpu.VMEM((chunk,), jnp.int32),
            pltpu.VMEM((chunk, d), table.dtype), pltpu.VMEM((chunk, d), table.dtype),
            pltpu.SemaphoreType.DMA, pltpu.SemaphoreType.DMA, pltpu.SemaphoreType.DMA, pltpu.SemaphoreType.DMA,
        ],
    )
    def gather(table_hbm, idx_hbm, out_hbm, idx_v0, idx_v1, rows_v0, rows_v1, gsem0, gsem1, osem0, osem1):
        idx_v, rows_v, gsem, osem = (idx_v0, idx_v1), (rows_v0, rows_v1), (gsem0, gsem1), (osem0, osem1)
        wid = lax.axis_index("s") * SC_CORES + lax.axis_index("c")
        base = wid * per_worker

        def gather_copy(slot):
            return pltpu.make_async_copy(table_hbm.at[idx_v[slot]], rows_v[slot], gsem[slot])

        def out_copy(c, slot):
            return pltpu.make_async_copy(rows_v[slot], out_hbm.at[pl.ds(base + c * chunk, chunk)], osem[slot])

        def start_gather(c, slot):
            pltpu.sync_copy(idx_hbm.at[pl.ds(base + c * chunk, chunk)], idx_v[slot])
            gather_copy(slot).start()

        start_gather(0, 0)

        @pl.loop(0, n_chunks, step=2)
        def _(j):
            for b in range(2):
                c = j + b
                cur, other = b, 1 - b

                @pl.when(c >= 1)
                def _():
                    out_copy(c - 1, other).wait()

                @pl.when(c + 1 < n_chunks)
                def _():
                    start_gather(c + 1, other)

                gather_copy(cur).wait()
                out_copy(c, cur).start()

        out_copy(n_chunks - 1, 1).wait()

    return gather(table, idx)


def _combine_kernel(h1_ref, wts_ref, wsgu_ref, wsd_ref, g_ref, b_ref, yg_ref, out_ref):
    h1 = h1_ref[...]
    gu = _dot(h1.astype(BF16), wsgu_ref[...])
    hs = _silu(gu[:, :D_SHARED]) * gu[:, D_SHARED:]
    acc = DEEPNORM_ALPHA * h1 + _dot(hs.astype(BF16), wsd_ref[...])
    wts = wts_ref[...]
    acc_hi = acc[:, :D_PACK]
    acc_lo = acc[:, D_PACK:]
    for k in range(TOP_K):
        y_hi, y_lo = _unpack_rows(yg_ref[k])
        acc_hi = acc_hi + y_hi * wts[:, k:k + 1]
        acc_lo = acc_lo + y_lo * wts[:, k:k + 1]
    out_ref[...] = _layer_norm(jnp.concatenate([acc_hi, acc_lo], axis=1), g_ref[...], b_ref[...])


def _combine(h1, wts_t, ws_gu, ws_down, g, b, yg, tm):
    t = h1.shape[0]
    const = lambda shape: pl.BlockSpec(shape, lambda i: (0,) * len(shape))
    return pl.pallas_call(
        _combine_kernel,
        grid=(t // tm,),
        in_specs=[
            pl.BlockSpec((tm, D_MODEL), lambda i: (i, 0)),
            pl.BlockSpec((tm, TOP_K), lambda i: (i, 0)),
            const((D_MODEL, 2 * D_SHARED)), const((D_SHARED, D_MODEL)),
            const((1, D_MODEL)), const((1, D_MODEL)),
            pl.BlockSpec((TOP_K, tm, D_PACK), lambda i: (0, i, 0)),
        ],
        out_specs=pl.BlockSpec((tm, D_MODEL), lambda i: (i, 0)),
        out_shape=jax.ShapeDtypeStruct((t, D_MODEL), F32),
        compiler_params=_params(("arbitrary",)),
        name="moe_combine",
    )(h1, wts_t, ws_gu, ws_down, g, b, yg)


def _block_diag(w):
    nb, bi, bo = w.shape
    eye = jnp.eye(nb, dtype=w.dtype)
    return (eye[:, None, :, None] * w[:, :, None, :]).reshape(nb * bi, nb * bo)


def _pad_lanes(v, offset, width):
    return jnp.zeros((1, width), F32).at[0, offset:offset + v.shape[0]].set(v)


def _layer(h_in_x, l, p, tiles):
    bsz, seq, _ = h_in_x.shape
    t = bsz * seq
    row = lambda v: v.reshape(1, -1)

    w_in = p['w_in'][l]
    w_main = w_in[:, :N_MAIN].astype(BF16)
    w_small = jnp.zeros((D_MODEL, LANES), F32).at[:, :2 * GDN_HEADS].set(w_in[:, N_MAIN:])
    h0, proj, small, small_t = _inproj(h_in_x.reshape(t, D_MODEL), row(p['ln_g']), row(p['ln_b']),
                                       w_main, w_small, tiles['inproj'])
    proj3 = proj.reshape(bsz, seq, N_MAIN)

    w_gates = jnp.concatenate([_block_diag(p['lru_w_rg'][l]), _block_diag(p['lru_w_ig'][l])], 1).astype(BF16)
    b_gates = jnp.concatenate([p['lru_b_rg'][l], p['lru_b_ig'][l]]).reshape(1, -1)
    y_lru = _lru(proj3, p['lru_conv_w'][l], row(p['lru_conv_b'][l]), w_gates, b_gates,
                 row(p['lru_lambda'][l]), row(p['lru_out_g'][l]), tiles['lru'])

    nch = seq // GDN_CHUNK
    small3 = small.reshape(bsz, seq, LANES)
    smallt3 = small_t.reshape(8, bsz, nch, GDN_CHUNK).transpose(1, 2, 0, 3)
    cw = p['gdn_conv_w'][l]
    a_log, dt_bias = p['gdn_a_log'][l], p['gdn_dt_bias'][l]
    alr = _pad_lanes(a_log, GDN_HEADS, LANES)
    dtr = _pad_lanes(dt_bias, GDN_HEADS, LANES)
    alc = _pad_lanes(a_log, GDN_HEADS, 8).reshape(8, 1)
    dtc = _pad_lanes(dt_bias, GDN_HEADS, 8).reshape(8, 1)
    y_gdn = _gdn(proj3, small3, smallt3, cw[:, :GDN_QK], cw[:, GDN_QK:2 * GDN_QK], cw[:, 2 * GDN_QK:],
                 alr, dtr, alc, dtc, row(p['gdn_norm_w'][l]), tiles['gdn_nb'])

    w_out = p['w_out'][l].astype(BF16)
    h1, h1p, top_e, wts, rank, counts = _router(
        y_lru.reshape(t, LRU_WIDTH), y_gdn.reshape(t, GDN_V), h0, w_out[:LRU_WIDTH], w_out[LRU_WIDTH:],
        row(p['ln1_g'][l]), row(p['ln1_b'][l]), p['w_router'][l].T, p['router_bias'][l].reshape(-1, 1),
        tiles['router'])

    counts = counts[:, 0]
    padded = (counts + MOE_BLOCK - 1) // MOE_BLOCK * MOE_BLOCK
    pad_end = jnp.cumsum(padded)
    pad_start = pad_end - padded
    n_blocks = (t * TOP_K) // MOE_BLOCK + N_EXPERTS
    n_rows = n_blocks * MOE_BLOCK
    n_used = (pad_end[-1] // MOE_BLOCK).astype(jnp.int32)
    blk_ids = jnp.minimum(jnp.arange(n_blocks, dtype=jnp.int32), n_used - 1)
    blk_e = jnp.minimum(jnp.sum(pad_end[None, :] <= (blk_ids * MOE_BLOCK)[:, None], axis=1),
                        N_EXPERTS - 1).astype(jnp.int32)

    dest = _dest(top_e, rank, pad_start.reshape(-1, 1), tiles['dest'])
    n_valid = jnp.clip(counts[blk_e] - (blk_ids * MOE_BLOCK - pad_start[blk_e]), 0, MOE_BLOCK).astype(jnp.int32)
    xs = _sc_scatter_rows(h1p, dest, n_rows, SC_CHUNK)
    active = jnp.arange(n_blocks, dtype=jnp.int32) < n_used
    first = (active & jnp.concatenate([jnp.ones((1,), bool), blk_e[1:] != blk_e[:-1]])).astype(jnp.int32)
    slot = ((jnp.cumsum(first) - 1) % 2).astype(jnp.int32)
    used = counts > 0
    later = jnp.where(used[None, :] & (jnp.arange(N_EXPERTS)[None, :] > jnp.arange(N_EXPERTS)[:, None]),
                      jnp.arange(N_EXPERTS, dtype=jnp.int32)[None, :], N_EXPERTS)
    next_used = jnp.min(later, axis=1)
    next_e = jnp.where(next_used < N_EXPERTS, next_used, -1)[blk_e].astype(jnp.int32)
    ys = _experts(blk_e, n_valid, first, slot, next_e, n_used.reshape(1), xs,
                  p['w_gate'][l], p['w_up'][l], p['w_down'][l])
    ws_gu = jnp.concatenate([p['ws_gate'][l], p['ws_up'][l]], 1).astype(BF16)
    yg = _sc_gather_rows(ys, dest.reshape(TOP_K * t), SC_CHUNK).reshape(TOP_K, t, D_PACK)
    out = _combine(h1, wts.T, ws_gu, p['ws_down'][l].astype(BF16),
                   row(p['ln2_g'][l]), row(p['ln2_b'][l]), yg, tiles['combine'])
    return out.reshape(bsz, seq, D_MODEL)


def _tiles(bsz, seq):
    t = bsz * seq
    return {
        'inproj': min(512, t),
        'lru': min(256, seq),
        'gdn_nb': bsz,
        'router': min(256, t),
        'dest': min(512, t),
        'combine': min(256, t),
    }


def kernel(x, ln_in_g, ln_in_b, w_in, lru_conv_w, lru_conv_b, lru_w_rg, lru_b_rg, lru_w_ig, lru_b_ig,
           lru_lambda, lru_out_g, gdn_conv_w, gdn_a_log, gdn_dt_bias, gdn_norm_w, w_out, ln1_g, ln1_b,
           w_router, router_bias, w_gate, w_up, w_down, ws_gate, ws_up, ws_down, ln2_g, ln2_b):
    assert w_in.shape[0] == DEPTH == 1
    p = dict(ln_g=ln_in_g, ln_b=ln_in_b, w_in=w_in, lru_conv_w=lru_conv_w, lru_conv_b=lru_conv_b,
             lru_w_rg=lru_w_rg, lru_b_rg=lru_b_rg, lru_w_ig=lru_w_ig, lru_b_ig=lru_b_ig,
             lru_lambda=lru_lambda, lru_out_g=lru_out_g, gdn_conv_w=gdn_conv_w, gdn_a_log=gdn_a_log,
             gdn_dt_bias=gdn_dt_bias, gdn_norm_w=gdn_norm_w, w_out=w_out, ln1_g=ln1_g, ln1_b=ln1_b,
             w_router=w_router, router_bias=router_bias, w_gate=w_gate, w_up=w_up, w_down=w_down,
             ws_gate=ws_gate, ws_up=ws_up, ws_down=ws_down, ln2_g=ln2_g, ln2_b=ln2_b)
    bsz, seq, _ = x.shape
    return _layer(x, 0, p, _tiles(bsz, seq))
```

```python
import functools

import jax
import jax.numpy as jnp
from jax import lax
from jax.experimental import pallas as pl
from jax.experimental.pallas import tpu as pltpu
from jax.experimental.pallas import tpu_sc as plsc

F32 = jnp.float32
BF16 = jnp.bfloat16

D_MODEL = 1024
LRU_WIDTH = 512
LRU_BLOCKS = 8
LRU_C = 8.0
CONV_WIDTH = 4
GDN_HEADS = 4
GDN_DK = 128
GDN_DV = 128
GDN_CHUNK = 64
GDN_GROUP = 2
GDN_QK = GDN_HEADS * GDN_DK
GDN_V = GDN_HEADS * GDN_DV
N_MAIN = 2 * LRU_WIDTH + 2 * GDN_QK + 2 * GDN_V
N_EXPERTS = 256
TOP_K = 8
N_GROUPS = 8
GROUP_SIZE = N_EXPERTS // N_GROUPS
TOPK_GROUPS = 4
D_EXPERT = 256
D_SHARED = 256
ROUTED_SCALE = 2.5
MOE_BLOCK = 256
D_PACK = D_MODEL // 2
LN_EPS = 1e-5
NORM_EPS = 1e-6
DEPTH = 1
DEEPNORM_ALPHA = (2.0 * DEPTH) ** 0.25

HALO = 8
CONV_GROUP = 512
CONV_GROUPS = (0, 2, 3, 4)
LANES = 128
VMEM_LIMIT = 56 * 1024 * 1024
SC_CORES = 2
SC_WORKERS = 32
SC_CHUNK = 64

NN = (((1,), (0,)), ((), ()))
NT = (((1,), (1,)), ((), ()))
TN = (((0,), (0,)), ((), ()))


def _dot(a, b, dims=NN):
    return lax.dot_general(a, b, dims, preferred_element_type=F32)


def _split(a):
    hi = a.astype(BF16)
    lo = (a - hi.astype(F32)).astype(BF16)
    return hi, lo


def _dot3(a, b, dims=NN):
    ah, al = _split(a)
    bh, bl = _split(b)
    return _dot(ah, bh, dims) + (_dot(ah, bl, dims) + _dot(al, bh, dims))


def _layer_norm(x, g, b):
    mu = jnp.mean(x, -1, keepdims=True)
    xc = x - mu
    var = jnp.mean(xc * xc, -1, keepdims=True)
    return xc * lax.rsqrt(var + LN_EPS) * g + b


def _sigmoid(x):
    return 1.0 / (1.0 + jnp.exp(-x))


def _silu(x):
    return x * _sigmoid(x)


def _softplus(x):
    return jnp.maximum(x, 0.0) + jnp.log1p(jnp.exp(-jnp.abs(x)))


def _gelu_tanh(x):
    c = 0.7978845608028654
    return x * (0.5 * (1.0 + jnp.tanh(c * (x + 0.044715 * (x * x * x)))))


def _pack_rows(x):
    hi = lax.bitcast_convert_type(x[:, :D_PACK].astype(BF16).astype(F32), jnp.uint32)
    lo = lax.bitcast_convert_type(x[:, D_PACK:].astype(BF16).astype(F32), jnp.uint32)
    return (hi & jnp.uint32(0xFFFF0000)) | (lo >> 16)


def _unpack_rows(w):
    hi = lax.bitcast_convert_type(w & jnp.uint32(0xFFFF0000), F32)
    lo = lax.bitcast_convert_type(w << 16, F32)
    return hi, lo


def _params(sem, **kw):
    return pltpu.CompilerParams(dimension_semantics=sem, vmem_limit_bytes=VMEM_LIMIT, **kw)


def _inproj_kernel(x_ref, g_ref, b_ref, w_ref, ws_ref, cw_ref, cb_ref,
                   h_ref, proj_ref, small_ref, smallt_ref, hist, *, tiles_per_seq):
    i = pl.program_id(0)
    tm = x_ref.shape[0]
    h = _layer_norm(x_ref[...], g_ref[...], b_ref[...])
    h_ref[...] = h
    hb = h.astype(BF16)

    @pl.when(i % tiles_per_seq == 0)
    def _():
        hist[...] = jnp.zeros_like(hist)

    for g in range(N_MAIN // CONV_GROUP):
        cols = slice(g * CONV_GROUP, (g + 1) * CONV_GROUP)
        p = _dot(hb, w_ref[:, cols])
        if g in CONV_GROUPS:
            xcat = jnp.concatenate([hist[:, cols], p], axis=0)
            acc = cb_ref[:, cols]
            for j in range(CONV_WIDTH):
                off = HALO - (CONV_WIDTH - 1) + j
                acc = acc + xcat[off:off + tm, :] * cw_ref[j:j + 1, cols]
            hist[:, cols] = p[tm - HALO:, :]
            p = acc
        proj_ref[:, cols] = p
    small = _dot3(h, ws_ref[...])
    small_ref[...] = small
    smallt_ref[...] = small.T[:smallt_ref.shape[0], :]


def _inproj(x2d, g, b, w_main, w_small, conv_w, conv_b, tm, seq):
    t = x2d.shape[0]
    return pl.pallas_call(
        functools.partial(_inproj_kernel, tiles_per_seq=seq // tm),
        grid=(t // tm,),
        in_specs=[
            pl.BlockSpec((tm, D_MODEL), lambda i: (i, 0)),
            pl.BlockSpec((1, D_MODEL), lambda i: (0, 0)),
            pl.BlockSpec((1, D_MODEL), lambda i: (0, 0)),
            pl.BlockSpec((D_MODEL, N_MAIN), lambda i: (0, 0)),
            pl.BlockSpec((D_MODEL, LANES), lambda i: (0, 0)),
            pl.BlockSpec((CONV_WIDTH, N_MAIN), lambda i: (0, 0)),
            pl.BlockSpec((1, N_MAIN), lambda i: (0, 0)),
        ],
        out_specs=[
            pl.BlockSpec((tm, D_MODEL), lambda i: (i, 0)),
            pl.BlockSpec((tm, N_MAIN), lambda i: (i, 0)),
            pl.BlockSpec((tm, LANES), lambda i: (i, 0)),
            pl.BlockSpec((8, tm), lambda i: (0, i)),
        ],
        out_shape=[
            jax.ShapeDtypeStruct((t, D_MODEL), F32),
            jax.ShapeDtypeStruct((t, N_MAIN), F32),
            jax.ShapeDtypeStruct((t, LANES), F32),
            jax.ShapeDtypeStruct((8, t), F32),
        ],
        scratch_shapes=[pltpu.VMEM((HALO, N_MAIN), F32)],
        compiler_params=_params(("arbitrary",)),
        name="ln_inproj",
    )(x2d, g, b, w_main, w_small, conv_w, conv_b)


def _shift_rows(x, d, fill):
    rows = x.shape[0]
    if d % 8 == 0:
        pad = jnp.full((d, x.shape[1]), fill, x.dtype)
        return jnp.concatenate([pad, x[:rows - d]], axis=0)
    rolled = pltpu.roll(x, d, 0)
    row = lax.broadcasted_iota(jnp.int32, x.shape, 0)
    return jnp.where(row < d, fill, rolled)


def _lru_kernel(xc_ref, gate_ref, wg_ref, bg_ref, lam_ref, og_ref, y_ref, hcarry):
    s = pl.program_id(1)
    rows = xc_ref.shape[0]

    @pl.when(s == 0)
    def _():
        hcarry[...] = jnp.zeros_like(hcarry)

    xc = xc_ref[...]
    gates = _dot(xc.astype(BF16), wg_ref[...]) + bg_ref[...]
    r = _sigmoid(gates[:, :LRU_WIDTH])
    i = _sigmoid(gates[:, LRU_WIDTH:])
    log_a = (-LRU_C) * r * _softplus(-lam_ref[...])
    a = jnp.exp(log_a)
    mult = jnp.sqrt(-jnp.tanh(log_a) * (a * a + 1.0))
    bv = mult * (i * xc)
    d = 1
    while d < rows:
        a_sh = _shift_rows(a, d, 1.0)
        b_sh = _shift_rows(bv, d, 0.0)
        bv = a * b_sh + bv
        a = a * a_sh
        d *= 2
    h = a * hcarry[...] + bv
    hcarry[...] = h[rows - 1:rows, :]
    y = h * _gelu_tanh(gate_ref[...])
    ms = jnp.mean(y * y, -1, keepdims=True)
    y_ref[...] = y * lax.rsqrt(ms + NORM_EPS) * og_ref[...]


def _lru(proj3, w_gates, b_gates, lam, out_g, ts):
    bsz, seq, _ = proj3.shape
    row = lambda n: pl.BlockSpec((1, n), lambda b, s: (0, 0))
    return pl.pallas_call(
        _lru_kernel,
        grid=(bsz, seq // ts),
        in_specs=[
            pl.BlockSpec((None, ts, LRU_WIDTH), lambda b, s: (b, s, 0)),
            pl.BlockSpec((None, ts, LRU_WIDTH), lambda b, s: (b, s, 1)),
            pl.BlockSpec((LRU_WIDTH, 2 * LRU_WIDTH), lambda b, s: (0, 0)),
            row(2 * LRU_WIDTH),
            row(LRU_WIDTH),
            row(LRU_WIDTH),
        ],
        out_specs=pl.BlockSpec((None, ts, LRU_WIDTH), lambda b, s: (b, s, 0)),
        out_shape=jax.ShapeDtypeStruct((bsz, seq, LRU_WIDTH), F32),
        scratch_shapes=[
            pltpu.VMEM((1, LRU_WIDTH), F32),
        ],
        compiler_params=_params(("arbitrary", "arbitrary")),
        name="rg_lru",
    )(proj3, proj3, w_gates, b_gates, lam, out_g)


def _bdot(a, b, dims=NN):
    return _dot(a.astype(BF16), b.astype(BF16), dims)


def _gdn_heads(args, norm_w):
    c = GDN_CHUNK
    r = GDN_GROUP * c
    ri = lax.broadcasted_iota(jnp.int32, (r, r), 0)
    ci = lax.broadcasted_iota(jnp.int32, (r, r), 1)
    same = (ri // c) == (ci // c)
    causal = same & (ri >= ci)
    strict = same & (ri > ci)
    upper = same & (ri <= ci)
    chunk_of_row = lax.broadcasted_iota(jnp.int32, (r, 1), 0) // c
    each = lambda f, *ls: [f(*xs) for xs in zip(*ls)]
    q, k, v, z, beta, g_col, g_row, st = [list(x) for x in zip(*args)]
    q = each(lambda x: x * lax.rsqrt(jnp.sum(x * x, -1, keepdims=True) + NORM_EPS) * (GDN_DK ** -0.5), q)
    k = each(lambda x: x * lax.rsqrt(jnp.sum(x * x, -1, keepdims=True) + NORM_EPS), k)
    gc_col = each(lambda g: jnp.sum(jnp.where(causal, g, 0.0), axis=1, keepdims=True), g_row)
    gc_row = each(lambda g: jnp.sum(jnp.where(upper, g, 0.0), axis=0, keepdims=True), g_col)
    decay = each(lambda gc, gr: jnp.exp(jnp.where(causal, gc - gr, -jnp.inf)), gc_col, gc_row)
    kb = each(lambda x, bt: x * bt, k, beta)
    vb = each(lambda x, bt: x * bt, v, beta)
    kk = each(lambda x, y: _bdot(x, y, NT), kb, k)
    a_mat = each(lambda m, d: jnp.where(strict, m * d, 0.0), kk, decay)
    e_col = each(jnp.exp, gc_col)
    rhs = each(lambda x, y, e: jnp.concatenate([x, y * e], axis=1), vb, kb, e_col)
    sol = each(lambda rr, a: rr - _bdot(a, rr), rhs, a_mat)
    p = a_mat
    for _ in range(5):
        p = each(lambda x: _bdot(x, x), p)
        sol = each(lambda x, y: y + _bdot(x, y), p, sol)
    qk = each(lambda x, y: _bdot(x, y, NT), q, k)
    qk = each(lambda m, d: jnp.where(causal, m * d, 0.0), qk, decay)
    q_dec = each(lambda x, e: x * e, q, e_col)
    g_last = [each(lambda gc: gc[(j + 1) * c - 1:(j + 1) * c, :], gc_col) for j in range(GDN_GROUP)]

    def last_of_own_chunk(*gl):
        out = gl[-1]
        for j in range(GDN_GROUP - 2, -1, -1):
            out = jnp.where(chunk_of_row == j, gl[j], out)
        return out

    g_end = each(last_of_own_chunk, *g_last)
    k_dec = each(lambda x, ge, gc: x * jnp.exp(ge - gc), k, g_end, gc_col)
    qs_parts, v_parts = [], []
    for j in range(GDN_GROUP):
        rows = slice(j * c, (j + 1) * c)
        ws = each(lambda x, s: _bdot(x[rows, GDN_DV:], s), sol, st)
        qs_parts.append(each(lambda x, s: _bdot(x[rows], s), q_dec, st))
        v_new = each(lambda x, w: x[rows, :GDN_DV] - w, sol, ws)
        v_parts.append(v_new)
        kv = each(lambda x, vn: _bdot(x[rows], vn, TN), k_dec, v_new)
        st = each(lambda s, gl, d: s * jnp.exp(gl) + d, st, g_last[j], kv)
    qs = each(lambda *parts: jnp.concatenate(parts, axis=0), *qs_parts)
    v_all = each(lambda *parts: jnp.concatenate(parts, axis=0), *v_parts)
    o = each(lambda a, m, vn: a + _bdot(m, vn), qs, qk, v_all)
    o = each(lambda x: x * lax.rsqrt(jnp.mean(x * x, -1, keepdims=True) + NORM_EPS) * norm_w, o)
    o = each(lambda x, zz: x * _silu(zz), o, z)
    return list(zip(o, st))


def _gdn_kernel(q_ref, k_ref, v_ref, z_ref, sm_ref, smt_ref,
                alr_ref, dtr_ref, alc_ref, dtc_ref, nw_ref, y_ref, state):
    n = pl.program_id(1)
    c = GDN_GROUP * GDN_CHUNK
    nb = q_ref.shape[0]
    first = n == 0

    @pl.when(first)
    def _():
        state[...] = jnp.zeros_like(state)

    norm_w = nw_ref[...]

    args = []
    for b in range(nb):
        q_all = _silu(q_ref[b])
        k_all = _silu(k_ref[b])
        v_all = _silu(v_ref[b])
        z_all = z_ref[b]
        sm = sm_ref[b]
        beta_all = _sigmoid(sm)
        g_cols = -jnp.exp(alr_ref[...]) * _softplus(sm + dtr_ref[...])
        g_rows = -jnp.exp(alc_ref[...]) * _softplus(smt_ref[b] + dtc_ref[...])
        for hd in range(GDN_HEADS):
            sl = slice(hd * GDN_DK, (hd + 1) * GDN_DK)
            args.append((q_all[:, sl], k_all[:, sl], v_all[:, sl], z_all[:, sl],
                         beta_all[:, hd:hd + 1],
                         g_cols[:, GDN_HEADS + hd:GDN_HEADS + hd + 1],
                         g_rows[GDN_HEADS + hd:GDN_HEADS + hd + 1, :],
                         state[b, hd]))
    outs = _gdn_heads(args, norm_w)
    for b in range(nb):
        for hd in range(GDN_HEADS):
            o, st_new = outs[b * GDN_HEADS + hd]
            state[b, hd] = st_new
            y_ref[b, :, hd * GDN_DK:(hd + 1) * GDN_DK] = o


def _gdn(proj3, small3, smallt3, alr, dtr, alc, dtc, norm_w, nb):
    bsz, seq, _ = proj3.shape
    c = GDN_GROUP * GDN_CHUNK
    nch = seq // c
    col = lambda j: pl.BlockSpec((nb, c, GDN_QK), lambda b, n: (b, n, j))
    const = lambda shape: pl.BlockSpec(shape, lambda b, n: (0,) * len(shape))
    return pl.pallas_call(
        _gdn_kernel,
        grid=(bsz // nb, nch),
        in_specs=[
            col(2), col(3), col(4), col(5),
            pl.BlockSpec((nb, c, LANES), lambda b, n: (b, n, 0)),
            pl.BlockSpec((nb, None, 8, c), lambda b, n: (b, n, 0, 0)),
            const((1, LANES)), const((1, LANES)), const((8, 1)), const((8, 1)),
            const((1, GDN_DV)),
        ],
        out_specs=pl.BlockSpec((nb, c, GDN_V), lambda b, n: (b, n, 0)),
        out_shape=jax.ShapeDtypeStruct((bsz, seq, GDN_V), F32),
        scratch_shapes=[
            pltpu.VMEM((nb, GDN_HEADS, GDN_DK, GDN_DV), F32),
        ],
        compiler_params=_params(("arbitrary", "arbitrary")),
        name="gated_deltanet",
    )(proj3, proj3, proj3, proj3, small3, smallt3, alr, dtr, alc, dtc, norm_w)


def _router_kernel(yl_ref, yg_ref, h0_ref, wo1_ref, wo2_ref, g_ref, b_ref, wrt_ref, rb_ref,
                   h1_ref, h1p_ref, e_ref, w_ref, rank_ref, cnt_ref, carry):
    i = pl.program_id(0)
    tm = h0_ref.shape[0]

    @pl.when(i == 0)
    def _():
        carry[...] = jnp.zeros_like(carry)

    mix = _dot(yl_ref[...].astype(BF16), wo1_ref[...]) + _dot(yg_ref[...].astype(BF16), wo2_ref[...])
    h1 = _layer_norm(DEEPNORM_ALPHA * h0_ref[...] + mix, g_ref[...], b_ref[...])
    h1_ref[...] = h1
    h1p_ref[...] = _pack_rows(h1)

    scores = _sigmoid(_dot3(wrt_ref[...], h1, NT))
    choice = scores + rb_ref[...]
    neg = -jnp.inf
    gs_rows = []
    sub = lax.broadcasted_iota(jnp.int32, (GROUP_SIZE, tm), 0).astype(F32)
    for g in range(N_GROUPS):
        cg = choice[g * GROUP_SIZE:(g + 1) * GROUP_SIZE, :]
        m1 = jnp.max(cg, axis=0, keepdims=True)
        i1 = jnp.min(jnp.where(cg == m1, sub, float(GROUP_SIZE)), axis=0, keepdims=True)
        m2 = jnp.max(jnp.where(sub == i1, neg, cg), axis=0, keepdims=True)
        gs_rows.append(m1 + m2)
    gs = jnp.concatenate(gs_rows, axis=0)
    gi = lax.broadcasted_iota(jnp.int32, (N_GROUPS, tm), 0).astype(F32)
    gsel = jnp.zeros((N_GROUPS, tm), jnp.bool_)
    for _ in range(TOPK_GROUPS):
        m = jnp.max(gs, axis=0, keepdims=True)
        idx = jnp.min(jnp.where(gs == m, gi, float(N_GROUPS)), axis=0, keepdims=True)
        hit = gi == idx
        gsel = jnp.logical_or(gsel, hit)
        gs = jnp.where(hit, neg, gs)
    masked = jnp.concatenate(
        [jnp.where(gsel[g:g + 1, :], choice[g * GROUP_SIZE:(g + 1) * GROUP_SIZE, :], neg)
         for g in range(N_GROUPS)], axis=0)

    ei = lax.broadcasted_iota(jnp.int32, (N_EXPERTS, tm), 0).astype(F32)
    hits = []
    e_rows, w_rows = [], []
    multi = jnp.zeros((N_EXPERTS, tm), F32)
    for _ in range(TOP_K):
        m = jnp.max(masked, axis=0, keepdims=True)
        idx = jnp.min(jnp.where(masked == m, ei, float(N_EXPERTS)), axis=0, keepdims=True)
        hit = ei == idx
        hits.append(hit)
        e_rows.append(idx)
        w_rows.append(jnp.sum(jnp.where(hit, scores, 0.0), axis=0, keepdims=True))
        multi = multi + hit.astype(F32)
        masked = jnp.where(hit, neg, masked)
    wts = jnp.concatenate(w_rows, axis=0)
    wts = wts / (jnp.sum(wts, axis=0, keepdims=True) + 1e-20) * ROUTED_SCALE
    ti = lax.broadcasted_iota(jnp.int32, (tm, tm), 0)
    tj = lax.broadcasted_iota(jnp.int32, (tm, tm), 1)
    before = (ti < tj).astype(BF16)
    cum = _dot(multi.astype(BF16), before) + carry[...]
    r_rows = [jnp.sum(jnp.where(hit, cum, 0.0), axis=0, keepdims=True) for hit in hits]
    carry[...] = carry[...] + jnp.sum(multi, axis=1, keepdims=True)
    e_ref[...] = jnp.concatenate(e_rows, axis=0).astype(jnp.int32)
    w_ref[...] = wts
    rank_ref[...] = jnp.concatenate(r_rows, axis=0).astype(jnp.int32)
    cnt_ref[...] = carry[...].astype(jnp.int32)


def _router(y_lru, y_gdn, h0, wo1, wo2, g, b, w_router_t, rbias, tm):
    t = h0.shape[0]
    const = lambda shape: pl.BlockSpec(shape, lambda i: (0,) * len(shape))
    return pl.pallas_call(
        _router_kernel,
        grid=(t // tm,),
        in_specs=[
            pl.BlockSpec((tm, LRU_WIDTH), lambda i: (i, 0)),
            pl.BlockSpec((tm, GDN_V), lambda i: (i, 0)),
            pl.BlockSpec((tm, D_MODEL), lambda i: (i, 0)),
            const((LRU_WIDTH, D_MODEL)), const((GDN_V, D_MODEL)),
            const((1, D_MODEL)), const((1, D_MODEL)),
            const((N_EXPERTS, D_MODEL)), const((N_EXPERTS, 1)),
        ],
        out_specs=[
            pl.BlockSpec((tm, D_MODEL), lambda i: (i, 0)),
            pl.BlockSpec((tm, D_PACK), lambda i: (i, 0)),
            pl.BlockSpec((TOP_K, tm), lambda i: (0, i)),
            pl.BlockSpec((TOP_K, tm), lambda i: (0, i)),
            pl.BlockSpec((TOP_K, tm), lambda i: (0, i)),
            const((N_EXPERTS, 1)),
        ],
        out_shape=[
            jax.ShapeDtypeStruct((t, D_MODEL), F32),
            jax.ShapeDtypeStruct((t, D_PACK), jnp.uint32),
            jax.ShapeDtypeStruct((TOP_K, t), jnp.int32),
            jax.ShapeDtypeStruct((TOP_K, t), F32),
            jax.ShapeDtypeStruct((TOP_K, t), jnp.int32),
            jax.ShapeDtypeStruct((N_EXPERTS, 1), jnp.int32),
        ],
        scratch_shapes=[pltpu.VMEM((N_EXPERTS, 1), F32)],
        compiler_params=_params(("arbitrary",)),
        name="outproj_router",
    )(y_lru, y_gdn, h0, wo1, wo2, g, b, w_router_t, rbias)


def _dest_kernel(e_ref, r_ref, ps_ref, d_ref):
    tm = e_ref.shape[1]
    ei = lax.broadcasted_iota(jnp.int32, (N_EXPERTS, tm), 0)
    rows = []
    for k in range(TOP_K):
        hit = ei == e_ref[k:k + 1, :]
        rows.append(jnp.sum(jnp.where(hit, ps_ref[...], 0), axis=0, keepdims=True))
    d_ref[...] = jnp.concatenate(rows, axis=0) + r_ref[...]


def _dest(top_e, rank, pad_start, tm):
    t = top_e.shape[1]
    blk = pl.BlockSpec((TOP_K, tm), lambda i: (0, i))
    return pl.pallas_call(
        _dest_kernel,
        grid=(t // tm,),
        in_specs=[blk, blk, pl.BlockSpec((N_EXPERTS, 1), lambda i: (0, 0))],
        out_specs=blk,
        out_shape=jax.ShapeDtypeStruct((TOP_K, t), jnp.int32),
        compiler_params=_params(("arbitrary",)),
        name="moe_dest",
    )(top_e, rank, pad_start)


def _sc_scatter_rows(rows, idx, n_out, chunk):
    n_copies, t = idx.shape
    d = rows.shape[1]
    per_worker = t // SC_WORKERS
    n_chunks = per_worker // chunk
    mesh = plsc.VectorSubcoreMesh(core_axis_name="c", subcore_axis_name="s")
    idx_flat = idx.reshape(n_copies * t)

    @functools.partial(
        pl.kernel, mesh=mesh,
        out_type=jax.ShapeDtypeStruct((n_out, d), rows.dtype),
        scratch_types=[pltpu.VMEM((chunk,), jnp.int32) for _ in range(n_copies)] + [
            pltpu.VMEM((chunk, d), rows.dtype),
            pltpu.SemaphoreType.DMA,
        ],
    )
    def scatter(rows_hbm, idx_hbm, out_hbm, *scratch):
        idx_v = scratch[:n_copies]
        rows_v, sem = scratch[n_copies:]
        wid = lax.axis_index("s") * SC_CORES + lax.axis_index("c")
        base = wid * per_worker

        @pl.loop(0, n_chunks)
        def _(j):
            off = base + j * chunk
            for k in range(n_copies):
                pltpu.sync_copy(idx_hbm.at[pl.ds(k * t + off, chunk)], idx_v[k])
            pltpu.sync_copy(rows_hbm.at[pl.ds(off, chunk)], rows_v)
            copies = [pltpu.async_copy(rows_v, out_hbm.at[idx_v[k]], sem) for k in range(n_copies)]
            for cp in copies:
                cp.wait()

    return scatter(rows, idx_flat)


def _expert_kernel(be_ref, nv_ref, first_ref, slot_ref, next_ref, nu_ref,
                   xs_ref, wg_hbm, wu_hbm, wd_hbm, ys_ref, wg_f, wu_f, wd_f, wgu_b, wd_b, sem):
    i = pl.program_id(0)

    def fetch(e, slot):
        return (pltpu.make_async_copy(wg_hbm.at[e], wg_f.at[slot], sem.at[slot]),
                pltpu.make_async_copy(wu_hbm.at[e], wu_f.at[slot], sem.at[slot]),
                pltpu.make_async_copy(wd_hbm.at[e], wd_f.at[slot], sem.at[slot]))

    @pl.when(i < nu_ref[0])
    def _():
        e = be_ref[i]
        slot = slot_ref[i]

        @pl.when(first_ref[i] == 1)
        def _():
            @pl.when(i == 0)
            def _():
                for cp in fetch(e, slot):
                    cp.start()

            for cp in fetch(e, slot):
                cp.wait()

            @pl.when(next_ref[i] >= 0)
            def _():
                for cp in fetch(next_ref[i], 1 - slot):
                    cp.start()

            wgu_b[:, :D_EXPERT] = wg_f[slot].astype(BF16)
            wgu_b[:, D_EXPERT:] = wu_f[slot].astype(BF16)
            wd_b[...] = wd_f[slot].astype(BF16)

        row = lax.broadcasted_iota(jnp.int32, xs_ref.shape, 0)
        x_hi, x_lo = _unpack_rows(jnp.where(row < nv_ref[i], xs_ref[...], jnp.uint32(0)))
        gu = _dot(x_hi.astype(BF16), wgu_b[:D_PACK, :]) + _dot(x_lo.astype(BF16), wgu_b[D_PACK:, :])
        h = _silu(gu[:, :D_EXPERT]) * gu[:, D_EXPERT:]
        ys_ref[...] = _pack_rows(_dot(h.astype(BF16), wd_b[...]))


def _experts(blk_e, n_valid, first, slot, next_e, n_used, xs, w_gate, w_up, w_down):
    n_rows = xs.shape[0]
    n_blocks = n_rows // MOE_BLOCK
    blk = lambda i, be, nv, fi, sl, nx, nu: (jnp.minimum(i, nu[0] - 1), 0)
    return pl.pallas_call(
        _expert_kernel,
        grid_spec=pltpu.PrefetchScalarGridSpec(
            num_scalar_prefetch=6,
            grid=(n_blocks,),
            in_specs=[
                pl.BlockSpec((MOE_BLOCK, D_PACK), blk),
                pl.BlockSpec(memory_space=pl.ANY),
                pl.BlockSpec(memory_space=pl.ANY),
                pl.BlockSpec(memory_space=pl.ANY),
            ],
            out_specs=pl.BlockSpec((MOE_BLOCK, D_PACK), blk),
            scratch_shapes=[
                pltpu.VMEM((2, D_MODEL, D_EXPERT), F32),
                pltpu.VMEM((2, D_MODEL, D_EXPERT), F32),
                pltpu.VMEM((2, D_EXPERT, D_MODEL), F32),
                pltpu.VMEM((D_MODEL, 2 * D_EXPERT), BF16),
                pltpu.VMEM((D_EXPERT, D_MODEL), BF16),
                pltpu.SemaphoreType.DMA((2,)),
            ],
        ),
        out_shape=jax.ShapeDtypeStruct((n_rows, D_PACK), jnp.uint32),
        compiler_params=_params(("arbitrary",)),
        name="moe_experts",
    )(blk_e, n_valid, first, slot, next_e, n_used, xs, w_gate, w_up, w_down)


def _sc_gather_rows(table, idx, chunk):
    n_idx = idx.shape[0]
    d = table.shape[1]
    per_worker = n_idx // SC_WORKERS
    n_chunks = per_worker // chunk
    assert n_chunks % 2 == 0 and n_chunks * chunk * SC_WORKERS == n_idx
    mesh = plsc.VectorSubcoreMesh(core_axis_name="c", subcore_axis_name="s")

    @functools.partial(
        pl.kernel, mesh=mesh,
        out_type=jax.ShapeDtypeStruct((n_idx, d), table.dtype),
        scratch_types=[
            pltpu.VMEM((chunk,), jnp.int32), pltpu.VMEM((chunk,), jnp.int32),
            pltpu.VMEM((chunk, d), table.dtype), pltpu.VMEM((chunk, d), table.dtype),
            pltpu.SemaphoreType.DMA, pltpu.SemaphoreType.DMA, pltpu.SemaphoreType.DMA, pltpu.SemaphoreType.DMA,
        ],
    )
    def gather(table_hbm, idx_hbm, out_hbm, idx_v0, idx_v1, rows_v0, rows_v1, gsem0, gsem1, osem0, osem1):
        idx_v, rows_v, gsem, osem = (idx_v0, idx_v1), (rows_v0, rows_v1), (gsem0, gsem1), (osem0, osem1)
        wid = lax.axis_index("s") * SC_CORES + lax.axis_index("c")
        base = wid * per_worker

        def gather_copy(slot):
            return pltpu.make_async_copy(table_hbm.at[idx_v[slot]], rows_v[slot], gsem[slot])

        def out_copy(c, slot):
            return pltpu.make_async_copy(rows_v[slot], out_hbm.at[pl.ds(base + c * chunk, chunk)], osem[slot])

        def start_gather(c, slot):
            pltpu.sync_copy(idx_hbm.at[pl.ds(base + c * chunk, chunk)], idx_v[slot])
            gather_copy(slot).start()

        start_gather(0, 0)

        @pl.loop(0, n_chunks, step=2)
        def _(j):
            for b in range(2):
                c = j + b
                cur, other = b, 1 - b

                @pl.when(c >= 1)
                def _():
                    out_copy(c - 1, other).wait()

                @pl.when(c + 1 < n_chunks)
                def _():
                    start_gather(c + 1, other)

                gather_copy(cur).wait()
                out_copy(c, cur).start()

        out_copy(n_chunks - 1, 1).wait()

    return gather(table, idx)


def _combine_kernel(h1_ref, wts_ref, wsgu_ref, wsd_ref, g_ref, b_ref, yg_ref, out_ref):
    h1 = h1_ref[...]
    gu = _dot(h1.astype(BF16), wsgu_ref[...])
    hs = _silu(gu[:, :D_SHARED]) * gu[:, D_SHARED:]
    acc = DEEPNORM_ALPHA * h1 + _dot(hs.astype(BF16), wsd_ref[...])
    wts = wts_ref[...]
    acc_hi = acc[:, :D_PACK]
    acc_lo = acc[:, D_PACK:]
    for k in range(TOP_K):
        y_hi, y_lo = _unpack_rows(yg_ref[k])
        acc_hi = acc_hi + y_hi * wts[:, k:k + 1]
        acc_lo = acc_lo + y_lo * wts[:, k:k + 1]
    out_ref[...] = _layer_norm(jnp.concatenate([acc_hi, acc_lo], axis=1), g_ref[...], b_ref[...])


def _combine(h1, wts_t, ws_gu, ws_down, g, b, yg, tm):
    t = h1.shape[0]
    const = lambda shape: pl.BlockSpec(shape, lambda i: (0,) * len(shape))
    return pl.pallas_call(
        _combine_kernel,
        grid=(t // tm,),
        in_specs=[
            pl.BlockSpec((tm, D_MODEL), lambda i: (i, 0)),
            pl.BlockSpec((tm, TOP_K), lambda i: (i, 0)),
            const((D_MODEL, 2 * D_SHARED)), const((D_SHARED, D_MODEL)),
            const((1, D_MODEL)), const((1, D_MODEL)),
            pl.BlockSpec((TOP_K, tm, D_PACK), lambda i: (0, i, 0)),
        ],
        out_specs=pl.BlockSpec((tm, D_MODEL), lambda i: (i, 0)),
        out_shape=jax.ShapeDtypeStruct((t, D_MODEL), F32),
        compiler_params=_params(("arbitrary",)),
        name="moe_combine",
    )(h1, wts_t, ws_gu, ws_down, g, b, yg)


def _block_diag(w):
    nb, bi, bo = w.shape
    eye = jnp.eye(nb, dtype=w.dtype)
    return (eye[:, None, :, None] * w[:, :, None, :]).reshape(nb * bi, nb * bo)


def _pad_lanes(v, offset, width):
    return jnp.zeros((1, width), F32).at[0, offset:offset + v.shape[0]].set(v)


def _layer(h_in_x, l, p, tiles):
    bsz, seq, _ = h_in_x.shape
    t = bsz * seq
    row = lambda v: v.reshape(1, -1)

    w_in = p['w_in'][l]
    w_main = w_in[:, :N_MAIN].astype(BF16)
    w_small = jnp.zeros((D_MODEL, LANES), F32).at[:, :2 * GDN_HEADS].set(w_in[:, N_MAIN:])
    zeros = lambda n: jnp.zeros((CONV_WIDTH, n), F32)
    conv_w = jnp.concatenate([p['lru_conv_w'][l], zeros(LRU_WIDTH), p['gdn_conv_w'][l], zeros(GDN_V)], 1)
    conv_b = jnp.zeros((1, N_MAIN), F32).at[0, :LRU_WIDTH].set(p['lru_conv_b'][l])
    h0, proj, small, small_t = _inproj(h_in_x.reshape(t, D_MODEL), row(p['ln_g']), row(p['ln_b']),
                                       w_main, w_small, conv_w, conv_b, tiles['inproj'], seq)
    proj3 = proj.reshape(bsz, seq, N_MAIN)

    w_gates = jnp.concatenate([_block_diag(p['lru_w_rg'][l]), _block_diag(p['lru_w_ig'][l])], 1).astype(BF16)
    b_gates = jnp.concatenate([p['lru_b_rg'][l], p['lru_b_ig'][l]]).reshape(1, -1)
    y_lru = _lru(proj3, w_gates, b_gates, row(p['lru_lambda'][l]), row(p['lru_out_g'][l]), tiles['lru'])

    rows = GDN_GROUP * GDN_CHUNK
    small3 = small.reshape(bsz, seq, LANES)
    smallt3 = small_t.reshape(8, bsz, seq // rows, rows).transpose(1, 2, 0, 3)
    a_log, dt_bias = p['gdn_a_log'][l], p['gdn_dt_bias'][l]
    alr = _pad_lanes(a_log, GDN_HEADS, LANES)
    dtr = _pad_lanes(dt_bias, GDN_HEADS, LANES)
    alc = _pad_lanes(a_log, GDN_HEADS, 8).reshape(8, 1)
    dtc = _pad_lanes(dt_bias, GDN_HEADS, 8).reshape(8, 1)
    y_gdn = _gdn(proj3, small3, smallt3, alr, dtr, alc, dtc, row(p['gdn_norm_w'][l]), tiles['gdn_nb'])

    w_out = p['w_out'][l].astype(BF16)
    h1, h1p, top_e, wts, rank, counts = _router(
        y_lru.reshape(t, LRU_WIDTH), y_gdn.reshape(t, GDN_V), h0, w_out[:LRU_WIDTH], w_out[LRU_WIDTH:],
        row(p['ln1_g'][l]), row(p['ln1_b'][l]), p['w_router'][l].T, p['router_bias'][l].reshape(-1, 1),
        tiles['router'])

    counts = counts[:, 0]
    padded = (counts + MOE_BLOCK - 1) // MOE_BLOCK * MOE_BLOCK
    pad_end = jnp.cumsum(padded)
    pad_start = pad_end - padded
    n_blocks = (t * TOP_K) // MOE_BLOCK + N_EXPERTS
    n_rows = n_blocks * MOE_BLOCK
    n_used = (pad_end[-1] // MOE_BLOCK).astype(jnp.int32)
    blk_ids = jnp.minimum(jnp.arange(n_blocks, dtype=jnp.int32), n_used - 1)
    blk_e = jnp.minimum(jnp.sum(pad_end[None, :] <= (blk_ids * MOE_BLOCK)[:, None], axis=1),
                        N_EXPERTS - 1).astype(jnp.int32)

    dest = _dest(top_e, rank, pad_start.reshape(-1, 1), tiles['dest'])
    n_valid = jnp.clip(counts[blk_e] - (blk_ids * MOE_BLOCK - pad_start[blk_e]), 0, MOE_BLOCK).astype(jnp.int32)
    xs = _sc_scatter_rows(h1p, dest, n_rows, SC_CHUNK)
    active = jnp.arange(n_blocks, dtype=jnp.int32) < n_used
    first = (active & jnp.concatenate([jnp.ones((1,), bool), blk_e[1:] != blk_e[:-1]])).astype(jnp.int32)
    slot = ((jnp.cumsum(first) - 1) % 2).astype(jnp.int32)
    used = counts > 0
    later = jnp.where(used[None, :] & (jnp.arange(N_EXPERTS)[None, :] > jnp.arange(N_EXPERTS)[:, None]),
                      jnp.arange(N_EXPERTS, dtype=jnp.int32)[None, :], N_EXPERTS)
    next_used = jnp.min(later, axis=1)
    next_e = jnp.where(next_used < N_EXPERTS, next_used, -1)[blk_e].astype(jnp.int32)
    ys = _experts(blk_e, n_valid, first, slot, next_e, n_used.reshape(1), xs,
                  p['w_gate'][l], p['w_up'][l], p['w_down'][l])
    ws_gu = jnp.concatenate([p['ws_gate'][l], p['ws_up'][l]], 1).astype(BF16)
    yg = _sc_gather_rows(ys, dest.reshape(TOP_K * t), SC_CHUNK).reshape(TOP_K, t, D_PACK)
    out = _combine(h1, wts.T, ws_gu, p['ws_down'][l].astype(BF16),
                   row(p['ln2_g'][l]), row(p['ln2_b'][l]), yg, tiles['combine'])
    return out.reshape(bsz, seq, D_MODEL)


def _tiles(bsz, seq):
    t = bsz * seq
    return {
        'inproj': min(512, t),
        'lru': min(256, seq),
        'gdn_nb': bsz,
        'router': min(256, t),
        'dest': min(512, t),
        'combine': min(256, t),
    }


def kernel(x, ln_in_g, ln_in_b, w_in, lru_conv_w, lru_conv_b, lru_w_rg, lru_b_rg, lru_w_ig, lru_b_ig,
           lru_lambda, lru_out_g, gdn_conv_w, gdn_a_log, gdn_dt_bias, gdn_norm_w, w_out, ln1_g, ln1_b,
           w_router, router_bias, w_gate, w_up, w_down, ws_gate, ws_up, ws_down, ln2_g, ln2_b):
    assert w_in.shape[0] == DEPTH == 1
    p = dict(ln_g=ln_in_g, ln_b=ln_in_b, w_in=w_in, lru_conv_w=lru_conv_w, lru_conv_b=lru_conv_b,
             lru_w_rg=lru_w_rg, lru_b_rg=lru_b_rg, lru_w_ig=lru_w_ig, lru_b_ig=lru_b_ig,
             lru_lambda=lru_lambda, lru_out_g=lru_out_g, gdn_conv_w=gdn_conv_w, gdn_a_log=gdn_a_log,
             gdn_dt_bias=gdn_dt_bias, gdn_norm_w=gdn_norm_w, w_out=w_out, ln1_g=ln1_g, ln1_b=ln1_b,
             w_router=w_router, router_bias=router_bias, w_gate=w_gate, w_up=w_up, w_down=w_down,
             ws_gate=ws_gate, ws_up=ws_up, ws_down=ws_down, ln2_g=ln2_g, ln2_b=ln2_b)
    bsz, seq, _ = x.shape
    return _layer(x, 0, p, _tiles(bsz, seq))
```

```python
import functools

import jax
import jax.numpy as jnp
from jax import lax
from jax.experimental import pallas as pl
from jax.experimental.pallas import tpu as pltpu
from jax.experimental.pallas import tpu_sc as plsc

F32 = jnp.float32
BF16 = jnp.bfloat16

D_MODEL = 1024
LRU_WIDTH = 512
LRU_BLOCKS = 8
LRU_C = 8.0
CONV_WIDTH = 4
GDN_HEADS = 4
GDN_DK = 128
GDN_DV = 128
GDN_CHUNK = 64
GDN_GROUP = 2
GDN_QK = GDN_HEADS * GDN_DK
GDN_V = GDN_HEADS * GDN_DV
N_MAIN = 2 * LRU_WIDTH + 2 * GDN_QK + 2 * GDN_V
N_EXPERTS = 256
TOP_K = 8
N_GROUPS = 8
GROUP_SIZE = N_EXPERTS // N_GROUPS
TOPK_GROUPS = 4
D_EXPERT = 256
D_SHARED = 256
ROUTED_SCALE = 2.5
MOE_BLOCK = 256
D_PACK = D_MODEL // 2
LN_EPS = 1e-5
NORM_EPS = 1e-6
DEPTH = 1
DEEPNORM_ALPHA = (2.0 * DEPTH) ** 0.25

HALO = 8
CONV_GROUP = 512
CONV_GROUPS = (0, 2, 3, 4)
LANES = 128
VMEM_LIMIT = 56 * 1024 * 1024
ROUTER_SUB = 256
SC_CORES = 2
SC_WORKERS = 32
SC_CHUNK = 64

NN = (((1,), (0,)), ((), ()))
NT = (((1,), (1,)), ((), ()))
TN = (((0,), (0,)), ((), ()))


def _dot(a, b, dims=NN):
    return lax.dot_general(a, b, dims, preferred_element_type=F32)


def _split(a):
    hi = a.astype(BF16)
    lo = (a - hi.astype(F32)).astype(BF16)
    return hi, lo


def _dot3(a, b, dims=NN):
    ah, al = _split(a)
    bh, bl = _split(b)
    return _dot(ah, bh, dims) + (_dot(ah, bl, dims) + _dot(al, bh, dims))


def _layer_norm(x, g, b):
    mu = jnp.mean(x, -1, keepdims=True)
    xc = x - mu
    var = jnp.mean(xc * xc, -1, keepdims=True)
    return xc * lax.rsqrt(var + LN_EPS) * g + b


def _sigmoid(x):
    return 1.0 / (1.0 + jnp.exp(-x))


def _silu(x):
    return x * _sigmoid(x)


def _softplus(x):
    return jnp.maximum(x, 0.0) + jnp.log1p(jnp.exp(-jnp.abs(x)))


def _gelu_tanh(x):
    c = 0.7978845608028654
    return x * (0.5 * (1.0 + jnp.tanh(c * (x + 0.044715 * (x * x * x)))))


def _pack_rows(x):
    hi = lax.bitcast_convert_type(x[:, :D_PACK].astype(BF16).astype(F32), jnp.uint32)
    lo = lax.bitcast_convert_type(x[:, D_PACK:].astype(BF16).astype(F32), jnp.uint32)
    return (hi & jnp.uint32(0xFFFF0000)) | (lo >> 16)


def _unpack_rows(w):
    hi = lax.bitcast_convert_type(w & jnp.uint32(0xFFFF0000), F32)
    lo = lax.bitcast_convert_type(w << 16, F32)
    return hi, lo


def _params(sem, **kw):
    return pltpu.CompilerParams(dimension_semantics=sem, vmem_limit_bytes=VMEM_LIMIT, **kw)


def _inproj_kernel(x_ref, g_ref, b_ref, w_ref, ws_ref, cw_ref, cb_ref,
                   h_ref, proj_ref, small_ref, smallt_ref, hist, *, tiles_per_seq):
    i = pl.program_id(0)
    tm = x_ref.shape[0]
    h = _layer_norm(x_ref[...], g_ref[...], b_ref[...])
    h_ref[...] = h
    hb = h.astype(BF16)

    @pl.when(i % tiles_per_seq == 0)
    def _():
        hist[...] = jnp.zeros_like(hist)

    for g in range(N_MAIN // CONV_GROUP):
        cols = slice(g * CONV_GROUP, (g + 1) * CONV_GROUP)
        p = _dot(hb, w_ref[:, cols])
        if g in CONV_GROUPS:
            xcat = jnp.concatenate([hist[:, cols], p], axis=0)
            acc = cb_ref[:, cols]
            for j in range(CONV_WIDTH):
                off = HALO - (CONV_WIDTH - 1) + j
                acc = acc + xcat[off:off + tm, :] * cw_ref[j:j + 1, cols]
            hist[:, cols] = p[tm - HALO:, :]
            p = acc
        proj_ref[:, cols] = p
    small = _dot3(h, ws_ref[...])
    small_ref[...] = small
    smallt_ref[...] = small.T[:smallt_ref.shape[0], :]


def _inproj(x2d, g, b, w_main, w_small, conv_w, conv_b, tm, seq):
    t = x2d.shape[0]
    return pl.pallas_call(
        functools.partial(_inproj_kernel, tiles_per_seq=seq // tm),
        grid=(t // tm,),
        in_specs=[
            pl.BlockSpec((tm, D_MODEL), lambda i: (i, 0)),
            pl.BlockSpec((1, D_MODEL), lambda i: (0, 0)),
            pl.BlockSpec((1, D_MODEL), lambda i: (0, 0)),
            pl.BlockSpec((D_MODEL, N_MAIN), lambda i: (0, 0)),
            pl.BlockSpec((D_MODEL, LANES), lambda i: (0, 0)),
            pl.BlockSpec((CONV_WIDTH, N_MAIN), lambda i: (0, 0)),
            pl.BlockSpec((1, N_MAIN), lambda i: (0, 0)),
        ],
        out_specs=[
            pl.BlockSpec((tm, D_MODEL), lambda i: (i, 0)),
            pl.BlockSpec((tm, N_MAIN), lambda i: (i, 0)),
            pl.BlockSpec((tm, LANES), lambda i: (i, 0)),
            pl.BlockSpec((8, tm), lambda i: (0, i)),
        ],
        out_shape=[
            jax.ShapeDtypeStruct((t, D_MODEL), F32),
            jax.ShapeDtypeStruct((t, N_MAIN), F32),
            jax.ShapeDtypeStruct((t, LANES), F32),
            jax.ShapeDtypeStruct((8, t), F32),
        ],
        scratch_shapes=[pltpu.VMEM((HALO, N_MAIN), F32)],
        compiler_params=_params(("arbitrary",)),
        name="ln_inproj",
    )(x2d, g, b, w_main, w_small, conv_w, conv_b)


def _shift_rows(x, d, fill):
    rows = x.shape[0]
    if d % 8 == 0:
        pad = jnp.full((d, x.shape[1]), fill, x.dtype)
        return jnp.concatenate([pad, x[:rows - d]], axis=0)
    rolled = pltpu.roll(x, d, 0)
    row = lax.broadcasted_iota(jnp.int32, x.shape, 0)
    return jnp.where(row < d, fill, rolled)


def _lru_kernel(xc_ref, gate_ref, wg_ref, bg_ref, lam_ref, og_ref, y_ref, hcarry):
    s = pl.program_id(1)
    rows = xc_ref.shape[0]

    @pl.when(s == 0)
    def _():
        hcarry[...] = jnp.zeros_like(hcarry)

    xc = xc_ref[...]
    gates = _dot(xc.astype(BF16), wg_ref[...]) + bg_ref[...]
    r = _sigmoid(gates[:, :LRU_WIDTH])
    i = _sigmoid(gates[:, LRU_WIDTH:])
    log_a = (-LRU_C) * r * _softplus(-lam_ref[...])
    a = jnp.exp(log_a)
    mult = jnp.sqrt(-jnp.tanh(log_a) * (a * a + 1.0))
    bv = mult * (i * xc)
    d = 1
    while d < rows:
        a_sh = _shift_rows(a, d, 1.0)
        b_sh = _shift_rows(bv, d, 0.0)
        bv = a * b_sh + bv
        a = a * a_sh
        d *= 2
    h = a * hcarry[...] + bv
    hcarry[...] = h[rows - 1:rows, :]
    y = h * _gelu_tanh(gate_ref[...])
    ms = jnp.mean(y * y, -1, keepdims=True)
    y_ref[...] = y * lax.rsqrt(ms + NORM_EPS) * og_ref[...]


def _lru(proj3, w_gates, b_gates, lam, out_g, ts):
    bsz, seq, _ = proj3.shape
    row = lambda n: pl.BlockSpec((1, n), lambda b, s: (0, 0))
    return pl.pallas_call(
        _lru_kernel,
        grid=(bsz, seq // ts),
        in_specs=[
            pl.BlockSpec((None, ts, LRU_WIDTH), lambda b, s: (b, s, 0)),
            pl.BlockSpec((None, ts, LRU_WIDTH), lambda b, s: (b, s, 1)),
            pl.BlockSpec((LRU_WIDTH, 2 * LRU_WIDTH), lambda b, s: (0, 0)),
            row(2 * LRU_WIDTH),
            row(LRU_WIDTH),
            row(LRU_WIDTH),
        ],
        out_specs=pl.BlockSpec((None, ts, LRU_WIDTH), lambda b, s: (b, s, 0)),
        out_shape=jax.ShapeDtypeStruct((bsz, seq, LRU_WIDTH), F32),
        scratch_shapes=[
            pltpu.VMEM((1, LRU_WIDTH), F32),
        ],
        compiler_params=_params(("arbitrary", "arbitrary")),
        name="rg_lru",
    )(proj3, proj3, w_gates, b_gates, lam, out_g)


def _bdot(a, b, dims=NN):
    return _dot(a.astype(BF16), b.astype(BF16), dims)


def _gdn_heads(args, norm_w):
    c = GDN_CHUNK
    r = GDN_GROUP * c
    ri = lax.broadcasted_iota(jnp.int32, (r, r), 0)
    ci = lax.broadcasted_iota(jnp.int32, (r, r), 1)
    same = (ri // c) == (ci // c)
    causal = same & (ri >= ci)
    strict = same & (ri > ci)
    upper = same & (ri <= ci)
    chunk_of_row = lax.broadcasted_iota(jnp.int32, (r, 1), 0) // c
    each = lambda f, *ls: [f(*xs) for xs in zip(*ls)]
    q, k, v, z, beta, g_col, g_row, st = [list(x) for x in zip(*args)]
    q = each(lambda x: x * lax.rsqrt(jnp.sum(x * x, -1, keepdims=True) + NORM_EPS) * (GDN_DK ** -0.5), q)
    k = each(lambda x: x * lax.rsqrt(jnp.sum(x * x, -1, keepdims=True) + NORM_EPS), k)
    gc_col = each(lambda g: jnp.sum(jnp.where(causal, g, 0.0), axis=1, keepdims=True), g_row)
    gc_row = each(lambda g: jnp.sum(jnp.where(upper, g, 0.0), axis=0, keepdims=True), g_col)
    decay = each(lambda gc, gr: jnp.exp(jnp.where(causal, gc - gr, -jnp.inf)), gc_col, gc_row)
    kb = each(lambda x, bt: x * bt, k, beta)
    vb = each(lambda x, bt: x * bt, v, beta)
    kk = each(lambda x, y: _bdot(x, y, NT), kb, k)
    a_mat = each(lambda m, d: jnp.where(strict, m * d, 0.0), kk, decay)
    e_col = each(jnp.exp, gc_col)
    rhs = each(lambda x, y, e: jnp.concatenate([x, y * e], axis=1), vb, kb, e_col)
    sol = each(lambda rr, a: rr - _bdot(a, rr), rhs, a_mat)
    p = a_mat
    for _ in range(5):
        p = each(lambda x: _bdot(x, x), p)
        sol = each(lambda x, y: y + _bdot(x, y), p, sol)
    qk = each(lambda x, y: _bdot(x, y, NT), q, k)
    qk = each(lambda m, d: jnp.where(causal, m * d, 0.0), qk, decay)
    q_dec = each(lambda x, e: x * e, q, e_col)
    g_last = [each(lambda gc: gc[(j + 1) * c - 1:(j + 1) * c, :], gc_col) for j in range(GDN_GROUP)]

    def last_of_own_chunk(*gl):
        out = gl[-1]
        for j in range(GDN_GROUP - 2, -1, -1):
            out = jnp.where(chunk_of_row == j, gl[j], out)
        return out

    g_end = each(last_of_own_chunk, *g_last)
    k_dec = each(lambda x, ge, gc: x * jnp.exp(ge - gc), k, g_end, gc_col)
    qs_parts, v_parts = [], []
    for j in range(GDN_GROUP):
        rows = slice(j * c, (j + 1) * c)
        ws = each(lambda x, s: _bdot(x[rows, GDN_DV:], s), sol, st)
        qs_parts.append(each(lambda x, s: _bdot(x[rows], s), q_dec, st))
        v_new = each(lambda x, w: x[rows, :GDN_DV] - w, sol, ws)
        v_parts.append(v_new)
        kv = each(lambda x, vn: _bdot(x[rows], vn, TN), k_dec, v_new)
        st = each(lambda s, gl, d: s * jnp.exp(gl) + d, st, g_last[j], kv)
    qs = each(lambda *parts: jnp.concatenate(parts, axis=0), *qs_parts)
    v_all = each(lambda *parts: jnp.concatenate(parts, axis=0), *v_parts)
    o = each(lambda a, m, vn: a + _bdot(m, vn), qs, qk, v_all)
    o = each(lambda x: x * lax.rsqrt(jnp.mean(x * x, -1, keepdims=True) + NORM_EPS) * norm_w, o)
    o = each(lambda x, zz: x * _silu(zz), o, z)
    return list(zip(o, st))


def _gdn_kernel(q_ref, k_ref, v_ref, z_ref, sm_ref, smt_ref,
                alr_ref, dtr_ref, alc_ref, dtc_ref, nw_ref, y_ref, state):
    n = pl.program_id(1)
    c = GDN_GROUP * GDN_CHUNK
    nb = q_ref.shape[0]
    first = n == 0

    @pl.when(first)
    def _():
        state[...] = jnp.zeros_like(state)

    norm_w = nw_ref[...]

    args = []
    for b in range(nb):
        q_all = _silu(q_ref[b])
        k_all = _silu(k_ref[b])
        v_all = _silu(v_ref[b])
        z_all = z_ref[b]
        sm = sm_ref[b]
        beta_all = _sigmoid(sm)
        g_cols = -jnp.exp(alr_ref[...]) * _softplus(sm + dtr_ref[...])
        g_rows = -jnp.exp(alc_ref[...]) * _softplus(smt_ref[b] + dtc_ref[...])
        for hd in range(GDN_HEADS):
            sl = slice(hd * GDN_DK, (hd + 1) * GDN_DK)
            args.append((q_all[:, sl], k_all[:, sl], v_all[:, sl], z_all[:, sl],
                         beta_all[:, hd:hd + 1],
                         g_cols[:, GDN_HEADS + hd:GDN_HEADS + hd + 1],
                         g_rows[GDN_HEADS + hd:GDN_HEADS + hd + 1, :],
                         state[b, hd]))
    outs = _gdn_heads(args, norm_w)
    for b in range(nb):
        for hd in range(GDN_HEADS):
            o, st_new = outs[b * GDN_HEADS + hd]
            state[b, hd] = st_new
            y_ref[b, :, hd * GDN_DK:(hd + 1) * GDN_DK] = o


def _gdn(proj3, small3, smallt3, alr, dtr, alc, dtc, norm_w, nb):
    bsz, seq, _ = proj3.shape
    c = GDN_GROUP * GDN_CHUNK
    nch = seq // c
    col = lambda j: pl.BlockSpec((nb, c, GDN_QK), lambda b, n: (b, n, j))
    const = lambda shape: pl.BlockSpec(shape, lambda b, n: (0,) * len(shape))
    return pl.pallas_call(
        _gdn_kernel,
        grid=(bsz // nb, nch),
        in_specs=[
            col(2), col(3), col(4), col(5),
            pl.BlockSpec((nb, c, LANES), lambda b, n: (b, n, 0)),
            pl.BlockSpec((nb, None, 8, c), lambda b, n: (b, n, 0, 0)),
            const((1, LANES)), const((1, LANES)), const((8, 1)), const((8, 1)),
            const((1, GDN_DV)),
        ],
        out_specs=pl.BlockSpec((nb, c, GDN_V), lambda b, n: (b, n, 0)),
        out_shape=jax.ShapeDtypeStruct((bsz, seq, GDN_V), F32),
        scratch_shapes=[
            pltpu.VMEM((nb, GDN_HEADS, GDN_DK, GDN_DV), F32),
        ],
        compiler_params=_params(("arbitrary", "arbitrary")),
        name="gated_deltanet",
    )(proj3, proj3, proj3, proj3, small3, smallt3, alr, dtr, alc, dtc, norm_w)


def _pick_experts(logits, rbias):
    n = logits.shape[1]
    scores = _sigmoid(logits)
    choice = scores + rbias
    neg = -jnp.inf
    gs_rows = []
    sub = lax.broadcasted_iota(jnp.int32, (GROUP_SIZE, n), 0).astype(F32)
    for g in range(N_GROUPS):
        cg = choice[g * GROUP_SIZE:(g + 1) * GROUP_SIZE, :]
        m1 = jnp.max(cg, axis=0, keepdims=True)
        i1 = jnp.min(jnp.where(cg == m1, sub, float(GROUP_SIZE)), axis=0, keepdims=True)
        m2 = jnp.max(jnp.where(sub == i1, neg, cg), axis=0, keepdims=True)
        gs_rows.append(m1 + m2)
    gs = jnp.concatenate(gs_rows, axis=0)
    gi = lax.broadcasted_iota(jnp.int32, (N_GROUPS, n), 0).astype(F32)
    gsel = jnp.zeros((N_GROUPS, n), jnp.bool_)
    for _ in range(TOPK_GROUPS):
        m = jnp.max(gs, axis=0, keepdims=True)
        idx = jnp.min(jnp.where(gs == m, gi, float(N_GROUPS)), axis=0, keepdims=True)
        hit = gi == idx
        gsel = jnp.logical_or(gsel, hit)
        gs = jnp.where(hit, neg, gs)
    masked = jnp.concatenate(
        [jnp.where(gsel[g:g + 1, :], choice[g * GROUP_SIZE:(g + 1) * GROUP_SIZE, :], neg)
         for g in range(N_GROUPS)], axis=0)
    ei = lax.broadcasted_iota(jnp.int32, (N_EXPERTS, n), 0).astype(F32)
    hits, e_rows, w_rows = [], [], []
    multi = jnp.zeros((N_EXPERTS, n), F32)
    for _ in range(TOP_K):
        m = jnp.max(masked, axis=0, keepdims=True)
        idx = jnp.min(jnp.where(masked == m, ei, float(N_EXPERTS)), axis=0, keepdims=True)
        hit = ei == idx
        hits.append(hit)
        e_rows.append(idx)
        w_rows.append(jnp.sum(jnp.where(hit, scores, 0.0), axis=0, keepdims=True))
        multi = multi + hit.astype(F32)
        masked = jnp.where(hit, neg, masked)
    wts = jnp.concatenate(w_rows, axis=0)
    wts = wts / (jnp.sum(wts, axis=0, keepdims=True) + 1e-20) * ROUTED_SCALE
    return jnp.concatenate(e_rows, axis=0), wts, hits, multi


def _router_kernel(yl_ref, yg_ref, h0_ref, wo1_ref, wo2_ref, g_ref, b_ref, wrt_ref, rb_ref,
                   h1_ref, h1p_ref, e_ref, w_ref, rank_ref, cnt_ref, carry):
    i = pl.program_id(0)
    tm = h0_ref.shape[0]
    n = min(ROUTER_SUB, tm)
    subs = [slice(j * n, (j + 1) * n) for j in range(tm // n)]

    @pl.when(i == 0)
    def _():
        carry[...] = jnp.zeros_like(carry)

    mixes = [_dot(yl_ref[r, :].astype(BF16), wo1_ref[...]) + _dot(yg_ref[r, :].astype(BF16), wo2_ref[...])
             for r in subs]
    h1s = [_layer_norm(DEEPNORM_ALPHA * h0_ref[r, :] + mix, g_ref[...], b_ref[...])
           for r, mix in zip(subs, mixes)]
    for r, h1 in zip(subs, h1s):
        h1_ref[r, :] = h1
        h1p_ref[r, :] = _pack_rows(h1)
    logits = [_dot3(wrt_ref[...], h1, NT) for h1 in h1s]
    picks = [_pick_experts(lg, rb_ref[...]) for lg in logits]
    ti = lax.broadcasted_iota(jnp.int32, (n, n), 0)
    tj = lax.broadcasted_iota(jnp.int32, (n, n), 1)
    before = (ti < tj).astype(BF16)
    cums = [_dot(multi.astype(BF16), before) for _, _, _, multi in picks]
    base = carry[...]
    for r, (e_rows, wts, hits, multi), cum in zip(subs, picks, cums):
        cum = cum + base
        r_rows = [jnp.sum(jnp.where(hit, cum, 0.0), axis=0, keepdims=True) for hit in hits]
        base = base + jnp.sum(multi, axis=1, keepdims=True)
        e_ref[:, r] = e_rows.astype(jnp.int32)
        w_ref[:, r] = wts
        rank_ref[:, r] = jnp.concatenate(r_rows, axis=0).astype(jnp.int32)
    carry[...] = base
    cnt_ref[...] = base.astype(jnp.int32)


def _router(y_lru, y_gdn, h0, wo1, wo2, g, b, w_router_t, rbias, tm):
    t = h0.shape[0]
    const = lambda shape: pl.BlockSpec(shape, lambda i: (0,) * len(shape))
    return pl.pallas_call(
        _router_kernel,
        grid=(t // tm,),
        in_specs=[
            pl.BlockSpec((tm, LRU_WIDTH), lambda i: (i, 0)),
            pl.BlockSpec((tm, GDN_V), lambda i: (i, 0)),
            pl.BlockSpec((tm, D_MODEL), lambda i: (i, 0)),
            const((LRU_WIDTH, D_MODEL)), const((GDN_V, D_MODEL)),
            const((1, D_MODEL)), const((1, D_MODEL)),
            const((N_EXPERTS, D_MODEL)), const((N_EXPERTS, 1)),
        ],
        out_specs=[
            pl.BlockSpec((tm, D_MODEL), lambda i: (i, 0)),
            pl.BlockSpec((tm, D_PACK), lambda i: (i, 0)),
            pl.BlockSpec((TOP_K, tm), lambda i: (0, i)),
            pl.BlockSpec((TOP_K, tm), lambda i: (0, i)),
            pl.BlockSpec((TOP_K, tm), lambda i: (0, i)),
            const((N_EXPERTS, 1)),
        ],
        out_shape=[
            jax.ShapeDtypeStruct((t, D_MODEL), F32),
            jax.ShapeDtypeStruct((t, D_PACK), jnp.uint32),
            jax.ShapeDtypeStruct((TOP_K, t), jnp.int32),
            jax.ShapeDtypeStruct((TOP_K, t), F32),
            jax.ShapeDtypeStruct((TOP_K, t), jnp.int32),
            jax.ShapeDtypeStruct((N_EXPERTS, 1), jnp.int32),
        ],
        scratch_shapes=[pltpu.VMEM((N_EXPERTS, 1), F32)],
        compiler_params=_params(("arbitrary",)),
        name="outproj_router",
    )(y_lru, y_gdn, h0, wo1, wo2, g, b, w_router_t, rbias)


def _dest_kernel(e_ref, r_ref, ps_ref, d_ref):
    tm = e_ref.shape[1]
    ei = lax.broadcasted_iota(jnp.int32, (N_EXPERTS, tm), 0)
    rows = []
    for k in range(TOP_K):
        hit = ei == e_ref[k:k + 1, :]
        rows.append(jnp.sum(jnp.where(hit, ps_ref[...], 0), axis=0, keepdims=True))
    d_ref[...] = jnp.concatenate(rows, axis=0) + r_ref[...]


def _dest(top_e, rank, pad_start, tm):
    t = top_e.shape[1]
    blk = pl.BlockSpec((TOP_K, tm), lambda i: (0, i))
    return pl.pallas_call(
        _dest_kernel,
        grid=(t // tm,),
        in_specs=[blk, blk, pl.BlockSpec((N_EXPERTS, 1), lambda i: (0, 0))],
        out_specs=blk,
        out_shape=jax.ShapeDtypeStruct((TOP_K, t), jnp.int32),
        compiler_params=_params(("arbitrary",)),
        name="moe_dest",
    )(top_e, rank, pad_start)


def _sc_scatter_rows(rows, idx, n_out, chunk):
    n_copies, t = idx.shape
    d = rows.shape[1]
    per_worker = t // SC_WORKERS
    n_chunks = per_worker // chunk
    mesh = plsc.VectorSubcoreMesh(core_axis_name="c", subcore_axis_name="s")
    idx_flat = idx.reshape(n_copies * t)

    @functools.partial(
        pl.kernel, mesh=mesh,
        out_type=jax.ShapeDtypeStruct((n_out, d), rows.dtype),
        scratch_types=[pltpu.VMEM((chunk,), jnp.int32) for _ in range(n_copies)] + [
            pltpu.VMEM((chunk, d), rows.dtype),
            pltpu.SemaphoreType.DMA,
        ],
    )
    def scatter(rows_hbm, idx_hbm, out_hbm, *scratch):
        idx_v = scratch[:n_copies]
        rows_v, sem = scratch[n_copies:]
        wid = lax.axis_index("s") * SC_CORES + lax.axis_index("c")
        base = wid * per_worker

        @pl.loop(0, n_chunks)
        def _(j):
            off = base + j * chunk
            for k in range(n_copies):
                pltpu.sync_copy(idx_hbm.at[pl.ds(k * t + off, chunk)], idx_v[k])
            pltpu.sync_copy(rows_hbm.at[pl.ds(off, chunk)], rows_v)
            copies = [pltpu.async_copy(rows_v, out_hbm.at[idx_v[k]], sem) for k in range(n_copies)]
            for cp in copies:
                cp.wait()

    return scatter(rows, idx_flat)


def _expert_kernel(be_ref, nv_ref, first_ref, slot_ref, next_ref, nu_ref,
                   xs_ref, wg_hbm, wu_hbm, wd_hbm, ys_ref, wg_f, wu_f, wd_f, wgu_b, wd_b, sem):
    i = pl.program_id(0)

    def fetch(e, slot):
        return (pltpu.make_async_copy(wg_hbm.at[e], wg_f.at[slot], sem.at[slot]),
                pltpu.make_async_copy(wu_hbm.at[e], wu_f.at[slot], sem.at[slot]),
                pltpu.make_async_copy(wd_hbm.at[e], wd_f.at[slot], sem.at[slot]))

    @pl.when(i < nu_ref[0])
    def _():
        e = be_ref[i]
        slot = slot_ref[i]

        @pl.when(first_ref[i] == 1)
        def _():
            @pl.when(i == 0)
            def _():
                for cp in fetch(e, slot):
                    cp.start()

            for cp in fetch(e, slot):
                cp.wait()

            @pl.when(next_ref[i] >= 0)
            def _():
                for cp in fetch(next_ref[i], 1 - slot):
                    cp.start()

            wgu_b[:, :D_EXPERT] = wg_f[slot].astype(BF16)
            wgu_b[:, D_EXPERT:] = wu_f[slot].astype(BF16)
            wd_b[...] = wd_f[slot].astype(BF16)

        row = lax.broadcasted_iota(jnp.int32, xs_ref.shape, 0)
        x_hi, x_lo = _unpack_rows(jnp.where(row < nv_ref[i], xs_ref[...], jnp.uint32(0)))
        gu = _dot(x_hi.astype(BF16), wgu_b[:D_PACK, :]) + _dot(x_lo.astype(BF16), wgu_b[D_PACK:, :])
        h = _silu(gu[:, :D_EXPERT]) * gu[:, D_EXPERT:]
        ys_ref[...] = _pack_rows(_dot(h.astype(BF16), wd_b[...]))


def _experts(blk_e, n_valid, first, slot, next_e, n_used, xs, w_gate, w_up, w_down):
    n_rows = xs.shape[0]
    n_blocks = n_rows // MOE_BLOCK
    blk = lambda i, be, nv, fi, sl, nx, nu: (jnp.minimum(i, nu[0] - 1), 0)
    return pl.pallas_call(
        _expert_kernel,
        grid_spec=pltpu.PrefetchScalarGridSpec(
            num_scalar_prefetch=6,
            grid=(n_blocks,),
            in_specs=[
                pl.BlockSpec((MOE_BLOCK, D_PACK), blk),
                pl.BlockSpec(memory_space=pl.ANY),
                pl.BlockSpec(memory_space=pl.ANY),
                pl.BlockSpec(memory_space=pl.ANY),
            ],
            out_specs=pl.BlockSpec((MOE_BLOCK, D_PACK), blk),
            scratch_shapes=[
                pltpu.VMEM((2, D_MODEL, D_EXPERT), F32),
                pltpu.VMEM((2, D_MODEL, D_EXPERT), F32),
                pltpu.VMEM((2, D_EXPERT, D_MODEL), F32),
                pltpu.VMEM((D_MODEL, 2 * D_EXPERT), BF16),
                pltpu.VMEM((D_EXPERT, D_MODEL), BF16),
                pltpu.SemaphoreType.DMA((2,)),
            ],
        ),
        out_shape=jax.ShapeDtypeStruct((n_rows, D_PACK), jnp.uint32),
        compiler_params=_params(("arbitrary",)),
        name="moe_experts",
    )(blk_e, n_valid, first, slot, next_e, n_used, xs, w_gate, w_up, w_down)


def _sc_gather_rows(table, idx, chunk):
    n_idx = idx.shape[0]
    d = table.shape[1]
    per_worker = n_idx // SC_WORKERS
    n_chunks = per_worker // chunk
    assert n_chunks % 2 == 0 and n_chunks * chunk * SC_WORKERS == n_idx
    mesh = plsc.VectorSubcoreMesh(core_axis_name="c", subcore_axis_name="s")

    @functools.partial(
        pl.kernel, mesh=mesh,
        out_type=jax.ShapeDtypeStruct((n_idx, d), table.dtype),
        scratch_types=[
            pltpu.VMEM((chunk,), jnp.int32), pltpu.VMEM((chunk,), jnp.int32),
            pltpu.VMEM((chunk, d), table.dtype), pltpu.VMEM((chunk, d), table.dtype),
            pltpu.SemaphoreType.DMA, pltpu.SemaphoreType.DMA, pltpu.SemaphoreType.DMA, pltpu.SemaphoreType.DMA,
        ],
    )
    def gather(table_hbm, idx_hbm, out_hbm, idx_v0, idx_v1, rows_v0, rows_v1, gsem0, gsem1, osem0, osem1):
        idx_v, rows_v, gsem, osem = (idx_v0, idx_v1), (rows_v0, rows_v1), (gsem0, gsem1), (osem0, osem1)
        wid = lax.axis_index("s") * SC_CORES + lax.axis_index("c")
        base = wid * per_worker

        def gather_copy(slot):
            return pltpu.make_async_copy(table_hbm.at[idx_v[slot]], rows_v[slot], gsem[slot])

        def out_copy(c, slot):
            return pltpu.make_async_copy(rows_v[slot], out_hbm.at[pl.ds(base + c * chunk, chunk)], osem[slot])

        def start_gather(c, slot):
            pltpu.sync_copy(idx_hbm.at[pl.ds(base + c * chunk, chunk)], idx_v[slot])
            gather_copy(slot).start()

        start_gather(0, 0)

        @pl.loop(0, n_chunks, step=2)
        def _(j):
            for b in range(2):
                c = j + b
                cur, other = b, 1 - b

                @pl.when(c >= 1)
                def _():
                    out_copy(c - 1, other).wait()

                @pl.when(c + 1 < n_chunks)
                def _():
                    start_gather(c + 1, other)

                gather_copy(cur).wait()
                out_copy(c, cur).start()

        out_copy(n_chunks - 1, 1).wait()

    return gather(table, idx)


def _combine_kernel(h1_ref, wts_ref, wsgu_ref, wsd_ref, g_ref, b_ref, yg_ref, out_ref):
    h1 = h1_ref[...]
    gu = _dot(h1.astype(BF16), wsgu_ref[...])
    hs = _silu(gu[:, :D_SHARED]) * gu[:, D_SHARED:]
    acc = DEEPNORM_ALPHA * h1 + _dot(hs.astype(BF16), wsd_ref[...])
    wts = wts_ref[...]
    acc_hi = acc[:, :D_PACK]
    acc_lo = acc[:, D_PACK:]
    for k in range(TOP_K):
        y_hi, y_lo = _unpack_rows(yg_ref[k])
        acc_hi = acc_hi + y_hi * wts[:, k:k + 1]
        acc_lo = acc_lo + y_lo * wts[:, k:k + 1]
    out_ref[...] = _layer_norm(jnp.concatenate([acc_hi, acc_lo], axis=1), g_ref[...], b_ref[...])


def _combine(h1, wts_t, ws_gu, ws_down, g, b, yg, tm):
    t = h1.shape[0]
    const = lambda shape: pl.BlockSpec(shape, lambda i: (0,) * len(shape))
    return pl.pallas_call(
        _combine_kernel,
        grid=(t // tm,),
        in_specs=[
            pl.BlockSpec((tm, D_MODEL), lambda i: (i, 0)),
            pl.BlockSpec((tm, TOP_K), lambda i: (i, 0)),
            const((D_MODEL, 2 * D_SHARED)), const((D_SHARED, D_MODEL)),
            const((1, D_MODEL)), const((1, D_MODEL)),
            pl.BlockSpec((TOP_K, tm, D_PACK), lambda i: (0, i, 0)),
        ],
        out_specs=pl.BlockSpec((tm, D_MODEL), lambda i: (i, 0)),
        out_shape=jax.ShapeDtypeStruct((t, D_MODEL), F32),
        compiler_params=_params(("arbitrary",)),
        name="moe_combine",
    )(h1, wts_t, ws_gu, ws_down, g, b, yg)


def _block_diag(w):
    nb, bi, bo = w.shape
    eye = jnp.eye(nb, dtype=w.dtype)
    return (eye[:, None, :, None] * w[:, :, None, :]).reshape(nb * bi, nb * bo)


def _pad_lanes(v, offset, width):
    return jnp.zeros((1, width), F32).at[0, offset:offset + v.shape[0]].set(v)


def _layer(h_in_x, l, p, tiles):
    bsz, seq, _ = h_in_x.shape
    t = bsz * seq
    row = lambda v: v.reshape(1, -1)

    w_in = p['w_in'][l]
    w_main = w_in[:, :N_MAIN].astype(BF16)
    w_small = jnp.zeros((D_MODEL, LANES), F32).at[:, :2 * GDN_HEADS].set(w_in[:, N_MAIN:])
    zeros = lambda n: jnp.zeros((CONV_WIDTH, n), F32)
    conv_w = jnp.concatenate([p['lru_conv_w'][l], zeros(LRU_WIDTH), p['gdn_conv_w'][l], zeros(GDN_V)], 1)
    conv_b = jnp.zeros((1, N_MAIN), F32).at[0, :LRU_WIDTH].set(p['lru_conv_b'][l])
    h0, proj, small, small_t = _inproj(h_in_x.reshape(t, D_MODEL), row(p['ln_g']), row(p['ln_b']),
                                       w_main, w_small, conv_w, conv_b, tiles['inproj'], seq)
    proj3 = proj.reshape(bsz, seq, N_MAIN)

    w_gates = jnp.concatenate([_block_diag(p['lru_w_rg'][l]), _block_diag(p['lru_w_ig'][l])], 1).astype(BF16)
    b_gates = jnp.concatenate([p['lru_b_rg'][l], p['lru_b_ig'][l]]).reshape(1, -1)
    y_lru = _lru(proj3, w_gates, b_gates, row(p['lru_lambda'][l]), row(p['lru_out_g'][l]), tiles['lru'])

    rows = GDN_GROUP * GDN_CHUNK
    small3 = small.reshape(bsz, seq, LANES)
    smallt3 = small_t.reshape(8, bsz, seq // rows, rows).transpose(1, 2, 0, 3)
    a_log, dt_bias = p['gdn_a_log'][l], p['gdn_dt_bias'][l]
    alr = _pad_lanes(a_log, GDN_HEADS, LANES)
    dtr = _pad_lanes(dt_bias, GDN_HEADS, LANES)
    alc = _pad_lanes(a_log, GDN_HEADS, 8).reshape(8, 1)
    dtc = _pad_lanes(dt_bias, GDN_HEADS, 8).reshape(8, 1)
    y_gdn = _gdn(proj3, small3, smallt3, alr, dtr, alc, dtc, row(p['gdn_norm_w'][l]), tiles['gdn_nb'])

    w_out = p['w_out'][l].astype(BF16)
    h1, h1p, top_e, wts, rank, counts = _router(
        y_lru.reshape(t, LRU_WIDTH), y_gdn.reshape(t, GDN_V), h0, w_out[:LRU_WIDTH], w_out[LRU_WIDTH:],
        row(p['ln1_g'][l]), row(p['ln1_b'][l]), p['w_router'][l].T, p['router_bias'][l].reshape(-1, 1),
        tiles['router'])

    counts = counts[:, 0]
    padded = (counts + MOE_BLOCK - 1) // MOE_BLOCK * MOE_BLOCK
    pad_end = jnp.cumsum(padded)
    pad_start = pad_end - padded
    n_blocks = (t * TOP_K) // MOE_BLOCK + N_EXPERTS
    n_rows = n_blocks * MOE_BLOCK
    n_used = (pad_end[-1] // MOE_BLOCK).astype(jnp.int32)
    blk_ids = jnp.minimum(jnp.arange(n_blocks, dtype=jnp.int32), n_used - 1)
    blk_e = jnp.minimum(jnp.sum(pad_end[None, :] <= (blk_ids * MOE_BLOCK)[:, None], axis=1),
                        N_EXPERTS - 1).astype(jnp.int32)

    dest = _dest(top_e, rank, pad_start.reshape(-1, 1), tiles['dest'])
    n_valid = jnp.clip(counts[blk_e] - (blk_ids * MOE_BLOCK - pad_start[blk_e]), 0, MOE_BLOCK).astype(jnp.int32)
    xs = _sc_scatter_rows(h1p, dest, n_rows, SC_CHUNK)
    active = jnp.arange(n_blocks, dtype=jnp.int32) < n_used
    first = (active & jnp.concatenate([jnp.ones((1,), bool), blk_e[1:] != blk_e[:-1]])).astype(jnp.int32)
    slot = ((jnp.cumsum(first) - 1) % 2).astype(jnp.int32)
    used = counts > 0
    later = jnp.where(used[None, :] & (jnp.arange(N_EXPERTS)[None, :] > jnp.arange(N_EXPERTS)[:, None]),
                      jnp.arange(N_EXPERTS, dtype=jnp.int32)[None, :], N_EXPERTS)
    next_used = jnp.min(later, axis=1)
    next_e = jnp.where(next_used < N_EXPERTS, next_used, -1)[blk_e].astype(jnp.int32)
    ys = _experts(blk_e, n_valid, first, slot, next_e, n_used.reshape(1), xs,
                  p['w_gate'][l], p['w_up'][l], p['w_down'][l])
    ws_gu = jnp.concatenate([p['ws_gate'][l], p['ws_up'][l]], 1).astype(BF16)
    yg = _sc_gather_rows(ys, dest.reshape(TOP_K * t), SC_CHUNK).reshape(TOP_K, t, D_PACK)
    out = _combine(h1, wts.T, ws_gu, p['ws_down'][l].astype(BF16),
                   row(p['ln2_g'][l]), row(p['ln2_b'][l]), yg, tiles['combine'])
    return out.reshape(bsz, seq, D_MODEL)


def _tiles(bsz, seq):
    t = bsz * seq
    return {
        'inproj': min(512, t),
        'lru': min(256, seq),
        'gdn_nb': bsz,
        'router': min(512, t),
        'dest': min(512, t),
        'combine': min(256, t),
    }


def kernel(x, ln_in_g, ln_in_b, w_in, lru_conv_w, lru_conv_b, lru_w_rg, lru_b_rg, lru_w_ig, lru_b_ig,
           lru_lambda, lru_out_g, gdn_conv_w, gdn_a_log, gdn_dt_bias, gdn_norm_w, w_out, ln1_g, ln1_b,
           w_router, router_bias, w_gate, w_up, w_down, ws_gate, ws_up, ws_down, ln2_g, ln2_b):
    assert w_in.shape[0] == DEPTH == 1
    p = dict(ln_g=ln_in_g, ln_b=ln_in_b, w_in=w_in, lru_conv_w=lru_conv_w, lru_conv_b=lru_conv_b,
             lru_w_rg=lru_w_rg, lru_b_rg=lru_b_rg, lru_w_ig=lru_w_ig, lru_b_ig=lru_b_ig,
             lru_lambda=lru_lambda, lru_out_g=lru_out_g, gdn_conv_w=gdn_conv_w, gdn_a_log=gdn_a_log,
             gdn_dt_bias=gdn_dt_bias, gdn_norm_w=gdn_norm_w, w_out=w_out, ln1_g=ln1_g, ln1_b=ln1_b,
             w_router=w_router, router_bias=router_bias, w_gate=w_gate, w_up=w_up, w_down=w_down,
             ws_gate=ws_gate, ws_up=ws_up, ws_down=ws_down, ln2_g=ln2_g, ln2_b=ln2_b)
    bsz, seq, _ = x.shape
    return _layer(x, 0, p, _tiles(bsz, seq))
```

```python
import functools

import jax
import jax.numpy as jnp
from jax import lax
from jax.experimental import pallas as pl
from jax.experimental.pallas import tpu as pltpu
from jax.experimental.pallas import tpu_sc as plsc

F32 = jnp.float32
BF16 = jnp.bfloat16

D_MODEL = 1024
LRU_WIDTH = 512
LRU_BLOCKS = 8
LRU_C = 8.0
CONV_WIDTH = 4
GDN_HEADS = 4
GDN_DK = 128
GDN_DV = 128
GDN_CHUNK = 64
GDN_GROUP = 2
GDN_QK = GDN_HEADS * GDN_DK
GDN_V = GDN_HEADS * GDN_DV
N_MAIN = 2 * LRU_WIDTH + 2 * GDN_QK + 2 * GDN_V
N_EXPERTS = 256
TOP_K = 8
N_GROUPS = 8
GROUP_SIZE = N_EXPERTS // N_GROUPS
TOPK_GROUPS = 4
D_EXPERT = 256
D_SHARED = 256
ROUTED_SCALE = 2.5
MOE_BLOCK = 256
D_PACK = D_MODEL // 2
LN_EPS = 1e-5
NORM_EPS = 1e-6
DEPTH = 1
DEEPNORM_ALPHA = (2.0 * DEPTH) ** 0.25

HALO = 8
CONV_GROUP = 512
CONV_GROUPS = (0, 2, 3, 4)
LANES = 128
VMEM_LIMIT = 56 * 1024 * 1024
ROUTER_SUB = 256
EXPERT_BANDS = 2
SC_CORES = 2
SC_WORKERS = 32
SC_CHUNK = 64

NN = (((1,), (0,)), ((), ()))
NT = (((1,), (1,)), ((), ()))
TN = (((0,), (0,)), ((), ()))


def _dot(a, b, dims=NN):
    return lax.dot_general(a, b, dims, preferred_element_type=F32)


def _split(a):
    hi = a.astype(BF16)
    lo = (a - hi.astype(F32)).astype(BF16)
    return hi, lo


def _dot3(a, b, dims=NN):
    ah, al = _split(a)
    bh, bl = _split(b)
    return _dot(ah, bh, dims) + (_dot(ah, bl, dims) + _dot(al, bh, dims))


def _layer_norm(x, g, b):
    mu = jnp.mean(x, -1, keepdims=True)
    xc = x - mu
    var = jnp.mean(xc * xc, -1, keepdims=True)
    return xc * lax.rsqrt(var + LN_EPS) * g + b


def _sigmoid(x):
    return 1.0 / (1.0 + jnp.exp(-x))


def _silu(x):
    return x * _sigmoid(x)


def _softplus(x):
    return jnp.maximum(x, 0.0) + jnp.log1p(jnp.exp(-jnp.abs(x)))


def _gelu_tanh(x):
    c = 0.7978845608028654
    return x * (0.5 * (1.0 + jnp.tanh(c * (x + 0.044715 * (x * x * x)))))


def _pack_rows(x):
    hi = lax.bitcast_convert_type(x[:, :D_PACK].astype(BF16).astype(F32), jnp.uint32)
    lo = lax.bitcast_convert_type(x[:, D_PACK:].astype(BF16).astype(F32), jnp.uint32)
    return (hi & jnp.uint32(0xFFFF0000)) | (lo >> 16)


def _unpack_rows(w):
    hi = lax.bitcast_convert_type(w & jnp.uint32(0xFFFF0000), F32)
    lo = lax.bitcast_convert_type(w << 16, F32)
    return hi, lo


def _params(sem, **kw):
    return pltpu.CompilerParams(dimension_semantics=sem, vmem_limit_bytes=VMEM_LIMIT, **kw)


def _inproj_kernel(x_ref, g_ref, b_ref, w_ref, ws_ref, cw_ref, cb_ref,
                   h_ref, proj_ref, small_ref, smallt_ref, hist, *, tiles_per_seq):
    i = pl.program_id(0)
    tm = x_ref.shape[0]
    h = _layer_norm(x_ref[...], g_ref[...], b_ref[...])
    h_ref[...] = h
    hb = h.astype(BF16)

    @pl.when(i % tiles_per_seq == 0)
    def _():
        hist[...] = jnp.zeros_like(hist)

    for g in range(N_MAIN // CONV_GROUP):
        cols = slice(g * CONV_GROUP, (g + 1) * CONV_GROUP)
        p = _dot(hb, w_ref[:, cols])
        if g in CONV_GROUPS:
            xcat = jnp.concatenate([hist[:, cols], p], axis=0)
            acc = cb_ref[:, cols]
            for j in range(CONV_WIDTH):
                off = HALO - (CONV_WIDTH - 1) + j
                acc = acc + xcat[off:off + tm, :] * cw_ref[j:j + 1, cols]
            hist[:, cols] = p[tm - HALO:, :]
            p = acc
        proj_ref[:, cols] = p
    small = _dot3(h, ws_ref[...])
    small_ref[...] = small
    smallt_ref[...] = small.T[:smallt_ref.shape[0], :]


def _inproj(x2d, g, b, w_main, w_small, conv_w, conv_b, tm, seq):
    t = x2d.shape[0]
    return pl.pallas_call(
        functools.partial(_inproj_kernel, tiles_per_seq=seq // tm),
        grid=(t // tm,),
        in_specs=[
            pl.BlockSpec((tm, D_MODEL), lambda i: (i, 0)),
            pl.BlockSpec((1, D_MODEL), lambda i: (0, 0)),
            pl.BlockSpec((1, D_MODEL), lambda i: (0, 0)),
            pl.BlockSpec((D_MODEL, N_MAIN), lambda i: (0, 0)),
            pl.BlockSpec((D_MODEL, LANES), lambda i: (0, 0)),
            pl.BlockSpec((CONV_WIDTH, N_MAIN), lambda i: (0, 0)),
            pl.BlockSpec((1, N_MAIN), lambda i: (0, 0)),
        ],
        out_specs=[
            pl.BlockSpec((tm, D_MODEL), lambda i: (i, 0)),
            pl.BlockSpec((tm, N_MAIN), lambda i: (i, 0)),
            pl.BlockSpec((tm, LANES), lambda i: (i, 0)),
            pl.BlockSpec((8, tm), lambda i: (0, i)),
        ],
        out_shape=[
            jax.ShapeDtypeStruct((t, D_MODEL), F32),
            jax.ShapeDtypeStruct((t, N_MAIN), F32),
            jax.ShapeDtypeStruct((t, LANES), F32),
            jax.ShapeDtypeStruct((8, t), F32),
        ],
        scratch_shapes=[pltpu.VMEM((HALO, N_MAIN), F32)],
        compiler_params=_params(("arbitrary",)),
        name="ln_inproj",
    )(x2d, g, b, w_main, w_small, conv_w, conv_b)


def _shift_rows(x, d, fill):
    rows = x.shape[0]
    if d % 8 == 0:
        pad = jnp.full((d, x.shape[1]), fill, x.dtype)
        return jnp.concatenate([pad, x[:rows - d]], axis=0)
    rolled = pltpu.roll(x, d, 0)
    row = lax.broadcasted_iota(jnp.int32, x.shape, 0)
    return jnp.where(row < d, fill, rolled)


def _lru_kernel(xc_ref, gate_ref, wg_ref, bg_ref, lam_ref, og_ref, y_ref, hcarry):
    s = pl.program_id(1)
    rows = xc_ref.shape[0]

    @pl.when(s == 0)
    def _():
        hcarry[...] = jnp.zeros_like(hcarry)

    xc = xc_ref[...]
    gates = _dot(xc.astype(BF16), wg_ref[...]) + bg_ref[...]
    r = _sigmoid(gates[:, :LRU_WIDTH])
    i = _sigmoid(gates[:, LRU_WIDTH:])
    log_a = (-LRU_C) * r * _softplus(-lam_ref[...])
    a = jnp.exp(log_a)
    mult = jnp.sqrt(-jnp.tanh(log_a) * (a * a + 1.0))
    bv = mult * (i * xc)
    d = 1
    while d < rows:
        a_sh = _shift_rows(a, d, 1.0)
        b_sh = _shift_rows(bv, d, 0.0)
        bv = a * b_sh + bv
        a = a * a_sh
        d *= 2
    h = a * hcarry[...] + bv
    hcarry[...] = h[rows - 1:rows, :]
    y = h * _gelu_tanh(gate_ref[...])
    ms = jnp.mean(y * y, -1, keepdims=True)
    y_ref[...] = y * lax.rsqrt(ms + NORM_EPS) * og_ref[...]


def _lru(proj3, w_gates, b_gates, lam, out_g, ts):
    bsz, seq, _ = proj3.shape
    row = lambda n: pl.BlockSpec((1, n), lambda b, s: (0, 0))
    return pl.pallas_call(
        _lru_kernel,
        grid=(bsz, seq // ts),
        in_specs=[
            pl.BlockSpec((None, ts, LRU_WIDTH), lambda b, s: (b, s, 0)),
            pl.BlockSpec((None, ts, LRU_WIDTH), lambda b, s: (b, s, 1)),
            pl.BlockSpec((LRU_WIDTH, 2 * LRU_WIDTH), lambda b, s: (0, 0)),
            row(2 * LRU_WIDTH),
            row(LRU_WIDTH),
            row(LRU_WIDTH),
        ],
        out_specs=pl.BlockSpec((None, ts, LRU_WIDTH), lambda b, s: (b, s, 0)),
        out_shape=jax.ShapeDtypeStruct((bsz, seq, LRU_WIDTH), F32),
        scratch_shapes=[
            pltpu.VMEM((1, LRU_WIDTH), F32),
        ],
        compiler_params=_params(("arbitrary", "arbitrary")),
        name="rg_lru",
    )(proj3, proj3, w_gates, b_gates, lam, out_g)


def _bdot(a, b, dims=NN):
    return _dot(a.astype(BF16), b.astype(BF16), dims)


def _gdn_heads(args, norm_w):
    c = GDN_CHUNK
    r = GDN_GROUP * c
    ri = lax.broadcasted_iota(jnp.int32, (r, r), 0)
    ci = lax.broadcasted_iota(jnp.int32, (r, r), 1)
    same = (ri // c) == (ci // c)
    causal = same & (ri >= ci)
    strict = same & (ri > ci)
    upper = same & (ri <= ci)
    chunk_of_row = lax.broadcasted_iota(jnp.int32, (r, 1), 0) // c
    each = lambda f, *ls: [f(*xs) for xs in zip(*ls)]
    q, k, v, z, beta, g_col, g_row, st = [list(x) for x in zip(*args)]
    q = each(lambda x: x * lax.rsqrt(jnp.sum(x * x, -1, keepdims=True) + NORM_EPS) * (GDN_DK ** -0.5), q)
    k = each(lambda x: x * lax.rsqrt(jnp.sum(x * x, -1, keepdims=True) + NORM_EPS), k)
    gc_col = each(lambda g: jnp.sum(jnp.where(causal, g, 0.0), axis=1, keepdims=True), g_row)
    gc_row = each(lambda g: jnp.sum(jnp.where(upper, g, 0.0), axis=0, keepdims=True), g_col)
    decay = each(lambda gc, gr: jnp.exp(jnp.where(causal, gc - gr, -jnp.inf)), gc_col, gc_row)
    kb = each(lambda x, bt: x * bt, k, beta)
    vb = each(lambda x, bt: x * bt, v, beta)
    kk = each(lambda x, y: _bdot(x, y, NT), kb, k)
    a_mat = each(lambda m, d: jnp.where(strict, m * d, 0.0), kk, decay)
    e_col = each(jnp.exp, gc_col)
    rhs = each(lambda x, y, e: jnp.concatenate([x, y * e], axis=1), vb, kb, e_col)
    sol = each(lambda rr, a: rr - _bdot(a, rr), rhs, a_mat)
    p = a_mat
    for _ in range(5):
        p = each(lambda x: _bdot(x, x), p)
        sol = each(lambda x, y: y + _bdot(x, y), p, sol)
    qk = each(lambda x, y: _bdot(x, y, NT), q, k)
    qk = each(lambda m, d: jnp.where(causal, m * d, 0.0), qk, decay)
    q_dec = each(lambda x, e: x * e, q, e_col)
    g_last = [each(lambda gc: gc[(j + 1) * c - 1:(j + 1) * c, :], gc_col) for j in range(GDN_GROUP)]

    def last_of_own_chunk(*gl):
        out = gl[-1]
        for j in range(GDN_GROUP - 2, -1, -1):
            out = jnp.where(chunk_of_row == j, gl[j], out)
        return out

    g_end = each(last_of_own_chunk, *g_last)
    k_dec = each(lambda x, ge, gc: x * jnp.exp(ge - gc), k, g_end, gc_col)
    qs_parts, v_parts = [], []
    for j in range(GDN_GROUP):
        rows = slice(j * c, (j + 1) * c)
        ws = each(lambda x, s: _bdot(x[rows, GDN_DV:], s), sol, st)
        qs_parts.append(each(lambda x, s: _bdot(x[rows], s), q_dec, st))
        v_new = each(lambda x, w: x[rows, :GDN_DV] - w, sol, ws)
        v_parts.append(v_new)
        kv = each(lambda x, vn: _bdot(x[rows], vn, TN), k_dec, v_new)
        st = each(lambda s, gl, d: s * jnp.exp(gl) + d, st, g_last[j], kv)
    qs = each(lambda *parts: jnp.concatenate(parts, axis=0), *qs_parts)
    v_all = each(lambda *parts: jnp.concatenate(parts, axis=0), *v_parts)
    o = each(lambda a, m, vn: a + _bdot(m, vn), qs, qk, v_all)
    o = each(lambda x: x * lax.rsqrt(jnp.mean(x * x, -1, keepdims=True) + NORM_EPS) * norm_w, o)
    o = each(lambda x, zz: x * _silu(zz), o, z)
    return list(zip(o, st))


def _gdn_kernel(q_ref, k_ref, v_ref, z_ref, sm_ref, smt_ref,
                alr_ref, dtr_ref, alc_ref, dtc_ref, nw_ref, y_ref, state):
    n = pl.program_id(1)
    c = GDN_GROUP * GDN_CHUNK
    nb = q_ref.shape[0]
    first = n == 0

    @pl.when(first)
    def _():
        state[...] = jnp.zeros_like(state)

    norm_w = nw_ref[...]

    args = []
    for b in range(nb):
        q_all = _silu(q_ref[b])
        k_all = _silu(k_ref[b])
        v_all = _silu(v_ref[b])
        z_all = z_ref[b]
        sm = sm_ref[b]
        beta_all = _sigmoid(sm)
        g_cols = -jnp.exp(alr_ref[...]) * _softplus(sm + dtr_ref[...])
        g_rows = -jnp.exp(alc_ref[...]) * _softplus(smt_ref[b] + dtc_ref[...])
        for hd in range(GDN_HEADS):
            sl = slice(hd * GDN_DK, (hd + 1) * GDN_DK)
            args.append((q_all[:, sl], k_all[:, sl], v_all[:, sl], z_all[:, sl],
                         beta_all[:, hd:hd + 1],
                         g_cols[:, GDN_HEADS + hd:GDN_HEADS + hd + 1],
                         g_rows[GDN_HEADS + hd:GDN_HEADS + hd + 1, :],
                         state[b, hd]))
    outs = _gdn_heads(args, norm_w)
    for b in range(nb):
        for hd in range(GDN_HEADS):
            o, st_new = outs[b * GDN_HEADS + hd]
            state[b, hd] = st_new
            y_ref[b, :, hd * GDN_DK:(hd + 1) * GDN_DK] = o


def _gdn(proj3, small3, smallt3, alr, dtr, alc, dtc, norm_w, nb):
    bsz, seq, _ = proj3.shape
    c = GDN_GROUP * GDN_CHUNK
    nch = seq // c
    col = lambda j: pl.BlockSpec((nb, c, GDN_QK), lambda b, n: (b, n, j))
    const = lambda shape: pl.BlockSpec(shape, lambda b, n: (0,) * len(shape))
    return pl.pallas_call(
        _gdn_kernel,
        grid=(bsz // nb, nch),
        in_specs=[
            col(2), col(3), col(4), col(5),
            pl.BlockSpec((nb, c, LANES), lambda b, n: (b, n, 0)),
            pl.BlockSpec((nb, None, 8, c), lambda b, n: (b, n, 0, 0)),
            const((1, LANES)), const((1, LANES)), const((8, 1)), const((8, 1)),
            const((1, GDN_DV)),
        ],
        out_specs=pl.BlockSpec((nb, c, GDN_V), lambda b, n: (b, n, 0)),
        out_shape=jax.ShapeDtypeStruct((bsz, seq, GDN_V), F32),
        scratch_shapes=[
            pltpu.VMEM((nb, GDN_HEADS, GDN_DK, GDN_DV), F32),
        ],
        compiler_params=_params(("arbitrary", "arbitrary")),
        name="gated_deltanet",
    )(proj3, proj3, proj3, proj3, small3, smallt3, alr, dtr, alc, dtc, norm_w)


def _pick_experts(logits, rbias):
    n = logits.shape[1]
    scores = _sigmoid(logits)
    choice = scores + rbias
    neg = -jnp.inf
    gs_rows = []
    sub = lax.broadcasted_iota(jnp.int32, (GROUP_SIZE, n), 0).astype(F32)
    for g in range(N_GROUPS):
        cg = choice[g * GROUP_SIZE:(g + 1) * GROUP_SIZE, :]
        m1 = jnp.max(cg, axis=0, keepdims=True)
        i1 = jnp.min(jnp.where(cg == m1, sub, float(GROUP_SIZE)), axis=0, keepdims=True)
        m2 = jnp.max(jnp.where(sub == i1, neg, cg), axis=0, keepdims=True)
        gs_rows.append(m1 + m2)
    gs = jnp.concatenate(gs_rows, axis=0)
    gi = lax.broadcasted_iota(jnp.int32, (N_GROUPS, n), 0).astype(F32)
    gsel = jnp.zeros((N_GROUPS, n), jnp.bool_)
    for _ in range(TOPK_GROUPS):
        m = jnp.max(gs, axis=0, keepdims=True)
        idx = jnp.min(jnp.where(gs == m, gi, float(N_GROUPS)), axis=0, keepdims=True)
        hit = gi == idx
        gsel = jnp.logical_or(gsel, hit)
        gs = jnp.where(hit, neg, gs)
    masked = jnp.concatenate(
        [jnp.where(gsel[g:g + 1, :], choice[g * GROUP_SIZE:(g + 1) * GROUP_SIZE, :], neg)
         for g in range(N_GROUPS)], axis=0)
    ei = lax.broadcasted_iota(jnp.int32, (N_EXPERTS, n), 0).astype(F32)
    hits, e_rows, w_rows = [], [], []
    multi = jnp.zeros((N_EXPERTS, n), F32)
    for _ in range(TOP_K):
        m = jnp.max(masked, axis=0, keepdims=True)
        idx = jnp.min(jnp.where(masked == m, ei, float(N_EXPERTS)), axis=0, keepdims=True)
        hit = ei == idx
        hits.append(hit)
        e_rows.append(idx)
        w_rows.append(jnp.sum(jnp.where(hit, scores, 0.0), axis=0, keepdims=True))
        multi = multi + hit.astype(F32)
        masked = jnp.where(hit, neg, masked)
    wts = jnp.concatenate(w_rows, axis=0)
    wts = wts / (jnp.sum(wts, axis=0, keepdims=True) + 1e-20) * ROUTED_SCALE
    return jnp.concatenate(e_rows, axis=0), wts, hits, multi


def _router_kernel(yl_ref, yg_ref, h0_ref, wo1_ref, wo2_ref, g_ref, b_ref, wrt_ref, rb_ref,
                   h1_ref, h1p_ref, e_ref, w_ref, rank_ref, cnt_ref, carry):
    i = pl.program_id(0)
    tm = h0_ref.shape[0]
    n = min(ROUTER_SUB, tm)
    subs = [slice(j * n, (j + 1) * n) for j in range(tm // n)]

    @pl.when(i == 0)
    def _():
        carry[...] = jnp.zeros_like(carry)

    mixes = [_dot(yl_ref[r, :].astype(BF16), wo1_ref[...]) + _dot(yg_ref[r, :].astype(BF16), wo2_ref[...])
             for r in subs]
    h1s = [_layer_norm(DEEPNORM_ALPHA * h0_ref[r, :] + mix, g_ref[...], b_ref[...])
           for r, mix in zip(subs, mixes)]
    for r, h1 in zip(subs, h1s):
        h1_ref[r, :] = h1
        h1p_ref[r, :] = _pack_rows(h1)
    logits = [_dot3(wrt_ref[...], h1, NT) for h1 in h1s]
    picks = [_pick_experts(lg, rb_ref[...]) for lg in logits]
    ti = lax.broadcasted_iota(jnp.int32, (n, n), 0)
    tj = lax.broadcasted_iota(jnp.int32, (n, n), 1)
    before = (ti < tj).astype(BF16)
    cums = [_dot(multi.astype(BF16), before) for _, _, _, multi in picks]
    base = carry[...]
    for r, (e_rows, wts, hits, multi), cum in zip(subs, picks, cums):
        cum = cum + base
        r_rows = [jnp.sum(jnp.where(hit, cum, 0.0), axis=0, keepdims=True) for hit in hits]
        base = base + jnp.sum(multi, axis=1, keepdims=True)
        e_ref[:, r] = e_rows.astype(jnp.int32)
        w_ref[:, r] = wts
        rank_ref[:, r] = jnp.concatenate(r_rows, axis=0).astype(jnp.int32)
    carry[...] = base
    cnt_ref[...] = base.astype(jnp.int32)


def _router(y_lru, y_gdn, h0, wo1, wo2, g, b, w_router_t, rbias, tm):
    t = h0.shape[0]
    const = lambda shape: pl.BlockSpec(shape, lambda i: (0,) * len(shape))
    return pl.pallas_call(
        _router_kernel,
        grid=(t // tm,),
        in_specs=[
            pl.BlockSpec((tm, LRU_WIDTH), lambda i: (i, 0)),
            pl.BlockSpec((tm, GDN_V), lambda i: (i, 0)),
            pl.BlockSpec((tm, D_MODEL), lambda i: (i, 0)),
            const((LRU_WIDTH, D_MODEL)), const((GDN_V, D_MODEL)),
            const((1, D_MODEL)), const((1, D_MODEL)),
            const((N_EXPERTS, D_MODEL)), const((N_EXPERTS, 1)),
        ],
        out_specs=[
            pl.BlockSpec((tm, D_MODEL), lambda i: (i, 0)),
            pl.BlockSpec((tm, D_PACK), lambda i: (i, 0)),
            pl.BlockSpec((TOP_K, tm), lambda i: (0, i)),
            pl.BlockSpec((TOP_K, tm), lambda i: (0, i)),
            pl.BlockSpec((TOP_K, tm), lambda i: (0, i)),
            const((N_EXPERTS, 1)),
        ],
        out_shape=[
            jax.ShapeDtypeStruct((t, D_MODEL), F32),
            jax.ShapeDtypeStruct((t, D_PACK), jnp.uint32),
            jax.ShapeDtypeStruct((TOP_K, t), jnp.int32),
            jax.ShapeDtypeStruct((TOP_K, t), F32),
            jax.ShapeDtypeStruct((TOP_K, t), jnp.int32),
            jax.ShapeDtypeStruct((N_EXPERTS, 1), jnp.int32),
        ],
        scratch_shapes=[pltpu.VMEM((N_EXPERTS, 1), F32)],
        compiler_params=_params(("arbitrary",)),
        name="outproj_router",
    )(y_lru, y_gdn, h0, wo1, wo2, g, b, w_router_t, rbias)


def _dest_kernel(e_ref, r_ref, ps_ref, d_ref):
    tm = e_ref.shape[1]
    ei = lax.broadcasted_iota(jnp.int32, (N_EXPERTS, tm), 0)
    rows = []
    for k in range(TOP_K):
        hit = ei == e_ref[k:k + 1, :]
        rows.append(jnp.sum(jnp.where(hit, ps_ref[...], 0), axis=0, keepdims=True))
    d_ref[...] = jnp.concatenate(rows, axis=0) + r_ref[...]


def _dest(top_e, rank, pad_start, tm):
    t = top_e.shape[1]
    blk = pl.BlockSpec((TOP_K, tm), lambda i: (0, i))
    return pl.pallas_call(
        _dest_kernel,
        grid=(t // tm,),
        in_specs=[blk, blk, pl.BlockSpec((N_EXPERTS, 1), lambda i: (0, 0))],
        out_specs=blk,
        out_shape=jax.ShapeDtypeStruct((TOP_K, t), jnp.int32),
        compiler_params=_params(("arbitrary",)),
        name="moe_dest",
    )(top_e, rank, pad_start)


def _sc_scatter_rows(rows, idx, n_out, chunk):
    n_copies, t = idx.shape
    d = rows.shape[1]
    per_worker = t // SC_WORKERS
    n_chunks = per_worker // chunk
    mesh = plsc.VectorSubcoreMesh(core_axis_name="c", subcore_axis_name="s")
    idx_flat = idx.reshape(n_copies * t)

    @functools.partial(
        pl.kernel, mesh=mesh,
        out_type=jax.ShapeDtypeStruct((n_out, d), rows.dtype),
        scratch_types=[pltpu.VMEM((chunk,), jnp.int32) for _ in range(n_copies)] + [
            pltpu.VMEM((chunk, d), rows.dtype),
            pltpu.SemaphoreType.DMA,
        ],
    )
    def scatter(rows_hbm, idx_hbm, out_hbm, *scratch):
        idx_v = scratch[:n_copies]
        rows_v, sem = scratch[n_copies:]
        wid = lax.axis_index("s") * SC_CORES + lax.axis_index("c")
        base = wid * per_worker

        @pl.loop(0, n_chunks)
        def _(j):
            off = base + j * chunk
            for k in range(n_copies):
                pltpu.sync_copy(idx_hbm.at[pl.ds(k * t + off, chunk)], idx_v[k])
            pltpu.sync_copy(rows_hbm.at[pl.ds(off, chunk)], rows_v)
            copies = [pltpu.async_copy(rows_v, out_hbm.at[idx_v[k]], sem) for k in range(n_copies)]
            for cp in copies:
                cp.wait()

    return scatter(rows, idx_flat)


def _expert_kernel(be_ref, nv_ref, first_ref, slot_ref, next_ref, nu_ref,
                   xs_ref, wg_hbm, wu_hbm, wd_hbm, ys_ref, wg_f, wu_f, wd_f, wgu_b, wd_b, sem):
    i = pl.program_id(0)

    def fetch(e, slot):
        return (pltpu.make_async_copy(wg_hbm.at[e], wg_f.at[slot], sem.at[slot]),
                pltpu.make_async_copy(wu_hbm.at[e], wu_f.at[slot], sem.at[slot]),
                pltpu.make_async_copy(wd_hbm.at[e], wd_f.at[slot], sem.at[slot]))

    @pl.when(i < nu_ref[0])
    def _():
        e = be_ref[i]
        slot = slot_ref[i]

        @pl.when(first_ref[i] == 1)
        def _():
            @pl.when(i == 0)
            def _():
                for cp in fetch(e, slot):
                    cp.start()

            for cp in fetch(e, slot):
                cp.wait()

            @pl.when(next_ref[i] >= 0)
            def _():
                for cp in fetch(next_ref[i], 1 - slot):
                    cp.start()

            wgu_b[:, :D_EXPERT] = wg_f[slot].astype(BF16)
            wgu_b[:, D_EXPERT:] = wu_f[slot].astype(BF16)
            wd_b[...] = wd_f[slot].astype(BF16)

        n = xs_ref.shape[0] // EXPERT_BANDS
        bands = [slice(j * n, (j + 1) * n) for j in range(EXPERT_BANDS)]
        row = lax.broadcasted_iota(jnp.int32, (n, D_PACK), 0)
        xs = [_unpack_rows(jnp.where(row + j * n < nv_ref[i], xs_ref[r, :], jnp.uint32(0)))
              for j, r in enumerate(bands)]
        gus = [_dot(x_hi.astype(BF16), wgu_b[:D_PACK, :]) + _dot(x_lo.astype(BF16), wgu_b[D_PACK:, :])
               for x_hi, x_lo in xs]
        hs = [_silu(gu[:, :D_EXPERT]) * gu[:, D_EXPERT:] for gu in gus]
        ys = [_dot(h.astype(BF16), wd_b[...]) for h in hs]
        for r, y in zip(bands, ys):
            ys_ref[r, :] = _pack_rows(y)


def _experts(blk_e, n_valid, first, slot, next_e, n_used, xs, w_gate, w_up, w_down):
    n_rows = xs.shape[0]
    n_blocks = n_rows // MOE_BLOCK
    blk = lambda i, be, nv, fi, sl, nx, nu: (jnp.minimum(i, nu[0] - 1), 0)
    return pl.pallas_call(
        _expert_kernel,
        grid_spec=pltpu.PrefetchScalarGridSpec(
            num_scalar_prefetch=6,
            grid=(n_blocks,),
            in_specs=[
                pl.BlockSpec((MOE_BLOCK, D_PACK), blk),
                pl.BlockSpec(memory_space=pl.ANY),
                pl.BlockSpec(memory_space=pl.ANY),
                pl.BlockSpec(memory_space=pl.ANY),
            ],
            out_specs=pl.BlockSpec((MOE_BLOCK, D_PACK), blk),
            scratch_shapes=[
                pltpu.VMEM((2, D_MODEL, D_EXPERT), F32),
                pltpu.VMEM((2, D_MODEL, D_EXPERT), F32),
                pltpu.VMEM((2, D_EXPERT, D_MODEL), F32),
                pltpu.VMEM((D_MODEL, 2 * D_EXPERT), BF16),
                pltpu.VMEM((D_EXPERT, D_MODEL), BF16),
                pltpu.SemaphoreType.DMA((2,)),
            ],
        ),
        out_shape=jax.ShapeDtypeStruct((n_rows, D_PACK), jnp.uint32),
        compiler_params=_params(("arbitrary",)),
        name="moe_experts",
    )(blk_e, n_valid, first, slot, next_e, n_used, xs, w_gate, w_up, w_down)


def _sc_gather_rows(table, idx, chunk):
    n_idx = idx.shape[0]
    d = table.shape[1]
    per_worker = n_idx // SC_WORKERS
    n_chunks = per_worker // chunk
    assert n_chunks % 2 == 0 and n_chunks * chunk * SC_WORKERS == n_idx
    mesh = plsc.VectorSubcoreMesh(core_axis_name="c", subcore_axis_name="s")

    @functools.partial(
        pl.kernel, mesh=mesh,
        out_type=jax.ShapeDtypeStruct((n_idx, d), table.dtype),
        scratch_types=[
            pltpu.VMEM((chunk,), jnp.int32), pltpu.VMEM((chunk,), jnp.int32),
            pltpu.VMEM((chunk, d), table.dtype), pltpu.VMEM((chunk, d), table.dtype),
            pltpu.SemaphoreType.DMA, pltpu.SemaphoreType.DMA, pltpu.SemaphoreType.DMA, pltpu.SemaphoreType.DMA,
        ],
    )
    def gather(table_hbm, idx_hbm, out_hbm, idx_v0, idx_v1, rows_v0, rows_v1, gsem0, gsem1, osem0, osem1):
        idx_v, rows_v, gsem, osem = (idx_v0, idx_v1), (rows_v0, rows_v1), (gsem0, gsem1), (osem0, osem1)
        wid = lax.axis_index("s") * SC_CORES + lax.axis_index("c")
        base = wid * per_worker

        def gather_copy(slot):
            return pltpu.make_async_copy(table_hbm.at[idx_v[slot]], rows_v[slot], gsem[slot])

        def out_copy(c, slot):
            return pltpu.make_async_copy(rows_v[slot], out_hbm.at[pl.ds(base + c * chunk, chunk)], osem[slot])

        def start_gather(c, slot):
            pltpu.sync_copy(idx_hbm.at[pl.ds(base + c * chunk, chunk)], idx_v[slot])
            gather_copy(slot).start()

        start_gather(0, 0)

        @pl.loop(0, n_chunks, step=2)
        def _(j):
            for b in range(2):
                c = j + b
                cur, other = b, 1 - b

                @pl.when(c >= 1)
                def _():
                    out_copy(c - 1, other).wait()

                @pl.when(c + 1 < n_chunks)
                def _():
                    start_gather(c + 1, other)

                gather_copy(cur).wait()
                out_copy(c, cur).start()

        out_copy(n_chunks - 1, 1).wait()

    return gather(table, idx)


def _combine_kernel(h1_ref, wts_ref, wsgu_ref, wsd_ref, g_ref, b_ref, yg_ref, out_ref):
    h1 = h1_ref[...]
    gu = _dot(h1.astype(BF16), wsgu_ref[...])
    hs = _silu(gu[:, :D_SHARED]) * gu[:, D_SHARED:]
    acc = DEEPNORM_ALPHA * h1 + _dot(hs.astype(BF16), wsd_ref[...])
    wts = wts_ref[...]
    acc_hi = acc[:, :D_PACK]
    acc_lo = acc[:, D_PACK:]
    for k in range(TOP_K):
        y_hi, y_lo = _unpack_rows(yg_ref[k])
        acc_hi = acc_hi + y_hi * wts[:, k:k + 1]
        acc_lo = acc_lo + y_lo * wts[:, k:k + 1]
    out_ref[...] = _layer_norm(jnp.concatenate([acc_hi, acc_lo], axis=1), g_ref[...], b_ref[...])


def _combine(h1, wts_t, ws_gu, ws_down, g, b, yg, tm):
    t = h1.shape[0]
    const = lambda shape: pl.BlockSpec(shape, lambda i: (0,) * len(shape))
    return pl.pallas_call(
        _combine_kernel,
        grid=(t // tm,),
        in_specs=[
            pl.BlockSpec((tm, D_MODEL), lambda i: (i, 0)),
            pl.BlockSpec((tm, TOP_K), lambda i: (i, 0)),
            const((D_MODEL, 2 * D_SHARED)), const((D_SHARED, D_MODEL)),
            const((1, D_MODEL)), const((1, D_MODEL)),
            pl.BlockSpec((TOP_K, tm, D_PACK), lambda i: (0, i, 0)),
        ],
        out_specs=pl.BlockSpec((tm, D_MODEL), lambda i: (i, 0)),
        out_shape=jax.ShapeDtypeStruct((t, D_MODEL), F32),
        compiler_params=_params(("arbitrary",)),
        name="moe_combine",
    )(h1, wts_t, ws_gu, ws_down, g, b, yg)


def _block_diag(w):
    nb, bi, bo = w.shape
    eye = jnp.eye(nb, dtype=w.dtype)
    return (eye[:, None, :, None] * w[:, :, None, :]).reshape(nb * bi, nb * bo)


def _pad_lanes(v, offset, width):
    return jnp.zeros((1, width), F32).at[0, offset:offset + v.shape[0]].set(v)


def _layer(h_in_x, l, p, tiles):
    bsz, seq, _ = h_in_x.shape
    t = bsz * seq
    row = lambda v: v.reshape(1, -1)

    w_in = p['w_in'][l]
    w_main = w_in[:, :N_MAIN].astype(BF16)
    w_small = jnp.zeros((D_MODEL, LANES), F32).at[:, :2 * GDN_HEADS].set(w_in[:, N_MAIN:])
    zeros = lambda n: jnp.zeros((CONV_WIDTH, n), F32)
    conv_w = jnp.concatenate([p['lru_conv_w'][l], zeros(LRU_WIDTH), p['gdn_conv_w'][l], zeros(GDN_V)], 1)
    conv_b = jnp.zeros((1, N_MAIN), F32).at[0, :LRU_WIDTH].set(p['lru_conv_b'][l])
    h0, proj, small, small_t = _inproj(h_in_x.reshape(t, D_MODEL), row(p['ln_g']), row(p['ln_b']),
                                       w_main, w_small, conv_w, conv_b, tiles['inproj'], seq)
    proj3 = proj.reshape(bsz, seq, N_MAIN)

    w_gates = jnp.concatenate([_block_diag(p['lru_w_rg'][l]), _block_diag(p['lru_w_ig'][l])], 1).astype(BF16)
    b_gates = jnp.concatenate([p['lru_b_rg'][l], p['lru_b_ig'][l]]).reshape(1, -1)
    y_lru = _lru(proj3, w_gates, b_gates, row(p['lru_lambda'][l]), row(p['lru_out_g'][l]), tiles['lru'])

    rows = GDN_GROUP * GDN_CHUNK
    small3 = small.reshape(bsz, seq, LANES)
    smallt3 = small_t.reshape(8, bsz, seq // rows, rows).transpose(1, 2, 0, 3)
    a_log, dt_bias = p['gdn_a_log'][l], p['gdn_dt_bias'][l]
    alr = _pad_lanes(a_log, GDN_HEADS, LANES)
    dtr = _pad_lanes(dt_bias, GDN_HEADS, LANES)
    alc = _pad_lanes(a_log, GDN_HEADS, 8).reshape(8, 1)
    dtc = _pad_lanes(dt_bias, GDN_HEADS, 8).reshape(8, 1)
    y_gdn = _gdn(proj3, small3, smallt3, alr, dtr, alc, dtc, row(p['gdn_norm_w'][l]), tiles['gdn_nb'])

    w_out = p['w_out'][l].astype(BF16)
    h1, h1p, top_e, wts, rank, counts = _router(
        y_lru.reshape(t, LRU_WIDTH), y_gdn.reshape(t, GDN_V), h0, w_out[:LRU_WIDTH], w_out[LRU_WIDTH:],
        row(p['ln1_g'][l]), row(p['ln1_b'][l]), p['w_router'][l].T, p['router_bias'][l].reshape(-1, 1),
        tiles['router'])

    counts = counts[:, 0]
    padded = (counts + MOE_BLOCK - 1) // MOE_BLOCK * MOE_BLOCK
    pad_end = jnp.cumsum(padded)
    pad_start = pad_end - padded
    n_blocks = (t * TOP_K) // MOE_BLOCK + N_EXPERTS
    n_rows = n_blocks * MOE_BLOCK
    n_used = (pad_end[-1] // MOE_BLOCK).astype(jnp.int32)
    blk_ids = jnp.minimum(jnp.arange(n_blocks, dtype=jnp.int32), n_used - 1)
    blk_e = jnp.minimum(jnp.sum(pad_end[None, :] <= (blk_ids * MOE_BLOCK)[:, None], axis=1),
                        N_EXPERTS - 1).astype(jnp.int32)

    dest = _dest(top_e, rank, pad_start.reshape(-1, 1), tiles['dest'])
    n_valid = jnp.clip(counts[blk_e] - (blk_ids * MOE_BLOCK - pad_start[blk_e]), 0, MOE_BLOCK).astype(jnp.int32)
    xs = _sc_scatter_rows(h1p, dest, n_rows, SC_CHUNK)
    active = jnp.arange(n_blocks, dtype=jnp.int32) < n_used
    first = (active & jnp.concatenate([jnp.ones((1,), bool), blk_e[1:] != blk_e[:-1]])).astype(jnp.int32)
    slot = ((jnp.cumsum(first) - 1) % 2).astype(jnp.int32)
    used = counts > 0
    later = jnp.where(used[None, :] & (jnp.arange(N_EXPERTS)[None, :] > jnp.arange(N_EXPERTS)[:, None]),
                      jnp.arange(N_EXPERTS, dtype=jnp.int32)[None, :], N_EXPERTS)
    next_used = jnp.min(later, axis=1)
    next_e = jnp.where(next_used < N_EXPERTS, next_used, -1)[blk_e].astype(jnp.int32)
    ys = _experts(blk_e, n_valid, first, slot, next_e, n_used.reshape(1), xs,
                  p['w_gate'][l], p['w_up'][l], p['w_down'][l])
    ws_gu = jnp.concatenate([p['ws_gate'][l], p['ws_up'][l]], 1).astype(BF16)
    yg = _sc_gather_rows(ys, dest.reshape(TOP_K * t), SC_CHUNK).reshape(TOP_K, t, D_PACK)
    out = _combine(h1, wts.T, ws_gu, p['ws_down'][l].astype(BF16),
                   row(p['ln2_g'][l]), row(p['ln2_b'][l]), yg, tiles['combine'])
    return out.reshape(bsz, seq, D_MODEL)


def _tiles(bsz, seq):
    t = bsz * seq
    return {
        'inproj': min(512, t),
        'lru': min(256, seq),
        'gdn_nb': bsz,
        'router': min(512, t),
        'dest': min(512, t),
        'combine': min(256, t),
    }


def kernel(x, ln_in_g, ln_in_b, w_in, lru_conv_w, lru_conv_b, lru_w_rg, lru_b_rg, lru_w_ig, lru_b_ig,
           lru_lambda, lru_out_g, gdn_conv_w, gdn_a_log, gdn_dt_bias, gdn_norm_w, w_out, ln1_g, ln1_b,
           w_router, router_bias, w_gate, w_up, w_down, ws_gate, ws_up, ws_down, ln2_g, ln2_b):
    assert w_in.shape[0] == DEPTH == 1
    p = dict(ln_g=ln_in_g, ln_b=ln_in_b, w_in=w_in, lru_conv_w=lru_conv_w, lru_conv_b=lru_conv_b,
             lru_w_rg=lru_w_rg, lru_b_rg=lru_b_rg, lru_w_ig=lru_w_ig, lru_b_ig=lru_b_ig,
             lru_lambda=lru_lambda, lru_out_g=lru_out_g, gdn_conv_w=gdn_conv_w, gdn_a_log=gdn_a_log,
             gdn_dt_bias=gdn_dt_bias, gdn_norm_w=gdn_norm_w, w_out=w_out, ln1_g=ln1_g, ln1_b=ln1_b,
             w_router=w_router, router_bias=router_bias, w_gate=w_gate, w_up=w_up, w_down=w_down,
             ws_gate=ws_gate, ws_up=ws_up, ws_down=ws_down, ln2_g=ln2_g, ln2_b=ln2_b)
    bsz, seq, _ = x.shape
    return _layer(x, 0, p, _tiles(bsz, seq))
```

```python
import functools

import jax
import jax.numpy as jnp
from jax import lax
from jax.experimental import pallas as pl
from jax.experimental.pallas import tpu as pltpu
from jax.experimental.pallas import tpu_sc as plsc

F32 = jnp.float32
BF16 = jnp.bfloat16

D_MODEL = 1024
LRU_WIDTH = 512
LRU_BLOCKS = 8
LRU_C = 8.0
CONV_WIDTH = 4
GDN_HEADS = 4
GDN_DK = 128
GDN_DV = 128
GDN_CHUNK = 64
GDN_GROUP = 2
GDN_QK = GDN_HEADS * GDN_DK
GDN_V = GDN_HEADS * GDN_DV
N_MAIN = 2 * LRU_WIDTH + 2 * GDN_QK + 2 * GDN_V
N_EXPERTS = 256
TOP_K = 8
N_GROUPS = 8
GROUP_SIZE = N_EXPERTS // N_GROUPS
TOPK_GROUPS = 4
D_EXPERT = 256
D_SHARED = 256
ROUTED_SCALE = 2.5
MOE_BLOCK = 512
D_PACK = D_MODEL // 2
LN_EPS = 1e-5
NORM_EPS = 1e-6
DEPTH = 1
DEEPNORM_ALPHA = (2.0 * DEPTH) ** 0.25

HALO = 8
CONV_GROUP = 512
CONV_GROUPS = (0, 2, 3, 4)
LANES = 128
VMEM_LIMIT = 56 * 1024 * 1024
ROUTER_SUB = 256
EXPERT_BANDS = 4
SC_CORES = 2
SC_WORKERS = 32
SC_CHUNK = 64

NN = (((1,), (0,)), ((), ()))
NT = (((1,), (1,)), ((), ()))
TN = (((0,), (0,)), ((), ()))


def _dot(a, b, dims=NN):
    return lax.dot_general(a, b, dims, preferred_element_type=F32)


def _split(a):
    hi = a.astype(BF16)
    lo = (a - hi.astype(F32)).astype(BF16)
    return hi, lo


def _dot3(a, b, dims=NN):
    ah, al = _split(a)
    bh, bl = _split(b)
    return _dot(ah, bh, dims) + (_dot(ah, bl, dims) + _dot(al, bh, dims))


def _layer_norm(x, g, b):
    mu = jnp.mean(x, -1, keepdims=True)
    xc = x - mu
    var = jnp.mean(xc * xc, -1, keepdims=True)
    return xc * lax.rsqrt(var + LN_EPS) * g + b


def _sigmoid(x):
    return 1.0 / (1.0 + jnp.exp(-x))


def _silu(x):
    return x * _sigmoid(x)


def _softplus(x):
    return jnp.maximum(x, 0.0) + jnp.log1p(jnp.exp(-jnp.abs(x)))


def _gelu_tanh(x):
    c = 0.7978845608028654
    return x * (0.5 * (1.0 + jnp.tanh(c * (x + 0.044715 * (x * x * x)))))


def _pack_rows(x):
    hi = lax.bitcast_convert_type(x[:, :D_PACK].astype(BF16).astype(F32), jnp.uint32)
    lo = lax.bitcast_convert_type(x[:, D_PACK:].astype(BF16).astype(F32), jnp.uint32)
    return (hi & jnp.uint32(0xFFFF0000)) | (lo >> 16)


def _unpack_rows(w):
    hi = lax.bitcast_convert_type(w & jnp.uint32(0xFFFF0000), F32)
    lo = lax.bitcast_convert_type(w << 16, F32)
    return hi, lo


def _params(sem, **kw):
    return pltpu.CompilerParams(dimension_semantics=sem, vmem_limit_bytes=VMEM_LIMIT, **kw)


def _inproj_kernel(x_ref, g_ref, b_ref, w_ref, ws_ref, cw_ref, cb_ref,
                   h_ref, proj_ref, small_ref, smallt_ref, hist, *, tiles_per_seq):
    i = pl.program_id(0)
    tm = x_ref.shape[0]
    h = _layer_norm(x_ref[...], g_ref[...], b_ref[...])
    h_ref[...] = h
    hb = h.astype(BF16)

    @pl.when(i % tiles_per_seq == 0)
    def _():
        hist[...] = jnp.zeros_like(hist)

    for g in range(N_MAIN // CONV_GROUP):
        cols = slice(g * CONV_GROUP, (g + 1) * CONV_GROUP)
        p = _dot(hb, w_ref[:, cols])
        if g in CONV_GROUPS:
            xcat = jnp.concatenate([hist[:, cols], p], axis=0)
            acc = cb_ref[:, cols]
            for j in range(CONV_WIDTH):
                off = HALO - (CONV_WIDTH - 1) + j
                acc = acc + xcat[off:off + tm, :] * cw_ref[j:j + 1, cols]
            hist[:, cols] = p[tm - HALO:, :]
            p = acc
        proj_ref[:, cols] = p
    small = _dot3(h, ws_ref[...])
    small_ref[...] = small
    smallt_ref[...] = small.T[:smallt_ref.shape[0], :]


def _inproj(x2d, g, b, w_main, w_small, conv_w, conv_b, tm, seq):
    t = x2d.shape[0]
    return pl.pallas_call(
        functools.partial(_inproj_kernel, tiles_per_seq=seq // tm),
        grid=(t // tm,),
        in_specs=[
            pl.BlockSpec((tm, D_MODEL), lambda i: (i, 0)),
            pl.BlockSpec((1, D_MODEL), lambda i: (0, 0)),
            pl.BlockSpec((1, D_MODEL), lambda i: (0, 0)),
            pl.BlockSpec((D_MODEL, N_MAIN), lambda i: (0, 0)),
            pl.BlockSpec((D_MODEL, LANES), lambda i: (0, 0)),
            pl.BlockSpec((CONV_WIDTH, N_MAIN), lambda i: (0, 0)),
            pl.BlockSpec((1, N_MAIN), lambda i: (0, 0)),
        ],
        out_specs=[
            pl.BlockSpec((tm, D_MODEL), lambda i: (i, 0)),
            pl.BlockSpec((tm, N_MAIN), lambda i: (i, 0)),
            pl.BlockSpec((tm, LANES), lambda i: (i, 0)),
            pl.BlockSpec((8, tm), lambda i: (0, i)),
        ],
        out_shape=[
            jax.ShapeDtypeStruct((t, D_MODEL), F32),
            jax.ShapeDtypeStruct((t, N_MAIN), F32),
            jax.ShapeDtypeStruct((t, LANES), F32),
            jax.ShapeDtypeStruct((8, t), F32),
        ],
        scratch_shapes=[pltpu.VMEM((HALO, N_MAIN), F32)],
        compiler_params=_params(("arbitrary",)),
        name="ln_inproj",
    )(x2d, g, b, w_main, w_small, conv_w, conv_b)


def _shift_rows(x, d, fill):
    rows = x.shape[0]
    if d % 8 == 0:
        pad = jnp.full((d, x.shape[1]), fill, x.dtype)
        return jnp.concatenate([pad, x[:rows - d]], axis=0)
    rolled = pltpu.roll(x, d, 0)
    row = lax.broadcasted_iota(jnp.int32, x.shape, 0)
    return jnp.where(row < d, fill, rolled)


def _lru_kernel(xc_ref, gate_ref, wg_ref, bg_ref, lam_ref, og_ref, y_ref, hcarry):
    s = pl.program_id(1)
    rows = xc_ref.shape[0]

    @pl.when(s == 0)
    def _():
        hcarry[...] = jnp.zeros_like(hcarry)

    xc = xc_ref[...]
    gates = _dot(xc.astype(BF16), wg_ref[...]) + bg_ref[...]
    r = _sigmoid(gates[:, :LRU_WIDTH])
    i = _sigmoid(gates[:, LRU_WIDTH:])
    log_a = (-LRU_C) * r * _softplus(-lam_ref[...])
    a = jnp.exp(log_a)
    mult = jnp.sqrt(-jnp.tanh(log_a) * (a * a + 1.0))
    bv = mult * (i * xc)
    d = 1
    while d < rows:
        a_sh = _shift_rows(a, d, 1.0)
        b_sh = _shift_rows(bv, d, 0.0)
        bv = a * b_sh + bv
        a = a * a_sh
        d *= 2
    h = a * hcarry[...] + bv
    hcarry[...] = h[rows - 1:rows, :]
    y = h * _gelu_tanh(gate_ref[...])
    ms = jnp.mean(y * y, -1, keepdims=True)
    y_ref[...] = y * lax.rsqrt(ms + NORM_EPS) * og_ref[...]


def _lru(proj3, w_gates, b_gates, lam, out_g, ts):
    bsz, seq, _ = proj3.shape
    row = lambda n: pl.BlockSpec((1, n), lambda b, s: (0, 0))
    return pl.pallas_call(
        _lru_kernel,
        grid=(bsz, seq // ts),
        in_specs=[
            pl.BlockSpec((None, ts, LRU_WIDTH), lambda b, s: (b, s, 0)),
            pl.BlockSpec((None, ts, LRU_WIDTH), lambda b, s: (b, s, 1)),
            pl.BlockSpec((LRU_WIDTH, 2 * LRU_WIDTH), lambda b, s: (0, 0)),
            row(2 * LRU_WIDTH),
            row(LRU_WIDTH),
            row(LRU_WIDTH),
        ],
        out_specs=pl.BlockSpec((None, ts, LRU_WIDTH), lambda b, s: (b, s, 0)),
        out_shape=jax.ShapeDtypeStruct((bsz, seq, LRU_WIDTH), F32),
        scratch_shapes=[
            pltpu.VMEM((1, LRU_WIDTH), F32),
        ],
        compiler_params=_params(("arbitrary", "arbitrary")),
        name="rg_lru",
    )(proj3, proj3, w_gates, b_gates, lam, out_g)


def _bdot(a, b, dims=NN):
    return _dot(a.astype(BF16), b.astype(BF16), dims)


def _gdn_heads(args, norm_w):
    c = GDN_CHUNK
    r = GDN_GROUP * c
    ri = lax.broadcasted_iota(jnp.int32, (r, r), 0)
    ci = lax.broadcasted_iota(jnp.int32, (r, r), 1)
    same = (ri // c) == (ci // c)
    causal = same & (ri >= ci)
    strict = same & (ri > ci)
    upper = same & (ri <= ci)
    chunk_of_row = lax.broadcasted_iota(jnp.int32, (r, 1), 0) // c
    each = lambda f, *ls: [f(*xs) for xs in zip(*ls)]
    q, k, v, z, beta, g_col, g_row, st = [list(x) for x in zip(*args)]
    q = each(lambda x: x * lax.rsqrt(jnp.sum(x * x, -1, keepdims=True) + NORM_EPS) * (GDN_DK ** -0.5), q)
    k = each(lambda x: x * lax.rsqrt(jnp.sum(x * x, -1, keepdims=True) + NORM_EPS), k)
    gc_col = each(lambda g: jnp.sum(jnp.where(causal, g, 0.0), axis=1, keepdims=True), g_row)
    gc_row = each(lambda g: jnp.sum(jnp.where(upper, g, 0.0), axis=0, keepdims=True), g_col)
    decay = each(lambda gc, gr: jnp.exp(jnp.where(causal, gc - gr, -jnp.inf)), gc_col, gc_row)
    kb = each(lambda x, bt: x * bt, k, beta)
    vb = each(lambda x, bt: x * bt, v, beta)
    kk = each(lambda x, y: _bdot(x, y, NT), kb, k)
    a_mat = each(lambda m, d: jnp.where(strict, m * d, 0.0), kk, decay)
    e_col = each(jnp.exp, gc_col)
    rhs = each(lambda x, y, e: jnp.concatenate([x, y * e], axis=1), vb, kb, e_col)
    sol = each(lambda rr, a: rr - _bdot(a, rr), rhs, a_mat)
    p = a_mat
    for _ in range(5):
        p = each(lambda x: _bdot(x, x), p)
        sol = each(lambda x, y: y + _bdot(x, y), p, sol)
    qk = each(lambda x, y: _bdot(x, y, NT), q, k)
    qk = each(lambda m, d: jnp.where(causal, m * d, 0.0), qk, decay)
    q_dec = each(lambda x, e: x * e, q, e_col)
    g_last = [each(lambda gc: gc[(j + 1) * c - 1:(j + 1) * c, :], gc_col) for j in range(GDN_GROUP)]

    def last_of_own_chunk(*gl):
        out = gl[-1]
        for j in range(GDN_GROUP - 2, -1, -1):
            out = jnp.where(chunk_of_row == j, gl[j], out)
        return out

    g_end = each(last_of_own_chunk, *g_last)
    k_dec = each(lambda x, ge, gc: x * jnp.exp(ge - gc), k, g_end, gc_col)
    qs_parts, v_parts = [], []
    for j in range(GDN_GROUP):
        rows = slice(j * c, (j + 1) * c)
        ws = each(lambda x, s: _bdot(x[rows, GDN_DV:], s), sol, st)
        qs_parts.append(each(lambda x, s: _bdot(x[rows], s), q_dec, st))
        v_new = each(lambda x, w: x[rows, :GDN_DV] - w, sol, ws)
        v_parts.append(v_new)
        kv = each(lambda x, vn: _bdot(x[rows], vn, TN), k_dec, v_new)
        st = each(lambda s, gl, d: s * jnp.exp(gl) + d, st, g_last[j], kv)
    qs = each(lambda *parts: jnp.concatenate(parts, axis=0), *qs_parts)
    v_all = each(lambda *parts: jnp.concatenate(parts, axis=0), *v_parts)
    o = each(lambda a, m, vn: a + _bdot(m, vn), qs, qk, v_all)
    o = each(lambda x: x * lax.rsqrt(jnp.mean(x * x, -1, keepdims=True) + NORM_EPS) * norm_w, o)
    o = each(lambda x, zz: x * _silu(zz), o, z)
    return list(zip(o, st))


def _gdn_kernel(q_ref, k_ref, v_ref, z_ref, sm_ref, smt_ref,
                alr_ref, dtr_ref, alc_ref, dtc_ref, nw_ref, y_ref, state):
    n = pl.program_id(1)
    c = GDN_GROUP * GDN_CHUNK
    nb = q_ref.shape[0]
    first = n == 0

    @pl.when(first)
    def _():
        state[...] = jnp.zeros_like(state)

    norm_w = nw_ref[...]

    args = []
    for b in range(nb):
        q_all = _silu(q_ref[b])
        k_all = _silu(k_ref[b])
        v_all = _silu(v_ref[b])
        z_all = z_ref[b]
        sm = sm_ref[b]
        beta_all = _sigmoid(sm)
        g_cols = -jnp.exp(alr_ref[...]) * _softplus(sm + dtr_ref[...])
        g_rows = -jnp.exp(alc_ref[...]) * _softplus(smt_ref[b] + dtc_ref[...])
        for hd in range(GDN_HEADS):
            sl = slice(hd * GDN_DK, (hd + 1) * GDN_DK)
            args.append((q_all[:, sl], k_all[:, sl], v_all[:, sl], z_all[:, sl],
                         beta_all[:, hd:hd + 1],
                         g_cols[:, GDN_HEADS + hd:GDN_HEADS + hd + 1],
                         g_rows[GDN_HEADS + hd:GDN_HEADS + hd + 1, :],
                         state[b, hd]))
    outs = _gdn_heads(args, norm_w)
    for b in range(nb):
        for hd in range(GDN_HEADS):
            o, st_new = outs[b * GDN_HEADS + hd]
            state[b, hd] = st_new
            y_ref[b, :, hd * GDN_DK:(hd + 1) * GDN_DK] = o


def _gdn(proj3, small3, smallt3, alr, dtr, alc, dtc, norm_w, nb):
    bsz, seq, _ = proj3.shape
    c = GDN_GROUP * GDN_CHUNK
    nch = seq // c
    col = lambda j: pl.BlockSpec((nb, c, GDN_QK), lambda b, n: (b, n, j))
    const = lambda shape: pl.BlockSpec(shape, lambda b, n: (0,) * len(shape))
    return pl.pallas_call(
        _gdn_kernel,
        grid=(bsz // nb, nch),
        in_specs=[
            col(2), col(3), col(4), col(5),
            pl.BlockSpec((nb, c, LANES), lambda b, n: (b, n, 0)),
            pl.BlockSpec((nb, None, 8, c), lambda b, n: (b, n, 0, 0)),
            const((1, LANES)), const((1, LANES)), const((8, 1)), const((8, 1)),
            const((1, GDN_DV)),
        ],
        out_specs=pl.BlockSpec((nb, c, GDN_V), lambda b, n: (b, n, 0)),
        out_shape=jax.ShapeDtypeStruct((bsz, seq, GDN_V), F32),
        scratch_shapes=[
            pltpu.VMEM((nb, GDN_HEADS, GDN_DK, GDN_DV), F32),
        ],
        compiler_params=_params(("arbitrary", "arbitrary")),
        name="gated_deltanet",
    )(proj3, proj3, proj3, proj3, small3, smallt3, alr, dtr, alc, dtc, norm_w)


def _pick_experts(logits, rbias):
    n = logits.shape[1]
    scores = _sigmoid(logits)
    choice = scores + rbias
    neg = -jnp.inf
    gs_rows = []
    sub = lax.broadcasted_iota(jnp.int32, (GROUP_SIZE, n), 0).astype(F32)
    for g in range(N_GROUPS):
        cg = choice[g * GROUP_SIZE:(g + 1) * GROUP_SIZE, :]
        m1 = jnp.max(cg, axis=0, keepdims=True)
        i1 = jnp.min(jnp.where(cg == m1, sub, float(GROUP_SIZE)), axis=0, keepdims=True)
        m2 = jnp.max(jnp.where(sub == i1, neg, cg), axis=0, keepdims=True)
        gs_rows.append(m1 + m2)
    gs = jnp.concatenate(gs_rows, axis=0)
    gi = lax.broadcasted_iota(jnp.int32, (N_GROUPS, n), 0).astype(F32)
    gsel = jnp.zeros((N_GROUPS, n), jnp.bool_)
    for _ in range(TOPK_GROUPS):
        m = jnp.max(gs, axis=0, keepdims=True)
        idx = jnp.min(jnp.where(gs == m, gi, float(N_GROUPS)), axis=0, keepdims=True)
        hit = gi == idx
        gsel = jnp.logical_or(gsel, hit)
        gs = jnp.where(hit, neg, gs)
    masked = jnp.concatenate(
        [jnp.where(gsel[g:g + 1, :], choice[g * GROUP_SIZE:(g + 1) * GROUP_SIZE, :], neg)
         for g in range(N_GROUPS)], axis=0)
    ei = lax.broadcasted_iota(jnp.int32, (N_EXPERTS, n), 0).astype(F32)
    hits, e_rows, w_rows = [], [], []
    multi = jnp.zeros((N_EXPERTS, n), F32)
    for _ in range(TOP_K):
        m = jnp.max(masked, axis=0, keepdims=True)
        idx = jnp.min(jnp.where(masked == m, ei, float(N_EXPERTS)), axis=0, keepdims=True)
        hit = ei == idx
        hits.append(hit)
        e_rows.append(idx)
        w_rows.append(jnp.sum(jnp.where(hit, scores, 0.0), axis=0, keepdims=True))
        multi = multi + hit.astype(F32)
        masked = jnp.where(hit, neg, masked)
    wts = jnp.concatenate(w_rows, axis=0)
    wts = wts / (jnp.sum(wts, axis=0, keepdims=True) + 1e-20) * ROUTED_SCALE
    return jnp.concatenate(e_rows, axis=0), wts, hits, multi


def _router_kernel(yl_ref, yg_ref, h0_ref, wo1_ref, wo2_ref, g_ref, b_ref, wrt_ref, rb_ref,
                   h1_ref, h1p_ref, e_ref, w_ref, rank_ref, cnt_ref, carry):
    i = pl.program_id(0)
    tm = h0_ref.shape[0]
    n = min(ROUTER_SUB, tm)
    subs = [slice(j * n, (j + 1) * n) for j in range(tm // n)]

    @pl.when(i == 0)
    def _():
        carry[...] = jnp.zeros_like(carry)

    mixes = [_dot(yl_ref[r, :].astype(BF16), wo1_ref[...]) + _dot(yg_ref[r, :].astype(BF16), wo2_ref[...])
             for r in subs]
    h1s = [_layer_norm(DEEPNORM_ALPHA * h0_ref[r, :] + mix, g_ref[...], b_ref[...])
           for r, mix in zip(subs, mixes)]
    for r, h1 in zip(subs, h1s):
        h1_ref[r, :] = h1
        h1p_ref[r, :] = _pack_rows(h1)
    logits = [_dot3(wrt_ref[...], h1, NT) for h1 in h1s]
    picks = [_pick_experts(lg, rb_ref[...]) for lg in logits]
    ti = lax.broadcasted_iota(jnp.int32, (n, n), 0)
    tj = lax.broadcasted_iota(jnp.int32, (n, n), 1)
    before = (ti < tj).astype(BF16)
    cums = [_dot(multi.astype(BF16), before) for _, _, _, multi in picks]
    base = carry[...]
    for r, (e_rows, wts, hits, multi), cum in zip(subs, picks, cums):
        cum = cum + base
        r_rows = [jnp.sum(jnp.where(hit, cum, 0.0), axis=0, keepdims=True) for hit in hits]
        base = base + jnp.sum(multi, axis=1, keepdims=True)
        e_ref[:, r] = e_rows.astype(jnp.int32)
        w_ref[:, r] = wts
        rank_ref[:, r] = jnp.concatenate(r_rows, axis=0).astype(jnp.int32)
    carry[...] = base
    cnt_ref[...] = base.astype(jnp.int32)


def _router(y_lru, y_gdn, h0, wo1, wo2, g, b, w_router_t, rbias, tm):
    t = h0.shape[0]
    const = lambda shape: pl.BlockSpec(shape, lambda i: (0,) * len(shape))
    return pl.pallas_call(
        _router_kernel,
        grid=(t // tm,),
        in_specs=[
            pl.BlockSpec((tm, LRU_WIDTH), lambda i: (i, 0)),
            pl.BlockSpec((tm, GDN_V), lambda i: (i, 0)),
            pl.BlockSpec((tm, D_MODEL), lambda i: (i, 0)),
            const((LRU_WIDTH, D_MODEL)), const((GDN_V, D_MODEL)),
            const((1, D_MODEL)), const((1, D_MODEL)),
            const((N_EXPERTS, D_MODEL)), const((N_EXPERTS, 1)),
        ],
        out_specs=[
            pl.BlockSpec((tm, D_MODEL), lambda i: (i, 0)),
            pl.BlockSpec((tm, D_PACK), lambda i: (i, 0)),
            pl.BlockSpec((TOP_K, tm), lambda i: (0, i)),
            pl.BlockSpec((TOP_K, tm), lambda i: (0, i)),
            pl.BlockSpec((TOP_K, tm), lambda i: (0, i)),
            const((N_EXPERTS, 1)),
        ],
        out_shape=[
            jax.ShapeDtypeStruct((t, D_MODEL), F32),
            jax.ShapeDtypeStruct((t, D_PACK), jnp.uint32),
            jax.ShapeDtypeStruct((TOP_K, t), jnp.int32),
            jax.ShapeDtypeStruct((TOP_K, t), F32),
            jax.ShapeDtypeStruct((TOP_K, t), jnp.int32),
            jax.ShapeDtypeStruct((N_EXPERTS, 1), jnp.int32),
        ],
        scratch_shapes=[pltpu.VMEM((N_EXPERTS, 1), F32)],
        compiler_params=_params(("arbitrary",)),
        name="outproj_router",
    )(y_lru, y_gdn, h0, wo1, wo2, g, b, w_router_t, rbias)


def _dest_kernel(e_ref, r_ref, ps_ref, d_ref):
    tm = e_ref.shape[1]
    ei = lax.broadcasted_iota(jnp.int32, (N_EXPERTS, tm), 0)
    rows = []
    for k in range(TOP_K):
        hit = ei == e_ref[k:k + 1, :]
        rows.append(jnp.sum(jnp.where(hit, ps_ref[...], 0), axis=0, keepdims=True))
    d_ref[...] = jnp.concatenate(rows, axis=0) + r_ref[...]


def _dest(top_e, rank, pad_start, tm):
    t = top_e.shape[1]
    blk = pl.BlockSpec((TOP_K, tm), lambda i: (0, i))
    return pl.pallas_call(
        _dest_kernel,
        grid=(t // tm,),
        in_specs=[blk, blk, pl.BlockSpec((N_EXPERTS, 1), lambda i: (0, 0))],
        out_specs=blk,
        out_shape=jax.ShapeDtypeStruct((TOP_K, t), jnp.int32),
        compiler_params=_params(("arbitrary",)),
        name="moe_dest",
    )(top_e, rank, pad_start)


def _sc_scatter_rows(rows, idx, n_out, chunk):
    n_copies, t = idx.shape
    d = rows.shape[1]
    per_worker = t // SC_WORKERS
    n_chunks = per_worker // chunk
    mesh = plsc.VectorSubcoreMesh(core_axis_name="c", subcore_axis_name="s")
    idx_flat = idx.reshape(n_copies * t)

    @functools.partial(
        pl.kernel, mesh=mesh,
        out_type=jax.ShapeDtypeStruct((n_out, d), rows.dtype),
        scratch_types=[pltpu.VMEM((chunk,), jnp.int32) for _ in range(n_copies)] + [
            pltpu.VMEM((chunk, d), rows.dtype),
            pltpu.SemaphoreType.DMA,
        ],
    )
    def scatter(rows_hbm, idx_hbm, out_hbm, *scratch):
        idx_v = scratch[:n_copies]
        rows_v, sem = scratch[n_copies:]
        wid = lax.axis_index("s") * SC_CORES + lax.axis_index("c")
        base = wid * per_worker

        @pl.loop(0, n_chunks)
        def _(j):
            off = base + j * chunk
            for k in range(n_copies):
                pltpu.sync_copy(idx_hbm.at[pl.ds(k * t + off, chunk)], idx_v[k])
            pltpu.sync_copy(rows_hbm.at[pl.ds(off, chunk)], rows_v)
            copies = [pltpu.async_copy(rows_v, out_hbm.at[idx_v[k]], sem) for k in range(n_copies)]
            for cp in copies:
                cp.wait()

    return scatter(rows, idx_flat)


def _expert_kernel(be_ref, nv_ref, first_ref, slot_ref, next_ref, nu_ref,
                   xs_ref, wg_hbm, wu_hbm, wd_hbm, ys_ref, wg_f, wu_f, wd_f, wgu_b, wd_b, sem):
    i = pl.program_id(0)

    def fetch(e, slot):
        return (pltpu.make_async_copy(wg_hbm.at[e], wg_f.at[slot], sem.at[slot]),
                pltpu.make_async_copy(wu_hbm.at[e], wu_f.at[slot], sem.at[slot]),
                pltpu.make_async_copy(wd_hbm.at[e], wd_f.at[slot], sem.at[slot]))

    @pl.when(i < nu_ref[0])
    def _():
        e = be_ref[i]
        slot = slot_ref[i]

        @pl.when(first_ref[i] == 1)
        def _():
            @pl.when(i == 0)
            def _():
                for cp in fetch(e, slot):
                    cp.start()

            for cp in fetch(e, slot):
                cp.wait()

            @pl.when(next_ref[i] >= 0)
            def _():
                for cp in fetch(next_ref[i], 1 - slot):
                    cp.start()

            wgu_b[:, :D_EXPERT] = wg_f[slot].astype(BF16)
            wgu_b[:, D_EXPERT:] = wu_f[slot].astype(BF16)
            wd_b[...] = wd_f[slot].astype(BF16)

        n = xs_ref.shape[0] // EXPERT_BANDS
        bands = [slice(j * n, (j + 1) * n) for j in range(EXPERT_BANDS)]
        row = lax.broadcasted_iota(jnp.int32, (n, D_PACK), 0)
        xs = [_unpack_rows(jnp.where(row + j * n < nv_ref[i], xs_ref[r, :], jnp.uint32(0)))
              for j, r in enumerate(bands)]
        gus = [_dot(x_hi.astype(BF16), wgu_b[:D_PACK, :]) + _dot(x_lo.astype(BF16), wgu_b[D_PACK:, :])
               for x_hi, x_lo in xs]
        hs = [_silu(gu[:, :D_EXPERT]) * gu[:, D_EXPERT:] for gu in gus]
        ys = [_dot(h.astype(BF16), wd_b[...]) for h in hs]
        for r, y in zip(bands, ys):
            ys_ref[r, :] = _pack_rows(y)


def _experts(blk_e, n_valid, first, slot, next_e, n_used, xs, w_gate, w_up, w_down):
    n_rows = xs.shape[0]
    n_blocks = n_rows // MOE_BLOCK
    blk = lambda i, be, nv, fi, sl, nx, nu: (jnp.minimum(i, nu[0] - 1), 0)
    return pl.pallas_call(
        _expert_kernel,
        grid_spec=pltpu.PrefetchScalarGridSpec(
            num_scalar_prefetch=6,
            grid=(n_blocks,),
            in_specs=[
                pl.BlockSpec((MOE_BLOCK, D_PACK), blk),
                pl.BlockSpec(memory_space=pl.ANY),
                pl.BlockSpec(memory_space=pl.ANY),
                pl.BlockSpec(memory_space=pl.ANY),
            ],
            out_specs=pl.BlockSpec((MOE_BLOCK, D_PACK), blk),
            scratch_shapes=[
                pltpu.VMEM((2, D_MODEL, D_EXPERT), F32),
                pltpu.VMEM((2, D_MODEL, D_EXPERT), F32),
                pltpu.VMEM((2, D_EXPERT, D_MODEL), F32),
                pltpu.VMEM((D_MODEL, 2 * D_EXPERT), BF16),
                pltpu.VMEM((D_EXPERT, D_MODEL), BF16),
                pltpu.SemaphoreType.DMA((2,)),
            ],
        ),
        out_shape=jax.ShapeDtypeStruct((n_rows, D_PACK), jnp.uint32),
        compiler_params=_params(("arbitrary",)),
        name="moe_experts",
    )(blk_e, n_valid, first, slot, next_e, n_used, xs, w_gate, w_up, w_down)


def _sc_gather_rows(table, idx, chunk):
    n_idx = idx.shape[0]
    d = table.shape[1]
    per_worker = n_idx // SC_WORKERS
    n_chunks = per_worker // chunk
    assert n_chunks % 2 == 0 and n_chunks * chunk * SC_WORKERS == n_idx
    mesh = plsc.VectorSubcoreMesh(core_axis_name="c", subcore_axis_name="s")

    @functools.partial(
        pl.kernel, mesh=mesh,
        out_type=jax.ShapeDtypeStruct((n_idx, d), table.dtype),
        scratch_types=[
            pltpu.VMEM((chunk,), jnp.int32), pltpu.VMEM((chunk,), jnp.int32),
            pltpu.VMEM((chunk, d), table.dtype), pltpu.VMEM((chunk, d), table.dtype),
            pltpu.SemaphoreType.DMA, pltpu.SemaphoreType.DMA, pltpu.SemaphoreType.DMA, pltpu.SemaphoreType.DMA,
        ],
    )
    def gather(table_hbm, idx_hbm, out_hbm, idx_v0, idx_v1, rows_v0, rows_v1, gsem0, gsem1, osem0, osem1):
        idx_v, rows_v, gsem, osem = (idx_v0, idx_v1), (rows_v0, rows_v1), (gsem0, gsem1), (osem0, osem1)
        wid = lax.axis_index("s") * SC_CORES + lax.axis_index("c")
        base = wid * per_worker

        def gather_copy(slot):
            return pltpu.make_async_copy(table_hbm.at[idx_v[slot]], rows_v[slot], gsem[slot])

        def out_copy(c, slot):
            return pltpu.make_async_copy(rows_v[slot], out_hbm.at[pl.ds(base + c * chunk, chunk)], osem[slot])

        def start_gather(c, slot):
            pltpu.sync_copy(idx_hbm.at[pl.ds(base + c * chunk, chunk)], idx_v[slot])
            gather_copy(slot).start()

        start_gather(0, 0)

        @pl.loop(0, n_chunks, step=2)
        def _(j):
            for b in range(2):
                c = j + b
                cur, other = b, 1 - b

                @pl.when(c >= 1)
                def _():
                    out_copy(c - 1, other).wait()

                @pl.when(c + 1 < n_chunks)
                def _():
                    start_gather(c + 1, other)

                gather_copy(cur).wait()
                out_copy(c, cur).start()

        out_copy(n_chunks - 1, 1).wait()

    return gather(table, idx)


def _combine_kernel(h1_ref, wts_ref, wsgu_ref, wsd_ref, g_ref, b_ref, yg_ref, out_ref):
    h1 = h1_ref[...]
    gu = _dot(h1.astype(BF16), wsgu_ref[...])
    hs = _silu(gu[:, :D_SHARED]) * gu[:, D_SHARED:]
    acc = DEEPNORM_ALPHA * h1 + _dot(hs.astype(BF16), wsd_ref[...])
    wts = wts_ref[...]
    acc_hi = acc[:, :D_PACK]
    acc_lo = acc[:, D_PACK:]
    for k in range(TOP_K):
        y_hi, y_lo = _unpack_rows(yg_ref[k])
        acc_hi = acc_hi + y_hi * wts[:, k:k + 1]
        acc_lo = acc_lo + y_lo * wts[:, k:k + 1]
    out_ref[...] = _layer_norm(jnp.concatenate([acc_hi, acc_lo], axis=1), g_ref[...], b_ref[...])


def _combine(h1, wts_t, ws_gu, ws_down, g, b, yg, tm):
    t = h1.shape[0]
    const = lambda shape: pl.BlockSpec(shape, lambda i: (0,) * len(shape))
    return pl.pallas_call(
        _combine_kernel,
        grid=(t // tm,),
        in_specs=[
            pl.BlockSpec((tm, D_MODEL), lambda i: (i, 0)),
            pl.BlockSpec((tm, TOP_K), lambda i: (i, 0)),
            const((D_MODEL, 2 * D_SHARED)), const((D_SHARED, D_MODEL)),
            const((1, D_MODEL)), const((1, D_MODEL)),
            pl.BlockSpec((TOP_K, tm, D_PACK), lambda i: (0, i, 0)),
        ],
        out_specs=pl.BlockSpec((tm, D_MODEL), lambda i: (i, 0)),
        out_shape=jax.ShapeDtypeStruct((t, D_MODEL), F32),
        compiler_params=_params(("arbitrary",)),
        name="moe_combine",
    )(h1, wts_t, ws_gu, ws_down, g, b, yg)


def _block_diag(w):
    nb, bi, bo = w.shape
    eye = jnp.eye(nb, dtype=w.dtype)
    return (eye[:, None, :, None] * w[:, :, None, :]).reshape(nb * bi, nb * bo)


def _pad_lanes(v, offset, width):
    return jnp.zeros((1, width), F32).at[0, offset:offset + v.shape[0]].set(v)


def _layer(h_in_x, l, p, tiles):
    bsz, seq, _ = h_in_x.shape
    t = bsz * seq
    row = lambda v: v.reshape(1, -1)

    w_in = p['w_in'][l]
    w_main = w_in[:, :N_MAIN].astype(BF16)
    w_small = jnp.zeros((D_MODEL, LANES), F32).at[:, :2 * GDN_HEADS].set(w_in[:, N_MAIN:])
    zeros = lambda n: jnp.zeros((CONV_WIDTH, n), F32)
    conv_w = jnp.concatenate([p['lru_conv_w'][l], zeros(LRU_WIDTH), p['gdn_conv_w'][l], zeros(GDN_V)], 1)
    conv_b = jnp.zeros((1, N_MAIN), F32).at[0, :LRU_WIDTH].set(p['lru_conv_b'][l])
    h0, proj, small, small_t = _inproj(h_in_x.reshape(t, D_MODEL), row(p['ln_g']), row(p['ln_b']),
                                       w_main, w_small, conv_w, conv_b, tiles['inproj'], seq)
    proj3 = proj.reshape(bsz, seq, N_MAIN)

    w_gates = jnp.concatenate([_block_diag(p['lru_w_rg'][l]), _block_diag(p['lru_w_ig'][l])], 1).astype(BF16)
    b_gates = jnp.concatenate([p['lru_b_rg'][l], p['lru_b_ig'][l]]).reshape(1, -1)
    y_lru = _lru(proj3, w_gates, b_gates, row(p['lru_lambda'][l]), row(p['lru_out_g'][l]), tiles['lru'])

    rows = GDN_GROUP * GDN_CHUNK
    small3 = small.reshape(bsz, seq, LANES)
    smallt3 = small_t.reshape(8, bsz, seq // rows, rows).transpose(1, 2, 0, 3)
    a_log, dt_bias = p['gdn_a_log'][l], p['gdn_dt_bias'][l]
    alr = _pad_lanes(a_log, GDN_HEADS, LANES)
    dtr = _pad_lanes(dt_bias, GDN_HEADS, LANES)
    alc = _pad_lanes(a_log, GDN_HEADS, 8).reshape(8, 1)
    dtc = _pad_lanes(dt_bias, GDN_HEADS, 8).reshape(8, 1)
    y_gdn = _gdn(proj3, small3, smallt3, alr, dtr, alc, dtc, row(p['gdn_norm_w'][l]), tiles['gdn_nb'])

    w_out = p['w_out'][l].astype(BF16)
    h1, h1p, top_e, wts, rank, counts = _router(
        y_lru.reshape(t, LRU_WIDTH), y_gdn.reshape(t, GDN_V), h0, w_out[:LRU_WIDTH], w_out[LRU_WIDTH:],
        row(p['ln1_g'][l]), row(p['ln1_b'][l]), p['w_router'][l].T, p['router_bias'][l].reshape(-1, 1),
        tiles['router'])

    counts = counts[:, 0]
    padded = (counts + MOE_BLOCK - 1) // MOE_BLOCK * MOE_BLOCK
    pad_end = jnp.cumsum(padded)
    pad_start = pad_end - padded
    n_blocks = (t * TOP_K) // MOE_BLOCK + N_EXPERTS
    n_rows = n_blocks * MOE_BLOCK
    n_used = (pad_end[-1] // MOE_BLOCK).astype(jnp.int32)
    blk_ids = jnp.minimum(jnp.arange(n_blocks, dtype=jnp.int32), n_used - 1)
    blk_e = jnp.minimum(jnp.sum(pad_end[None, :] <= (blk_ids * MOE_BLOCK)[:, None], axis=1),
                        N_EXPERTS - 1).astype(jnp.int32)

    dest = _dest(top_e, rank, pad_start.reshape(-1, 1), tiles['dest'])
    n_valid = jnp.clip(counts[blk_e] - (blk_ids * MOE_BLOCK - pad_start[blk_e]), 0, MOE_BLOCK).astype(jnp.int32)
    xs = _sc_scatter_rows(h1p, dest, n_rows, SC_CHUNK)
    active = jnp.arange(n_blocks, dtype=jnp.int32) < n_used
    first = (active & jnp.concatenate([jnp.ones((1,), bool), blk_e[1:] != blk_e[:-1]])).astype(jnp.int32)
    slot = ((jnp.cumsum(first) - 1) % 2).astype(jnp.int32)
    used = counts > 0
    later = jnp.where(used[None, :] & (jnp.arange(N_EXPERTS)[None, :] > jnp.arange(N_EXPERTS)[:, None]),
                      jnp.arange(N_EXPERTS, dtype=jnp.int32)[None, :], N_EXPERTS)
    next_used = jnp.min(later, axis=1)
    next_e = jnp.where(next_used < N_EXPERTS, next_used, -1)[blk_e].astype(jnp.int32)
    ys = _experts(blk_e, n_valid, first, slot, next_e, n_used.reshape(1), xs,
                  p['w_gate'][l], p['w_up'][l], p['w_down'][l])
    ws_gu = jnp.concatenate([p['ws_gate'][l], p['ws_up'][l]], 1).astype(BF16)
    yg = _sc_gather_rows(ys, dest.reshape(TOP_K * t), SC_CHUNK).reshape(TOP_K, t, D_PACK)
    out = _combine(h1, wts.T, ws_gu, p['ws_down'][l].astype(BF16),
                   row(p['ln2_g'][l]), row(p['ln2_b'][l]), yg, tiles['combine'])
    return out.reshape(bsz, seq, D_MODEL)


def _tiles(bsz, seq):
    t = bsz * seq
    return {
        'inproj': min(512, t),
        'lru': min(256, seq),
        'gdn_nb': bsz,
        'router': min(512, t),
        'dest': min(512, t),
        'combine': min(256, t),
    }


def kernel(x, ln_in_g, ln_in_b, w_in, lru_conv_w, lru_conv_b, lru_w_rg, lru_b_rg, lru_w_ig, lru_b_ig,
           lru_lambda, lru_out_g, gdn_conv_w, gdn_a_log, gdn_dt_bias, gdn_norm_w, w_out, ln1_g, ln1_b,
           w_router, router_bias, w_gate, w_up, w_down, ws_gate, ws_up, ws_down, ln2_g, ln2_b):
    assert w_in.shape[0] == DEPTH == 1
    p = dict(ln_g=ln_in_g, ln_b=ln_in_b, w_in=w_in, lru_conv_w=lru_conv_w, lru_conv_b=lru_conv_b,
             lru_w_rg=lru_w_rg, lru_b_rg=lru_b_rg, lru_w_ig=lru_w_ig, lru_b_ig=lru_b_ig,
             lru_lambda=lru_lambda, lru_out_g=lru_out_g, gdn_conv_w=gdn_conv_w, gdn_a_log=gdn_a_log,
             gdn_dt_bias=gdn_dt_bias, gdn_norm_w=gdn_norm_w, w_out=w_out, ln1_g=ln1_g, ln1_b=ln1_b,
             w_router=w_router, router_bias=router_bias, w_gate=w_gate, w_up=w_up, w_down=w_down,
             ws_gate=ws_gate, ws_up=ws_up, ws_down=ws_down, ln2_g=ln2_g, ln2_b=ln2_b)
    bsz, seq, _ = x.shape
    return _layer(x, 0, p, _tiles(bsz, seq))
```

```python
import functools

import jax
import jax.numpy as jnp
from jax import lax
from jax.experimental import pallas as pl
from jax.experimental.pallas import tpu as pltpu
from jax.experimental.pallas import tpu_sc as plsc

F32 = jnp.float32
BF16 = jnp.bfloat16

D_MODEL = 1024
LRU_WIDTH = 512
LRU_BLOCKS = 8
LRU_C = 8.0
CONV_WIDTH = 4
GDN_HEADS = 4
GDN_DK = 128
GDN_DV = 128
GDN_CHUNK = 64
GDN_GROUP = 2
GDN_QK = GDN_HEADS * GDN_DK
GDN_V = GDN_HEADS * GDN_DV
N_MAIN = 2 * LRU_WIDTH + 2 * GDN_QK + 2 * GDN_V
N_EXPERTS = 256
TOP_K = 8
N_GROUPS = 8
GROUP_SIZE = N_EXPERTS // N_GROUPS
TOPK_GROUPS = 4
D_EXPERT = 256
D_SHARED = 256
ROUTED_SCALE = 2.5
MOE_BLOCK = 512
D_PACK = D_MODEL // 2
LN_EPS = 1e-5
NORM_EPS = 1e-6
DEPTH = 1
DEEPNORM_ALPHA = (2.0 * DEPTH) ** 0.25

HALO = 8
CONV_GROUP = 512
CONV_GROUPS = (0, 2, 3, 4)
LANES = 128
VMEM_LIMIT = 56 * 1024 * 1024
ROUTER_SUB = 256
EXPERT_BANDS = 4
SC_CORES = 2
SC_WORKERS = 32
SC_CHUNK = 64

NN = (((1,), (0,)), ((), ()))
NT = (((1,), (1,)), ((), ()))
TN = (((0,), (0,)), ((), ()))


def _dot(a, b, dims=NN):
    return lax.dot_general(a, b, dims, preferred_element_type=F32)


def _split(a):
    hi = a.astype(BF16)
    lo = (a - hi.astype(F32)).astype(BF16)
    return hi, lo


def _dot3(a, b, dims=NN):
    ah, al = _split(a)
    bh, bl = _split(b)
    return _dot(ah, bh, dims) + (_dot(ah, bl, dims) + _dot(al, bh, dims))


def _layer_norm(x, g, b):
    mu = jnp.mean(x, -1, keepdims=True)
    xc = x - mu
    var = jnp.mean(xc * xc, -1, keepdims=True)
    return xc * lax.rsqrt(var + LN_EPS) * g + b


def _sigmoid(x):
    return 1.0 / (1.0 + jnp.exp(-x))


def _silu(x):
    return x * _sigmoid(x)


def _softplus(x):
    return jnp.maximum(x, 0.0) + jnp.log1p(jnp.exp(-jnp.abs(x)))


def _gelu_tanh(x):
    c = 0.7978845608028654
    return x * (0.5 * (1.0 + jnp.tanh(c * (x + 0.044715 * (x * x * x)))))


def _pack_rows(x):
    hi = lax.bitcast_convert_type(x[:, :D_PACK].astype(BF16).astype(F32), jnp.uint32)
    lo = lax.bitcast_convert_type(x[:, D_PACK:].astype(BF16).astype(F32), jnp.uint32)
    return (hi & jnp.uint32(0xFFFF0000)) | (lo >> 16)


def _unpack_rows(w):
    hi = lax.bitcast_convert_type(w & jnp.uint32(0xFFFF0000), F32)
    lo = lax.bitcast_convert_type(w << 16, F32)
    return hi, lo


def _params(sem, **kw):
    return pltpu.CompilerParams(dimension_semantics=sem, vmem_limit_bytes=VMEM_LIMIT, **kw)


def _inproj_kernel(x_ref, g_ref, b_ref, w_ref, ws_ref, cw_ref, cb_ref,
                   h_ref, proj_ref, small_ref, smallt_ref, hist, *, tiles_per_seq):
    i = pl.program_id(0)
    tm = x_ref.shape[0]
    h = _layer_norm(x_ref[...], g_ref[...], b_ref[...])
    h_ref[...] = h
    hb = h.astype(BF16)

    @pl.when(i % tiles_per_seq == 0)
    def _():
        hist[...] = jnp.zeros_like(hist)

    for g in range(N_MAIN // CONV_GROUP):
        cols = slice(g * CONV_GROUP, (g + 1) * CONV_GROUP)
        p = _dot(hb, w_ref[:, cols])
        if g in CONV_GROUPS:
            xcat = jnp.concatenate([hist[:, cols], p], axis=0)
            acc = cb_ref[:, cols]
            for j in range(CONV_WIDTH):
                off = HALO - (CONV_WIDTH - 1) + j
                acc = acc + xcat[off:off + tm, :] * cw_ref[j:j + 1, cols]
            hist[:, cols] = p[tm - HALO:, :]
            p = acc
        proj_ref[:, cols] = p
    small = _dot3(h, ws_ref[...])
    small_ref[...] = small
    smallt_ref[...] = small.T[:smallt_ref.shape[0], :]


def _inproj(x2d, g, b, w_main, w_small, conv_w, conv_b, tm, seq):
    t = x2d.shape[0]
    return pl.pallas_call(
        functools.partial(_inproj_kernel, tiles_per_seq=seq // tm),
        grid=(t // tm,),
        in_specs=[
            pl.BlockSpec((tm, D_MODEL), lambda i: (i, 0)),
            pl.BlockSpec((1, D_MODEL), lambda i: (0, 0)),
            pl.BlockSpec((1, D_MODEL), lambda i: (0, 0)),
            pl.BlockSpec((D_MODEL, N_MAIN), lambda i: (0, 0)),
            pl.BlockSpec((D_MODEL, LANES), lambda i: (0, 0)),
            pl.BlockSpec((CONV_WIDTH, N_MAIN), lambda i: (0, 0)),
            pl.BlockSpec((1, N_MAIN), lambda i: (0, 0)),
        ],
        out_specs=[
            pl.BlockSpec((tm, D_MODEL), lambda i: (i, 0)),
            pl.BlockSpec((tm, N_MAIN), lambda i: (i, 0)),
            pl.BlockSpec((tm, LANES), lambda i: (i, 0)),
            pl.BlockSpec((8, tm), lambda i: (0, i)),
        ],
        out_shape=[
            jax.ShapeDtypeStruct((t, D_MODEL), F32),
            jax.ShapeDtypeStruct((t, N_MAIN), F32),
            jax.ShapeDtypeStruct((t, LANES), F32),
            jax.ShapeDtypeStruct((8, t), F32),
        ],
        scratch_shapes=[pltpu.VMEM((HALO, N_MAIN), F32)],
        compiler_params=_params(("arbitrary",)),
        name="ln_inproj",
    )(x2d, g, b, w_main, w_small, conv_w, conv_b)


def _shift_rows(x, d, fill):
    rows = x.shape[0]
    if d % 8 == 0:
        pad = jnp.full((d, x.shape[1]), fill, x.dtype)
        return jnp.concatenate([pad, x[:rows - d]], axis=0)
    rolled = pltpu.roll(x, d, 0)
    row = lax.broadcasted_iota(jnp.int32, x.shape, 0)
    return jnp.where(row < d, fill, rolled)


def _lru_kernel(xc_ref, gate_ref, wg_ref, bg_ref, lam_ref, og_ref, y_ref, hcarry):
    s = pl.program_id(1)
    rows = xc_ref.shape[0]

    @pl.when(s == 0)
    def _():
        hcarry[...] = jnp.zeros_like(hcarry)

    xc = xc_ref[...]
    gates = _dot(xc.astype(BF16), wg_ref[...]) + bg_ref[...]
    r = _sigmoid(gates[:, :LRU_WIDTH])
    i = _sigmoid(gates[:, LRU_WIDTH:])
    log_a = (-LRU_C) * r * _softplus(-lam_ref[...])
    a = jnp.exp(log_a)
    mult = jnp.sqrt(-jnp.tanh(log_a) * (a * a + 1.0))
    bv = mult * (i * xc)
    d = 1
    while d < rows:
        a_sh = _shift_rows(a, d, 1.0)
        b_sh = _shift_rows(bv, d, 0.0)
        bv = a * b_sh + bv
        a = a * a_sh
        d *= 2
    h = a * hcarry[...] + bv
    hcarry[...] = h[rows - 1:rows, :]
    y = h * _gelu_tanh(gate_ref[...])
    ms = jnp.mean(y * y, -1, keepdims=True)
    y_ref[...] = y * lax.rsqrt(ms + NORM_EPS) * og_ref[...]


def _lru(proj3, w_gates, b_gates, lam, out_g, ts):
    bsz, seq, _ = proj3.shape
    row = lambda n: pl.BlockSpec((1, n), lambda b, s: (0, 0))
    return pl.pallas_call(
        _lru_kernel,
        grid=(bsz, seq // ts),
        in_specs=[
            pl.BlockSpec((None, ts, LRU_WIDTH), lambda b, s: (b, s, 0)),
            pl.BlockSpec((None, ts, LRU_WIDTH), lambda b, s: (b, s, 1)),
            pl.BlockSpec((LRU_WIDTH, 2 * LRU_WIDTH), lambda b, s: (0, 0)),
            row(2 * LRU_WIDTH),
            row(LRU_WIDTH),
            row(LRU_WIDTH),
        ],
        out_specs=pl.BlockSpec((None, ts, LRU_WIDTH), lambda b, s: (b, s, 0)),
        out_shape=jax.ShapeDtypeStruct((bsz, seq, LRU_WIDTH), F32),
        scratch_shapes=[
            pltpu.VMEM((1, LRU_WIDTH), F32),
        ],
        compiler_params=_params(("arbitrary", "arbitrary")),
        name="rg_lru",
    )(proj3, proj3, w_gates, b_gates, lam, out_g)


def _bdot(a, b, dims=NN):
    return _dot(a.astype(BF16), b.astype(BF16), dims)


def _gdn_heads(args, norm_w):
    c = GDN_CHUNK
    r = GDN_GROUP * c
    ri = lax.broadcasted_iota(jnp.int32, (r, r), 0)
    ci = lax.broadcasted_iota(jnp.int32, (r, r), 1)
    same = (ri // c) == (ci // c)
    causal = same & (ri >= ci)
    strict = same & (ri > ci)
    upper = same & (ri <= ci)
    chunk_of_row = lax.broadcasted_iota(jnp.int32, (r, 1), 0) // c
    each = lambda f, *ls: [f(*xs) for xs in zip(*ls)]
    q, k, v, z, beta, g_col, g_row, st = [list(x) for x in zip(*args)]
    q = each(lambda x: x * lax.rsqrt(jnp.sum(x * x, -1, keepdims=True) + NORM_EPS) * (GDN_DK ** -0.5), q)
    k = each(lambda x: x * lax.rsqrt(jnp.sum(x * x, -1, keepdims=True) + NORM_EPS), k)
    gc_col = each(lambda g: jnp.sum(jnp.where(causal, g, 0.0), axis=1, keepdims=True), g_row)
    gc_row = each(lambda g: jnp.sum(jnp.where(upper, g, 0.0), axis=0, keepdims=True), g_col)
    decay = each(lambda gc, gr: jnp.exp(jnp.where(causal, gc - gr, -jnp.inf)), gc_col, gc_row)
    kb = each(lambda x, bt: x * bt, k, beta)
    vb = each(lambda x, bt: x * bt, v, beta)
    kk = each(lambda x, y: _bdot(x, y, NT), kb, k)
    a_mat = each(lambda m, d: jnp.where(strict, m * d, 0.0), kk, decay)
    e_col = each(jnp.exp, gc_col)
    rhs = each(lambda x, y, e: jnp.concatenate([x, y * e], axis=1), vb, kb, e_col)
    sol = each(lambda rr, a: rr - _bdot(a, rr), rhs, a_mat)
    p = a_mat
    for _ in range(5):
        p = each(lambda x: _bdot(x, x), p)
        sol = each(lambda x, y: y + _bdot(x, y), p, sol)
    qk = each(lambda x, y: _bdot(x, y, NT), q, k)
    qk = each(lambda m, d: jnp.where(causal, m * d, 0.0), qk, decay)
    q_dec = each(lambda x, e: x * e, q, e_col)
    g_last = [each(lambda gc: gc[(j + 1) * c - 1:(j + 1) * c, :], gc_col) for j in range(GDN_GROUP)]

    def last_of_own_chunk(*gl):
        out = gl[-1]
        for j in range(GDN_GROUP - 2, -1, -1):
            out = jnp.where(chunk_of_row == j, gl[j], out)
        return out

    g_end = each(last_of_own_chunk, *g_last)
    k_dec = each(lambda x, ge, gc: x * jnp.exp(ge - gc), k, g_end, gc_col)
    qs_parts, v_parts = [], []
    for j in range(GDN_GROUP):
        rows = slice(j * c, (j + 1) * c)
        ws = each(lambda x, s: _bdot(x[rows, GDN_DV:], s), sol, st)
        qs_parts.append(each(lambda x, s: _bdot(x[rows], s), q_dec, st))
        v_new = each(lambda x, w: x[rows, :GDN_DV] - w, sol, ws)
        v_parts.append(v_new)
        kv = each(lambda x, vn: _bdot(x[rows], vn, TN), k_dec, v_new)
        st = each(lambda s, gl, d: s * jnp.exp(gl) + d, st, g_last[j], kv)
    qs = each(lambda *parts: jnp.concatenate(parts, axis=0), *qs_parts)
    v_all = each(lambda *parts: jnp.concatenate(parts, axis=0), *v_parts)
    o = each(lambda a, m, vn: a + _bdot(m, vn), qs, qk, v_all)
    o = each(lambda x: x * lax.rsqrt(jnp.mean(x * x, -1, keepdims=True) + NORM_EPS) * norm_w, o)
    o = each(lambda x, zz: x * _silu(zz), o, z)
    return list(zip(o, st))


def _gdn_kernel(q_ref, k_ref, v_ref, z_ref, sm_ref, smt_ref,
                alr_ref, dtr_ref, alc_ref, dtc_ref, nw_ref, y_ref, state):
    n = pl.program_id(1)
    c = GDN_GROUP * GDN_CHUNK
    nb = q_ref.shape[0]
    first = n == 0

    @pl.when(first)
    def _():
        state[...] = jnp.zeros_like(state)

    norm_w = nw_ref[...]

    args = []
    for b in range(nb):
        q_all = _silu(q_ref[b])
        k_all = _silu(k_ref[b])
        v_all = _silu(v_ref[b])
        z_all = z_ref[b]
        sm = sm_ref[b]
        beta_all = _sigmoid(sm)
        g_cols = -jnp.exp(alr_ref[...]) * _softplus(sm + dtr_ref[...])
        g_rows = -jnp.exp(alc_ref[...]) * _softplus(smt_ref[b] + dtc_ref[...])
        for hd in range(GDN_HEADS):
            sl = slice(hd * GDN_DK, (hd + 1) * GDN_DK)
            args.append((q_all[:, sl], k_all[:, sl], v_all[:, sl], z_all[:, sl],
                         beta_all[:, hd:hd + 1],
                         g_cols[:, GDN_HEADS + hd:GDN_HEADS + hd + 1],
                         g_rows[GDN_HEADS + hd:GDN_HEADS + hd + 1, :],
                         state[b, hd]))
    outs = _gdn_heads(args, norm_w)
    for b in range(nb):
        for hd in range(GDN_HEADS):
            o, st_new = outs[b * GDN_HEADS + hd]
            state[b, hd] = st_new
            y_ref[b, :, hd * GDN_DK:(hd + 1) * GDN_DK] = o


def _gdn(proj3, small3, smallt3, alr, dtr, alc, dtc, norm_w, nb):
    bsz, seq, _ = proj3.shape
    c = GDN_GROUP * GDN_CHUNK
    nch = seq // c
    col = lambda j: pl.BlockSpec((nb, c, GDN_QK), lambda b, n: (b, n, j))
    const = lambda shape: pl.BlockSpec(shape, lambda b, n: (0,) * len(shape))
    return pl.pallas_call(
        _gdn_kernel,
        grid=(bsz // nb, nch),
        in_specs=[
            col(2), col(3), col(4), col(5),
            pl.BlockSpec((nb, c, LANES), lambda b, n: (b, n, 0)),
            pl.BlockSpec((nb, None, 8, c), lambda b, n: (b, n, 0, 0)),
            const((1, LANES)), const((1, LANES)), const((8, 1)), const((8, 1)),
            const((1, GDN_DV)),
        ],
        out_specs=pl.BlockSpec((nb, c, GDN_V), lambda b, n: (b, n, 0)),
        out_shape=jax.ShapeDtypeStruct((bsz, seq, GDN_V), F32),
        scratch_shapes=[
            pltpu.VMEM((nb, GDN_HEADS, GDN_DK, GDN_DV), F32),
        ],
        compiler_params=_params(("arbitrary", "arbitrary")),
        name="gated_deltanet",
    )(proj3, proj3, proj3, proj3, small3, smallt3, alr, dtr, alc, dtc, norm_w)


def _pick_experts(logits, rbias):
    n = logits.shape[1]
    scores = _sigmoid(logits)
    choice = scores + rbias
    neg = -jnp.inf
    gs_rows = []
    sub = lax.broadcasted_iota(jnp.int32, (GROUP_SIZE, n), 0).astype(F32)
    for g in range(N_GROUPS):
        cg = choice[g * GROUP_SIZE:(g + 1) * GROUP_SIZE, :]
        m1 = jnp.max(cg, axis=0, keepdims=True)
        i1 = jnp.min(jnp.where(cg == m1, sub, float(GROUP_SIZE)), axis=0, keepdims=True)
        m2 = jnp.max(jnp.where(sub == i1, neg, cg), axis=0, keepdims=True)
        gs_rows.append(m1 + m2)
    gs = jnp.concatenate(gs_rows, axis=0)
    gi = lax.broadcasted_iota(jnp.int32, (N_GROUPS, n), 0).astype(F32)
    gsel = jnp.zeros((N_GROUPS, n), jnp.bool_)
    for _ in range(TOPK_GROUPS):
        m = jnp.max(gs, axis=0, keepdims=True)
        idx = jnp.min(jnp.where(gs == m, gi, float(N_GROUPS)), axis=0, keepdims=True)
        hit = gi == idx
        gsel = jnp.logical_or(gsel, hit)
        gs = jnp.where(hit, neg, gs)
    masked = jnp.concatenate(
        [jnp.where(gsel[g:g + 1, :], choice[g * GROUP_SIZE:(g + 1) * GROUP_SIZE, :], neg)
         for g in range(N_GROUPS)], axis=0)
    ei = lax.broadcasted_iota(jnp.int32, (N_EXPERTS, n), 0).astype(F32)
    hits, e_rows, w_rows = [], [], []
    multi = jnp.zeros((N_EXPERTS, n), F32)
    for _ in range(TOP_K):
        m = jnp.max(masked, axis=0, keepdims=True)
        idx = jnp.min(jnp.where(masked == m, ei, float(N_EXPERTS)), axis=0, keepdims=True)
        hit = ei == idx
        hits.append(hit)
        e_rows.append(idx)
        w_rows.append(jnp.sum(jnp.where(hit, scores, 0.0), axis=0, keepdims=True))
        multi = multi + hit.astype(F32)
        masked = jnp.where(hit, neg, masked)
    wts = jnp.concatenate(w_rows, axis=0)
    wts = wts / (jnp.sum(wts, axis=0, keepdims=True) + 1e-20) * ROUTED_SCALE
    return jnp.concatenate(e_rows, axis=0), wts, hits, multi


def _router_kernel(yl_ref, yg_ref, h0_ref, wo1_ref, wo2_ref, g_ref, b_ref, wrt_ref, rb_ref,
                   h1_ref, h1p_ref, e_ref, w_ref, rank_ref, cnt_ref, carry):
    i = pl.program_id(0)
    tm = h0_ref.shape[0]
    n = min(ROUTER_SUB, tm)
    subs = [slice(j * n, (j + 1) * n) for j in range(tm // n)]

    @pl.when(i == 0)
    def _():
        carry[...] = jnp.zeros_like(carry)

    mixes = [_dot(yl_ref[r, :].astype(BF16), wo1_ref[...]) + _dot(yg_ref[r, :].astype(BF16), wo2_ref[...])
             for r in subs]
    h1s = [_layer_norm(DEEPNORM_ALPHA * h0_ref[r, :] + mix, g_ref[...], b_ref[...])
           for r, mix in zip(subs, mixes)]
    for r, h1 in zip(subs, h1s):
        h1_ref[r, :] = h1
        h1p_ref[r, :] = _pack_rows(h1)
    logits = [_dot3(wrt_ref[...], h1, NT) for h1 in h1s]
    picks = [_pick_experts(lg, rb_ref[...]) for lg in logits]
    ti = lax.broadcasted_iota(jnp.int32, (n, n), 0)
    tj = lax.broadcasted_iota(jnp.int32, (n, n), 1)
    before = (ti < tj).astype(BF16)
    cums = [_dot(multi.astype(BF16), before) for _, _, _, multi in picks]
    base = carry[...]
    for r, (e_rows, wts, hits, multi), cum in zip(subs, picks, cums):
        cum = cum + base
        r_rows = [jnp.sum(jnp.where(hit, cum, 0.0), axis=0, keepdims=True) for hit in hits]
        base = base + jnp.sum(multi, axis=1, keepdims=True)
        e_ref[:, r] = e_rows.astype(jnp.int32)
        w_ref[:, r] = wts
        rank_ref[:, r] = jnp.concatenate(r_rows, axis=0).astype(jnp.int32)
    carry[...] = base
    cnt_ref[...] = base.astype(jnp.int32)


def _router(y_lru, y_gdn, h0, wo1, wo2, g, b, w_router_t, rbias, tm):
    t = h0.shape[0]
    const = lambda shape: pl.BlockSpec(shape, lambda i: (0,) * len(shape))
    return pl.pallas_call(
        _router_kernel,
        grid=(t // tm,),
        in_specs=[
            pl.BlockSpec((tm, LRU_WIDTH), lambda i: (i, 0)),
            pl.BlockSpec((tm, GDN_V), lambda i: (i, 0)),
            pl.BlockSpec((tm, D_MODEL), lambda i: (i, 0)),
            const((LRU_WIDTH, D_MODEL)), const((GDN_V, D_MODEL)),
            const((1, D_MODEL)), const((1, D_MODEL)),
            const((N_EXPERTS, D_MODEL)), const((N_EXPERTS, 1)),
        ],
        out_specs=[
            pl.BlockSpec((tm, D_MODEL), lambda i: (i, 0)),
            pl.BlockSpec((tm, D_PACK), lambda i: (i, 0)),
            pl.BlockSpec((TOP_K, tm), lambda i: (0, i)),
            pl.BlockSpec((TOP_K, tm), lambda i: (0, i)),
            pl.BlockSpec((TOP_K, tm), lambda i: (0, i)),
            const((N_EXPERTS, 1)),
        ],
        out_shape=[
            jax.ShapeDtypeStruct((t, D_MODEL), F32),
            jax.ShapeDtypeStruct((t, D_PACK), jnp.uint32),
            jax.ShapeDtypeStruct((TOP_K, t), jnp.int32),
            jax.ShapeDtypeStruct((TOP_K, t), F32),
            jax.ShapeDtypeStruct((TOP_K, t), jnp.int32),
            jax.ShapeDtypeStruct((N_EXPERTS, 1), jnp.int32),
        ],
        scratch_shapes=[pltpu.VMEM((N_EXPERTS, 1), F32)],
        compiler_params=_params(("arbitrary",)),
        name="outproj_router",
    )(y_lru, y_gdn, h0, wo1, wo2, g, b, w_router_t, rbias)


def _dest_kernel(e_ref, r_ref, ps_ref, d_ref):
    tm = e_ref.shape[1]
    ei = lax.broadcasted_iota(jnp.int32, (N_EXPERTS, tm), 0)
    rows = []
    for k in range(TOP_K):
        hit = ei == e_ref[k:k + 1, :]
        rows.append(jnp.sum(jnp.where(hit, ps_ref[...], 0), axis=0, keepdims=True))
    d_ref[...] = jnp.concatenate(rows, axis=0) + r_ref[...]


def _dest(top_e, rank, pad_start, tm):
    t = top_e.shape[1]
    blk = pl.BlockSpec((TOP_K, tm), lambda i: (0, i))
    return pl.pallas_call(
        _dest_kernel,
        grid=(t // tm,),
        in_specs=[blk, blk, pl.BlockSpec((N_EXPERTS, 1), lambda i: (0, 0))],
        out_specs=blk,
        out_shape=jax.ShapeDtypeStruct((TOP_K, t), jnp.int32),
        compiler_params=_params(("arbitrary",)),
        name="moe_dest",
    )(top_e, rank, pad_start)


def _sc_scatter_rows(rows, idx, n_out, chunk):
    n_copies, t = idx.shape
    d = rows.shape[1]
    per_worker = t // SC_WORKERS
    n_chunks = per_worker // chunk
    mesh = plsc.VectorSubcoreMesh(core_axis_name="c", subcore_axis_name="s")
    idx_flat = idx.reshape(n_copies * t)

    @functools.partial(
        pl.kernel, mesh=mesh,
        out_type=jax.ShapeDtypeStruct((n_out, d), rows.dtype),
        scratch_types=[pltpu.VMEM((chunk,), jnp.int32) for _ in range(n_copies)] + [
            pltpu.VMEM((chunk, d), rows.dtype),
            pltpu.SemaphoreType.DMA,
        ],
    )
    def scatter(rows_hbm, idx_hbm, out_hbm, *scratch):
        idx_v = scratch[:n_copies]
        rows_v, sem = scratch[n_copies:]
        wid = lax.axis_index("s") * SC_CORES + lax.axis_index("c")
        base = wid * per_worker

        @pl.loop(0, n_chunks)
        def _(j):
            off = base + j * chunk
            for k in range(n_copies):
                pltpu.sync_copy(idx_hbm.at[pl.ds(k * t + off, chunk)], idx_v[k])
            pltpu.sync_copy(rows_hbm.at[pl.ds(off, chunk)], rows_v)
            copies = [pltpu.async_copy(rows_v, out_hbm.at[idx_v[k]], sem) for k in range(n_copies)]
            for cp in copies:
                cp.wait()

    return scatter(rows, idx_flat)


def _expert_kernel(be_ref, nv_ref, first_ref, slot_ref, next_ref, nu_ref,
                   xs_ref, wg_hbm, wu_hbm, wd_hbm, ys_ref, wg_f, wu_f, wd_f, wgu_b, wd_b, sem):
    i = pl.program_id(0)

    def fetch(e, slot):
        return (pltpu.make_async_copy(wg_hbm.at[e], wg_f.at[slot], sem.at[slot]),
                pltpu.make_async_copy(wu_hbm.at[e], wu_f.at[slot], sem.at[slot]),
                pltpu.make_async_copy(wd_hbm.at[e], wd_f.at[slot], sem.at[slot]))

    @pl.when(i < nu_ref[0])
    def _():
        e = be_ref[i]
        slot = slot_ref[i]

        @pl.when(first_ref[i] == 1)
        def _():
            @pl.when(i == 0)
            def _():
                for cp in fetch(e, slot):
                    cp.start()

            for cp in fetch(e, slot):
                cp.wait()

            @pl.when(next_ref[i] >= 0)
            def _():
                for cp in fetch(next_ref[i], 1 - slot):
                    cp.start()

            wgu_b[:, :D_EXPERT] = wg_f[slot].astype(BF16)
            wgu_b[:, D_EXPERT:] = wu_f[slot].astype(BF16)
            wd_b[...] = wd_f[slot].astype(BF16)

        n = xs_ref.shape[0] // EXPERT_BANDS
        row = lax.broadcasted_iota(jnp.int32, (n, D_PACK), 0)
        n_valid = nv_ref[i]

        def ffn(n_bands):
            bands = [slice(j * n, (j + 1) * n) for j in range(n_bands)]
            xs = [_unpack_rows(jnp.where(row + j * n < n_valid, xs_ref[r, :], jnp.uint32(0)))
                  for j, r in enumerate(bands)]
            gus = [_dot(x_hi.astype(BF16), wgu_b[:D_PACK, :]) + _dot(x_lo.astype(BF16), wgu_b[D_PACK:, :])
                   for x_hi, x_lo in xs]
            hs = [_silu(gu[:, :D_EXPERT]) * gu[:, D_EXPERT:] for gu in gus]
            ys = [_dot(h.astype(BF16), wd_b[...]) for h in hs]
            for r, y in zip(bands, ys):
                ys_ref[r, :] = _pack_rows(y)

        bands_needed = (n_valid + n - 1) // n
        for n_bands in range(1, EXPERT_BANDS + 1):
            pl.when(bands_needed == n_bands)(functools.partial(ffn, n_bands))


def _experts(blk_e, n_valid, first, slot, next_e, n_used, xs, w_gate, w_up, w_down):
    n_rows = xs.shape[0]
    n_blocks = n_rows // MOE_BLOCK
    blk = lambda i, be, nv, fi, sl, nx, nu: (jnp.minimum(i, nu[0] - 1), 0)
    return pl.pallas_call(
        _expert_kernel,
        grid_spec=pltpu.PrefetchScalarGridSpec(
            num_scalar_prefetch=6,
            grid=(n_blocks,),
            in_specs=[
                pl.BlockSpec((MOE_BLOCK, D_PACK), blk),
                pl.BlockSpec(memory_space=pl.ANY),
                pl.BlockSpec(memory_space=pl.ANY),
                pl.BlockSpec(memory_space=pl.ANY),
            ],
            out_specs=pl.BlockSpec((MOE_BLOCK, D_PACK), blk),
            scratch_shapes=[
                pltpu.VMEM((2, D_MODEL, D_EXPERT), F32),
                pltpu.VMEM((2, D_MODEL, D_EXPERT), F32),
                pltpu.VMEM((2, D_EXPERT, D_MODEL), F32),
                pltpu.VMEM((D_MODEL, 2 * D_EXPERT), BF16),
                pltpu.VMEM((D_EXPERT, D_MODEL), BF16),
                pltpu.SemaphoreType.DMA((2,)),
            ],
        ),
        out_shape=jax.ShapeDtypeStruct((n_rows, D_PACK), jnp.uint32),
        compiler_params=_params(("arbitrary",)),
        name="moe_experts",
    )(blk_e, n_valid, first, slot, next_e, n_used, xs, w_gate, w_up, w_down)


def _sc_gather_rows(table, idx, chunk):
    n_idx = idx.shape[0]
    d = table.shape[1]
    per_worker = n_idx // SC_WORKERS
    n_chunks = per_worker // chunk
    assert n_chunks % 2 == 0 and n_chunks * chunk * SC_WORKERS == n_idx
    mesh = plsc.VectorSubcoreMesh(core_axis_name="c", subcore_axis_name="s")

    @functools.partial(
        pl.kernel, mesh=mesh,
        out_type=jax.ShapeDtypeStruct((n_idx, d), table.dtype),
        scratch_types=[
            pltpu.VMEM((chunk,), jnp.int32), pltpu.VMEM((chunk,), jnp.int32),
            pltpu.VMEM((chunk, d), table.dtype), pltpu.VMEM((chunk, d), table.dtype),
            pltpu.SemaphoreType.DMA, pltpu.SemaphoreType.DMA, pltpu.SemaphoreType.DMA, pltpu.SemaphoreType.DMA,
        ],
    )
    def gather(table_hbm, idx_hbm, out_hbm, idx_v0, idx_v1, rows_v0, rows_v1, gsem0, gsem1, osem0, osem1):
        idx_v, rows_v, gsem, osem = (idx_v0, idx_v1), (rows_v0, rows_v1), (gsem0, gsem1), (osem0, osem1)
        wid = lax.axis_index("s") * SC_CORES + lax.axis_index("c")
        base = wid * per_worker

        def gather_copy(slot):
            return pltpu.make_async_copy(table_hbm.at[idx_v[slot]], rows_v[slot], gsem[slot])

        def out_copy(c, slot):
            return pltpu.make_async_copy(rows_v[slot], out_hbm.at[pl.ds(base + c * chunk, chunk)], osem[slot])

        def start_gather(c, slot):
            pltpu.sync_copy(idx_hbm.at[pl.ds(base + c * chunk, chunk)], idx_v[slot])
            gather_copy(slot).start()

        start_gather(0, 0)

        @pl.loop(0, n_chunks, step=2)
        def _(j):
            for b in range(2):
                c = j + b
                cur, other = b, 1 - b

                @pl.when(c >= 1)
                def _():
                    out_copy(c - 1, other).wait()

                @pl.when(c + 1 < n_chunks)
                def _():
                    start_gather(c + 1, other)

                gather_copy(cur).wait()
                out_copy(c, cur).start()

        out_copy(n_chunks - 1, 1).wait()

    return gather(table, idx)


def _combine_kernel(h1_ref, wts_ref, wsgu_ref, wsd_ref, g_ref, b_ref, yg_ref, out_ref):
    h1 = h1_ref[...]
    gu = _dot(h1.astype(BF16), wsgu_ref[...])
    hs = _silu(gu[:, :D_SHARED]) * gu[:, D_SHARED:]
    acc = DEEPNORM_ALPHA * h1 + _dot(hs.astype(BF16), wsd_ref[...])
    wts = wts_ref[...]
    acc_hi = acc[:, :D_PACK]
    acc_lo = acc[:, D_PACK:]
    for k in range(TOP_K):
        y_hi, y_lo = _unpack_rows(yg_ref[k])
        acc_hi = acc_hi + y_hi * wts[:, k:k + 1]
        acc_lo = acc_lo + y_lo * wts[:, k:k + 1]
    out_ref[...] = _layer_norm(jnp.concatenate([acc_hi, acc_lo], axis=1), g_ref[...], b_ref[...])


def _combine(h1, wts_t, ws_gu, ws_down, g, b, yg, tm):
    t = h1.shape[0]
    const = lambda shape: pl.BlockSpec(shape, lambda i: (0,) * len(shape))
    return pl.pallas_call(
        _combine_kernel,
        grid=(t // tm,),
        in_specs=[
            pl.BlockSpec((tm, D_MODEL), lambda i: (i, 0)),
            pl.BlockSpec((tm, TOP_K), lambda i: (i, 0)),
            const((D_MODEL, 2 * D_SHARED)), const((D_SHARED, D_MODEL)),
            const((1, D_MODEL)), const((1, D_MODEL)),
            pl.BlockSpec((TOP_K, tm, D_PACK), lambda i: (0, i, 0)),
        ],
        out_specs=pl.BlockSpec((tm, D_MODEL), lambda i: (i, 0)),
        out_shape=jax.ShapeDtypeStruct((t, D_MODEL), F32),
        compiler_params=_params(("arbitrary",)),
        name="moe_combine",
    )(h1, wts_t, ws_gu, ws_down, g, b, yg)


def _block_diag(w):
    nb, bi, bo = w.shape
    eye = jnp.eye(nb, dtype=w.dtype)
    return (eye[:, None, :, None] * w[:, :, None, :]).reshape(nb * bi, nb * bo)


def _pad_lanes(v, offset, width):
    return jnp.zeros((1, width), F32).at[0, offset:offset + v.shape[0]].set(v)


def _layer(h_in_x, l, p, tiles):
    bsz, seq, _ = h_in_x.shape
    t = bsz * seq
    row = lambda v: v.reshape(1, -1)

    w_in = p['w_in'][l]
    w_main = w_in[:, :N_MAIN].astype(BF16)
    w_small = jnp.zeros((D_MODEL, LANES), F32).at[:, :2 * GDN_HEADS].set(w_in[:, N_MAIN:])
    zeros = lambda n: jnp.zeros((CONV_WIDTH, n), F32)
    conv_w = jnp.concatenate([p['lru_conv_w'][l], zeros(LRU_WIDTH), p['gdn_conv_w'][l], zeros(GDN_V)], 1)
    conv_b = jnp.zeros((1, N_MAIN), F32).at[0, :LRU_WIDTH].set(p['lru_conv_b'][l])
    h0, proj, small, small_t = _inproj(h_in_x.reshape(t, D_MODEL), row(p['ln_g']), row(p['ln_b']),
                                       w_main, w_small, conv_w, conv_b, tiles['inproj'], seq)
    proj3 = proj.reshape(bsz, seq, N_MAIN)

    w_gates = jnp.concatenate([_block_diag(p['lru_w_rg'][l]), _block_diag(p['lru_w_ig'][l])], 1).astype(BF16)
    b_gates = jnp.concatenate([p['lru_b_rg'][l], p['lru_b_ig'][l]]).reshape(1, -1)
    y_lru = _lru(proj3, w_gates, b_gates, row(p['lru_lambda'][l]), row(p['lru_out_g'][l]), tiles['lru'])

    rows = GDN_GROUP * GDN_CHUNK
    small3 = small.reshape(bsz, seq, LANES)
    smallt3 = small_t.reshape(8, bsz, seq // rows, rows).transpose(1, 2, 0, 3)
    a_log, dt_bias = p['gdn_a_log'][l], p['gdn_dt_bias'][l]
    alr = _pad_lanes(a_log, GDN_HEADS, LANES)
    dtr = _pad_lanes(dt_bias, GDN_HEADS, LANES)
    alc = _pad_lanes(a_log, GDN_HEADS, 8).reshape(8, 1)
    dtc = _pad_lanes(dt_bias, GDN_HEADS, 8).reshape(8, 1)
    y_gdn = _gdn(proj3, small3, smallt3, alr, dtr, alc, dtc, row(p['gdn_norm_w'][l]), tiles['gdn_nb'])

    w_out = p['w_out'][l].astype(BF16)
    h1, h1p, top_e, wts, rank, counts = _router(
        y_lru.reshape(t, LRU_WIDTH), y_gdn.reshape(t, GDN_V), h0, w_out[:LRU_WIDTH], w_out[LRU_WIDTH:],
        row(p['ln1_g'][l]), row(p['ln1_b'][l]), p['w_router'][l].T, p['router_bias'][l].reshape(-1, 1),
        tiles['router'])

    counts = counts[:, 0]
    padded = (counts + MOE_BLOCK - 1) // MOE_BLOCK * MOE_BLOCK
    pad_end = jnp.cumsum(padded)
    pad_start = pad_end - padded
    n_blocks = (t * TOP_K) // MOE_BLOCK + N_EXPERTS
    n_rows = n_blocks * MOE_BLOCK
    n_used = (pad_end[-1] // MOE_BLOCK).astype(jnp.int32)
    blk_ids = jnp.minimum(jnp.arange(n_blocks, dtype=jnp.int32), n_used - 1)
    blk_e = jnp.minimum(jnp.sum(pad_end[None, :] <= (blk_ids * MOE_BLOCK)[:, None], axis=1),
                        N_EXPERTS - 1).astype(jnp.int32)

    dest = _dest(top_e, rank, pad_start.reshape(-1, 1), tiles['dest'])
    n_valid = jnp.clip(counts[blk_e] - (blk_ids * MOE_BLOCK - pad_start[blk_e]), 0, MOE_BLOCK).astype(jnp.int32)
    xs = _sc_scatter_rows(h1p, dest, n_rows, SC_CHUNK)
    active = jnp.arange(n_blocks, dtype=jnp.int32) < n_used
    first = (active & jnp.concatenate([jnp.ones((1,), bool), blk_e[1:] != blk_e[:-1]])).astype(jnp.int32)
    slot = ((jnp.cumsum(first) - 1) % 2).astype(jnp.int32)
    used = counts > 0
    later = jnp.where(used[None, :] & (jnp.arange(N_EXPERTS)[None, :] > jnp.arange(N_EXPERTS)[:, None]),
                      jnp.arange(N_EXPERTS, dtype=jnp.int32)[None, :], N_EXPERTS)
    next_used = jnp.min(later, axis=1)
    next_e = jnp.where(next_used < N_EXPERTS, next_used, -1)[blk_e].astype(jnp.int32)
    ys = _experts(blk_e, n_valid, first, slot, next_e, n_used.reshape(1), xs,
                  p['w_gate'][l], p['w_up'][l], p['w_down'][l])
    ws_gu = jnp.concatenate([p['ws_gate'][l], p['ws_up'][l]], 1).astype(BF16)
    yg = _sc_gather_rows(ys, dest.reshape(TOP_K * t), SC_CHUNK).reshape(TOP_K, t, D_PACK)
    out = _combine(h1, wts.T, ws_gu, p['ws_down'][l].astype(BF16),
                   row(p['ln2_g'][l]), row(p['ln2_b'][l]), yg, tiles['combine'])
    return out.reshape(bsz, seq, D_MODEL)


def _tiles(bsz, seq):
    t = bsz * seq
    return {
        'inproj': min(512, t),
        'lru': min(256, seq),
        'gdn_nb': bsz,
        'router': min(512, t),
        'dest': min(512, t),
        'combine': min(256, t),
    }


def kernel(x, ln_in_g, ln_in_b, w_in, lru_conv_w, lru_conv_b, lru_w_rg, lru_b_rg, lru_w_ig, lru_b_ig,
           lru_lambda, lru_out_g, gdn_conv_w, gdn_a_log, gdn_dt_bias, gdn_norm_w, w_out, ln1_g, ln1_b,
           w_router, router_bias, w_gate, w_up, w_down, ws_gate, ws_up, ws_down, ln2_g, ln2_b):
    assert w_in.shape[0] == DEPTH == 1
    p = dict(ln_g=ln_in_g, ln_b=ln_in_b, w_in=w_in, lru_conv_w=lru_conv_w, lru_conv_b=lru_conv_b,
             lru_w_rg=lru_w_rg, lru_b_rg=lru_b_rg, lru_w_ig=lru_w_ig, lru_b_ig=lru_b_ig,
             lru_lambda=lru_lambda, lru_out_g=lru_out_g, gdn_conv_w=gdn_conv_w, gdn_a_log=gdn_a_log,
             gdn_dt_bias=gdn_dt_bias, gdn_norm_w=gdn_norm_w, w_out=w_out, ln1_g=ln1_g, ln1_b=ln1_b,
             w_router=w_router, router_bias=router_bias, w_gate=w_gate, w_up=w_up, w_down=w_down,
             ws_gate=ws_gate, ws_up=ws_up, ws_down=ws_down, ln2_g=ln2_g, ln2_b=ln2_b)
    bsz, seq, _ = x.shape
    return _layer(x, 0, p, _tiles(bsz, seq))
```

```python
import functools

import jax
import jax.numpy as jnp
from jax import lax
from jax.experimental import pallas as pl
from jax.experimental.pallas import tpu as pltpu
from jax.experimental.pallas import tpu_sc as plsc

F32 = jnp.float32
BF16 = jnp.bfloat16

D_MODEL = 1024
LRU_WIDTH = 512
LRU_BLOCKS = 8
LRU_C = 8.0
CONV_WIDTH = 4
GDN_HEADS = 4
GDN_DK = 128
GDN_DV = 128
GDN_CHUNK = 64
GDN_GROUP = 2
GDN_QK = GDN_HEADS * GDN_DK
GDN_V = GDN_HEADS * GDN_DV
N_MAIN = 2 * LRU_WIDTH + 2 * GDN_QK + 2 * GDN_V
N_EXPERTS = 256
TOP_K = 8
N_GROUPS = 8
GROUP_SIZE = N_EXPERTS // N_GROUPS
TOPK_GROUPS = 4
D_EXPERT = 256
D_SHARED = 256
ROUTED_SCALE = 2.5
MOE_BLOCK = 512
D_PACK = D_MODEL // 2
LN_EPS = 1e-5
NORM_EPS = 1e-6
DEPTH = 1
DEEPNORM_ALPHA = (2.0 * DEPTH) ** 0.25

SCAN_GROUP = 8
HALO = 8
CONV_GROUP = 512
CONV_GROUPS = (0, 2, 3, 4)
LANES = 128
VMEM_LIMIT = 56 * 1024 * 1024
ROUTER_SUB = 256
EXPERT_BANDS = 4
SC_CORES = 2
SC_WORKERS = 32
SC_CHUNK = 64

NN = (((1,), (0,)), ((), ()))
NT = (((1,), (1,)), ((), ()))
TN = (((0,), (0,)), ((), ()))


def _dot(a, b, dims=NN):
    return lax.dot_general(a, b, dims, preferred_element_type=F32)


def _split(a):
    hi = a.astype(BF16)
    lo = (a - hi.astype(F32)).astype(BF16)
    return hi, lo


def _dot3(a, b, dims=NN):
    ah, al = _split(a)
    bh, bl = _split(b)
    return _dot(ah, bh, dims) + (_dot(ah, bl, dims) + _dot(al, bh, dims))


def _layer_norm(x, g, b):
    mu = jnp.mean(x, -1, keepdims=True)
    xc = x - mu
    var = jnp.mean(xc * xc, -1, keepdims=True)
    return xc * lax.rsqrt(var + LN_EPS) * g + b


def _sigmoid(x):
    return 1.0 / (1.0 + jnp.exp(-x))


def _silu(x):
    return x * _sigmoid(x)


def _softplus(x):
    return jnp.maximum(x, 0.0) + jnp.log1p(jnp.exp(-jnp.abs(x)))


def _gelu_tanh(x):
    c = 0.7978845608028654
    return x * (0.5 * (1.0 + jnp.tanh(c * (x + 0.044715 * (x * x * x)))))


def _pack_rows(x):
    hi = lax.bitcast_convert_type(x[:, :D_PACK].astype(BF16).astype(F32), jnp.uint32)
    lo = lax.bitcast_convert_type(x[:, D_PACK:].astype(BF16).astype(F32), jnp.uint32)
    return (hi & jnp.uint32(0xFFFF0000)) | (lo >> 16)


def _unpack_rows(w):
    hi = lax.bitcast_convert_type(w & jnp.uint32(0xFFFF0000), F32)
    lo = lax.bitcast_convert_type(w << 16, F32)
    return hi, lo


def _params(sem, **kw):
    return pltpu.CompilerParams(dimension_semantics=sem, vmem_limit_bytes=VMEM_LIMIT, **kw)


def _inproj_kernel(x_ref, g_ref, b_ref, w_ref, ws_ref, cw_ref, cb_ref,
                   h_ref, proj_ref, small_ref, smallt_ref, hist, *, tiles_per_seq):
    i = pl.program_id(0)
    tm = x_ref.shape[0]
    h = _layer_norm(x_ref[...], g_ref[...], b_ref[...])
    h_ref[...] = h
    hb = h.astype(BF16)

    @pl.when(i % tiles_per_seq == 0)
    def _():
        hist[...] = jnp.zeros_like(hist)

    for g in range(N_MAIN // CONV_GROUP):
        cols = slice(g * CONV_GROUP, (g + 1) * CONV_GROUP)
        p = _dot(hb, w_ref[:, cols])
        if g in CONV_GROUPS:
            xcat = jnp.concatenate([hist[:, cols], p], axis=0)
            acc = cb_ref[:, cols]
            for j in range(CONV_WIDTH):
                off = HALO - (CONV_WIDTH - 1) + j
                acc = acc + xcat[off:off + tm, :] * cw_ref[j:j + 1, cols]
            hist[:, cols] = p[tm - HALO:, :]
            p = acc
        proj_ref[:, cols] = p
    small = _dot3(h, ws_ref[...])
    small_ref[...] = small
    smallt_ref[...] = small.T[:smallt_ref.shape[0], :]


def _inproj(x2d, g, b, w_main, w_small, conv_w, conv_b, tm, seq):
    t = x2d.shape[0]
    return pl.pallas_call(
        functools.partial(_inproj_kernel, tiles_per_seq=seq // tm),
        grid=(t // tm,),
        in_specs=[
            pl.BlockSpec((tm, D_MODEL), lambda i: (i, 0)),
            pl.BlockSpec((1, D_MODEL), lambda i: (0, 0)),
            pl.BlockSpec((1, D_MODEL), lambda i: (0, 0)),
            pl.BlockSpec((D_MODEL, N_MAIN), lambda i: (0, 0)),
            pl.BlockSpec((D_MODEL, LANES), lambda i: (0, 0)),
            pl.BlockSpec((CONV_WIDTH, N_MAIN), lambda i: (0, 0)),
            pl.BlockSpec((1, N_MAIN), lambda i: (0, 0)),
        ],
        out_specs=[
            pl.BlockSpec((tm, D_MODEL), lambda i: (i, 0)),
            pl.BlockSpec((tm, N_MAIN), lambda i: (i, 0)),
            pl.BlockSpec((tm, LANES), lambda i: (i, 0)),
            pl.BlockSpec((8, tm), lambda i: (0, i)),
        ],
        out_shape=[
            jax.ShapeDtypeStruct((t, D_MODEL), F32),
            jax.ShapeDtypeStruct((t, N_MAIN), F32),
            jax.ShapeDtypeStruct((t, LANES), F32),
            jax.ShapeDtypeStruct((8, t), F32),
        ],
        scratch_shapes=[pltpu.VMEM((HALO, N_MAIN), F32)],
        compiler_params=_params(("arbitrary",)),
        name="ln_inproj",
    )(x2d, g, b, w_main, w_small, conv_w, conv_b)


def _lru_kernel(xc_ref, gate_ref, wg_ref, bg_ref, lam_ref, og_ref, y_ref, hcarry):
    s = pl.program_id(1)
    rows = xc_ref.shape[0]

    @pl.when(s == 0)
    def _():
        hcarry[...] = jnp.zeros_like(hcarry)

    xc = xc_ref[...]
    gates = _dot(xc.astype(BF16), wg_ref[...]) + bg_ref[...]
    r = _sigmoid(gates[:, :LRU_WIDTH])
    i = _sigmoid(gates[:, LRU_WIDTH:])
    log_a = (-LRU_C) * r * _softplus(-lam_ref[...])
    a = jnp.exp(log_a)
    one_minus_a2 = -jnp.tanh(log_a) * (a * a + 1.0)
    mult = jnp.where(one_minus_a2 > 0.0, one_minus_a2 * lax.rsqrt(one_minus_a2), 0.0)
    bv = mult * (i * xc)
    a = a.reshape(rows // SCAN_GROUP, SCAN_GROUP, LRU_WIDTH)
    bv = bv.reshape(rows // SCAN_GROUP, SCAN_GROUP, LRU_WIDTH)
    row_in_group = lax.broadcasted_iota(jnp.int32, a.shape, 1)
    d = 1
    while d < SCAN_GROUP:
        a_sh = jnp.where(row_in_group < d, 1.0, pltpu.roll(a, d, 1))
        b_sh = jnp.where(row_in_group < d, 0.0, pltpu.roll(bv, d, 1))
        bv = a * b_sh + bv
        a = a * a_sh
        d *= 2
    a = a.reshape(rows, LRU_WIDTH)
    bv = bv.reshape(rows, LRU_WIDTH)
    carry = hcarry[...]
    parts = []
    for g in range(rows // SCAN_GROUP):
        grp = slice(g * SCAN_GROUP, (g + 1) * SCAN_GROUP)
        hg = a[grp] * carry + bv[grp]
        carry = hg[SCAN_GROUP - 1:, :]
        parts.append(hg)
    h = jnp.concatenate(parts, axis=0)
    hcarry[...] = carry
    y = h * _gelu_tanh(gate_ref[...])
    ms = jnp.mean(y * y, -1, keepdims=True)
    y_ref[...] = y * lax.rsqrt(ms + NORM_EPS) * og_ref[...]


def _lru(proj3, w_gates, b_gates, lam, out_g, ts):
    bsz, seq, _ = proj3.shape
    row = lambda n: pl.BlockSpec((1, n), lambda b, s: (0, 0))
    return pl.pallas_call(
        _lru_kernel,
        grid=(bsz, seq // ts),
        in_specs=[
            pl.BlockSpec((None, ts, LRU_WIDTH), lambda b, s: (b, s, 0)),
            pl.BlockSpec((None, ts, LRU_WIDTH), lambda b, s: (b, s, 1)),
            pl.BlockSpec((LRU_WIDTH, 2 * LRU_WIDTH), lambda b, s: (0, 0)),
            row(2 * LRU_WIDTH),
            row(LRU_WIDTH),
            row(LRU_WIDTH),
        ],
        out_specs=pl.BlockSpec((None, ts, LRU_WIDTH), lambda b, s: (b, s, 0)),
        out_shape=jax.ShapeDtypeStruct((bsz, seq, LRU_WIDTH), F32),
        scratch_shapes=[
            pltpu.VMEM((1, LRU_WIDTH), F32),
        ],
        compiler_params=_params(("arbitrary", "arbitrary")),
        name="rg_lru",
    )(proj3, proj3, w_gates, b_gates, lam, out_g)


def _bdot(a, b, dims=NN):
    return _dot(a.astype(BF16), b.astype(BF16), dims)


def _gdn_heads(args, norm_w):
    c = GDN_CHUNK
    r = GDN_GROUP * c
    ri = lax.broadcasted_iota(jnp.int32, (r, r), 0)
    ci = lax.broadcasted_iota(jnp.int32, (r, r), 1)
    same = (ri // c) == (ci // c)
    causal = same & (ri >= ci)
    strict = same & (ri > ci)
    upper = same & (ri <= ci)
    chunk_of_row = lax.broadcasted_iota(jnp.int32, (r, 1), 0) // c
    each = lambda f, *ls: [f(*xs) for xs in zip(*ls)]
    q, k, v, z, beta, g_col, g_row, st = [list(x) for x in zip(*args)]
    q = each(lambda x: x * lax.rsqrt(jnp.sum(x * x, -1, keepdims=True) + NORM_EPS) * (GDN_DK ** -0.5), q)
    k = each(lambda x: x * lax.rsqrt(jnp.sum(x * x, -1, keepdims=True) + NORM_EPS), k)
    gc_col = each(lambda g: jnp.sum(jnp.where(causal, g, 0.0), axis=1, keepdims=True), g_row)
    gc_row = each(lambda g: jnp.sum(jnp.where(upper, g, 0.0), axis=0, keepdims=True), g_col)
    decay = each(lambda gc, gr: jnp.exp(jnp.where(causal, gc - gr, -jnp.inf)), gc_col, gc_row)
    kb = each(lambda x, bt: x * bt, k, beta)
    vb = each(lambda x, bt: x * bt, v, beta)
    kk = each(lambda x, y: _bdot(x, y, NT), kb, k)
    a_mat = each(lambda m, d: jnp.where(strict, m * d, 0.0), kk, decay)
    e_col = each(jnp.exp, gc_col)
    rhs = each(lambda x, y, e: jnp.concatenate([x, y * e], axis=1), vb, kb, e_col)
    sol = each(lambda rr, a: rr - _bdot(a, rr), rhs, a_mat)
    p = a_mat
    for _ in range(5):
        p = each(lambda x: _bdot(x, x), p)
        sol = each(lambda x, y: y + _bdot(x, y), p, sol)
    qk = each(lambda x, y: _bdot(x, y, NT), q, k)
    qk = each(lambda m, d: jnp.where(causal, m * d, 0.0), qk, decay)
    q_dec = each(lambda x, e: x * e, q, e_col)
    g_last = [each(lambda gc: gc[(j + 1) * c - 1:(j + 1) * c, :], gc_col) for j in range(GDN_GROUP)]

    def last_of_own_chunk(*gl):
        out = gl[-1]
        for j in range(GDN_GROUP - 2, -1, -1):
            out = jnp.where(chunk_of_row == j, gl[j], out)
        return out

    g_end = each(last_of_own_chunk, *g_last)
    k_dec = each(lambda x, ge, gc: x * jnp.exp(ge - gc), k, g_end, gc_col)
    qs_parts, v_parts = [], []
    for j in range(GDN_GROUP):
        rows = slice(j * c, (j + 1) * c)
        ws = each(lambda x, s: _bdot(x[rows, GDN_DV:], s), sol, st)
        qs_parts.append(each(lambda x, s: _bdot(x[rows], s), q_dec, st))
        v_new = each(lambda x, w: x[rows, :GDN_DV] - w, sol, ws)
        v_parts.append(v_new)
        kv = each(lambda x, vn: _bdot(x[rows], vn, TN), k_dec, v_new)
        st = each(lambda s, gl, d: s * jnp.exp(gl) + d, st, g_last[j], kv)
    qs = each(lambda *parts: jnp.concatenate(parts, axis=0), *qs_parts)
    v_all = each(lambda *parts: jnp.concatenate(parts, axis=0), *v_parts)
    o = each(lambda a, m, vn: a + _bdot(m, vn), qs, qk, v_all)
    o = each(lambda x: x * lax.rsqrt(jnp.mean(x * x, -1, keepdims=True) + NORM_EPS) * norm_w, o)
    o = each(lambda x, zz: x * _silu(zz), o, z)
    return list(zip(o, st))


def _gdn_kernel(q_ref, k_ref, v_ref, z_ref, sm_ref, smt_ref,
                alr_ref, dtr_ref, alc_ref, dtc_ref, nw_ref, y_ref, state):
    n = pl.program_id(1)
    c = GDN_GROUP * GDN_CHUNK
    nb = q_ref.shape[0]
    first = n == 0

    @pl.when(first)
    def _():
        state[...] = jnp.zeros_like(state)

    norm_w = nw_ref[...]

    args = []
    for b in range(nb):
        q_all = _silu(q_ref[b])
        k_all = _silu(k_ref[b])
        v_all = _silu(v_ref[b])
        z_all = z_ref[b]
        sm = sm_ref[b]
        beta_all = _sigmoid(sm)
        g_cols = -jnp.exp(alr_ref[...]) * _softplus(sm + dtr_ref[...])
        g_rows = -jnp.exp(alc_ref[...]) * _softplus(smt_ref[b] + dtc_ref[...])
        for hd in range(GDN_HEADS):
            sl = slice(hd * GDN_DK, (hd + 1) * GDN_DK)
            args.append((q_all[:, sl], k_all[:, sl], v_all[:, sl], z_all[:, sl],
                         beta_all[:, hd:hd + 1],
                         g_cols[:, GDN_HEADS + hd:GDN_HEADS + hd + 1],
                         g_rows[GDN_HEADS + hd:GDN_HEADS + hd + 1, :],
                         state[b, hd]))
    outs = _gdn_heads(args, norm_w)
    for b in range(nb):
        for hd in range(GDN_HEADS):
            o, st_new = outs[b * GDN_HEADS + hd]
            state[b, hd] = st_new
            y_ref[b, :, hd * GDN_DK:(hd + 1) * GDN_DK] = o


def _gdn(proj3, small3, smallt3, alr, dtr, alc, dtc, norm_w, nb):
    bsz, seq, _ = proj3.shape
    c = GDN_GROUP * GDN_CHUNK
    nch = seq // c
    col = lambda j: pl.BlockSpec((nb, c, GDN_QK), lambda b, n: (b, n, j))
    const = lambda shape: pl.BlockSpec(shape, lambda b, n: (0,) * len(shape))
    return pl.pallas_call(
        _gdn_kernel,
        grid=(bsz // nb, nch),
        in_specs=[
            col(2), col(3), col(4), col(5),
            pl.BlockSpec((nb, c, LANES), lambda b, n: (b, n, 0)),
            pl.BlockSpec((nb, None, 8, c), lambda b, n: (b, n, 0, 0)),
            const((1, LANES)), const((1, LANES)), const((8, 1)), const((8, 1)),
            const((1, GDN_DV)),
        ],
        out_specs=pl.BlockSpec((nb, c, GDN_V), lambda b, n: (b, n, 0)),
        out_shape=jax.ShapeDtypeStruct((bsz, seq, GDN_V), F32),
        scratch_shapes=[
            pltpu.VMEM((nb, GDN_HEADS, GDN_DK, GDN_DV), F32),
        ],
        compiler_params=_params(("arbitrary", "arbitrary")),
        name="gated_deltanet",
    )(proj3, proj3, proj3, proj3, small3, smallt3, alr, dtr, alc, dtc, norm_w)


def _pick_experts(logits, rbias):
    n = logits.shape[1]
    scores = _sigmoid(logits)
    choice = scores + rbias
    neg = -jnp.inf
    gs_rows = []
    sub = lax.broadcasted_iota(jnp.int32, (GROUP_SIZE, n), 0).astype(F32)
    for g in range(N_GROUPS):
        cg = choice[g * GROUP_SIZE:(g + 1) * GROUP_SIZE, :]
        m1 = jnp.max(cg, axis=0, keepdims=True)
        i1 = jnp.min(jnp.where(cg == m1, sub, float(GROUP_SIZE)), axis=0, keepdims=True)
        m2 = jnp.max(jnp.where(sub == i1, neg, cg), axis=0, keepdims=True)
        gs_rows.append(m1 + m2)
    gs = jnp.concatenate(gs_rows, axis=0)
    gi = lax.broadcasted_iota(jnp.int32, (N_GROUPS, n), 0).astype(F32)
    gsel = jnp.zeros((N_GROUPS, n), jnp.bool_)
    for _ in range(TOPK_GROUPS):
        m = jnp.max(gs, axis=0, keepdims=True)
        idx = jnp.min(jnp.where(gs == m, gi, float(N_GROUPS)), axis=0, keepdims=True)
        hit = gi == idx
        gsel = jnp.logical_or(gsel, hit)
        gs = jnp.where(hit, neg, gs)
    masked = jnp.concatenate(
        [jnp.where(gsel[g:g + 1, :], choice[g * GROUP_SIZE:(g + 1) * GROUP_SIZE, :], neg)
         for g in range(N_GROUPS)], axis=0)
    ei = lax.broadcasted_iota(jnp.int32, (N_EXPERTS, n), 0).astype(F32)
    hits, e_rows, w_rows = [], [], []
    multi = jnp.zeros((N_EXPERTS, n), F32)
    for _ in range(TOP_K):
        m = jnp.max(masked, axis=0, keepdims=True)
        idx = jnp.min(jnp.where(masked == m, ei, float(N_EXPERTS)), axis=0, keepdims=True)
        hit = ei == idx
        hits.append(hit)
        e_rows.append(idx)
        w_rows.append(jnp.sum(jnp.where(hit, scores, 0.0), axis=0, keepdims=True))
        multi = multi + hit.astype(F32)
        masked = jnp.where(hit, neg, masked)
    wts = jnp.concatenate(w_rows, axis=0)
    wts = wts / (jnp.sum(wts, axis=0, keepdims=True) + 1e-20) * ROUTED_SCALE
    return jnp.concatenate(e_rows, axis=0), wts, hits, multi


def _router_kernel(yl_ref, yg_ref, h0_ref, wo1_ref, wo2_ref, g_ref, b_ref, wrt_ref, rb_ref,
                   h1_ref, h1p_ref, e_ref, w_ref, rank_ref, cnt_ref, carry):
    i = pl.program_id(0)
    tm = h0_ref.shape[0]
    n = min(ROUTER_SUB, tm)
    subs = [slice(j * n, (j + 1) * n) for j in range(tm // n)]

    @pl.when(i == 0)
    def _():
        carry[...] = jnp.zeros_like(carry)

    mixes = [_dot(yl_ref[r, :].astype(BF16), wo1_ref[...]) + _dot(yg_ref[r, :].astype(BF16), wo2_ref[...])
             for r in subs]
    h1s = [_layer_norm(DEEPNORM_ALPHA * h0_ref[r, :] + mix, g_ref[...], b_ref[...])
           for r, mix in zip(subs, mixes)]
    for r, h1 in zip(subs, h1s):
        h1_ref[r, :] = h1
        h1p_ref[r, :] = _pack_rows(h1)
    logits = [_dot3(wrt_ref[...], h1, NT) for h1 in h1s]
    picks = [_pick_experts(lg, rb_ref[...]) for lg in logits]
    ti = lax.broadcasted_iota(jnp.int32, (n, n), 0)
    tj = lax.broadcasted_iota(jnp.int32, (n, n), 1)
    before = (ti < tj).astype(BF16)
    cums = [_dot(multi.astype(BF16), before) for _, _, _, multi in picks]
    base = carry[...]
    for r, (e_rows, wts, hits, multi), cum in zip(subs, picks, cums):
        cum = cum + base
        r_rows = [jnp.sum(jnp.where(hit, cum, 0.0), axis=0, keepdims=True) for hit in hits]
        base = base + jnp.sum(multi, axis=1, keepdims=True)
        e_ref[:, r] = e_rows.astype(jnp.int32)
        w_ref[:, r] = wts
        rank_ref[:, r] = jnp.concatenate(r_rows, axis=0).astype(jnp.int32)
    carry[...] = base
    cnt_ref[...] = base.astype(jnp.int32)


def _router(y_lru, y_gdn, h0, wo1, wo2, g, b, w_router_t, rbias, tm):
    t = h0.shape[0]
    const = lambda shape: pl.BlockSpec(shape, lambda i: (0,) * len(shape))
    return pl.pallas_call(
        _router_kernel,
        grid=(t // tm,),
        in_specs=[
            pl.BlockSpec((tm, LRU_WIDTH), lambda i: (i, 0)),
            pl.BlockSpec((tm, GDN_V), lambda i: (i, 0)),
            pl.BlockSpec((tm, D_MODEL), lambda i: (i, 0)),
            const((LRU_WIDTH, D_MODEL)), const((GDN_V, D_MODEL)),
            const((1, D_MODEL)), const((1, D_MODEL)),
            const((N_EXPERTS, D_MODEL)), const((N_EXPERTS, 1)),
        ],
        out_specs=[
            pl.BlockSpec((tm, D_MODEL), lambda i: (i, 0)),
            pl.BlockSpec((tm, D_PACK), lambda i: (i, 0)),
            pl.BlockSpec((TOP_K, tm), lambda i: (0, i)),
            pl.BlockSpec((TOP_K, tm), lambda i: (0, i)),
            pl.BlockSpec((TOP_K, tm), lambda i: (0, i)),
            const((N_EXPERTS, 1)),
        ],
        out_shape=[
            jax.ShapeDtypeStruct((t, D_MODEL), F32),
            jax.ShapeDtypeStruct((t, D_PACK), jnp.uint32),
            jax.ShapeDtypeStruct((TOP_K, t), jnp.int32),
            jax.ShapeDtypeStruct((TOP_K, t), F32),
            jax.ShapeDtypeStruct((TOP_K, t), jnp.int32),
            jax.ShapeDtypeStruct((N_EXPERTS, 1), jnp.int32),
        ],
        scratch_shapes=[pltpu.VMEM((N_EXPERTS, 1), F32)],
        compiler_params=_params(("arbitrary",)),
        name="outproj_router",
    )(y_lru, y_gdn, h0, wo1, wo2, g, b, w_router_t, rbias)


def _dest_kernel(e_ref, r_ref, ps_ref, d_ref):
    tm = e_ref.shape[1]
    ei = lax.broadcasted_iota(jnp.int32, (N_EXPERTS, tm), 0)
    rows = []
    for k in range(TOP_K):
        hit = ei == e_ref[k:k + 1, :]
        rows.append(jnp.sum(jnp.where(hit, ps_ref[...], 0), axis=0, keepdims=True))
    d_ref[...] = jnp.concatenate(rows, axis=0) + r_ref[...]


def _dest(top_e, rank, pad_start, tm):
    t = top_e.shape[1]
    blk = pl.BlockSpec((TOP_K, tm), lambda i: (0, i))
    return pl.pallas_call(
        _dest_kernel,
        grid=(t // tm,),
        in_specs=[blk, blk, pl.BlockSpec((N_EXPERTS, 1), lambda i: (0, 0))],
        out_specs=blk,
        out_shape=jax.ShapeDtypeStruct((TOP_K, t), jnp.int32),
        compiler_params=_params(("arbitrary",)),
        name="moe_dest",
    )(top_e, rank, pad_start)


def _sc_scatter_rows(rows, idx, n_out, chunk):
    n_copies, t = idx.shape
    d = rows.shape[1]
    per_worker = t // SC_WORKERS
    n_chunks = per_worker // chunk
    mesh = plsc.VectorSubcoreMesh(core_axis_name="c", subcore_axis_name="s")
    idx_flat = idx.reshape(n_copies * t)

    @functools.partial(
        pl.kernel, mesh=mesh,
        out_type=jax.ShapeDtypeStruct((n_out, d), rows.dtype),
        scratch_types=[pltpu.VMEM((chunk,), jnp.int32) for _ in range(n_copies)] + [
            pltpu.VMEM((chunk, d), rows.dtype),
            pltpu.SemaphoreType.DMA,
        ],
    )
    def scatter(rows_hbm, idx_hbm, out_hbm, *scratch):
        idx_v = scratch[:n_copies]
        rows_v, sem = scratch[n_copies:]
        wid = lax.axis_index("s") * SC_CORES + lax.axis_index("c")
        base = wid * per_worker

        @pl.loop(0, n_chunks)
        def _(j):
            off = base + j * chunk
            for k in range(n_copies):
                pltpu.sync_copy(idx_hbm.at[pl.ds(k * t + off, chunk)], idx_v[k])
            pltpu.sync_copy(rows_hbm.at[pl.ds(off, chunk)], rows_v)
            copies = [pltpu.async_copy(rows_v, out_hbm.at[idx_v[k]], sem) for k in range(n_copies)]
            for cp in copies:
                cp.wait()

    return scatter(rows, idx_flat)


def _expert_kernel(be_ref, nv_ref, first_ref, slot_ref, next_ref, nu_ref,
                   xs_ref, wg_hbm, wu_hbm, wd_hbm, ys_ref, wg_f, wu_f, wd_f, wgu_b, wd_b, sem):
    i = pl.program_id(0)

    def fetch(e, slot):
        return (pltpu.make_async_copy(wg_hbm.at[e], wg_f.at[slot], sem.at[slot]),
                pltpu.make_async_copy(wu_hbm.at[e], wu_f.at[slot], sem.at[slot]),
                pltpu.make_async_copy(wd_hbm.at[e], wd_f.at[slot], sem.at[slot]))

    @pl.when(i < nu_ref[0])
    def _():
        e = be_ref[i]
        slot = slot_ref[i]

        @pl.when(first_ref[i] == 1)
        def _():
            @pl.when(i == 0)
            def _():
                for cp in fetch(e, slot):
                    cp.start()

            for cp in fetch(e, slot):
                cp.wait()

            @pl.when(next_ref[i] >= 0)
            def _():
                for cp in fetch(next_ref[i], 1 - slot):
                    cp.start()

            wgu_b[:, :D_EXPERT] = wg_f[slot].astype(BF16)
            wgu_b[:, D_EXPERT:] = wu_f[slot].astype(BF16)
            wd_b[...] = wd_f[slot].astype(BF16)

        n = xs_ref.shape[0] // EXPERT_BANDS
        bands = [slice(j * n, (j + 1) * n) for j in range(EXPERT_BANDS)]
        row = lax.broadcasted_iota(jnp.int32, (n, D_PACK), 0)
        xs = [_unpack_rows(jnp.where(row + j * n < nv_ref[i], xs_ref[r, :], jnp.uint32(0)))
              for j, r in enumerate(bands)]
        gus = [_dot(x_hi.astype(BF16), wgu_b[:D_PACK, :]) + _dot(x_lo.astype(BF16), wgu_b[D_PACK:, :])
               for x_hi, x_lo in xs]
        hs = [_silu(gu[:, :D_EXPERT]) * gu[:, D_EXPERT:] for gu in gus]
        ys = [_dot(h.astype(BF16), wd_b[...]) for h in hs]
        for r, y in zip(bands, ys):
            ys_ref[r, :] = _pack_rows(y)


def _experts(blk_e, n_valid, first, slot, next_e, n_used, xs, w_gate, w_up, w_down):
    n_rows = xs.shape[0]
    n_blocks = n_rows // MOE_BLOCK
    blk = lambda i, be, nv, fi, sl, nx, nu: (jnp.minimum(i, nu[0] - 1), 0)
    return pl.pallas_call(
        _expert_kernel,
        grid_spec=pltpu.PrefetchScalarGridSpec(
            num_scalar_prefetch=6,
            grid=(n_blocks,),
            in_specs=[
                pl.BlockSpec((MOE_BLOCK, D_PACK), blk),
                pl.BlockSpec(memory_space=pl.ANY),
                pl.BlockSpec(memory_space=pl.ANY),
                pl.BlockSpec(memory_space=pl.ANY),
            ],
            out_specs=pl.BlockSpec((MOE_BLOCK, D_PACK), blk),
            scratch_shapes=[
                pltpu.VMEM((2, D_MODEL, D_EXPERT), F32),
                pltpu.VMEM((2, D_MODEL, D_EXPERT), F32),
                pltpu.VMEM((2, D_EXPERT, D_MODEL), F32),
                pltpu.VMEM((D_MODEL, 2 * D_EXPERT), BF16),
                pltpu.VMEM((D_EXPERT, D_MODEL), BF16),
                pltpu.SemaphoreType.DMA((2,)),
            ],
        ),
        out_shape=jax.ShapeDtypeStruct((n_rows, D_PACK), jnp.uint32),
        compiler_params=_params(("arbitrary",)),
        name="moe_experts",
    )(blk_e, n_valid, first, slot, next_e, n_used, xs, w_gate, w_up, w_down)


def _sc_gather_rows(table, idx, chunk):
    n_idx = idx.shape[0]
    d = table.shape[1]
    per_worker = n_idx // SC_WORKERS
    n_chunks = per_worker // chunk
    assert n_chunks % 2 == 0 and n_chunks * chunk * SC_WORKERS == n_idx
    mesh = plsc.VectorSubcoreMesh(core_axis_name="c", subcore_axis_name="s")

    @functools.partial(
        pl.kernel, mesh=mesh,
        out_type=jax.ShapeDtypeStruct((n_idx, d), table.dtype),
        scratch_types=[
            pltpu.VMEM((chunk,), jnp.int32), pltpu.VMEM((chunk,), jnp.int32),
            pltpu.VMEM((chunk, d), table.dtype), pltpu.VMEM((chunk, d), table.dtype),
            pltpu.SemaphoreType.DMA, pltpu.SemaphoreType.DMA, pltpu.SemaphoreType.DMA, pltpu.SemaphoreType.DMA,
        ],
    )
    def gather(table_hbm, idx_hbm, out_hbm, idx_v0, idx_v1, rows_v0, rows_v1, gsem0, gsem1, osem0, osem1):
        idx_v, rows_v, gsem, osem = (idx_v0, idx_v1), (rows_v0, rows_v1), (gsem0, gsem1), (osem0, osem1)
        wid = lax.axis_index("s") * SC_CORES + lax.axis_index("c")
        base = wid * per_worker

        def gather_copy(slot):
            return pltpu.make_async_copy(table_hbm.at[idx_v[slot]], rows_v[slot], gsem[slot])

        def out_copy(c, slot):
            return pltpu.make_async_copy(rows_v[slot], out_hbm.at[pl.ds(base + c * chunk, chunk)], osem[slot])

        def start_gather(c, slot):
            pltpu.sync_copy(idx_hbm.at[pl.ds(base + c * chunk, chunk)], idx_v[slot])
            gather_copy(slot).start()

        start_gather(0, 0)

        @pl.loop(0, n_chunks, step=2)
        def _(j):
            for b in range(2):
                c = j + b
                cur, other = b, 1 - b

                @pl.when(c >= 1)
                def _():
                    out_copy(c - 1, other).wait()

                @pl.when(c + 1 < n_chunks)
                def _():
                    start_gather(c + 1, other)

                gather_copy(cur).wait()
                out_copy(c, cur).start()

        out_copy(n_chunks - 1, 1).wait()

    return gather(table, idx)


def _combine_kernel(h1_ref, wts_ref, wsgu_ref, wsd_ref, g_ref, b_ref, yg_ref, out_ref):
    h1 = h1_ref[...]
    gu = _dot(h1.astype(BF16), wsgu_ref[...])
    hs = _silu(gu[:, :D_SHARED]) * gu[:, D_SHARED:]
    acc = DEEPNORM_ALPHA * h1 + _dot(hs.astype(BF16), wsd_ref[...])
    wts = wts_ref[...]
    acc_hi = acc[:, :D_PACK]
    acc_lo = acc[:, D_PACK:]
    for k in range(TOP_K):
        y_hi, y_lo = _unpack_rows(yg_ref[k])
        acc_hi = acc_hi + y_hi * wts[:, k:k + 1]
        acc_lo = acc_lo + y_lo * wts[:, k:k + 1]
    out_ref[...] = _layer_norm(jnp.concatenate([acc_hi, acc_lo], axis=1), g_ref[...], b_ref[...])


def _combine(h1, wts_t, ws_gu, ws_down, g, b, yg, tm):
    t = h1.shape[0]
    const = lambda shape: pl.BlockSpec(shape, lambda i: (0,) * len(shape))
    return pl.pallas_call(
        _combine_kernel,
        grid=(t // tm,),
        in_specs=[
            pl.BlockSpec((tm, D_MODEL), lambda i: (i, 0)),
            pl.BlockSpec((tm, TOP_K), lambda i: (i, 0)),
            const((D_MODEL, 2 * D_SHARED)), const((D_SHARED, D_MODEL)),
            const((1, D_MODEL)), const((1, D_MODEL)),
            pl.BlockSpec((TOP_K, tm, D_PACK), lambda i: (0, i, 0)),
        ],
        out_specs=pl.BlockSpec((tm, D_MODEL), lambda i: (i, 0)),
        out_shape=jax.ShapeDtypeStruct((t, D_MODEL), F32),
        compiler_params=_params(("arbitrary",)),
        name="moe_combine",
    )(h1, wts_t, ws_gu, ws_down, g, b, yg)


def _block_diag(w):
    nb, bi, bo = w.shape
    eye = jnp.eye(nb, dtype=w.dtype)
    return (eye[:, None, :, None] * w[:, :, None, :]).reshape(nb * bi, nb * bo)


def _pad_lanes(v, offset, width):
    return jnp.zeros((1, width), F32).at[0, offset:offset + v.shape[0]].set(v)


def _layer(h_in_x, l, p, tiles):
    bsz, seq, _ = h_in_x.shape
    t = bsz * seq
    row = lambda v: v.reshape(1, -1)

    w_in = p['w_in'][l]
    w_main = w_in[:, :N_MAIN].astype(BF16)
    w_small = jnp.zeros((D_MODEL, LANES), F32).at[:, :2 * GDN_HEADS].set(w_in[:, N_MAIN:])
    zeros = lambda n: jnp.zeros((CONV_WIDTH, n), F32)
    conv_w = jnp.concatenate([p['lru_conv_w'][l], zeros(LRU_WIDTH), p['gdn_conv_w'][l], zeros(GDN_V)], 1)
    conv_b = jnp.zeros((1, N_MAIN), F32).at[0, :LRU_WIDTH].set(p['lru_conv_b'][l])
    h0, proj, small, small_t = _inproj(h_in_x.reshape(t, D_MODEL), row(p['ln_g']), row(p['ln_b']),
                                       w_main, w_small, conv_w, conv_b, tiles['inproj'], seq)
    proj3 = proj.reshape(bsz, seq, N_MAIN)

    w_gates = jnp.concatenate([_block_diag(p['lru_w_rg'][l]), _block_diag(p['lru_w_ig'][l])], 1).astype(BF16)
    b_gates = jnp.concatenate([p['lru_b_rg'][l], p['lru_b_ig'][l]]).reshape(1, -1)
    y_lru = _lru(proj3, w_gates, b_gates, row(p['lru_lambda'][l]), row(p['lru_out_g'][l]), tiles['lru'])

    rows = GDN_GROUP * GDN_CHUNK
    small3 = small.reshape(bsz, seq, LANES)
    smallt3 = small_t.reshape(8, bsz, seq // rows, rows).transpose(1, 2, 0, 3)
    a_log, dt_bias = p['gdn_a_log'][l], p['gdn_dt_bias'][l]
    alr = _pad_lanes(a_log, GDN_HEADS, LANES)
    dtr = _pad_lanes(dt_bias, GDN_HEADS, LANES)
    alc = _pad_lanes(a_log, GDN_HEADS, 8).reshape(8, 1)
    dtc = _pad_lanes(dt_bias, GDN_HEADS, 8).reshape(8, 1)
    y_gdn = _gdn(proj3, small3, smallt3, alr, dtr, alc, dtc, row(p['gdn_norm_w'][l]), tiles['gdn_nb'])

    w_out = p['w_out'][l].astype(BF16)
    h1, h1p, top_e, wts, rank, counts = _router(
        y_lru.reshape(t, LRU_WIDTH), y_gdn.reshape(t, GDN_V), h0, w_out[:LRU_WIDTH], w_out[LRU_WIDTH:],
        row(p['ln1_g'][l]), row(p['ln1_b'][l]), p['w_router'][l].T, p['router_bias'][l].reshape(-1, 1),
        tiles['router'])

    counts = counts[:, 0]
    padded = (counts + MOE_BLOCK - 1) // MOE_BLOCK * MOE_BLOCK
    pad_end = jnp.cumsum(padded)
    pad_start = pad_end - padded
    n_blocks = (t * TOP_K) // MOE_BLOCK + N_EXPERTS
    n_rows = n_blocks * MOE_BLOCK
    n_used = (pad_end[-1] // MOE_BLOCK).astype(jnp.int32)
    blk_ids = jnp.minimum(jnp.arange(n_blocks, dtype=jnp.int32), n_used - 1)
    blk_e = jnp.minimum(jnp.sum(pad_end[None, :] <= (blk_ids * MOE_BLOCK)[:, None], axis=1),
                        N_EXPERTS - 1).astype(jnp.int32)

    dest = _dest(top_e, rank, pad_start.reshape(-1, 1), tiles['dest'])
    n_valid = jnp.clip(counts[blk_e] - (blk_ids * MOE_BLOCK - pad_start[blk_e]), 0, MOE_BLOCK).astype(jnp.int32)
    xs = _sc_scatter_rows(h1p, dest, n_rows, SC_CHUNK)
    active = jnp.arange(n_blocks, dtype=jnp.int32) < n_used
    first = (active & jnp.concatenate([jnp.ones((1,), bool), blk_e[1:] != blk_e[:-1]])).astype(jnp.int32)
    slot = ((jnp.cumsum(first) - 1) % 2).astype(jnp.int32)
    used = counts > 0
    later = jnp.where(used[None, :] & (jnp.arange(N_EXPERTS)[None, :] > jnp.arange(N_EXPERTS)[:, None]),
                      jnp.arange(N_EXPERTS, dtype=jnp.int32)[None, :], N_EXPERTS)
    next_used = jnp.min(later, axis=1)
    next_e = jnp.where(next_used < N_EXPERTS, next_used, -1)[blk_e].astype(jnp.int32)
    ys = _experts(blk_e, n_valid, first, slot, next_e, n_used.reshape(1), xs,
                  p['w_gate'][l], p['w_up'][l], p['w_down'][l])
    ws_gu = jnp.concatenate([p['ws_gate'][l], p['ws_up'][l]], 1).astype(BF16)
    yg = _sc_gather_rows(ys, dest.reshape(TOP_K * t), SC_CHUNK).reshape(TOP_K, t, D_PACK)
    out = _combine(h1, wts.T, ws_gu, p['ws_down'][l].astype(BF16),
                   row(p['ln2_g'][l]), row(p['ln2_b'][l]), yg, tiles['combine'])
    return out.reshape(bsz, seq, D_MODEL)


def _tiles(bsz, seq):
    t = bsz * seq
    return {
        'inproj': min(512, t),
        'lru': min(256, seq),
        'gdn_nb': bsz,
        'router': min(512, t),
        'dest': min(512, t),
        'combine': min(256, t),
    }


def kernel(x, ln_in_g, ln_in_b, w_in, lru_conv_w, lru_conv_b, lru_w_rg, lru_b_rg, lru_w_ig, lru_b_ig,
           lru_lambda, lru_out_g, gdn_conv_w, gdn_a_log, gdn_dt_bias, gdn_norm_w, w_out, ln1_g, ln1_b,
           w_router, router_bias, w_gate, w_up, w_down, ws_gate, ws_up, ws_down, ln2_g, ln2_b):
    assert w_in.shape[0] == DEPTH == 1
    p = dict(ln_g=ln_in_g, ln_b=ln_in_b, w_in=w_in, lru_conv_w=lru_conv_w, lru_conv_b=lru_conv_b,
             lru_w_rg=lru_w_rg, lru_b_rg=lru_b_rg, lru_w_ig=lru_w_ig, lru_b_ig=lru_b_ig,
             lru_lambda=lru_lambda, lru_out_g=lru_out_g, gdn_conv_w=gdn_conv_w, gdn_a_log=gdn_a_log,
             gdn_dt_bias=gdn_dt_bias, gdn_norm_w=gdn_norm_w, w_out=w_out, ln1_g=ln1_g, ln1_b=ln1_b,
             w_router=w_router, router_bias=router_bias, w_gate=w_gate, w_up=w_up, w_down=w_down,
             ws_gate=ws_gate, ws_up=ws_up, ws_down=ws_down, ln2_g=ln2_g, ln2_b=ln2_b)
    bsz, seq, _ = x.shape
    return _layer(x, 0, p, _tiles(bsz, seq))
```

```python
import functools

import jax
import jax.numpy as jnp
from jax import lax
from jax.experimental import pallas as pl
from jax.experimental.pallas import tpu as pltpu
from jax.experimental.pallas import tpu_sc as plsc

F32 = jnp.float32
BF16 = jnp.bfloat16

D_MODEL = 1024
LRU_WIDTH = 512
LRU_BLOCKS = 8
LRU_C = 8.0
CONV_WIDTH = 4
GDN_HEADS = 4
GDN_DK = 128
GDN_DV = 128
GDN_CHUNK = 64
GDN_GROUP = 2
GDN_QK = GDN_HEADS * GDN_DK
GDN_V = GDN_HEADS * GDN_DV
N_MAIN = 2 * LRU_WIDTH + 2 * GDN_QK + 2 * GDN_V
N_EXPERTS = 256
TOP_K = 8
N_GROUPS = 8
GROUP_SIZE = N_EXPERTS // N_GROUPS
TOPK_GROUPS = 4
D_EXPERT = 256
D_SHARED = 256
ROUTED_SCALE = 2.5
MOE_BLOCK = 512
D_PACK = D_MODEL // 2
LN_EPS = 1e-5
NORM_EPS = 1e-6
DEPTH = 1
DEEPNORM_ALPHA = (2.0 * DEPTH) ** 0.25

SCAN_GROUP = 8
HALO = 8
CONV_GROUP = 512
CONV_GROUPS = (0, 2, 3, 4)
LANES = 128
VMEM_LIMIT = 56 * 1024 * 1024
ROUTER_SUB = 256
EXPERT_BANDS = 4
SC_CORES = 2
SC_WORKERS = 32
SC_CHUNK = 64

NN = (((1,), (0,)), ((), ()))
NT = (((1,), (1,)), ((), ()))
TN = (((0,), (0,)), ((), ()))


def _dot(a, b, dims=NN):
    return lax.dot_general(a, b, dims, preferred_element_type=F32)


def _split(a):
    hi = a.astype(BF16)
    lo = (a - hi.astype(F32)).astype(BF16)
    return hi, lo


def _dot3(a, b, dims=NN):
    ah, al = _split(a)
    bh, bl = _split(b)
    return _dot(ah, bh, dims) + (_dot(ah, bl, dims) + _dot(al, bh, dims))


def _layer_norm(x, g, b):
    mu = jnp.mean(x, -1, keepdims=True)
    xc = x - mu
    var = jnp.mean(xc * xc, -1, keepdims=True)
    return xc * lax.rsqrt(var + LN_EPS) * g + b


def _sigmoid(x):
    return 1.0 / (1.0 + jnp.exp(-x))


def _silu(x):
    return x * _sigmoid(x)


def _softplus(x):
    return jnp.maximum(x, 0.0) + jnp.log1p(jnp.exp(-jnp.abs(x)))


def _gelu_tanh(x):
    c = 0.7978845608028654
    return x * (0.5 * (1.0 + jnp.tanh(c * (x + 0.044715 * (x * x * x)))))


def _pack_rows(x):
    hi = lax.bitcast_convert_type(x[:, :D_PACK].astype(BF16).astype(F32), jnp.uint32)
    lo = lax.bitcast_convert_type(x[:, D_PACK:].astype(BF16).astype(F32), jnp.uint32)
    return (hi & jnp.uint32(0xFFFF0000)) | (lo >> 16)


def _unpack_rows(w):
    hi = lax.bitcast_convert_type(w & jnp.uint32(0xFFFF0000), F32)
    lo = lax.bitcast_convert_type(w << 16, F32)
    return hi, lo


def _params(sem, **kw):
    return pltpu.CompilerParams(dimension_semantics=sem, vmem_limit_bytes=VMEM_LIMIT, **kw)


def _inproj_kernel(x_ref, g_ref, b_ref, w_ref, ws_ref, cw_ref, cb_ref,
                   h_ref, proj_ref, small_ref, smallt_ref, hist, *, tiles_per_seq):
    i = pl.program_id(0)
    tm = x_ref.shape[0]
    h = _layer_norm(x_ref[...], g_ref[...], b_ref[...])
    h_ref[...] = h
    hb = h.astype(BF16)

    @pl.when(i % tiles_per_seq == 0)
    def _():
        hist[...] = jnp.zeros_like(hist)

    for g in range(N_MAIN // CONV_GROUP):
        cols = slice(g * CONV_GROUP, (g + 1) * CONV_GROUP)
        p = _dot(hb, w_ref[:, cols])
        if g in CONV_GROUPS:
            xcat = jnp.concatenate([hist[:, cols], p], axis=0)
            acc = cb_ref[:, cols]
            for j in range(CONV_WIDTH):
                off = HALO - (CONV_WIDTH - 1) + j
                acc = acc + xcat[off:off + tm, :] * cw_ref[j:j + 1, cols]
            hist[:, cols] = p[tm - HALO:, :]
            p = acc
        proj_ref[:, cols] = p
    small = _dot3(h, ws_ref[...])
    small_ref[...] = small
    smallt_ref[...] = small.T[:smallt_ref.shape[0], :]


def _inproj(x2d, g, b, w_main, w_small, conv_w, conv_b, tm, seq):
    t = x2d.shape[0]
    return pl.pallas_call(
        functools.partial(_inproj_kernel, tiles_per_seq=seq // tm),
        grid=(t // tm,),
        in_specs=[
            pl.BlockSpec((tm, D_MODEL), lambda i: (i, 0)),
            pl.BlockSpec((1, D_MODEL), lambda i: (0, 0)),
            pl.BlockSpec((1, D_MODEL), lambda i: (0, 0)),
            pl.BlockSpec((D_MODEL, N_MAIN), lambda i: (0, 0)),
            pl.BlockSpec((D_MODEL, LANES), lambda i: (0, 0)),
            pl.BlockSpec((CONV_WIDTH, N_MAIN), lambda i: (0, 0)),
            pl.BlockSpec((1, N_MAIN), lambda i: (0, 0)),
        ],
        out_specs=[
            pl.BlockSpec((tm, D_MODEL), lambda i: (i, 0)),
            pl.BlockSpec((tm, N_MAIN), lambda i: (i, 0)),
            pl.BlockSpec((tm, LANES), lambda i: (i, 0)),
            pl.BlockSpec((8, tm), lambda i: (0, i)),
        ],
        out_shape=[
            jax.ShapeDtypeStruct((t, D_MODEL), F32),
            jax.ShapeDtypeStruct((t, N_MAIN), F32),
            jax.ShapeDtypeStruct((t, LANES), F32),
            jax.ShapeDtypeStruct((8, t), F32),
        ],
        scratch_shapes=[pltpu.VMEM((HALO, N_MAIN), F32)],
        compiler_params=_params(("arbitrary",)),
        name="ln_inproj",
    )(x2d, g, b, w_main, w_small, conv_w, conv_b)


def _lru_tile(xc, gate, wg, bg, lam, og, carry):
    rows = xc.shape[0]
    gates = _dot(xc.astype(BF16), wg) + bg
    r = _sigmoid(gates[:, :LRU_WIDTH])
    i = _sigmoid(gates[:, LRU_WIDTH:])
    log_a = (-LRU_C) * r * _softplus(-lam)
    a = jnp.exp(log_a)
    one_minus_a2 = -jnp.tanh(log_a) * (a * a + 1.0)
    mult = jnp.where(one_minus_a2 > 0.0, one_minus_a2 * lax.rsqrt(one_minus_a2), 0.0)
    bv = mult * (i * xc)
    a = a.reshape(rows // SCAN_GROUP, SCAN_GROUP, LRU_WIDTH)
    bv = bv.reshape(rows // SCAN_GROUP, SCAN_GROUP, LRU_WIDTH)
    row_in_group = lax.broadcasted_iota(jnp.int32, a.shape, 1)
    d = 1
    while d < SCAN_GROUP:
        a_sh = jnp.where(row_in_group < d, 1.0, pltpu.roll(a, d, 1))
        b_sh = jnp.where(row_in_group < d, 0.0, pltpu.roll(bv, d, 1))
        bv = a * b_sh + bv
        a = a * a_sh
        d *= 2
    a = a.reshape(rows, LRU_WIDTH)
    bv = bv.reshape(rows, LRU_WIDTH)
    parts = []
    for g in range(rows // SCAN_GROUP):
        grp = slice(g * SCAN_GROUP, (g + 1) * SCAN_GROUP)
        hg = a[grp] * carry + bv[grp]
        carry = hg[SCAN_GROUP - 1:, :]
        parts.append(hg)
    h = jnp.concatenate(parts, axis=0)
    y = h * _gelu_tanh(gate)
    ms = jnp.mean(y * y, -1, keepdims=True)
    return y * lax.rsqrt(ms + NORM_EPS) * og, carry


def _bdot(a, b, dims=NN):
    return _dot(a.astype(BF16), b.astype(BF16), dims)


def _gdn_heads(args, norm_w):
    c = GDN_CHUNK
    r = GDN_GROUP * c
    ri = lax.broadcasted_iota(jnp.int32, (r, r), 0)
    ci = lax.broadcasted_iota(jnp.int32, (r, r), 1)
    same = (ri // c) == (ci // c)
    causal = same & (ri >= ci)
    strict = same & (ri > ci)
    upper = same & (ri <= ci)
    chunk_of_row = lax.broadcasted_iota(jnp.int32, (r, 1), 0) // c
    each = lambda f, *ls: [f(*xs) for xs in zip(*ls)]
    q, k, v, z, beta, g_col, g_row, st = [list(x) for x in zip(*args)]
    q = each(lambda x: x * lax.rsqrt(jnp.sum(x * x, -1, keepdims=True) + NORM_EPS) * (GDN_DK ** -0.5), q)
    k = each(lambda x: x * lax.rsqrt(jnp.sum(x * x, -1, keepdims=True) + NORM_EPS), k)
    gc_col = each(lambda g: jnp.sum(jnp.where(causal, g, 0.0), axis=1, keepdims=True), g_row)
    gc_row = each(lambda g: jnp.sum(jnp.where(upper, g, 0.0), axis=0, keepdims=True), g_col)
    decay = each(lambda gc, gr: jnp.exp(jnp.where(causal, gc - gr, -jnp.inf)), gc_col, gc_row)
    kb = each(lambda x, bt: x * bt, k, beta)
    vb = each(lambda x, bt: x * bt, v, beta)
    kk = each(lambda x, y: _bdot(x, y, NT), kb, k)
    a_mat = each(lambda m, d: jnp.where(strict, m * d, 0.0), kk, decay)
    e_col = each(jnp.exp, gc_col)
    rhs = each(lambda x, y, e: jnp.concatenate([x, y * e], axis=1), vb, kb, e_col)
    sol = each(lambda rr, a: rr - _bdot(a, rr), rhs, a_mat)
    p = a_mat
    for _ in range(5):
        p = each(lambda x: _bdot(x, x), p)
        sol = each(lambda x, y: y + _bdot(x, y), p, sol)
    qk = each(lambda x, y: _bdot(x, y, NT), q, k)
    qk = each(lambda m, d: jnp.where(causal, m * d, 0.0), qk, decay)
    q_dec = each(lambda x, e: x * e, q, e_col)
    g_last = [each(lambda gc: gc[(j + 1) * c - 1:(j + 1) * c, :], gc_col) for j in range(GDN_GROUP)]

    def last_of_own_chunk(*gl):
        out = gl[-1]
        for j in range(GDN_GROUP - 2, -1, -1):
            out = jnp.where(chunk_of_row == j, gl[j], out)
        return out

    g_end = each(last_of_own_chunk, *g_last)
    k_dec = each(lambda x, ge, gc: x * jnp.exp(ge - gc), k, g_end, gc_col)
    qs_parts, v_parts = [], []
    for j in range(GDN_GROUP):
        rows = slice(j * c, (j + 1) * c)
        ws = each(lambda x, s: _bdot(x[rows, GDN_DV:], s), sol, st)
        qs_parts.append(each(lambda x, s: _bdot(x[rows], s), q_dec, st))
        v_new = each(lambda x, w: x[rows, :GDN_DV] - w, sol, ws)
        v_parts.append(v_new)
        kv = each(lambda x, vn: _bdot(x[rows], vn, TN), k_dec, v_new)
        st = each(lambda s, gl, d: s * jnp.exp(gl) + d, st, g_last[j], kv)
    qs = each(lambda *parts: jnp.concatenate(parts, axis=0), *qs_parts)
    v_all = each(lambda *parts: jnp.concatenate(parts, axis=0), *v_parts)
    o = each(lambda a, m, vn: a + _bdot(m, vn), qs, qk, v_all)
    o = each(lambda x: x * lax.rsqrt(jnp.mean(x * x, -1, keepdims=True) + NORM_EPS) * norm_w, o)
    o = each(lambda x, zz: x * _silu(zz), o, z)
    return list(zip(o, st))


def _mixer_kernel(xc_ref, gate_ref, q_ref, k_ref, v_ref, z_ref, sm_ref, smt_ref,
                  wg_ref, bg_ref, lam_ref, og_ref, alr_ref, dtr_ref, alc_ref, dtc_ref, nw_ref,
                  ylru_ref, y_ref, hcarry, state):
    n = pl.program_id(1)
    c = GDN_GROUP * GDN_CHUNK
    nb = q_ref.shape[0]
    first = n == 0

    @pl.when(first)
    def _():
        state[...] = jnp.zeros_like(state)
        hcarry[...] = jnp.zeros_like(hcarry)

    lru_out = [_lru_tile(xc_ref[b], gate_ref[b], wg_ref[...], bg_ref[...], lam_ref[...], og_ref[...], hcarry[b])
               for b in range(nb)]
    norm_w = nw_ref[...]

    args = []
    for b in range(nb):
        q_all = _silu(q_ref[b])
        k_all = _silu(k_ref[b])
        v_all = _silu(v_ref[b])
        z_all = z_ref[b]
        sm = sm_ref[b]
        beta_all = _sigmoid(sm)
        g_cols = -jnp.exp(alr_ref[...]) * _softplus(sm + dtr_ref[...])
        g_rows = -jnp.exp(alc_ref[...]) * _softplus(smt_ref[b] + dtc_ref[...])
        for hd in range(GDN_HEADS):
            sl = slice(hd * GDN_DK, (hd + 1) * GDN_DK)
            args.append((q_all[:, sl], k_all[:, sl], v_all[:, sl], z_all[:, sl],
                         beta_all[:, hd:hd + 1],
                         g_cols[:, GDN_HEADS + hd:GDN_HEADS + hd + 1],
                         g_rows[GDN_HEADS + hd:GDN_HEADS + hd + 1, :],
                         state[b, hd]))
    outs = _gdn_heads(args, norm_w)
    for b in range(nb):
        for hd in range(GDN_HEADS):
            o, st_new = outs[b * GDN_HEADS + hd]
            state[b, hd] = st_new
            y_ref[b, :, hd * GDN_DK:(hd + 1) * GDN_DK] = o
    for b, (y_lru, carry) in enumerate(lru_out):
        ylru_ref[b] = y_lru
        hcarry[b] = carry


def _mixers(proj3, small3, smallt3, w_gates, b_gates, lam, out_g, alr, dtr, alc, dtc, norm_w, nb):
    bsz, seq, _ = proj3.shape
    c = GDN_GROUP * GDN_CHUNK
    nch = seq // c
    col = lambda j: pl.BlockSpec((nb, c, GDN_QK), lambda b, n: (b, n, j))
    const = lambda shape: pl.BlockSpec(shape, lambda b, n: (0,) * len(shape))
    return pl.pallas_call(
        _mixer_kernel,
        grid=(bsz // nb, nch),
        in_specs=[
            col(0), col(1), col(2), col(3), col(4), col(5),
            pl.BlockSpec((nb, c, LANES), lambda b, n: (b, n, 0)),
            pl.BlockSpec((nb, None, 8, c), lambda b, n: (b, n, 0, 0)),
            const((LRU_WIDTH, 2 * LRU_WIDTH)), const((1, 2 * LRU_WIDTH)), const((1, LRU_WIDTH)), const((1, LRU_WIDTH)),
            const((1, LANES)), const((1, LANES)), const((8, 1)), const((8, 1)),
            const((1, GDN_DV)),
        ],
        out_specs=[pl.BlockSpec((nb, c, LRU_WIDTH), lambda b, n: (b, n, 0)),
                   pl.BlockSpec((nb, c, GDN_V), lambda b, n: (b, n, 0))],
        out_shape=[jax.ShapeDtypeStruct((bsz, seq, LRU_WIDTH), F32),
                   jax.ShapeDtypeStruct((bsz, seq, GDN_V), F32)],
        scratch_shapes=[
            pltpu.VMEM((nb, 1, LRU_WIDTH), F32),
            pltpu.VMEM((nb, GDN_HEADS, GDN_DK, GDN_DV), F32),
        ],
        compiler_params=_params(("arbitrary", "arbitrary")),
        name="mixers",
    )(proj3, proj3, proj3, proj3, proj3, proj3, small3, smallt3, w_gates, b_gates, lam, out_g,
      alr, dtr, alc, dtc, norm_w)


def _pick_experts(logits, rbias):
    n = logits.shape[1]
    scores = _sigmoid(logits)
    choice = scores + rbias
    neg = -jnp.inf
    gs_rows = []
    sub = lax.broadcasted_iota(jnp.int32, (GROUP_SIZE, n), 0).astype(F32)
    for g in range(N_GROUPS):
        cg = choice[g * GROUP_SIZE:(g + 1) * GROUP_SIZE, :]
        m1 = jnp.max(cg, axis=0, keepdims=True)
        i1 = jnp.min(jnp.where(cg == m1, sub, float(GROUP_SIZE)), axis=0, keepdims=True)
        m2 = jnp.max(jnp.where(sub == i1, neg, cg), axis=0, keepdims=True)
        gs_rows.append(m1 + m2)
    gs = jnp.concatenate(gs_rows, axis=0)
    gi = lax.broadcasted_iota(jnp.int32, (N_GROUPS, n), 0).astype(F32)
    gsel = jnp.zeros((N_GROUPS, n), jnp.bool_)
    for _ in range(TOPK_GROUPS):
        m = jnp.max(gs, axis=0, keepdims=True)
        idx = jnp.min(jnp.where(gs == m, gi, float(N_GROUPS)), axis=0, keepdims=True)
        hit = gi == idx
        gsel = jnp.logical_or(gsel, hit)
        gs = jnp.where(hit, neg, gs)
    masked = jnp.concatenate(
        [jnp.where(gsel[g:g + 1, :], choice[g * GROUP_SIZE:(g + 1) * GROUP_SIZE, :], neg)
         for g in range(N_GROUPS)], axis=0)
    ei = lax.broadcasted_iota(jnp.int32, (N_EXPERTS, n), 0).astype(F32)
    hits, e_rows, w_rows = [], [], []
    multi = jnp.zeros((N_EXPERTS, n), F32)
    for _ in range(TOP_K):
        m = jnp.max(masked, axis=0, keepdims=True)
        idx = jnp.min(jnp.where(masked == m, ei, float(N_EXPERTS)), axis=0, keepdims=True)
        hit = ei == idx
        hits.append(hit)
        e_rows.append(idx)
        w_rows.append(jnp.sum(jnp.where(hit, scores, 0.0), axis=0, keepdims=True))
        multi = multi + hit.astype(F32)
        masked = jnp.where(hit, neg, masked)
    wts = jnp.concatenate(w_rows, axis=0)
    wts = wts / (jnp.sum(wts, axis=0, keepdims=True) + 1e-20) * ROUTED_SCALE
    return jnp.concatenate(e_rows, axis=0), wts, hits, multi


def _router_kernel(yl_ref, yg_ref, h0_ref, wo1_ref, wo2_ref, g_ref, b_ref, wrt_ref, rb_ref,
                   h1_ref, h1p_ref, e_ref, w_ref, rank_ref, cnt_ref, carry):
    i = pl.program_id(0)
    tm = h0_ref.shape[0]
    n = min(ROUTER_SUB, tm)
    subs = [slice(j * n, (j + 1) * n) for j in range(tm // n)]

    @pl.when(i == 0)
    def _():
        carry[...] = jnp.zeros_like(carry)

    mixes = [_dot(yl_ref[r, :].astype(BF16), wo1_ref[...]) + _dot(yg_ref[r, :].astype(BF16), wo2_ref[...])
             for r in subs]
    h1s = [_layer_norm(DEEPNORM_ALPHA * h0_ref[r, :] + mix, g_ref[...], b_ref[...])
           for r, mix in zip(subs, mixes)]
    for r, h1 in zip(subs, h1s):
        h1_ref[r, :] = h1
        h1p_ref[r, :] = _pack_rows(h1)
    logits = [_dot3(wrt_ref[...], h1, NT) for h1 in h1s]
    picks = [_pick_experts(lg, rb_ref[...]) for lg in logits]
    ti = lax.broadcasted_iota(jnp.int32, (n, n), 0)
    tj = lax.broadcasted_iota(jnp.int32, (n, n), 1)
    before = (ti < tj).astype(BF16)
    cums = [_dot(multi.astype(BF16), before) for _, _, _, multi in picks]
    base = carry[...]
    for r, (e_rows, wts, hits, multi), cum in zip(subs, picks, cums):
        cum = cum + base
        r_rows = [jnp.sum(jnp.where(hit, cum, 0.0), axis=0, keepdims=True) for hit in hits]
        base = base + jnp.sum(multi, axis=1, keepdims=True)
        e_ref[:, r] = e_rows.astype(jnp.int32)
        w_ref[:, r] = wts
        rank_ref[:, r] = jnp.concatenate(r_rows, axis=0).astype(jnp.int32)
    carry[...] = base
    cnt_ref[...] = base.astype(jnp.int32)


def _router(y_lru, y_gdn, h0, wo1, wo2, g, b, w_router_t, rbias, tm):
    t = h0.shape[0]
    const = lambda shape: pl.BlockSpec(shape, lambda i: (0,) * len(shape))
    return pl.pallas_call(
        _router_kernel,
        grid=(t // tm,),
        in_specs=[
            pl.BlockSpec((tm, LRU_WIDTH), lambda i: (i, 0)),
            pl.BlockSpec((tm, GDN_V), lambda i: (i, 0)),
            pl.BlockSpec((tm, D_MODEL), lambda i: (i, 0)),
            const((LRU_WIDTH, D_MODEL)), const((GDN_V, D_MODEL)),
            const((1, D_MODEL)), const((1, D_MODEL)),
            const((N_EXPERTS, D_MODEL)), const((N_EXPERTS, 1)),
        ],
        out_specs=[
            pl.BlockSpec((tm, D_MODEL), lambda i: (i, 0)),
            pl.BlockSpec((tm, D_PACK), lambda i: (i, 0)),
            pl.BlockSpec((TOP_K, tm), lambda i: (0, i)),
            pl.BlockSpec((TOP_K, tm), lambda i: (0, i)),
            pl.BlockSpec((TOP_K, tm), lambda i: (0, i)),
            const((N_EXPERTS, 1)),
        ],
        out_shape=[
            jax.ShapeDtypeStruct((t, D_MODEL), F32),
            jax.ShapeDtypeStruct((t, D_PACK), jnp.uint32),
            jax.ShapeDtypeStruct((TOP_K, t), jnp.int32),
            jax.ShapeDtypeStruct((TOP_K, t), F32),
            jax.ShapeDtypeStruct((TOP_K, t), jnp.int32),
            jax.ShapeDtypeStruct((N_EXPERTS, 1), jnp.int32),
        ],
        scratch_shapes=[pltpu.VMEM((N_EXPERTS, 1), F32)],
        compiler_params=_params(("arbitrary",)),
        name="outproj_router",
    )(y_lru, y_gdn, h0, wo1, wo2, g, b, w_router_t, rbias)


def _dest_kernel(e_ref, r_ref, ps_ref, d_ref):
    tm = e_ref.shape[1]
    ei = lax.broadcasted_iota(jnp.int32, (N_EXPERTS, tm), 0)
    rows = []
    for k in range(TOP_K):
        hit = ei == e_ref[k:k + 1, :]
        rows.append(jnp.sum(jnp.where(hit, ps_ref[...], 0), axis=0, keepdims=True))
    d_ref[...] = jnp.concatenate(rows, axis=0) + r_ref[...]


def _dest(top_e, rank, pad_start, tm):
    t = top_e.shape[1]
    blk = pl.BlockSpec((TOP_K, tm), lambda i: (0, i))
    return pl.pallas_call(
        _dest_kernel,
        grid=(t // tm,),
        in_specs=[blk, blk, pl.BlockSpec((N_EXPERTS, 1), lambda i: (0, 0))],
        out_specs=blk,
        out_shape=jax.ShapeDtypeStruct((TOP_K, t), jnp.int32),
        compiler_params=_params(("arbitrary",)),
        name="moe_dest",
    )(top_e, rank, pad_start)


def _sc_scatter_rows(rows, idx, n_out, chunk):
    n_copies, t = idx.shape
    d = rows.shape[1]
    per_worker = t // SC_WORKERS
    n_chunks = per_worker // chunk
    mesh = plsc.VectorSubcoreMesh(core_axis_name="c", subcore_axis_name="s")
    idx_flat = idx.reshape(n_copies * t)

    @functools.partial(
        pl.kernel, mesh=mesh,
        out_type=jax.ShapeDtypeStruct((n_out, d), rows.dtype),
        scratch_types=[pltpu.VMEM((chunk,), jnp.int32) for _ in range(n_copies)] + [
            pltpu.VMEM((chunk, d), rows.dtype),
            pltpu.SemaphoreType.DMA,
        ],
    )
    def scatter(rows_hbm, idx_hbm, out_hbm, *scratch):
        idx_v = scratch[:n_copies]
        rows_v, sem = scratch[n_copies:]
        wid = lax.axis_index("s") * SC_CORES + lax.axis_index("c")
        base = wid * per_worker

        @pl.loop(0, n_chunks)
        def _(j):
            off = base + j * chunk
            for k in range(n_copies):
                pltpu.sync_copy(idx_hbm.at[pl.ds(k * t + off, chunk)], idx_v[k])
            pltpu.sync_copy(rows_hbm.at[pl.ds(off, chunk)], rows_v)
            copies = [pltpu.async_copy(rows_v, out_hbm.at[idx_v[k]], sem) for k in range(n_copies)]
            for cp in copies:
                cp.wait()

    return scatter(rows, idx_flat)


def _expert_kernel(be_ref, nv_ref, first_ref, slot_ref, next_ref, nu_ref,
                   xs_ref, wg_hbm, wu_hbm, wd_hbm, ys_ref, wg_f, wu_f, wd_f, wgu_b, wd_b, sem):
    i = pl.program_id(0)

    def fetch(e, slot):
        return (pltpu.make_async_copy(wg_hbm.at[e], wg_f.at[slot], sem.at[slot]),
                pltpu.make_async_copy(wu_hbm.at[e], wu_f.at[slot], sem.at[slot]),
                pltpu.make_async_copy(wd_hbm.at[e], wd_f.at[slot], sem.at[slot]))

    @pl.when(i < nu_ref[0])
    def _():
        e = be_ref[i]
        slot = slot_ref[i]

        @pl.when(first_ref[i] == 1)
        def _():
            @pl.when(i == 0)
            def _():
                for cp in fetch(e, slot):
                    cp.start()

            for cp in fetch(e, slot):
                cp.wait()

            @pl.when(next_ref[i] >= 0)
            def _():
                for cp in fetch(next_ref[i], 1 - slot):
                    cp.start()

            wgu_b[:, :D_EXPERT] = wg_f[slot].astype(BF16)
            wgu_b[:, D_EXPERT:] = wu_f[slot].astype(BF16)
            wd_b[...] = wd_f[slot].astype(BF16)

        n = xs_ref.shape[0] // EXPERT_BANDS
        bands = [slice(j * n, (j + 1) * n) for j in range(EXPERT_BANDS)]
        row = lax.broadcasted_iota(jnp.int32, (n, D_PACK), 0)
        xs = [_unpack_rows(jnp.where(row + j * n < nv_ref[i], xs_ref[r, :], jnp.uint32(0)))
              for j, r in enumerate(bands)]
        gus = [_dot(x_hi.astype(BF16), wgu_b[:D_PACK, :]) + _dot(x_lo.astype(BF16), wgu_b[D_PACK:, :])
               for x_hi, x_lo in xs]
        hs = [_silu(gu[:, :D_EXPERT]) * gu[:, D_EXPERT:] for gu in gus]
        ys = [_dot(h.astype(BF16), wd_b[...]) for h in hs]
        for r, y in zip(bands, ys):
            ys_ref[r, :] = _pack_rows(y)


def _experts(blk_e, n_valid, first, slot, next_e, n_used, xs, w_gate, w_up, w_down):
    n_rows = xs.shape[0]
    n_blocks = n_rows // MOE_BLOCK
    blk = lambda i, be, nv, fi, sl, nx, nu: (jnp.minimum(i, nu[0] - 1), 0)
    return pl.pallas_call(
        _expert_kernel,
        grid_spec=pltpu.PrefetchScalarGridSpec(
            num_scalar_prefetch=6,
            grid=(n_blocks,),
            in_specs=[
                pl.BlockSpec((MOE_BLOCK, D_PACK), blk),
                pl.BlockSpec(memory_space=pl.ANY),
                pl.BlockSpec(memory_space=pl.ANY),
                pl.BlockSpec(memory_space=pl.ANY),
            ],
            out_specs=pl.BlockSpec((MOE_BLOCK, D_PACK), blk),
            scratch_shapes=[
                pltpu.VMEM((2, D_MODEL, D_EXPERT), F32),
                pltpu.VMEM((2, D_MODEL, D_EXPERT), F32),
                pltpu.VMEM((2, D_EXPERT, D_MODEL), F32),
                pltpu.VMEM((D_MODEL, 2 * D_EXPERT), BF16),
                pltpu.VMEM((D_EXPERT, D_MODEL), BF16),
                pltpu.SemaphoreType.DMA((2,)),
            ],
        ),
        out_shape=jax.ShapeDtypeStruct((n_rows, D_PACK), jnp.uint32),
        compiler_params=_params(("arbitrary",)),
        name="moe_experts",
    )(blk_e, n_valid, first, slot, next_e, n_used, xs, w_gate, w_up, w_down)


def _sc_gather_rows(table, idx, chunk):
    n_idx = idx.shape[0]
    d = table.shape[1]
    per_worker = n_idx // SC_WORKERS
    n_chunks = per_worker // chunk
    assert n_chunks % 2 == 0 and n_chunks * chunk * SC_WORKERS == n_idx
    mesh = plsc.VectorSubcoreMesh(core_axis_name="c", subcore_axis_name="s")

    @functools.partial(
        pl.kernel, mesh=mesh,
        out_type=jax.ShapeDtypeStruct((n_idx, d), table.dtype),
        scratch_types=[
            pltpu.VMEM((chunk,), jnp.int32), pltpu.VMEM((chunk,), jnp.int32),
            pltpu.VMEM((chunk, d), table.dtype), pltpu.VMEM((chunk, d), table.dtype),
            pltpu.SemaphoreType.DMA, pltpu.SemaphoreType.DMA, pltpu.SemaphoreType.DMA, pltpu.SemaphoreType.DMA,
        ],
    )
    def gather(table_hbm, idx_hbm, out_hbm, idx_v0, idx_v1, rows_v0, rows_v1, gsem0, gsem1, osem0, osem1):
        idx_v, rows_v, gsem, osem = (idx_v0, idx_v1), (rows_v0, rows_v1), (gsem0, gsem1), (osem0, osem1)
        wid = lax.axis_index("s") * SC_CORES + lax.axis_index("c")
        base = wid * per_worker

        def gather_copy(slot):
            return pltpu.make_async_copy(table_hbm.at[idx_v[slot]], rows_v[slot], gsem[slot])

        def out_copy(c, slot):
            return pltpu.make_async_copy(rows_v[slot], out_hbm.at[pl.ds(base + c * chunk, chunk)], osem[slot])

        def start_gather(c, slot):
            pltpu.sync_copy(idx_hbm.at[pl.ds(base + c * chunk, chunk)], idx_v[slot])
            gather_copy(slot).start()

        start_gather(0, 0)

        @pl.loop(0, n_chunks, step=2)
        def _(j):
            for b in range(2):
                c = j + b
                cur, other = b, 1 - b

                @pl.when(c >= 1)
                def _():
                    out_copy(c - 1, other).wait()

                @pl.when(c + 1 < n_chunks)
                def _():
                    start_gather(c + 1, other)

                gather_copy(cur).wait()
                out_copy(c, cur).start()

        out_copy(n_chunks - 1, 1).wait()

    return gather(table, idx)


def _combine_kernel(h1_ref, wts_ref, wsgu_ref, wsd_ref, g_ref, b_ref, yg_ref, out_ref):
    h1 = h1_ref[...]
    gu = _dot(h1.astype(BF16), wsgu_ref[...])
    hs = _silu(gu[:, :D_SHARED]) * gu[:, D_SHARED:]
    acc = DEEPNORM_ALPHA * h1 + _dot(hs.astype(BF16), wsd_ref[...])
    wts = wts_ref[...]
    acc_hi = acc[:, :D_PACK]
    acc_lo = acc[:, D_PACK:]
    for k in range(TOP_K):
        y_hi, y_lo = _unpack_rows(yg_ref[k])
        acc_hi = acc_hi + y_hi * wts[:, k:k + 1]
        acc_lo = acc_lo + y_lo * wts[:, k:k + 1]
    out_ref[...] = _layer_norm(jnp.concatenate([acc_hi, acc_lo], axis=1), g_ref[...], b_ref[...])


def _combine(h1, wts_t, ws_gu, ws_down, g, b, yg, tm):
    t = h1.shape[0]
    const = lambda shape: pl.BlockSpec(shape, lambda i: (0,) * len(shape))
    return pl.pallas_call(
        _combine_kernel,
        grid=(t // tm,),
        in_specs=[
            pl.BlockSpec((tm, D_MODEL), lambda i: (i, 0)),
            pl.BlockSpec((tm, TOP_K), lambda i: (i, 0)),
            const((D_MODEL, 2 * D_SHARED)), const((D_SHARED, D_MODEL)),
            const((1, D_MODEL)), const((1, D_MODEL)),
            pl.BlockSpec((TOP_K, tm, D_PACK), lambda i: (0, i, 0)),
        ],
        out_specs=pl.BlockSpec((tm, D_MODEL), lambda i: (i, 0)),
        out_shape=jax.ShapeDtypeStruct((t, D_MODEL), F32),
        compiler_params=_params(("arbitrary",)),
        name="moe_combine",
    )(h1, wts_t, ws_gu, ws_down, g, b, yg)


def _block_diag(w):
    nb, bi, bo = w.shape
    eye = jnp.eye(nb, dtype=w.dtype)
    return (eye[:, None, :, None] * w[:, :, None, :]).reshape(nb * bi, nb * bo)


def _pad_lanes(v, offset, width):
    return jnp.zeros((1, width), F32).at[0, offset:offset + v.shape[0]].set(v)


def _layer(h_in_x, l, p, tiles):
    bsz, seq, _ = h_in_x.shape
    t = bsz * seq
    row = lambda v: v.reshape(1, -1)

    w_in = p['w_in'][l]
    w_main = w_in[:, :N_MAIN].astype(BF16)
    w_small = jnp.zeros((D_MODEL, LANES), F32).at[:, :2 * GDN_HEADS].set(w_in[:, N_MAIN:])
    zeros = lambda n: jnp.zeros((CONV_WIDTH, n), F32)
    conv_w = jnp.concatenate([p['lru_conv_w'][l], zeros(LRU_WIDTH), p['gdn_conv_w'][l], zeros(GDN_V)], 1)
    conv_b = jnp.zeros((1, N_MAIN), F32).at[0, :LRU_WIDTH].set(p['lru_conv_b'][l])
    h0, proj, small, small_t = _inproj(h_in_x.reshape(t, D_MODEL), row(p['ln_g']), row(p['ln_b']),
                                       w_main, w_small, conv_w, conv_b, tiles['inproj'], seq)
    proj3 = proj.reshape(bsz, seq, N_MAIN)

    w_gates = jnp.concatenate([_block_diag(p['lru_w_rg'][l]), _block_diag(p['lru_w_ig'][l])], 1).astype(BF16)
    b_gates = jnp.concatenate([p['lru_b_rg'][l], p['lru_b_ig'][l]]).reshape(1, -1)
    rows = GDN_GROUP * GDN_CHUNK
    small3 = small.reshape(bsz, seq, LANES)
    smallt3 = small_t.reshape(8, bsz, seq // rows, rows).transpose(1, 2, 0, 3)
    a_log, dt_bias = p['gdn_a_log'][l], p['gdn_dt_bias'][l]
    alr = _pad_lanes(a_log, GDN_HEADS, LANES)
    dtr = _pad_lanes(dt_bias, GDN_HEADS, LANES)
    alc = _pad_lanes(a_log, GDN_HEADS, 8).reshape(8, 1)
    dtc = _pad_lanes(dt_bias, GDN_HEADS, 8).reshape(8, 1)
    y_lru, y_gdn = _mixers(proj3, small3, smallt3, w_gates, b_gates, row(p['lru_lambda'][l]), row(p['lru_out_g'][l]),
                           alr, dtr, alc, dtc, row(p['gdn_norm_w'][l]), tiles['gdn_nb'])

    w_out = p['w_out'][l].astype(BF16)
    h1, h1p, top_e, wts, rank, counts = _router(
        y_lru.reshape(t, LRU_WIDTH), y_gdn.reshape(t, GDN_V), h0, w_out[:LRU_WIDTH], w_out[LRU_WIDTH:],
        row(p['ln1_g'][l]), row(p['ln1_b'][l]), p['w_router'][l].T, p['router_bias'][l].reshape(-1, 1),
        tiles['router'])

    counts = counts[:, 0]
    padded = (counts + MOE_BLOCK - 1) // MOE_BLOCK * MOE_BLOCK
    pad_end = jnp.cumsum(padded)
    pad_start = pad_end - padded
    n_blocks = (t * TOP_K) // MOE_BLOCK + N_EXPERTS
    n_rows = n_blocks * MOE_BLOCK
    n_used = (pad_end[-1] // MOE_BLOCK).astype(jnp.int32)
    blk_ids = jnp.minimum(jnp.arange(n_blocks, dtype=jnp.int32), n_used - 1)
    blk_e = jnp.minimum(jnp.sum(pad_end[None, :] <= (blk_ids * MOE_BLOCK)[:, None], axis=1),
                        N_EXPERTS - 1).astype(jnp.int32)

    dest = _dest(top_e, rank, pad_start.reshape(-1, 1), tiles['dest'])
    n_valid = jnp.clip(counts[blk_e] - (blk_ids * MOE_BLOCK - pad_start[blk_e]), 0, MOE_BLOCK).astype(jnp.int32)
    xs = _sc_scatter_rows(h1p, dest, n_rows, SC_CHUNK)
    active = jnp.arange(n_blocks, dtype=jnp.int32) < n_used
    first = (active & jnp.concatenate([jnp.ones((1,), bool), blk_e[1:] != blk_e[:-1]])).astype(jnp.int32)
    slot = ((jnp.cumsum(first) - 1) % 2).astype(jnp.int32)
    used = counts > 0
    later = jnp.where(used[None, :] & (jnp.arange(N_EXPERTS)[None, :] > jnp.arange(N_EXPERTS)[:, None]),
                      jnp.arange(N_EXPERTS, dtype=jnp.int32)[None, :], N_EXPERTS)
    next_used = jnp.min(later, axis=1)
    next_e = jnp.where(next_used < N_EXPERTS, next_used, -1)[blk_e].astype(jnp.int32)
    ys = _experts(blk_e, n_valid, first, slot, next_e, n_used.reshape(1), xs,
                  p['w_gate'][l], p['w_up'][l], p['w_down'][l])
    ws_gu = jnp.concatenate([p['ws_gate'][l], p['ws_up'][l]], 1).astype(BF16)
    yg = _sc_gather_rows(ys, dest.reshape(TOP_K * t), SC_CHUNK).reshape(TOP_K, t, D_PACK)
    out = _combine(h1, wts.T, ws_gu, p['ws_down'][l].astype(BF16),
                   row(p['ln2_g'][l]), row(p['ln2_b'][l]), yg, tiles['combine'])
    return out.reshape(bsz, seq, D_MODEL)


def _tiles(bsz, seq):
    t = bsz * seq
    return {
        'inproj': min(512, t),
        'gdn_nb': bsz,
        'router': min(512, t),
        'dest': min(512, t),
        'combine': min(256, t),
    }


def kernel(x, ln_in_g, ln_in_b, w_in, lru_conv_w, lru_conv_b, lru_w_rg, lru_b_rg, lru_w_ig, lru_b_ig,
           lru_lambda, lru_out_g, gdn_conv_w, gdn_a_log, gdn_dt_bias, gdn_norm_w, w_out, ln1_g, ln1_b,
           w_router, router_bias, w_gate, w_up, w_down, ws_gate, ws_up, ws_down, ln2_g, ln2_b):
    assert w_in.shape[0] == DEPTH == 1
    p = dict(ln_g=ln_in_g, ln_b=ln_in_b, w_in=w_in, lru_conv_w=lru_conv_w, lru_conv_b=lru_conv_b,
             lru_w_rg=lru_w_rg, lru_b_rg=lru_b_rg, lru_w_ig=lru_w_ig, lru_b_ig=lru_b_ig,
             lru_lambda=lru_lambda, lru_out_g=lru_out_g, gdn_conv_w=gdn_conv_w, gdn_a_log=gdn_a_log,
             gdn_dt_bias=gdn_dt_bias, gdn_norm_w=gdn_norm_w, w_out=w_out, ln1_g=ln1_g, ln1_b=ln1_b,
             w_router=w_router, router_bias=router_bias, w_gate=w_gate, w_up=w_up, w_down=w_down,
             ws_gate=ws_gate, ws_up=ws_up, ws_down=ws_down, ln2_g=ln2_g, ln2_b=ln2_b)
    bsz, seq, _ = x.shape
    return _layer(x, 0, p, _tiles(bsz, seq))
```

```python
import functools

import jax
import jax.numpy as jnp
from jax import lax
from jax.experimental import pallas as pl
from jax.experimental.pallas import tpu as pltpu
from jax.experimental.pallas import tpu_sc as plsc

F32 = jnp.float32
BF16 = jnp.bfloat16

D_MODEL = 1024
LRU_WIDTH = 512
LRU_BLOCKS = 8
LRU_C = 8.0
CONV_WIDTH = 4
GDN_HEADS = 4
GDN_DK = 128
GDN_DV = 128
GDN_CHUNK = 64
GDN_GROUP = 2
GDN_QK = GDN_HEADS * GDN_DK
GDN_V = GDN_HEADS * GDN_DV
N_MAIN = 2 * LRU_WIDTH + 2 * GDN_QK + 2 * GDN_V
N_EXPERTS = 256
TOP_K = 8
N_GROUPS = 8
GROUP_SIZE = N_EXPERTS // N_GROUPS
TOPK_GROUPS = 4
D_EXPERT = 256
D_SHARED = 256
ROUTED_SCALE = 2.5
MOE_BLOCK = 640
D_PACK = D_MODEL // 2
LN_EPS = 1e-5
NORM_EPS = 1e-6
DEPTH = 1
DEEPNORM_ALPHA = (2.0 * DEPTH) ** 0.25

SCAN_GROUP = 8
HALO = 8
CONV_GROUP = 512
CONV_GROUPS = (0, 2, 3, 4)
LANES = 128
VMEM_LIMIT = 56 * 1024 * 1024
ROUTER_SUB = 256
EXPERT_BANDS = 5
SC_CORES = 2
SC_WORKERS = 32
SC_CHUNK = 64

NN = (((1,), (0,)), ((), ()))
NT = (((1,), (1,)), ((), ()))
TN = (((0,), (0,)), ((), ()))


def _dot(a, b, dims=NN):
    return lax.dot_general(a, b, dims, preferred_element_type=F32)


def _split(a):
    hi = a.astype(BF16)
    lo = (a - hi.astype(F32)).astype(BF16)
    return hi, lo


def _dot3(a, b, dims=NN):
    ah, al = _split(a)
    bh, bl = _split(b)
    return _dot(ah, bh, dims) + (_dot(ah, bl, dims) + _dot(al, bh, dims))


def _layer_norm(x, g, b):
    mu = jnp.mean(x, -1, keepdims=True)
    xc = x - mu
    var = jnp.mean(xc * xc, -1, keepdims=True)
    return xc * lax.rsqrt(var + LN_EPS) * g + b


def _sigmoid(x):
    return 1.0 / (1.0 + jnp.exp(-x))


def _silu(x):
    return x * _sigmoid(x)


def _softplus(x):
    return jnp.maximum(x, 0.0) + jnp.log1p(jnp.exp(-jnp.abs(x)))


def _gelu_tanh(x):
    c = 0.7978845608028654
    return x * (0.5 * (1.0 + jnp.tanh(c * (x + 0.044715 * (x * x * x)))))


def _pack_rows(x):
    hi = lax.bitcast_convert_type(x[:, :D_PACK].astype(BF16).astype(F32), jnp.uint32)
    lo = lax.bitcast_convert_type(x[:, D_PACK:].astype(BF16).astype(F32), jnp.uint32)
    return (hi & jnp.uint32(0xFFFF0000)) | (lo >> 16)


def _unpack_rows(w):
    hi = lax.bitcast_convert_type(w & jnp.uint32(0xFFFF0000), F32)
    lo = lax.bitcast_convert_type(w << 16, F32)
    return hi, lo


def _params(sem, **kw):
    return pltpu.CompilerParams(dimension_semantics=sem, vmem_limit_bytes=VMEM_LIMIT, **kw)


def _inproj_kernel(x_ref, g_ref, b_ref, w_ref, ws_ref, cw_ref, cb_ref,
                   h_ref, proj_ref, small_ref, smallt_ref, hist, *, tiles_per_seq):
    i = pl.program_id(0)
    tm = x_ref.shape[0]
    h = _layer_norm(x_ref[...], g_ref[...], b_ref[...])
    h_ref[...] = h
    hb = h.astype(BF16)

    @pl.when(i % tiles_per_seq == 0)
    def _():
        hist[...] = jnp.zeros_like(hist)

    for g in range(N_MAIN // CONV_GROUP):
        cols = slice(g * CONV_GROUP, (g + 1) * CONV_GROUP)
        p = _dot(hb, w_ref[:, cols])
        if g in CONV_GROUPS:
            xcat = jnp.concatenate([hist[:, cols], p], axis=0)
            acc = cb_ref[:, cols]
            for j in range(CONV_WIDTH):
                off = HALO - (CONV_WIDTH - 1) + j
                acc = acc + xcat[off:off + tm, :] * cw_ref[j:j + 1, cols]
            hist[:, cols] = p[tm - HALO:, :]
            p = acc
        proj_ref[:, cols] = p
    small = _dot3(h, ws_ref[...])
    small_ref[...] = small
    smallt_ref[...] = small.T[:smallt_ref.shape[0], :]


def _inproj(x2d, g, b, w_main, w_small, conv_w, conv_b, tm, seq):
    t = x2d.shape[0]
    return pl.pallas_call(
        functools.partial(_inproj_kernel, tiles_per_seq=seq // tm),
        grid=(t // tm,),
        in_specs=[
            pl.BlockSpec((tm, D_MODEL), lambda i: (i, 0)),
            pl.BlockSpec((1, D_MODEL), lambda i: (0, 0)),
            pl.BlockSpec((1, D_MODEL), lambda i: (0, 0)),
            pl.BlockSpec((D_MODEL, N_MAIN), lambda i: (0, 0)),
            pl.BlockSpec((D_MODEL, LANES), lambda i: (0, 0)),
            pl.BlockSpec((CONV_WIDTH, N_MAIN), lambda i: (0, 0)),
            pl.BlockSpec((1, N_MAIN), lambda i: (0, 0)),
        ],
        out_specs=[
            pl.BlockSpec((tm, D_MODEL), lambda i: (i, 0)),
            pl.BlockSpec((tm, N_MAIN), lambda i: (i, 0)),
            pl.BlockSpec((tm, LANES), lambda i: (i, 0)),
            pl.BlockSpec((8, tm), lambda i: (0, i)),
        ],
        out_shape=[
            jax.ShapeDtypeStruct((t, D_MODEL), F32),
            jax.ShapeDtypeStruct((t, N_MAIN), F32),
            jax.ShapeDtypeStruct((t, LANES), F32),
            jax.ShapeDtypeStruct((8, t), F32),
        ],
        scratch_shapes=[pltpu.VMEM((HALO, N_MAIN), F32)],
        compiler_params=_params(("arbitrary",)),
        name="ln_inproj",
    )(x2d, g, b, w_main, w_small, conv_w, conv_b)


def _lru_tile(xc, gate, wg, bg, lam, og, carry):
    rows = xc.shape[0]
    gates = _dot(xc.astype(BF16), wg) + bg
    r = _sigmoid(gates[:, :LRU_WIDTH])
    i = _sigmoid(gates[:, LRU_WIDTH:])
    log_a = (-LRU_C) * r * _softplus(-lam)
    a = jnp.exp(log_a)
    one_minus_a2 = -jnp.tanh(log_a) * (a * a + 1.0)
    mult = jnp.where(one_minus_a2 > 0.0, one_minus_a2 * lax.rsqrt(one_minus_a2), 0.0)
    bv = mult * (i * xc)
    a = a.reshape(rows // SCAN_GROUP, SCAN_GROUP, LRU_WIDTH)
    bv = bv.reshape(rows // SCAN_GROUP, SCAN_GROUP, LRU_WIDTH)
    row_in_group = lax.broadcasted_iota(jnp.int32, a.shape, 1)
    d = 1
    while d < SCAN_GROUP:
        a_sh = jnp.where(row_in_group < d, 1.0, pltpu.roll(a, d, 1))
        b_sh = jnp.where(row_in_group < d, 0.0, pltpu.roll(bv, d, 1))
        bv = a * b_sh + bv
        a = a * a_sh
        d *= 2
    a = a.reshape(rows, LRU_WIDTH)
    bv = bv.reshape(rows, LRU_WIDTH)
    parts = []
    for g in range(rows // SCAN_GROUP):
        grp = slice(g * SCAN_GROUP, (g + 1) * SCAN_GROUP)
        hg = a[grp] * carry + bv[grp]
        carry = hg[SCAN_GROUP - 1:, :]
        parts.append(hg)
    h = jnp.concatenate(parts, axis=0)
    y = h * _gelu_tanh(gate)
    ms = jnp.mean(y * y, -1, keepdims=True)
    return y * lax.rsqrt(ms + NORM_EPS) * og, carry


def _bdot(a, b, dims=NN):
    return _dot(a.astype(BF16), b.astype(BF16), dims)


def _gdn_heads(args, norm_w):
    c = GDN_CHUNK
    r = GDN_GROUP * c
    ri = lax.broadcasted_iota(jnp.int32, (r, r), 0)
    ci = lax.broadcasted_iota(jnp.int32, (r, r), 1)
    same = (ri // c) == (ci // c)
    causal = same & (ri >= ci)
    strict = same & (ri > ci)
    upper = same & (ri <= ci)
    chunk_of_row = lax.broadcasted_iota(jnp.int32, (r, 1), 0) // c
    each = lambda f, *ls: [f(*xs) for xs in zip(*ls)]
    q, k, v, z, beta, g_col, g_row, st = [list(x) for x in zip(*args)]
    q = each(lambda x: x * lax.rsqrt(jnp.sum(x * x, -1, keepdims=True) + NORM_EPS) * (GDN_DK ** -0.5), q)
    k = each(lambda x: x * lax.rsqrt(jnp.sum(x * x, -1, keepdims=True) + NORM_EPS), k)
    gc_col = each(lambda g: jnp.sum(jnp.where(causal, g, 0.0), axis=1, keepdims=True), g_row)
    gc_row = each(lambda g: jnp.sum(jnp.where(upper, g, 0.0), axis=0, keepdims=True), g_col)
    decay = each(lambda gc, gr: jnp.exp(jnp.where(causal, gc - gr, -jnp.inf)), gc_col, gc_row)
    kb = each(lambda x, bt: x * bt, k, beta)
    vb = each(lambda x, bt: x * bt, v, beta)
    kk = each(lambda x, y: _bdot(x, y, NT), kb, k)
    a_mat = each(lambda m, d: jnp.where(strict, m * d, 0.0), kk, decay)
    e_col = each(jnp.exp, gc_col)
    rhs = each(lambda x, y, e: jnp.concatenate([x, y * e], axis=1), vb, kb, e_col)
    sol = each(lambda rr, a: rr - _bdot(a, rr), rhs, a_mat)
    p = a_mat
    for _ in range(5):
        p = each(lambda x: _bdot(x, x), p)
        sol = each(lambda x, y: y + _bdot(x, y), p, sol)
    qk = each(lambda x, y: _bdot(x, y, NT), q, k)
    qk = each(lambda m, d: jnp.where(causal, m * d, 0.0), qk, decay)
    q_dec = each(lambda x, e: x * e, q, e_col)
    g_last = [each(lambda gc: gc[(j + 1) * c - 1:(j + 1) * c, :], gc_col) for j in range(GDN_GROUP)]

    def last_of_own_chunk(*gl):
        out = gl[-1]
        for j in range(GDN_GROUP - 2, -1, -1):
            out = jnp.where(chunk_of_row == j, gl[j], out)
        return out

    g_end = each(last_of_own_chunk, *g_last)
    k_dec = each(lambda x, ge, gc: x * jnp.exp(ge - gc), k, g_end, gc_col)
    qs_parts, v_parts = [], []
    for j in range(GDN_GROUP):
        rows = slice(j * c, (j + 1) * c)
        ws = each(lambda x, s: _bdot(x[rows, GDN_DV:], s), sol, st)
        qs_parts.append(each(lambda x, s: _bdot(x[rows], s), q_dec, st))
        v_new = each(lambda x, w: x[rows, :GDN_DV] - w, sol, ws)
        v_parts.append(v_new)
        kv = each(lambda x, vn: _bdot(x[rows], vn, TN), k_dec, v_new)
        st = each(lambda s, gl, d: s * jnp.exp(gl) + d, st, g_last[j], kv)
    qs = each(lambda *parts: jnp.concatenate(parts, axis=0), *qs_parts)
    v_all = each(lambda *parts: jnp.concatenate(parts, axis=0), *v_parts)
    o = each(lambda a, m, vn: a + _bdot(m, vn), qs, qk, v_all)
    o = each(lambda x: x * lax.rsqrt(jnp.mean(x * x, -1, keepdims=True) + NORM_EPS) * norm_w, o)
    o = each(lambda x, zz: x * _silu(zz), o, z)
    return list(zip(o, st))


def _mixer_kernel(xc_ref, gate_ref, q_ref, k_ref, v_ref, z_ref, sm_ref, smt_ref,
                  wg_ref, bg_ref, lam_ref, og_ref, alr_ref, dtr_ref, alc_ref, dtc_ref, nw_ref,
                  ylru_ref, y_ref, hcarry, state):
    n = pl.program_id(1)
    c = GDN_GROUP * GDN_CHUNK
    nb = q_ref.shape[0]
    first = n == 0

    @pl.when(first)
    def _():
        state[...] = jnp.zeros_like(state)
        hcarry[...] = jnp.zeros_like(hcarry)

    lru_out = [_lru_tile(xc_ref[b], gate_ref[b], wg_ref[...], bg_ref[...], lam_ref[...], og_ref[...], hcarry[b])
               for b in range(nb)]
    norm_w = nw_ref[...]

    args = []
    for b in range(nb):
        q_all = _silu(q_ref[b])
        k_all = _silu(k_ref[b])
        v_all = _silu(v_ref[b])
        z_all = z_ref[b]
        sm = sm_ref[b]
        beta_all = _sigmoid(sm)
        g_cols = -jnp.exp(alr_ref[...]) * _softplus(sm + dtr_ref[...])
        g_rows = -jnp.exp(alc_ref[...]) * _softplus(smt_ref[b] + dtc_ref[...])
        for hd in range(GDN_HEADS):
            sl = slice(hd * GDN_DK, (hd + 1) * GDN_DK)
            args.append((q_all[:, sl], k_all[:, sl], v_all[:, sl], z_all[:, sl],
                         beta_all[:, hd:hd + 1],
                         g_cols[:, GDN_HEADS + hd:GDN_HEADS + hd + 1],
                         g_rows[GDN_HEADS + hd:GDN_HEADS + hd + 1, :],
                         state[b, hd]))
    outs = _gdn_heads(args, norm_w)
    for b in range(nb):
        for hd in range(GDN_HEADS):
            o, st_new = outs[b * GDN_HEADS + hd]
            state[b, hd] = st_new
            y_ref[b, :, hd * GDN_DK:(hd + 1) * GDN_DK] = o
    for b, (y_lru, carry) in enumerate(lru_out):
        ylru_ref[b] = y_lru
        hcarry[b] = carry


def _mixers(proj3, small3, smallt3, w_gates, b_gates, lam, out_g, alr, dtr, alc, dtc, norm_w, nb):
    bsz, seq, _ = proj3.shape
    c = GDN_GROUP * GDN_CHUNK
    nch = seq // c
    col = lambda j: pl.BlockSpec((nb, c, GDN_QK), lambda b, n: (b, n, j))
    const = lambda shape: pl.BlockSpec(shape, lambda b, n: (0,) * len(shape))
    return pl.pallas_call(
        _mixer_kernel,
        grid=(bsz // nb, nch),
        in_specs=[
            col(0), col(1), col(2), col(3), col(4), col(5),
            pl.BlockSpec((nb, c, LANES), lambda b, n: (b, n, 0)),
            pl.BlockSpec((nb, None, 8, c), lambda b, n: (b, n, 0, 0)),
            const((LRU_WIDTH, 2 * LRU_WIDTH)), const((1, 2 * LRU_WIDTH)), const((1, LRU_WIDTH)), const((1, LRU_WIDTH)),
            const((1, LANES)), const((1, LANES)), const((8, 1)), const((8, 1)),
            const((1, GDN_DV)),
        ],
        out_specs=[pl.BlockSpec((nb, c, LRU_WIDTH), lambda b, n: (b, n, 0)),
                   pl.BlockSpec((nb, c, GDN_V), lambda b, n: (b, n, 0))],
        out_shape=[jax.ShapeDtypeStruct((bsz, seq, LRU_WIDTH), F32),
                   jax.ShapeDtypeStruct((bsz, seq, GDN_V), F32)],
        scratch_shapes=[
            pltpu.VMEM((nb, 1, LRU_WIDTH), F32),
            pltpu.VMEM((nb, GDN_HEADS, GDN_DK, GDN_DV), F32),
        ],
        compiler_params=_params(("arbitrary", "arbitrary")),
        name="mixers",
    )(proj3, proj3, proj3, proj3, proj3, proj3, small3, smallt3, w_gates, b_gates, lam, out_g,
      alr, dtr, alc, dtc, norm_w)


def _pick_experts(logits, rbias):
    n = logits.shape[1]
    scores = _sigmoid(logits)
    choice = scores + rbias
    neg = -jnp.inf
    gs_rows = []
    sub = lax.broadcasted_iota(jnp.int32, (GROUP_SIZE, n), 0).astype(F32)
    for g in range(N_GROUPS):
        cg = choice[g * GROUP_SIZE:(g + 1) * GROUP_SIZE, :]
        m1 = jnp.max(cg, axis=0, keepdims=True)
        i1 = jnp.min(jnp.where(cg == m1, sub, float(GROUP_SIZE)), axis=0, keepdims=True)
        m2 = jnp.max(jnp.where(sub == i1, neg, cg), axis=0, keepdims=True)
        gs_rows.append(m1 + m2)
    gs = jnp.concatenate(gs_rows, axis=0)
    gi = lax.broadcasted_iota(jnp.int32, (N_GROUPS, n), 0).astype(F32)
    gsel = jnp.zeros((N_GROUPS, n), jnp.bool_)
    for _ in range(TOPK_GROUPS):
        m = jnp.max(gs, axis=0, keepdims=True)
        idx = jnp.min(jnp.where(gs == m, gi, float(N_GROUPS)), axis=0, keepdims=True)
        hit = gi == idx
        gsel = jnp.logical_or(gsel, hit)
        gs = jnp.where(hit, neg, gs)
    masked = jnp.concatenate(
        [jnp.where(gsel[g:g + 1, :], choice[g * GROUP_SIZE:(g + 1) * GROUP_SIZE, :], neg)
         for g in range(N_GROUPS)], axis=0)
    ei = lax.broadcasted_iota(jnp.int32, (N_EXPERTS, n), 0).astype(F32)
    hits, e_rows, w_rows = [], [], []
    multi = jnp.zeros((N_EXPERTS, n), F32)
    for _ in range(TOP_K):
        m = jnp.max(masked, axis=0, keepdims=True)
        idx = jnp.min(jnp.where(masked == m, ei, float(N_EXPERTS)), axis=0, keepdims=True)
        hit = ei == idx
        hits.append(hit)
        e_rows.append(idx)
        w_rows.append(jnp.sum(jnp.where(hit, scores, 0.0), axis=0, keepdims=True))
        multi = multi + hit.astype(F32)
        masked = jnp.where(hit, neg, masked)
    wts = jnp.concatenate(w_rows, axis=0)
    wts = wts / (jnp.sum(wts, axis=0, keepdims=True) + 1e-20) * ROUTED_SCALE
    return jnp.concatenate(e_rows, axis=0), wts, hits, multi


def _router_kernel(yl_ref, yg_ref, h0_ref, wo1_ref, wo2_ref, g_ref, b_ref, wrt_ref, rb_ref,
                   h1_ref, h1p_ref, e_ref, w_ref, rank_ref, cnt_ref, carry):
    i = pl.program_id(0)
    tm = h0_ref.shape[0]
    n = min(ROUTER_SUB, tm)
    subs = [slice(j * n, (j + 1) * n) for j in range(tm // n)]

    @pl.when(i == 0)
    def _():
        carry[...] = jnp.zeros_like(carry)

    mixes = [_dot(yl_ref[r, :].astype(BF16), wo1_ref[...]) + _dot(yg_ref[r, :].astype(BF16), wo2_ref[...])
             for r in subs]
    h1s = [_layer_norm(DEEPNORM_ALPHA * h0_ref[r, :] + mix, g_ref[...], b_ref[...])
           for r, mix in zip(subs, mixes)]
    for r, h1 in zip(subs, h1s):
        h1_ref[r, :] = h1
        h1p_ref[r, :] = _pack_rows(h1)
    logits = [_dot3(wrt_ref[...], h1, NT) for h1 in h1s]
    picks = [_pick_experts(lg, rb_ref[...]) for lg in logits]
    ti = lax.broadcasted_iota(jnp.int32, (n, n), 0)
    tj = lax.broadcasted_iota(jnp.int32, (n, n), 1)
    before = (ti < tj).astype(BF16)
    cums = [_dot(multi.astype(BF16), before) for _, _, _, multi in picks]
    base = carry[...]
    for r, (e_rows, wts, hits, multi), cum in zip(subs, picks, cums):
        cum = cum + base
        r_rows = [jnp.sum(jnp.where(hit, cum, 0.0), axis=0, keepdims=True) for hit in hits]
        base = base + jnp.sum(multi, axis=1, keepdims=True)
        e_ref[:, r] = e_rows.astype(jnp.int32)
        w_ref[:, r] = wts
        rank_ref[:, r] = jnp.concatenate(r_rows, axis=0).astype(jnp.int32)
    carry[...] = base
    cnt_ref[...] = base.astype(jnp.int32)


def _router(y_lru, y_gdn, h0, wo1, wo2, g, b, w_router_t, rbias, tm):
    t = h0.shape[0]
    const = lambda shape: pl.BlockSpec(shape, lambda i: (0,) * len(shape))
    return pl.pallas_call(
        _router_kernel,
        grid=(t // tm,),
        in_specs=[
            pl.BlockSpec((tm, LRU_WIDTH), lambda i: (i, 0)),
            pl.BlockSpec((tm, GDN_V), lambda i: (i, 0)),
            pl.BlockSpec((tm, D_MODEL), lambda i: (i, 0)),
            const((LRU_WIDTH, D_MODEL)), const((GDN_V, D_MODEL)),
            const((1, D_MODEL)), const((1, D_MODEL)),
            const((N_EXPERTS, D_MODEL)), const((N_EXPERTS, 1)),
        ],
        out_specs=[
            pl.BlockSpec((tm, D_MODEL), lambda i: (i, 0)),
            pl.BlockSpec((tm, D_PACK), lambda i: (i, 0)),
            pl.BlockSpec((TOP_K, tm), lambda i: (0, i)),
            pl.BlockSpec((TOP_K, tm), lambda i: (0, i)),
            pl.BlockSpec((TOP_K, tm), lambda i: (0, i)),
            const((N_EXPERTS, 1)),
        ],
        out_shape=[
            jax.ShapeDtypeStruct((t, D_MODEL), F32),
            jax.ShapeDtypeStruct((t, D_PACK), jnp.uint32),
            jax.ShapeDtypeStruct((TOP_K, t), jnp.int32),
            jax.ShapeDtypeStruct((TOP_K, t), F32),
            jax.ShapeDtypeStruct((TOP_K, t), jnp.int32),
            jax.ShapeDtypeStruct((N_EXPERTS, 1), jnp.int32),
        ],
        scratch_shapes=[pltpu.VMEM((N_EXPERTS, 1), F32)],
        compiler_params=_params(("arbitrary",)),
        name="outproj_router",
    )(y_lru, y_gdn, h0, wo1, wo2, g, b, w_router_t, rbias)


def _dest_kernel(e_ref, r_ref, ps_ref, d_ref):
    tm = e_ref.shape[1]
    ei = lax.broadcasted_iota(jnp.int32, (N_EXPERTS, tm), 0)
    rows = []
    for k in range(TOP_K):
        hit = ei == e_ref[k:k + 1, :]
        rows.append(jnp.sum(jnp.where(hit, ps_ref[...], 0), axis=0, keepdims=True))
    d_ref[...] = jnp.concatenate(rows, axis=0) + r_ref[...]


def _dest(top_e, rank, pad_start, tm):
    t = top_e.shape[1]
    blk = pl.BlockSpec((TOP_K, tm), lambda i: (0, i))
    return pl.pallas_call(
        _dest_kernel,
        grid=(t // tm,),
        in_specs=[blk, blk, pl.BlockSpec((N_EXPERTS, 1), lambda i: (0, 0))],
        out_specs=blk,
        out_shape=jax.ShapeDtypeStruct((TOP_K, t), jnp.int32),
        compiler_params=_params(("arbitrary",)),
        name="moe_dest",
    )(top_e, rank, pad_start)


def _sc_scatter_rows(rows, idx, n_out, chunk):
    n_copies, t = idx.shape
    d = rows.shape[1]
    per_worker = t // SC_WORKERS
    n_chunks = per_worker // chunk
    mesh = plsc.VectorSubcoreMesh(core_axis_name="c", subcore_axis_name="s")
    idx_flat = idx.reshape(n_copies * t)

    @functools.partial(
        pl.kernel, mesh=mesh,
        out_type=jax.ShapeDtypeStruct((n_out, d), rows.dtype),
        scratch_types=[pltpu.VMEM((chunk,), jnp.int32) for _ in range(n_copies)] + [
            pltpu.VMEM((chunk, d), rows.dtype),
            pltpu.SemaphoreType.DMA,
        ],
    )
    def scatter(rows_hbm, idx_hbm, out_hbm, *scratch):
        idx_v = scratch[:n_copies]
        rows_v, sem = scratch[n_copies:]
        wid = lax.axis_index("s") * SC_CORES + lax.axis_index("c")
        base = wid * per_worker

        @pl.loop(0, n_chunks)
        def _(j):
            off = base + j * chunk
            for k in range(n_copies):
                pltpu.sync_copy(idx_hbm.at[pl.ds(k * t + off, chunk)], idx_v[k])
            pltpu.sync_copy(rows_hbm.at[pl.ds(off, chunk)], rows_v)
            copies = [pltpu.async_copy(rows_v, out_hbm.at[idx_v[k]], sem) for k in range(n_copies)]
            for cp in copies:
                cp.wait()

    return scatter(rows, idx_flat)


def _expert_kernel(be_ref, nv_ref, first_ref, slot_ref, next_ref, nu_ref,
                   xs_ref, wg_hbm, wu_hbm, wd_hbm, ys_ref, wg_f, wu_f, wd_f, wgu_b, wd_b, sem):
    i = pl.program_id(0)

    def fetch(e, slot):
        return (pltpu.make_async_copy(wg_hbm.at[e], wg_f.at[slot], sem.at[slot]),
                pltpu.make_async_copy(wu_hbm.at[e], wu_f.at[slot], sem.at[slot]),
                pltpu.make_async_copy(wd_hbm.at[e], wd_f.at[slot], sem.at[slot]))

    @pl.when(i < nu_ref[0])
    def _():
        e = be_ref[i]
        slot = slot_ref[i]

        @pl.when(first_ref[i] == 1)
        def _():
            @pl.when(i == 0)
            def _():
                for cp in fetch(e, slot):
                    cp.start()

            for cp in fetch(e, slot):
                cp.wait()

            @pl.when(next_ref[i] >= 0)
            def _():
                for cp in fetch(next_ref[i], 1 - slot):
                    cp.start()

            wgu_b[:, :D_EXPERT] = wg_f[slot].astype(BF16)
            wgu_b[:, D_EXPERT:] = wu_f[slot].astype(BF16)
            wd_b[...] = wd_f[slot].astype(BF16)

        n = xs_ref.shape[0] // EXPERT_BANDS
        bands = [slice(j * n, (j + 1) * n) for j in range(EXPERT_BANDS)]
        row = lax.broadcasted_iota(jnp.int32, (n, D_PACK), 0)
        xs = [_unpack_rows(jnp.where(row + j * n < nv_ref[i], xs_ref[r, :], jnp.uint32(0)))
              for j, r in enumerate(bands)]
        gus = [_dot(x_hi.astype(BF16), wgu_b[:D_PACK, :]) + _dot(x_lo.astype(BF16), wgu_b[D_PACK:, :])
               for x_hi, x_lo in xs]
        hs = [_silu(gu[:, :D_EXPERT]) * gu[:, D_EXPERT:] for gu in gus]
        ys = [_dot(h.astype(BF16), wd_b[...]) for h in hs]
        for r, y in zip(bands, ys):
            ys_ref[r, :] = _pack_rows(y)


def _experts(blk_e, n_valid, first, slot, next_e, n_used, xs, w_gate, w_up, w_down):
    n_rows = xs.shape[0]
    n_blocks = n_rows // MOE_BLOCK
    blk = lambda i, be, nv, fi, sl, nx, nu: (jnp.minimum(i, nu[0] - 1), 0)
    return pl.pallas_call(
        _expert_kernel,
        grid_spec=pltpu.PrefetchScalarGridSpec(
            num_scalar_prefetch=6,
            grid=(n_blocks,),
            in_specs=[
                pl.BlockSpec((MOE_BLOCK, D_PACK), blk),
                pl.BlockSpec(memory_space=pl.ANY),
                pl.BlockSpec(memory_space=pl.ANY),
                pl.BlockSpec(memory_space=pl.ANY),
            ],
            out_specs=pl.BlockSpec((MOE_BLOCK, D_PACK), blk),
            scratch_shapes=[
                pltpu.VMEM((2, D_MODEL, D_EXPERT), F32),
                pltpu.VMEM((2, D_MODEL, D_EXPERT), F32),
                pltpu.VMEM((2, D_EXPERT, D_MODEL), F32),
                pltpu.VMEM((D_MODEL, 2 * D_EXPERT), BF16),
                pltpu.VMEM((D_EXPERT, D_MODEL), BF16),
                pltpu.SemaphoreType.DMA((2,)),
            ],
        ),
        out_shape=jax.ShapeDtypeStruct((n_rows, D_PACK), jnp.uint32),
        compiler_params=_params(("arbitrary",)),
        name="moe_experts",
    )(blk_e, n_valid, first, slot, next_e, n_used, xs, w_gate, w_up, w_down)


def _sc_gather_rows(table, idx, chunk):
    n_idx = idx.shape[0]
    d = table.shape[1]
    per_worker = n_idx // SC_WORKERS
    n_chunks = per_worker // chunk
    assert n_chunks % 2 == 0 and n_chunks * chunk * SC_WORKERS == n_idx
    mesh = plsc.VectorSubcoreMesh(core_axis_name="c", subcore_axis_name="s")

    @functools.partial(
        pl.kernel, mesh=mesh,
        out_type=jax.ShapeDtypeStruct((n_idx, d), table.dtype),
        scratch_types=[
            pltpu.VMEM((chunk,), jnp.int32), pltpu.VMEM((chunk,), jnp.int32),
            pltpu.VMEM((chunk, d), table.dtype), pltpu.VMEM((chunk, d), table.dtype),
            pltpu.SemaphoreType.DMA, pltpu.SemaphoreType.DMA, pltpu.SemaphoreType.DMA, pltpu.SemaphoreType.DMA,
        ],
    )
    def gather(table_hbm, idx_hbm, out_hbm, idx_v0, idx_v1, rows_v0, rows_v1, gsem0, gsem1, osem0, osem1):
        idx_v, rows_v, gsem, osem = (idx_v0, idx_v1), (rows_v0, rows_v1), (gsem0, gsem1), (osem0, osem1)
        wid = lax.axis_index("s") * SC_CORES + lax.axis_index("c")
        base = wid * per_worker

        def gather_copy(slot):
            return pltpu.make_async_copy(table_hbm.at[idx_v[slot]], rows_v[slot], gsem[slot])

        def out_copy(c, slot):
            return pltpu.make_async_copy(rows_v[slot], out_hbm.at[pl.ds(base + c * chunk, chunk)], osem[slot])

        def start_gather(c, slot):
            pltpu.sync_copy(idx_hbm.at[pl.ds(base + c * chunk, chunk)], idx_v[slot])
            gather_copy(slot).start()

        start_gather(0, 0)

        @pl.loop(0, n_chunks, step=2)
        def _(j):
            for b in range(2):
                c = j + b
                cur, other = b, 1 - b

                @pl.when(c >= 1)
                def _():
                    out_copy(c - 1, other).wait()

                @pl.when(c + 1 < n_chunks)
                def _():
                    start_gather(c + 1, other)

                gather_copy(cur).wait()
                out_copy(c, cur).start()

        out_copy(n_chunks - 1, 1).wait()

    return gather(table, idx)


def _combine_kernel(h1_ref, wts_ref, wsgu_ref, wsd_ref, g_ref, b_ref, yg_ref, out_ref):
    h1 = h1_ref[...]
    gu = _dot(h1.astype(BF16), wsgu_ref[...])
    hs = _silu(gu[:, :D_SHARED]) * gu[:, D_SHARED:]
    acc = DEEPNORM_ALPHA * h1 + _dot(hs.astype(BF16), wsd_ref[...])
    wts = wts_ref[...]
    acc_hi = acc[:, :D_PACK]
    acc_lo = acc[:, D_PACK:]
    for k in range(TOP_K):
        y_hi, y_lo = _unpack_rows(yg_ref[k])
        acc_hi = acc_hi + y_hi * wts[:, k:k + 1]
        acc_lo = acc_lo + y_lo * wts[:, k:k + 1]
    out_ref[...] = _layer_norm(jnp.concatenate([acc_hi, acc_lo], axis=1), g_ref[...], b_ref[...])


def _combine(h1, wts_t, ws_gu, ws_down, g, b, yg, tm):
    t = h1.shape[0]
    const = lambda shape: pl.BlockSpec(shape, lambda i: (0,) * len(shape))
    return pl.pallas_call(
        _combine_kernel,
        grid=(t // tm,),
        in_specs=[
            pl.BlockSpec((tm, D_MODEL), lambda i: (i, 0)),
            pl.BlockSpec((tm, TOP_K), lambda i: (i, 0)),
            const((D_MODEL, 2 * D_SHARED)), const((D_SHARED, D_MODEL)),
            const((1, D_MODEL)), const((1, D_MODEL)),
            pl.BlockSpec((TOP_K, tm, D_PACK), lambda i: (0, i, 0)),
        ],
        out_specs=pl.BlockSpec((tm, D_MODEL), lambda i: (i, 0)),
        out_shape=jax.ShapeDtypeStruct((t, D_MODEL), F32),
        compiler_params=_params(("arbitrary",)),
        name="moe_combine",
    )(h1, wts_t, ws_gu, ws_down, g, b, yg)


def _block_diag(w):
    nb, bi, bo = w.shape
    eye = jnp.eye(nb, dtype=w.dtype)
    return (eye[:, None, :, None] * w[:, :, None, :]).reshape(nb * bi, nb * bo)


def _pad_lanes(v, offset, width):
    return jnp.zeros((1, width), F32).at[0, offset:offset + v.shape[0]].set(v)


def _layer(h_in_x, l, p, tiles):
    bsz, seq, _ = h_in_x.shape
    t = bsz * seq
    row = lambda v: v.reshape(1, -1)

    w_in = p['w_in'][l]
    w_main = w_in[:, :N_MAIN].astype(BF16)
    w_small = jnp.zeros((D_MODEL, LANES), F32).at[:, :2 * GDN_HEADS].set(w_in[:, N_MAIN:])
    zeros = lambda n: jnp.zeros((CONV_WIDTH, n), F32)
    conv_w = jnp.concatenate([p['lru_conv_w'][l], zeros(LRU_WIDTH), p['gdn_conv_w'][l], zeros(GDN_V)], 1)
    conv_b = jnp.zeros((1, N_MAIN), F32).at[0, :LRU_WIDTH].set(p['lru_conv_b'][l])
    h0, proj, small, small_t = _inproj(h_in_x.reshape(t, D_MODEL), row(p['ln_g']), row(p['ln_b']),
                                       w_main, w_small, conv_w, conv_b, tiles['inproj'], seq)
    proj3 = proj.reshape(bsz, seq, N_MAIN)

    w_gates = jnp.concatenate([_block_diag(p['lru_w_rg'][l]), _block_diag(p['lru_w_ig'][l])], 1).astype(BF16)
    b_gates = jnp.concatenate([p['lru_b_rg'][l], p['lru_b_ig'][l]]).reshape(1, -1)
    rows = GDN_GROUP * GDN_CHUNK
    small3 = small.reshape(bsz, seq, LANES)
    smallt3 = small_t.reshape(8, bsz, seq // rows, rows).transpose(1, 2, 0, 3)
    a_log, dt_bias = p['gdn_a_log'][l], p['gdn_dt_bias'][l]
    alr = _pad_lanes(a_log, GDN_HEADS, LANES)
    dtr = _pad_lanes(dt_bias, GDN_HEADS, LANES)
    alc = _pad_lanes(a_log, GDN_HEADS, 8).reshape(8, 1)
    dtc = _pad_lanes(dt_bias, GDN_HEADS, 8).reshape(8, 1)
    y_lru, y_gdn = _mixers(proj3, small3, smallt3, w_gates, b_gates, row(p['lru_lambda'][l]), row(p['lru_out_g'][l]),
                           alr, dtr, alc, dtc, row(p['gdn_norm_w'][l]), tiles['gdn_nb'])

    w_out = p['w_out'][l].astype(BF16)
    h1, h1p, top_e, wts, rank, counts = _router(
        y_lru.reshape(t, LRU_WIDTH), y_gdn.reshape(t, GDN_V), h0, w_out[:LRU_WIDTH], w_out[LRU_WIDTH:],
        row(p['ln1_g'][l]), row(p['ln1_b'][l]), p['w_router'][l].T, p['router_bias'][l].reshape(-1, 1),
        tiles['router'])

    counts = counts[:, 0]
    padded = (counts + MOE_BLOCK - 1) // MOE_BLOCK * MOE_BLOCK
    pad_end = jnp.cumsum(padded)
    pad_start = pad_end - padded
    n_blocks = (t * TOP_K + N_EXPERTS * (MOE_BLOCK - 1)) // MOE_BLOCK
    n_rows = n_blocks * MOE_BLOCK
    n_used = (pad_end[-1] // MOE_BLOCK).astype(jnp.int32)
    blk_ids = jnp.minimum(jnp.arange(n_blocks, dtype=jnp.int32), n_used - 1)
    blk_e = jnp.minimum(jnp.sum(pad_end[None, :] <= (blk_ids * MOE_BLOCK)[:, None], axis=1),
                        N_EXPERTS - 1).astype(jnp.int32)

    dest = _dest(top_e, rank, pad_start.reshape(-1, 1), tiles['dest'])
    n_valid = jnp.clip(counts[blk_e] - (blk_ids * MOE_BLOCK - pad_start[blk_e]), 0, MOE_BLOCK).astype(jnp.int32)
    xs = _sc_scatter_rows(h1p, dest, n_rows, SC_CHUNK)
    active = jnp.arange(n_blocks, dtype=jnp.int32) < n_used
    first = (active & jnp.concatenate([jnp.ones((1,), bool), blk_e[1:] != blk_e[:-1]])).astype(jnp.int32)
    slot = ((jnp.cumsum(first) - 1) % 2).astype(jnp.int32)
    used = counts > 0
    later = jnp.where(used[None, :] & (jnp.arange(N_EXPERTS)[None, :] > jnp.arange(N_EXPERTS)[:, None]),
                      jnp.arange(N_EXPERTS, dtype=jnp.int32)[None, :], N_EXPERTS)
    next_used = jnp.min(later, axis=1)
    next_e = jnp.where(next_used < N_EXPERTS, next_used, -1)[blk_e].astype(jnp.int32)
    ys = _experts(blk_e, n_valid, first, slot, next_e, n_used.reshape(1), xs,
                  p['w_gate'][l], p['w_up'][l], p['w_down'][l])
    ws_gu = jnp.concatenate([p['ws_gate'][l], p['ws_up'][l]], 1).astype(BF16)
    yg = _sc_gather_rows(ys, dest.reshape(TOP_K * t), SC_CHUNK).reshape(TOP_K, t, D_PACK)
    out = _combine(h1, wts.T, ws_gu, p['ws_down'][l].astype(BF16),
                   row(p['ln2_g'][l]), row(p['ln2_b'][l]), yg, tiles['combine'])
    return out.reshape(bsz, seq, D_MODEL)


def _tiles(bsz, seq):
    t = bsz * seq
    return {
        'inproj': min(512, t),
        'gdn_nb': bsz,
        'router': min(512, t),
        'dest': min(512, t),
        'combine': min(256, t),
    }


def kernel(x, ln_in_g, ln_in_b, w_in, lru_conv_w, lru_conv_b, lru_w_rg, lru_b_rg, lru_w_ig, lru_b_ig,
           lru_lambda, lru_out_g, gdn_conv_w, gdn_a_log, gdn_dt_bias, gdn_norm_w, w_out, ln1_g, ln1_b,
           w_router, router_bias, w_gate, w_up, w_down, ws_gate, ws_up, ws_down, ln2_g, ln2_b):
    assert w_in.shape[0] == DEPTH == 1
    p = dict(ln_g=ln_in_g, ln_b=ln_in_b, w_in=w_in, lru_conv_w=lru_conv_w, lru_conv_b=lru_conv_b,
             lru_w_rg=lru_w_rg, lru_b_rg=lru_b_rg, lru_w_ig=lru_w_ig, lru_b_ig=lru_b_ig,
             lru_lambda=lru_lambda, lru_out_g=lru_out_g, gdn_conv_w=gdn_conv_w, gdn_a_log=gdn_a_log,
             gdn_dt_bias=gdn_dt_bias, gdn_norm_w=gdn_norm_w, w_out=w_out, ln1_g=ln1_g, ln1_b=ln1_b,
             w_router=w_router, router_bias=router_bias, w_gate=w_gate, w_up=w_up, w_down=w_down,
             ws_gate=ws_gate, ws_up=ws_up, ws_down=ws_down, ln2_g=ln2_g, ln2_b=ln2_b)
    bsz, seq, _ = x.shape
    return _layer(x, 0, p, _tiles(bsz, seq))
```

```python
import functools

import jax
import jax.numpy as jnp
from jax import lax
from jax.experimental import pallas as pl
from jax.experimental.pallas import tpu as pltpu
from jax.experimental.pallas import tpu_sc as plsc

F32 = jnp.float32
BF16 = jnp.bfloat16

D_MODEL = 1024
LRU_WIDTH = 512
LRU_BLOCKS = 8
LRU_C = 8.0
CONV_WIDTH = 4
GDN_HEADS = 4
GDN_DK = 128
GDN_DV = 128
GDN_CHUNK = 64
GDN_GROUP = 2
GDN_QK = GDN_HEADS * GDN_DK
GDN_V = GDN_HEADS * GDN_DV
N_MAIN = 2 * LRU_WIDTH + 2 * GDN_QK + 2 * GDN_V
N_EXPERTS = 256
TOP_K = 8
N_GROUPS = 8
GROUP_SIZE = N_EXPERTS // N_GROUPS
TOPK_GROUPS = 4
D_EXPERT = 256
D_SHARED = 256
ROUTED_SCALE = 2.5
MOE_BLOCK = 640
D_PACK = D_MODEL // 2
LN_EPS = 1e-5
NORM_EPS = 1e-6
DEPTH = 1
DEEPNORM_ALPHA = (2.0 * DEPTH) ** 0.25

SCAN_GROUP = 8
HALO = 8
CONV_GROUP = 512
CONV_GROUPS = (0, 2, 3, 4)
LANES = 128
VMEM_LIMIT = 56 * 1024 * 1024
ROUTER_SUB = 256
EXPERT_BANDS = 5
SC_CORES = 2
SC_WORKERS = 32
SC_CHUNK = 64

NN = (((1,), (0,)), ((), ()))
NT = (((1,), (1,)), ((), ()))
TN = (((0,), (0,)), ((), ()))


def _dot(a, b, dims=NN):
    return lax.dot_general(a, b, dims, preferred_element_type=F32)


def _split(a):
    hi = a.astype(BF16)
    lo = (a - hi.astype(F32)).astype(BF16)
    return hi, lo


def _dot3(a, b, dims=NN):
    ah, al = _split(a)
    bh, bl = _split(b)
    return _dot(ah, bh, dims) + (_dot(ah, bl, dims) + _dot(al, bh, dims))


def _layer_norm(x, g, b):
    mu = jnp.mean(x, -1, keepdims=True)
    xc = x - mu
    var = jnp.mean(xc * xc, -1, keepdims=True)
    return xc * lax.rsqrt(var + LN_EPS) * g + b


def _sigmoid(x):
    return 1.0 / (1.0 + jnp.exp(-x))


def _silu(x):
    return x * _sigmoid(x)


def _softplus(x):
    return jnp.maximum(x, 0.0) + jnp.log1p(jnp.exp(-jnp.abs(x)))


def _gelu_tanh(x):
    c = 0.7978845608028654
    return x * (0.5 * (1.0 + jnp.tanh(c * (x + 0.044715 * (x * x * x)))))


def _pack_rows(x):
    hi = lax.bitcast_convert_type(x[:, :D_PACK].astype(BF16).astype(F32), jnp.uint32)
    lo = lax.bitcast_convert_type(x[:, D_PACK:].astype(BF16).astype(F32), jnp.uint32)
    return (hi & jnp.uint32(0xFFFF0000)) | (lo >> 16)


def _unpack_rows(w):
    hi = lax.bitcast_convert_type(w & jnp.uint32(0xFFFF0000), F32)
    lo = lax.bitcast_convert_type(w << 16, F32)
    return hi, lo


def _params(sem, **kw):
    return pltpu.CompilerParams(dimension_semantics=sem, vmem_limit_bytes=VMEM_LIMIT, **kw)


def _inproj_kernel(x_ref, g_ref, b_ref, w_ref, ws_ref, cw_ref, cb_ref,
                   h_ref, proj_ref, small_ref, smallt_ref, hist, *, tiles_per_seq):
    i = pl.program_id(0)
    tm = x_ref.shape[0]
    h = _layer_norm(x_ref[...], g_ref[...], b_ref[...])
    h_ref[...] = h
    hb = h.astype(BF16)

    @pl.when(i % tiles_per_seq == 0)
    def _():
        hist[...] = jnp.zeros_like(hist)

    for g in range(N_MAIN // CONV_GROUP):
        cols = slice(g * CONV_GROUP, (g + 1) * CONV_GROUP)
        p = _dot(hb, w_ref[:, cols])
        if g in CONV_GROUPS:
            xcat = jnp.concatenate([hist[:, cols], p], axis=0)
            acc = cb_ref[:, cols]
            for j in range(CONV_WIDTH):
                off = HALO - (CONV_WIDTH - 1) + j
                acc = acc + xcat[off:off + tm, :] * cw_ref[j:j + 1, cols]
            hist[:, cols] = p[tm - HALO:, :]
            p = acc
        proj_ref[:, cols] = p
    small = _dot3(h, ws_ref[...])
    small_ref[...] = small
    smallt_ref[...] = small.T[:smallt_ref.shape[0], :]


def _inproj(x2d, g, b, w_main, w_small, conv_w, conv_b, tm, seq):
    t = x2d.shape[0]
    return pl.pallas_call(
        functools.partial(_inproj_kernel, tiles_per_seq=seq // tm),
        grid=(t // tm,),
        in_specs=[
            pl.BlockSpec((tm, D_MODEL), lambda i: (i, 0)),
            pl.BlockSpec((1, D_MODEL), lambda i: (0, 0)),
            pl.BlockSpec((1, D_MODEL), lambda i: (0, 0)),
            pl.BlockSpec((D_MODEL, N_MAIN), lambda i: (0, 0)),
            pl.BlockSpec((D_MODEL, LANES), lambda i: (0, 0)),
            pl.BlockSpec((CONV_WIDTH, N_MAIN), lambda i: (0, 0)),
            pl.BlockSpec((1, N_MAIN), lambda i: (0, 0)),
        ],
        out_specs=[
            pl.BlockSpec((tm, D_MODEL), lambda i: (i, 0)),
            pl.BlockSpec((tm, N_MAIN), lambda i: (i, 0)),
            pl.BlockSpec((tm, LANES), lambda i: (i, 0)),
            pl.BlockSpec((8, tm), lambda i: (0, i)),
        ],
        out_shape=[
            jax.ShapeDtypeStruct((t, D_MODEL), F32),
            jax.ShapeDtypeStruct((t, N_MAIN), F32),
            jax.ShapeDtypeStruct((t, LANES), F32),
            jax.ShapeDtypeStruct((8, t), F32),
        ],
        scratch_shapes=[pltpu.VMEM((HALO, N_MAIN), F32)],
        compiler_params=_params(("arbitrary",)),
        name="ln_inproj",
    )(x2d, g, b, w_main, w_small, conv_w, conv_b)


def _lru_tile(xc, gate, wg, bg, lam, og, carry):
    rows = xc.shape[0]
    gates = _dot(xc.astype(BF16), wg) + bg
    r = _sigmoid(gates[:, :LRU_WIDTH])
    i = _sigmoid(gates[:, LRU_WIDTH:])
    log_a = (-LRU_C) * r * _softplus(-lam)
    a = jnp.exp(log_a)
    one_minus_a2 = -jnp.tanh(log_a) * (a * a + 1.0)
    mult = jnp.where(one_minus_a2 > 0.0, one_minus_a2 * lax.rsqrt(one_minus_a2), 0.0)
    bv = mult * (i * xc)
    a = a.reshape(rows // SCAN_GROUP, SCAN_GROUP, LRU_WIDTH)
    bv = bv.reshape(rows // SCAN_GROUP, SCAN_GROUP, LRU_WIDTH)
    row_in_group = lax.broadcasted_iota(jnp.int32, a.shape, 1)
    d = 1
    while d < SCAN_GROUP:
        a_sh = jnp.where(row_in_group < d, 1.0, pltpu.roll(a, d, 1))
        b_sh = jnp.where(row_in_group < d, 0.0, pltpu.roll(bv, d, 1))
        bv = a * b_sh + bv
        a = a * a_sh
        d *= 2
    a = a.reshape(rows, LRU_WIDTH)
    bv = bv.reshape(rows, LRU_WIDTH)
    parts = []
    for g in range(rows // SCAN_GROUP):
        grp = slice(g * SCAN_GROUP, (g + 1) * SCAN_GROUP)
        hg = a[grp] * carry + bv[grp]
        carry = hg[SCAN_GROUP - 1:, :]
        parts.append(hg)
    h = jnp.concatenate(parts, axis=0)
    y = h * _gelu_tanh(gate)
    ms = jnp.mean(y * y, -1, keepdims=True)
    return y * lax.rsqrt(ms + NORM_EPS) * og, carry


def _bdot(a, b, dims=NN):
    return _dot(a.astype(BF16), b.astype(BF16), dims)


def _gdn_heads(args, norm_w):
    c = GDN_CHUNK
    r = GDN_GROUP * c
    ri = lax.broadcasted_iota(jnp.int32, (r, r), 0)
    ci = lax.broadcasted_iota(jnp.int32, (r, r), 1)
    same = (ri // c) == (ci // c)
    causal = same & (ri >= ci)
    strict = same & (ri > ci)
    upper = same & (ri <= ci)
    chunk_of_row = lax.broadcasted_iota(jnp.int32, (r, 1), 0) // c
    each = lambda f, *ls: [f(*xs) for xs in zip(*ls)]
    q, k, v, z, beta, g_col, g_row, st = [list(x) for x in zip(*args)]
    q = each(lambda x: x * lax.rsqrt(jnp.sum(x * x, -1, keepdims=True) + NORM_EPS) * (GDN_DK ** -0.5), q)
    k = each(lambda x: x * lax.rsqrt(jnp.sum(x * x, -1, keepdims=True) + NORM_EPS), k)
    gc_col = each(lambda g: jnp.sum(jnp.where(causal, g, 0.0), axis=1, keepdims=True), g_row)
    gc_row = each(lambda g: jnp.sum(jnp.where(upper, g, 0.0), axis=0, keepdims=True), g_col)
    decay = each(lambda gc, gr: jnp.exp(jnp.where(causal, gc - gr, -jnp.inf)), gc_col, gc_row)
    kb = each(lambda x, bt: x * bt, k, beta)
    vb = each(lambda x, bt: x * bt, v, beta)
    kk = each(lambda x, y: _bdot(x, y, NT), kb, k)
    a_mat = each(lambda m, d: jnp.where(strict, m * d, 0.0), kk, decay)
    e_col = each(jnp.exp, gc_col)
    rhs = each(lambda x, y, e: jnp.concatenate([x, y * e], axis=1), vb, kb, e_col)
    sol = each(lambda rr, a: rr - _bdot(a, rr), rhs, a_mat)
    p = a_mat
    for _ in range(5):
        p = each(lambda x: _bdot(x, x), p)
        sol = each(lambda x, y: y + _bdot(x, y), p, sol)
    qk = each(lambda x, y: _bdot(x, y, NT), q, k)
    qk = each(lambda m, d: jnp.where(causal, m * d, 0.0), qk, decay)
    q_dec = each(lambda x, e: x * e, q, e_col)
    g_last = [each(lambda gc: gc[(j + 1) * c - 1:(j + 1) * c, :], gc_col) for j in range(GDN_GROUP)]

    def last_of_own_chunk(*gl):
        out = gl[-1]
        for j in range(GDN_GROUP - 2, -1, -1):
            out = jnp.where(chunk_of_row == j, gl[j], out)
        return out

    g_end = each(last_of_own_chunk, *g_last)
    k_dec = each(lambda x, ge, gc: x * jnp.exp(ge - gc), k, g_end, gc_col)
    qs_parts, v_parts = [], []
    for j in range(GDN_GROUP):
        rows = slice(j * c, (j + 1) * c)
        ws = each(lambda x, s: _bdot(x[rows, GDN_DV:], s), sol, st)
        qs_parts.append(each(lambda x, s: _bdot(x[rows], s), q_dec, st))
        v_new = each(lambda x, w: x[rows, :GDN_DV] - w, sol, ws)
        v_parts.append(v_new)
        kv = each(lambda x, vn: _bdot(x[rows], vn, TN), k_dec, v_new)
        st = each(lambda s, gl, d: s * jnp.exp(gl) + d, st, g_last[j], kv)
    qs = each(lambda *parts: jnp.concatenate(parts, axis=0), *qs_parts)
    v_all = each(lambda *parts: jnp.concatenate(parts, axis=0), *v_parts)
    o = each(lambda a, m, vn: a + _bdot(m, vn), qs, qk, v_all)
    o = each(lambda x: x * lax.rsqrt(jnp.mean(x * x, -1, keepdims=True) + NORM_EPS) * norm_w, o)
    o = each(lambda x, zz: x * _silu(zz), o, z)
    return list(zip(o, st))


def _mixer_kernel(xc_ref, gate_ref, q_ref, k_ref, v_ref, z_ref, sm_ref, smt_ref,
                  wg_ref, bg_ref, lam_ref, og_ref, alr_ref, dtr_ref, alc_ref, dtc_ref, nw_ref,
                  ylru_ref, y_ref, hcarry, state):
    n = pl.program_id(1)
    c = GDN_GROUP * GDN_CHUNK
    nb = q_ref.shape[0]
    first = n == 0

    @pl.when(first)
    def _():
        state[...] = jnp.zeros_like(state)
        hcarry[...] = jnp.zeros_like(hcarry)

    lru_out = [_lru_tile(xc_ref[b], gate_ref[b], wg_ref[...], bg_ref[...], lam_ref[...], og_ref[...], hcarry[b])
               for b in range(nb)]
    norm_w = nw_ref[...]

    args = []
    for b in range(nb):
        q_all = _silu(q_ref[b])
        k_all = _silu(k_ref[b])
        v_all = _silu(v_ref[b])
        z_all = z_ref[b]
        sm = sm_ref[b]
        beta_all = _sigmoid(sm)
        g_cols = -jnp.exp(alr_ref[...]) * _softplus(sm + dtr_ref[...])
        g_rows = -jnp.exp(alc_ref[...]) * _softplus(smt_ref[b] + dtc_ref[...])
        for hd in range(GDN_HEADS):
            sl = slice(hd * GDN_DK, (hd + 1) * GDN_DK)
            args.append((q_all[:, sl], k_all[:, sl], v_all[:, sl], z_all[:, sl],
                         beta_all[:, hd:hd + 1],
                         g_cols[:, GDN_HEADS + hd:GDN_HEADS + hd + 1],
                         g_rows[GDN_HEADS + hd:GDN_HEADS + hd + 1, :],
                         state[b, hd]))
    outs = _gdn_heads(args, norm_w)
    for b in range(nb):
        for hd in range(GDN_HEADS):
            o, st_new = outs[b * GDN_HEADS + hd]
            state[b, hd] = st_new
            y_ref[b, :, hd * GDN_DK:(hd + 1) * GDN_DK] = o
    for b, (y_lru, carry) in enumerate(lru_out):
        ylru_ref[b] = y_lru
        hcarry[b] = carry


def _mixers(proj3, small3, smallt3, w_gates, b_gates, lam, out_g, alr, dtr, alc, dtc, norm_w, nb):
    bsz, seq, _ = proj3.shape
    c = GDN_GROUP * GDN_CHUNK
    nch = seq // c
    col = lambda j: pl.BlockSpec((nb, c, GDN_QK), lambda b, n: (b, n, j))
    const = lambda shape: pl.BlockSpec(shape, lambda b, n: (0,) * len(shape))
    return pl.pallas_call(
        _mixer_kernel,
        grid=(bsz // nb, nch),
        in_specs=[
            col(0), col(1), col(2), col(3), col(4), col(5),
            pl.BlockSpec((nb, c, LANES), lambda b, n: (b, n, 0)),
            pl.BlockSpec((nb, None, 8, c), lambda b, n: (b, n, 0, 0)),
            const((LRU_WIDTH, 2 * LRU_WIDTH)), const((1, 2 * LRU_WIDTH)), const((1, LRU_WIDTH)), const((1, LRU_WIDTH)),
            const((1, LANES)), const((1, LANES)), const((8, 1)), const((8, 1)),
            const((1, GDN_DV)),
        ],
        out_specs=[pl.BlockSpec((nb, c, LRU_WIDTH), lambda b, n: (b, n, 0)),
                   pl.BlockSpec((nb, c, GDN_V), lambda b, n: (b, n, 0))],
        out_shape=[jax.ShapeDtypeStruct((bsz, seq, LRU_WIDTH), F32),
                   jax.ShapeDtypeStruct((bsz, seq, GDN_V), F32)],
        scratch_shapes=[
            pltpu.VMEM((nb, 1, LRU_WIDTH), F32),
            pltpu.VMEM((nb, GDN_HEADS, GDN_DK, GDN_DV), F32),
        ],
        compiler_params=_params(("arbitrary", "arbitrary")),
        name="mixers",
    )(proj3, proj3, proj3, proj3, proj3, proj3, small3, smallt3, w_gates, b_gates, lam, out_g,
      alr, dtr, alc, dtc, norm_w)


def _pick_experts(logits, rbias):
    n = logits.shape[1]
    scores = _sigmoid(logits)
    choice = scores + rbias
    neg = -jnp.inf
    gs_rows = []
    sub = lax.broadcasted_iota(jnp.int32, (GROUP_SIZE, n), 0).astype(F32)
    for g in range(N_GROUPS):
        cg = choice[g * GROUP_SIZE:(g + 1) * GROUP_SIZE, :]
        m1 = jnp.max(cg, axis=0, keepdims=True)
        i1 = jnp.min(jnp.where(cg == m1, sub, float(GROUP_SIZE)), axis=0, keepdims=True)
        m2 = jnp.max(jnp.where(sub == i1, neg, cg), axis=0, keepdims=True)
        gs_rows.append(m1 + m2)
    gs = jnp.concatenate(gs_rows, axis=0)
    gi = lax.broadcasted_iota(jnp.int32, (N_GROUPS, n), 0).astype(F32)
    gsel = jnp.zeros((N_GROUPS, n), jnp.bool_)
    for _ in range(TOPK_GROUPS):
        m = jnp.max(gs, axis=0, keepdims=True)
        idx = jnp.min(jnp.where(gs == m, gi, float(N_GROUPS)), axis=0, keepdims=True)
        hit = gi == idx
        gsel = jnp.logical_or(gsel, hit)
        gs = jnp.where(hit, neg, gs)
    masked = jnp.concatenate(
        [jnp.where(gsel[g:g + 1, :], choice[g * GROUP_SIZE:(g + 1) * GROUP_SIZE, :], neg)
         for g in range(N_GROUPS)], axis=0)
    ei = lax.broadcasted_iota(jnp.int32, (N_EXPERTS, n), 0).astype(F32)
    hits, e_rows, w_rows = [], [], []
    multi = jnp.zeros((N_EXPERTS, n), F32)
    for _ in range(TOP_K):
        m = jnp.max(masked, axis=0, keepdims=True)
        idx = jnp.min(jnp.where(masked == m, ei, float(N_EXPERTS)), axis=0, keepdims=True)
        hit = ei == idx
        hits.append(hit)
        e_rows.append(idx)
        w_rows.append(jnp.sum(jnp.where(hit, scores, 0.0), axis=0, keepdims=True))
        multi = multi + hit.astype(F32)
        masked = jnp.where(hit, neg, masked)
    wts = jnp.concatenate(w_rows, axis=0)
    wts = wts / (jnp.sum(wts, axis=0, keepdims=True) + 1e-20) * ROUTED_SCALE
    return jnp.concatenate(e_rows, axis=0), wts, hits, multi


def _router_kernel(yl_ref, yg_ref, h0_ref, wo1_ref, wo2_ref, g_ref, b_ref, wrt_ref, rb_ref,
                   h1_ref, h1p_ref, e_ref, w_ref, rank_ref, cnt_ref, carry):
    i = pl.program_id(0)
    tm = h0_ref.shape[0]
    n = min(ROUTER_SUB, tm)
    subs = [slice(j * n, (j + 1) * n) for j in range(tm // n)]

    @pl.when(i == 0)
    def _():
        carry[...] = jnp.zeros_like(carry)

    mixes = [_dot(yl_ref[r, :].astype(BF16), wo1_ref[...]) + _dot(yg_ref[r, :].astype(BF16), wo2_ref[...])
             for r in subs]
    h1s = [_layer_norm(DEEPNORM_ALPHA * h0_ref[r, :] + mix, g_ref[...], b_ref[...])
           for r, mix in zip(subs, mixes)]
    for r, h1 in zip(subs, h1s):
        h1_ref[r, :] = h1
        h1p_ref[r, :] = _pack_rows(h1)
    logits = [_dot3(wrt_ref[...], h1, NT) for h1 in h1s]
    picks = [_pick_experts(lg, rb_ref[...]) for lg in logits]
    ti = lax.broadcasted_iota(jnp.int32, (n, n), 0)
    tj = lax.broadcasted_iota(jnp.int32, (n, n), 1)
    before = (ti < tj).astype(BF16)
    cums = [_dot(multi.astype(BF16), before) for _, _, _, multi in picks]
    base = carry[...]
    for r, (e_rows, wts, hits, multi), cum in zip(subs, picks, cums):
        cum = cum + base
        r_rows = [jnp.sum(jnp.where(hit, cum, 0.0), axis=0, keepdims=True) for hit in hits]
        base = base + jnp.sum(multi, axis=1, keepdims=True)
        e_ref[:, r] = e_rows.astype(jnp.int32)
        w_ref[:, r] = wts
        rank_ref[:, r] = jnp.concatenate(r_rows, axis=0).astype(jnp.int32)
    carry[...] = base
    cnt_ref[...] = base.astype(jnp.int32)


def _router(y_lru, y_gdn, h0, wo1, wo2, g, b, w_router_t, rbias, tm):
    t = h0.shape[0]
    const = lambda shape: pl.BlockSpec(shape, lambda i: (0,) * len(shape))
    return pl.pallas_call(
        _router_kernel,
        grid=(t // tm,),
        in_specs=[
            pl.BlockSpec((tm, LRU_WIDTH), lambda i: (i, 0)),
            pl.BlockSpec((tm, GDN_V), lambda i: (i, 0)),
            pl.BlockSpec((tm, D_MODEL), lambda i: (i, 0)),
            const((LRU_WIDTH, D_MODEL)), const((GDN_V, D_MODEL)),
            const((1, D_MODEL)), const((1, D_MODEL)),
            const((N_EXPERTS, D_MODEL)), const((N_EXPERTS, 1)),
        ],
        out_specs=[
            pl.BlockSpec((tm, D_MODEL), lambda i: (i, 0)),
            pl.BlockSpec((tm, D_PACK), lambda i: (i, 0)),
            pl.BlockSpec((TOP_K, tm), lambda i: (0, i)),
            pl.BlockSpec((TOP_K, tm), lambda i: (0, i)),
            pl.BlockSpec((TOP_K, tm), lambda i: (0, i)),
            const((N_EXPERTS, 1)),
        ],
        out_shape=[
            jax.ShapeDtypeStruct((t, D_MODEL), F32),
            jax.ShapeDtypeStruct((t, D_PACK), jnp.uint32),
            jax.ShapeDtypeStruct((TOP_K, t), jnp.int32),
            jax.ShapeDtypeStruct((TOP_K, t), F32),
            jax.ShapeDtypeStruct((TOP_K, t), jnp.int32),
            jax.ShapeDtypeStruct((N_EXPERTS, 1), jnp.int32),
        ],
        scratch_shapes=[pltpu.VMEM((N_EXPERTS, 1), F32)],
        compiler_params=_params(("arbitrary",)),
        name="outproj_router",
    )(y_lru, y_gdn, h0, wo1, wo2, g, b, w_router_t, rbias)


def _dest_kernel(e_ref, r_ref, ps_ref, d_ref):
    tm = e_ref.shape[1]
    ei = lax.broadcasted_iota(jnp.int32, (N_EXPERTS, tm), 0)
    rows = []
    for k in range(TOP_K):
        hit = ei == e_ref[k:k + 1, :]
        rows.append(jnp.sum(jnp.where(hit, ps_ref[...], 0), axis=0, keepdims=True))
    d_ref[...] = jnp.concatenate(rows, axis=0) + r_ref[...]


def _dest(top_e, rank, pad_start, tm):
    t = top_e.shape[1]
    blk = pl.BlockSpec((TOP_K, tm), lambda i: (0, i))
    return pl.pallas_call(
        _dest_kernel,
        grid=(t // tm,),
        in_specs=[blk, blk, pl.BlockSpec((N_EXPERTS, 1), lambda i: (0, 0))],
        out_specs=blk,
        out_shape=jax.ShapeDtypeStruct((TOP_K, t), jnp.int32),
        compiler_params=_params(("arbitrary",)),
        name="moe_dest",
    )(top_e, rank, pad_start)


def _sc_scatter_rows(rows, idx, n_out, chunk):
    n_copies, t = idx.shape
    d = rows.shape[1]
    per_worker = t // SC_WORKERS
    n_chunks = per_worker // chunk
    mesh = plsc.VectorSubcoreMesh(core_axis_name="c", subcore_axis_name="s")
    idx_flat = idx.reshape(n_copies * t)

    @functools.partial(
        pl.kernel, mesh=mesh,
        out_type=jax.ShapeDtypeStruct((n_out, d), rows.dtype),
        scratch_types=[pltpu.VMEM((chunk,), jnp.int32) for _ in range(n_copies)] + [
            pltpu.VMEM((chunk, d), rows.dtype),
            pltpu.SemaphoreType.DMA,
        ],
    )
    def scatter(rows_hbm, idx_hbm, out_hbm, *scratch):
        idx_v = scratch[:n_copies]
        rows_v, sem = scratch[n_copies:]
        wid = lax.axis_index("s") * SC_CORES + lax.axis_index("c")
        base = wid * per_worker

        @pl.loop(0, n_chunks)
        def _(j):
            off = base + j * chunk
            for k in range(n_copies):
                pltpu.sync_copy(idx_hbm.at[pl.ds(k * t + off, chunk)], idx_v[k])
            pltpu.sync_copy(rows_hbm.at[pl.ds(off, chunk)], rows_v)
            copies = [pltpu.async_copy(rows_v, out_hbm.at[idx_v[k]], sem) for k in range(n_copies)]
            for cp in copies:
                cp.wait()

    return scatter(rows, idx_flat)


def _expert_kernel(be_ref, nv_ref, first_ref, slot_ref, next_ref, nu_ref,
                   xs_ref, wg_hbm, wu_hbm, wd_hbm, ys_ref, wg_f, wu_f, wd_f, wgu_b, wd_b, sem):
    i = pl.program_id(0)

    def fetch(e, slot):
        return (pltpu.make_async_copy(wg_hbm.at[e], wg_f.at[slot], sem.at[slot]),
                pltpu.make_async_copy(wu_hbm.at[e], wu_f.at[slot], sem.at[slot]),
                pltpu.make_async_copy(wd_hbm.at[e], wd_f.at[slot], sem.at[slot]))

    @pl.when(i < nu_ref[0])
    def _():
        e = be_ref[i]
        slot = slot_ref[i]

        @pl.when(first_ref[i] == 1)
        def _():
            @pl.when(i == 0)
            def _():
                for cp in fetch(e, slot):
                    cp.start()

            for cp in fetch(e, slot):
                cp.wait()

            @pl.when(next_ref[i] >= 0)
            def _():
                for cp in fetch(next_ref[i], 1 - slot):
                    cp.start()

            wgu_b[:, :D_EXPERT] = wg_f[slot].astype(BF16)
            wgu_b[:, D_EXPERT:] = wu_f[slot].astype(BF16)
            wd_b[...] = wd_f[slot].astype(BF16)

        n = xs_ref.shape[0] // EXPERT_BANDS
        bands = [slice(j * n, (j + 1) * n) for j in range(EXPERT_BANDS)]
        row = lax.broadcasted_iota(jnp.int32, (n, D_PACK), 0)
        xs = [_unpack_rows(jnp.where(row + j * n < nv_ref[i], xs_ref[r, :], jnp.uint32(0)))
              for j, r in enumerate(bands)]
        gus = [_dot(x_hi.astype(BF16), wgu_b[:D_PACK, :]) + _dot(x_lo.astype(BF16), wgu_b[D_PACK:, :])
               for x_hi, x_lo in xs]
        hs = [_silu(gu[:, :D_EXPERT]) * gu[:, D_EXPERT:] for gu in gus]
        ys = [_dot(h.astype(BF16), wd_b[...]) for h in hs]
        for r, y in zip(bands, ys):
            ys_ref[r, :] = _pack_rows(y)


def _experts(blk_e, n_valid, first, slot, next_e, n_used, xs, w_gate, w_up, w_down):
    n_rows = xs.shape[0]
    n_blocks = n_rows // MOE_BLOCK
    blk = lambda i, be, nv, fi, sl, nx, nu: (jnp.minimum(i, nu[0] - 1), 0)
    return pl.pallas_call(
        _expert_kernel,
        grid_spec=pltpu.PrefetchScalarGridSpec(
            num_scalar_prefetch=6,
            grid=(n_blocks,),
            in_specs=[
                pl.BlockSpec((MOE_BLOCK, D_PACK), blk),
                pl.BlockSpec(memory_space=pl.ANY),
                pl.BlockSpec(memory_space=pl.ANY),
                pl.BlockSpec(memory_space=pl.ANY),
            ],
            out_specs=pl.BlockSpec((MOE_BLOCK, D_PACK), blk),
            scratch_shapes=[
                pltpu.VMEM((2, D_MODEL, D_EXPERT), F32),
                pltpu.VMEM((2, D_MODEL, D_EXPERT), F32),
                pltpu.VMEM((2, D_EXPERT, D_MODEL), F32),
                pltpu.VMEM((D_MODEL, 2 * D_EXPERT), BF16),
                pltpu.VMEM((D_EXPERT, D_MODEL), BF16),
                pltpu.SemaphoreType.DMA((2,)),
            ],
        ),
        out_shape=jax.ShapeDtypeStruct((n_rows, D_PACK), jnp.uint32),
        compiler_params=_params(("arbitrary",)),
        name="moe_experts",
    )(blk_e, n_valid, first, slot, next_e, n_used, xs, w_gate, w_up, w_down)


def _sc_gather_rows(table, idx, chunk):
    n_idx = idx.shape[0]
    d = table.shape[1]
    per_worker = n_idx // SC_WORKERS
    n_chunks = per_worker // chunk
    assert n_chunks % 2 == 0 and n_chunks * chunk * SC_WORKERS == n_idx
    mesh = plsc.VectorSubcoreMesh(core_axis_name="c", subcore_axis_name="s")

    @functools.partial(
        pl.kernel, mesh=mesh,
        out_type=jax.ShapeDtypeStruct((n_idx, d), table.dtype),
        scratch_types=[
            pltpu.VMEM((chunk,), jnp.int32), pltpu.VMEM((chunk,), jnp.int32),
            pltpu.VMEM((chunk, d), table.dtype), pltpu.VMEM((chunk, d), table.dtype),
            pltpu.SemaphoreType.DMA, pltpu.SemaphoreType.DMA, pltpu.SemaphoreType.DMA, pltpu.SemaphoreType.DMA,
        ],
    )
    def gather(table_hbm, idx_hbm, out_hbm, idx_v0, idx_v1, rows_v0, rows_v1, gsem0, gsem1, osem0, osem1):
        idx_v, rows_v, gsem, osem = (idx_v0, idx_v1), (rows_v0, rows_v1), (gsem0, gsem1), (osem0, osem1)
        wid = lax.axis_index("s") * SC_CORES + lax.axis_index("c")
        base = wid * per_worker

        def gather_copy(slot):
            return pltpu.make_async_copy(table_hbm.at[idx_v[slot]], rows_v[slot], gsem[slot])

        def out_copy(c, slot):
            return pltpu.make_async_copy(rows_v[slot], out_hbm.at[pl.ds(base + c * chunk, chunk)], osem[slot])

        def start_gather(c, slot):
            pltpu.sync_copy(idx_hbm.at[pl.ds(base + c * chunk, chunk)], idx_v[slot])
            gather_copy(slot).start()

        start_gather(0, 0)

        @pl.loop(0, n_chunks, step=2)
        def _(j):
            for b in range(2):
                c = j + b
                cur, other = b, 1 - b

                @pl.when(c >= 1)
                def _():
                    out_copy(c - 1, other).wait()

                @pl.when(c + 1 < n_chunks)
                def _():
                    start_gather(c + 1, other)

                gather_copy(cur).wait()
                out_copy(c, cur).start()

        out_copy(n_chunks - 1, 1).wait()

    return gather(table, idx)


def _combine_kernel(h1_ref, wts_ref, wsgu_ref, wsd_ref, g_ref, b_ref, yg_ref, out_ref):
    h1 = h1_ref[...]
    gu = _dot(h1.astype(BF16), wsgu_ref[...])
    hs = _silu(gu[:, :D_SHARED]) * gu[:, D_SHARED:]
    acc = DEEPNORM_ALPHA * h1 + _dot(hs.astype(BF16), wsd_ref[...])
    wts = wts_ref[...].T
    acc_hi = acc[:, :D_PACK]
    acc_lo = acc[:, D_PACK:]
    for k in range(TOP_K):
        y_hi, y_lo = _unpack_rows(yg_ref[k])
        acc_hi = acc_hi + y_hi * wts[:, k:k + 1]
        acc_lo = acc_lo + y_lo * wts[:, k:k + 1]
    out_ref[...] = _layer_norm(jnp.concatenate([acc_hi, acc_lo], axis=1), g_ref[...], b_ref[...])


def _combine(h1, wts, ws_gu, ws_down, g, b, yg, tm):
    t = h1.shape[0]
    const = lambda shape: pl.BlockSpec(shape, lambda i: (0,) * len(shape))
    return pl.pallas_call(
        _combine_kernel,
        grid=(t // tm,),
        in_specs=[
            pl.BlockSpec((tm, D_MODEL), lambda i: (i, 0)),
            pl.BlockSpec((TOP_K, tm), lambda i: (0, i)),
            const((D_MODEL, 2 * D_SHARED)), const((D_SHARED, D_MODEL)),
            const((1, D_MODEL)), const((1, D_MODEL)),
            pl.BlockSpec((TOP_K, tm, D_PACK), lambda i: (0, i, 0)),
        ],
        out_specs=pl.BlockSpec((tm, D_MODEL), lambda i: (i, 0)),
        out_shape=jax.ShapeDtypeStruct((t, D_MODEL), F32),
        compiler_params=_params(("arbitrary",)),
        name="moe_combine",
    )(h1, wts, ws_gu, ws_down, g, b, yg)


def _block_diag(w):
    nb, bi, bo = w.shape
    eye = jnp.eye(nb, dtype=w.dtype)
    return (eye[:, None, :, None] * w[:, :, None, :]).reshape(nb * bi, nb * bo)


def _pad_lanes(v, offset, width):
    return jnp.zeros((1, width), F32).at[0, offset:offset + v.shape[0]].set(v)


def _layer(h_in_x, l, p, tiles):
    bsz, seq, _ = h_in_x.shape
    t = bsz * seq
    row = lambda v: v.reshape(1, -1)

    w_in = p['w_in'][l]
    w_main = w_in.astype(BF16)
    w_small = jnp.zeros((D_MODEL, LANES), F32).at[:, :2 * GDN_HEADS].set(w_in[:, N_MAIN:])
    zeros = lambda n: jnp.zeros((CONV_WIDTH, n), F32)
    conv_w = jnp.concatenate([p['lru_conv_w'][l], zeros(LRU_WIDTH), p['gdn_conv_w'][l], zeros(GDN_V)], 1)
    conv_b = jnp.zeros((1, N_MAIN), F32).at[0, :LRU_WIDTH].set(p['lru_conv_b'][l])
    h0, proj, small, small_t = _inproj(h_in_x.reshape(t, D_MODEL), row(p['ln_g']), row(p['ln_b']),
                                       w_main, w_small, conv_w, conv_b, tiles['inproj'], seq)
    proj3 = proj.reshape(bsz, seq, N_MAIN)

    w_gates = jnp.concatenate([_block_diag(p['lru_w_rg'][l]), _block_diag(p['lru_w_ig'][l])], 1).astype(BF16)
    b_gates = jnp.concatenate([p['lru_b_rg'][l], p['lru_b_ig'][l]]).reshape(1, -1)
    rows = GDN_GROUP * GDN_CHUNK
    small3 = small.reshape(bsz, seq, LANES)
    smallt3 = small_t.reshape(8, bsz, seq // rows, rows).transpose(1, 2, 0, 3)
    a_log, dt_bias = p['gdn_a_log'][l], p['gdn_dt_bias'][l]
    alr = _pad_lanes(a_log, GDN_HEADS, LANES)
    dtr = _pad_lanes(dt_bias, GDN_HEADS, LANES)
    alc = _pad_lanes(a_log, GDN_HEADS, 8).reshape(8, 1)
    dtc = _pad_lanes(dt_bias, GDN_HEADS, 8).reshape(8, 1)
    y_lru, y_gdn = _mixers(proj3, small3, smallt3, w_gates, b_gates, row(p['lru_lambda'][l]), row(p['lru_out_g'][l]),
                           alr, dtr, alc, dtc, row(p['gdn_norm_w'][l]), tiles['gdn_nb'])

    w_out = p['w_out'][l].astype(BF16)
    h1, h1p, top_e, wts, rank, counts = _router(
        y_lru.reshape(t, LRU_WIDTH), y_gdn.reshape(t, GDN_V), h0, w_out[:LRU_WIDTH], w_out[LRU_WIDTH:],
        row(p['ln1_g'][l]), row(p['ln1_b'][l]), p['w_router'][l].T, p['router_bias'][l].reshape(-1, 1),
        tiles['router'])

    counts = counts[:, 0]
    padded = (counts + MOE_BLOCK - 1) // MOE_BLOCK * MOE_BLOCK
    pad_end = jnp.cumsum(padded)
    pad_start = pad_end - padded
    n_blocks = (t * TOP_K + N_EXPERTS * (MOE_BLOCK - 1)) // MOE_BLOCK
    n_rows = n_blocks * MOE_BLOCK
    n_used = (pad_end[-1] // MOE_BLOCK).astype(jnp.int32)
    blk_ids = jnp.minimum(jnp.arange(n_blocks, dtype=jnp.int32), n_used - 1)
    blk_e = jnp.minimum(jnp.sum(pad_end[None, :] <= (blk_ids * MOE_BLOCK)[:, None], axis=1),
                        N_EXPERTS - 1).astype(jnp.int32)

    dest = _dest(top_e, rank, pad_start.reshape(-1, 1), tiles['dest'])
    n_valid = jnp.clip(counts[blk_e] - (blk_ids * MOE_BLOCK - pad_start[blk_e]), 0, MOE_BLOCK).astype(jnp.int32)
    xs = _sc_scatter_rows(h1p, dest, n_rows, SC_CHUNK)
    active = jnp.arange(n_blocks, dtype=jnp.int32) < n_used
    first = (active & jnp.concatenate([jnp.ones((1,), bool), blk_e[1:] != blk_e[:-1]])).astype(jnp.int32)
    slot = ((jnp.cumsum(first) - 1) % 2).astype(jnp.int32)
    used = counts > 0
    later = jnp.where(used[None, :] & (jnp.arange(N_EXPERTS)[None, :] > jnp.arange(N_EXPERTS)[:, None]),
                      jnp.arange(N_EXPERTS, dtype=jnp.int32)[None, :], N_EXPERTS)
    next_used = jnp.min(later, axis=1)
    next_e = jnp.where(next_used < N_EXPERTS, next_used, -1)[blk_e].astype(jnp.int32)
    ys = _experts(blk_e, n_valid, first, slot, next_e, n_used.reshape(1), xs,
                  p['w_gate'][l], p['w_up'][l], p['w_down'][l])
    ws_gu = jnp.concatenate([p['ws_gate'][l], p['ws_up'][l]], 1).astype(BF16)
    yg = _sc_gather_rows(ys, dest.reshape(TOP_K * t), SC_CHUNK).reshape(TOP_K, t, D_PACK)
    out = _combine(h1, wts, ws_gu, p['ws_down'][l].astype(BF16),
                   row(p['ln2_g'][l]), row(p['ln2_b'][l]), yg, tiles['combine'])
    return out.reshape(bsz, seq, D_MODEL)


def _tiles(bsz, seq):
    t = bsz * seq
    return {
        'inproj': min(512, t),
        'gdn_nb': bsz,
        'router': min(512, t),
        'dest': min(512, t),
        'combine': min(256, t),
    }


def kernel(x, ln_in_g, ln_in_b, w_in, lru_conv_w, lru_conv_b, lru_w_rg, lru_b_rg, lru_w_ig, lru_b_ig,
           lru_lambda, lru_out_g, gdn_conv_w, gdn_a_log, gdn_dt_bias, gdn_norm_w, w_out, ln1_g, ln1_b,
           w_router, router_bias, w_gate, w_up, w_down, ws_gate, ws_up, ws_down, ln2_g, ln2_b):
    assert w_in.shape[0] == DEPTH == 1
    p = dict(ln_g=ln_in_g, ln_b=ln_in_b, w_in=w_in, lru_conv_w=lru_conv_w, lru_conv_b=lru_conv_b,
             lru_w_rg=lru_w_rg, lru_b_rg=lru_b_rg, lru_w_ig=lru_w_ig, lru_b_ig=lru_b_ig,
             lru_lambda=lru_lambda, lru_out_g=lru_out_g, gdn_conv_w=gdn_conv_w, gdn_a_log=gdn_a_log,
             gdn_dt_bias=gdn_dt_bias, gdn_norm_w=gdn_norm_w, w_out=w_out, ln1_g=ln1_g, ln1_b=ln1_b,
             w_router=w_router, router_bias=router_bias, w_gate=w_gate, w_up=w_up, w_down=w_down,
             ws_gate=ws_gate, ws_up=ws_up, ws_down=ws_down, ln2_g=ln2_g, ln2_b=ln2_b)
    bsz, seq, _ = x.shape
    return _layer(x, 0, p, _tiles(bsz, seq))
```

```python
import functools

import jax
import jax.numpy as jnp
from jax import lax
from jax.experimental import pallas as pl
from jax.experimental.pallas import tpu as pltpu
from jax.experimental.pallas import tpu_sc as plsc

F32 = jnp.float32
BF16 = jnp.bfloat16

D_MODEL = 1024
LRU_WIDTH = 512
LRU_BLOCKS = 8
LRU_C = 8.0
CONV_WIDTH = 4
GDN_HEADS = 4
GDN_DK = 128
GDN_DV = 128
GDN_CHUNK = 64
GDN_GROUP = 2
GDN_QK = GDN_HEADS * GDN_DK
GDN_V = GDN_HEADS * GDN_DV
N_MAIN = 2 * LRU_WIDTH + 2 * GDN_QK + 2 * GDN_V
N_EXPERTS = 256
TOP_K = 8
N_GROUPS = 8
GROUP_SIZE = N_EXPERTS // N_GROUPS
TOPK_GROUPS = 4
D_EXPERT = 256
D_SHARED = 256
ROUTED_SCALE = 2.5
MOE_BLOCK = 640
D_PACK = D_MODEL // 2
LN_EPS = 1e-5
NORM_EPS = 1e-6
DEPTH = 1
DEEPNORM_ALPHA = (2.0 * DEPTH) ** 0.25

SCAN_GROUP = 8
HALO = 8
CONV_GROUP = 512
CONV_GROUPS = (0, 2, 3, 4)
LANES = 128
VMEM_LIMIT = 56 * 1024 * 1024
ROUTER_SUB = 256
EXPERT_BANDS = 5
SC_CORES = 2
SC_WORKERS = 32
SC_CHUNK = 64

NN = (((1,), (0,)), ((), ()))
NT = (((1,), (1,)), ((), ()))
TN = (((0,), (0,)), ((), ()))


def _dot(a, b, dims=NN):
    return lax.dot_general(a, b, dims, preferred_element_type=F32)


def _split(a):
    hi = a.astype(BF16)
    lo = (a - hi.astype(F32)).astype(BF16)
    return hi, lo


def _dot3(a, b, dims=NN):
    ah, al = _split(a)
    bh, bl = _split(b)
    return _dot(ah, bh, dims) + (_dot(ah, bl, dims) + _dot(al, bh, dims))


def _layer_norm(x, g, b):
    mu = jnp.mean(x, -1, keepdims=True)
    xc = x - mu
    var = jnp.mean(xc * xc, -1, keepdims=True)
    return xc * lax.rsqrt(var + LN_EPS) * g + b


def _sigmoid(x):
    return 1.0 / (1.0 + jnp.exp(-x))


def _silu(x):
    return x * _sigmoid(x)


def _softplus(x):
    return jnp.maximum(x, 0.0) + jnp.log1p(jnp.exp(-jnp.abs(x)))


def _gelu_tanh(x):
    c = 0.7978845608028654
    return x * (0.5 * (1.0 + jnp.tanh(c * (x + 0.044715 * (x * x * x)))))


def _pack_rows(x):
    hi = lax.bitcast_convert_type(x[:, :D_PACK].astype(BF16).astype(F32), jnp.uint32)
    lo = lax.bitcast_convert_type(x[:, D_PACK:].astype(BF16).astype(F32), jnp.uint32)
    return (hi & jnp.uint32(0xFFFF0000)) | (lo >> 16)


def _unpack_rows(w):
    hi = lax.bitcast_convert_type(w & jnp.uint32(0xFFFF0000), F32)
    lo = lax.bitcast_convert_type(w << 16, F32)
    return hi, lo


def _params(sem, **kw):
    return pltpu.CompilerParams(dimension_semantics=sem, vmem_limit_bytes=VMEM_LIMIT, **kw)


def _inproj_kernel(x_ref, g_ref, b_ref, w_ref, ws_ref, cw_ref, cb_ref,
                   h_ref, proj_ref, small_ref, smallt_ref, hist, *, tiles_per_seq):
    i = pl.program_id(0)
    tm = x_ref.shape[0]
    h = _layer_norm(x_ref[...], g_ref[...], b_ref[...])
    h_ref[...] = h
    hb = h.astype(BF16)

    @pl.when(i % tiles_per_seq == 0)
    def _():
        hist[...] = jnp.zeros_like(hist)

    for g in range(N_MAIN // CONV_GROUP):
        cols = slice(g * CONV_GROUP, (g + 1) * CONV_GROUP)
        p = _dot(hb, w_ref[:, cols])
        if g in CONV_GROUPS:
            xcat = jnp.concatenate([hist[:, cols], p], axis=0)
            acc = cb_ref[:, cols]
            for j in range(CONV_WIDTH):
                off = HALO - (CONV_WIDTH - 1) + j
                acc = acc + xcat[off:off + tm, :] * cw_ref[j:j + 1, cols]
            hist[:, cols] = p[tm - HALO:, :]
            p = acc
        proj_ref[:, cols] = p
    small = _dot3(h, ws_ref[...])
    small_ref[...] = small
    smallt_ref[...] = small.T[:smallt_ref.shape[0], :]


def _inproj(x2d, g, b, w_main, w_small, conv_w, conv_b, tm, seq):
    t = x2d.shape[0]
    return pl.pallas_call(
        functools.partial(_inproj_kernel, tiles_per_seq=seq // tm),
        grid=(t // tm,),
        in_specs=[
            pl.BlockSpec((tm, D_MODEL), lambda i: (i, 0)),
            pl.BlockSpec((1, D_MODEL), lambda i: (0, 0)),
            pl.BlockSpec((1, D_MODEL), lambda i: (0, 0)),
            pl.BlockSpec((D_MODEL, N_MAIN), lambda i: (0, 0)),
            pl.BlockSpec((D_MODEL, LANES), lambda i: (0, 0)),
            pl.BlockSpec((CONV_WIDTH, N_MAIN), lambda i: (0, 0)),
            pl.BlockSpec((1, N_MAIN), lambda i: (0, 0)),
        ],
        out_specs=[
            pl.BlockSpec((tm, D_MODEL), lambda i: (i, 0)),
            pl.BlockSpec((tm, N_MAIN), lambda i: (i, 0)),
            pl.BlockSpec((tm, LANES), lambda i: (i, 0)),
            pl.BlockSpec((8, tm), lambda i: (0, i)),
        ],
        out_shape=[
            jax.ShapeDtypeStruct((t, D_MODEL), F32),
            jax.ShapeDtypeStruct((t, N_MAIN), F32),
            jax.ShapeDtypeStruct((t, LANES), F32),
            jax.ShapeDtypeStruct((8, t), F32),
        ],
        scratch_shapes=[pltpu.VMEM((HALO, N_MAIN), F32)],
        compiler_params=_params(("arbitrary",)),
        name="ln_inproj",
    )(x2d, g, b, w_main, w_small, conv_w, conv_b)


def _lru_tile(xc, gate, wg, bg, lam, og, carry):
    rows = xc.shape[0]
    gates = _dot(xc.astype(BF16), wg) + bg
    r = _sigmoid(gates[:, :LRU_WIDTH])
    i = _sigmoid(gates[:, LRU_WIDTH:])
    log_a = (-LRU_C) * r * _softplus(-lam)
    a = jnp.exp(log_a)
    one_minus_a2 = -jnp.tanh(log_a) * (a * a + 1.0)
    mult = jnp.where(one_minus_a2 > 0.0, one_minus_a2 * lax.rsqrt(one_minus_a2), 0.0)
    bv = mult * (i * xc)
    a = a.reshape(rows // SCAN_GROUP, SCAN_GROUP, LRU_WIDTH)
    bv = bv.reshape(rows // SCAN_GROUP, SCAN_GROUP, LRU_WIDTH)
    row_in_group = lax.broadcasted_iota(jnp.int32, a.shape, 1)
    d = 1
    while d < SCAN_GROUP:
        a_sh = jnp.where(row_in_group < d, 1.0, pltpu.roll(a, d, 1))
        b_sh = jnp.where(row_in_group < d, 0.0, pltpu.roll(bv, d, 1))
        bv = a * b_sh + bv
        a = a * a_sh
        d *= 2
    a = a.reshape(rows, LRU_WIDTH)
    bv = bv.reshape(rows, LRU_WIDTH)
    parts = []
    for g in range(rows // SCAN_GROUP):
        grp = slice(g * SCAN_GROUP, (g + 1) * SCAN_GROUP)
        hg = a[grp] * carry + bv[grp]
        carry = hg[SCAN_GROUP - 1:, :]
        parts.append(hg)
    h = jnp.concatenate(parts, axis=0)
    y = h * _gelu_tanh(gate)
    ms = jnp.mean(y * y, -1, keepdims=True)
    return y * lax.rsqrt(ms + NORM_EPS) * og, carry


def _bdot(a, b, dims=NN):
    return _dot(a.astype(BF16), b.astype(BF16), dims)


def _gdn_heads(args, norm_w):
    c = GDN_CHUNK
    r = GDN_GROUP * c
    ri = lax.broadcasted_iota(jnp.int32, (r, r), 0)
    ci = lax.broadcasted_iota(jnp.int32, (r, r), 1)
    same = (ri // c) == (ci // c)
    causal = same & (ri >= ci)
    strict = same & (ri > ci)
    upper = same & (ri <= ci)
    chunk_of_row = lax.broadcasted_iota(jnp.int32, (r, 1), 0) // c
    each = lambda f, *ls: [f(*xs) for xs in zip(*ls)]
    q, k, v, z, beta, g_col, g_row, st = [list(x) for x in zip(*args)]
    q = each(lambda x: x * lax.rsqrt(jnp.sum(x * x, -1, keepdims=True) + NORM_EPS) * (GDN_DK ** -0.5), q)
    k = each(lambda x: x * lax.rsqrt(jnp.sum(x * x, -1, keepdims=True) + NORM_EPS), k)
    gc_col = each(lambda g: jnp.sum(jnp.where(causal, g, 0.0), axis=1, keepdims=True), g_row)
    gc_row = each(lambda g: jnp.sum(jnp.where(upper, g, 0.0), axis=0, keepdims=True), g_col)
    decay = each(lambda gc, gr: jnp.exp(jnp.where(causal, gc - gr, -jnp.inf)), gc_col, gc_row)
    kb = each(lambda x, bt: x * bt, k, beta)
    vb = each(lambda x, bt: x * bt, v, beta)
    kk = each(lambda x, y: _bdot(x, y, NT), kb, k)
    a_mat = each(lambda m, d: jnp.where(strict, m * d, 0.0), kk, decay)
    e_col = each(jnp.exp, gc_col)
    rhs = each(lambda x, y, e: jnp.concatenate([x, y * e], axis=1), vb, kb, e_col)
    sol = each(lambda rr, a: rr - _bdot(a, rr), rhs, a_mat)
    p = a_mat
    for _ in range(5):
        p = each(lambda x: _bdot(x, x), p)
        sol = each(lambda x, y: y + _bdot(x, y), p, sol)
    qk = each(lambda x, y: _bdot(x, y, NT), q, k)
    qk = each(lambda m, d: jnp.where(causal, m * d, 0.0), qk, decay)
    q_dec = each(lambda x, e: x * e, q, e_col)
    g_last = [each(lambda gc: gc[(j + 1) * c - 1:(j + 1) * c, :], gc_col) for j in range(GDN_GROUP)]

    def last_of_own_chunk(*gl):
        out = gl[-1]
        for j in range(GDN_GROUP - 2, -1, -1):
            out = jnp.where(chunk_of_row == j, gl[j], out)
        return out

    g_end = each(last_of_own_chunk, *g_last)
    k_dec = each(lambda x, ge, gc: x * jnp.exp(ge - gc), k, g_end, gc_col)
    qs_parts, v_parts = [], []
    for j in range(GDN_GROUP):
        rows = slice(j * c, (j + 1) * c)
        ws = each(lambda x, s: _bdot(x[rows, GDN_DV:], s), sol, st)
        qs_parts.append(each(lambda x, s: _bdot(x[rows], s), q_dec, st))
        v_new = each(lambda x, w: x[rows, :GDN_DV] - w, sol, ws)
        v_parts.append(v_new)
        kv = each(lambda x, vn: _bdot(x[rows], vn, TN), k_dec, v_new)
        st = each(lambda s, gl, d: s * jnp.exp(gl) + d, st, g_last[j], kv)
    qs = each(lambda *parts: jnp.concatenate(parts, axis=0), *qs_parts)
    v_all = each(lambda *parts: jnp.concatenate(parts, axis=0), *v_parts)
    o = each(lambda a, m, vn: a + _bdot(m, vn), qs, qk, v_all)
    o = each(lambda x: x * lax.rsqrt(jnp.mean(x * x, -1, keepdims=True) + NORM_EPS) * norm_w, o)
    o = each(lambda x, zz: x * _silu(zz), o, z)
    return list(zip(o, st))


def _mixer_kernel(xc_ref, gate_ref, q_ref, k_ref, v_ref, z_ref, sm_ref, smt_ref,
                  wg_ref, bg_ref, lam_ref, og_ref, alr_ref, dtr_ref, alc_ref, dtc_ref, nw_ref,
                  ylru_ref, y_ref, hcarry, state):
    n = pl.program_id(1)
    c = GDN_GROUP * GDN_CHUNK
    nb = q_ref.shape[0]
    first = n == 0

    @pl.when(first)
    def _():
        state[...] = jnp.zeros_like(state)
        hcarry[...] = jnp.zeros_like(hcarry)

    lru_out = [_lru_tile(xc_ref[b], gate_ref[b], wg_ref[...], bg_ref[...], lam_ref[...], og_ref[...], hcarry[b])
               for b in range(nb)]
    norm_w = nw_ref[...]

    args = []
    for b in range(nb):
        q_all = _silu(q_ref[b])
        k_all = _silu(k_ref[b])
        v_all = _silu(v_ref[b])
        z_all = z_ref[b]
        sm = sm_ref[b]
        beta_all = _sigmoid(sm)
        g_cols = -jnp.exp(alr_ref[...]) * _softplus(sm + dtr_ref[...])
        g_rows = -jnp.exp(alc_ref[...]) * _softplus(smt_ref[b] + dtc_ref[...])
        for hd in range(GDN_HEADS):
            sl = slice(hd * GDN_DK, (hd + 1) * GDN_DK)
            args.append((q_all[:, sl], k_all[:, sl], v_all[:, sl], z_all[:, sl],
                         beta_all[:, hd:hd + 1],
                         g_cols[:, GDN_HEADS + hd:GDN_HEADS + hd + 1],
                         g_rows[GDN_HEADS + hd:GDN_HEADS + hd + 1, :],
                         state[b, hd]))
    outs = _gdn_heads(args, norm_w)
    for b in range(nb):
        for hd in range(GDN_HEADS):
            o, st_new = outs[b * GDN_HEADS + hd]
            state[b, hd] = st_new
            y_ref[b, :, hd * GDN_DK:(hd + 1) * GDN_DK] = o
    for b, (y_lru, carry) in enumerate(lru_out):
        ylru_ref[b] = y_lru
        hcarry[b] = carry


def _mixers(proj3, small3, smallt3, w_gates, b_gates, lam, out_g, alr, dtr, alc, dtc, norm_w, nb):
    bsz, seq, _ = proj3.shape
    c = GDN_GROUP * GDN_CHUNK
    nch = seq // c
    col = lambda j: pl.BlockSpec((nb, c, GDN_QK), lambda b, n: (b, n, j))
    const = lambda shape: pl.BlockSpec(shape, lambda b, n: (0,) * len(shape))
    return pl.pallas_call(
        _mixer_kernel,
        grid=(bsz // nb, nch),
        in_specs=[
            col(0), col(1), col(2), col(3), col(4), col(5),
            pl.BlockSpec((nb, c, LANES), lambda b, n: (b, n, 0)),
            pl.BlockSpec((nb, None, 8, c), lambda b, n: (b, n, 0, 0)),
            const((LRU_WIDTH, 2 * LRU_WIDTH)), const((1, 2 * LRU_WIDTH)), const((1, LRU_WIDTH)), const((1, LRU_WIDTH)),
            const((1, LANES)), const((1, LANES)), const((8, 1)), const((8, 1)),
            const((1, GDN_DV)),
        ],
        out_specs=[pl.BlockSpec((nb, c, LRU_WIDTH), lambda b, n: (b, n, 0)),
                   pl.BlockSpec((nb, c, GDN_V), lambda b, n: (b, n, 0))],
        out_shape=[jax.ShapeDtypeStruct((bsz, seq, LRU_WIDTH), F32),
                   jax.ShapeDtypeStruct((bsz, seq, GDN_V), F32)],
        scratch_shapes=[
            pltpu.VMEM((nb, 1, LRU_WIDTH), F32),
            pltpu.VMEM((nb, GDN_HEADS, GDN_DK, GDN_DV), F32),
        ],
        compiler_params=_params(("arbitrary", "arbitrary")),
        name="mixers",
    )(proj3, proj3, proj3, proj3, proj3, proj3, small3, smallt3, w_gates, b_gates, lam, out_g,
      alr, dtr, alc, dtc, norm_w)


def _pick_experts(logits, rbias):
    n = logits.shape[1]
    scores = _sigmoid(logits)
    choice = scores + rbias
    neg = -jnp.inf
    gs_rows = []
    sub = lax.broadcasted_iota(jnp.int32, (GROUP_SIZE, n), 0).astype(F32)
    for g in range(N_GROUPS):
        cg = choice[g * GROUP_SIZE:(g + 1) * GROUP_SIZE, :]
        m1 = jnp.max(cg, axis=0, keepdims=True)
        i1 = jnp.min(jnp.where(cg == m1, sub, float(GROUP_SIZE)), axis=0, keepdims=True)
        m2 = jnp.max(jnp.where(sub == i1, neg, cg), axis=0, keepdims=True)
        gs_rows.append(m1 + m2)
    gs = jnp.concatenate(gs_rows, axis=0)
    gi = lax.broadcasted_iota(jnp.int32, (N_GROUPS, n), 0).astype(F32)
    gsel = jnp.zeros((N_GROUPS, n), jnp.bool_)
    for _ in range(TOPK_GROUPS):
        m = jnp.max(gs, axis=0, keepdims=True)
        idx = jnp.min(jnp.where(gs == m, gi, float(N_GROUPS)), axis=0, keepdims=True)
        hit = gi == idx
        gsel = jnp.logical_or(gsel, hit)
        gs = jnp.where(hit, neg, gs)
    masked = jnp.concatenate(
        [jnp.where(gsel[g:g + 1, :], choice[g * GROUP_SIZE:(g + 1) * GROUP_SIZE, :], neg)
         for g in range(N_GROUPS)], axis=0)
    ei = lax.broadcasted_iota(jnp.int32, (N_EXPERTS, n), 0).astype(F32)
    hits, e_rows, w_rows = [], [], []
    multi = jnp.zeros((N_EXPERTS, n), F32)
    for _ in range(TOP_K):
        m = jnp.max(masked, axis=0, keepdims=True)
        idx = jnp.min(jnp.where(masked == m, ei, float(N_EXPERTS)), axis=0, keepdims=True)
        hit = ei == idx
        hits.append(hit)
        e_rows.append(idx)
        w_rows.append(jnp.sum(jnp.where(hit, scores, 0.0), axis=0, keepdims=True))
        multi = multi + hit.astype(F32)
        masked = jnp.where(hit, neg, masked)
    wts = jnp.concatenate(w_rows, axis=0)
    wts = wts / (jnp.sum(wts, axis=0, keepdims=True) + 1e-20) * ROUTED_SCALE
    return jnp.concatenate(e_rows, axis=0), wts, hits, multi


def _router_kernel(yl_ref, yg_ref, h0_ref, wo1_ref, wo2_ref, g_ref, b_ref, wrt_ref, rb_ref,
                   h1_ref, h1p_ref, e_ref, w_ref, rank_ref, cnt_ref, carry):
    i = pl.program_id(0)
    tm = h0_ref.shape[0]
    n = min(ROUTER_SUB, tm)
    subs = [slice(j * n, (j + 1) * n) for j in range(tm // n)]

    @pl.when(i == 0)
    def _():
        carry[...] = jnp.zeros_like(carry)

    mixes = [_dot(yl_ref[r, :].astype(BF16), wo1_ref[...]) + _dot(yg_ref[r, :].astype(BF16), wo2_ref[...])
             for r in subs]
    h1s = [_layer_norm(DEEPNORM_ALPHA * h0_ref[r, :] + mix, g_ref[...], b_ref[...])
           for r, mix in zip(subs, mixes)]
    for r, h1 in zip(subs, h1s):
        h1_ref[r, :] = h1
        h1p_ref[r, :] = _pack_rows(h1)
    logits = [_dot3(wrt_ref[...], h1, NT) for h1 in h1s]
    picks = [_pick_experts(lg, rb_ref[...]) for lg in logits]
    ti = lax.broadcasted_iota(jnp.int32, (n, n), 0)
    tj = lax.broadcasted_iota(jnp.int32, (n, n), 1)
    before = (ti < tj).astype(BF16)
    cums = [_dot(multi.astype(BF16), before) for _, _, _, multi in picks]
    base = carry[...]
    for r, (e_rows, wts, hits, multi), cum in zip(subs, picks, cums):
        cum = cum + base
        r_rows = [jnp.sum(jnp.where(hit, cum, 0.0), axis=0, keepdims=True) for hit in hits]
        base = base + jnp.sum(multi, axis=1, keepdims=True)
        e_ref[:, r] = e_rows.astype(jnp.int32)
        w_ref[:, r] = wts
        rank_ref[:, r] = jnp.concatenate(r_rows, axis=0).astype(jnp.int32)
    carry[...] = base
    cnt_ref[...] = base.astype(jnp.int32)


def _router(y_lru, y_gdn, h0, wo1, wo2, g, b, w_router_t, rbias, tm):
    t = h0.shape[0]
    const = lambda shape: pl.BlockSpec(shape, lambda i: (0,) * len(shape))
    return pl.pallas_call(
        _router_kernel,
        grid=(t // tm,),
        in_specs=[
            pl.BlockSpec((tm, LRU_WIDTH), lambda i: (i, 0)),
            pl.BlockSpec((tm, GDN_V), lambda i: (i, 0)),
            pl.BlockSpec((tm, D_MODEL), lambda i: (i, 0)),
            const((LRU_WIDTH, D_MODEL)), const((GDN_V, D_MODEL)),
            const((1, D_MODEL)), const((1, D_MODEL)),
            const((N_EXPERTS, D_MODEL)), const((N_EXPERTS, 1)),
        ],
        out_specs=[
            pl.BlockSpec((tm, D_MODEL), lambda i: (i, 0)),
            pl.BlockSpec((tm, D_PACK), lambda i: (i, 0)),
            pl.BlockSpec((TOP_K, tm), lambda i: (0, i)),
            pl.BlockSpec((TOP_K, tm), lambda i: (0, i)),
            pl.BlockSpec((TOP_K, tm), lambda i: (0, i)),
            const((N_EXPERTS, 1)),
        ],
        out_shape=[
            jax.ShapeDtypeStruct((t, D_MODEL), F32),
            jax.ShapeDtypeStruct((t, D_PACK), jnp.uint32),
            jax.ShapeDtypeStruct((TOP_K, t), jnp.int32),
            jax.ShapeDtypeStruct((TOP_K, t), F32),
            jax.ShapeDtypeStruct((TOP_K, t), jnp.int32),
            jax.ShapeDtypeStruct((N_EXPERTS, 1), jnp.int32),
        ],
        scratch_shapes=[pltpu.VMEM((N_EXPERTS, 1), F32)],
        compiler_params=_params(("arbitrary",)),
        name="outproj_router",
    )(y_lru, y_gdn, h0, wo1, wo2, g, b, w_router_t, rbias)


def _dest_kernel(e_ref, r_ref, ps_ref, d_ref):
    tm = e_ref.shape[1]
    ei = lax.broadcasted_iota(jnp.int32, (N_EXPERTS, tm), 0)
    rows = []
    for k in range(TOP_K):
        hit = ei == e_ref[k:k + 1, :]
        rows.append(jnp.sum(jnp.where(hit, ps_ref[...], 0), axis=0, keepdims=True))
    d_ref[...] = jnp.concatenate(rows, axis=0) + r_ref[...]


def _dest(top_e, rank, pad_start, tm):
    t = top_e.shape[1]
    blk = pl.BlockSpec((TOP_K, tm), lambda i: (0, i))
    return pl.pallas_call(
        _dest_kernel,
        grid=(t // tm,),
        in_specs=[blk, blk, pl.BlockSpec((N_EXPERTS, 1), lambda i: (0, 0))],
        out_specs=blk,
        out_shape=jax.ShapeDtypeStruct((TOP_K, t), jnp.int32),
        compiler_params=_params(("arbitrary",)),
        name="moe_dest",
    )(top_e, rank, pad_start)


def _sc_scatter_rows(rows, idx, n_out, chunk):
    n_copies, t = idx.shape
    d = rows.shape[1]
    per_worker = t // SC_WORKERS
    n_chunks = per_worker // chunk
    mesh = plsc.VectorSubcoreMesh(core_axis_name="c", subcore_axis_name="s")
    idx_flat = idx.reshape(n_copies * t)

    @functools.partial(
        pl.kernel, mesh=mesh,
        out_type=jax.ShapeDtypeStruct((n_out, d), rows.dtype),
        scratch_types=[pltpu.VMEM((chunk,), jnp.int32) for _ in range(n_copies)] + [
            pltpu.VMEM((chunk, d), rows.dtype),
            pltpu.SemaphoreType.DMA,
        ],
    )
    def scatter(rows_hbm, idx_hbm, out_hbm, *scratch):
        idx_v = scratch[:n_copies]
        rows_v, sem = scratch[n_copies:]
        wid = lax.axis_index("s") * SC_CORES + lax.axis_index("c")
        base = wid * per_worker

        @pl.loop(0, n_chunks)
        def _(j):
            off = base + j * chunk
            for k in range(n_copies):
                pltpu.sync_copy(idx_hbm.at[pl.ds(k * t + off, chunk)], idx_v[k])
            pltpu.sync_copy(rows_hbm.at[pl.ds(off, chunk)], rows_v)
            copies = [pltpu.async_copy(rows_v, out_hbm.at[idx_v[k]], sem) for k in range(n_copies)]
            for cp in copies:
                cp.wait()

    return scatter(rows, idx_flat)


def _expert_kernel(be_ref, nv_ref, first_ref, slot_ref, next_ref, nu_ref,
                   xs_ref, wg_hbm, wu_hbm, wd_hbm, ys_ref, wg_f, wu_f, wd_f, wgu_b, wd_b, sem):
    i = pl.program_id(0)

    def fetch(e, slot):
        return (pltpu.make_async_copy(wg_hbm.at[e], wg_f.at[slot], sem.at[slot]),
                pltpu.make_async_copy(wu_hbm.at[e], wu_f.at[slot], sem.at[slot]),
                pltpu.make_async_copy(wd_hbm.at[e], wd_f.at[slot], sem.at[slot]))

    @pl.when(i < nu_ref[0])
    def _():
        e = be_ref[i]
        slot = slot_ref[i]

        @pl.when(first_ref[i] == 1)
        def _():
            @pl.when(i == 0)
            def _():
                for cp in fetch(e, slot):
                    cp.start()

            for cp in fetch(e, slot):
                cp.wait()

            @pl.when(next_ref[i] >= 0)
            def _():
                for cp in fetch(next_ref[i], 1 - slot):
                    cp.start()

            wgu_b[:, :D_EXPERT] = wg_f[slot].astype(BF16)
            wgu_b[:, D_EXPERT:] = wu_f[slot].astype(BF16)
            wd_b[...] = wd_f[slot].astype(BF16)

        n = xs_ref.shape[0] // EXPERT_BANDS
        bands = [slice(j * n, (j + 1) * n) for j in range(EXPERT_BANDS)]
        row = lax.broadcasted_iota(jnp.int32, (n, D_PACK), 0)
        xs = [_unpack_rows(jnp.where(row + j * n < nv_ref[i], xs_ref[r, :], jnp.uint32(0)))
              for j, r in enumerate(bands)]
        gus = [_dot(x_hi.astype(BF16), wgu_b[:D_PACK, :]) + _dot(x_lo.astype(BF16), wgu_b[D_PACK:, :])
               for x_hi, x_lo in xs]
        hs = [_silu(gu[:, :D_EXPERT]) * gu[:, D_EXPERT:] for gu in gus]
        ys = [_dot(h.astype(BF16), wd_b[...]) for h in hs]
        for r, y in zip(bands, ys):
            ys_ref[r, :] = _pack_rows(y)


def _experts(blk_e, n_valid, first, slot, next_e, n_used, xs, w_gate, w_up, w_down):
    n_rows = xs.shape[0]
    n_blocks = n_rows // MOE_BLOCK
    blk = lambda i, be, nv, fi, sl, nx, nu: (jnp.minimum(i, nu[0] - 1), 0)
    return pl.pallas_call(
        _expert_kernel,
        grid_spec=pltpu.PrefetchScalarGridSpec(
            num_scalar_prefetch=6,
            grid=(n_blocks,),
            in_specs=[
                pl.BlockSpec((MOE_BLOCK, D_PACK), blk),
                pl.BlockSpec(memory_space=pl.ANY),
                pl.BlockSpec(memory_space=pl.ANY),
                pl.BlockSpec(memory_space=pl.ANY),
            ],
            out_specs=pl.BlockSpec((MOE_BLOCK, D_PACK), blk),
            scratch_shapes=[
                pltpu.VMEM((2, D_MODEL, D_EXPERT), F32),
                pltpu.VMEM((2, D_MODEL, D_EXPERT), F32),
                pltpu.VMEM((2, D_EXPERT, D_MODEL), F32),
                pltpu.VMEM((D_MODEL, 2 * D_EXPERT), BF16),
                pltpu.VMEM((D_EXPERT, D_MODEL), BF16),
                pltpu.SemaphoreType.DMA((2,)),
            ],
        ),
        out_shape=jax.ShapeDtypeStruct((n_rows, D_PACK), jnp.uint32),
        compiler_params=_params(("arbitrary",)),
        name="moe_experts",
    )(blk_e, n_valid, first, slot, next_e, n_used, xs, w_gate, w_up, w_down)


def _sc_gather_rows(table, idx, chunk):
    n_idx = idx.shape[0]
    d = table.shape[1]
    per_worker = n_idx // SC_WORKERS
    n_chunks = per_worker // chunk
    assert n_chunks % 2 == 0 and n_chunks * chunk * SC_WORKERS == n_idx
    mesh = plsc.VectorSubcoreMesh(core_axis_name="c", subcore_axis_name="s")

    @functools.partial(
        pl.kernel, mesh=mesh,
        out_type=jax.ShapeDtypeStruct((n_idx, d), table.dtype),
        scratch_types=[
            pltpu.VMEM((chunk,), jnp.int32), pltpu.VMEM((chunk,), jnp.int32),
            pltpu.VMEM((chunk, d), table.dtype), pltpu.VMEM((chunk, d), table.dtype),
            pltpu.SemaphoreType.DMA, pltpu.SemaphoreType.DMA, pltpu.SemaphoreType.DMA, pltpu.SemaphoreType.DMA,
        ],
    )
    def gather(table_hbm, idx_hbm, out_hbm, idx_v0, idx_v1, rows_v0, rows_v1, gsem0, gsem1, osem0, osem1):
        idx_v, rows_v, gsem, osem = (idx_v0, idx_v1), (rows_v0, rows_v1), (gsem0, gsem1), (osem0, osem1)
        wid = lax.axis_index("s") * SC_CORES + lax.axis_index("c")
        base = wid * per_worker

        def gather_copy(slot):
            return pltpu.make_async_copy(table_hbm.at[idx_v[slot]], rows_v[slot], gsem[slot])

        def out_copy(c, slot):
            return pltpu.make_async_copy(rows_v[slot], out_hbm.at[pl.ds(base + c * chunk, chunk)], osem[slot])

        def start_gather(c, slot):
            pltpu.sync_copy(idx_hbm.at[pl.ds(base + c * chunk, chunk)], idx_v[slot])
            gather_copy(slot).start()

        start_gather(0, 0)

        @pl.loop(0, n_chunks, step=2)
        def _(j):
            for b in range(2):
                c = j + b
                cur, other = b, 1 - b

                @pl.when(c >= 1)
                def _():
                    out_copy(c - 1, other).wait()

                @pl.when(c + 1 < n_chunks)
                def _():
                    start_gather(c + 1, other)

                gather_copy(cur).wait()
                out_copy(c, cur).start()

        out_copy(n_chunks - 1, 1).wait()

    return gather(table, idx)


def _combine_kernel(h1_ref, wts_ref, wsgu_ref, wsd_ref, g_ref, b_ref, yg_ref, out_ref):
    h1 = h1_ref[...]
    gu = _dot(h1.astype(BF16), wsgu_ref[...])
    hs = _silu(gu[:, :D_SHARED]) * gu[:, D_SHARED:]
    acc = DEEPNORM_ALPHA * h1 + _dot(hs.astype(BF16), wsd_ref[...])
    wts = wts_ref[...].T
    acc_hi = acc[:, :D_PACK]
    acc_lo = acc[:, D_PACK:]
    for k in range(TOP_K):
        y_hi, y_lo = _unpack_rows(yg_ref[k])
        acc_hi = acc_hi + y_hi * wts[:, k:k + 1]
        acc_lo = acc_lo + y_lo * wts[:, k:k + 1]
    out_ref[...] = _layer_norm(jnp.concatenate([acc_hi, acc_lo], axis=1), g_ref[...], b_ref[...])


def _combine(h1, wts, ws_gu, ws_down, g, b, yg, tm):
    t = h1.shape[0]
    const = lambda shape: pl.BlockSpec(shape, lambda i: (0,) * len(shape))
    return pl.pallas_call(
        _combine_kernel,
        grid=(t // tm,),
        in_specs=[
            pl.BlockSpec((tm, D_MODEL), lambda i: (i, 0)),
            pl.BlockSpec((TOP_K, tm), lambda i: (0, i)),
            const((D_MODEL, 2 * D_SHARED)), const((D_SHARED, D_MODEL)),
            const((1, D_MODEL)), const((1, D_MODEL)),
            pl.BlockSpec((TOP_K, tm, D_PACK), lambda i: (0, i, 0)),
        ],
        out_specs=pl.BlockSpec((tm, D_MODEL), lambda i: (i, 0)),
        out_shape=jax.ShapeDtypeStruct((t, D_MODEL), F32),
        compiler_params=_params(("arbitrary",)),
        name="moe_combine",
    )(h1, wts, ws_gu, ws_down, g, b, yg)


def _block_diag(w):
    nb, bi, bo = w.shape
    eye = jnp.eye(nb, dtype=w.dtype)
    return (eye[:, None, :, None] * w[:, :, None, :]).reshape(nb * bi, nb * bo)


def _pad_lanes(v, offset, width):
    return jnp.zeros((1, width), F32).at[0, offset:offset + v.shape[0]].set(v)


def _layer(h_in_x, l, p, tiles):
    bsz, seq, _ = h_in_x.shape
    t = bsz * seq
    row = lambda v: v.reshape(1, -1)

    w_in = p['w_in'][l]
    w_main = w_in.astype(BF16)
    w_small = jnp.zeros((D_MODEL, LANES), F32).at[:, :2 * GDN_HEADS].set(w_in[:, N_MAIN:])
    zeros = lambda n: jnp.zeros((CONV_WIDTH, n), F32)
    conv_w = jnp.concatenate([p['lru_conv_w'][l], zeros(LRU_WIDTH), p['gdn_conv_w'][l], zeros(GDN_V)], 1)
    conv_b = jnp.zeros((1, N_MAIN), F32).at[0, :LRU_WIDTH].set(p['lru_conv_b'][l])
    h0, proj, small, small_t = _inproj(h_in_x.reshape(t, D_MODEL), row(p['ln_g']), row(p['ln_b']),
                                       w_main, w_small, conv_w, conv_b, tiles['inproj'], seq)
    proj3 = proj.reshape(bsz, seq, N_MAIN)

    w_gates = jnp.concatenate([_block_diag(p['lru_w_rg'][l]), _block_diag(p['lru_w_ig'][l])], 1).astype(BF16)
    b_gates = jnp.concatenate([p['lru_b_rg'][l], p['lru_b_ig'][l]]).reshape(1, -1)
    rows = GDN_GROUP * GDN_CHUNK
    small3 = small.reshape(bsz, seq, LANES)
    smallt3 = small_t.reshape(8, bsz, seq // rows, rows).transpose(1, 2, 0, 3)
    a_log, dt_bias = p['gdn_a_log'][l], p['gdn_dt_bias'][l]
    alr = _pad_lanes(a_log, GDN_HEADS, LANES)
    dtr = _pad_lanes(dt_bias, GDN_HEADS, LANES)
    alc = _pad_lanes(a_log, GDN_HEADS, 8).reshape(8, 1)
    dtc = _pad_lanes(dt_bias, GDN_HEADS, 8).reshape(8, 1)
    y_lru, y_gdn = _mixers(proj3, small3, smallt3, w_gates, b_gates, row(p['lru_lambda'][l]), row(p['lru_out_g'][l]),
                           alr, dtr, alc, dtc, row(p['gdn_norm_w'][l]), tiles['gdn_nb'])

    w_out = p['w_out'][l].astype(BF16)
    h1, h1p, top_e, wts, rank, counts = _router(
        y_lru.reshape(t, LRU_WIDTH), y_gdn.reshape(t, GDN_V), h0, w_out[:LRU_WIDTH], w_out[LRU_WIDTH:],
        row(p['ln1_g'][l]), row(p['ln1_b'][l]), p['w_router'][l].T, p['router_bias'][l].reshape(-1, 1),
        tiles['router'])

    counts = counts[:, 0]
    padded = (counts + MOE_BLOCK - 1) // MOE_BLOCK * MOE_BLOCK
    pad_end = jnp.cumsum(padded)
    pad_start = pad_end - padded
    n_blocks = (t * TOP_K + N_EXPERTS * (MOE_BLOCK - 1)) // MOE_BLOCK
    n_rows = n_blocks * MOE_BLOCK
    n_used = (pad_end[-1] // MOE_BLOCK).astype(jnp.int32)
    blk_ids = jnp.minimum(jnp.arange(n_blocks, dtype=jnp.int32), n_used - 1)
    blk_e = jnp.minimum(jnp.sum(pad_end[None, :] <= (blk_ids * MOE_BLOCK)[:, None], axis=1),
                        N_EXPERTS - 1).astype(jnp.int32)

    dest = _dest(top_e, rank, pad_start.reshape(-1, 1), tiles['dest'])
    n_valid = jnp.clip(counts[blk_e] - (blk_ids * MOE_BLOCK - pad_start[blk_e]), 0, MOE_BLOCK).astype(jnp.int32)
    xs = _sc_scatter_rows(h1p, dest, n_rows, 2 * SC_CHUNK)
    active = jnp.arange(n_blocks, dtype=jnp.int32) < n_used
    first = (active & jnp.concatenate([jnp.ones((1,), bool), blk_e[1:] != blk_e[:-1]])).astype(jnp.int32)
    slot = ((jnp.cumsum(first) - 1) % 2).astype(jnp.int32)
    used = counts > 0
    later = jnp.where(used[None, :] & (jnp.arange(N_EXPERTS)[None, :] > jnp.arange(N_EXPERTS)[:, None]),
                      jnp.arange(N_EXPERTS, dtype=jnp.int32)[None, :], N_EXPERTS)
    next_used = jnp.min(later, axis=1)
    next_e = jnp.where(next_used < N_EXPERTS, next_used, -1)[blk_e].astype(jnp.int32)
    ys = _experts(blk_e, n_valid, first, slot, next_e, n_used.reshape(1), xs,
                  p['w_gate'][l], p['w_up'][l], p['w_down'][l])
    ws_gu = jnp.concatenate([p['ws_gate'][l], p['ws_up'][l]], 1).astype(BF16)
    yg = _sc_gather_rows(ys, dest.reshape(TOP_K * t), SC_CHUNK).reshape(TOP_K, t, D_PACK)
    out = _combine(h1, wts, ws_gu, p['ws_down'][l].astype(BF16),
                   row(p['ln2_g'][l]), row(p['ln2_b'][l]), yg, tiles['combine'])
    return out.reshape(bsz, seq, D_MODEL)


def _tiles(bsz, seq):
    t = bsz * seq
    return {
        'inproj': min(512, t),
        'gdn_nb': bsz,
        'router': min(512, t),
        'dest': min(512, t),
        'combine': min(512, t),
    }


def kernel(x, ln_in_g, ln_in_b, w_in, lru_conv_w, lru_conv_b, lru_w_rg, lru_b_rg, lru_w_ig, lru_b_ig,
           lru_lambda, lru_out_g, gdn_conv_w, gdn_a_log, gdn_dt_bias, gdn_norm_w, w_out, ln1_g, ln1_b,
           w_router, router_bias, w_gate, w_up, w_down, ws_gate, ws_up, ws_down, ln2_g, ln2_b):
    assert w_in.shape[0] == DEPTH == 1
    p = dict(ln_g=ln_in_g, ln_b=ln_in_b, w_in=w_in, lru_conv_w=lru_conv_w, lru_conv_b=lru_conv_b,
             lru_w_rg=lru_w_rg, lru_b_rg=lru_b_rg, lru_w_ig=lru_w_ig, lru_b_ig=lru_b_ig,
             lru_lambda=lru_lambda, lru_out_g=lru_out_g, gdn_conv_w=gdn_conv_w, gdn_a_log=gdn_a_log,
             gdn_dt_bias=gdn_dt_bias, gdn_norm_w=gdn_norm_w, w_out=w_out, ln1_g=ln1_g, ln1_b=ln1_b,
             w_router=w_router, router_bias=router_bias, w_gate=w_gate, w_up=w_up, w_down=w_down,
             ws_gate=ws_gate, ws_up=ws_up, ws_down=ws_down, ln2_g=ln2_g, ln2_b=ln2_b)
    bsz, seq, _ = x.shape
    return _layer(x, 0, p, _tiles(bsz, seq))
```

```python
import functools

import jax
import jax.numpy as jnp
from jax import lax
from jax.experimental import pallas as pl
from jax.experimental.pallas import tpu as pltpu
from jax.experimental.pallas import tpu_sc as plsc

F32 = jnp.float32
BF16 = jnp.bfloat16

D_MODEL = 1024
LRU_WIDTH = 512
LRU_C = 8.0
CONV_WIDTH = 4
GDN_HEADS = 4
GDN_DK = 128
GDN_DV = 128
GDN_CHUNK = 64
GDN_GROUP = 2
GDN_QK = GDN_HEADS * GDN_DK
GDN_V = GDN_HEADS * GDN_DV
N_MAIN = 2 * LRU_WIDTH + 2 * GDN_QK + 2 * GDN_V
N_EXPERTS = 256
TOP_K = 8
N_GROUPS = 8
GROUP_SIZE = N_EXPERTS // N_GROUPS
TOPK_GROUPS = 4
D_EXPERT = 256
D_SHARED = 256
ROUTED_SCALE = 2.5
MOE_BLOCK = 640
D_PACK = D_MODEL // 2
LN_EPS = 1e-5
NORM_EPS = 1e-6
DEPTH = 1
DEEPNORM_ALPHA = (2.0 * DEPTH) ** 0.25

SCAN_GROUP = 8
HALO = 8
CONV_GROUP = 512
CONV_GROUPS = (0, 2, 3, 4)
LANES = 128
VMEM_LIMIT = 56 * 1024 * 1024
ROUTER_SUB = 256
EXPERT_BANDS = 5
SC_CORES = 2
SC_WORKERS = 32
SC_GATHER_CHUNK = 64
SC_SCATTER_CHUNK = 128

NN = (((1,), (0,)), ((), ()))
NT = (((1,), (1,)), ((), ()))
TN = (((0,), (0,)), ((), ()))


def _dot(a, b, dims=NN):
    return lax.dot_general(a, b, dims, preferred_element_type=F32)


def _split(a):
    hi = a.astype(BF16)
    lo = (a - hi.astype(F32)).astype(BF16)
    return hi, lo


def _dot3(a, b, dims=NN):
    ah, al = _split(a)
    bh, bl = _split(b)
    return _dot(ah, bh, dims) + (_dot(ah, bl, dims) + _dot(al, bh, dims))


def _layer_norm(x, g, b):
    mu = jnp.mean(x, -1, keepdims=True)
    xc = x - mu
    var = jnp.mean(xc * xc, -1, keepdims=True)
    return xc * lax.rsqrt(var + LN_EPS) * g + b


def _sigmoid(x):
    return 1.0 / (1.0 + jnp.exp(-x))


def _silu(x):
    return x * _sigmoid(x)


def _softplus(x):
    return jnp.maximum(x, 0.0) + jnp.log1p(jnp.exp(-jnp.abs(x)))


def _gelu_tanh(x):
    c = 0.7978845608028654
    return x * (0.5 * (1.0 + jnp.tanh(c * (x + 0.044715 * (x * x * x)))))


def _pack_rows(x):
    hi = lax.bitcast_convert_type(x[:, :D_PACK].astype(BF16).astype(F32), jnp.uint32)
    lo = lax.bitcast_convert_type(x[:, D_PACK:].astype(BF16).astype(F32), jnp.uint32)
    return (hi & jnp.uint32(0xFFFF0000)) | (lo >> 16)


def _unpack_rows(w):
    hi = lax.bitcast_convert_type(w & jnp.uint32(0xFFFF0000), F32)
    lo = lax.bitcast_convert_type(w << 16, F32)
    return hi, lo


def _params(sem, **kw):
    return pltpu.CompilerParams(dimension_semantics=sem, vmem_limit_bytes=VMEM_LIMIT, **kw)


def _inproj_kernel(x_ref, g_ref, b_ref, w_ref, ws_ref, cw_ref, cb_ref,
                   h_ref, proj_ref, small_ref, smallt_ref, hist, *, tiles_per_seq):
    i = pl.program_id(0)
    tm = x_ref.shape[0]
    h = _layer_norm(x_ref[...], g_ref[...], b_ref[...])
    h_ref[...] = h
    hb = h.astype(BF16)

    @pl.when(i % tiles_per_seq == 0)
    def _():
        hist[...] = jnp.zeros_like(hist)

    for g in range(N_MAIN // CONV_GROUP):
        cols = slice(g * CONV_GROUP, (g + 1) * CONV_GROUP)
        p = _dot(hb, w_ref[:, cols])
        if g in CONV_GROUPS:
            xcat = jnp.concatenate([hist[:, cols], p], axis=0)
            acc = cb_ref[:, cols]
            for j in range(CONV_WIDTH):
                off = HALO - (CONV_WIDTH - 1) + j
                acc = acc + xcat[off:off + tm, :] * cw_ref[j:j + 1, cols]
            hist[:, cols] = p[tm - HALO:, :]
            p = acc
        proj_ref[:, cols] = p
    small = _dot3(h, ws_ref[...])
    small_ref[...] = small
    smallt_ref[...] = small.T[:smallt_ref.shape[0], :]


def _inproj(x2d, g, b, w_main, w_small, conv_w, conv_b, tm, seq):
    t = x2d.shape[0]
    return pl.pallas_call(
        functools.partial(_inproj_kernel, tiles_per_seq=seq // tm),
        grid=(t // tm,),
        in_specs=[
            pl.BlockSpec((tm, D_MODEL), lambda i: (i, 0)),
            pl.BlockSpec((1, D_MODEL), lambda i: (0, 0)),
            pl.BlockSpec((1, D_MODEL), lambda i: (0, 0)),
            pl.BlockSpec((D_MODEL, N_MAIN), lambda i: (0, 0)),
            pl.BlockSpec((D_MODEL, LANES), lambda i: (0, 0)),
            pl.BlockSpec((CONV_WIDTH, N_MAIN), lambda i: (0, 0)),
            pl.BlockSpec((1, N_MAIN), lambda i: (0, 0)),
        ],
        out_specs=[
            pl.BlockSpec((tm, D_MODEL), lambda i: (i, 0)),
            pl.BlockSpec((tm, N_MAIN), lambda i: (i, 0)),
            pl.BlockSpec((tm, LANES), lambda i: (i, 0)),
            pl.BlockSpec((8, tm), lambda i: (0, i)),
        ],
        out_shape=[
            jax.ShapeDtypeStruct((t, D_MODEL), F32),
            jax.ShapeDtypeStruct((t, N_MAIN), F32),
            jax.ShapeDtypeStruct((t, LANES), F32),
            jax.ShapeDtypeStruct((8, t), F32),
        ],
        scratch_shapes=[pltpu.VMEM((HALO, N_MAIN), F32)],
        compiler_params=_params(("arbitrary",)),
        name="ln_inproj",
    )(x2d, g, b, w_main, w_small, conv_w, conv_b)


def _lru_tile(xc, gate, wg, bg, lam, og, carry):
    rows = xc.shape[0]
    gates = _dot(xc.astype(BF16), wg) + bg
    r = _sigmoid(gates[:, :LRU_WIDTH])
    i = _sigmoid(gates[:, LRU_WIDTH:])
    log_a = (-LRU_C) * r * _softplus(-lam)
    a = jnp.exp(log_a)
    one_minus_a2 = -jnp.tanh(log_a) * (a * a + 1.0)
    mult = jnp.where(one_minus_a2 > 0.0, one_minus_a2 * lax.rsqrt(one_minus_a2), 0.0)
    bv = mult * (i * xc)
    a = a.reshape(rows // SCAN_GROUP, SCAN_GROUP, LRU_WIDTH)
    bv = bv.reshape(rows // SCAN_GROUP, SCAN_GROUP, LRU_WIDTH)
    row_in_group = lax.broadcasted_iota(jnp.int32, a.shape, 1)
    d = 1
    while d < SCAN_GROUP:
        a_sh = jnp.where(row_in_group < d, 1.0, pltpu.roll(a, d, 1))
        b_sh = jnp.where(row_in_group < d, 0.0, pltpu.roll(bv, d, 1))
        bv = a * b_sh + bv
        a = a * a_sh
        d *= 2
    a = a.reshape(rows, LRU_WIDTH)
    bv = bv.reshape(rows, LRU_WIDTH)
    parts = []
    for g in range(rows // SCAN_GROUP):
        grp = slice(g * SCAN_GROUP, (g + 1) * SCAN_GROUP)
        hg = a[grp] * carry + bv[grp]
        carry = hg[SCAN_GROUP - 1:, :]
        parts.append(hg)
    h = jnp.concatenate(parts, axis=0)
    y = h * _gelu_tanh(gate)
    ms = jnp.mean(y * y, -1, keepdims=True)
    return y * lax.rsqrt(ms + NORM_EPS) * og, carry


def _bdot(a, b, dims=NN):
    return _dot(a.astype(BF16), b.astype(BF16), dims)


def _gdn_heads(args, norm_w):
    c = GDN_CHUNK
    r = GDN_GROUP * c
    ri = lax.broadcasted_iota(jnp.int32, (r, r), 0)
    ci = lax.broadcasted_iota(jnp.int32, (r, r), 1)
    same = (ri // c) == (ci // c)
    causal = same & (ri >= ci)
    strict = same & (ri > ci)
    upper = same & (ri <= ci)
    chunk_of_row = lax.broadcasted_iota(jnp.int32, (r, 1), 0) // c
    each = lambda f, *ls: [f(*xs) for xs in zip(*ls)]
    q, k, v, z, beta, g_col, g_row, st = [list(x) for x in zip(*args)]
    q = each(lambda x: x * lax.rsqrt(jnp.sum(x * x, -1, keepdims=True) + NORM_EPS) * (GDN_DK ** -0.5), q)
    k = each(lambda x: x * lax.rsqrt(jnp.sum(x * x, -1, keepdims=True) + NORM_EPS), k)
    gc_col = each(lambda g: jnp.sum(jnp.where(causal, g, 0.0), axis=1, keepdims=True), g_row)
    gc_row = each(lambda g: jnp.sum(jnp.where(upper, g, 0.0), axis=0, keepdims=True), g_col)
    decay = each(lambda gc, gr: jnp.exp(jnp.where(causal, gc - gr, -jnp.inf)), gc_col, gc_row)
    kb = each(lambda x, bt: x * bt, k, beta)
    vb = each(lambda x, bt: x * bt, v, beta)
    kk = each(lambda x, y: _bdot(x, y, NT), kb, k)
    a_mat = each(lambda m, d: jnp.where(strict, m * d, 0.0), kk, decay)
    e_col = each(jnp.exp, gc_col)
    rhs = each(lambda x, y, e: jnp.concatenate([x, y * e], axis=1), vb, kb, e_col)
    eye = (ri == ci).astype(F32)
    t_mat = each(lambda a: eye - a, a_mat)
    p = a_mat
    for _ in range(5):
        p = each(lambda x: _bdot(x, x), p)
        t_mat = each(lambda tm_, x: tm_ + _bdot(tm_, x), t_mat, p)
    sol = each(lambda tm_, rr: _bdot(tm_, rr), t_mat, rhs)
    qk = each(lambda x, y: _bdot(x, y, NT), q, k)
    qk = each(lambda m, d: jnp.where(causal, m * d, 0.0), qk, decay)
    q_dec = each(lambda x, e: x * e, q, e_col)
    g_last = [each(lambda gc: gc[(j + 1) * c - 1:(j + 1) * c, :], gc_col) for j in range(GDN_GROUP)]

    def last_of_own_chunk(*gl):
        out = gl[-1]
        for j in range(GDN_GROUP - 2, -1, -1):
            out = jnp.where(chunk_of_row == j, gl[j], out)
        return out

    g_end = each(last_of_own_chunk, *g_last)
    k_dec = each(lambda x, ge, gc: x * jnp.exp(ge - gc), k, g_end, gc_col)
    qs_parts, v_parts = [], []
    for j in range(GDN_GROUP):
        rows = slice(j * c, (j + 1) * c)
        ws = each(lambda x, s: _bdot(x[rows, GDN_DV:], s), sol, st)
        qs_parts.append(each(lambda x, s: _bdot(x[rows], s), q_dec, st))
        v_new = each(lambda x, w: x[rows, :GDN_DV] - w, sol, ws)
        v_parts.append(v_new)
        kv = each(lambda x, vn: _bdot(x[rows], vn, TN), k_dec, v_new)
        st = each(lambda s, gl, d: s * jnp.exp(gl) + d, st, g_last[j], kv)
    qs = each(lambda *parts: jnp.concatenate(parts, axis=0), *qs_parts)
    v_all = each(lambda *parts: jnp.concatenate(parts, axis=0), *v_parts)
    o = each(lambda a, m, vn: a + _bdot(m, vn), qs, qk, v_all)
    o = each(lambda x: x * lax.rsqrt(jnp.mean(x * x, -1, keepdims=True) + NORM_EPS) * norm_w, o)
    o = each(lambda x, zz: x * _silu(zz), o, z)
    return list(zip(o, st))


def _mixer_kernel(xc_ref, gate_ref, q_ref, k_ref, v_ref, z_ref, sm_ref, smt_ref,
                  wg_ref, bg_ref, lam_ref, og_ref, alr_ref, dtr_ref, alc_ref, dtc_ref, nw_ref,
                  ylru_ref, y_ref, hcarry, state):
    n = pl.program_id(1)
    c = GDN_GROUP * GDN_CHUNK
    nb = q_ref.shape[0]
    first = n == 0

    @pl.when(first)
    def _():
        state[...] = jnp.zeros_like(state)
        hcarry[...] = jnp.zeros_like(hcarry)

    lru_out = [_lru_tile(xc_ref[b], gate_ref[b], wg_ref[...], bg_ref[...], lam_ref[...], og_ref[...], hcarry[b])
               for b in range(nb)]
    norm_w = nw_ref[...]

    args = []
    for b in range(nb):
        q_all = _silu(q_ref[b])
        k_all = _silu(k_ref[b])
        v_all = _silu(v_ref[b])
        z_all = z_ref[b]
        sm = sm_ref[b]
        beta_all = _sigmoid(sm)
        g_cols = -jnp.exp(alr_ref[...]) * _softplus(sm + dtr_ref[...])
        g_rows = -jnp.exp(alc_ref[...]) * _softplus(smt_ref[b] + dtc_ref[...])
        for hd in range(GDN_HEADS):
            sl = slice(hd * GDN_DK, (hd + 1) * GDN_DK)
            args.append((q_all[:, sl], k_all[:, sl], v_all[:, sl], z_all[:, sl],
                         beta_all[:, hd:hd + 1],
                         g_cols[:, GDN_HEADS + hd:GDN_HEADS + hd + 1],
                         g_rows[GDN_HEADS + hd:GDN_HEADS + hd + 1, :],
                         state[b, hd]))
    outs = _gdn_heads(args, norm_w)
    for b in range(nb):
        for hd in range(GDN_HEADS):
            o, st_new = outs[b * GDN_HEADS + hd]
            state[b, hd] = st_new
            y_ref[b, :, hd * GDN_DK:(hd + 1) * GDN_DK] = o
    for b, (y_lru, carry) in enumerate(lru_out):
        ylru_ref[b] = y_lru
        hcarry[b] = carry


def _mixers(proj3, small3, smallt3, w_gates, b_gates, lam, out_g, alr, dtr, alc, dtc, norm_w, nb):
    bsz, seq, _ = proj3.shape
    c = GDN_GROUP * GDN_CHUNK
    nch = seq // c
    col = lambda j: pl.BlockSpec((nb, c, GDN_QK), lambda b, n: (b, n, j))
    const = lambda shape: pl.BlockSpec(shape, lambda b, n: (0,) * len(shape))
    return pl.pallas_call(
        _mixer_kernel,
        grid=(bsz // nb, nch),
        in_specs=[
            col(0), col(1), col(2), col(3), col(4), col(5),
            pl.BlockSpec((nb, c, LANES), lambda b, n: (b, n, 0)),
            pl.BlockSpec((nb, None, 8, c), lambda b, n: (b, n, 0, 0)),
            const((LRU_WIDTH, 2 * LRU_WIDTH)), const((1, 2 * LRU_WIDTH)), const((1, LRU_WIDTH)), const((1, LRU_WIDTH)),
            const((1, LANES)), const((1, LANES)), const((8, 1)), const((8, 1)),
            const((1, GDN_DV)),
        ],
        out_specs=[pl.BlockSpec((nb, c, LRU_WIDTH), lambda b, n: (b, n, 0)),
                   pl.BlockSpec((nb, c, GDN_V), lambda b, n: (b, n, 0))],
        out_shape=[jax.ShapeDtypeStruct((bsz, seq, LRU_WIDTH), F32),
                   jax.ShapeDtypeStruct((bsz, seq, GDN_V), F32)],
        scratch_shapes=[
            pltpu.VMEM((nb, 1, LRU_WIDTH), F32),
            pltpu.VMEM((nb, GDN_HEADS, GDN_DK, GDN_DV), F32),
        ],
        compiler_params=_params(("arbitrary", "arbitrary")),
        name="mixers",
    )(proj3, proj3, proj3, proj3, proj3, proj3, small3, smallt3, w_gates, b_gates, lam, out_g,
      alr, dtr, alc, dtc, norm_w)


def _pick_experts(logits, rbias):
    n = logits.shape[1]
    scores = _sigmoid(logits)
    choice = scores + rbias
    neg = -jnp.inf
    gs_rows = []
    sub = lax.broadcasted_iota(jnp.int32, (GROUP_SIZE, n), 0).astype(F32)
    for g in range(N_GROUPS):
        cg = choice[g * GROUP_SIZE:(g + 1) * GROUP_SIZE, :]
        m1 = jnp.max(cg, axis=0, keepdims=True)
        i1 = jnp.min(jnp.where(cg == m1, sub, float(GROUP_SIZE)), axis=0, keepdims=True)
        m2 = jnp.max(jnp.where(sub == i1, neg, cg), axis=0, keepdims=True)
        gs_rows.append(m1 + m2)
    gs = jnp.concatenate(gs_rows, axis=0)
    gi = lax.broadcasted_iota(jnp.int32, (N_GROUPS, n), 0).astype(F32)
    gsel = jnp.zeros((N_GROUPS, n), jnp.bool_)
    for _ in range(TOPK_GROUPS):
        m = jnp.max(gs, axis=0, keepdims=True)
        idx = jnp.min(jnp.where(gs == m, gi, float(N_GROUPS)), axis=0, keepdims=True)
        hit = gi == idx
        gsel = jnp.logical_or(gsel, hit)
        gs = jnp.where(hit, neg, gs)
    masked = jnp.concatenate(
        [jnp.where(gsel[g:g + 1, :], choice[g * GROUP_SIZE:(g + 1) * GROUP_SIZE, :], neg)
         for g in range(N_GROUPS)], axis=0)
    ei = lax.broadcasted_iota(jnp.int32, (N_EXPERTS, n), 0).astype(F32)
    hits, e_rows, w_rows = [], [], []
    multi = jnp.zeros((N_EXPERTS, n), F32)
    for _ in range(TOP_K):
        m = jnp.max(masked, axis=0, keepdims=True)
        idx = jnp.min(jnp.where(masked == m, ei, float(N_EXPERTS)), axis=0, keepdims=True)
        hit = ei == idx
        hits.append(hit)
        e_rows.append(idx)
        w_rows.append(jnp.sum(jnp.where(hit, scores, 0.0), axis=0, keepdims=True))
        multi = multi + hit.astype(F32)
        masked = jnp.where(hit, neg, masked)
    wts = jnp.concatenate(w_rows, axis=0)
    wts = wts / (jnp.sum(wts, axis=0, keepdims=True) + 1e-20) * ROUTED_SCALE
    return jnp.concatenate(e_rows, axis=0), wts, hits, multi


def _router_kernel(yl_ref, yg_ref, h0_ref, wo1_ref, wo2_ref, g_ref, b_ref, wrt_ref, rb_ref,
                   h1_ref, h1p_ref, e_ref, w_ref, rank_ref, cnt_ref, carry):
    i = pl.program_id(0)
    tm = h0_ref.shape[0]
    n = min(ROUTER_SUB, tm)
    subs = [slice(j * n, (j + 1) * n) for j in range(tm // n)]

    @pl.when(i == 0)
    def _():
        carry[...] = jnp.zeros_like(carry)

    mixes = [_dot(yl_ref[r, :].astype(BF16), wo1_ref[...]) + _dot(yg_ref[r, :].astype(BF16), wo2_ref[...])
             for r in subs]
    h1s = [_layer_norm(DEEPNORM_ALPHA * h0_ref[r, :] + mix, g_ref[...], b_ref[...])
           for r, mix in zip(subs, mixes)]
    for r, h1 in zip(subs, h1s):
        h1_ref[r, :] = h1
        h1p_ref[r, :] = _pack_rows(h1)
    logits = [_dot3(wrt_ref[...], h1, NT) for h1 in h1s]
    picks = [_pick_experts(lg, rb_ref[...]) for lg in logits]
    ti = lax.broadcasted_iota(jnp.int32, (n, n), 0)
    tj = lax.broadcasted_iota(jnp.int32, (n, n), 1)
    before = (ti < tj).astype(BF16)
    cums = [_dot(multi.astype(BF16), before) for _, _, _, multi in picks]
    base = carry[...]
    for r, (e_rows, wts, hits, multi), cum in zip(subs, picks, cums):
        cum = cum + base
        r_rows = [jnp.sum(jnp.where(hit, cum, 0.0), axis=0, keepdims=True) for hit in hits]
        base = base + jnp.sum(multi, axis=1, keepdims=True)
        e_ref[:, r] = e_rows.astype(jnp.int32)
        w_ref[:, r] = wts
        rank_ref[:, r] = jnp.concatenate(r_rows, axis=0).astype(jnp.int32)
    carry[...] = base
    cnt_ref[...] = base.astype(jnp.int32)


def _router(y_lru, y_gdn, h0, wo1, wo2, g, b, w_router_t, rbias, tm):
    t = h0.shape[0]
    const = lambda shape: pl.BlockSpec(shape, lambda i: (0,) * len(shape))
    return pl.pallas_call(
        _router_kernel,
        grid=(t // tm,),
        in_specs=[
            pl.BlockSpec((tm, LRU_WIDTH), lambda i: (i, 0)),
            pl.BlockSpec((tm, GDN_V), lambda i: (i, 0)),
            pl.BlockSpec((tm, D_MODEL), lambda i: (i, 0)),
            const((LRU_WIDTH, D_MODEL)), const((GDN_V, D_MODEL)),
            const((1, D_MODEL)), const((1, D_MODEL)),
            const((N_EXPERTS, D_MODEL)), const((N_EXPERTS, 1)),
        ],
        out_specs=[
            pl.BlockSpec((tm, D_MODEL), lambda i: (i, 0)),
            pl.BlockSpec((tm, D_PACK), lambda i: (i, 0)),
            pl.BlockSpec((TOP_K, tm), lambda i: (0, i)),
            pl.BlockSpec((TOP_K, tm), lambda i: (0, i)),
            pl.BlockSpec((TOP_K, tm), lambda i: (0, i)),
            const((N_EXPERTS, 1)),
        ],
        out_shape=[
            jax.ShapeDtypeStruct((t, D_MODEL), F32),
            jax.ShapeDtypeStruct((t, D_PACK), jnp.uint32),
            jax.ShapeDtypeStruct((TOP_K, t), jnp.int32),
            jax.ShapeDtypeStruct((TOP_K, t), F32),
            jax.ShapeDtypeStruct((TOP_K, t), jnp.int32),
            jax.ShapeDtypeStruct((N_EXPERTS, 1), jnp.int32),
        ],
        scratch_shapes=[pltpu.VMEM((N_EXPERTS, 1), F32)],
        compiler_params=_params(("arbitrary",)),
        name="outproj_router",
    )(y_lru, y_gdn, h0, wo1, wo2, g, b, w_router_t, rbias)


def _dest_kernel(e_ref, r_ref, ps_ref, d_ref):
    tm = e_ref.shape[1]
    ei = lax.broadcasted_iota(jnp.int32, (N_EXPERTS, tm), 0)
    rows = []
    for k in range(TOP_K):
        hit = ei == e_ref[k:k + 1, :]
        rows.append(jnp.sum(jnp.where(hit, ps_ref[...], 0), axis=0, keepdims=True))
    d_ref[...] = jnp.concatenate(rows, axis=0) + r_ref[...]


def _dest(top_e, rank, pad_start, tm):
    t = top_e.shape[1]
    blk = pl.BlockSpec((TOP_K, tm), lambda i: (0, i))
    return pl.pallas_call(
        _dest_kernel,
        grid=(t // tm,),
        in_specs=[blk, blk, pl.BlockSpec((N_EXPERTS, 1), lambda i: (0, 0))],
        out_specs=blk,
        out_shape=jax.ShapeDtypeStruct((TOP_K, t), jnp.int32),
        compiler_params=_params(("arbitrary",)),
        name="moe_dest",
    )(top_e, rank, pad_start)


def _sc_scatter_rows(rows, idx, n_out, chunk):
    n_copies, t = idx.shape
    d = rows.shape[1]
    per_worker = t // SC_WORKERS
    n_chunks = per_worker // chunk
    mesh = plsc.VectorSubcoreMesh(core_axis_name="c", subcore_axis_name="s")
    idx_flat = idx.reshape(n_copies * t)

    @functools.partial(
        pl.kernel, mesh=mesh,
        out_type=jax.ShapeDtypeStruct((n_out, d), rows.dtype),
        scratch_types=[pltpu.VMEM((chunk,), jnp.int32) for _ in range(n_copies)] + [
            pltpu.VMEM((chunk, d), rows.dtype),
            pltpu.SemaphoreType.DMA,
        ],
    )
    def scatter(rows_hbm, idx_hbm, out_hbm, *scratch):
        idx_v = scratch[:n_copies]
        rows_v, sem = scratch[n_copies:]
        wid = lax.axis_index("s") * SC_CORES + lax.axis_index("c")
        base = wid * per_worker

        @pl.loop(0, n_chunks)
        def _(j):
            off = base + j * chunk
            for k in range(n_copies):
                pltpu.sync_copy(idx_hbm.at[pl.ds(k * t + off, chunk)], idx_v[k])
            pltpu.sync_copy(rows_hbm.at[pl.ds(off, chunk)], rows_v)
            copies = [pltpu.async_copy(rows_v, out_hbm.at[idx_v[k]], sem) for k in range(n_copies)]
            for cp in copies:
                cp.wait()

    return scatter(rows, idx_flat)


def _expert_kernel(be_ref, nv_ref, first_ref, slot_ref, next_ref, nu_ref,
                   xs_ref, wg_hbm, wu_hbm, wd_hbm, ys_ref, wg_f, wu_f, wd_f, wgu_b, wd_b, sem):
    i = pl.program_id(0)

    def fetch(e, slot):
        return (pltpu.make_async_copy(wg_hbm.at[e], wg_f.at[slot], sem.at[slot]),
                pltpu.make_async_copy(wu_hbm.at[e], wu_f.at[slot], sem.at[slot]),
                pltpu.make_async_copy(wd_hbm.at[e], wd_f.at[slot], sem.at[slot]))

    @pl.when(i < nu_ref[0])
    def _():
        e = be_ref[i]
        slot = slot_ref[i]

        @pl.when(first_ref[i] == 1)
        def _():
            @pl.when(i == 0)
            def _():
                for cp in fetch(e, slot):
                    cp.start()

            for cp in fetch(e, slot):
                cp.wait()

            @pl.when(next_ref[i] >= 0)
            def _():
                for cp in fetch(next_ref[i], 1 - slot):
                    cp.start()

            wgu_b[:, :D_EXPERT] = wg_f[slot].astype(BF16)
            wgu_b[:, D_EXPERT:] = wu_f[slot].astype(BF16)
            wd_b[...] = wd_f[slot].astype(BF16)

        n = xs_ref.shape[0] // EXPERT_BANDS
        bands = [slice(j * n, (j + 1) * n) for j in range(EXPERT_BANDS)]
        row = lax.broadcasted_iota(jnp.int32, (n, D_PACK), 0)
        xs = [_unpack_rows(jnp.where(row + j * n < nv_ref[i], xs_ref[r, :], jnp.uint32(0)))
              for j, r in enumerate(bands)]
        gus = [_dot(x_hi.astype(BF16), wgu_b[:D_PACK, :]) + _dot(x_lo.astype(BF16), wgu_b[D_PACK:, :])
               for x_hi, x_lo in xs]
        hs = [_silu(gu[:, :D_EXPERT]) * gu[:, D_EXPERT:] for gu in gus]
        ys = [_dot(h.astype(BF16), wd_b[...]) for h in hs]
        for r, y in zip(bands, ys):
            ys_ref[r, :] = _pack_rows(y)


def _experts(blk_e, n_valid, first, slot, next_e, n_used, xs, w_gate, w_up, w_down):
    n_rows = xs.shape[0]
    n_blocks = n_rows // MOE_BLOCK
    blk = lambda i, be, nv, fi, sl, nx, nu: (jnp.minimum(i, nu[0] - 1), 0)
    return pl.pallas_call(
        _expert_kernel,
        grid_spec=pltpu.PrefetchScalarGridSpec(
            num_scalar_prefetch=6,
            grid=(n_blocks,),
            in_specs=[
                pl.BlockSpec((MOE_BLOCK, D_PACK), blk),
                pl.BlockSpec(memory_space=pl.ANY),
                pl.BlockSpec(memory_space=pl.ANY),
                pl.BlockSpec(memory_space=pl.ANY),
            ],
            out_specs=pl.BlockSpec((MOE_BLOCK, D_PACK), blk),
            scratch_shapes=[
                pltpu.VMEM((2, D_MODEL, D_EXPERT), F32),
                pltpu.VMEM((2, D_MODEL, D_EXPERT), F32),
                pltpu.VMEM((2, D_EXPERT, D_MODEL), F32),
                pltpu.VMEM((D_MODEL, 2 * D_EXPERT), BF16),
                pltpu.VMEM((D_EXPERT, D_MODEL), BF16),
                pltpu.SemaphoreType.DMA((2,)),
            ],
        ),
        out_shape=jax.ShapeDtypeStruct((n_rows, D_PACK), jnp.uint32),
        compiler_params=_params(("arbitrary",)),
        name="moe_experts",
    )(blk_e, n_valid, first, slot, next_e, n_used, xs, w_gate, w_up, w_down)


def _sc_gather_rows(table, idx, chunk):
    n_idx = idx.shape[0]
    d = table.shape[1]
    per_worker = n_idx // SC_WORKERS
    n_chunks = per_worker // chunk
    assert n_chunks % 2 == 0 and n_chunks * chunk * SC_WORKERS == n_idx
    mesh = plsc.VectorSubcoreMesh(core_axis_name="c", subcore_axis_name="s")

    @functools.partial(
        pl.kernel, mesh=mesh,
        out_type=jax.ShapeDtypeStruct((n_idx, d), table.dtype),
        scratch_types=[
            pltpu.VMEM((chunk,), jnp.int32), pltpu.VMEM((chunk,), jnp.int32),
            pltpu.VMEM((chunk, d), table.dtype), pltpu.VMEM((chunk, d), table.dtype),
            pltpu.SemaphoreType.DMA, pltpu.SemaphoreType.DMA, pltpu.SemaphoreType.DMA, pltpu.SemaphoreType.DMA,
        ],
    )
    def gather(table_hbm, idx_hbm, out_hbm, idx_v0, idx_v1, rows_v0, rows_v1, gsem0, gsem1, osem0, osem1):
        idx_v, rows_v, gsem, osem = (idx_v0, idx_v1), (rows_v0, rows_v1), (gsem0, gsem1), (osem0, osem1)
        wid = lax.axis_index("s") * SC_CORES + lax.axis_index("c")
        base = wid * per_worker

        def gather_copy(slot):
            return pltpu.make_async_copy(table_hbm.at[idx_v[slot]], rows_v[slot], gsem[slot])

        def out_copy(c, slot):
            return pltpu.make_async_copy(rows_v[slot], out_hbm.at[pl.ds(base + c * chunk, chunk)], osem[slot])

        def start_gather(c, slot):
            pltpu.sync_copy(idx_hbm.at[pl.ds(base + c * chunk, chunk)], idx_v[slot])
            gather_copy(slot).start()

        start_gather(0, 0)

        @pl.loop(0, n_chunks, step=2)
        def _(j):
            for b in range(2):
                c = j + b
                cur, other = b, 1 - b

                @pl.when(c >= 1)
                def _():
                    out_copy(c - 1, other).wait()

                @pl.when(c + 1 < n_chunks)
                def _():
                    start_gather(c + 1, other)

                gather_copy(cur).wait()
                out_copy(c, cur).start()

        out_copy(n_chunks - 1, 1).wait()

    return gather(table, idx)


def _combine_kernel(h1_ref, wts_ref, wsgu_ref, wsd_ref, g_ref, b_ref, yg_ref, out_ref):
    h1 = h1_ref[...]
    gu = _dot(h1.astype(BF16), wsgu_ref[...])
    hs = _silu(gu[:, :D_SHARED]) * gu[:, D_SHARED:]
    acc = DEEPNORM_ALPHA * h1 + _dot(hs.astype(BF16), wsd_ref[...])
    wts = wts_ref[...].T
    acc_hi = acc[:, :D_PACK]
    acc_lo = acc[:, D_PACK:]
    for k in range(TOP_K):
        y_hi, y_lo = _unpack_rows(yg_ref[k])
        acc_hi = acc_hi + y_hi * wts[:, k:k + 1]
        acc_lo = acc_lo + y_lo * wts[:, k:k + 1]
    out_ref[...] = _layer_norm(jnp.concatenate([acc_hi, acc_lo], axis=1), g_ref[...], b_ref[...])


def _combine(h1, wts, ws_gu, ws_down, g, b, yg, tm):
    t = h1.shape[0]
    const = lambda shape: pl.BlockSpec(shape, lambda i: (0,) * len(shape))
    return pl.pallas_call(
        _combine_kernel,
        grid=(t // tm,),
        in_specs=[
            pl.BlockSpec((tm, D_MODEL), lambda i: (i, 0)),
            pl.BlockSpec((TOP_K, tm), lambda i: (0, i)),
            const((D_MODEL, 2 * D_SHARED)), const((D_SHARED, D_MODEL)),
            const((1, D_MODEL)), const((1, D_MODEL)),
            pl.BlockSpec((TOP_K, tm, D_PACK), lambda i: (0, i, 0)),
        ],
        out_specs=pl.BlockSpec((tm, D_MODEL), lambda i: (i, 0)),
        out_shape=jax.ShapeDtypeStruct((t, D_MODEL), F32),
        compiler_params=_params(("arbitrary",)),
        name="moe_combine",
    )(h1, wts, ws_gu, ws_down, g, b, yg)


def _block_diag(w):
    nb, bi, bo = w.shape
    eye = jnp.eye(nb, dtype=w.dtype)
    return (eye[:, None, :, None] * w[:, :, None, :]).reshape(nb * bi, nb * bo)


def _pad_lanes(v, offset, width):
    return jnp.zeros((1, width), F32).at[0, offset:offset + v.shape[0]].set(v)


def _layer(h_in_x, l, p, tiles):
    bsz, seq, _ = h_in_x.shape
    t = bsz * seq
    row = lambda v: v.reshape(1, -1)

    w_in = p['w_in'][l]
    w_main = w_in.astype(BF16)
    w_small = jnp.zeros((D_MODEL, LANES), F32).at[:, :2 * GDN_HEADS].set(w_in[:, N_MAIN:])
    zeros = lambda n: jnp.zeros((CONV_WIDTH, n), F32)
    conv_w = jnp.concatenate([p['lru_conv_w'][l], zeros(LRU_WIDTH), p['gdn_conv_w'][l], zeros(GDN_V)], 1)
    conv_b = jnp.zeros((1, N_MAIN), F32).at[0, :LRU_WIDTH].set(p['lru_conv_b'][l])
    h0, proj, small, small_t = _inproj(h_in_x.reshape(t, D_MODEL), row(p['ln_g']), row(p['ln_b']),
                                       w_main, w_small, conv_w, conv_b, tiles['inproj'], seq)
    proj3 = proj.reshape(bsz, seq, N_MAIN)

    w_gates = jnp.concatenate([_block_diag(p['lru_w_rg'][l]), _block_diag(p['lru_w_ig'][l])], 1).astype(BF16)
    b_gates = jnp.concatenate([p['lru_b_rg'][l], p['lru_b_ig'][l]]).reshape(1, -1)
    rows = GDN_GROUP * GDN_CHUNK
    small3 = small.reshape(bsz, seq, LANES)
    smallt3 = small_t.reshape(8, bsz, seq // rows, rows).transpose(1, 2, 0, 3)
    a_log, dt_bias = p['gdn_a_log'][l], p['gdn_dt_bias'][l]
    alr = _pad_lanes(a_log, GDN_HEADS, LANES)
    dtr = _pad_lanes(dt_bias, GDN_HEADS, LANES)
    alc = _pad_lanes(a_log, GDN_HEADS, 8).reshape(8, 1)
    dtc = _pad_lanes(dt_bias, GDN_HEADS, 8).reshape(8, 1)
    y_lru, y_gdn = _mixers(proj3, small3, smallt3, w_gates, b_gates, row(p['lru_lambda'][l]), row(p['lru_out_g'][l]),
                           alr, dtr, alc, dtc, row(p['gdn_norm_w'][l]), tiles['gdn_nb'])

    w_out = p['w_out'][l].astype(BF16)
    h1, h1p, top_e, wts, rank, counts = _router(
        y_lru.reshape(t, LRU_WIDTH), y_gdn.reshape(t, GDN_V), h0, w_out[:LRU_WIDTH], w_out[LRU_WIDTH:],
        row(p['ln1_g'][l]), row(p['ln1_b'][l]), p['w_router'][l].T, p['router_bias'][l].reshape(-1, 1),
        tiles['router'])

    counts = counts[:, 0]
    padded = (counts + MOE_BLOCK - 1) // MOE_BLOCK * MOE_BLOCK
    pad_end = jnp.cumsum(padded)
    pad_start = pad_end - padded
    n_blocks = (t * TOP_K + N_EXPERTS * (MOE_BLOCK - 1)) // MOE_BLOCK
    n_rows = n_blocks * MOE_BLOCK
    n_used = (pad_end[-1] // MOE_BLOCK).astype(jnp.int32)
    blk_ids = jnp.minimum(jnp.arange(n_blocks, dtype=jnp.int32), n_used - 1)
    blk_e = jnp.minimum(jnp.sum(pad_end[None, :] <= (blk_ids * MOE_BLOCK)[:, None], axis=1),
                        N_EXPERTS - 1).astype(jnp.int32)

    dest = _dest(top_e, rank, pad_start.reshape(-1, 1), tiles['dest'])
    n_valid = jnp.clip(counts[blk_e] - (blk_ids * MOE_BLOCK - pad_start[blk_e]), 0, MOE_BLOCK).astype(jnp.int32)
    xs = _sc_scatter_rows(h1p, dest, n_rows, SC_SCATTER_CHUNK)
    active = jnp.arange(n_blocks, dtype=jnp.int32) < n_used
    first = (active & jnp.concatenate([jnp.ones((1,), bool), blk_e[1:] != blk_e[:-1]])).astype(jnp.int32)
    slot = ((jnp.cumsum(first) - 1) % 2).astype(jnp.int32)
    used = counts > 0
    later = jnp.where(used[None, :] & (jnp.arange(N_EXPERTS)[None, :] > jnp.arange(N_EXPERTS)[:, None]),
                      jnp.arange(N_EXPERTS, dtype=jnp.int32)[None, :], N_EXPERTS)
    next_used = jnp.min(later, axis=1)
    next_e = jnp.where(next_used < N_EXPERTS, next_used, -1)[blk_e].astype(jnp.int32)
    ys = _experts(blk_e, n_valid, first, slot, next_e, n_used.reshape(1), xs,
                  p['w_gate'][l], p['w_up'][l], p['w_down'][l])
    ws_gu = jnp.concatenate([p['ws_gate'][l], p['ws_up'][l]], 1).astype(BF16)
    yg = _sc_gather_rows(ys, dest.reshape(TOP_K * t), SC_GATHER_CHUNK).reshape(TOP_K, t, D_PACK)
    out = _combine(h1, wts, ws_gu, p['ws_down'][l].astype(BF16),
                   row(p['ln2_g'][l]), row(p['ln2_b'][l]), yg, tiles['combine'])
    return out.reshape(bsz, seq, D_MODEL)


def _tiles(bsz, seq):
    t = bsz * seq
    return {
        'inproj': min(512, t),
        'gdn_nb': bsz,
        'router': min(512, t),
        'dest': min(512, t),
        'combine': min(512, t),
    }


def kernel(x, ln_in_g, ln_in_b, w_in, lru_conv_w, lru_conv_b, lru_w_rg, lru_b_rg, lru_w_ig, lru_b_ig,
           lru_lambda, lru_out_g, gdn_conv_w, gdn_a_log, gdn_dt_bias, gdn_norm_w, w_out, ln1_g, ln1_b,
           w_router, router_bias, w_gate, w_up, w_down, ws_gate, ws_up, ws_down, ln2_g, ln2_b):
    assert w_in.shape[0] == DEPTH == 1
    p = dict(ln_g=ln_in_g, ln_b=ln_in_b, w_in=w_in, lru_conv_w=lru_conv_w, lru_conv_b=lru_conv_b,
             lru_w_rg=lru_w_rg, lru_b_rg=lru_b_rg, lru_w_ig=lru_w_ig, lru_b_ig=lru_b_ig,
             lru_lambda=lru_lambda, lru_out_g=lru_out_g, gdn_conv_w=gdn_conv_w, gdn_a_log=gdn_a_log,
             gdn_dt_bias=gdn_dt_bias, gdn_norm_w=gdn_norm_w, w_out=w_out, ln1_g=ln1_g, ln1_b=ln1_b,
             w_router=w_router, router_bias=router_bias, w_gate=w_gate, w_up=w_up, w_down=w_down,
             ws_gate=ws_gate, ws_up=ws_up, ws_down=ws_down, ln2_g=ln2_g, ln2_b=ln2_b)
    bsz, seq, _ = x.shape
    return _layer(x, 0, p, _tiles(bsz, seq))
```

```python
import functools

import jax
import jax.numpy as jnp
from jax import lax
from jax.experimental import pallas as pl
from jax.experimental.pallas import tpu as pltpu
from jax.experimental.pallas import tpu_sc as plsc

F32 = jnp.float32
BF16 = jnp.bfloat16

D_MODEL = 1024
LRU_WIDTH = 512
LRU_C = 8.0
CONV_WIDTH = 4
GDN_HEADS = 4
GDN_DK = 128
GDN_DV = 128
GDN_CHUNK = 64
GDN_GROUP = 2
GDN_QK = GDN_HEADS * GDN_DK
GDN_V = GDN_HEADS * GDN_DV
N_MAIN = 2 * LRU_WIDTH + 2 * GDN_QK + 2 * GDN_V
N_EXPERTS = 256
TOP_K = 8
N_GROUPS = 8
GROUP_SIZE = N_EXPERTS // N_GROUPS
TOPK_GROUPS = 4
D_EXPERT = 256
D_SHARED = 256
ROUTED_SCALE = 2.5
MOE_BLOCK = 640
D_PACK = D_MODEL // 2
LN_EPS = 1e-5
NORM_EPS = 1e-6
DEPTH = 1
DEEPNORM_ALPHA = (2.0 * DEPTH) ** 0.25

SCAN_GROUP = 8
HALO = 8
CONV_GROUP = 512
CONV_GROUPS = (0, 2, 3, 4)
LANES = 128
VMEM_LIMIT = 56 * 1024 * 1024
ROUTER_SUB = 256
EXPERT_BANDS = 5
SC_CORES = 2
SC_WORKERS = 32
SC_GATHER_CHUNK = 64
SC_SCATTER_CHUNK = 128

NN = (((1,), (0,)), ((), ()))
NT = (((1,), (1,)), ((), ()))
TN = (((0,), (0,)), ((), ()))


def _dot(a, b, dims=NN):
    return lax.dot_general(a, b, dims, preferred_element_type=F32)


def _split(a):
    hi = a.astype(BF16)
    lo = (a - hi.astype(F32)).astype(BF16)
    return hi, lo


def _dot3(a, b, dims=NN):
    ah, al = _split(a)
    bh, bl = _split(b)
    return _dot(ah, bh, dims) + (_dot(ah, bl, dims) + _dot(al, bh, dims))


def _layer_norm(x, g, b):
    mu = jnp.mean(x, -1, keepdims=True)
    xc = x - mu
    var = jnp.mean(xc * xc, -1, keepdims=True)
    return xc * lax.rsqrt(var + LN_EPS) * g + b


def _sigmoid(x):
    return 1.0 / (1.0 + jnp.exp(-x))


def _silu(x):
    return x * _sigmoid(x)


def _softplus(x):
    return jnp.maximum(x, 0.0) + jnp.log1p(jnp.exp(-jnp.abs(x)))


def _gelu_tanh(x):
    c = 0.7978845608028654
    return x * (0.5 * (1.0 + jnp.tanh(c * (x + 0.044715 * (x * x * x)))))


def _pack_rows(x):
    hi = lax.bitcast_convert_type(x[:, :D_PACK].astype(BF16).astype(F32), jnp.uint32)
    lo = lax.bitcast_convert_type(x[:, D_PACK:].astype(BF16).astype(F32), jnp.uint32)
    return (hi & jnp.uint32(0xFFFF0000)) | (lo >> 16)


def _unpack_rows(w):
    hi = lax.bitcast_convert_type(w & jnp.uint32(0xFFFF0000), F32)
    lo = lax.bitcast_convert_type(w << 16, F32)
    return hi, lo


def _params(sem, **kw):
    return pltpu.CompilerParams(dimension_semantics=sem, vmem_limit_bytes=VMEM_LIMIT, **kw)


def _inproj_kernel(x_ref, g_ref, b_ref, w_ref, ws_ref, cw_ref, cb_ref,
                   h_ref, proj_ref, small_ref, smallt_ref, hist, *, tiles_per_seq):
    i = pl.program_id(0)
    tm = x_ref.shape[0]
    h = _layer_norm(x_ref[...], g_ref[...], b_ref[...])
    h_ref[...] = h
    hb = h.astype(BF16)

    @pl.when(i % tiles_per_seq == 0)
    def _():
        hist[...] = jnp.zeros_like(hist)

    for g in range(N_MAIN // CONV_GROUP):
        cols = slice(g * CONV_GROUP, (g + 1) * CONV_GROUP)
        p = _dot(hb, w_ref[:, cols])
        if g in CONV_GROUPS:
            xcat = jnp.concatenate([hist[:, cols], p], axis=0)
            acc = cb_ref[:, cols]
            for j in range(CONV_WIDTH):
                off = HALO - (CONV_WIDTH - 1) + j
                acc = acc + xcat[off:off + tm, :] * cw_ref[j:j + 1, cols]
            hist[:, cols] = p[tm - HALO:, :]
            p = acc
        proj_ref[:, cols] = p
    small = _dot3(h, ws_ref[...])
    small_ref[...] = small
    smallt_ref[...] = small.T[:smallt_ref.shape[0], :]


def _inproj(x2d, g, b, w_main, w_small, conv_w, conv_b, tm, seq):
    t = x2d.shape[0]
    return pl.pallas_call(
        functools.partial(_inproj_kernel, tiles_per_seq=seq // tm),
        grid=(t // tm,),
        in_specs=[
            pl.BlockSpec((tm, D_MODEL), lambda i: (i, 0)),
            pl.BlockSpec((1, D_MODEL), lambda i: (0, 0)),
            pl.BlockSpec((1, D_MODEL), lambda i: (0, 0)),
            pl.BlockSpec((D_MODEL, N_MAIN), lambda i: (0, 0)),
            pl.BlockSpec((D_MODEL, LANES), lambda i: (0, 0)),
            pl.BlockSpec((CONV_WIDTH, N_MAIN), lambda i: (0, 0)),
            pl.BlockSpec((1, N_MAIN), lambda i: (0, 0)),
        ],
        out_specs=[
            pl.BlockSpec((tm, D_MODEL), lambda i: (i, 0)),
            pl.BlockSpec((tm, N_MAIN), lambda i: (i, 0)),
            pl.BlockSpec((tm, LANES), lambda i: (i, 0)),
            pl.BlockSpec((8, tm), lambda i: (0, i)),
        ],
        out_shape=[
            jax.ShapeDtypeStruct((t, D_MODEL), F32),
            jax.ShapeDtypeStruct((t, N_MAIN), F32),
            jax.ShapeDtypeStruct((t, LANES), F32),
            jax.ShapeDtypeStruct((8, t), F32),
        ],
        scratch_shapes=[pltpu.VMEM((HALO, N_MAIN), F32)],
        compiler_params=_params(("arbitrary",)),
        name="ln_inproj",
    )(x2d, g, b, w_main, w_small, conv_w, conv_b)


def _lru_tile(xc, gate, wg, bg, lam, og, carry):
    rows = xc.shape[0]
    gates = _dot(xc.astype(BF16), wg) + bg
    r = _sigmoid(gates[:, :LRU_WIDTH])
    i = _sigmoid(gates[:, LRU_WIDTH:])
    log_a = (-LRU_C) * r * _softplus(-lam)
    a = jnp.exp(log_a)
    one_minus_a2 = -jnp.tanh(log_a) * (a * a + 1.0)
    mult = jnp.where(one_minus_a2 > 0.0, one_minus_a2 * lax.rsqrt(one_minus_a2), 0.0)
    bv = mult * (i * xc)
    a = a.reshape(rows // SCAN_GROUP, SCAN_GROUP, LRU_WIDTH)
    bv = bv.reshape(rows // SCAN_GROUP, SCAN_GROUP, LRU_WIDTH)
    row_in_group = lax.broadcasted_iota(jnp.int32, a.shape, 1)
    d = 1
    while d < SCAN_GROUP:
        a_sh = jnp.where(row_in_group < d, 1.0, pltpu.roll(a, d, 1))
        b_sh = jnp.where(row_in_group < d, 0.0, pltpu.roll(bv, d, 1))
        bv = a * b_sh + bv
        a = a * a_sh
        d *= 2
    a = a.reshape(rows, LRU_WIDTH)
    bv = bv.reshape(rows, LRU_WIDTH)
    parts = []
    for g in range(rows // SCAN_GROUP):
        grp = slice(g * SCAN_GROUP, (g + 1) * SCAN_GROUP)
        hg = a[grp] * carry + bv[grp]
        carry = hg[SCAN_GROUP - 1:, :]
        parts.append(hg)
    h = jnp.concatenate(parts, axis=0)
    y = h * _gelu_tanh(gate)
    ms = jnp.mean(y * y, -1, keepdims=True)
    return y * lax.rsqrt(ms + NORM_EPS) * og, carry


def _bdot(a, b, dims=NN):
    return _dot(a.astype(BF16), b.astype(BF16), dims)


def _gdn_heads(args, norm_w):
    c = GDN_CHUNK
    r = GDN_GROUP * c
    ri = lax.broadcasted_iota(jnp.int32, (r, r), 0)
    ci = lax.broadcasted_iota(jnp.int32, (r, r), 1)
    same = (ri // c) == (ci // c)
    causal = same & (ri >= ci)
    strict = same & (ri > ci)
    upper = same & (ri <= ci)
    chunk_of_row = lax.broadcasted_iota(jnp.int32, (r, 1), 0) // c
    each = lambda f, *ls: [f(*xs) for xs in zip(*ls)]
    q, k, v, z, beta, g_col, g_row, st = [list(x) for x in zip(*args)]
    q = each(lambda x: x * lax.rsqrt(jnp.sum(x * x, -1, keepdims=True) + NORM_EPS) * (GDN_DK ** -0.5), q)
    k = each(lambda x: x * lax.rsqrt(jnp.sum(x * x, -1, keepdims=True) + NORM_EPS), k)
    gc_col = each(lambda g: jnp.sum(jnp.where(causal, g, 0.0), axis=1, keepdims=True), g_row)
    gc_row = each(lambda g: jnp.sum(jnp.where(upper, g, 0.0), axis=0, keepdims=True), g_col)
    decay = each(lambda gc, gr: jnp.exp(jnp.where(causal, gc - gr, -jnp.inf)), gc_col, gc_row)
    kb = each(lambda x, bt: x * bt, k, beta)
    vb = each(lambda x, bt: x * bt, v, beta)
    kk = each(lambda x, y: _bdot(x, y, NT), kb, k)
    a_mat = each(lambda m, d: jnp.where(strict, m * d, 0.0), kk, decay)
    e_col = each(jnp.exp, gc_col)
    rhs = each(lambda x, y, e: jnp.concatenate([x, y * e], axis=1), vb, kb, e_col)
    eye = (ri == ci).astype(F32)
    t_mat = each(lambda a: eye - a, a_mat)
    p = a_mat
    for _ in range(5):
        p = each(lambda x: _bdot(x, x), p)
        t_mat = each(lambda tm_, x: tm_ + _bdot(tm_, x), t_mat, p)
    sol = each(lambda tm_, rr: _bdot(tm_, rr), t_mat, rhs)
    qk = each(lambda x, y: _bdot(x, y, NT), q, k)
    qk = each(lambda m, d: jnp.where(causal, m * d, 0.0), qk, decay)
    q_dec = each(lambda x, e: x * e, q, e_col)
    g_last = [each(lambda gc: gc[(j + 1) * c - 1:(j + 1) * c, :], gc_col) for j in range(GDN_GROUP)]

    def last_of_own_chunk(*gl):
        out = gl[-1]
        for j in range(GDN_GROUP - 2, -1, -1):
            out = jnp.where(chunk_of_row == j, gl[j], out)
        return out

    g_end = each(last_of_own_chunk, *g_last)
    k_dec = each(lambda x, ge, gc: x * jnp.exp(ge - gc), k, g_end, gc_col)
    qs_parts, v_parts = [], []
    for j in range(GDN_GROUP):
        rows = slice(j * c, (j + 1) * c)
        ws = each(lambda x, s: _bdot(x[rows, GDN_DV:], s), sol, st)
        qs_parts.append(each(lambda x, s: _bdot(x[rows], s), q_dec, st))
        v_new = each(lambda x, w: x[rows, :GDN_DV] - w, sol, ws)
        v_parts.append(v_new)
        kv = each(lambda x, vn: _bdot(x[rows], vn, TN), k_dec, v_new)
        st = each(lambda s, gl, d: s * jnp.exp(gl) + d, st, g_last[j], kv)
    qs = each(lambda *parts: jnp.concatenate(parts, axis=0), *qs_parts)
    v_all = each(lambda *parts: jnp.concatenate(parts, axis=0), *v_parts)
    o = each(lambda a, m, vn: a + _bdot(m, vn), qs, qk, v_all)
    o = each(lambda x: x * lax.rsqrt(jnp.mean(x * x, -1, keepdims=True) + NORM_EPS) * norm_w, o)
    o = each(lambda x, zz: x * _silu(zz), o, z)
    return list(zip(o, st))


def _mixer_kernel(xc_ref, gate_ref, q_ref, k_ref, v_ref, z_ref, sm_ref, smt_ref,
                  wg_ref, bg_ref, lam_ref, og_ref, alr_ref, dtr_ref, alc_ref, dtc_ref, nw_ref,
                  ylru_ref, y_ref, hcarry, state):
    n = pl.program_id(1)
    c = GDN_GROUP * GDN_CHUNK
    nb = q_ref.shape[0]
    first = n == 0

    @pl.when(first)
    def _():
        state[...] = jnp.zeros_like(state)
        hcarry[...] = jnp.zeros_like(hcarry)

    lru_out = [_lru_tile(xc_ref[b], gate_ref[b], wg_ref[...], bg_ref[...], lam_ref[...], og_ref[...], hcarry[b])
               for b in range(nb)]
    norm_w = nw_ref[...]

    args = []
    for b in range(nb):
        q_all = _silu(q_ref[b])
        k_all = _silu(k_ref[b])
        v_all = _silu(v_ref[b])
        z_all = z_ref[b]
        sm = sm_ref[b]
        beta_all = _sigmoid(sm)
        g_cols = -jnp.exp(alr_ref[...]) * _softplus(sm + dtr_ref[...])
        g_rows = -jnp.exp(alc_ref[...]) * _softplus(smt_ref[b] + dtc_ref[...])
        for hd in range(GDN_HEADS):
            sl = slice(hd * GDN_DK, (hd + 1) * GDN_DK)
            args.append((q_all[:, sl], k_all[:, sl], v_all[:, sl], z_all[:, sl],
                         beta_all[:, hd:hd + 1],
                         g_cols[:, GDN_HEADS + hd:GDN_HEADS + hd + 1],
                         g_rows[GDN_HEADS + hd:GDN_HEADS + hd + 1, :],
                         state[b, hd]))
    outs = _gdn_heads(args, norm_w)
    for b in range(nb):
        for hd in range(GDN_HEADS):
            o, st_new = outs[b * GDN_HEADS + hd]
            state[b, hd] = st_new
            y_ref[b, :, hd * GDN_DK:(hd + 1) * GDN_DK] = o
    for b, (y_lru, carry) in enumerate(lru_out):
        ylru_ref[b] = y_lru
        hcarry[b] = carry


def _mixers(proj3, small3, smallt3, w_gates, b_gates, lam, out_g, alr, dtr, alc, dtc, norm_w, nb):
    bsz, seq, _ = proj3.shape
    c = GDN_GROUP * GDN_CHUNK
    nch = seq // c
    col = lambda j: pl.BlockSpec((nb, c, GDN_QK), lambda b, n: (b, n, j))
    const = lambda shape: pl.BlockSpec(shape, lambda b, n: (0,) * len(shape))
    return pl.pallas_call(
        _mixer_kernel,
        grid=(bsz // nb, nch),
        in_specs=[
            col(0), col(1), col(2), col(3), col(4), col(5),
            pl.BlockSpec((nb, c, LANES), lambda b, n: (b, n, 0)),
            pl.BlockSpec((nb, None, 8, c), lambda b, n: (b, n, 0, 0)),
            const((LRU_WIDTH, 2 * LRU_WIDTH)), const((1, 2 * LRU_WIDTH)), const((1, LRU_WIDTH)), const((1, LRU_WIDTH)),
            const((1, LANES)), const((1, LANES)), const((8, 1)), const((8, 1)),
            const((1, GDN_DV)),
        ],
        out_specs=[pl.BlockSpec((nb, c, LRU_WIDTH), lambda b, n: (b, n, 0)),
                   pl.BlockSpec((nb, c, GDN_V), lambda b, n: (b, n, 0))],
        out_shape=[jax.ShapeDtypeStruct((bsz, seq, LRU_WIDTH), F32),
                   jax.ShapeDtypeStruct((bsz, seq, GDN_V), F32)],
        scratch_shapes=[
            pltpu.VMEM((nb, 1, LRU_WIDTH), F32),
            pltpu.VMEM((nb, GDN_HEADS, GDN_DK, GDN_DV), F32),
        ],
        compiler_params=_params(("arbitrary", "arbitrary")),
        name="mixers",
    )(proj3, proj3, proj3, proj3, proj3, proj3, small3, smallt3, w_gates, b_gates, lam, out_g,
      alr, dtr, alc, dtc, norm_w)


def _pick_experts(logits, rbias):
    n = logits.shape[1]
    scores = _sigmoid(logits)
    choice = scores + rbias
    neg = -jnp.inf
    gs_rows = []
    sub = lax.broadcasted_iota(jnp.int32, (GROUP_SIZE, n), 0).astype(F32)
    for g in range(N_GROUPS):
        cg = choice[g * GROUP_SIZE:(g + 1) * GROUP_SIZE, :]
        m1 = jnp.max(cg, axis=0, keepdims=True)
        i1 = jnp.min(jnp.where(cg == m1, sub, float(GROUP_SIZE)), axis=0, keepdims=True)
        m2 = jnp.max(jnp.where(sub == i1, neg, cg), axis=0, keepdims=True)
        gs_rows.append(m1 + m2)
    gs = jnp.concatenate(gs_rows, axis=0)
    gi = lax.broadcasted_iota(jnp.int32, (N_GROUPS, n), 0).astype(F32)
    gsel = jnp.zeros((N_GROUPS, n), jnp.bool_)
    for _ in range(TOPK_GROUPS):
        m = jnp.max(gs, axis=0, keepdims=True)
        idx = jnp.min(jnp.where(gs == m, gi, float(N_GROUPS)), axis=0, keepdims=True)
        hit = gi == idx
        gsel = jnp.logical_or(gsel, hit)
        gs = jnp.where(hit, neg, gs)
    masked = jnp.concatenate(
        [jnp.where(gsel[g:g + 1, :], choice[g * GROUP_SIZE:(g + 1) * GROUP_SIZE, :], neg)
         for g in range(N_GROUPS)], axis=0)
    ei = lax.broadcasted_iota(jnp.int32, (N_EXPERTS, n), 0).astype(F32)
    hits, e_rows, w_rows = [], [], []
    multi = jnp.zeros((N_EXPERTS, n), F32)
    for _ in range(TOP_K):
        m = jnp.max(masked, axis=0, keepdims=True)
        idx = jnp.min(jnp.where(masked == m, ei, float(N_EXPERTS)), axis=0, keepdims=True)
        hit = ei == idx
        hits.append(hit)
        e_rows.append(idx)
        w_rows.append(jnp.sum(jnp.where(hit, scores, 0.0), axis=0, keepdims=True))
        multi = multi + hit.astype(F32)
        masked = jnp.where(hit, neg, masked)
    wts = jnp.concatenate(w_rows, axis=0)
    wts = wts / (jnp.sum(wts, axis=0, keepdims=True) + 1e-20) * ROUTED_SCALE
    return jnp.concatenate(e_rows, axis=0), wts, hits, multi


def _router_kernel(yl_ref, yg_ref, h0_ref, wo1_ref, wo2_ref, g_ref, b_ref, wrt_ref, rb_ref,
                   h1_ref, h1p_ref, e_ref, w_ref, rank_ref, cnt_ref, carry):
    i = pl.program_id(0)
    tm = h0_ref.shape[0]
    n = min(ROUTER_SUB, tm)
    subs = [slice(j * n, (j + 1) * n) for j in range(tm // n)]

    @pl.when(i == 0)
    def _():
        carry[...] = jnp.zeros_like(carry)

    mixes = [_dot(yl_ref[r, :].astype(BF16), wo1_ref[...]) + _dot(yg_ref[r, :].astype(BF16), wo2_ref[...])
             for r in subs]
    h1s = [_layer_norm(DEEPNORM_ALPHA * h0_ref[r, :] + mix, g_ref[...], b_ref[...])
           for r, mix in zip(subs, mixes)]
    for r, h1 in zip(subs, h1s):
        h1_ref[r, :] = h1
        h1p_ref[r, :] = _pack_rows(h1)
    logits = [_dot3(wrt_ref[...], h1, NT) for h1 in h1s]
    picks = [_pick_experts(lg, rb_ref[...]) for lg in logits]
    ti = lax.broadcasted_iota(jnp.int32, (n, n), 0)
    tj = lax.broadcasted_iota(jnp.int32, (n, n), 1)
    before = (ti < tj).astype(BF16)
    cums = [_dot(multi.astype(BF16), before) for _, _, _, multi in picks]
    base = carry[...]
    for r, (e_rows, wts, hits, multi), cum in zip(subs, picks, cums):
        cum = cum + base
        r_rows = [jnp.sum(jnp.where(hit, cum, 0.0), axis=0, keepdims=True) for hit in hits]
        base = base + jnp.sum(multi, axis=1, keepdims=True)
        e_ref[:, r] = e_rows.astype(jnp.int32)
        w_ref[:, r] = wts
        rank_ref[:, r] = jnp.concatenate(r_rows, axis=0).astype(jnp.int32)
    carry[...] = base
    cnt_ref[...] = base.astype(jnp.int32)


def _router(y_lru, y_gdn, h0, wo1, wo2, g, b, w_router_t, rbias, tm):
    t = h0.shape[0]
    const = lambda shape: pl.BlockSpec(shape, lambda i: (0,) * len(shape))
    return pl.pallas_call(
        _router_kernel,
        grid=(t // tm,),
        in_specs=[
            pl.BlockSpec((tm, LRU_WIDTH), lambda i: (i, 0)),
            pl.BlockSpec((tm, GDN_V), lambda i: (i, 0)),
            pl.BlockSpec((tm, D_MODEL), lambda i: (i, 0)),
            const((LRU_WIDTH, D_MODEL)), const((GDN_V, D_MODEL)),
            const((1, D_MODEL)), const((1, D_MODEL)),
            const((N_EXPERTS, D_MODEL)), const((N_EXPERTS, 1)),
        ],
        out_specs=[
            pl.BlockSpec((tm, D_MODEL), lambda i: (i, 0)),
            pl.BlockSpec((tm, D_PACK), lambda i: (i, 0)),
            pl.BlockSpec((TOP_K, tm), lambda i: (0, i)),
            pl.BlockSpec((TOP_K, tm), lambda i: (0, i)),
            pl.BlockSpec((TOP_K, tm), lambda i: (0, i)),
            const((N_EXPERTS, 1)),
        ],
        out_shape=[
            jax.ShapeDtypeStruct((t, D_MODEL), F32),
            jax.ShapeDtypeStruct((t, D_PACK), jnp.uint32),
            jax.ShapeDtypeStruct((TOP_K, t), jnp.int32),
            jax.ShapeDtypeStruct((TOP_K, t), F32),
            jax.ShapeDtypeStruct((TOP_K, t), jnp.int32),
            jax.ShapeDtypeStruct((N_EXPERTS, 1), jnp.int32),
        ],
        scratch_shapes=[pltpu.VMEM((N_EXPERTS, 1), F32)],
        compiler_params=_params(("arbitrary",)),
        name="outproj_router",
    )(y_lru, y_gdn, h0, wo1, wo2, g, b, w_router_t, rbias)


def _dest_kernel(e_ref, r_ref, ps_ref, d_ref):
    tm = e_ref.shape[1]
    ei = lax.broadcasted_iota(jnp.int32, (N_EXPERTS, tm), 0)
    rows = []
    for k in range(TOP_K):
        hit = ei == e_ref[k:k + 1, :]
        rows.append(jnp.sum(jnp.where(hit, ps_ref[...], 0), axis=0, keepdims=True))
    d_ref[...] = jnp.concatenate(rows, axis=0) + r_ref[...]


def _dest(top_e, rank, pad_start, tm):
    t = top_e.shape[1]
    blk = pl.BlockSpec((TOP_K, tm), lambda i: (0, i))
    return pl.pallas_call(
        _dest_kernel,
        grid=(t // tm,),
        in_specs=[blk, blk, pl.BlockSpec((N_EXPERTS, 1), lambda i: (0, 0))],
        out_specs=blk,
        out_shape=jax.ShapeDtypeStruct((TOP_K, t), jnp.int32),
        compiler_params=_params(("arbitrary",)),
        name="moe_dest",
    )(top_e, rank, pad_start)


def _sc_scatter_rows(rows, idx, n_out, chunk):
    n_copies, t = idx.shape
    d = rows.shape[1]
    per_worker = t // SC_WORKERS
    n_chunks = per_worker // chunk
    mesh = plsc.VectorSubcoreMesh(core_axis_name="c", subcore_axis_name="s")
    idx_flat = idx.reshape(n_copies * t)

    @functools.partial(
        pl.kernel, mesh=mesh,
        out_type=jax.ShapeDtypeStruct((n_out, d), rows.dtype),
        scratch_types=[pltpu.VMEM((chunk,), jnp.int32) for _ in range(n_copies)] + [
            pltpu.VMEM((chunk, d), rows.dtype),
            pltpu.SemaphoreType.DMA,
        ],
    )
    def scatter(rows_hbm, idx_hbm, out_hbm, *scratch):
        idx_v = scratch[:n_copies]
        rows_v, sem = scratch[n_copies:]
        wid = lax.axis_index("s") * SC_CORES + lax.axis_index("c")
        base = wid * per_worker

        @pl.loop(0, n_chunks)
        def _(j):
            off = base + j * chunk
            for k in range(n_copies):
                pltpu.sync_copy(idx_hbm.at[pl.ds(k * t + off, chunk)], idx_v[k])
            pltpu.sync_copy(rows_hbm.at[pl.ds(off, chunk)], rows_v)
            copies = [pltpu.async_copy(rows_v, out_hbm.at[idx_v[k]], sem) for k in range(n_copies)]
            for cp in copies:
                cp.wait()

    return scatter(rows, idx_flat)


def _expert_kernel(be_ref, nv_ref, first_ref, slot_ref, next_ref, nu_ref,
                   xs_ref, wg_hbm, wu_hbm, wd_hbm, ys_ref, wg_f, wu_f, wd_f, wgu_b, wd_b, sem):
    i = pl.program_id(0)

    def fetch(e, slot):
        return (pltpu.make_async_copy(wg_hbm.at[e], wg_f.at[slot], sem.at[slot]),
                pltpu.make_async_copy(wu_hbm.at[e], wu_f.at[slot], sem.at[slot]),
                pltpu.make_async_copy(wd_hbm.at[e], wd_f.at[slot], sem.at[slot]))

    @pl.when(i < nu_ref[0])
    def _():
        e = be_ref[i]
        slot = slot_ref[i]

        @pl.when(first_ref[i] == 1)
        def _():
            @pl.when(i == 0)
            def _():
                for cp in fetch(e, slot):
                    cp.start()

            for cp in fetch(e, slot):
                cp.wait()

            @pl.when(next_ref[i] >= 0)
            def _():
                for cp in fetch(next_ref[i], 1 - slot):
                    cp.start()

            wgu_b[:, :D_EXPERT] = wg_f[slot].astype(BF16)
            wgu_b[:, D_EXPERT:] = wu_f[slot].astype(BF16)
            wd_b[...] = wd_f[slot].astype(BF16)

        n = xs_ref.shape[0] // EXPERT_BANDS
        bands = [slice(j * n, (j + 1) * n) for j in range(EXPERT_BANDS)]
        row = lax.broadcasted_iota(jnp.int32, (n, D_PACK), 0)
        xs = [_unpack_rows(jnp.where(row + j * n < nv_ref[i], xs_ref[r, :], jnp.uint32(0)))
              for j, r in enumerate(bands)]
        gus = [_dot(x_hi.astype(BF16), wgu_b[:D_PACK, :]) + _dot(x_lo.astype(BF16), wgu_b[D_PACK:, :])
               for x_hi, x_lo in xs]
        hs = [_silu(gu[:, :D_EXPERT]) * gu[:, D_EXPERT:] for gu in gus]
        ys = [_dot(h.astype(BF16), wd_b[...]) for h in hs]
        for r, y in zip(bands, ys):
            ys_ref[r, :] = _pack_rows(y)


def _experts(blk_e, n_valid, first, slot, next_e, n_used, xs, w_gate, w_up, w_down):
    n_rows = xs.shape[0]
    n_blocks = n_rows // MOE_BLOCK
    blk = lambda i, be, nv, fi, sl, nx, nu: (jnp.minimum(i, nu[0] - 1), 0)
    return pl.pallas_call(
        _expert_kernel,
        grid_spec=pltpu.PrefetchScalarGridSpec(
            num_scalar_prefetch=6,
            grid=(n_blocks,),
            in_specs=[
                pl.BlockSpec((MOE_BLOCK, D_PACK), blk),
                pl.BlockSpec(memory_space=pl.ANY),
                pl.BlockSpec(memory_space=pl.ANY),
                pl.BlockSpec(memory_space=pl.ANY),
            ],
            out_specs=pl.BlockSpec((MOE_BLOCK, D_PACK), blk),
            scratch_shapes=[
                pltpu.VMEM((2, D_MODEL, D_EXPERT), F32),
                pltpu.VMEM((2, D_MODEL, D_EXPERT), F32),
                pltpu.VMEM((2, D_EXPERT, D_MODEL), F32),
                pltpu.VMEM((D_MODEL, 2 * D_EXPERT), BF16),
                pltpu.VMEM((D_EXPERT, D_MODEL), BF16),
                pltpu.SemaphoreType.DMA((2,)),
            ],
        ),
        out_shape=jax.ShapeDtypeStruct((n_rows, D_PACK), jnp.uint32),
        compiler_params=_params(("arbitrary",)),
        name="moe_experts",
    )(blk_e, n_valid, first, slot, next_e, n_used, xs, w_gate, w_up, w_down)


def _sc_gather_rows(table, idx, chunk):
    n_idx = idx.shape[0]
    d = table.shape[1]
    per_worker = n_idx // SC_WORKERS
    n_chunks = per_worker // chunk
    assert n_chunks % 2 == 0 and n_chunks * chunk * SC_WORKERS == n_idx
    mesh = plsc.VectorSubcoreMesh(core_axis_name="c", subcore_axis_name="s")

    @functools.partial(
        pl.kernel, mesh=mesh,
        out_type=jax.ShapeDtypeStruct((n_idx, d), table.dtype),
        scratch_types=[
            pltpu.VMEM((chunk,), jnp.int32), pltpu.VMEM((chunk,), jnp.int32),
            pltpu.VMEM((chunk, d), table.dtype), pltpu.VMEM((chunk, d), table.dtype),
            pltpu.SemaphoreType.DMA, pltpu.SemaphoreType.DMA, pltpu.SemaphoreType.DMA, pltpu.SemaphoreType.DMA,
        ],
    )
    def gather(table_hbm, idx_hbm, out_hbm, idx_v0, idx_v1, rows_v0, rows_v1, gsem0, gsem1, osem0, osem1):
        idx_v, rows_v, gsem, osem = (idx_v0, idx_v1), (rows_v0, rows_v1), (gsem0, gsem1), (osem0, osem1)
        wid = lax.axis_index("s") * SC_CORES + lax.axis_index("c")
        base = wid * per_worker

        def gather_copy(slot):
            return pltpu.make_async_copy(table_hbm.at[idx_v[slot]], rows_v[slot], gsem[slot])

        def out_copy(c, slot):
            return pltpu.make_async_copy(rows_v[slot], out_hbm.at[pl.ds(base + c * chunk, chunk)], osem[slot])

        def start_gather(c, slot):
            pltpu.sync_copy(idx_hbm.at[pl.ds(base + c * chunk, chunk)], idx_v[slot])
            gather_copy(slot).start()

        start_gather(0, 0)

        @pl.loop(0, n_chunks, step=2)
        def _(j):
            for b in range(2):
                c = j + b
                cur, other = b, 1 - b

                @pl.when(c >= 1)
                def _():
                    out_copy(c - 1, other).wait()

                @pl.when(c + 1 < n_chunks)
                def _():
                    start_gather(c + 1, other)

                gather_copy(cur).wait()
                out_copy(c, cur).start()

        out_copy(n_chunks - 1, 1).wait()

    return gather(table, idx)


def _combine_kernel(h1_ref, wts_ref, wsgu_ref, wsd_ref, g_ref, b_ref, yg_ref, out_ref):
    h1 = h1_ref[...]
    gu = _dot(h1.astype(BF16), wsgu_ref[...])
    hs = _silu(gu[:, :D_SHARED]) * gu[:, D_SHARED:]
    acc = DEEPNORM_ALPHA * h1 + _dot(hs.astype(BF16), wsd_ref[...])
    wts = wts_ref[...].T
    acc_hi = acc[:, :D_PACK]
    acc_lo = acc[:, D_PACK:]
    for k in range(TOP_K):
        y_hi, y_lo = _unpack_rows(yg_ref[k])
        acc_hi = acc_hi + y_hi * wts[:, k:k + 1]
        acc_lo = acc_lo + y_lo * wts[:, k:k + 1]
    out_ref[...] = _layer_norm(jnp.concatenate([acc_hi, acc_lo], axis=1), g_ref[...], b_ref[...])


def _combine(h1, wts, ws_gu, ws_down, g, b, yg, tm):
    t = h1.shape[0]
    const = lambda shape: pl.BlockSpec(shape, lambda i: (0,) * len(shape))
    return pl.pallas_call(
        _combine_kernel,
        grid=(t // tm,),
        in_specs=[
            pl.BlockSpec((tm, D_MODEL), lambda i: (i, 0)),
            pl.BlockSpec((TOP_K, tm), lambda i: (0, i)),
            const((D_MODEL, 2 * D_SHARED)), const((D_SHARED, D_MODEL)),
            const((1, D_MODEL)), const((1, D_MODEL)),
            pl.BlockSpec((TOP_K, tm, D_PACK), lambda i: (0, i, 0)),
        ],
        out_specs=pl.BlockSpec((tm, D_MODEL), lambda i: (i, 0)),
        out_shape=jax.ShapeDtypeStruct((t, D_MODEL), F32),
        compiler_params=_params(("arbitrary",)),
        name="moe_combine",
    )(h1, wts, ws_gu, ws_down, g, b, yg)


def _block_diag(w):
    nb, bi, bo = w.shape
    eye = jnp.eye(nb, dtype=w.dtype)
    return (eye[:, None, :, None] * w[:, :, None, :]).reshape(nb * bi, nb * bo)


def _pad_lanes(v, offset, width):
    return jnp.zeros((1, width), F32).at[0, offset:offset + v.shape[0]].set(v)


def _layer(h_in_x, l, p, tiles):
    bsz, seq, _ = h_in_x.shape
    t = bsz * seq
    row = lambda v: v.reshape(1, -1)

    w_in = p['w_in'][l]
    w_main = w_in.astype(BF16)
    w_small = jnp.zeros((D_MODEL, LANES), F32).at[:, :2 * GDN_HEADS].set(w_in[:, N_MAIN:])
    zeros = lambda n: jnp.zeros((CONV_WIDTH, n), F32)
    conv_w = jnp.concatenate([p['lru_conv_w'][l], zeros(LRU_WIDTH), p['gdn_conv_w'][l], zeros(GDN_V)], 1)
    conv_b = jnp.zeros((1, N_MAIN), F32).at[0, :LRU_WIDTH].set(p['lru_conv_b'][l])
    h0, proj, small, small_t = _inproj(h_in_x.reshape(t, D_MODEL), row(p['ln_g']), row(p['ln_b']),
                                       w_main, w_small, conv_w, conv_b, tiles['inproj'], seq)
    proj3 = proj.reshape(bsz, seq, N_MAIN)

    w_gates = jnp.concatenate([_block_diag(p['lru_w_rg'][l]), _block_diag(p['lru_w_ig'][l])], 1).astype(BF16)
    b_gates = jnp.concatenate([p['lru_b_rg'][l], p['lru_b_ig'][l]]).reshape(1, -1)
    rows = GDN_GROUP * GDN_CHUNK
    small3 = small.reshape(bsz, seq, LANES)
    smallt3 = small_t.reshape(8, bsz, seq // rows, rows).transpose(1, 2, 0, 3)
    a_log, dt_bias = p['gdn_a_log'][l], p['gdn_dt_bias'][l]
    alr = _pad_lanes(a_log, GDN_HEADS, LANES)
    dtr = _pad_lanes(dt_bias, GDN_HEADS, LANES)
    alc = _pad_lanes(a_log, GDN_HEADS, 8).reshape(8, 1)
    dtc = _pad_lanes(dt_bias, GDN_HEADS, 8).reshape(8, 1)
    y_lru, y_gdn = _mixers(proj3, small3, smallt3, w_gates, b_gates, row(p['lru_lambda'][l]), row(p['lru_out_g'][l]),
                           alr, dtr, alc, dtc, row(p['gdn_norm_w'][l]), tiles['gdn_nb'])

    w_out = p['w_out'][l].astype(BF16)
    h1, h1p, top_e, wts, rank, counts = _router(
        y_lru.reshape(t, LRU_WIDTH), y_gdn.reshape(t, GDN_V), h0, w_out[:LRU_WIDTH], w_out[LRU_WIDTH:],
        row(p['ln1_g'][l]), row(p['ln1_b'][l]), p['w_router'][l].T, p['router_bias'][l].reshape(-1, 1),
        tiles['router'])

    counts = counts[:, 0]
    padded = (counts + MOE_BLOCK - 1) // MOE_BLOCK * MOE_BLOCK
    pad_end = jnp.cumsum(padded)
    pad_start = pad_end - padded
    n_blocks = (t * TOP_K + N_EXPERTS * (MOE_BLOCK - 1)) // MOE_BLOCK
    n_rows = n_blocks * MOE_BLOCK
    n_used = (pad_end[-1] // MOE_BLOCK).astype(jnp.int32)
    blk_ids = jnp.minimum(jnp.arange(n_blocks, dtype=jnp.int32), n_used - 1)
    blk_e = jnp.minimum(jnp.sum(pad_end[None, :] <= (blk_ids * MOE_BLOCK)[:, None], axis=1),
                        N_EXPERTS - 1).astype(jnp.int32)

    dest = _dest(top_e, rank, pad_start.reshape(-1, 1), tiles['dest'])
    n_valid = jnp.clip(counts[blk_e] - (blk_ids * MOE_BLOCK - pad_start[blk_e]), 0, MOE_BLOCK).astype(jnp.int32)
    xs = _sc_scatter_rows(h1p, dest, n_rows, SC_SCATTER_CHUNK)
    active = jnp.arange(n_blocks, dtype=jnp.int32) < n_used
    first = (active & jnp.concatenate([jnp.ones((1,), bool), blk_e[1:] != blk_e[:-1]])).astype(jnp.int32)
    slot = ((jnp.cumsum(first) - 1) % 2).astype(jnp.int32)
    used = counts > 0
    later = jnp.where(used[None, :] & (jnp.arange(N_EXPERTS)[None, :] > jnp.arange(N_EXPERTS)[:, None]),
                      jnp.arange(N_EXPERTS, dtype=jnp.int32)[None, :], N_EXPERTS)
    next_used = jnp.min(later, axis=1)
    next_e = jnp.where(next_used < N_EXPERTS, next_used, -1)[blk_e].astype(jnp.int32)
    ys = _experts(blk_e, n_valid, first, slot, next_e, n_used.reshape(1), xs,
                  p['w_gate'][l], p['w_up'][l], p['w_down'][l])
    ws_gu = jnp.concatenate([p['ws_gate'][l], p['ws_up'][l]], 1).astype(BF16)
    yg = _sc_gather_rows(ys, dest.reshape(TOP_K * t), SC_GATHER_CHUNK).reshape(TOP_K, t, D_PACK)
    out = _combine(h1, wts, ws_gu, p['ws_down'][l].astype(BF16),
                   row(p['ln2_g'][l]), row(p['ln2_b'][l]), yg, tiles['combine'])
    return out.reshape(bsz, seq, D_MODEL)


def _tiles(bsz, seq):
    t = bsz * seq
    return {
        'inproj': min(512, t),
        'gdn_nb': bsz,
        'router': min(1024, t),
        'dest': min(512, t),
        'combine': min(512, t),
    }


def kernel(x, ln_in_g, ln_in_b, w_in, lru_conv_w, lru_conv_b, lru_w_rg, lru_b_rg, lru_w_ig, lru_b_ig,
           lru_lambda, lru_out_g, gdn_conv_w, gdn_a_log, gdn_dt_bias, gdn_norm_w, w_out, ln1_g, ln1_b,
           w_router, router_bias, w_gate, w_up, w_down, ws_gate, ws_up, ws_down, ln2_g, ln2_b):
    assert w_in.shape[0] == DEPTH == 1
    p = dict(ln_g=ln_in_g, ln_b=ln_in_b, w_in=w_in, lru_conv_w=lru_conv_w, lru_conv_b=lru_conv_b,
             lru_w_rg=lru_w_rg, lru_b_rg=lru_b_rg, lru_w_ig=lru_w_ig, lru_b_ig=lru_b_ig,
             lru_lambda=lru_lambda, lru_out_g=lru_out_g, gdn_conv_w=gdn_conv_w, gdn_a_log=gdn_a_log,
             gdn_dt_bias=gdn_dt_bias, gdn_norm_w=gdn_norm_w, w_out=w_out, ln1_g=ln1_g, ln1_b=ln1_b,
             w_router=w_router, router_bias=router_bias, w_gate=w_gate, w_up=w_up, w_down=w_down,
             ws_gate=ws_gate, ws_up=ws_up, ws_down=ws_down, ln2_g=ln2_g, ln2_b=ln2_b)
    bsz, seq, _ = x.shape
    return _layer(x, 0, p, _tiles(bsz, seq))
```

```python
import functools

import jax
import jax.numpy as jnp
from jax import lax
from jax.experimental import pallas as pl
from jax.experimental.pallas import tpu as pltpu
from jax.experimental.pallas import tpu_sc as plsc

F32 = jnp.float32
BF16 = jnp.bfloat16

D_MODEL = 1024
LRU_WIDTH = 512
LRU_C = 8.0
CONV_WIDTH = 4
GDN_HEADS = 4
GDN_DK = 128
GDN_DV = 128
GDN_CHUNK = 64
GDN_GROUP = 2
GDN_QK = GDN_HEADS * GDN_DK
GDN_V = GDN_HEADS * GDN_DV
N_MAIN = 2 * LRU_WIDTH + 2 * GDN_QK + 2 * GDN_V
N_EXPERTS = 256
TOP_K = 8
N_GROUPS = 8
GROUP_SIZE = N_EXPERTS // N_GROUPS
TOPK_GROUPS = 4
D_EXPERT = 256
D_SHARED = 256
ROUTED_SCALE = 2.5
MOE_BLOCK = 640
D_PACK = D_MODEL // 2
LN_EPS = 1e-5
NORM_EPS = 1e-6
DEPTH = 1
DEEPNORM_ALPHA = (2.0 * DEPTH) ** 0.25

SCAN_GROUP = 8
HALO = 8
CONV_GROUP = 512
CONV_GROUPS = (0, 2, 3, 4)
LANES = 128
VMEM_LIMIT = 56 * 1024 * 1024
ROUTER_SUB = 256
EXPERT_BANDS = 5
SC_CORES = 2
SC_WORKERS = 32
SC_GATHER_CHUNK = 64
SC_SCATTER_CHUNK = 128

NN = (((1,), (0,)), ((), ()))
NT = (((1,), (1,)), ((), ()))
TN = (((0,), (0,)), ((), ()))


def _dot(a, b, dims=NN):
    return lax.dot_general(a, b, dims, preferred_element_type=F32)


def _split(a):
    hi = a.astype(BF16)
    lo = (a - hi.astype(F32)).astype(BF16)
    return hi, lo


def _dot3(a, b, dims=NN):
    ah, al = _split(a)
    bh, bl = _split(b)
    return _dot(ah, bh, dims) + (_dot(ah, bl, dims) + _dot(al, bh, dims))


def _layer_norm(x, g, b):
    mu = jnp.mean(x, -1, keepdims=True)
    xc = x - mu
    var = jnp.mean(xc * xc, -1, keepdims=True)
    return xc * lax.rsqrt(var + LN_EPS) * g + b


def _sigmoid(x):
    return 0.5 * jnp.tanh(0.5 * x) + 0.5


def _silu(x):
    return x * _sigmoid(x)


def _softplus(x):
    return jnp.maximum(x, 0.0) + jnp.log1p(jnp.exp(-jnp.abs(x)))


def _gelu_tanh(x):
    c = 0.7978845608028654
    return x * (0.5 * (1.0 + jnp.tanh(c * (x + 0.044715 * (x * x * x)))))


def _pack_rows(x):
    hi = lax.bitcast_convert_type(x[:, :D_PACK].astype(BF16).astype(F32), jnp.uint32)
    lo = lax.bitcast_convert_type(x[:, D_PACK:].astype(BF16).astype(F32), jnp.uint32)
    return (hi & jnp.uint32(0xFFFF0000)) | (lo >> 16)


def _unpack_rows(w):
    hi = lax.bitcast_convert_type(w & jnp.uint32(0xFFFF0000), F32)
    lo = lax.bitcast_convert_type(w << 16, F32)
    return hi, lo


def _params(sem, **kw):
    return pltpu.CompilerParams(dimension_semantics=sem, vmem_limit_bytes=VMEM_LIMIT, **kw)


def _inproj_kernel(x_ref, g_ref, b_ref, w_ref, ws_ref, cw_ref, cb_ref,
                   h_ref, proj_ref, smallt_ref, hist, *, tiles_per_seq):
    i = pl.program_id(0)
    tm = x_ref.shape[0]
    h = _layer_norm(x_ref[...], g_ref[...], b_ref[...])
    h_ref[...] = h
    hb = h.astype(BF16)

    @pl.when(i % tiles_per_seq == 0)
    def _():
        hist[...] = jnp.zeros_like(hist)

    for g in range(N_MAIN // CONV_GROUP):
        cols = slice(g * CONV_GROUP, (g + 1) * CONV_GROUP)
        p = _dot(hb, w_ref[:, cols])
        if g in CONV_GROUPS:
            xcat = jnp.concatenate([hist[:, cols], p], axis=0)
            acc = cb_ref[:, cols]
            for j in range(CONV_WIDTH):
                off = HALO - (CONV_WIDTH - 1) + j
                acc = acc + xcat[off:off + tm, :] * cw_ref[j:j + 1, cols]
            hist[:, cols] = p[tm - HALO:, :]
            p = acc
        proj_ref[:, cols] = p
    smallt_ref[...] = _dot3(h, ws_ref[...]).T[:smallt_ref.shape[0], :]


def _inproj(x2d, g, b, w_main, w_small, conv_w, conv_b, tm, seq):
    t = x2d.shape[0]
    return pl.pallas_call(
        functools.partial(_inproj_kernel, tiles_per_seq=seq // tm),
        grid=(t // tm,),
        in_specs=[
            pl.BlockSpec((tm, D_MODEL), lambda i: (i, 0)),
            pl.BlockSpec((1, D_MODEL), lambda i: (0, 0)),
            pl.BlockSpec((1, D_MODEL), lambda i: (0, 0)),
            pl.BlockSpec((D_MODEL, N_MAIN), lambda i: (0, 0)),
            pl.BlockSpec((D_MODEL, LANES), lambda i: (0, 0)),
            pl.BlockSpec((CONV_WIDTH, N_MAIN), lambda i: (0, 0)),
            pl.BlockSpec((1, N_MAIN), lambda i: (0, 0)),
        ],
        out_specs=[
            pl.BlockSpec((tm, D_MODEL), lambda i: (i, 0)),
            pl.BlockSpec((tm, N_MAIN), lambda i: (i, 0)),
            pl.BlockSpec((8, tm), lambda i: (0, i)),
        ],
        out_shape=[
            jax.ShapeDtypeStruct((t, D_MODEL), F32),
            jax.ShapeDtypeStruct((t, N_MAIN), F32),
            jax.ShapeDtypeStruct((8, t), F32),
        ],
        scratch_shapes=[pltpu.VMEM((HALO, N_MAIN), F32)],
        compiler_params=_params(("arbitrary",)),
        name="ln_inproj",
    )(x2d, g, b, w_main, w_small, conv_w, conv_b)


def _lru_tile(xc, gate, wg, bg, lam, og, carry):
    rows = xc.shape[0]
    gates = _dot(xc.astype(BF16), wg) + bg
    r = _sigmoid(gates[:, :LRU_WIDTH])
    i = _sigmoid(gates[:, LRU_WIDTH:])
    log_a = (-LRU_C) * r * _softplus(-lam)
    a = jnp.exp(log_a)
    one_minus_a2 = -jnp.tanh(log_a) * (a * a + 1.0)
    mult = jnp.where(one_minus_a2 > 0.0, one_minus_a2 * lax.rsqrt(one_minus_a2), 0.0)
    bv = mult * (i * xc)
    a = a.reshape(rows // SCAN_GROUP, SCAN_GROUP, LRU_WIDTH)
    bv = bv.reshape(rows // SCAN_GROUP, SCAN_GROUP, LRU_WIDTH)
    row_in_group = lax.broadcasted_iota(jnp.int32, a.shape, 1)
    d = 1
    while d < SCAN_GROUP:
        a_sh = jnp.where(row_in_group < d, 1.0, pltpu.roll(a, d, 1))
        b_sh = jnp.where(row_in_group < d, 0.0, pltpu.roll(bv, d, 1))
        bv = a * b_sh + bv
        a = a * a_sh
        d *= 2
    a = a.reshape(rows, LRU_WIDTH)
    bv = bv.reshape(rows, LRU_WIDTH)
    parts = []
    for g in range(rows // SCAN_GROUP):
        grp = slice(g * SCAN_GROUP, (g + 1) * SCAN_GROUP)
        hg = a[grp] * carry + bv[grp]
        carry = hg[SCAN_GROUP - 1:, :]
        parts.append(hg)
    h = jnp.concatenate(parts, axis=0)
    y = h * _gelu_tanh(gate)
    ms = jnp.mean(y * y, -1, keepdims=True)
    return y * lax.rsqrt(ms + NORM_EPS) * og, carry


def _bdot(a, b, dims=NN):
    return _dot(a.astype(BF16), b.astype(BF16), dims)


def _gdn_heads(args, norm_w):
    c = GDN_CHUNK
    r = GDN_GROUP * c
    ri = lax.broadcasted_iota(jnp.int32, (r, r), 0)
    ci = lax.broadcasted_iota(jnp.int32, (r, r), 1)
    same = (ri // c) == (ci // c)
    causal = same & (ri >= ci)
    strict = same & (ri > ci)
    upper = same & (ri <= ci)
    chunk_of_row = lax.broadcasted_iota(jnp.int32, (r, 1), 0) // c
    each = lambda f, *ls: [f(*xs) for xs in zip(*ls)]
    q, k, v, z, beta, g_col, g_row, st = [list(x) for x in zip(*args)]
    q = each(lambda x: x * lax.rsqrt(jnp.sum(x * x, -1, keepdims=True) + NORM_EPS) * (GDN_DK ** -0.5), q)
    k = each(lambda x: x * lax.rsqrt(jnp.sum(x * x, -1, keepdims=True) + NORM_EPS), k)
    gc_col = each(lambda g: jnp.sum(jnp.where(causal, g, 0.0), axis=1, keepdims=True), g_row)
    gc_row = each(lambda g: jnp.sum(jnp.where(upper, g, 0.0), axis=0, keepdims=True), g_col)
    decay = each(lambda gc, gr: jnp.exp(jnp.where(causal, gc - gr, -jnp.inf)), gc_col, gc_row)
    kb = each(lambda x, bt: x * bt, k, beta)
    vb = each(lambda x, bt: x * bt, v, beta)
    kk = each(lambda x, y: _bdot(x, y, NT), kb, k)
    a_mat = each(lambda m, d: jnp.where(strict, m * d, 0.0), kk, decay)
    e_col = each(jnp.exp, gc_col)
    rhs = each(lambda x, y, e: jnp.concatenate([x, y * e], axis=1), vb, kb, e_col)
    eye = (ri == ci).astype(F32)
    t_mat = each(lambda a: eye - a, a_mat)
    p = a_mat
    for _ in range(5):
        p = each(lambda x: _bdot(x, x), p)
        t_mat = each(lambda tm_, x: tm_ + _bdot(tm_, x), t_mat, p)
    sol = each(lambda tm_, rr: _bdot(tm_, rr), t_mat, rhs)
    qk = each(lambda x, y: _bdot(x, y, NT), q, k)
    qk = each(lambda m, d: jnp.where(causal, m * d, 0.0), qk, decay)
    q_dec = each(lambda x, e: x * e, q, e_col)
    g_last = [each(lambda gc: gc[(j + 1) * c - 1:(j + 1) * c, :], gc_col) for j in range(GDN_GROUP)]

    def last_of_own_chunk(*gl):
        out = gl[-1]
        for j in range(GDN_GROUP - 2, -1, -1):
            out = jnp.where(chunk_of_row == j, gl[j], out)
        return out

    g_end = each(last_of_own_chunk, *g_last)
    k_dec = each(lambda x, ge, gc: x * jnp.exp(ge - gc), k, g_end, gc_col)
    qs_parts, v_parts = [], []
    for j in range(GDN_GROUP):
        rows = slice(j * c, (j + 1) * c)
        ws = each(lambda x, s: _bdot(x[rows, GDN_DV:], s), sol, st)
        qs_parts.append(each(lambda x, s: _bdot(x[rows], s), q_dec, st))
        v_new = each(lambda x, w: x[rows, :GDN_DV] - w, sol, ws)
        v_parts.append(v_new)
        kv = each(lambda x, vn: _bdot(x[rows], vn, TN), k_dec, v_new)
        st = each(lambda s, gl, d: s * jnp.exp(gl) + d, st, g_last[j], kv)
    qs = each(lambda *parts: jnp.concatenate(parts, axis=0), *qs_parts)
    v_all = each(lambda *parts: jnp.concatenate(parts, axis=0), *v_parts)
    o = each(lambda a, m, vn: a + _bdot(m, vn), qs, qk, v_all)
    o = each(lambda x: x * lax.rsqrt(jnp.mean(x * x, -1, keepdims=True) + NORM_EPS) * norm_w, o)
    o = each(lambda x, zz: x * _silu(zz), o, z)
    return list(zip(o, st))


def _mixer_kernel(xc_ref, gate_ref, q_ref, k_ref, v_ref, z_ref, smt_ref,
                  wg_ref, bg_ref, lam_ref, og_ref, alc_ref, dtc_ref, nw_ref,
                  ylru_ref, y_ref, hcarry, state):
    n = pl.program_id(1)
    c = GDN_GROUP * GDN_CHUNK
    nb = q_ref.shape[0]
    first = n == 0

    @pl.when(first)
    def _():
        state[...] = jnp.zeros_like(state)
        hcarry[...] = jnp.zeros_like(hcarry)

    lru_out = [_lru_tile(xc_ref[b], gate_ref[b], wg_ref[...], bg_ref[...], lam_ref[...], og_ref[...], hcarry[b])
               for b in range(nb)]
    norm_w = nw_ref[...]

    args = []
    for b in range(nb):
        q_all = _silu(q_ref[b])
        k_all = _silu(k_ref[b])
        v_all = _silu(v_ref[b])
        z_all = z_ref[b]
        smt = smt_ref[b]
        beta_rows = _sigmoid(smt)
        g_rows = -jnp.exp(alc_ref[...]) * _softplus(smt + dtc_ref[...])
        head_row = lax.broadcasted_iota(jnp.int32, smt.shape, 0)
        stacked = jnp.where(head_row < GDN_HEADS, beta_rows, g_rows)
        cols = jnp.concatenate([stacked, jnp.zeros((LANES - 8, c), F32)], axis=0).T
        for hd in range(GDN_HEADS):
            sl = slice(hd * GDN_DK, (hd + 1) * GDN_DK)
            args.append((q_all[:, sl], k_all[:, sl], v_all[:, sl], z_all[:, sl],
                         cols[:, hd:hd + 1],
                         cols[:, GDN_HEADS + hd:GDN_HEADS + hd + 1],
                         g_rows[GDN_HEADS + hd:GDN_HEADS + hd + 1, :],
                         state[b, hd]))
    outs = _gdn_heads(args, norm_w)
    for b in range(nb):
        for hd in range(GDN_HEADS):
            o, st_new = outs[b * GDN_HEADS + hd]
            state[b, hd] = st_new
            y_ref[b, :, hd * GDN_DK:(hd + 1) * GDN_DK] = o
    for b, (y_lru, carry) in enumerate(lru_out):
        ylru_ref[b] = y_lru
        hcarry[b] = carry


def _mixers(proj3, smallt3, w_gates, b_gates, lam, out_g, alc, dtc, norm_w, nb):
    bsz, seq, _ = proj3.shape
    c = GDN_GROUP * GDN_CHUNK
    nch = seq // c
    col = lambda j: pl.BlockSpec((nb, c, GDN_QK), lambda b, n: (b, n, j))
    const = lambda shape: pl.BlockSpec(shape, lambda b, n: (0,) * len(shape))
    return pl.pallas_call(
        _mixer_kernel,
        grid=(bsz // nb, nch),
        in_specs=[
            col(0), col(1), col(2), col(3), col(4), col(5),
            pl.BlockSpec((nb, None, 8, c), lambda b, n: (b, n, 0, 0)),
            const((LRU_WIDTH, 2 * LRU_WIDTH)), const((1, 2 * LRU_WIDTH)), const((1, LRU_WIDTH)), const((1, LRU_WIDTH)),
            const((8, 1)), const((8, 1)),
            const((1, GDN_DV)),
        ],
        out_specs=[pl.BlockSpec((nb, c, LRU_WIDTH), lambda b, n: (b, n, 0)),
                   pl.BlockSpec((nb, c, GDN_V), lambda b, n: (b, n, 0))],
        out_shape=[jax.ShapeDtypeStruct((bsz, seq, LRU_WIDTH), F32),
                   jax.ShapeDtypeStruct((bsz, seq, GDN_V), F32)],
        scratch_shapes=[
            pltpu.VMEM((nb, 1, LRU_WIDTH), F32),
            pltpu.VMEM((nb, GDN_HEADS, GDN_DK, GDN_DV), F32),
        ],
        compiler_params=_params(("arbitrary", "arbitrary")),
        name="mixers",
    )(proj3, proj3, proj3, proj3, proj3, proj3, smallt3, w_gates, b_gates, lam, out_g, alc, dtc, norm_w)


def _pick_experts(logits, rbias):
    n = logits.shape[1]
    scores = _sigmoid(logits)
    choice = scores + rbias
    neg = -jnp.inf
    gs_rows = []
    sub = lax.broadcasted_iota(jnp.int32, (GROUP_SIZE, n), 0).astype(F32)
    for g in range(N_GROUPS):
        cg = choice[g * GROUP_SIZE:(g + 1) * GROUP_SIZE, :]
        m1 = jnp.max(cg, axis=0, keepdims=True)
        i1 = jnp.min(jnp.where(cg == m1, sub, float(GROUP_SIZE)), axis=0, keepdims=True)
        m2 = jnp.max(jnp.where(sub == i1, neg, cg), axis=0, keepdims=True)
        gs_rows.append(m1 + m2)
    gs = jnp.concatenate(gs_rows, axis=0)
    gi = lax.broadcasted_iota(jnp.int32, (N_GROUPS, n), 0).astype(F32)
    gsel = jnp.zeros((N_GROUPS, n), jnp.bool_)
    for _ in range(TOPK_GROUPS):
        m = jnp.max(gs, axis=0, keepdims=True)
        idx = jnp.min(jnp.where(gs == m, gi, float(N_GROUPS)), axis=0, keepdims=True)
        hit = gi == idx
        gsel = jnp.logical_or(gsel, hit)
        gs = jnp.where(hit, neg, gs)
    masked = jnp.concatenate(
        [jnp.where(gsel[g:g + 1, :], choice[g * GROUP_SIZE:(g + 1) * GROUP_SIZE, :], neg)
         for g in range(N_GROUPS)], axis=0)
    ei = lax.broadcasted_iota(jnp.int32, (N_EXPERTS, n), 0).astype(F32)
    hits, e_rows, w_rows = [], [], []
    multi = jnp.zeros((N_EXPERTS, n), F32)
    for _ in range(TOP_K):
        m = jnp.max(masked, axis=0, keepdims=True)
        idx = jnp.min(jnp.where(masked == m, ei, float(N_EXPERTS)), axis=0, keepdims=True)
        hit = ei == idx
        hits.append(hit)
        e_rows.append(idx)
        w_rows.append(jnp.sum(jnp.where(hit, scores, 0.0), axis=0, keepdims=True))
        multi = multi + hit.astype(F32)
        masked = jnp.where(hit, neg, masked)
    wts = jnp.concatenate(w_rows, axis=0)
    wts = wts / (jnp.sum(wts, axis=0, keepdims=True) + 1e-20) * ROUTED_SCALE
    return jnp.concatenate(e_rows, axis=0), wts, hits, multi


def _router_kernel(yl_ref, yg_ref, h0_ref, wo1_ref, wo2_ref, g_ref, b_ref, wrt_ref, rb_ref,
                   h1_ref, h1p_ref, e_ref, w_ref, rank_ref, cnt_ref, carry):
    i = pl.program_id(0)
    tm = h0_ref.shape[0]
    n = min(ROUTER_SUB, tm)
    subs = [slice(j * n, (j + 1) * n) for j in range(tm // n)]

    @pl.when(i == 0)
    def _():
        carry[...] = jnp.zeros_like(carry)

    mixes = [_dot(yl_ref[r, :].astype(BF16), wo1_ref[...]) + _dot(yg_ref[r, :].astype(BF16), wo2_ref[...])
             for r in subs]
    h1s = [_layer_norm(DEEPNORM_ALPHA * h0_ref[r, :] + mix, g_ref[...], b_ref[...])
           for r, mix in zip(subs, mixes)]
    for r, h1 in zip(subs, h1s):
        h1_ref[r, :] = h1
        h1p_ref[r, :] = _pack_rows(h1)
    logits = [_dot3(wrt_ref[...], h1, NT) for h1 in h1s]
    picks = [_pick_experts(lg, rb_ref[...]) for lg in logits]
    ti = lax.broadcasted_iota(jnp.int32, (n, n), 0)
    tj = lax.broadcasted_iota(jnp.int32, (n, n), 1)
    before = (ti < tj).astype(BF16)
    cums = [_dot(multi.astype(BF16), before) for _, _, _, multi in picks]
    base = carry[...]
    for r, (e_rows, wts, hits, multi), cum in zip(subs, picks, cums):
        cum = cum + base
        r_rows = [jnp.sum(jnp.where(hit, cum, 0.0), axis=0, keepdims=True) for hit in hits]
        base = base + jnp.sum(multi, axis=1, keepdims=True)
        e_ref[:, r] = e_rows.astype(jnp.int32)
        w_ref[:, r] = wts
        rank_ref[:, r] = jnp.concatenate(r_rows, axis=0).astype(jnp.int32)
    carry[...] = base
    cnt_ref[...] = base.astype(jnp.int32)


def _router(y_lru, y_gdn, h0, wo1, wo2, g, b, w_router_t, rbias, tm):
    t = h0.shape[0]
    const = lambda shape: pl.BlockSpec(shape, lambda i: (0,) * len(shape))
    return pl.pallas_call(
        _router_kernel,
        grid=(t // tm,),
        in_specs=[
            pl.BlockSpec((tm, LRU_WIDTH), lambda i: (i, 0)),
            pl.BlockSpec((tm, GDN_V), lambda i: (i, 0)),
            pl.BlockSpec((tm, D_MODEL), lambda i: (i, 0)),
            const((LRU_WIDTH, D_MODEL)), const((GDN_V, D_MODEL)),
            const((1, D_MODEL)), const((1, D_MODEL)),
            const((N_EXPERTS, D_MODEL)), const((N_EXPERTS, 1)),
        ],
        out_specs=[
            pl.BlockSpec((tm, D_MODEL), lambda i: (i, 0)),
            pl.BlockSpec((tm, D_PACK), lambda i: (i, 0)),
            pl.BlockSpec((TOP_K, tm), lambda i: (0, i)),
            pl.BlockSpec((TOP_K, tm), lambda i: (0, i)),
            pl.BlockSpec((TOP_K, tm), lambda i: (0, i)),
            const((N_EXPERTS, 1)),
        ],
        out_shape=[
            jax.ShapeDtypeStruct((t, D_MODEL), F32),
            jax.ShapeDtypeStruct((t, D_PACK), jnp.uint32),
            jax.ShapeDtypeStruct((TOP_K, t), jnp.int32),
            jax.ShapeDtypeStruct((TOP_K, t), F32),
            jax.ShapeDtypeStruct((TOP_K, t), jnp.int32),
            jax.ShapeDtypeStruct((N_EXPERTS, 1), jnp.int32),
        ],
        scratch_shapes=[pltpu.VMEM((N_EXPERTS, 1), F32)],
        compiler_params=_params(("arbitrary",)),
        name="outproj_router",
    )(y_lru, y_gdn, h0, wo1, wo2, g, b, w_router_t, rbias)


def _dest_kernel(e_ref, r_ref, ps_ref, d_ref):
    tm = e_ref.shape[1]
    ei = lax.broadcasted_iota(jnp.int32, (N_EXPERTS, tm), 0)
    rows = []
    for k in range(TOP_K):
        hit = ei == e_ref[k:k + 1, :]
        rows.append(jnp.sum(jnp.where(hit, ps_ref[...], 0), axis=0, keepdims=True))
    d_ref[...] = jnp.concatenate(rows, axis=0) + r_ref[...]


def _dest(top_e, rank, pad_start, tm):
    t = top_e.shape[1]
    blk = pl.BlockSpec((TOP_K, tm), lambda i: (0, i))
    return pl.pallas_call(
        _dest_kernel,
        grid=(t // tm,),
        in_specs=[blk, blk, pl.BlockSpec((N_EXPERTS, 1), lambda i: (0, 0))],
        out_specs=blk,
        out_shape=jax.ShapeDtypeStruct((TOP_K, t), jnp.int32),
        compiler_params=_params(("arbitrary",)),
        name="moe_dest",
    )(top_e, rank, pad_start)


def _sc_scatter_rows(rows, idx, n_out, chunk):
    n_copies, t = idx.shape
    d = rows.shape[1]
    per_worker = t // SC_WORKERS
    n_chunks = per_worker // chunk
    mesh = plsc.VectorSubcoreMesh(core_axis_name="c", subcore_axis_name="s")
    idx_flat = idx.reshape(n_copies * t)

    @functools.partial(
        pl.kernel, mesh=mesh,
        out_type=jax.ShapeDtypeStruct((n_out, d), rows.dtype),
        scratch_types=[pltpu.VMEM((chunk,), jnp.int32) for _ in range(n_copies)] + [
            pltpu.VMEM((chunk, d), rows.dtype),
            pltpu.SemaphoreType.DMA,
        ],
    )
    def scatter(rows_hbm, idx_hbm, out_hbm, *scratch):
        idx_v = scratch[:n_copies]
        rows_v, sem = scratch[n_copies:]
        wid = lax.axis_index("s") * SC_CORES + lax.axis_index("c")
        base = wid * per_worker

        @pl.loop(0, n_chunks)
        def _(j):
            off = base + j * chunk
            for k in range(n_copies):
                pltpu.sync_copy(idx_hbm.at[pl.ds(k * t + off, chunk)], idx_v[k])
            pltpu.sync_copy(rows_hbm.at[pl.ds(off, chunk)], rows_v)
            copies = [pltpu.async_copy(rows_v, out_hbm.at[idx_v[k]], sem) for k in range(n_copies)]
            for cp in copies:
                cp.wait()

    return scatter(rows, idx_flat)


def _expert_kernel(be_ref, nv_ref, first_ref, slot_ref, next_ref, nu_ref,
                   xs_ref, wg_hbm, wu_hbm, wd_hbm, ys_ref, wg_f, wu_f, wd_f, wgu_b, wd_b, sem):
    i = pl.program_id(0)

    def fetch(e, slot):
        return (pltpu.make_async_copy(wg_hbm.at[e], wg_f.at[slot], sem.at[slot]),
                pltpu.make_async_copy(wu_hbm.at[e], wu_f.at[slot], sem.at[slot]),
                pltpu.make_async_copy(wd_hbm.at[e], wd_f.at[slot], sem.at[slot]))

    @pl.when(i < nu_ref[0])
    def _():
        e = be_ref[i]
        slot = slot_ref[i]

        @pl.when(first_ref[i] == 1)
        def _():
            @pl.when(i == 0)
            def _():
                for cp in fetch(e, slot):
                    cp.start()

            for cp in fetch(e, slot):
                cp.wait()

            @pl.when(next_ref[i] >= 0)
            def _():
                for cp in fetch(next_ref[i], 1 - slot):
                    cp.start()

            wgu_b[:, :D_EXPERT] = wg_f[slot].astype(BF16)
            wgu_b[:, D_EXPERT:] = wu_f[slot].astype(BF16)
            wd_b[...] = wd_f[slot].astype(BF16)

        n = xs_ref.shape[0] // EXPERT_BANDS
        bands = [slice(j * n, (j + 1) * n) for j in range(EXPERT_BANDS)]
        row = lax.broadcasted_iota(jnp.int32, (n, D_PACK), 0)
        xs = [_unpack_rows(jnp.where(row + j * n < nv_ref[i], xs_ref[r, :], jnp.uint32(0)))
              for j, r in enumerate(bands)]
        gus = [_dot(x_hi.astype(BF16), wgu_b[:D_PACK, :]) + _dot(x_lo.astype(BF16), wgu_b[D_PACK:, :])
               for x_hi, x_lo in xs]
        hs = [_silu(gu[:, :D_EXPERT]) * gu[:, D_EXPERT:] for gu in gus]
        ys = [_dot(h.astype(BF16), wd_b[...]) for h in hs]
        for r, y in zip(bands, ys):
            ys_ref[r, :] = _pack_rows(y)


def _experts(blk_e, n_valid, first, slot, next_e, n_used, xs, w_gate, w_up, w_down):
    n_rows = xs.shape[0]
    n_blocks = n_rows // MOE_BLOCK
    blk = lambda i, be, nv, fi, sl, nx, nu: (jnp.minimum(i, nu[0] - 1), 0)
    return pl.pallas_call(
        _expert_kernel,
        grid_spec=pltpu.PrefetchScalarGridSpec(
            num_scalar_prefetch=6,
            grid=(n_blocks,),
            in_specs=[
                pl.BlockSpec((MOE_BLOCK, D_PACK), blk),
                pl.BlockSpec(memory_space=pl.ANY),
                pl.BlockSpec(memory_space=pl.ANY),
                pl.BlockSpec(memory_space=pl.ANY),
            ],
            out_specs=pl.BlockSpec((MOE_BLOCK, D_PACK), blk),
            scratch_shapes=[
                pltpu.VMEM((2, D_MODEL, D_EXPERT), F32),
                pltpu.VMEM((2, D_MODEL, D_EXPERT), F32),
                pltpu.VMEM((2, D_EXPERT, D_MODEL), F32),
                pltpu.VMEM((D_MODEL, 2 * D_EXPERT), BF16),
                pltpu.VMEM((D_EXPERT, D_MODEL), BF16),
                pltpu.SemaphoreType.DMA((2,)),
            ],
        ),
        out_shape=jax.ShapeDtypeStruct((n_rows, D_PACK), jnp.uint32),
        compiler_params=_params(("arbitrary",)),
        name="moe_experts",
    )(blk_e, n_valid, first, slot, next_e, n_used, xs, w_gate, w_up, w_down)


def _sc_gather_rows(table, idx, chunk):
    n_idx = idx.shape[0]
    d = table.shape[1]
    per_worker = n_idx // SC_WORKERS
    n_chunks = per_worker // chunk
    assert n_chunks % 2 == 0 and n_chunks * chunk * SC_WORKERS == n_idx
    mesh = plsc.VectorSubcoreMesh(core_axis_name="c", subcore_axis_name="s")

    @functools.partial(
        pl.kernel, mesh=mesh,
        out_type=jax.ShapeDtypeStruct((n_idx, d), table.dtype),
        scratch_types=[
            pltpu.VMEM((chunk,), jnp.int32), pltpu.VMEM((chunk,), jnp.int32),
            pltpu.VMEM((chunk, d), table.dtype), pltpu.VMEM((chunk, d), table.dtype),
            pltpu.SemaphoreType.DMA, pltpu.SemaphoreType.DMA, pltpu.SemaphoreType.DMA, pltpu.SemaphoreType.DMA,
        ],
    )
    def gather(table_hbm, idx_hbm, out_hbm, idx_v0, idx_v1, rows_v0, rows_v1, gsem0, gsem1, osem0, osem1):
        idx_v, rows_v, gsem, osem = (idx_v0, idx_v1), (rows_v0, rows_v1), (gsem0, gsem1), (osem0, osem1)
        wid = lax.axis_index("s") * SC_CORES + lax.axis_index("c")
        base = wid * per_worker

        def gather_copy(slot):
            return pltpu.make_async_copy(table_hbm.at[idx_v[slot]], rows_v[slot], gsem[slot])

        def out_copy(c, slot):
            return pltpu.make_async_copy(rows_v[slot], out_hbm.at[pl.ds(base + c * chunk, chunk)], osem[slot])

        def start_gather(c, slot):
            pltpu.sync_copy(idx_hbm.at[pl.ds(base + c * chunk, chunk)], idx_v[slot])
            gather_copy(slot).start()

        start_gather(0, 0)

        @pl.loop(0, n_chunks, step=2)
        def _(j):
            for b in range(2):
                c = j + b
                cur, other = b, 1 - b

                @pl.when(c >= 1)
                def _():
                    out_copy(c - 1, other).wait()

                @pl.when(c + 1 < n_chunks)
                def _():
                    start_gather(c + 1, other)

                gather_copy(cur).wait()
                out_copy(c, cur).start()

        out_copy(n_chunks - 1, 1).wait()

    return gather(table, idx)


def _combine_kernel(h1_ref, wts_ref, wsgu_ref, wsd_ref, g_ref, b_ref, yg_ref, out_ref):
    h1 = h1_ref[...]
    gu = _dot(h1.astype(BF16), wsgu_ref[...])
    hs = _silu(gu[:, :D_SHARED]) * gu[:, D_SHARED:]
    acc = DEEPNORM_ALPHA * h1 + _dot(hs.astype(BF16), wsd_ref[...])
    wts = wts_ref[...].T
    acc_hi = acc[:, :D_PACK]
    acc_lo = acc[:, D_PACK:]
    for k in range(TOP_K):
        y_hi, y_lo = _unpack_rows(yg_ref[k])
        acc_hi = acc_hi + y_hi * wts[:, k:k + 1]
        acc_lo = acc_lo + y_lo * wts[:, k:k + 1]
    out_ref[...] = _layer_norm(jnp.concatenate([acc_hi, acc_lo], axis=1), g_ref[...], b_ref[...])


def _combine(h1, wts, ws_gu, ws_down, g, b, yg, tm):
    t = h1.shape[0]
    const = lambda shape: pl.BlockSpec(shape, lambda i: (0,) * len(shape))
    return pl.pallas_call(
        _combine_kernel,
        grid=(t // tm,),
        in_specs=[
            pl.BlockSpec((tm, D_MODEL), lambda i: (i, 0)),
            pl.BlockSpec((TOP_K, tm), lambda i: (0, i)),
            const((D_MODEL, 2 * D_SHARED)), const((D_SHARED, D_MODEL)),
            const((1, D_MODEL)), const((1, D_MODEL)),
            pl.BlockSpec((TOP_K, tm, D_PACK), lambda i: (0, i, 0)),
        ],
        out_specs=pl.BlockSpec((tm, D_MODEL), lambda i: (i, 0)),
        out_shape=jax.ShapeDtypeStruct((t, D_MODEL), F32),
        compiler_params=_params(("arbitrary",)),
        name="moe_combine",
    )(h1, wts, ws_gu, ws_down, g, b, yg)


def _block_diag(w):
    nb, bi, bo = w.shape
    eye = jnp.eye(nb, dtype=w.dtype)
    return (eye[:, None, :, None] * w[:, :, None, :]).reshape(nb * bi, nb * bo)


def _pad_lanes(v, offset, width):
    return jnp.zeros((1, width), F32).at[0, offset:offset + v.shape[0]].set(v)


def _layer(h_in_x, l, p, tiles):
    bsz, seq, _ = h_in_x.shape
    t = bsz * seq
    row = lambda v: v.reshape(1, -1)

    w_in = p['w_in'][l]
    w_main = w_in.astype(BF16)
    w_small = jnp.zeros((D_MODEL, LANES), F32).at[:, :2 * GDN_HEADS].set(w_in[:, N_MAIN:])
    zeros = lambda n: jnp.zeros((CONV_WIDTH, n), F32)
    conv_w = jnp.concatenate([p['lru_conv_w'][l], zeros(LRU_WIDTH), p['gdn_conv_w'][l], zeros(GDN_V)], 1)
    conv_b = jnp.zeros((1, N_MAIN), F32).at[0, :LRU_WIDTH].set(p['lru_conv_b'][l])
    h0, proj, small_t = _inproj(h_in_x.reshape(t, D_MODEL), row(p['ln_g']), row(p['ln_b']),
                                       w_main, w_small, conv_w, conv_b, tiles['inproj'], seq)
    proj3 = proj.reshape(bsz, seq, N_MAIN)

    w_gates = jnp.concatenate([_block_diag(p['lru_w_rg'][l]), _block_diag(p['lru_w_ig'][l])], 1).astype(BF16)
    b_gates = jnp.concatenate([p['lru_b_rg'][l], p['lru_b_ig'][l]]).reshape(1, -1)
    rows = GDN_GROUP * GDN_CHUNK
    smallt3 = small_t.reshape(8, bsz, seq // rows, rows).transpose(1, 2, 0, 3)
    a_log, dt_bias = p['gdn_a_log'][l], p['gdn_dt_bias'][l]
    alc = _pad_lanes(a_log, GDN_HEADS, 8).reshape(8, 1)
    dtc = _pad_lanes(dt_bias, GDN_HEADS, 8).reshape(8, 1)
    y_lru, y_gdn = _mixers(proj3, smallt3, w_gates, b_gates, row(p['lru_lambda'][l]), row(p['lru_out_g'][l]),
                           alc, dtc, row(p['gdn_norm_w'][l]), tiles['gdn_nb'])

    w_out = p['w_out'][l].astype(BF16)
    h1, h1p, top_e, wts, rank, counts = _router(
        y_lru.reshape(t, LRU_WIDTH), y_gdn.reshape(t, GDN_V), h0, w_out[:LRU_WIDTH], w_out[LRU_WIDTH:],
        row(p['ln1_g'][l]), row(p['ln1_b'][l]), p['w_router'][l].T, p['router_bias'][l].reshape(-1, 1),
        tiles['router'])

    counts = counts[:, 0]
    padded = (counts + MOE_BLOCK - 1) // MOE_BLOCK * MOE_BLOCK
    pad_end = jnp.cumsum(padded)
    pad_start = pad_end - padded
    n_blocks = (t * TOP_K + N_EXPERTS * (MOE_BLOCK - 1)) // MOE_BLOCK
    n_rows = n_blocks * MOE_BLOCK
    n_used = (pad_end[-1] // MOE_BLOCK).astype(jnp.int32)
    blk_ids = jnp.minimum(jnp.arange(n_blocks, dtype=jnp.int32), n_used - 1)
    blk_e = jnp.minimum(jnp.sum(pad_end[None, :] <= (blk_ids * MOE_BLOCK)[:, None], axis=1),
                        N_EXPERTS - 1).astype(jnp.int32)

    dest = _dest(top_e, rank, pad_start.reshape(-1, 1), tiles['dest'])
    n_valid = jnp.clip(counts[blk_e] - (blk_ids * MOE_BLOCK - pad_start[blk_e]), 0, MOE_BLOCK).astype(jnp.int32)
    xs = _sc_scatter_rows(h1p, dest, n_rows, SC_SCATTER_CHUNK)
    active = jnp.arange(n_blocks, dtype=jnp.int32) < n_used
    first = (active & jnp.concatenate([jnp.ones((1,), bool), blk_e[1:] != blk_e[:-1]])).astype(jnp.int32)
    slot = ((jnp.cumsum(first) - 1) % 2).astype(jnp.int32)
    used = counts > 0
    later = jnp.where(used[None, :] & (jnp.arange(N_EXPERTS)[None, :] > jnp.arange(N_EXPERTS)[:, None]),
                      jnp.arange(N_EXPERTS, dtype=jnp.int32)[None, :], N_EXPERTS)
    next_used = jnp.min(later, axis=1)
    next_e = jnp.where(next_used < N_EXPERTS, next_used, -1)[blk_e].astype(jnp.int32)
    ys = _experts(blk_e, n_valid, first, slot, next_e, n_used.reshape(1), xs,
                  p['w_gate'][l], p['w_up'][l], p['w_down'][l])
    ws_gu = jnp.concatenate([p['ws_gate'][l], p['ws_up'][l]], 1).astype(BF16)
    yg = _sc_gather_rows(ys, dest.reshape(TOP_K * t), SC_GATHER_CHUNK).reshape(TOP_K, t, D_PACK)
    out = _combine(h1, wts, ws_gu, p['ws_down'][l].astype(BF16),
                   row(p['ln2_g'][l]), row(p['ln2_b'][l]), yg, tiles['combine'])
    return out.reshape(bsz, seq, D_MODEL)


def _tiles(bsz, seq):
    t = bsz * seq
    return {
        'inproj': min(512, t),
        'gdn_nb': bsz,
        'router': min(1024, t),
        'dest': min(512, t),
        'combine': min(512, t),
    }


def kernel(x, ln_in_g, ln_in_b, w_in, lru_conv_w, lru_conv_b, lru_w_rg, lru_b_rg, lru_w_ig, lru_b_ig,
           lru_lambda, lru_out_g, gdn_conv_w, gdn_a_log, gdn_dt_bias, gdn_norm_w, w_out, ln1_g, ln1_b,
           w_router, router_bias, w_gate, w_up, w_down, ws_gate, ws_up, ws_down, ln2_g, ln2_b):
    assert w_in.shape[0] == DEPTH == 1
    p = dict(ln_g=ln_in_g, ln_b=ln_in_b, w_in=w_in, lru_conv_w=lru_conv_w, lru_conv_b=lru_conv_b,
             lru_w_rg=lru_w_rg, lru_b_rg=lru_b_rg, lru_w_ig=lru_w_ig, lru_b_ig=lru_b_ig,
             lru_lambda=lru_lambda, lru_out_g=lru_out_g, gdn_conv_w=gdn_conv_w, gdn_a_log=gdn_a_log,
             gdn_dt_bias=gdn_dt_bias, gdn_norm_w=gdn_norm_w, w_out=w_out, ln1_g=ln1_g, ln1_b=ln1_b,
             w_router=w_router, router_bias=router_bias, w_gate=w_gate, w_up=w_up, w_down=w_down,
             ws_gate=ws_gate, ws_up=ws_up, ws_down=ws_down, ln2_g=ln2_g, ln2_b=ln2_b)
    bsz, seq, _ = x.shape
    return _layer(x, 0, p, _tiles(bsz, seq))
```

```python
import functools

import jax
import jax.numpy as jnp
from jax import lax
from jax.experimental import pallas as pl
from jax.experimental.pallas import tpu as pltpu
from jax.experimental.pallas import tpu_sc as plsc

F32 = jnp.float32
BF16 = jnp.bfloat16

D_MODEL = 1024
LRU_WIDTH = 512
LRU_C = 8.0
CONV_WIDTH = 4
GDN_HEADS = 4
GDN_DK = 128
GDN_DV = 128
GDN_CHUNK = 64
GDN_GROUP = 2
GDN_QK = GDN_HEADS * GDN_DK
GDN_V = GDN_HEADS * GDN_DV
N_MAIN = 2 * LRU_WIDTH + 2 * GDN_QK + 2 * GDN_V
N_EXPERTS = 256
TOP_K = 8
N_GROUPS = 8
GROUP_SIZE = N_EXPERTS // N_GROUPS
TOPK_GROUPS = 4
D_EXPERT = 256
D_SHARED = 256
ROUTED_SCALE = 2.5
MOE_BLOCK = 576
D_PACK = D_MODEL // 2
LN_EPS = 1e-5
NORM_EPS = 1e-6
DEPTH = 1
DEEPNORM_ALPHA = (2.0 * DEPTH) ** 0.25

SCAN_GROUP = 8
HALO = 8
CONV_GROUP = 512
CONV_GROUPS = (0, 2, 3, 4)
LANES = 128
VMEM_LIMIT = 56 * 1024 * 1024
ROUTER_SUB = 256
EXPERT_BANDS = 3
SC_CORES = 2
SC_WORKERS = 32
SC_GATHER_CHUNK = 64
SC_SCATTER_CHUNK = 128

NN = (((1,), (0,)), ((), ()))
NT = (((1,), (1,)), ((), ()))
TN = (((0,), (0,)), ((), ()))


def _dot(a, b, dims=NN):
    return lax.dot_general(a, b, dims, preferred_element_type=F32)


def _split(a):
    hi = a.astype(BF16)
    lo = (a - hi.astype(F32)).astype(BF16)
    return hi, lo


def _dot3(a, b, dims=NN):
    ah, al = _split(a)
    bh, bl = _split(b)
    return _dot(ah, bh, dims) + (_dot(ah, bl, dims) + _dot(al, bh, dims))


def _layer_norm(x, g, b):
    mu = jnp.mean(x, -1, keepdims=True)
    xc = x - mu
    var = jnp.mean(xc * xc, -1, keepdims=True)
    return xc * lax.rsqrt(var + LN_EPS) * g + b


def _sigmoid(x):
    return 0.5 * jnp.tanh(0.5 * x) + 0.5


def _silu(x):
    return x * _sigmoid(x)


def _softplus(x):
    return jnp.maximum(x, 0.0) + jnp.log1p(jnp.exp(-jnp.abs(x)))


def _gelu_tanh(x):
    c = 0.7978845608028654
    return x * (0.5 * (1.0 + jnp.tanh(c * (x + 0.044715 * (x * x * x)))))


def _pack_rows(x):
    hi = lax.bitcast_convert_type(x[:, :D_PACK].astype(BF16).astype(F32), jnp.uint32)
    lo = lax.bitcast_convert_type(x[:, D_PACK:].astype(BF16).astype(F32), jnp.uint32)
    return (hi & jnp.uint32(0xFFFF0000)) | (lo >> 16)


def _unpack_rows(w):
    hi = lax.bitcast_convert_type(w & jnp.uint32(0xFFFF0000), F32)
    lo = lax.bitcast_convert_type(w << 16, F32)
    return hi, lo


def _params(sem, **kw):
    return pltpu.CompilerParams(dimension_semantics=sem, vmem_limit_bytes=VMEM_LIMIT, **kw)


def _inproj_kernel(x_ref, g_ref, b_ref, w_ref, ws_ref, cw_ref, cb_ref,
                   h_ref, proj_ref, smallt_ref, hist, *, tiles_per_seq):
    i = pl.program_id(0)
    tm = x_ref.shape[0]
    h = _layer_norm(x_ref[...], g_ref[...], b_ref[...])
    h_ref[...] = h
    hb = h.astype(BF16)

    @pl.when(i % tiles_per_seq == 0)
    def _():
        hist[...] = jnp.zeros_like(hist)

    for g in range(N_MAIN // CONV_GROUP):
        cols = slice(g * CONV_GROUP, (g + 1) * CONV_GROUP)
        p = _dot(hb, w_ref[:, cols])
        if g in CONV_GROUPS:
            xcat = jnp.concatenate([hist[:, cols], p], axis=0)
            acc = cb_ref[:, cols]
            for j in range(CONV_WIDTH):
                off = HALO - (CONV_WIDTH - 1) + j
                acc = acc + xcat[off:off + tm, :] * cw_ref[j:j + 1, cols]
            hist[:, cols] = p[tm - HALO:, :]
            p = acc
        proj_ref[:, cols] = p
    smallt_ref[...] = _dot3(h, ws_ref[...]).T[:smallt_ref.shape[0], :]


def _inproj(x2d, g, b, w_main, w_small, conv_w, conv_b, tm, seq):
    t = x2d.shape[0]
    return pl.pallas_call(
        functools.partial(_inproj_kernel, tiles_per_seq=seq // tm),
        grid=(t // tm,),
        in_specs=[
            pl.BlockSpec((tm, D_MODEL), lambda i: (i, 0)),
            pl.BlockSpec((1, D_MODEL), lambda i: (0, 0)),
            pl.BlockSpec((1, D_MODEL), lambda i: (0, 0)),
            pl.BlockSpec((D_MODEL, N_MAIN), lambda i: (0, 0)),
            pl.BlockSpec((D_MODEL, LANES), lambda i: (0, 0)),
            pl.BlockSpec((CONV_WIDTH, N_MAIN), lambda i: (0, 0)),
            pl.BlockSpec((1, N_MAIN), lambda i: (0, 0)),
        ],
        out_specs=[
            pl.BlockSpec((tm, D_MODEL), lambda i: (i, 0)),
            pl.BlockSpec((tm, N_MAIN), lambda i: (i, 0)),
            pl.BlockSpec((8, tm), lambda i: (0, i)),
        ],
        out_shape=[
            jax.ShapeDtypeStruct((t, D_MODEL), F32),
            jax.ShapeDtypeStruct((t, N_MAIN), F32),
            jax.ShapeDtypeStruct((8, t), F32),
        ],
        scratch_shapes=[pltpu.VMEM((HALO, N_MAIN), F32)],
        compiler_params=_params(("arbitrary",)),
        name="ln_inproj",
    )(x2d, g, b, w_main, w_small, conv_w, conv_b)


def _lru_tile(xc, gate, wg, bg, lam, og, carry):
    rows = xc.shape[0]
    gates = _dot(xc.astype(BF16), wg) + bg
    r = _sigmoid(gates[:, :LRU_WIDTH])
    i = _sigmoid(gates[:, LRU_WIDTH:])
    log_a = (-LRU_C) * r * _softplus(-lam)
    a = jnp.exp(log_a)
    one_minus_a2 = -jnp.tanh(log_a) * (a * a + 1.0)
    mult = jnp.where(one_minus_a2 > 0.0, one_minus_a2 * lax.rsqrt(one_minus_a2), 0.0)
    bv = mult * (i * xc)
    a = a.reshape(rows // SCAN_GROUP, SCAN_GROUP, LRU_WIDTH)
    bv = bv.reshape(rows // SCAN_GROUP, SCAN_GROUP, LRU_WIDTH)
    row_in_group = lax.broadcasted_iota(jnp.int32, a.shape, 1)
    d = 1
    while d < SCAN_GROUP:
        a_sh = jnp.where(row_in_group < d, 1.0, pltpu.roll(a, d, 1))
        b_sh = jnp.where(row_in_group < d, 0.0, pltpu.roll(bv, d, 1))
        bv = a * b_sh + bv
        a = a * a_sh
        d *= 2
    a = a.reshape(rows, LRU_WIDTH)
    bv = bv.reshape(rows, LRU_WIDTH)
    parts = []
    for g in range(rows // SCAN_GROUP):
        grp = slice(g * SCAN_GROUP, (g + 1) * SCAN_GROUP)
        hg = a[grp] * carry + bv[grp]
        carry = hg[SCAN_GROUP - 1:, :]
        parts.append(hg)
    h = jnp.concatenate(parts, axis=0)
    y = h * _gelu_tanh(gate)
    ms = jnp.mean(y * y, -1, keepdims=True)
    return y * lax.rsqrt(ms + NORM_EPS) * og, carry


def _bdot(a, b, dims=NN):
    return _dot(a.astype(BF16), b.astype(BF16), dims)


def _gdn_heads(args, norm_w):
    c = GDN_CHUNK
    r = GDN_GROUP * c
    ri = lax.broadcasted_iota(jnp.int32, (r, r), 0)
    ci = lax.broadcasted_iota(jnp.int32, (r, r), 1)
    same = (ri // c) == (ci // c)
    causal = same & (ri >= ci)
    strict = same & (ri > ci)
    upper = same & (ri <= ci)
    chunk_of_row = lax.broadcasted_iota(jnp.int32, (r, 1), 0) // c
    each = lambda f, *ls: [f(*xs) for xs in zip(*ls)]
    q, k, v, z, beta, g_col, g_row, st = [list(x) for x in zip(*args)]
    q = each(lambda x: x * lax.rsqrt(jnp.sum(x * x, -1, keepdims=True) + NORM_EPS) * (GDN_DK ** -0.5), q)
    k = each(lambda x: x * lax.rsqrt(jnp.sum(x * x, -1, keepdims=True) + NORM_EPS), k)
    gc_col = each(lambda g: jnp.sum(jnp.where(causal, g, 0.0), axis=1, keepdims=True), g_row)
    gc_row = each(lambda g: jnp.sum(jnp.where(upper, g, 0.0), axis=0, keepdims=True), g_col)
    decay = each(lambda gc, gr: jnp.exp(jnp.where(causal, gc - gr, -jnp.inf)), gc_col, gc_row)
    kb = each(lambda x, bt: x * bt, k, beta)
    vb = each(lambda x, bt: x * bt, v, beta)
    kk = each(lambda x, y: _bdot(x, y, NT), kb, k)
    a_mat = each(lambda m, d: jnp.where(strict, m * d, 0.0), kk, decay)
    e_col = each(jnp.exp, gc_col)
    rhs = each(lambda x, y, e: jnp.concatenate([x, y * e], axis=1), vb, kb, e_col)
    eye = (ri == ci).astype(F32)
    t_mat = each(lambda a: eye - a, a_mat)
    p = a_mat
    for _ in range(5):
        p = each(lambda x: _bdot(x, x), p)
        t_mat = each(lambda tm_, x: tm_ + _bdot(tm_, x), t_mat, p)
    sol = each(lambda tm_, rr: _bdot(tm_, rr), t_mat, rhs)
    qk = each(lambda x, y: _bdot(x, y, NT), q, k)
    qk = each(lambda m, d: jnp.where(causal, m * d, 0.0), qk, decay)
    q_dec = each(lambda x, e: x * e, q, e_col)
    g_last = [each(lambda gc: gc[(j + 1) * c - 1:(j + 1) * c, :], gc_col) for j in range(GDN_GROUP)]

    def last_of_own_chunk(*gl):
        out = gl[-1]
        for j in range(GDN_GROUP - 2, -1, -1):
            out = jnp.where(chunk_of_row == j, gl[j], out)
        return out

    g_end = each(last_of_own_chunk, *g_last)
    k_dec = each(lambda x, ge, gc: x * jnp.exp(ge - gc), k, g_end, gc_col)
    qs_parts, v_parts = [], []
    for j in range(GDN_GROUP):
        rows = slice(j * c, (j + 1) * c)
        ws = each(lambda x, s: _bdot(x[rows, GDN_DV:], s), sol, st)
        qs_parts.append(each(lambda x, s: _bdot(x[rows], s), q_dec, st))
        v_new = each(lambda x, w: x[rows, :GDN_DV] - w, sol, ws)
        v_parts.append(v_new)
        kv = each(lambda x, vn: _bdot(x[rows], vn, TN), k_dec, v_new)
        st = each(lambda s, gl, d: s * jnp.exp(gl) + d, st, g_last[j], kv)
    qs = each(lambda *parts: jnp.concatenate(parts, axis=0), *qs_parts)
    v_all = each(lambda *parts: jnp.concatenate(parts, axis=0), *v_parts)
    o = each(lambda a, m, vn: a + _bdot(m, vn), qs, qk, v_all)
    o = each(lambda x: x * lax.rsqrt(jnp.mean(x * x, -1, keepdims=True) + NORM_EPS) * norm_w, o)
    o = each(lambda x, zz: x * _silu(zz), o, z)
    return list(zip(o, st))


def _mixer_kernel(xc_ref, gate_ref, q_ref, k_ref, v_ref, z_ref, smt_ref,
                  wg_ref, bg_ref, lam_ref, og_ref, alc_ref, dtc_ref, nw_ref,
                  ylru_ref, y_ref, hcarry, state):
    n = pl.program_id(1)
    c = GDN_GROUP * GDN_CHUNK
    nb = q_ref.shape[0]
    first = n == 0

    @pl.when(first)
    def _():
        state[...] = jnp.zeros_like(state)
        hcarry[...] = jnp.zeros_like(hcarry)

    lru_out = [_lru_tile(xc_ref[b], gate_ref[b], wg_ref[...], bg_ref[...], lam_ref[...], og_ref[...], hcarry[b])
               for b in range(nb)]
    norm_w = nw_ref[...]

    args = []
    for b in range(nb):
        q_all = _silu(q_ref[b])
        k_all = _silu(k_ref[b])
        v_all = _silu(v_ref[b])
        z_all = z_ref[b]
        smt = smt_ref[b]
        beta_rows = _sigmoid(smt)
        g_rows = -jnp.exp(alc_ref[...]) * _softplus(smt + dtc_ref[...])
        head_row = lax.broadcasted_iota(jnp.int32, smt.shape, 0)
        stacked = jnp.where(head_row < GDN_HEADS, beta_rows, g_rows)
        cols = jnp.concatenate([stacked, jnp.zeros((LANES - 8, c), F32)], axis=0).T
        for hd in range(GDN_HEADS):
            sl = slice(hd * GDN_DK, (hd + 1) * GDN_DK)
            args.append((q_all[:, sl], k_all[:, sl], v_all[:, sl], z_all[:, sl],
                         cols[:, hd:hd + 1],
                         cols[:, GDN_HEADS + hd:GDN_HEADS + hd + 1],
                         g_rows[GDN_HEADS + hd:GDN_HEADS + hd + 1, :],
                         state[b, hd]))
    outs = _gdn_heads(args, norm_w)
    for b in range(nb):
        for hd in range(GDN_HEADS):
            o, st_new = outs[b * GDN_HEADS + hd]
            state[b, hd] = st_new
            y_ref[b, :, hd * GDN_DK:(hd + 1) * GDN_DK] = o
    for b, (y_lru, carry) in enumerate(lru_out):
        ylru_ref[b] = y_lru
        hcarry[b] = carry


def _mixers(proj3, smallt3, w_gates, b_gates, lam, out_g, alc, dtc, norm_w, nb):
    bsz, seq, _ = proj3.shape
    c = GDN_GROUP * GDN_CHUNK
    nch = seq // c
    col = lambda j: pl.BlockSpec((nb, c, GDN_QK), lambda b, n: (b, n, j))
    const = lambda shape: pl.BlockSpec(shape, lambda b, n: (0,) * len(shape))
    return pl.pallas_call(
        _mixer_kernel,
        grid=(bsz // nb, nch),
        in_specs=[
            col(0), col(1), col(2), col(3), col(4), col(5),
            pl.BlockSpec((nb, None, 8, c), lambda b, n: (b, n, 0, 0)),
            const((LRU_WIDTH, 2 * LRU_WIDTH)), const((1, 2 * LRU_WIDTH)), const((1, LRU_WIDTH)), const((1, LRU_WIDTH)),
            const((8, 1)), const((8, 1)),
            const((1, GDN_DV)),
        ],
        out_specs=[pl.BlockSpec((nb, c, LRU_WIDTH), lambda b, n: (b, n, 0)),
                   pl.BlockSpec((nb, c, GDN_V), lambda b, n: (b, n, 0))],
        out_shape=[jax.ShapeDtypeStruct((bsz, seq, LRU_WIDTH), F32),
                   jax.ShapeDtypeStruct((bsz, seq, GDN_V), F32)],
        scratch_shapes=[
            pltpu.VMEM((nb, 1, LRU_WIDTH), F32),
            pltpu.VMEM((nb, GDN_HEADS, GDN_DK, GDN_DV), F32),
        ],
        compiler_params=_params(("arbitrary", "arbitrary")),
        name="mixers",
    )(proj3, proj3, proj3, proj3, proj3, proj3, smallt3, w_gates, b_gates, lam, out_g, alc, dtc, norm_w)


def _pick_experts(logits, rbias):
    n = logits.shape[1]
    scores = _sigmoid(logits)
    choice = scores + rbias
    neg = -jnp.inf
    gs_rows = []
    sub = lax.broadcasted_iota(jnp.int32, (GROUP_SIZE, n), 0).astype(F32)
    for g in range(N_GROUPS):
        cg = choice[g * GROUP_SIZE:(g + 1) * GROUP_SIZE, :]
        m1 = jnp.max(cg, axis=0, keepdims=True)
        i1 = jnp.min(jnp.where(cg == m1, sub, float(GROUP_SIZE)), axis=0, keepdims=True)
        m2 = jnp.max(jnp.where(sub == i1, neg, cg), axis=0, keepdims=True)
        gs_rows.append(m1 + m2)
    gs = jnp.concatenate(gs_rows, axis=0)
    gi = lax.broadcasted_iota(jnp.int32, (N_GROUPS, n), 0).astype(F32)
    gsel = jnp.zeros((N_GROUPS, n), jnp.bool_)
    for _ in range(TOPK_GROUPS):
        m = jnp.max(gs, axis=0, keepdims=True)
        idx = jnp.min(jnp.where(gs == m, gi, float(N_GROUPS)), axis=0, keepdims=True)
        hit = gi == idx
        gsel = jnp.logical_or(gsel, hit)
        gs = jnp.where(hit, neg, gs)
    masked = jnp.concatenate(
        [jnp.where(gsel[g:g + 1, :], choice[g * GROUP_SIZE:(g + 1) * GROUP_SIZE, :], neg)
         for g in range(N_GROUPS)], axis=0)
    ei = lax.broadcasted_iota(jnp.int32, (N_EXPERTS, n), 0).astype(F32)
    hits, e_rows, w_rows = [], [], []
    multi = jnp.zeros((N_EXPERTS, n), F32)
    for _ in range(TOP_K):
        m = jnp.max(masked, axis=0, keepdims=True)
        idx = jnp.min(jnp.where(masked == m, ei, float(N_EXPERTS)), axis=0, keepdims=True)
        hit = ei == idx
        hits.append(hit)
        e_rows.append(idx)
        w_rows.append(jnp.sum(jnp.where(hit, scores, 0.0), axis=0, keepdims=True))
        multi = multi + hit.astype(F32)
        masked = jnp.where(hit, neg, masked)
    wts = jnp.concatenate(w_rows, axis=0)
    wts = wts / (jnp.sum(wts, axis=0, keepdims=True) + 1e-20) * ROUTED_SCALE
    return jnp.concatenate(e_rows, axis=0), wts, hits, multi


def _router_kernel(yl_ref, yg_ref, h0_ref, wo1_ref, wo2_ref, g_ref, b_ref, wrt_ref, rb_ref,
                   h1_ref, h1p_ref, e_ref, w_ref, rank_ref, cnt_ref, carry):
    i = pl.program_id(0)
    tm = h0_ref.shape[0]
    n = min(ROUTER_SUB, tm)
    subs = [slice(j * n, (j + 1) * n) for j in range(tm // n)]

    @pl.when(i == 0)
    def _():
        carry[...] = jnp.zeros_like(carry)

    mixes = [_dot(yl_ref[r, :].astype(BF16), wo1_ref[...]) + _dot(yg_ref[r, :].astype(BF16), wo2_ref[...])
             for r in subs]
    h1s = [_layer_norm(DEEPNORM_ALPHA * h0_ref[r, :] + mix, g_ref[...], b_ref[...])
           for r, mix in zip(subs, mixes)]
    for r, h1 in zip(subs, h1s):
        h1_ref[r, :] = h1
        h1p_ref[r, :] = _pack_rows(h1)
    logits = [_dot3(wrt_ref[...], h1, NT) for h1 in h1s]
    picks = [_pick_experts(lg, rb_ref[...]) for lg in logits]
    ti = lax.broadcasted_iota(jnp.int32, (n, n), 0)
    tj = lax.broadcasted_iota(jnp.int32, (n, n), 1)
    before = (ti < tj).astype(BF16)
    cums = [_dot(multi.astype(BF16), before) for _, _, _, multi in picks]
    base = carry[...]
    for r, (e_rows, wts, hits, multi), cum in zip(subs, picks, cums):
        cum = cum + base
        r_rows = [jnp.sum(jnp.where(hit, cum, 0.0), axis=0, keepdims=True) for hit in hits]
        base = base + jnp.sum(multi, axis=1, keepdims=True)
        e_ref[:, r] = e_rows.astype(jnp.int32)
        w_ref[:, r] = wts
        rank_ref[:, r] = jnp.concatenate(r_rows, axis=0).astype(jnp.int32)
    carry[...] = base
    cnt_ref[...] = base.astype(jnp.int32)


def _router(y_lru, y_gdn, h0, wo1, wo2, g, b, w_router_t, rbias, tm):
    t = h0.shape[0]
    const = lambda shape: pl.BlockSpec(shape, lambda i: (0,) * len(shape))
    return pl.pallas_call(
        _router_kernel,
        grid=(t // tm,),
        in_specs=[
            pl.BlockSpec((tm, LRU_WIDTH), lambda i: (i, 0)),
            pl.BlockSpec((tm, GDN_V), lambda i: (i, 0)),
            pl.BlockSpec((tm, D_MODEL), lambda i: (i, 0)),
            const((LRU_WIDTH, D_MODEL)), const((GDN_V, D_MODEL)),
            const((1, D_MODEL)), const((1, D_MODEL)),
            const((N_EXPERTS, D_MODEL)), const((N_EXPERTS, 1)),
        ],
        out_specs=[
            pl.BlockSpec((tm, D_MODEL), lambda i: (i, 0)),
            pl.BlockSpec((tm, D_PACK), lambda i: (i, 0)),
            pl.BlockSpec((TOP_K, tm), lambda i: (0, i)),
            pl.BlockSpec((TOP_K, tm), lambda i: (0, i)),
            pl.BlockSpec((TOP_K, tm), lambda i: (0, i)),
            const((N_EXPERTS, 1)),
        ],
        out_shape=[
            jax.ShapeDtypeStruct((t, D_MODEL), F32),
            jax.ShapeDtypeStruct((t, D_PACK), jnp.uint32),
            jax.ShapeDtypeStruct((TOP_K, t), jnp.int32),
            jax.ShapeDtypeStruct((TOP_K, t), F32),
            jax.ShapeDtypeStruct((TOP_K, t), jnp.int32),
            jax.ShapeDtypeStruct((N_EXPERTS, 1), jnp.int32),
        ],
        scratch_shapes=[pltpu.VMEM((N_EXPERTS, 1), F32)],
        compiler_params=_params(("arbitrary",)),
        name="outproj_router",
    )(y_lru, y_gdn, h0, wo1, wo2, g, b, w_router_t, rbias)


def _dest_kernel(e_ref, r_ref, ps_ref, d_ref):
    tm = e_ref.shape[1]
    ei = lax.broadcasted_iota(jnp.int32, (N_EXPERTS, tm), 0)
    rows = []
    for k in range(TOP_K):
        hit = ei == e_ref[k:k + 1, :]
        rows.append(jnp.sum(jnp.where(hit, ps_ref[...], 0), axis=0, keepdims=True))
    d_ref[...] = jnp.concatenate(rows, axis=0) + r_ref[...]


def _dest(top_e, rank, pad_start, tm):
    t = top_e.shape[1]
    blk = pl.BlockSpec((TOP_K, tm), lambda i: (0, i))
    return pl.pallas_call(
        _dest_kernel,
        grid=(t // tm,),
        in_specs=[blk, blk, pl.BlockSpec((N_EXPERTS, 1), lambda i: (0, 0))],
        out_specs=blk,
        out_shape=jax.ShapeDtypeStruct((TOP_K, t), jnp.int32),
        compiler_params=_params(("arbitrary",)),
        name="moe_dest",
    )(top_e, rank, pad_start)


def _sc_scatter_rows(rows, idx, n_out, chunk):
    n_copies, t = idx.shape
    d = rows.shape[1]
    per_worker = t // SC_WORKERS
    n_chunks = per_worker // chunk
    mesh = plsc.VectorSubcoreMesh(core_axis_name="c", subcore_axis_name="s")
    idx_flat = idx.reshape(n_copies * t)

    @functools.partial(
        pl.kernel, mesh=mesh,
        out_type=jax.ShapeDtypeStruct((n_out, d), rows.dtype),
        scratch_types=[pltpu.VMEM((chunk,), jnp.int32) for _ in range(n_copies)] + [
            pltpu.VMEM((chunk, d), rows.dtype),
            pltpu.SemaphoreType.DMA,
        ],
    )
    def scatter(rows_hbm, idx_hbm, out_hbm, *scratch):
        idx_v = scratch[:n_copies]
        rows_v, sem = scratch[n_copies:]
        wid = lax.axis_index("s") * SC_CORES + lax.axis_index("c")
        base = wid * per_worker

        @pl.loop(0, n_chunks)
        def _(j):
            off = base + j * chunk
            for k in range(n_copies):
                pltpu.sync_copy(idx_hbm.at[pl.ds(k * t + off, chunk)], idx_v[k])
            pltpu.sync_copy(rows_hbm.at[pl.ds(off, chunk)], rows_v)
            copies = [pltpu.async_copy(rows_v, out_hbm.at[idx_v[k]], sem) for k in range(n_copies)]
            for cp in copies:
                cp.wait()

    return scatter(rows, idx_flat)


def _expert_kernel(be_ref, nv_ref, first_ref, slot_ref, next_ref, nu_ref,
                   xs_ref, wg_hbm, wu_hbm, wd_hbm, ys_ref, wg_f, wu_f, wd_f, wgu_b, wd_b, sem):
    i = pl.program_id(0)

    def fetch(e, slot):
        return (pltpu.make_async_copy(wg_hbm.at[e], wg_f.at[slot], sem.at[slot]),
                pltpu.make_async_copy(wu_hbm.at[e], wu_f.at[slot], sem.at[slot]),
                pltpu.make_async_copy(wd_hbm.at[e], wd_f.at[slot], sem.at[slot]))

    @pl.when(i < nu_ref[0])
    def _():
        e = be_ref[i]
        slot = slot_ref[i]

        @pl.when(first_ref[i] == 1)
        def _():
            @pl.when(i == 0)
            def _():
                for cp in fetch(e, slot):
                    cp.start()

            for cp in fetch(e, slot):
                cp.wait()

            @pl.when(next_ref[i] >= 0)
            def _():
                for cp in fetch(next_ref[i], 1 - slot):
                    cp.start()

            wgu_b[:, :D_EXPERT] = wg_f[slot].astype(BF16)
            wgu_b[:, D_EXPERT:] = wu_f[slot].astype(BF16)
            wd_b[...] = wd_f[slot].astype(BF16)

        n = xs_ref.shape[0] // EXPERT_BANDS
        bands = [slice(j * n, (j + 1) * n) for j in range(EXPERT_BANDS)]
        row = lax.broadcasted_iota(jnp.int32, (n, D_PACK), 0)
        xs = [_unpack_rows(jnp.where(row + j * n < nv_ref[i], xs_ref[r, :], jnp.uint32(0)))
              for j, r in enumerate(bands)]
        gus = [_dot(x_hi.astype(BF16), wgu_b[:D_PACK, :]) + _dot(x_lo.astype(BF16), wgu_b[D_PACK:, :])
               for x_hi, x_lo in xs]
        hs = [_silu(gu[:, :D_EXPERT]) * gu[:, D_EXPERT:] for gu in gus]
        ys = [_dot(h.astype(BF16), wd_b[...]) for h in hs]
        for r, y in zip(bands, ys):
            ys_ref[r, :] = _pack_rows(y)


def _experts(blk_e, n_valid, first, slot, next_e, n_used, xs, w_gate, w_up, w_down):
    n_rows = xs.shape[0]
    n_blocks = n_rows // MOE_BLOCK
    blk = lambda i, be, nv, fi, sl, nx, nu: (jnp.minimum(i, nu[0] - 1), 0)
    return pl.pallas_call(
        _expert_kernel,
        grid_spec=pltpu.PrefetchScalarGridSpec(
            num_scalar_prefetch=6,
            grid=(n_blocks,),
            in_specs=[
                pl.BlockSpec((MOE_BLOCK, D_PACK), blk),
                pl.BlockSpec(memory_space=pl.ANY),
                pl.BlockSpec(memory_space=pl.ANY),
                pl.BlockSpec(memory_space=pl.ANY),
            ],
            out_specs=pl.BlockSpec((MOE_BLOCK, D_PACK), blk),
            scratch_shapes=[
                pltpu.VMEM((2, D_MODEL, D_EXPERT), F32),
                pltpu.VMEM((2, D_MODEL, D_EXPERT), F32),
                pltpu.VMEM((2, D_EXPERT, D_MODEL), F32),
                pltpu.VMEM((D_MODEL, 2 * D_EXPERT), BF16),
                pltpu.VMEM((D_EXPERT, D_MODEL), BF16),
                pltpu.SemaphoreType.DMA((2,)),
            ],
        ),
        out_shape=jax.ShapeDtypeStruct((n_rows, D_PACK), jnp.uint32),
        compiler_params=_params(("arbitrary",)),
        name="moe_experts",
    )(blk_e, n_valid, first, slot, next_e, n_used, xs, w_gate, w_up, w_down)


def _sc_gather_rows(table, idx, chunk):
    n_idx = idx.shape[0]
    d = table.shape[1]
    per_worker = n_idx // SC_WORKERS
    n_chunks = per_worker // chunk
    assert n_chunks % 2 == 0 and n_chunks * chunk * SC_WORKERS == n_idx
    mesh = plsc.VectorSubcoreMesh(core_axis_name="c", subcore_axis_name="s")

    @functools.partial(
        pl.kernel, mesh=mesh,
        out_type=jax.ShapeDtypeStruct((n_idx, d), table.dtype),
        scratch_types=[
            pltpu.VMEM((chunk,), jnp.int32), pltpu.VMEM((chunk,), jnp.int32),
            pltpu.VMEM((chunk, d), table.dtype), pltpu.VMEM((chunk, d), table.dtype),
            pltpu.SemaphoreType.DMA, pltpu.SemaphoreType.DMA, pltpu.SemaphoreType.DMA, pltpu.SemaphoreType.DMA,
        ],
    )
    def gather(table_hbm, idx_hbm, out_hbm, idx_v0, idx_v1, rows_v0, rows_v1, gsem0, gsem1, osem0, osem1):
        idx_v, rows_v, gsem, osem = (idx_v0, idx_v1), (rows_v0, rows_v1), (gsem0, gsem1), (osem0, osem1)
        wid = lax.axis_index("s") * SC_CORES + lax.axis_index("c")
        base = wid * per_worker

        def gather_copy(slot):
            return pltpu.make_async_copy(table_hbm.at[idx_v[slot]], rows_v[slot], gsem[slot])

        def out_copy(c, slot):
            return pltpu.make_async_copy(rows_v[slot], out_hbm.at[pl.ds(base + c * chunk, chunk)], osem[slot])

        def start_gather(c, slot):
            pltpu.sync_copy(idx_hbm.at[pl.ds(base + c * chunk, chunk)], idx_v[slot])
            gather_copy(slot).start()

        start_gather(0, 0)

        @pl.loop(0, n_chunks, step=2)
        def _(j):
            for b in range(2):
                c = j + b
                cur, other = b, 1 - b

                @pl.when(c >= 1)
                def _():
                    out_copy(c - 1, other).wait()

                @pl.when(c + 1 < n_chunks)
                def _():
                    start_gather(c + 1, other)

                gather_copy(cur).wait()
                out_copy(c, cur).start()

        out_copy(n_chunks - 1, 1).wait()

    return gather(table, idx)


def _combine_kernel(h1_ref, wts_ref, wsgu_ref, wsd_ref, g_ref, b_ref, yg_ref, out_ref):
    h1 = h1_ref[...]
    gu = _dot(h1.astype(BF16), wsgu_ref[...])
    hs = _silu(gu[:, :D_SHARED]) * gu[:, D_SHARED:]
    acc = DEEPNORM_ALPHA * h1 + _dot(hs.astype(BF16), wsd_ref[...])
    wts = wts_ref[...].T
    acc_hi = acc[:, :D_PACK]
    acc_lo = acc[:, D_PACK:]
    for k in range(TOP_K):
        y_hi, y_lo = _unpack_rows(yg_ref[k])
        acc_hi = acc_hi + y_hi * wts[:, k:k + 1]
        acc_lo = acc_lo + y_lo * wts[:, k:k + 1]
    out_ref[...] = _layer_norm(jnp.concatenate([acc_hi, acc_lo], axis=1), g_ref[...], b_ref[...])


def _combine(h1, wts, ws_gu, ws_down, g, b, yg, tm):
    t = h1.shape[0]
    const = lambda shape: pl.BlockSpec(shape, lambda i: (0,) * len(shape))
    return pl.pallas_call(
        _combine_kernel,
        grid=(t // tm,),
        in_specs=[
            pl.BlockSpec((tm, D_MODEL), lambda i: (i, 0)),
            pl.BlockSpec((TOP_K, tm), lambda i: (0, i)),
            const((D_MODEL, 2 * D_SHARED)), const((D_SHARED, D_MODEL)),
            const((1, D_MODEL)), const((1, D_MODEL)),
            pl.BlockSpec((TOP_K, tm, D_PACK), lambda i: (0, i, 0)),
        ],
        out_specs=pl.BlockSpec((tm, D_MODEL), lambda i: (i, 0)),
        out_shape=jax.ShapeDtypeStruct((t, D_MODEL), F32),
        compiler_params=_params(("arbitrary",)),
        name="moe_combine",
    )(h1, wts, ws_gu, ws_down, g, b, yg)


def _block_diag(w):
    nb, bi, bo = w.shape
    eye = jnp.eye(nb, dtype=w.dtype)
    return (eye[:, None, :, None] * w[:, :, None, :]).reshape(nb * bi, nb * bo)


def _pad_lanes(v, offset, width):
    return jnp.zeros((1, width), F32).at[0, offset:offset + v.shape[0]].set(v)


def _layer(h_in_x, l, p, tiles):
    bsz, seq, _ = h_in_x.shape
    t = bsz * seq
    row = lambda v: v.reshape(1, -1)

    w_in = p['w_in'][l]
    w_main = w_in.astype(BF16)
    w_small = jnp.zeros((D_MODEL, LANES), F32).at[:, :2 * GDN_HEADS].set(w_in[:, N_MAIN:])
    zeros = lambda n: jnp.zeros((CONV_WIDTH, n), F32)
    conv_w = jnp.concatenate([p['lru_conv_w'][l], zeros(LRU_WIDTH), p['gdn_conv_w'][l], zeros(GDN_V)], 1)
    conv_b = jnp.zeros((1, N_MAIN), F32).at[0, :LRU_WIDTH].set(p['lru_conv_b'][l])
    h0, proj, small_t = _inproj(h_in_x.reshape(t, D_MODEL), row(p['ln_g']), row(p['ln_b']),
                                       w_main, w_small, conv_w, conv_b, tiles['inproj'], seq)
    proj3 = proj.reshape(bsz, seq, N_MAIN)

    w_gates = jnp.concatenate([_block_diag(p['lru_w_rg'][l]), _block_diag(p['lru_w_ig'][l])], 1).astype(BF16)
    b_gates = jnp.concatenate([p['lru_b_rg'][l], p['lru_b_ig'][l]]).reshape(1, -1)
    rows = GDN_GROUP * GDN_CHUNK
    smallt3 = small_t.reshape(8, bsz, seq // rows, rows).transpose(1, 2, 0, 3)
    a_log, dt_bias = p['gdn_a_log'][l], p['gdn_dt_bias'][l]
    alc = _pad_lanes(a_log, GDN_HEADS, 8).reshape(8, 1)
    dtc = _pad_lanes(dt_bias, GDN_HEADS, 8).reshape(8, 1)
    y_lru, y_gdn = _mixers(proj3, smallt3, w_gates, b_gates, row(p['lru_lambda'][l]), row(p['lru_out_g'][l]),
                           alc, dtc, row(p['gdn_norm_w'][l]), tiles['gdn_nb'])

    w_out = p['w_out'][l].astype(BF16)
    h1, h1p, top_e, wts, rank, counts = _router(
        y_lru.reshape(t, LRU_WIDTH), y_gdn.reshape(t, GDN_V), h0, w_out[:LRU_WIDTH], w_out[LRU_WIDTH:],
        row(p['ln1_g'][l]), row(p['ln1_b'][l]), p['w_router'][l].T, p['router_bias'][l].reshape(-1, 1),
        tiles['router'])

    counts = counts[:, 0]
    padded = (counts + MOE_BLOCK - 1) // MOE_BLOCK * MOE_BLOCK
    pad_end = jnp.cumsum(padded)
    pad_start = pad_end - padded
    n_blocks = (t * TOP_K + N_EXPERTS * (MOE_BLOCK - 1)) // MOE_BLOCK
    n_rows = n_blocks * MOE_BLOCK
    n_used = (pad_end[-1] // MOE_BLOCK).astype(jnp.int32)
    blk_ids = jnp.minimum(jnp.arange(n_blocks, dtype=jnp.int32), n_used - 1)
    blk_e = jnp.minimum(jnp.sum(pad_end[None, :] <= (blk_ids * MOE_BLOCK)[:, None], axis=1),
                        N_EXPERTS - 1).astype(jnp.int32)

    dest = _dest(top_e, rank, pad_start.reshape(-1, 1), tiles['dest'])
    n_valid = jnp.clip(counts[blk_e] - (blk_ids * MOE_BLOCK - pad_start[blk_e]), 0, MOE_BLOCK).astype(jnp.int32)
    xs = _sc_scatter_rows(h1p, dest, n_rows, SC_SCATTER_CHUNK)
    active = jnp.arange(n_blocks, dtype=jnp.int32) < n_used
    first = (active & jnp.concatenate([jnp.ones((1,), bool), blk_e[1:] != blk_e[:-1]])).astype(jnp.int32)
    slot = ((jnp.cumsum(first) - 1) % 2).astype(jnp.int32)
    used = counts > 0
    later = jnp.where(used[None, :] & (jnp.arange(N_EXPERTS)[None, :] > jnp.arange(N_EXPERTS)[:, None]),
                      jnp.arange(N_EXPERTS, dtype=jnp.int32)[None, :], N_EXPERTS)
    next_used = jnp.min(later, axis=1)
    next_e = jnp.where(next_used < N_EXPERTS, next_used, -1)[blk_e].astype(jnp.int32)
    ys = _experts(blk_e, n_valid, first, slot, next_e, n_used.reshape(1), xs,
                  p['w_gate'][l], p['w_up'][l], p['w_down'][l])
    ws_gu = jnp.concatenate([p['ws_gate'][l], p['ws_up'][l]], 1).astype(BF16)
    yg = _sc_gather_rows(ys, dest.reshape(TOP_K * t), SC_GATHER_CHUNK).reshape(TOP_K, t, D_PACK)
    out = _combine(h1, wts, ws_gu, p['ws_down'][l].astype(BF16),
                   row(p['ln2_g'][l]), row(p['ln2_b'][l]), yg, tiles['combine'])
    return out.reshape(bsz, seq, D_MODEL)


def _tiles(bsz, seq):
    t = bsz * seq
    return {
        'inproj': min(512, t),
        'gdn_nb': bsz,
        'router': min(1024, t),
        'dest': min(512, t),
        'combine': min(512, t),
    }


def kernel(x, ln_in_g, ln_in_b, w_in, lru_conv_w, lru_conv_b, lru_w_rg, lru_b_rg, lru_w_ig, lru_b_ig,
           lru_lambda, lru_out_g, gdn_conv_w, gdn_a_log, gdn_dt_bias, gdn_norm_w, w_out, ln1_g, ln1_b,
           w_router, router_bias, w_gate, w_up, w_down, ws_gate, ws_up, ws_down, ln2_g, ln2_b):
    assert w_in.shape[0] == DEPTH == 1
    p = dict(ln_g=ln_in_g, ln_b=ln_in_b, w_in=w_in, lru_conv_w=lru_conv_w, lru_conv_b=lru_conv_b,
             lru_w_rg=lru_w_rg, lru_b_rg=lru_b_rg, lru_w_ig=lru_w_ig, lru_b_ig=lru_b_ig,
             lru_lambda=lru_lambda, lru_out_g=lru_out_g, gdn_conv_w=gdn_conv_w, gdn_a_log=gdn_a_log,
             gdn_dt_bias=gdn_dt_bias, gdn_norm_w=gdn_norm_w, w_out=w_out, ln1_g=ln1_g, ln1_b=ln1_b,
             w_router=w_router, router_bias=router_bias, w_gate=w_gate, w_up=w_up, w_down=w_down,
             ws_gate=ws_gate, ws_up=ws_up, ws_down=ws_down, ln2_g=ln2_g, ln2_b=ln2_b)
    bsz, seq, _ = x.shape
    return _layer(x, 0, p, _tiles(bsz, seq))
```

```python
import functools

import jax
import jax.numpy as jnp
from jax import lax
from jax.experimental import pallas as pl
from jax.experimental.pallas import tpu as pltpu
from jax.experimental.pallas import tpu_sc as plsc

F32 = jnp.float32
BF16 = jnp.bfloat16

D_MODEL = 1024
LRU_WIDTH = 512
LRU_C = 8.0
CONV_WIDTH = 4
GDN_HEADS = 4
GDN_DK = 128
GDN_DV = 128
GDN_CHUNK = 64
GDN_GROUP = 2
GDN_QK = GDN_HEADS * GDN_DK
GDN_V = GDN_HEADS * GDN_DV
N_MAIN = 2 * LRU_WIDTH + 2 * GDN_QK + 2 * GDN_V
N_EXPERTS = 256
TOP_K = 8
N_GROUPS = 8
GROUP_SIZE = N_EXPERTS // N_GROUPS
TOPK_GROUPS = 4
D_EXPERT = 256
D_SHARED = 256
ROUTED_SCALE = 2.5
MOE_BLOCK = 640
D_PACK = D_MODEL // 2
LN_EPS = 1e-5
NORM_EPS = 1e-6
DEPTH = 1
DEEPNORM_ALPHA = (2.0 * DEPTH) ** 0.25

SCAN_GROUP = 8
HALO = 8
CONV_GROUP = 512
CONV_GROUPS = (0, 2, 3, 4)
LANES = 128
VMEM_LIMIT = 56 * 1024 * 1024
ROUTER_SUB = 256
EXPERT_BANDS = 5
SC_CORES = 2
SC_WORKERS = 32
SC_GATHER_CHUNK = 64
SC_SCATTER_CHUNK = 128

NN = (((1,), (0,)), ((), ()))
NT = (((1,), (1,)), ((), ()))
TN = (((0,), (0,)), ((), ()))


def _dot(a, b, dims=NN):
    return lax.dot_general(a, b, dims, preferred_element_type=F32)


def _split(a):
    hi = a.astype(BF16)
    lo = (a - hi.astype(F32)).astype(BF16)
    return hi, lo


def _dot3(a, b, dims=NN):
    ah, al = _split(a)
    bh, bl = _split(b)
    return _dot(ah, bh, dims) + (_dot(ah, bl, dims) + _dot(al, bh, dims))


def _layer_norm(x, g, b):
    mu = jnp.mean(x, -1, keepdims=True)
    xc = x - mu
    var = jnp.mean(xc * xc, -1, keepdims=True)
    return xc * lax.rsqrt(var + LN_EPS) * g + b


def _sigmoid(x):
    return 0.5 * jnp.tanh(0.5 * x) + 0.5


def _silu(x):
    return x * _sigmoid(x)


def _softplus(x):
    return jnp.maximum(x, 0.0) + jnp.log1p(jnp.exp(-jnp.abs(x)))


def _gelu_tanh(x):
    c = 0.7978845608028654
    return x * (0.5 * (1.0 + jnp.tanh(c * (x + 0.044715 * (x * x * x)))))


def _pack_rows(x):
    hi = lax.bitcast_convert_type(x[:, :D_PACK].astype(BF16).astype(F32), jnp.uint32)
    lo = lax.bitcast_convert_type(x[:, D_PACK:].astype(BF16).astype(F32), jnp.uint32)
    return (hi & jnp.uint32(0xFFFF0000)) | (lo >> 16)


def _unpack_rows(w):
    hi = lax.bitcast_convert_type(w & jnp.uint32(0xFFFF0000), F32)
    lo = lax.bitcast_convert_type(w << 16, F32)
    return hi, lo


def _params(sem, **kw):
    return pltpu.CompilerParams(dimension_semantics=sem, vmem_limit_bytes=VMEM_LIMIT, **kw)


def _inproj_kernel(x_ref, g_ref, b_ref, w_ref, ws_ref, cw_ref, cb_ref,
                   h_ref, proj_ref, smallt_ref, hist, *, tiles_per_seq):
    i = pl.program_id(0)
    tm = x_ref.shape[0]
    h = _layer_norm(x_ref[...], g_ref[...], b_ref[...])
    h_ref[...] = h
    hb = h.astype(BF16)

    @pl.when(i % tiles_per_seq == 0)
    def _():
        hist[...] = jnp.zeros_like(hist)

    for g in range(N_MAIN // CONV_GROUP):
        cols = slice(g * CONV_GROUP, (g + 1) * CONV_GROUP)
        p = _dot(hb, w_ref[:, cols])
        if g in CONV_GROUPS:
            xcat = jnp.concatenate([hist[:, cols], p], axis=0)
            acc = cb_ref[:, cols]
            for j in range(CONV_WIDTH):
                off = HALO - (CONV_WIDTH - 1) + j
                acc = acc + xcat[off:off + tm, :] * cw_ref[j:j + 1, cols]
            hist[:, cols] = p[tm - HALO:, :]
            p = acc
        proj_ref[:, cols] = p
    small = _dot3(h, ws_ref[...])
    rows = smallt_ref.shape[2]
    for j in range(tm // rows):
        smallt_ref[j] = small[j * rows:(j + 1) * rows, :].T[:smallt_ref.shape[1], :]


def _inproj(x2d, g, b, w_main, w_small, conv_w, conv_b, tm, seq):
    t = x2d.shape[0]
    group = GDN_GROUP * GDN_CHUNK
    return pl.pallas_call(
        functools.partial(_inproj_kernel, tiles_per_seq=seq // tm),
        grid=(t // tm,),
        in_specs=[
            pl.BlockSpec((tm, D_MODEL), lambda i: (i, 0)),
            pl.BlockSpec((1, D_MODEL), lambda i: (0, 0)),
            pl.BlockSpec((1, D_MODEL), lambda i: (0, 0)),
            pl.BlockSpec((D_MODEL, N_MAIN), lambda i: (0, 0)),
            pl.BlockSpec((D_MODEL, LANES), lambda i: (0, 0)),
            pl.BlockSpec((CONV_WIDTH, N_MAIN), lambda i: (0, 0)),
            pl.BlockSpec((1, N_MAIN), lambda i: (0, 0)),
        ],
        out_specs=[
            pl.BlockSpec((tm, D_MODEL), lambda i: (i, 0)),
            pl.BlockSpec((tm, N_MAIN), lambda i: (i, 0)),
            pl.BlockSpec((tm // group, 8, group), lambda i: (i, 0, 0)),
        ],
        out_shape=[
            jax.ShapeDtypeStruct((t, D_MODEL), F32),
            jax.ShapeDtypeStruct((t, N_MAIN), F32),
            jax.ShapeDtypeStruct((t // group, 8, group), F32),
        ],
        scratch_shapes=[pltpu.VMEM((HALO, N_MAIN), F32)],
        compiler_params=_params(("arbitrary",)),
        name="ln_inproj",
    )(x2d, g, b, w_main, w_small, conv_w, conv_b)


def _lru_tile(xc, gate, wg, bg, lam, og, carry):
    rows = xc.shape[0]
    gates = _dot(xc.astype(BF16), wg) + bg
    r = _sigmoid(gates[:, :LRU_WIDTH])
    i = _sigmoid(gates[:, LRU_WIDTH:])
    log_a = (-LRU_C) * r * _softplus(-lam)
    a = jnp.exp(log_a)
    one_minus_a2 = -jnp.tanh(log_a) * (a * a + 1.0)
    mult = jnp.where(one_minus_a2 > 0.0, one_minus_a2 * lax.rsqrt(one_minus_a2), 0.0)
    bv = mult * (i * xc)
    a = a.reshape(rows // SCAN_GROUP, SCAN_GROUP, LRU_WIDTH)
    bv = bv.reshape(rows // SCAN_GROUP, SCAN_GROUP, LRU_WIDTH)
    row_in_group = lax.broadcasted_iota(jnp.int32, a.shape, 1)
    d = 1
    while d < SCAN_GROUP:
        a_sh = jnp.where(row_in_group < d, 1.0, pltpu.roll(a, d, 1))
        b_sh = jnp.where(row_in_group < d, 0.0, pltpu.roll(bv, d, 1))
        bv = a * b_sh + bv
        a = a * a_sh
        d *= 2
    a = a.reshape(rows, LRU_WIDTH)
    bv = bv.reshape(rows, LRU_WIDTH)
    parts = []
    for g in range(rows // SCAN_GROUP):
        grp = slice(g * SCAN_GROUP, (g + 1) * SCAN_GROUP)
        hg = a[grp] * carry + bv[grp]
        carry = hg[SCAN_GROUP - 1:, :]
        parts.append(hg)
    h = jnp.concatenate(parts, axis=0)
    y = h * _gelu_tanh(gate)
    ms = jnp.mean(y * y, -1, keepdims=True)
    return y * lax.rsqrt(ms + NORM_EPS) * og, carry


def _bdot(a, b, dims=NN):
    return _dot(a.astype(BF16), b.astype(BF16), dims)


def _gdn_heads(args, norm_w):
    c = GDN_CHUNK
    r = GDN_GROUP * c
    ri = lax.broadcasted_iota(jnp.int32, (r, r), 0)
    ci = lax.broadcasted_iota(jnp.int32, (r, r), 1)
    same = (ri // c) == (ci // c)
    causal = same & (ri >= ci)
    strict = same & (ri > ci)
    upper = same & (ri <= ci)
    chunk_of_row = lax.broadcasted_iota(jnp.int32, (r, 1), 0) // c
    each = lambda f, *ls: [f(*xs) for xs in zip(*ls)]
    q, k, v, z, beta, g_col, g_row, st = [list(x) for x in zip(*args)]
    q = each(lambda x: x * lax.rsqrt(jnp.sum(x * x, -1, keepdims=True) + NORM_EPS) * (GDN_DK ** -0.5), q)
    k = each(lambda x: x * lax.rsqrt(jnp.sum(x * x, -1, keepdims=True) + NORM_EPS), k)
    gc_col = each(lambda g: jnp.sum(jnp.where(causal, g, 0.0), axis=1, keepdims=True), g_row)
    gc_row = each(lambda g: jnp.sum(jnp.where(upper, g, 0.0), axis=0, keepdims=True), g_col)
    decay = each(lambda gc, gr: jnp.exp(jnp.where(causal, gc - gr, -jnp.inf)), gc_col, gc_row)
    kb = each(lambda x, bt: x * bt, k, beta)
    vb = each(lambda x, bt: x * bt, v, beta)
    kk = each(lambda x, y: _bdot(x, y, NT), kb, k)
    a_mat = each(lambda m, d: jnp.where(strict, m * d, 0.0), kk, decay)
    e_col = each(jnp.exp, gc_col)
    rhs = each(lambda x, y, e: jnp.concatenate([x, y * e], axis=1), vb, kb, e_col)
    eye = (ri == ci).astype(F32)
    t_mat = each(lambda a: eye - a, a_mat)
    p = a_mat
    for _ in range(5):
        p = each(lambda x: _bdot(x, x), p)
        t_mat = each(lambda tm_, x: tm_ + _bdot(tm_, x), t_mat, p)
    sol = each(lambda tm_, rr: _bdot(tm_, rr), t_mat, rhs)
    qk = each(lambda x, y: _bdot(x, y, NT), q, k)
    qk = each(lambda m, d: jnp.where(causal, m * d, 0.0), qk, decay)
    q_dec = each(lambda x, e: x * e, q, e_col)
    g_last = [each(lambda gc: gc[(j + 1) * c - 1:(j + 1) * c, :], gc_col) for j in range(GDN_GROUP)]

    def last_of_own_chunk(*gl):
        out = gl[-1]
        for j in range(GDN_GROUP - 2, -1, -1):
            out = jnp.where(chunk_of_row == j, gl[j], out)
        return out

    g_end = each(last_of_own_chunk, *g_last)
    k_dec = each(lambda x, ge, gc: x * jnp.exp(ge - gc), k, g_end, gc_col)
    qs_parts, v_parts = [], []
    for j in range(GDN_GROUP):
        rows = slice(j * c, (j + 1) * c)
        ws = each(lambda x, s: _bdot(x[rows, GDN_DV:], s), sol, st)
        qs_parts.append(each(lambda x, s: _bdot(x[rows], s), q_dec, st))
        v_new = each(lambda x, w: x[rows, :GDN_DV] - w, sol, ws)
        v_parts.append(v_new)
        kv = each(lambda x, vn: _bdot(x[rows], vn, TN), k_dec, v_new)
        st = each(lambda s, gl, d: s * jnp.exp(gl) + d, st, g_last[j], kv)
    qs = each(lambda *parts: jnp.concatenate(parts, axis=0), *qs_parts)
    v_all = each(lambda *parts: jnp.concatenate(parts, axis=0), *v_parts)
    o = each(lambda a, m, vn: a + _bdot(m, vn), qs, qk, v_all)
    o = each(lambda x: x * lax.rsqrt(jnp.mean(x * x, -1, keepdims=True) + NORM_EPS) * norm_w, o)
    o = each(lambda x, zz: x * _silu(zz), o, z)
    return list(zip(o, st))


def _mixer_kernel(xc_ref, gate_ref, q_ref, k_ref, v_ref, z_ref, smt_ref,
                  wg_ref, bg_ref, lam_ref, og_ref, alc_ref, dtc_ref, nw_ref,
                  ylru_ref, y_ref, hcarry, state):
    n = pl.program_id(1)
    c = GDN_GROUP * GDN_CHUNK
    nb = q_ref.shape[0]
    first = n == 0

    @pl.when(first)
    def _():
        state[...] = jnp.zeros_like(state)
        hcarry[...] = jnp.zeros_like(hcarry)

    lru_out = [_lru_tile(xc_ref[b], gate_ref[b], wg_ref[...], bg_ref[...], lam_ref[...], og_ref[...], hcarry[b])
               for b in range(nb)]
    norm_w = nw_ref[...]

    args = []
    for b in range(nb):
        q_all = _silu(q_ref[b])
        k_all = _silu(k_ref[b])
        v_all = _silu(v_ref[b])
        z_all = z_ref[b]
        smt = smt_ref[b]
        beta_rows = _sigmoid(smt)
        g_rows = -jnp.exp(alc_ref[...]) * _softplus(smt + dtc_ref[...])
        head_row = lax.broadcasted_iota(jnp.int32, smt.shape, 0)
        stacked = jnp.where(head_row < GDN_HEADS, beta_rows, g_rows)
        cols = jnp.concatenate([stacked, jnp.zeros((LANES - 8, c), F32)], axis=0).T
        for hd in range(GDN_HEADS):
            sl = slice(hd * GDN_DK, (hd + 1) * GDN_DK)
            args.append((q_all[:, sl], k_all[:, sl], v_all[:, sl], z_all[:, sl],
                         cols[:, hd:hd + 1],
                         cols[:, GDN_HEADS + hd:GDN_HEADS + hd + 1],
                         g_rows[GDN_HEADS + hd:GDN_HEADS + hd + 1, :],
                         state[b, hd]))
    outs = _gdn_heads(args, norm_w)
    for b in range(nb):
        for hd in range(GDN_HEADS):
            o, st_new = outs[b * GDN_HEADS + hd]
            state[b, hd] = st_new
            y_ref[b, :, hd * GDN_DK:(hd + 1) * GDN_DK] = o
    for b, (y_lru, carry) in enumerate(lru_out):
        ylru_ref[b] = y_lru
        hcarry[b] = carry


def _mixers(proj3, smallt3, w_gates, b_gates, lam, out_g, alc, dtc, norm_w, nb):
    bsz, seq, _ = proj3.shape
    c = GDN_GROUP * GDN_CHUNK
    nch = seq // c
    col = lambda j: pl.BlockSpec((nb, c, GDN_QK), lambda b, n: (b, n, j))
    const = lambda shape: pl.BlockSpec(shape, lambda b, n: (0,) * len(shape))
    return pl.pallas_call(
        _mixer_kernel,
        grid=(bsz // nb, nch),
        in_specs=[
            col(0), col(1), col(2), col(3), col(4), col(5),
            pl.BlockSpec((nb, None, 8, c), lambda b, n: (b, n, 0, 0)),
            const((LRU_WIDTH, 2 * LRU_WIDTH)), const((1, 2 * LRU_WIDTH)), const((1, LRU_WIDTH)), const((1, LRU_WIDTH)),
            const((8, 1)), const((8, 1)),
            const((1, GDN_DV)),
        ],
        out_specs=[pl.BlockSpec((nb, c, LRU_WIDTH), lambda b, n: (b, n, 0)),
                   pl.BlockSpec((nb, c, GDN_V), lambda b, n: (b, n, 0))],
        out_shape=[jax.ShapeDtypeStruct((bsz, seq, LRU_WIDTH), F32),
                   jax.ShapeDtypeStruct((bsz, seq, GDN_V), F32)],
        scratch_shapes=[
            pltpu.VMEM((nb, 1, LRU_WIDTH), F32),
            pltpu.VMEM((nb, GDN_HEADS, GDN_DK, GDN_DV), F32),
        ],
        compiler_params=_params(("arbitrary", "arbitrary")),
        name="mixers",
    )(proj3, proj3, proj3, proj3, proj3, proj3, smallt3, w_gates, b_gates, lam, out_g, alc, dtc, norm_w)


def _pick_experts(logits, rbias):
    n = logits.shape[1]
    scores = _sigmoid(logits)
    choice = scores + rbias
    neg = -jnp.inf
    gs_rows = []
    sub = lax.broadcasted_iota(jnp.int32, (GROUP_SIZE, n), 0).astype(F32)
    for g in range(N_GROUPS):
        cg = choice[g * GROUP_SIZE:(g + 1) * GROUP_SIZE, :]
        m1 = jnp.max(cg, axis=0, keepdims=True)
        i1 = jnp.min(jnp.where(cg == m1, sub, float(GROUP_SIZE)), axis=0, keepdims=True)
        m2 = jnp.max(jnp.where(sub == i1, neg, cg), axis=0, keepdims=True)
        gs_rows.append(m1 + m2)
    gs = jnp.concatenate(gs_rows, axis=0)
    gi = lax.broadcasted_iota(jnp.int32, (N_GROUPS, n), 0).astype(F32)
    gsel = jnp.zeros((N_GROUPS, n), jnp.bool_)
    for _ in range(TOPK_GROUPS):
        m = jnp.max(gs, axis=0, keepdims=True)
        idx = jnp.min(jnp.where(gs == m, gi, float(N_GROUPS)), axis=0, keepdims=True)
        hit = gi == idx
        gsel = jnp.logical_or(gsel, hit)
        gs = jnp.where(hit, neg, gs)
    masked = jnp.concatenate(
        [jnp.where(gsel[g:g + 1, :], choice[g * GROUP_SIZE:(g + 1) * GROUP_SIZE, :], neg)
         for g in range(N_GROUPS)], axis=0)
    ei = lax.broadcasted_iota(jnp.int32, (N_EXPERTS, n), 0).astype(F32)
    hits, e_rows, w_rows = [], [], []
    multi = jnp.zeros((N_EXPERTS, n), F32)
    for _ in range(TOP_K):
        m = jnp.max(masked, axis=0, keepdims=True)
        idx = jnp.min(jnp.where(masked == m, ei, float(N_EXPERTS)), axis=0, keepdims=True)
        hit = ei == idx
        hits.append(hit)
        e_rows.append(idx)
        w_rows.append(jnp.sum(jnp.where(hit, scores, 0.0), axis=0, keepdims=True))
        multi = multi + hit.astype(F32)
        masked = jnp.where(hit, neg, masked)
    wts = jnp.concatenate(w_rows, axis=0)
    wts = wts / (jnp.sum(wts, axis=0, keepdims=True) + 1e-20) * ROUTED_SCALE
    return jnp.concatenate(e_rows, axis=0), wts, hits, multi


def _router_kernel(yl_ref, yg_ref, h0_ref, wo1_ref, wo2_ref, g_ref, b_ref, wrt_ref, rb_ref,
                   h1_ref, h1p_ref, e_ref, w_ref, rank_ref, cnt_ref, carry):
    i = pl.program_id(0)
    tm = h0_ref.shape[0]
    n = min(ROUTER_SUB, tm)
    subs = [slice(j * n, (j + 1) * n) for j in range(tm // n)]

    @pl.when(i == 0)
    def _():
        carry[...] = jnp.zeros_like(carry)

    mixes = [_dot(yl_ref[r, :].astype(BF16), wo1_ref[...]) + _dot(yg_ref[r, :].astype(BF16), wo2_ref[...])
             for r in subs]
    h1s = [_layer_norm(DEEPNORM_ALPHA * h0_ref[r, :] + mix, g_ref[...], b_ref[...])
           for r, mix in zip(subs, mixes)]
    for r, h1 in zip(subs, h1s):
        h1_ref[r, :] = h1
        h1p_ref[r, :] = _pack_rows(h1)
    logits = [_dot3(wrt_ref[...], h1, NT) for h1 in h1s]
    picks = [_pick_experts(lg, rb_ref[...]) for lg in logits]
    ti = lax.broadcasted_iota(jnp.int32, (n, n), 0)
    tj = lax.broadcasted_iota(jnp.int32, (n, n), 1)
    before = (ti < tj).astype(BF16)
    cums = [_dot(multi.astype(BF16), before) for _, _, _, multi in picks]
    base = carry[...]
    for r, (e_rows, wts, hits, multi), cum in zip(subs, picks, cums):
        cum = cum + base
        r_rows = [jnp.sum(jnp.where(hit, cum, 0.0), axis=0, keepdims=True) for hit in hits]
        base = base + jnp.sum(multi, axis=1, keepdims=True)
        e_ref[:, r] = e_rows.astype(jnp.int32)
        w_ref[:, r] = wts
        rank_ref[:, r] = jnp.concatenate(r_rows, axis=0).astype(jnp.int32)
    carry[...] = base
    cnt_ref[...] = base.astype(jnp.int32)


def _router(y_lru, y_gdn, h0, wo1, wo2, g, b, w_router_t, rbias, tm):
    t = h0.shape[0]
    const = lambda shape: pl.BlockSpec(shape, lambda i: (0,) * len(shape))
    return pl.pallas_call(
        _router_kernel,
        grid=(t // tm,),
        in_specs=[
            pl.BlockSpec((tm, LRU_WIDTH), lambda i: (i, 0)),
            pl.BlockSpec((tm, GDN_V), lambda i: (i, 0)),
            pl.BlockSpec((tm, D_MODEL), lambda i: (i, 0)),
            const((LRU_WIDTH, D_MODEL)), const((GDN_V, D_MODEL)),
            const((1, D_MODEL)), const((1, D_MODEL)),
            const((N_EXPERTS, D_MODEL)), const((N_EXPERTS, 1)),
        ],
        out_specs=[
            pl.BlockSpec((tm, D_MODEL), lambda i: (i, 0)),
            pl.BlockSpec((tm, D_PACK), lambda i: (i, 0)),
            pl.BlockSpec((TOP_K, tm), lambda i: (0, i)),
            pl.BlockSpec((TOP_K, tm), lambda i: (0, i)),
            pl.BlockSpec((TOP_K, tm), lambda i: (0, i)),
            const((N_EXPERTS, 1)),
        ],
        out_shape=[
            jax.ShapeDtypeStruct((t, D_MODEL), F32),
            jax.ShapeDtypeStruct((t, D_PACK), jnp.uint32),
            jax.ShapeDtypeStruct((TOP_K, t), jnp.int32),
            jax.ShapeDtypeStruct((TOP_K, t), F32),
            jax.ShapeDtypeStruct((TOP_K, t), jnp.int32),
            jax.ShapeDtypeStruct((N_EXPERTS, 1), jnp.int32),
        ],
        scratch_shapes=[pltpu.VMEM((N_EXPERTS, 1), F32)],
        compiler_params=_params(("arbitrary",)),
        name="outproj_router",
    )(y_lru, y_gdn, h0, wo1, wo2, g, b, w_router_t, rbias)


def _dest_kernel(e_ref, r_ref, ps_ref, d_ref):
    tm = e_ref.shape[1]
    ei = lax.broadcasted_iota(jnp.int32, (N_EXPERTS, tm), 0)
    rows = []
    for k in range(TOP_K):
        hit = ei == e_ref[k:k + 1, :]
        rows.append(jnp.sum(jnp.where(hit, ps_ref[...], 0), axis=0, keepdims=True))
    d_ref[...] = jnp.concatenate(rows, axis=0) + r_ref[...]


def _dest(top_e, rank, pad_start, tm):
    t = top_e.shape[1]
    blk = pl.BlockSpec((TOP_K, tm), lambda i: (0, i))
    return pl.pallas_call(
        _dest_kernel,
        grid=(t // tm,),
        in_specs=[blk, blk, pl.BlockSpec((N_EXPERTS, 1), lambda i: (0, 0))],
        out_specs=blk,
        out_shape=jax.ShapeDtypeStruct((TOP_K, t), jnp.int32),
        compiler_params=_params(("arbitrary",)),
        name="moe_dest",
    )(top_e, rank, pad_start)


def _sc_scatter_rows(rows, idx, n_out, chunk):
    n_copies, t = idx.shape
    d = rows.shape[1]
    per_worker = t // SC_WORKERS
    n_chunks = per_worker // chunk
    mesh = plsc.VectorSubcoreMesh(core_axis_name="c", subcore_axis_name="s")
    idx_flat = idx.reshape(n_copies * t)

    @functools.partial(
        pl.kernel, mesh=mesh,
        out_type=jax.ShapeDtypeStruct((n_out, d), rows.dtype),
        scratch_types=[pltpu.VMEM((chunk,), jnp.int32) for _ in range(n_copies)] + [
            pltpu.VMEM((chunk, d), rows.dtype),
            pltpu.SemaphoreType.DMA,
        ],
    )
    def scatter(rows_hbm, idx_hbm, out_hbm, *scratch):
        idx_v = scratch[:n_copies]
        rows_v, sem = scratch[n_copies:]
        wid = lax.axis_index("s") * SC_CORES + lax.axis_index("c")
        base = wid * per_worker

        @pl.loop(0, n_chunks)
        def _(j):
            off = base + j * chunk
            for k in range(n_copies):
                pltpu.sync_copy(idx_hbm.at[pl.ds(k * t + off, chunk)], idx_v[k])
            pltpu.sync_copy(rows_hbm.at[pl.ds(off, chunk)], rows_v)
            copies = [pltpu.async_copy(rows_v, out_hbm.at[idx_v[k]], sem) for k in range(n_copies)]
            for cp in copies:
                cp.wait()

    return scatter(rows, idx_flat)


def _expert_kernel(be_ref, nv_ref, first_ref, slot_ref, next_ref, nu_ref,
                   xs_ref, wg_hbm, wu_hbm, wd_hbm, ys_ref, wg_f, wu_f, wd_f, wgu_b, wd_b, sem):
    i = pl.program_id(0)

    def fetch(e, slot):
        return (pltpu.make_async_copy(wg_hbm.at[e], wg_f.at[slot], sem.at[slot]),
                pltpu.make_async_copy(wu_hbm.at[e], wu_f.at[slot], sem.at[slot]),
                pltpu.make_async_copy(wd_hbm.at[e], wd_f.at[slot], sem.at[slot]))

    @pl.when(i < nu_ref[0])
    def _():
        e = be_ref[i]
        slot = slot_ref[i]

        @pl.when(first_ref[i] == 1)
        def _():
            @pl.when(i == 0)
            def _():
                for cp in fetch(e, slot):
                    cp.start()

            for cp in fetch(e, slot):
                cp.wait()

            @pl.when(next_ref[i] >= 0)
            def _():
                for cp in fetch(next_ref[i], 1 - slot):
                    cp.start()

            wgu_b[:, :D_EXPERT] = wg_f[slot].astype(BF16)
            wgu_b[:, D_EXPERT:] = wu_f[slot].astype(BF16)
            wd_b[...] = wd_f[slot].astype(BF16)

        n = xs_ref.shape[0] // EXPERT_BANDS
        bands = [slice(j * n, (j + 1) * n) for j in range(EXPERT_BANDS)]
        row = lax.broadcasted_iota(jnp.int32, (n, D_PACK), 0)
        xs = [_unpack_rows(jnp.where(row + j * n < nv_ref[i], xs_ref[r, :], jnp.uint32(0)))
              for j, r in enumerate(bands)]
        gus = [_dot(x_hi.astype(BF16), wgu_b[:D_PACK, :]) + _dot(x_lo.astype(BF16), wgu_b[D_PACK:, :])
               for x_hi, x_lo in xs]
        hs = [_silu(gu[:, :D_EXPERT]) * gu[:, D_EXPERT:] for gu in gus]
        ys = [_dot(h.astype(BF16), wd_b[...]) for h in hs]
        for r, y in zip(bands, ys):
            ys_ref[r, :] = _pack_rows(y)


def _experts(blk_e, n_valid, first, slot, next_e, n_used, xs, w_gate, w_up, w_down):
    n_rows = xs.shape[0]
    n_blocks = n_rows // MOE_BLOCK
    blk = lambda i, be, nv, fi, sl, nx, nu: (jnp.minimum(i, nu[0] - 1), 0)
    return pl.pallas_call(
        _expert_kernel,
        grid_spec=pltpu.PrefetchScalarGridSpec(
            num_scalar_prefetch=6,
            grid=(n_blocks,),
            in_specs=[
                pl.BlockSpec((MOE_BLOCK, D_PACK), blk),
                pl.BlockSpec(memory_space=pl.ANY),
                pl.BlockSpec(memory_space=pl.ANY),
                pl.BlockSpec(memory_space=pl.ANY),
            ],
            out_specs=pl.BlockSpec((MOE_BLOCK, D_PACK), blk),
            scratch_shapes=[
                pltpu.VMEM((2, D_MODEL, D_EXPERT), F32),
                pltpu.VMEM((2, D_MODEL, D_EXPERT), F32),
                pltpu.VMEM((2, D_EXPERT, D_MODEL), F32),
                pltpu.VMEM((D_MODEL, 2 * D_EXPERT), BF16),
                pltpu.VMEM((D_EXPERT, D_MODEL), BF16),
                pltpu.SemaphoreType.DMA((2,)),
            ],
        ),
        out_shape=jax.ShapeDtypeStruct((n_rows, D_PACK), jnp.uint32),
        compiler_params=_params(("arbitrary",)),
        name="moe_experts",
    )(blk_e, n_valid, first, slot, next_e, n_used, xs, w_gate, w_up, w_down)


def _sc_gather_rows(table, idx, chunk):
    n_idx = idx.shape[0]
    d = table.shape[1]
    per_worker = n_idx // SC_WORKERS
    n_chunks = per_worker // chunk
    assert n_chunks % 2 == 0 and n_chunks * chunk * SC_WORKERS == n_idx
    mesh = plsc.VectorSubcoreMesh(core_axis_name="c", subcore_axis_name="s")

    @functools.partial(
        pl.kernel, mesh=mesh,
        out_type=jax.ShapeDtypeStruct((n_idx, d), table.dtype),
        scratch_types=[
            pltpu.VMEM((chunk,), jnp.int32), pltpu.VMEM((chunk,), jnp.int32),
            pltpu.VMEM((chunk, d), table.dtype), pltpu.VMEM((chunk, d), table.dtype),
            pltpu.SemaphoreType.DMA, pltpu.SemaphoreType.DMA, pltpu.SemaphoreType.DMA, pltpu.SemaphoreType.DMA,
        ],
    )
    def gather(table_hbm, idx_hbm, out_hbm, idx_v0, idx_v1, rows_v0, rows_v1, gsem0, gsem1, osem0, osem1):
        idx_v, rows_v, gsem, osem = (idx_v0, idx_v1), (rows_v0, rows_v1), (gsem0, gsem1), (osem0, osem1)
        wid = lax.axis_index("s") * SC_CORES + lax.axis_index("c")
        base = wid * per_worker

        def gather_copy(slot):
            return pltpu.make_async_copy(table_hbm.at[idx_v[slot]], rows_v[slot], gsem[slot])

        def out_copy(c, slot):
            return pltpu.make_async_copy(rows_v[slot], out_hbm.at[pl.ds(base + c * chunk, chunk)], osem[slot])

        def start_gather(c, slot):
            pltpu.sync_copy(idx_hbm.at[pl.ds(base + c * chunk, chunk)], idx_v[slot])
            gather_copy(slot).start()

        start_gather(0, 0)

        @pl.loop(0, n_chunks, step=2)
        def _(j):
            for b in range(2):
                c = j + b
                cur, other = b, 1 - b

                @pl.when(c >= 1)
                def _():
                    out_copy(c - 1, other).wait()

                @pl.when(c + 1 < n_chunks)
                def _():
                    start_gather(c + 1, other)

                gather_copy(cur).wait()
                out_copy(c, cur).start()

        out_copy(n_chunks - 1, 1).wait()

    return gather(table, idx)


def _combine_kernel(h1_ref, wts_ref, wsgu_ref, wsd_ref, g_ref, b_ref, yg_ref, out_ref):
    h1 = h1_ref[...]
    gu = _dot(h1.astype(BF16), wsgu_ref[...])
    hs = _silu(gu[:, :D_SHARED]) * gu[:, D_SHARED:]
    acc = DEEPNORM_ALPHA * h1 + _dot(hs.astype(BF16), wsd_ref[...])
    wts = wts_ref[...].T
    acc_hi = acc[:, :D_PACK]
    acc_lo = acc[:, D_PACK:]
    for k in range(TOP_K):
        y_hi, y_lo = _unpack_rows(yg_ref[k])
        acc_hi = acc_hi + y_hi * wts[:, k:k + 1]
        acc_lo = acc_lo + y_lo * wts[:, k:k + 1]
    out_ref[...] = _layer_norm(jnp.concatenate([acc_hi, acc_lo], axis=1), g_ref[...], b_ref[...])


def _combine(h1, wts, ws_gu, ws_down, g, b, yg, tm):
    t = h1.shape[0]
    const = lambda shape: pl.BlockSpec(shape, lambda i: (0,) * len(shape))
    return pl.pallas_call(
        _combine_kernel,
        grid=(t // tm,),
        in_specs=[
            pl.BlockSpec((tm, D_MODEL), lambda i: (i, 0)),
            pl.BlockSpec((TOP_K, tm), lambda i: (0, i)),
            const((D_MODEL, 2 * D_SHARED)), const((D_SHARED, D_MODEL)),
            const((1, D_MODEL)), const((1, D_MODEL)),
            pl.BlockSpec((TOP_K, tm, D_PACK), lambda i: (0, i, 0)),
        ],
        out_specs=pl.BlockSpec((tm, D_MODEL), lambda i: (i, 0)),
        out_shape=jax.ShapeDtypeStruct((t, D_MODEL), F32),
        compiler_params=_params(("arbitrary",)),
        name="moe_combine",
    )(h1, wts, ws_gu, ws_down, g, b, yg)


def _block_diag(w):
    nb, bi, bo = w.shape
    eye = jnp.eye(nb, dtype=w.dtype)
    return (eye[:, None, :, None] * w[:, :, None, :]).reshape(nb * bi, nb * bo)


def _pad_lanes(v, offset, width):
    return jnp.zeros((1, width), F32).at[0, offset:offset + v.shape[0]].set(v)


def _layer(h_in_x, l, p, tiles):
    bsz, seq, _ = h_in_x.shape
    t = bsz * seq
    row = lambda v: v.reshape(1, -1)

    w_in = p['w_in'][l]
    w_main = w_in.astype(BF16)
    w_small = jnp.zeros((D_MODEL, LANES), F32).at[:, :2 * GDN_HEADS].set(w_in[:, N_MAIN:])
    zeros = lambda n: jnp.zeros((CONV_WIDTH, n), F32)
    conv_w = jnp.concatenate([p['lru_conv_w'][l], zeros(LRU_WIDTH), p['gdn_conv_w'][l], zeros(GDN_V)], 1)
    conv_b = jnp.zeros((1, N_MAIN), F32).at[0, :LRU_WIDTH].set(p['lru_conv_b'][l])
    h0, proj, small_t = _inproj(h_in_x.reshape(t, D_MODEL), row(p['ln_g']), row(p['ln_b']),
                                       w_main, w_small, conv_w, conv_b, tiles['inproj'], seq)
    proj3 = proj.reshape(bsz, seq, N_MAIN)

    w_gates = jnp.concatenate([_block_diag(p['lru_w_rg'][l]), _block_diag(p['lru_w_ig'][l])], 1).astype(BF16)
    b_gates = jnp.concatenate([p['lru_b_rg'][l], p['lru_b_ig'][l]]).reshape(1, -1)
    rows = GDN_GROUP * GDN_CHUNK
    smallt3 = small_t.reshape(bsz, seq // rows, 8, rows)
    a_log, dt_bias = p['gdn_a_log'][l], p['gdn_dt_bias'][l]
    alc = _pad_lanes(a_log, GDN_HEADS, 8).reshape(8, 1)
    dtc = _pad_lanes(dt_bias, GDN_HEADS, 8).reshape(8, 1)
    y_lru, y_gdn = _mixers(proj3, smallt3, w_gates, b_gates, row(p['lru_lambda'][l]), row(p['lru_out_g'][l]),
                           alc, dtc, row(p['gdn_norm_w'][l]), tiles['gdn_nb'])

    w_out = p['w_out'][l].astype(BF16)
    h1, h1p, top_e, wts, rank, counts = _router(
        y_lru.reshape(t, LRU_WIDTH), y_gdn.reshape(t, GDN_V), h0, w_out[:LRU_WIDTH], w_out[LRU_WIDTH:],
        row(p['ln1_g'][l]), row(p['ln1_b'][l]), p['w_router'][l].T, p['router_bias'][l].reshape(-1, 1),
        tiles['router'])

    counts = counts[:, 0]
    padded = (counts + MOE_BLOCK - 1) // MOE_BLOCK * MOE_BLOCK
    pad_end = jnp.cumsum(padded)
    pad_start = pad_end - padded
    n_blocks = (t * TOP_K + N_EXPERTS * (MOE_BLOCK - 1)) // MOE_BLOCK
    n_rows = n_blocks * MOE_BLOCK
    n_used = (pad_end[-1] // MOE_BLOCK).astype(jnp.int32)
    blk_ids = jnp.minimum(jnp.arange(n_blocks, dtype=jnp.int32), n_used - 1)
    blk_e = jnp.minimum(jnp.sum(pad_end[None, :] <= (blk_ids * MOE_BLOCK)[:, None], axis=1),
                        N_EXPERTS - 1).astype(jnp.int32)

    dest = _dest(top_e, rank, pad_start.reshape(-1, 1), tiles['dest'])
    n_valid = jnp.clip(counts[blk_e] - (blk_ids * MOE_BLOCK - pad_start[blk_e]), 0, MOE_BLOCK).astype(jnp.int32)
    xs = _sc_scatter_rows(h1p, dest, n_rows, SC_SCATTER_CHUNK)
    active = jnp.arange(n_blocks, dtype=jnp.int32) < n_used
    first = (active & jnp.concatenate([jnp.ones((1,), bool), blk_e[1:] != blk_e[:-1]])).astype(jnp.int32)
    slot = ((jnp.cumsum(first) - 1) % 2).astype(jnp.int32)
    used = counts > 0
    later = jnp.where(used[None, :] & (jnp.arange(N_EXPERTS)[None, :] > jnp.arange(N_EXPERTS)[:, None]),
                      jnp.arange(N_EXPERTS, dtype=jnp.int32)[None, :], N_EXPERTS)
    next_used = jnp.min(later, axis=1)
    next_e = jnp.where(next_used < N_EXPERTS, next_used, -1)[blk_e].astype(jnp.int32)
    ys = _experts(blk_e, n_valid, first, slot, next_e, n_used.reshape(1), xs,
                  p['w_gate'][l], p['w_up'][l], p['w_down'][l])
    ws_gu = jnp.concatenate([p['ws_gate'][l], p['ws_up'][l]], 1).astype(BF16)
    yg = _sc_gather_rows(ys, dest.reshape(TOP_K * t), SC_GATHER_CHUNK).reshape(TOP_K, t, D_PACK)
    out = _combine(h1, wts, ws_gu, p['ws_down'][l].astype(BF16),
                   row(p['ln2_g'][l]), row(p['ln2_b'][l]), yg, tiles['combine'])
    return out.reshape(bsz, seq, D_MODEL)


def _tiles(bsz, seq):
    t = bsz * seq
    return {
        'inproj': min(512, t),
        'gdn_nb': bsz,
        'router': min(1024, t),
        'dest': min(512, t),
        'combine': min(512, t),
    }


def kernel(x, ln_in_g, ln_in_b, w_in, lru_conv_w, lru_conv_b, lru_w_rg, lru_b_rg, lru_w_ig, lru_b_ig,
           lru_lambda, lru_out_g, gdn_conv_w, gdn_a_log, gdn_dt_bias, gdn_norm_w, w_out, ln1_g, ln1_b,
           w_router, router_bias, w_gate, w_up, w_down, ws_gate, ws_up, ws_down, ln2_g, ln2_b):
    assert w_in.shape[0] == DEPTH == 1
    p = dict(ln_g=ln_in_g, ln_b=ln_in_b, w_in=w_in, lru_conv_w=lru_conv_w, lru_conv_b=lru_conv_b,
             lru_w_rg=lru_w_rg, lru_b_rg=lru_b_rg, lru_w_ig=lru_w_ig, lru_b_ig=lru_b_ig,
             lru_lambda=lru_lambda, lru_out_g=lru_out_g, gdn_conv_w=gdn_conv_w, gdn_a_log=gdn_a_log,
             gdn_dt_bias=gdn_dt_bias, gdn_norm_w=gdn_norm_w, w_out=w_out, ln1_g=ln1_g, ln1_b=ln1_b,
             w_router=w_router, router_bias=router_bias, w_gate=w_gate, w_up=w_up, w_down=w_down,
             ws_gate=ws_gate, ws_up=ws_up, ws_down=ws_down, ln2_g=ln2_g, ln2_b=ln2_b)
    bsz, seq, _ = x.shape
    return _layer(x, 0, p, _tiles(bsz, seq))
```

```python
import functools

import jax
import jax.numpy as jnp
from jax import lax
from jax.experimental import pallas as pl
from jax.experimental.pallas import tpu as pltpu
from jax.experimental.pallas import tpu_sc as plsc

F32 = jnp.float32
BF16 = jnp.bfloat16

D_MODEL = 1024
LRU_WIDTH = 512
LRU_C = 8.0
CONV_WIDTH = 4
GDN_HEADS = 4
GDN_DK = 128
GDN_DV = 128
GDN_CHUNK = 64
GDN_GROUP = 2
GDN_QK = GDN_HEADS * GDN_DK
GDN_V = GDN_HEADS * GDN_DV
N_MAIN = 2 * LRU_WIDTH + 2 * GDN_QK + 2 * GDN_V
N_EXPERTS = 256
TOP_K = 8
N_GROUPS = 8
GROUP_SIZE = N_EXPERTS // N_GROUPS
TOPK_GROUPS = 4
D_EXPERT = 256
D_SHARED = 256
ROUTED_SCALE = 2.5
MOE_BLOCK = 640
D_PACK = D_MODEL // 2
LN_EPS = 1e-5
NORM_EPS = 1e-6
DEPTH = 1
DEEPNORM_ALPHA = (2.0 * DEPTH) ** 0.25

SCAN_GROUP = 8
HALO = 8
CONV_GROUP = 512
CONV_GROUPS = (0, 2, 3, 4)
LANES = 128
VMEM_LIMIT = 56 * 1024 * 1024
ROUTER_SUB = 256
EXPERT_BANDS = 5
SC_CORES = 2
SC_WORKERS = 32
SC_GATHER_CHUNK = 64
SC_SCATTER_CHUNK = 128

NN = (((1,), (0,)), ((), ()))
NT = (((1,), (1,)), ((), ()))
TN = (((0,), (0,)), ((), ()))


def _dot(a, b, dims=NN):
    return lax.dot_general(a, b, dims, preferred_element_type=F32)


def _split(a):
    hi = a.astype(BF16)
    lo = (a - hi.astype(F32)).astype(BF16)
    return hi, lo


def _dot3(a, b, dims=NN):
    ah, al = _split(a)
    bh, bl = _split(b)
    return _dot(ah, bh, dims) + (_dot(ah, bl, dims) + _dot(al, bh, dims))


def _layer_norm(x, g, b):
    mu = jnp.mean(x, -1, keepdims=True)
    xc = x - mu
    var = jnp.mean(xc * xc, -1, keepdims=True)
    return xc * lax.rsqrt(var + LN_EPS) * g + b


def _sigmoid(x):
    return 0.5 * jnp.tanh(0.5 * x) + 0.5


def _silu(x):
    return x * _sigmoid(x)


def _softplus(x):
    return jnp.maximum(x, 0.0) + jnp.log1p(jnp.exp(-jnp.abs(x)))


def _gelu_tanh(x):
    c = 0.7978845608028654
    return x * (0.5 * (1.0 + jnp.tanh(c * (x + 0.044715 * (x * x * x)))))


def _pack_rows(x):
    hi = lax.bitcast_convert_type(x[:, :D_PACK].astype(BF16).astype(F32), jnp.uint32)
    lo = lax.bitcast_convert_type(x[:, D_PACK:].astype(BF16).astype(F32), jnp.uint32)
    return (hi & jnp.uint32(0xFFFF0000)) | (lo >> 16)


def _unpack_rows(w):
    hi = lax.bitcast_convert_type(w & jnp.uint32(0xFFFF0000), F32)
    lo = lax.bitcast_convert_type(w << 16, F32)
    return hi, lo


def _params(sem, **kw):
    return pltpu.CompilerParams(dimension_semantics=sem, vmem_limit_bytes=VMEM_LIMIT, **kw)


def _inproj_kernel(x_ref, g_ref, b_ref, w_ref, ws_ref, cw_ref, cb_ref,
                   h_ref, proj_ref, smallt_ref, hist, *, tiles_per_seq):
    i = pl.program_id(0)
    tm = x_ref.shape[0]
    h = _layer_norm(x_ref[...], g_ref[...], b_ref[...])
    h_ref[...] = h
    hb = h.astype(BF16)

    @pl.when(i % tiles_per_seq == 0)
    def _():
        hist[...] = jnp.zeros_like(hist)

    for g in range(N_MAIN // CONV_GROUP):
        cols = slice(g * CONV_GROUP, (g + 1) * CONV_GROUP)
        p = _dot(hb, w_ref[:, cols])
        if g in CONV_GROUPS:
            xcat = jnp.concatenate([hist[:, cols], p], axis=0)
            acc = cb_ref[:, cols]
            for j in range(CONV_WIDTH):
                off = HALO - (CONV_WIDTH - 1) + j
                acc = acc + xcat[off:off + tm, :] * cw_ref[j:j + 1, cols]
            hist[:, cols] = p[tm - HALO:, :]
            p = acc
        proj_ref[:, cols] = p
    smallt_ref[...] = _dot3(h, ws_ref[...]).T[:smallt_ref.shape[0], :]


def _inproj(x2d, g, b, w_main, w_small, conv_w, conv_b, tm, seq):
    t = x2d.shape[0]
    return pl.pallas_call(
        functools.partial(_inproj_kernel, tiles_per_seq=seq // tm),
        grid=(t // tm,),
        in_specs=[
            pl.BlockSpec((tm, D_MODEL), lambda i: (i, 0)),
            pl.BlockSpec((1, D_MODEL), lambda i: (0, 0)),
            pl.BlockSpec((1, D_MODEL), lambda i: (0, 0)),
            pl.BlockSpec((D_MODEL, N_MAIN), lambda i: (0, 0)),
            pl.BlockSpec((D_MODEL, LANES), lambda i: (0, 0)),
            pl.BlockSpec((CONV_WIDTH, N_MAIN), lambda i: (0, 0)),
            pl.BlockSpec((1, N_MAIN), lambda i: (0, 0)),
        ],
        out_specs=[
            pl.BlockSpec((tm, D_MODEL), lambda i: (i, 0)),
            pl.BlockSpec((tm, N_MAIN), lambda i: (i, 0)),
            pl.BlockSpec((8, tm), lambda i: (0, i)),
        ],
        out_shape=[
            jax.ShapeDtypeStruct((t, D_MODEL), F32),
            jax.ShapeDtypeStruct((t, N_MAIN), F32),
            jax.ShapeDtypeStruct((8, t), F32),
        ],
        scratch_shapes=[pltpu.VMEM((HALO, N_MAIN), F32)],
        compiler_params=_params(("arbitrary",)),
        name="ln_inproj",
    )(x2d, g, b, w_main, w_small, conv_w, conv_b)


def _lru_tile(xc, gate, wg, bg, lam, og, carry):
    rows = xc.shape[0]
    gates = _dot(xc.astype(BF16), wg) + bg
    r = _sigmoid(gates[:, :LRU_WIDTH])
    i = _sigmoid(gates[:, LRU_WIDTH:])
    log_a = (-LRU_C) * r * _softplus(-lam)
    a = jnp.exp(log_a)
    one_minus_a2 = -jnp.tanh(log_a) * (a * a + 1.0)
    mult = jnp.where(one_minus_a2 > 0.0, one_minus_a2 * lax.rsqrt(one_minus_a2), 0.0)
    bv = mult * (i * xc)
    a = a.reshape(rows // SCAN_GROUP, SCAN_GROUP, LRU_WIDTH)
    bv = bv.reshape(rows // SCAN_GROUP, SCAN_GROUP, LRU_WIDTH)
    row_in_group = lax.broadcasted_iota(jnp.int32, a.shape, 1)
    d = 1
    while d < SCAN_GROUP:
        a_sh = jnp.where(row_in_group < d, 1.0, pltpu.roll(a, d, 1))
        b_sh = jnp.where(row_in_group < d, 0.0, pltpu.roll(bv, d, 1))
        bv = a * b_sh + bv
        a = a * a_sh
        d *= 2
    a = a.reshape(rows, LRU_WIDTH)
    bv = bv.reshape(rows, LRU_WIDTH)
    parts = []
    for g in range(rows // SCAN_GROUP):
        grp = slice(g * SCAN_GROUP, (g + 1) * SCAN_GROUP)
        hg = a[grp] * carry + bv[grp]
        carry = hg[SCAN_GROUP - 1:, :]
        parts.append(hg)
    h = jnp.concatenate(parts, axis=0)
    y = h * _gelu_tanh(gate)
    ms = jnp.mean(y * y, -1, keepdims=True)
    return y * lax.rsqrt(ms + NORM_EPS) * og, carry


def _bdot(a, b, dims=NN):
    return _dot(a.astype(BF16), b.astype(BF16), dims)


def _gdn_heads(args, norm_w):
    c = GDN_CHUNK
    r = GDN_GROUP * c
    ri = lax.broadcasted_iota(jnp.int32, (r, r), 0)
    ci = lax.broadcasted_iota(jnp.int32, (r, r), 1)
    same = (ri // c) == (ci // c)
    causal = same & (ri >= ci)
    strict = same & (ri > ci)
    upper = same & (ri <= ci)
    chunk_of_row = lax.broadcasted_iota(jnp.int32, (r, 1), 0) // c
    each = lambda f, *ls: [f(*xs) for xs in zip(*ls)]
    q, k, v, z, beta, g_col, g_row, st = [list(x) for x in zip(*args)]
    q = each(lambda x: x * lax.rsqrt(jnp.sum(x * x, -1, keepdims=True) + NORM_EPS) * (GDN_DK ** -0.5), q)
    k = each(lambda x: x * lax.rsqrt(jnp.sum(x * x, -1, keepdims=True) + NORM_EPS), k)
    gc_col = each(lambda g: jnp.sum(jnp.where(causal, g, 0.0), axis=1, keepdims=True), g_row)
    gc_row = each(lambda g: jnp.sum(jnp.where(upper, g, 0.0), axis=0, keepdims=True), g_col)
    decay = each(lambda gc, gr: jnp.exp(jnp.where(causal, gc - gr, -jnp.inf)), gc_col, gc_row)
    kb = each(lambda x, bt: x * bt, k, beta)
    vb = each(lambda x, bt: x * bt, v, beta)
    kk = each(lambda x, y: _bdot(x, y, NT), kb, k)
    a_mat = each(lambda m, d: jnp.where(strict, m * d, 0.0), kk, decay)
    e_col = each(jnp.exp, gc_col)
    rhs = each(lambda x, y, e: jnp.concatenate([x, y * e], axis=1), vb, kb, e_col)
    eye = (ri == ci).astype(F32)
    t_mat = each(lambda a: eye - a, a_mat)
    p = a_mat
    for _ in range(5):
        p = each(lambda x: _bdot(x, x), p)
        t_mat = each(lambda tm_, x: tm_ + _bdot(tm_, x), t_mat, p)
    sol = each(lambda tm_, rr: _bdot(tm_, rr), t_mat, rhs)
    qk = each(lambda x, y: _bdot(x, y, NT), q, k)
    qk = each(lambda m, d: jnp.where(causal, m * d, 0.0), qk, decay)
    q_dec = each(lambda x, e: x * e, q, e_col)
    g_last = [each(lambda gc: gc[(j + 1) * c - 1:(j + 1) * c, :], gc_col) for j in range(GDN_GROUP)]

    def last_of_own_chunk(*gl):
        out = gl[-1]
        for j in range(GDN_GROUP - 2, -1, -1):
            out = jnp.where(chunk_of_row == j, gl[j], out)
        return out

    g_end = each(last_of_own_chunk, *g_last)
    k_dec = each(lambda x, ge, gc: x * jnp.exp(ge - gc), k, g_end, gc_col)
    qs_parts, v_parts = [], []
    for j in range(GDN_GROUP):
        rows = slice(j * c, (j + 1) * c)
        ws = each(lambda x, s: _bdot(x[rows, GDN_DV:], s), sol, st)
        qs_parts.append(each(lambda x, s: _bdot(x[rows], s), q_dec, st))
        v_new = each(lambda x, w: x[rows, :GDN_DV] - w, sol, ws)
        v_parts.append(v_new)
        kv = each(lambda x, vn: _bdot(x[rows], vn, TN), k_dec, v_new)
        st = each(lambda s, gl, d: s * jnp.exp(gl) + d, st, g_last[j], kv)
    qs = each(lambda *parts: jnp.concatenate(parts, axis=0), *qs_parts)
    v_all = each(lambda *parts: jnp.concatenate(parts, axis=0), *v_parts)
    o = each(lambda a, m, vn: a + _bdot(m, vn), qs, qk, v_all)
    o = each(lambda x: x * lax.rsqrt(jnp.mean(x * x, -1, keepdims=True) + NORM_EPS) * norm_w, o)
    o = each(lambda x, zz: x * _silu(zz), o, z)
    return list(zip(o, st))


def _mixer_kernel(xc_ref, gate_ref, q_ref, k_ref, v_ref, z_ref, smt_ref,
                  wg_ref, bg_ref, lam_ref, og_ref, alc_ref, dtc_ref, nw_ref,
                  ylru_ref, y_ref, hcarry, state):
    n = pl.program_id(1)
    c = GDN_GROUP * GDN_CHUNK
    nb = q_ref.shape[0]
    first = n == 0

    @pl.when(first)
    def _():
        state[...] = jnp.zeros_like(state)
        hcarry[...] = jnp.zeros_like(hcarry)

    lru_out = [_lru_tile(xc_ref[b], gate_ref[b], wg_ref[...], bg_ref[...], lam_ref[...], og_ref[...], hcarry[b])
               for b in range(nb)]
    norm_w = nw_ref[...]

    args = []
    for b in range(nb):
        q_all = _silu(q_ref[b])
        k_all = _silu(k_ref[b])
        v_all = _silu(v_ref[b])
        z_all = z_ref[b]
        smt = smt_ref[b]
        beta_rows = _sigmoid(smt)
        g_rows = -jnp.exp(alc_ref[...]) * _softplus(smt + dtc_ref[...])
        head_row = lax.broadcasted_iota(jnp.int32, smt.shape, 0)
        stacked = jnp.where(head_row < GDN_HEADS, beta_rows, g_rows)
        cols = jnp.concatenate([stacked, jnp.zeros((LANES - 8, c), F32)], axis=0).T
        for hd in range(GDN_HEADS):
            sl = slice(hd * GDN_DK, (hd + 1) * GDN_DK)
            args.append((q_all[:, sl], k_all[:, sl], v_all[:, sl], z_all[:, sl],
                         cols[:, hd:hd + 1],
                         cols[:, GDN_HEADS + hd:GDN_HEADS + hd + 1],
                         g_rows[GDN_HEADS + hd:GDN_HEADS + hd + 1, :],
                         state[b, hd]))
    outs = _gdn_heads(args, norm_w)
    for b in range(nb):
        for hd in range(GDN_HEADS):
            o, st_new = outs[b * GDN_HEADS + hd]
            state[b, hd] = st_new
            y_ref[b, :, hd * GDN_DK:(hd + 1) * GDN_DK] = o
    for b, (y_lru, carry) in enumerate(lru_out):
        ylru_ref[b] = y_lru
        hcarry[b] = carry


def _mixers(proj3, smallt3, w_gates, b_gates, lam, out_g, alc, dtc, norm_w, nb):
    bsz, seq, _ = proj3.shape
    c = GDN_GROUP * GDN_CHUNK
    nch = seq // c
    col = lambda j: pl.BlockSpec((nb, c, GDN_QK), lambda b, n: (b, n, j))
    const = lambda shape: pl.BlockSpec(shape, lambda b, n: (0,) * len(shape))
    return pl.pallas_call(
        _mixer_kernel,
        grid=(bsz // nb, nch),
        in_specs=[
            col(0), col(1), col(2), col(3), col(4), col(5),
            pl.BlockSpec((nb, None, 8, c), lambda b, n: (b, n, 0, 0)),
            const((LRU_WIDTH, 2 * LRU_WIDTH)), const((1, 2 * LRU_WIDTH)), const((1, LRU_WIDTH)), const((1, LRU_WIDTH)),
            const((8, 1)), const((8, 1)),
            const((1, GDN_DV)),
        ],
        out_specs=[pl.BlockSpec((nb, c, LRU_WIDTH), lambda b, n: (b, n, 0)),
                   pl.BlockSpec((nb, c, GDN_V), lambda b, n: (b, n, 0))],
        out_shape=[jax.ShapeDtypeStruct((bsz, seq, LRU_WIDTH), F32),
                   jax.ShapeDtypeStruct((bsz, seq, GDN_V), F32)],
        scratch_shapes=[
            pltpu.VMEM((nb, 1, LRU_WIDTH), F32),
            pltpu.VMEM((nb, GDN_HEADS, GDN_DK, GDN_DV), F32),
        ],
        compiler_params=_params(("arbitrary", "arbitrary")),
        name="mixers",
    )(proj3, proj3, proj3, proj3, proj3, proj3, smallt3, w_gates, b_gates, lam, out_g, alc, dtc, norm_w)


def _pick_experts(logits, rbias):
    n = logits.shape[1]
    scores = _sigmoid(logits)
    choice = scores + rbias
    neg = -jnp.inf
    gs_rows = []
    sub = lax.broadcasted_iota(jnp.int32, (GROUP_SIZE, n), 0).astype(F32)
    for g in range(N_GROUPS):
        cg = choice[g * GROUP_SIZE:(g + 1) * GROUP_SIZE, :]
        m1 = jnp.max(cg, axis=0, keepdims=True)
        i1 = jnp.min(jnp.where(cg == m1, sub, float(GROUP_SIZE)), axis=0, keepdims=True)
        m2 = jnp.max(jnp.where(sub == i1, neg, cg), axis=0, keepdims=True)
        gs_rows.append(m1 + m2)
    gs = jnp.concatenate(gs_rows, axis=0)
    gi = lax.broadcasted_iota(jnp.int32, (N_GROUPS, n), 0).astype(F32)
    gsel = jnp.zeros((N_GROUPS, n), jnp.bool_)
    for _ in range(TOPK_GROUPS):
        m = jnp.max(gs, axis=0, keepdims=True)
        idx = jnp.min(jnp.where(gs == m, gi, float(N_GROUPS)), axis=0, keepdims=True)
        hit = gi == idx
        gsel = jnp.logical_or(gsel, hit)
        gs = jnp.where(hit, neg, gs)
    masked = jnp.concatenate(
        [jnp.where(gsel[g:g + 1, :], choice[g * GROUP_SIZE:(g + 1) * GROUP_SIZE, :], neg)
         for g in range(N_GROUPS)], axis=0)
    ei = lax.broadcasted_iota(jnp.int32, (N_EXPERTS, n), 0).astype(F32)
    hits, e_rows, w_rows = [], [], []
    multi = jnp.zeros((N_EXPERTS, n), F32)
    for _ in range(TOP_K):
        m = jnp.max(masked, axis=0, keepdims=True)
        idx = jnp.min(jnp.where(masked == m, ei, float(N_EXPERTS)), axis=0, keepdims=True)
        hit = ei == idx
        hits.append(hit)
        e_rows.append(idx)
        w_rows.append(jnp.sum(jnp.where(hit, scores, 0.0), axis=0, keepdims=True))
        multi = multi + hit.astype(F32)
        masked = jnp.where(hit, neg, masked)
    wts = jnp.concatenate(w_rows, axis=0)
    wts = wts / (jnp.sum(wts, axis=0, keepdims=True) + 1e-20) * ROUTED_SCALE
    return jnp.concatenate(e_rows, axis=0), wts, hits, multi


def _router_kernel(yl_ref, yg_ref, h0_ref, wo1_ref, wo2_ref, g_ref, b_ref, wrt_ref, rb_ref,
                   h1_ref, h1p_ref, e_ref, w_ref, rank_ref, cnt_ref, carry):
    i = pl.program_id(0)
    tm = h0_ref.shape[0]
    n = min(ROUTER_SUB, tm)
    subs = [slice(j * n, (j + 1) * n) for j in range(tm // n)]

    @pl.when(i == 0)
    def _():
        carry[...] = jnp.zeros_like(carry)

    mixes = [_dot(yl_ref[r, :].astype(BF16), wo1_ref[...]) + _dot(yg_ref[r, :].astype(BF16), wo2_ref[...])
             for r in subs]
    h1s = [_layer_norm(DEEPNORM_ALPHA * h0_ref[r, :] + mix, g_ref[...], b_ref[...])
           for r, mix in zip(subs, mixes)]
    for r, h1 in zip(subs, h1s):
        h1_ref[r, :] = h1
        h1p_ref[r, :] = _pack_rows(h1)
    logits = [_dot3(wrt_ref[...], h1, NT) for h1 in h1s]
    picks = [_pick_experts(lg, rb_ref[...]) for lg in logits]
    ti = lax.broadcasted_iota(jnp.int32, (n, n), 0)
    tj = lax.broadcasted_iota(jnp.int32, (n, n), 1)
    before = (ti < tj).astype(BF16)
    cums = [_dot(multi.astype(BF16), before) for _, _, _, multi in picks]
    base = carry[...]
    for r, (e_rows, wts, hits, multi), cum in zip(subs, picks, cums):
        cum = cum + base
        r_rows = [jnp.sum(jnp.where(hit, cum, 0.0), axis=0, keepdims=True) for hit in hits]
        base = base + jnp.sum(multi, axis=1, keepdims=True)
        e_ref[:, r] = e_rows.astype(jnp.int32)
        w_ref[:, r] = wts
        rank_ref[:, r] = jnp.concatenate(r_rows, axis=0).astype(jnp.int32)
    carry[...] = base
    cnt_ref[...] = base.astype(jnp.int32)


def _router(y_lru, y_gdn, h0, wo1, wo2, g, b, w_router_t, rbias, tm):
    t = h0.shape[0]
    const = lambda shape: pl.BlockSpec(shape, lambda i: (0,) * len(shape))
    return pl.pallas_call(
        _router_kernel,
        grid=(t // tm,),
        in_specs=[
            pl.BlockSpec((tm, LRU_WIDTH), lambda i: (i, 0)),
            pl.BlockSpec((tm, GDN_V), lambda i: (i, 0)),
            pl.BlockSpec((tm, D_MODEL), lambda i: (i, 0)),
            const((LRU_WIDTH, D_MODEL)), const((GDN_V, D_MODEL)),
            const((1, D_MODEL)), const((1, D_MODEL)),
            const((N_EXPERTS, D_MODEL)), const((N_EXPERTS, 1)),
        ],
        out_specs=[
            pl.BlockSpec((tm, D_MODEL), lambda i: (i, 0)),
            pl.BlockSpec((tm, D_PACK), lambda i: (i, 0)),
            pl.BlockSpec((TOP_K, tm), lambda i: (0, i)),
            pl.BlockSpec((TOP_K, tm), lambda i: (0, i)),
            pl.BlockSpec((TOP_K, tm), lambda i: (0, i)),
            const((N_EXPERTS, 1)),
        ],
        out_shape=[
            jax.ShapeDtypeStruct((t, D_MODEL), F32),
            jax.ShapeDtypeStruct((t, D_PACK), jnp.uint32),
            jax.ShapeDtypeStruct((TOP_K, t), jnp.int32),
            jax.ShapeDtypeStruct((TOP_K, t), F32),
            jax.ShapeDtypeStruct((TOP_K, t), jnp.int32),
            jax.ShapeDtypeStruct((N_EXPERTS, 1), jnp.int32),
        ],
        scratch_shapes=[pltpu.VMEM((N_EXPERTS, 1), F32)],
        compiler_params=_params(("arbitrary",)),
        name="outproj_router",
    )(y_lru, y_gdn, h0, wo1, wo2, g, b, w_router_t, rbias)


def _dest_kernel(e_ref, r_ref, ps_ref, d_ref):
    tm = e_ref.shape[1]
    ei = lax.broadcasted_iota(jnp.int32, (N_EXPERTS, tm), 0)
    rows = []
    for k in range(TOP_K):
        hit = ei == e_ref[k:k + 1, :]
        rows.append(jnp.sum(jnp.where(hit, ps_ref[...], 0), axis=0, keepdims=True))
    d_ref[...] = jnp.concatenate(rows, axis=0) + r_ref[...]


def _dest(top_e, rank, pad_start, tm):
    t = top_e.shape[1]
    blk = pl.BlockSpec((TOP_K, tm), lambda i: (0, i))
    return pl.pallas_call(
        _dest_kernel,
        grid=(t // tm,),
        in_specs=[blk, blk, pl.BlockSpec((N_EXPERTS, 1), lambda i: (0, 0))],
        out_specs=blk,
        out_shape=jax.ShapeDtypeStruct((TOP_K, t), jnp.int32),
        compiler_params=_params(("arbitrary",)),
        name="moe_dest",
    )(top_e, rank, pad_start)


def _sc_scatter_rows(rows, idx, n_out, chunk):
    n_copies, t = idx.shape
    d = rows.shape[1]
    per_worker = t // SC_WORKERS
    n_chunks = per_worker // chunk
    mesh = plsc.VectorSubcoreMesh(core_axis_name="c", subcore_axis_name="s")
    idx_flat = idx.reshape(n_copies * t)

    @functools.partial(
        pl.kernel, mesh=mesh,
        out_type=jax.ShapeDtypeStruct((n_out, d), rows.dtype),
        scratch_types=[pltpu.VMEM((chunk,), jnp.int32) for _ in range(n_copies)] + [
            pltpu.VMEM((chunk, d), rows.dtype),
            pltpu.SemaphoreType.DMA,
        ],
    )
    def scatter(rows_hbm, idx_hbm, out_hbm, *scratch):
        idx_v = scratch[:n_copies]
        rows_v, sem = scratch[n_copies:]
        wid = lax.axis_index("s") * SC_CORES + lax.axis_index("c")
        base = wid * per_worker

        @pl.loop(0, n_chunks)
        def _(j):
            off = base + j * chunk
            for k in range(n_copies):
                pltpu.sync_copy(idx_hbm.at[pl.ds(k * t + off, chunk)], idx_v[k])
            pltpu.sync_copy(rows_hbm.at[pl.ds(off, chunk)], rows_v)
            copies = [pltpu.async_copy(rows_v, out_hbm.at[idx_v[k]], sem) for k in range(n_copies)]
            for cp in copies:
                cp.wait()

    return scatter(rows, idx_flat)


def _expert_kernel(be_ref, nv_ref, first_ref, slot_ref, next_ref, nu_ref,
                   xs_ref, wg_hbm, wu_hbm, wd_hbm, ys_ref, wg_f, wu_f, wd_f, wgu_b, wd_b, sem):
    i = pl.program_id(0)

    def fetch(e, slot):
        return (pltpu.make_async_copy(wg_hbm.at[e], wg_f.at[slot], sem.at[slot]),
                pltpu.make_async_copy(wu_hbm.at[e], wu_f.at[slot], sem.at[slot]),
                pltpu.make_async_copy(wd_hbm.at[e], wd_f.at[slot], sem.at[slot]))

    @pl.when(i < nu_ref[0])
    def _():
        e = be_ref[i]
        slot = slot_ref[i]

        @pl.when(first_ref[i] == 1)
        def _():
            @pl.when(i == 0)
            def _():
                for cp in fetch(e, slot):
                    cp.start()

            for cp in fetch(e, slot):
                cp.wait()

            @pl.when(next_ref[i] >= 0)
            def _():
                for cp in fetch(next_ref[i], 1 - slot):
                    cp.start()

            wgu_b[:, :D_EXPERT] = wg_f[slot].astype(BF16)
            wgu_b[:, D_EXPERT:] = wu_f[slot].astype(BF16)
            wd_b[...] = wd_f[slot].astype(BF16)

        n = xs_ref.shape[0] // EXPERT_BANDS
        bands = [slice(j * n, (j + 1) * n) for j in range(EXPERT_BANDS)]
        row = lax.broadcasted_iota(jnp.int32, (n, D_PACK), 0)
        xs = [_unpack_rows(jnp.where(row + j * n < nv_ref[i], xs_ref[r, :], jnp.uint32(0)))
              for j, r in enumerate(bands)]
        gus = [_dot(x_hi.astype(BF16), wgu_b[:D_PACK, :]) + _dot(x_lo.astype(BF16), wgu_b[D_PACK:, :])
               for x_hi, x_lo in xs]
        hs = [_silu(gu[:, :D_EXPERT]) * gu[:, D_EXPERT:] for gu in gus]
        ys = [_dot(h.astype(BF16), wd_b[...]) for h in hs]
        for r, y in zip(bands, ys):
            ys_ref[r, :] = _pack_rows(y)


def _experts(blk_e, n_valid, first, slot, next_e, n_used, xs, w_gate, w_up, w_down):
    n_rows = xs.shape[0]
    n_blocks = n_rows // MOE_BLOCK
    blk = lambda i, be, nv, fi, sl, nx, nu: (jnp.minimum(i, nu[0] - 1), 0)
    return pl.pallas_call(
        _expert_kernel,
        grid_spec=pltpu.PrefetchScalarGridSpec(
            num_scalar_prefetch=6,
            grid=(n_used[0],),
            in_specs=[
                pl.BlockSpec((MOE_BLOCK, D_PACK), blk),
                pl.BlockSpec(memory_space=pl.ANY),
                pl.BlockSpec(memory_space=pl.ANY),
                pl.BlockSpec(memory_space=pl.ANY),
            ],
            out_specs=pl.BlockSpec((MOE_BLOCK, D_PACK), blk),
            scratch_shapes=[
                pltpu.VMEM((2, D_MODEL, D_EXPERT), F32),
                pltpu.VMEM((2, D_MODEL, D_EXPERT), F32),
                pltpu.VMEM((2, D_EXPERT, D_MODEL), F32),
                pltpu.VMEM((D_MODEL, 2 * D_EXPERT), BF16),
                pltpu.VMEM((D_EXPERT, D_MODEL), BF16),
                pltpu.SemaphoreType.DMA((2,)),
            ],
        ),
        out_shape=jax.ShapeDtypeStruct((n_rows, D_PACK), jnp.uint32),
        compiler_params=_params(("arbitrary",)),
        name="moe_experts",
    )(blk_e, n_valid, first, slot, next_e, n_used, xs, w_gate, w_up, w_down)


def _sc_gather_rows(table, idx, chunk):
    n_idx = idx.shape[0]
    d = table.shape[1]
    per_worker = n_idx // SC_WORKERS
    n_chunks = per_worker // chunk
    assert n_chunks % 2 == 0 and n_chunks * chunk * SC_WORKERS == n_idx
    mesh = plsc.VectorSubcoreMesh(core_axis_name="c", subcore_axis_name="s")

    @functools.partial(
        pl.kernel, mesh=mesh,
        out_type=jax.ShapeDtypeStruct((n_idx, d), table.dtype),
        scratch_types=[
            pltpu.VMEM((chunk,), jnp.int32), pltpu.VMEM((chunk,), jnp.int32),
            pltpu.VMEM((chunk, d), table.dtype), pltpu.VMEM((chunk, d), table.dtype),
            pltpu.SemaphoreType.DMA, pltpu.SemaphoreType.DMA, pltpu.SemaphoreType.DMA, pltpu.SemaphoreType.DMA,
        ],
    )
    def gather(table_hbm, idx_hbm, out_hbm, idx_v0, idx_v1, rows_v0, rows_v1, gsem0, gsem1, osem0, osem1):
        idx_v, rows_v, gsem, osem = (idx_v0, idx_v1), (rows_v0, rows_v1), (gsem0, gsem1), (osem0, osem1)
        wid = lax.axis_index("s") * SC_CORES + lax.axis_index("c")
        base = wid * per_worker

        def gather_copy(slot):
            return pltpu.make_async_copy(table_hbm.at[idx_v[slot]], rows_v[slot], gsem[slot])

        def out_copy(c, slot):
            return pltpu.make_async_copy(rows_v[slot], out_hbm.at[pl.ds(base + c * chunk, chunk)], osem[slot])

        def start_gather(c, slot):
            pltpu.sync_copy(idx_hbm.at[pl.ds(base + c * chunk, chunk)], idx_v[slot])
            gather_copy(slot).start()

        start_gather(0, 0)

        @pl.loop(0, n_chunks, step=2)
        def _(j):
            for b in range(2):
                c = j + b
                cur, other = b, 1 - b

                @pl.when(c >= 1)
                def _():
                    out_copy(c - 1, other).wait()

                @pl.when(c + 1 < n_chunks)
                def _():
                    start_gather(c + 1, other)

                gather_copy(cur).wait()
                out_copy(c, cur).start()

        out_copy(n_chunks - 1, 1).wait()

    return gather(table, idx)


def _combine_kernel(h1_ref, wts_ref, wsgu_ref, wsd_ref, g_ref, b_ref, yg_ref, out_ref):
    h1 = h1_ref[...]
    gu = _dot(h1.astype(BF16), wsgu_ref[...])
    hs = _silu(gu[:, :D_SHARED]) * gu[:, D_SHARED:]
    acc = DEEPNORM_ALPHA * h1 + _dot(hs.astype(BF16), wsd_ref[...])
    wts = wts_ref[...].T
    acc_hi = acc[:, :D_PACK]
    acc_lo = acc[:, D_PACK:]
    for k in range(TOP_K):
        y_hi, y_lo = _unpack_rows(yg_ref[k])
        acc_hi = acc_hi + y_hi * wts[:, k:k + 1]
        acc_lo = acc_lo + y_lo * wts[:, k:k + 1]
    out_ref[...] = _layer_norm(jnp.concatenate([acc_hi, acc_lo], axis=1), g_ref[...], b_ref[...])


def _combine(h1, wts, ws_gu, ws_down, g, b, yg, tm):
    t = h1.shape[0]
    const = lambda shape: pl.BlockSpec(shape, lambda i: (0,) * len(shape))
    return pl.pallas_call(
        _combine_kernel,
        grid=(t // tm,),
        in_specs=[
            pl.BlockSpec((tm, D_MODEL), lambda i: (i, 0)),
            pl.BlockSpec((TOP_K, tm), lambda i: (0, i)),
            const((D_MODEL, 2 * D_SHARED)), const((D_SHARED, D_MODEL)),
            const((1, D_MODEL)), const((1, D_MODEL)),
            pl.BlockSpec((TOP_K, tm, D_PACK), lambda i: (0, i, 0)),
        ],
        out_specs=pl.BlockSpec((tm, D_MODEL), lambda i: (i, 0)),
        out_shape=jax.ShapeDtypeStruct((t, D_MODEL), F32),
        compiler_params=_params(("arbitrary",)),
        name="moe_combine",
    )(h1, wts, ws_gu, ws_down, g, b, yg)


def _block_diag(w):
    nb, bi, bo = w.shape
    eye = jnp.eye(nb, dtype=w.dtype)
    return (eye[:, None, :, None] * w[:, :, None, :]).reshape(nb * bi, nb * bo)


def _pad_lanes(v, offset, width):
    return jnp.zeros((1, width), F32).at[0, offset:offset + v.shape[0]].set(v)


def _layer(h_in_x, l, p, tiles):
    bsz, seq, _ = h_in_x.shape
    t = bsz * seq
    row = lambda v: v.reshape(1, -1)

    w_in = p['w_in'][l]
    w_main = w_in.astype(BF16)
    w_small = jnp.zeros((D_MODEL, LANES), F32).at[:, :2 * GDN_HEADS].set(w_in[:, N_MAIN:])
    zeros = lambda n: jnp.zeros((CONV_WIDTH, n), F32)
    conv_w = jnp.concatenate([p['lru_conv_w'][l], zeros(LRU_WIDTH), p['gdn_conv_w'][l], zeros(GDN_V)], 1)
    conv_b = jnp.zeros((1, N_MAIN), F32).at[0, :LRU_WIDTH].set(p['lru_conv_b'][l])
    h0, proj, small_t = _inproj(h_in_x.reshape(t, D_MODEL), row(p['ln_g']), row(p['ln_b']),
                                       w_main, w_small, conv_w, conv_b, tiles['inproj'], seq)
    proj3 = proj.reshape(bsz, seq, N_MAIN)

    w_gates = jnp.concatenate([_block_diag(p['lru_w_rg'][l]), _block_diag(p['lru_w_ig'][l])], 1).astype(BF16)
    b_gates = jnp.concatenate([p['lru_b_rg'][l], p['lru_b_ig'][l]]).reshape(1, -1)
    rows = GDN_GROUP * GDN_CHUNK
    smallt3 = small_t.reshape(8, bsz, seq // rows, rows).transpose(1, 2, 0, 3)
    a_log, dt_bias = p['gdn_a_log'][l], p['gdn_dt_bias'][l]
    alc = _pad_lanes(a_log, GDN_HEADS, 8).reshape(8, 1)
    dtc = _pad_lanes(dt_bias, GDN_HEADS, 8).reshape(8, 1)
    y_lru, y_gdn = _mixers(proj3, smallt3, w_gates, b_gates, row(p['lru_lambda'][l]), row(p['lru_out_g'][l]),
                           alc, dtc, row(p['gdn_norm_w'][l]), tiles['gdn_nb'])

    w_out = p['w_out'][l].astype(BF16)
    h1, h1p, top_e, wts, rank, counts = _router(
        y_lru.reshape(t, LRU_WIDTH), y_gdn.reshape(t, GDN_V), h0, w_out[:LRU_WIDTH], w_out[LRU_WIDTH:],
        row(p['ln1_g'][l]), row(p['ln1_b'][l]), p['w_router'][l].T, p['router_bias'][l].reshape(-1, 1),
        tiles['router'])

    counts = counts[:, 0]
    padded = (counts + MOE_BLOCK - 1) // MOE_BLOCK * MOE_BLOCK
    pad_end = jnp.cumsum(padded)
    pad_start = pad_end - padded
    n_blocks = (t * TOP_K + N_EXPERTS * (MOE_BLOCK - 1)) // MOE_BLOCK
    n_rows = n_blocks * MOE_BLOCK
    n_used = (pad_end[-1] // MOE_BLOCK).astype(jnp.int32)
    blk_ids = jnp.minimum(jnp.arange(n_blocks, dtype=jnp.int32), n_used - 1)
    blk_e = jnp.minimum(jnp.sum(pad_end[None, :] <= (blk_ids * MOE_BLOCK)[:, None], axis=1),
                        N_EXPERTS - 1).astype(jnp.int32)

    dest = _dest(top_e, rank, pad_start.reshape(-1, 1), tiles['dest'])
    n_valid = jnp.clip(counts[blk_e] - (blk_ids * MOE_BLOCK - pad_start[blk_e]), 0, MOE_BLOCK).astype(jnp.int32)
    xs = _sc_scatter_rows(h1p, dest, n_rows, SC_SCATTER_CHUNK)
    active = jnp.arange(n_blocks, dtype=jnp.int32) < n_used
    first = (active & jnp.concatenate([jnp.ones((1,), bool), blk_e[1:] != blk_e[:-1]])).astype(jnp.int32)
    slot = ((jnp.cumsum(first) - 1) % 2).astype(jnp.int32)
    used = counts > 0
    later = jnp.where(used[None, :] & (jnp.arange(N_EXPERTS)[None, :] > jnp.arange(N_EXPERTS)[:, None]),
                      jnp.arange(N_EXPERTS, dtype=jnp.int32)[None, :], N_EXPERTS)
    next_used = jnp.min(later, axis=1)
    next_e = jnp.where(next_used < N_EXPERTS, next_used, -1)[blk_e].astype(jnp.int32)
    ys = _experts(blk_e, n_valid, first, slot, next_e, n_used.reshape(1), xs,
                  p['w_gate'][l], p['w_up'][l], p['w_down'][l])
    ws_gu = jnp.concatenate([p['ws_gate'][l], p['ws_up'][l]], 1).astype(BF16)
    yg = _sc_gather_rows(ys, dest.reshape(TOP_K * t), SC_GATHER_CHUNK).reshape(TOP_K, t, D_PACK)
    out = _combine(h1, wts, ws_gu, p['ws_down'][l].astype(BF16),
                   row(p['ln2_g'][l]), row(p['ln2_b'][l]), yg, tiles['combine'])
    return out.reshape(bsz, seq, D_MODEL)


def _tiles(bsz, seq):
    t = bsz * seq
    return {
        'inproj': min(512, t),
        'gdn_nb': bsz,
        'router': min(1024, t),
        'dest': min(512, t),
        'combine': min(512, t),
    }


def kernel(x, ln_in_g, ln_in_b, w_in, lru_conv_w, lru_conv_b, lru_w_rg, lru_b_rg, lru_w_ig, lru_b_ig,
           lru_lambda, lru_out_g, gdn_conv_w, gdn_a_log, gdn_dt_bias, gdn_norm_w, w_out, ln1_g, ln1_b,
           w_router, router_bias, w_gate, w_up, w_down, ws_gate, ws_up, ws_down, ln2_g, ln2_b):
    assert w_in.shape[0] == DEPTH == 1
    p = dict(ln_g=ln_in_g, ln_b=ln_in_b, w_in=w_in, lru_conv_w=lru_conv_w, lru_conv_b=lru_conv_b,
             lru_w_rg=lru_w_rg, lru_b_rg=lru_b_rg, lru_w_ig=lru_w_ig, lru_b_ig=lru_b_ig,
             lru_lambda=lru_lambda, lru_out_g=lru_out_g, gdn_conv_w=gdn_conv_w, gdn_a_log=gdn_a_log,
             gdn_dt_bias=gdn_dt_bias, gdn_norm_w=gdn_norm_w, w_out=w_out, ln1_g=ln1_g, ln1_b=ln1_b,
             w_router=w_router, router_bias=router_bias, w_gate=w_gate, w_up=w_up, w_down=w_down,
             ws_gate=ws_gate, ws_up=ws_up, ws_down=ws_down, ln2_g=ln2_g, ln2_b=ln2_b)
    bsz, seq, _ = x.shape
    return _layer(x, 0, p, _tiles(bsz, seq))
```

```python
import functools

import jax
import jax.numpy as jnp
from jax import lax
from jax.experimental import pallas as pl
from jax.experimental.pallas import tpu as pltpu
from jax.experimental.pallas import tpu_sc as plsc

F32 = jnp.float32
BF16 = jnp.bfloat16

D_MODEL = 1024
LRU_WIDTH = 512
LRU_C = 8.0
CONV_WIDTH = 4
GDN_HEADS = 4
GDN_DK = 128
GDN_DV = 128
GDN_CHUNK = 64
GDN_GROUP = 2
GDN_QK = GDN_HEADS * GDN_DK
GDN_V = GDN_HEADS * GDN_DV
N_MAIN = 2 * LRU_WIDTH + 2 * GDN_QK + 2 * GDN_V
N_EXPERTS = 256
TOP_K = 8
N_GROUPS = 8
GROUP_SIZE = N_EXPERTS // N_GROUPS
TOPK_GROUPS = 4
D_EXPERT = 256
D_SHARED = 256
ROUTED_SCALE = 2.5
MOE_BLOCK = 640
D_PACK = D_MODEL // 2
LN_EPS = 1e-5
NORM_EPS = 1e-6
DEPTH = 1
DEEPNORM_ALPHA = (2.0 * DEPTH) ** 0.25

SCAN_GROUP = 8
HALO = 8
CONV_GROUP = 512
CONV_GROUPS = (0, 2, 3, 4)
LANES = 128
VMEM_LIMIT = 56 * 1024 * 1024
ROUTER_SUB = 256
EXPERT_BANDS = 5
SC_CORES = 2
SC_WORKERS = 32
SC_GATHER_CHUNK = 64
SC_SCATTER_CHUNK = 128

NN = (((1,), (0,)), ((), ()))
NT = (((1,), (1,)), ((), ()))
TN = (((0,), (0,)), ((), ()))


def _dot(a, b, dims=NN):
    return lax.dot_general(a, b, dims, preferred_element_type=F32)


def _split(a):
    hi = a.astype(BF16)
    lo = (a - hi.astype(F32)).astype(BF16)
    return hi, lo


def _dot3(a, b, dims=NN):
    ah, al = _split(a)
    bh, bl = _split(b)
    return _dot(ah, bh, dims) + (_dot(ah, bl, dims) + _dot(al, bh, dims))


def _layer_norm(x, g, b):
    mu = jnp.mean(x, -1, keepdims=True)
    xc = x - mu
    var = jnp.mean(xc * xc, -1, keepdims=True)
    return xc * lax.rsqrt(var + LN_EPS) * g + b


def _sigmoid(x):
    return 0.5 * jnp.tanh(0.5 * x) + 0.5


def _silu(x):
    return x * _sigmoid(x)


def _softplus(x):
    return jnp.maximum(x, 0.0) + jnp.log1p(jnp.exp(-jnp.abs(x)))


def _gelu_tanh(x):
    c = 0.7978845608028654
    return x * (0.5 * (1.0 + jnp.tanh(c * (x + 0.044715 * (x * x * x)))))


def _pack_rows(x):
    hi = lax.bitcast_convert_type(x[:, :D_PACK].astype(BF16).astype(F32), jnp.uint32)
    lo = lax.bitcast_convert_type(x[:, D_PACK:].astype(BF16).astype(F32), jnp.uint32)
    return (hi & jnp.uint32(0xFFFF0000)) | (lo >> 16)


def _unpack_rows(w):
    hi = lax.bitcast_convert_type(w & jnp.uint32(0xFFFF0000), F32)
    lo = lax.bitcast_convert_type(w << 16, F32)
    return hi, lo


def _params(sem, **kw):
    return pltpu.CompilerParams(dimension_semantics=sem, vmem_limit_bytes=VMEM_LIMIT, **kw)


def _inproj_kernel(x_ref, g_ref, b_ref, w_ref, ws_ref, cw_ref, cb_ref,
                   h_ref, proj_ref, smallt_ref, hist, stage, *, tiles_per_seq):
    i = pl.program_id(0)
    tm = x_ref.shape[0]
    h = _layer_norm(x_ref[...], g_ref[...], b_ref[...])
    h_ref[...] = h
    hb = h.astype(BF16)

    @pl.when(i % tiles_per_seq == 0)
    def _():
        hist[...] = jnp.zeros_like(hist)

    for g in range(N_MAIN // CONV_GROUP):
        cols = slice(g * CONV_GROUP, (g + 1) * CONV_GROUP)
        p = _dot(hb, w_ref[:, cols])
        if g in CONV_GROUPS:
            stage[HALO:, :] = p
            stage[:HALO, :] = hist[:, cols]
            acc = cb_ref[:, cols]
            for j in range(CONV_WIDTH):
                off = HALO - (CONV_WIDTH - 1) + j
                acc = acc + stage[off:off + tm, :] * cw_ref[j:j + 1, cols]
            hist[:, cols] = p[tm - HALO:, :]
            p = acc
        proj_ref[:, cols] = p
    smallt_ref[...] = _dot3(h, ws_ref[...]).T[:smallt_ref.shape[0], :]


def _inproj(x2d, g, b, w_main, w_small, conv_w, conv_b, tm, seq):
    t = x2d.shape[0]
    return pl.pallas_call(
        functools.partial(_inproj_kernel, tiles_per_seq=seq // tm),
        grid=(t // tm,),
        in_specs=[
            pl.BlockSpec((tm, D_MODEL), lambda i: (i, 0)),
            pl.BlockSpec((1, D_MODEL), lambda i: (0, 0)),
            pl.BlockSpec((1, D_MODEL), lambda i: (0, 0)),
            pl.BlockSpec((D_MODEL, N_MAIN), lambda i: (0, 0)),
            pl.BlockSpec((D_MODEL, LANES), lambda i: (0, 0)),
            pl.BlockSpec((CONV_WIDTH, N_MAIN), lambda i: (0, 0)),
            pl.BlockSpec((1, N_MAIN), lambda i: (0, 0)),
        ],
        out_specs=[
            pl.BlockSpec((tm, D_MODEL), lambda i: (i, 0)),
            pl.BlockSpec((tm, N_MAIN), lambda i: (i, 0)),
            pl.BlockSpec((8, tm), lambda i: (0, i)),
        ],
        out_shape=[
            jax.ShapeDtypeStruct((t, D_MODEL), F32),
            jax.ShapeDtypeStruct((t, N_MAIN), F32),
            jax.ShapeDtypeStruct((8, t), F32),
        ],
        scratch_shapes=[pltpu.VMEM((HALO, N_MAIN), F32), pltpu.VMEM((HALO + tm, CONV_GROUP), F32)],
        compiler_params=_params(("arbitrary",)),
        name="ln_inproj",
    )(x2d, g, b, w_main, w_small, conv_w, conv_b)


def _lru_tile(xc, gate, wg, bg, lam, og, carry):
    rows = xc.shape[0]
    gates = _dot(xc.astype(BF16), wg) + bg
    r = _sigmoid(gates[:, :LRU_WIDTH])
    i = _sigmoid(gates[:, LRU_WIDTH:])
    log_a = (-LRU_C) * r * _softplus(-lam)
    a = jnp.exp(log_a)
    one_minus_a2 = -jnp.tanh(log_a) * (a * a + 1.0)
    mult = jnp.where(one_minus_a2 > 0.0, one_minus_a2 * lax.rsqrt(one_minus_a2), 0.0)
    bv = mult * (i * xc)
    a = a.reshape(rows // SCAN_GROUP, SCAN_GROUP, LRU_WIDTH)
    bv = bv.reshape(rows // SCAN_GROUP, SCAN_GROUP, LRU_WIDTH)
    row_in_group = lax.broadcasted_iota(jnp.int32, a.shape, 1)
    d = 1
    while d < SCAN_GROUP:
        a_sh = jnp.where(row_in_group < d, 1.0, pltpu.roll(a, d, 1))
        b_sh = jnp.where(row_in_group < d, 0.0, pltpu.roll(bv, d, 1))
        bv = a * b_sh + bv
        a = a * a_sh
        d *= 2
    a = a.reshape(rows, LRU_WIDTH)
    bv = bv.reshape(rows, LRU_WIDTH)
    parts = []
    for g in range(rows // SCAN_GROUP):
        grp = slice(g * SCAN_GROUP, (g + 1) * SCAN_GROUP)
        hg = a[grp] * carry + bv[grp]
        carry = hg[SCAN_GROUP - 1:, :]
        parts.append(hg)
    h = jnp.concatenate(parts, axis=0)
    y = h * _gelu_tanh(gate)
    ms = jnp.mean(y * y, -1, keepdims=True)
    return y * lax.rsqrt(ms + NORM_EPS) * og, carry


def _bdot(a, b, dims=NN):
    return _dot(a.astype(BF16), b.astype(BF16), dims)


def _gdn_heads(args, norm_w):
    c = GDN_CHUNK
    r = GDN_GROUP * c
    ri = lax.broadcasted_iota(jnp.int32, (r, r), 0)
    ci = lax.broadcasted_iota(jnp.int32, (r, r), 1)
    same = (ri // c) == (ci // c)
    causal = same & (ri >= ci)
    strict = same & (ri > ci)
    upper = same & (ri <= ci)
    chunk_of_row = lax.broadcasted_iota(jnp.int32, (r, 1), 0) // c
    each = lambda f, *ls: [f(*xs) for xs in zip(*ls)]
    q, k, v, z, beta, g_col, g_row, st = [list(x) for x in zip(*args)]
    q = each(lambda x: x * lax.rsqrt(jnp.sum(x * x, -1, keepdims=True) + NORM_EPS) * (GDN_DK ** -0.5), q)
    k = each(lambda x: x * lax.rsqrt(jnp.sum(x * x, -1, keepdims=True) + NORM_EPS), k)
    gc_col = each(lambda g: jnp.sum(jnp.where(causal, g, 0.0), axis=1, keepdims=True), g_row)
    gc_row = each(lambda g: jnp.sum(jnp.where(upper, g, 0.0), axis=0, keepdims=True), g_col)
    decay = each(lambda gc, gr: jnp.exp(jnp.where(causal, gc - gr, -jnp.inf)), gc_col, gc_row)
    kb = each(lambda x, bt: x * bt, k, beta)
    vb = each(lambda x, bt: x * bt, v, beta)
    kk = each(lambda x, y: _bdot(x, y, NT), kb, k)
    a_mat = each(lambda m, d: jnp.where(strict, m * d, 0.0), kk, decay)
    e_col = each(jnp.exp, gc_col)
    rhs = each(lambda x, y, e: jnp.concatenate([x, y * e], axis=1), vb, kb, e_col)
    eye = (ri == ci).astype(F32)
    t_mat = each(lambda a: eye - a, a_mat)
    p = a_mat
    for _ in range(5):
        p = each(lambda x: _bdot(x, x), p)
        t_mat = each(lambda tm_, x: tm_ + _bdot(tm_, x), t_mat, p)
    sol = each(lambda tm_, rr: _bdot(tm_, rr), t_mat, rhs)
    qk = each(lambda x, y: _bdot(x, y, NT), q, k)
    qk = each(lambda m, d: jnp.where(causal, m * d, 0.0), qk, decay)
    q_dec = each(lambda x, e: x * e, q, e_col)
    g_last = [each(lambda gc: gc[(j + 1) * c - 1:(j + 1) * c, :], gc_col) for j in range(GDN_GROUP)]

    def last_of_own_chunk(*gl):
        out = gl[-1]
        for j in range(GDN_GROUP - 2, -1, -1):
            out = jnp.where(chunk_of_row == j, gl[j], out)
        return out

    g_end = each(last_of_own_chunk, *g_last)
    k_dec = each(lambda x, ge, gc: x * jnp.exp(ge - gc), k, g_end, gc_col)
    qs_parts, v_parts = [], []
    for j in range(GDN_GROUP):
        rows = slice(j * c, (j + 1) * c)
        ws = each(lambda x, s: _bdot(x[rows, GDN_DV:], s), sol, st)
        qs_parts.append(each(lambda x, s: _bdot(x[rows], s), q_dec, st))
        v_new = each(lambda x, w: x[rows, :GDN_DV] - w, sol, ws)
        v_parts.append(v_new)
        kv = each(lambda x, vn: _bdot(x[rows], vn, TN), k_dec, v_new)
        st = each(lambda s, gl, d: s * jnp.exp(gl) + d, st, g_last[j], kv)
    qs = each(lambda *parts: jnp.concatenate(parts, axis=0), *qs_parts)
    v_all = each(lambda *parts: jnp.concatenate(parts, axis=0), *v_parts)
    o = each(lambda a, m, vn: a + _bdot(m, vn), qs, qk, v_all)
    o = each(lambda x: x * lax.rsqrt(jnp.mean(x * x, -1, keepdims=True) + NORM_EPS) * norm_w, o)
    o = each(lambda x, zz: x * _silu(zz), o, z)
    return list(zip(o, st))


def _mixer_kernel(xc_ref, gate_ref, q_ref, k_ref, v_ref, z_ref, smt_ref,
                  wg_ref, bg_ref, lam_ref, og_ref, alc_ref, dtc_ref, nw_ref,
                  ylru_ref, y_ref, hcarry, state):
    n = pl.program_id(1)
    c = GDN_GROUP * GDN_CHUNK
    nb = q_ref.shape[0]
    first = n == 0

    @pl.when(first)
    def _():
        state[...] = jnp.zeros_like(state)
        hcarry[...] = jnp.zeros_like(hcarry)

    lru_out = [_lru_tile(xc_ref[b], gate_ref[b], wg_ref[...], bg_ref[...], lam_ref[...], og_ref[...], hcarry[b])
               for b in range(nb)]
    norm_w = nw_ref[...]

    args = []
    for b in range(nb):
        q_all = _silu(q_ref[b])
        k_all = _silu(k_ref[b])
        v_all = _silu(v_ref[b])
        z_all = z_ref[b]
        smt = smt_ref[b]
        beta_rows = _sigmoid(smt)
        g_rows = -jnp.exp(alc_ref[...]) * _softplus(smt + dtc_ref[...])
        head_row = lax.broadcasted_iota(jnp.int32, smt.shape, 0)
        stacked = jnp.where(head_row < GDN_HEADS, beta_rows, g_rows)
        cols = jnp.concatenate([stacked, jnp.zeros((LANES - 8, c), F32)], axis=0).T
        for hd in range(GDN_HEADS):
            sl = slice(hd * GDN_DK, (hd + 1) * GDN_DK)
            args.append((q_all[:, sl], k_all[:, sl], v_all[:, sl], z_all[:, sl],
                         cols[:, hd:hd + 1],
                         cols[:, GDN_HEADS + hd:GDN_HEADS + hd + 1],
                         g_rows[GDN_HEADS + hd:GDN_HEADS + hd + 1, :],
                         state[b, hd]))
    outs = _gdn_heads(args, norm_w)
    for b in range(nb):
        for hd in range(GDN_HEADS):
            o, st_new = outs[b * GDN_HEADS + hd]
            state[b, hd] = st_new
            y_ref[b, :, hd * GDN_DK:(hd + 1) * GDN_DK] = o
    for b, (y_lru, carry) in enumerate(lru_out):
        ylru_ref[b] = y_lru
        hcarry[b] = carry


def _mixers(proj3, smallt3, w_gates, b_gates, lam, out_g, alc, dtc, norm_w, nb):
    bsz, seq, _ = proj3.shape
    c = GDN_GROUP * GDN_CHUNK
    nch = seq // c
    col = lambda j: pl.BlockSpec((nb, c, GDN_QK), lambda b, n: (b, n, j))
    const = lambda shape: pl.BlockSpec(shape, lambda b, n: (0,) * len(shape))
    return pl.pallas_call(
        _mixer_kernel,
        grid=(bsz // nb, nch),
        in_specs=[
            col(0), col(1), col(2), col(3), col(4), col(5),
            pl.BlockSpec((nb, None, 8, c), lambda b, n: (b, n, 0, 0)),
            const((LRU_WIDTH, 2 * LRU_WIDTH)), const((1, 2 * LRU_WIDTH)), const((1, LRU_WIDTH)), const((1, LRU_WIDTH)),
            const((8, 1)), const((8, 1)),
            const((1, GDN_DV)),
        ],
        out_specs=[pl.BlockSpec((nb, c, LRU_WIDTH), lambda b, n: (b, n, 0)),
                   pl.BlockSpec((nb, c, GDN_V), lambda b, n: (b, n, 0))],
        out_shape=[jax.ShapeDtypeStruct((bsz, seq, LRU_WIDTH), F32),
                   jax.ShapeDtypeStruct((bsz, seq, GDN_V), F32)],
        scratch_shapes=[
            pltpu.VMEM((nb, 1, LRU_WIDTH), F32),
            pltpu.VMEM((nb, GDN_HEADS, GDN_DK, GDN_DV), F32),
        ],
        compiler_params=_params(("arbitrary", "arbitrary")),
        name="mixers",
    )(proj3, proj3, proj3, proj3, proj3, proj3, smallt3, w_gates, b_gates, lam, out_g, alc, dtc, norm_w)


def _pick_experts(logits, rbias):
    n = logits.shape[1]
    scores = _sigmoid(logits)
    choice = scores + rbias
    neg = -jnp.inf
    gs_rows = []
    sub = lax.broadcasted_iota(jnp.int32, (GROUP_SIZE, n), 0).astype(F32)
    for g in range(N_GROUPS):
        cg = choice[g * GROUP_SIZE:(g + 1) * GROUP_SIZE, :]
        m1 = jnp.max(cg, axis=0, keepdims=True)
        i1 = jnp.min(jnp.where(cg == m1, sub, float(GROUP_SIZE)), axis=0, keepdims=True)
        m2 = jnp.max(jnp.where(sub == i1, neg, cg), axis=0, keepdims=True)
        gs_rows.append(m1 + m2)
    gs = jnp.concatenate(gs_rows, axis=0)
    gi = lax.broadcasted_iota(jnp.int32, (N_GROUPS, n), 0).astype(F32)
    gsel = jnp.zeros((N_GROUPS, n), jnp.bool_)
    for _ in range(TOPK_GROUPS):
        m = jnp.max(gs, axis=0, keepdims=True)
        idx = jnp.min(jnp.where(gs == m, gi, float(N_GROUPS)), axis=0, keepdims=True)
        hit = gi == idx
        gsel = jnp.logical_or(gsel, hit)
        gs = jnp.where(hit, neg, gs)
    masked = jnp.concatenate(
        [jnp.where(gsel[g:g + 1, :], choice[g * GROUP_SIZE:(g + 1) * GROUP_SIZE, :], neg)
         for g in range(N_GROUPS)], axis=0)
    ei = lax.broadcasted_iota(jnp.int32, (N_EXPERTS, n), 0).astype(F32)
    hits, e_rows, w_rows = [], [], []
    multi = jnp.zeros((N_EXPERTS, n), F32)
    for _ in range(TOP_K):
        m = jnp.max(masked, axis=0, keepdims=True)
        idx = jnp.min(jnp.where(masked == m, ei, float(N_EXPERTS)), axis=0, keepdims=True)
        hit = ei == idx
        hits.append(hit)
        e_rows.append(idx)
        w_rows.append(jnp.sum(jnp.where(hit, scores, 0.0), axis=0, keepdims=True))
        multi = multi + hit.astype(F32)
        masked = jnp.where(hit, neg, masked)
    wts = jnp.concatenate(w_rows, axis=0)
    wts = wts / (jnp.sum(wts, axis=0, keepdims=True) + 1e-20) * ROUTED_SCALE
    return jnp.concatenate(e_rows, axis=0), wts, hits, multi


def _router_kernel(yl_ref, yg_ref, h0_ref, wo1_ref, wo2_ref, g_ref, b_ref, wrt_ref, rb_ref,
                   h1_ref, h1p_ref, e_ref, w_ref, rank_ref, cnt_ref, carry):
    i = pl.program_id(0)
    tm = h0_ref.shape[0]
    n = min(ROUTER_SUB, tm)
    subs = [slice(j * n, (j + 1) * n) for j in range(tm // n)]

    @pl.when(i == 0)
    def _():
        carry[...] = jnp.zeros_like(carry)

    mixes = [_dot(yl_ref[r, :].astype(BF16), wo1_ref[...]) + _dot(yg_ref[r, :].astype(BF16), wo2_ref[...])
             for r in subs]
    h1s = [_layer_norm(DEEPNORM_ALPHA * h0_ref[r, :] + mix, g_ref[...], b_ref[...])
           for r, mix in zip(subs, mixes)]
    for r, h1 in zip(subs, h1s):
        h1_ref[r, :] = h1
        h1p_ref[r, :] = _pack_rows(h1)
    logits = [_dot3(wrt_ref[...], h1, NT) for h1 in h1s]
    picks = [_pick_experts(lg, rb_ref[...]) for lg in logits]
    ti = lax.broadcasted_iota(jnp.int32, (n, n), 0)
    tj = lax.broadcasted_iota(jnp.int32, (n, n), 1)
    before = (ti < tj).astype(BF16)
    cums = [_dot(multi.astype(BF16), before) for _, _, _, multi in picks]
    base = carry[...]
    for r, (e_rows, wts, hits, multi), cum in zip(subs, picks, cums):
        cum = cum + base
        r_rows = [jnp.sum(jnp.where(hit, cum, 0.0), axis=0, keepdims=True) for hit in hits]
        base = base + jnp.sum(multi, axis=1, keepdims=True)
        e_ref[:, r] = e_rows.astype(jnp.int32)
        w_ref[:, r] = wts
        rank_ref[:, r] = jnp.concatenate(r_rows, axis=0).astype(jnp.int32)
    carry[...] = base
    cnt_ref[...] = base.astype(jnp.int32)


def _router(y_lru, y_gdn, h0, wo1, wo2, g, b, w_router_t, rbias, tm):
    t = h0.shape[0]
    const = lambda shape: pl.BlockSpec(shape, lambda i: (0,) * len(shape))
    return pl.pallas_call(
        _router_kernel,
        grid=(t // tm,),
        in_specs=[
            pl.BlockSpec((tm, LRU_WIDTH), lambda i: (i, 0)),
            pl.BlockSpec((tm, GDN_V), lambda i: (i, 0)),
            pl.BlockSpec((tm, D_MODEL), lambda i: (i, 0)),
            const((LRU_WIDTH, D_MODEL)), const((GDN_V, D_MODEL)),
            const((1, D_MODEL)), const((1, D_MODEL)),
            const((N_EXPERTS, D_MODEL)), const((N_EXPERTS, 1)),
        ],
        out_specs=[
            pl.BlockSpec((tm, D_MODEL), lambda i: (i, 0)),
            pl.BlockSpec((tm, D_PACK), lambda i: (i, 0)),
            pl.BlockSpec((TOP_K, tm), lambda i: (0, i)),
            pl.BlockSpec((TOP_K, tm), lambda i: (0, i)),
            pl.BlockSpec((TOP_K, tm), lambda i: (0, i)),
            const((N_EXPERTS, 1)),
        ],
        out_shape=[
            jax.ShapeDtypeStruct((t, D_MODEL), F32),
            jax.ShapeDtypeStruct((t, D_PACK), jnp.uint32),
            jax.ShapeDtypeStruct((TOP_K, t), jnp.int32),
            jax.ShapeDtypeStruct((TOP_K, t), F32),
            jax.ShapeDtypeStruct((TOP_K, t), jnp.int32),
            jax.ShapeDtypeStruct((N_EXPERTS, 1), jnp.int32),
        ],
        scratch_shapes=[pltpu.VMEM((N_EXPERTS, 1), F32)],
        compiler_params=_params(("arbitrary",)),
        name="outproj_router",
    )(y_lru, y_gdn, h0, wo1, wo2, g, b, w_router_t, rbias)


def _dest_kernel(e_ref, r_ref, ps_ref, d_ref):
    tm = e_ref.shape[1]
    ei = lax.broadcasted_iota(jnp.int32, (N_EXPERTS, tm), 0)
    rows = []
    for k in range(TOP_K):
        hit = ei == e_ref[k:k + 1, :]
        rows.append(jnp.sum(jnp.where(hit, ps_ref[...], 0), axis=0, keepdims=True))
    d_ref[...] = jnp.concatenate(rows, axis=0) + r_ref[...]


def _dest(top_e, rank, pad_start, tm):
    t = top_e.shape[1]
    blk = pl.BlockSpec((TOP_K, tm), lambda i: (0, i))
    return pl.pallas_call(
        _dest_kernel,
        grid=(t // tm,),
        in_specs=[blk, blk, pl.BlockSpec((N_EXPERTS, 1), lambda i: (0, 0))],
        out_specs=blk,
        out_shape=jax.ShapeDtypeStruct((TOP_K, t), jnp.int32),
        compiler_params=_params(("arbitrary",)),
        name="moe_dest",
    )(top_e, rank, pad_start)


def _sc_scatter_rows(rows, idx, n_out, chunk):
    n_copies, t = idx.shape
    d = rows.shape[1]
    per_worker = t // SC_WORKERS
    n_chunks = per_worker // chunk
    mesh = plsc.VectorSubcoreMesh(core_axis_name="c", subcore_axis_name="s")
    idx_flat = idx.reshape(n_copies * t)

    @functools.partial(
        pl.kernel, mesh=mesh,
        out_type=jax.ShapeDtypeStruct((n_out, d), rows.dtype),
        scratch_types=[pltpu.VMEM((chunk,), jnp.int32) for _ in range(n_copies)] + [
            pltpu.VMEM((chunk, d), rows.dtype),
            pltpu.SemaphoreType.DMA,
        ],
    )
    def scatter(rows_hbm, idx_hbm, out_hbm, *scratch):
        idx_v = scratch[:n_copies]
        rows_v, sem = scratch[n_copies:]
        wid = lax.axis_index("s") * SC_CORES + lax.axis_index("c")
        base = wid * per_worker

        @pl.loop(0, n_chunks)
        def _(j):
            off = base + j * chunk
            for k in range(n_copies):
                pltpu.sync_copy(idx_hbm.at[pl.ds(k * t + off, chunk)], idx_v[k])
            pltpu.sync_copy(rows_hbm.at[pl.ds(off, chunk)], rows_v)
            copies = [pltpu.async_copy(rows_v, out_hbm.at[idx_v[k]], sem) for k in range(n_copies)]
            for cp in copies:
                cp.wait()

    return scatter(rows, idx_flat)


def _expert_kernel(be_ref, nv_ref, first_ref, slot_ref, next_ref, nu_ref,
                   xs_ref, wg_hbm, wu_hbm, wd_hbm, ys_ref, wg_f, wu_f, wd_f, wgu_b, wd_b, sem):
    i = pl.program_id(0)

    def fetch(e, slot):
        return (pltpu.make_async_copy(wg_hbm.at[e], wg_f.at[slot], sem.at[slot]),
                pltpu.make_async_copy(wu_hbm.at[e], wu_f.at[slot], sem.at[slot]),
                pltpu.make_async_copy(wd_hbm.at[e], wd_f.at[slot], sem.at[slot]))

    @pl.when(i < nu_ref[0])
    def _():
        e = be_ref[i]
        slot = slot_ref[i]

        @pl.when(first_ref[i] == 1)
        def _():
            @pl.when(i == 0)
            def _():
                for cp in fetch(e, slot):
                    cp.start()

            for cp in fetch(e, slot):
                cp.wait()

            @pl.when(next_ref[i] >= 0)
            def _():
                for cp in fetch(next_ref[i], 1 - slot):
                    cp.start()

            wgu_b[:, :D_EXPERT] = wg_f[slot].astype(BF16)
            wgu_b[:, D_EXPERT:] = wu_f[slot].astype(BF16)
            wd_b[...] = wd_f[slot].astype(BF16)

        n = xs_ref.shape[0] // EXPERT_BANDS
        bands = [slice(j * n, (j + 1) * n) for j in range(EXPERT_BANDS)]
        row = lax.broadcasted_iota(jnp.int32, (n, D_PACK), 0)
        xs = [_unpack_rows(jnp.where(row + j * n < nv_ref[i], xs_ref[r, :], jnp.uint32(0)))
              for j, r in enumerate(bands)]
        gus = [_dot(x_hi.astype(BF16), wgu_b[:D_PACK, :]) + _dot(x_lo.astype(BF16), wgu_b[D_PACK:, :])
               for x_hi, x_lo in xs]
        hs = [_silu(gu[:, :D_EXPERT]) * gu[:, D_EXPERT:] for gu in gus]
        ys = [_dot(h.astype(BF16), wd_b[...]) for h in hs]
        for r, y in zip(bands, ys):
            ys_ref[r, :] = _pack_rows(y)


def _experts(blk_e, n_valid, first, slot, next_e, n_used, xs, w_gate, w_up, w_down):
    n_rows = xs.shape[0]
    n_blocks = n_rows // MOE_BLOCK
    blk = lambda i, be, nv, fi, sl, nx, nu: (jnp.minimum(i, nu[0] - 1), 0)
    return pl.pallas_call(
        _expert_kernel,
        grid_spec=pltpu.PrefetchScalarGridSpec(
            num_scalar_prefetch=6,
            grid=(n_used[0],),
            in_specs=[
                pl.BlockSpec((MOE_BLOCK, D_PACK), blk),
                pl.BlockSpec(memory_space=pl.ANY),
                pl.BlockSpec(memory_space=pl.ANY),
                pl.BlockSpec(memory_space=pl.ANY),
            ],
            out_specs=pl.BlockSpec((MOE_BLOCK, D_PACK), blk),
            scratch_shapes=[
                pltpu.VMEM((2, D_MODEL, D_EXPERT), F32),
                pltpu.VMEM((2, D_MODEL, D_EXPERT), F32),
                pltpu.VMEM((2, D_EXPERT, D_MODEL), F32),
                pltpu.VMEM((D_MODEL, 2 * D_EXPERT), BF16),
                pltpu.VMEM((D_EXPERT, D_MODEL), BF16),
                pltpu.SemaphoreType.DMA((2,)),
            ],
        ),
        out_shape=jax.ShapeDtypeStruct((n_rows, D_PACK), jnp.uint32),
        compiler_params=_params(("arbitrary",)),
        name="moe_experts",
    )(blk_e, n_valid, first, slot, next_e, n_used, xs, w_gate, w_up, w_down)


def _sc_gather_rows(table, idx, chunk):
    n_idx = idx.shape[0]
    d = table.shape[1]
    per_worker = n_idx // SC_WORKERS
    n_chunks = per_worker // chunk
    assert n_chunks % 2 == 0 and n_chunks * chunk * SC_WORKERS == n_idx
    mesh = plsc.VectorSubcoreMesh(core_axis_name="c", subcore_axis_name="s")

    @functools.partial(
        pl.kernel, mesh=mesh,
        out_type=jax.ShapeDtypeStruct((n_idx, d), table.dtype),
        scratch_types=[
            pltpu.VMEM((chunk,), jnp.int32), pltpu.VMEM((chunk,), jnp.int32),
            pltpu.VMEM((chunk, d), table.dtype), pltpu.VMEM((chunk, d), table.dtype),
            pltpu.SemaphoreType.DMA, pltpu.SemaphoreType.DMA, pltpu.SemaphoreType.DMA, pltpu.SemaphoreType.DMA,
        ],
    )
    def gather(table_hbm, idx_hbm, out_hbm, idx_v0, idx_v1, rows_v0, rows_v1, gsem0, gsem1, osem0, osem1):
        idx_v, rows_v, gsem, osem = (idx_v0, idx_v1), (rows_v0, rows_v1), (gsem0, gsem1), (osem0, osem1)
        wid = lax.axis_index("s") * SC_CORES + lax.axis_index("c")
        base = wid * per_worker

        def gather_copy(slot):
            return pltpu.make_async_copy(table_hbm.at[idx_v[slot]], rows_v[slot], gsem[slot])

        def out_copy(c, slot):
            return pltpu.make_async_copy(rows_v[slot], out_hbm.at[pl.ds(base + c * chunk, chunk)], osem[slot])

        def start_gather(c, slot):
            pltpu.sync_copy(idx_hbm.at[pl.ds(base + c * chunk, chunk)], idx_v[slot])
            gather_copy(slot).start()

        start_gather(0, 0)

        @pl.loop(0, n_chunks, step=2)
        def _(j):
            for b in range(2):
                c = j + b
                cur, other = b, 1 - b

                @pl.when(c >= 1)
                def _():
                    out_copy(c - 1, other).wait()

                @pl.when(c + 1 < n_chunks)
                def _():
                    start_gather(c + 1, other)

                gather_copy(cur).wait()
                out_copy(c, cur).start()

        out_copy(n_chunks - 1, 1).wait()

    return gather(table, idx)


def _combine_kernel(h1_ref, wts_ref, wsgu_ref, wsd_ref, g_ref, b_ref, yg_ref, out_ref):
    h1 = h1_ref[...]
    gu = _dot(h1.astype(BF16), wsgu_ref[...])
    hs = _silu(gu[:, :D_SHARED]) * gu[:, D_SHARED:]
    acc = DEEPNORM_ALPHA * h1 + _dot(hs.astype(BF16), wsd_ref[...])
    wts = wts_ref[...].T
    acc_hi = acc[:, :D_PACK]
    acc_lo = acc[:, D_PACK:]
    for k in range(TOP_K):
        y_hi, y_lo = _unpack_rows(yg_ref[k])
        acc_hi = acc_hi + y_hi * wts[:, k:k + 1]
        acc_lo = acc_lo + y_lo * wts[:, k:k + 1]
    out_ref[...] = _layer_norm(jnp.concatenate([acc_hi, acc_lo], axis=1), g_ref[...], b_ref[...])


def _combine(h1, wts, ws_gu, ws_down, g, b, yg, tm):
    t = h1.shape[0]
    const = lambda shape: pl.BlockSpec(shape, lambda i: (0,) * len(shape))
    return pl.pallas_call(
        _combine_kernel,
        grid=(t // tm,),
        in_specs=[
            pl.BlockSpec((tm, D_MODEL), lambda i: (i, 0)),
            pl.BlockSpec((TOP_K, tm), lambda i: (0, i)),
            const((D_MODEL, 2 * D_SHARED)), const((D_SHARED, D_MODEL)),
            const((1, D_MODEL)), const((1, D_MODEL)),
            pl.BlockSpec((TOP_K, tm, D_PACK), lambda i: (0, i, 0)),
        ],
        out_specs=pl.BlockSpec((tm, D_MODEL), lambda i: (i, 0)),
        out_shape=jax.ShapeDtypeStruct((t, D_MODEL), F32),
        compiler_params=_params(("arbitrary",)),
        name="moe_combine",
    )(h1, wts, ws_gu, ws_down, g, b, yg)


def _block_diag(w):
    nb, bi, bo = w.shape
    eye = jnp.eye(nb, dtype=w.dtype)
    return (eye[:, None, :, None] * w[:, :, None, :]).reshape(nb * bi, nb * bo)


def _pad_lanes(v, offset, width):
    return jnp.zeros((1, width), F32).at[0, offset:offset + v.shape[0]].set(v)


def _layer(h_in_x, l, p, tiles):
    bsz, seq, _ = h_in_x.shape
    t = bsz * seq
    row = lambda v: v.reshape(1, -1)

    w_in = p['w_in'][l]
    w_main = w_in.astype(BF16)
    w_small = jnp.zeros((D_MODEL, LANES), F32).at[:, :2 * GDN_HEADS].set(w_in[:, N_MAIN:])
    zeros = lambda n: jnp.zeros((CONV_WIDTH, n), F32)
    conv_w = jnp.concatenate([p['lru_conv_w'][l], zeros(LRU_WIDTH), p['gdn_conv_w'][l], zeros(GDN_V)], 1)
    conv_b = jnp.zeros((1, N_MAIN), F32).at[0, :LRU_WIDTH].set(p['lru_conv_b'][l])
    h0, proj, small_t = _inproj(h_in_x.reshape(t, D_MODEL), row(p['ln_g']), row(p['ln_b']),
                                       w_main, w_small, conv_w, conv_b, tiles['inproj'], seq)
    proj3 = proj.reshape(bsz, seq, N_MAIN)

    w_gates = jnp.concatenate([_block_diag(p['lru_w_rg'][l]), _block_diag(p['lru_w_ig'][l])], 1).astype(BF16)
    b_gates = jnp.concatenate([p['lru_b_rg'][l], p['lru_b_ig'][l]]).reshape(1, -1)
    rows = GDN_GROUP * GDN_CHUNK
    smallt3 = small_t.reshape(8, bsz, seq // rows, rows).transpose(1, 2, 0, 3)
    a_log, dt_bias = p['gdn_a_log'][l], p['gdn_dt_bias'][l]
    alc = _pad_lanes(a_log, GDN_HEADS, 8).reshape(8, 1)
    dtc = _pad_lanes(dt_bias, GDN_HEADS, 8).reshape(8, 1)
    y_lru, y_gdn = _mixers(proj3, smallt3, w_gates, b_gates, row(p['lru_lambda'][l]), row(p['lru_out_g'][l]),
                           alc, dtc, row(p['gdn_norm_w'][l]), tiles['gdn_nb'])

    w_out = p['w_out'][l].astype(BF16)
    h1, h1p, top_e, wts, rank, counts = _router(
        y_lru.reshape(t, LRU_WIDTH), y_gdn.reshape(t, GDN_V), h0, w_out[:LRU_WIDTH], w_out[LRU_WIDTH:],
        row(p['ln1_g'][l]), row(p['ln1_b'][l]), p['w_router'][l].T, p['router_bias'][l].reshape(-1, 1),
        tiles['router'])

    counts = counts[:, 0]
    padded = (counts + MOE_BLOCK - 1) // MOE_BLOCK * MOE_BLOCK
    pad_end = jnp.cumsum(padded)
    pad_start = pad_end - padded
    n_blocks = (t * TOP_K + N_EXPERTS * (MOE_BLOCK - 1)) // MOE_BLOCK
    n_rows = n_blocks * MOE_BLOCK
    n_used = (pad_end[-1] // MOE_BLOCK).astype(jnp.int32)
    blk_ids = jnp.minimum(jnp.arange(n_blocks, dtype=jnp.int32), n_used - 1)
    blk_e = jnp.minimum(jnp.sum(pad_end[None, :] <= (blk_ids * MOE_BLOCK)[:, None], axis=1),
                        N_EXPERTS - 1).astype(jnp.int32)

    dest = _dest(top_e, rank, pad_start.reshape(-1, 1), tiles['dest'])
    n_valid = jnp.clip(counts[blk_e] - (blk_ids * MOE_BLOCK - pad_start[blk_e]), 0, MOE_BLOCK).astype(jnp.int32)
    xs = _sc_scatter_rows(h1p, dest, n_rows, SC_SCATTER_CHUNK)
    active = jnp.arange(n_blocks, dtype=jnp.int32) < n_used
    first = (active & jnp.concatenate([jnp.ones((1,), bool), blk_e[1:] != blk_e[:-1]])).astype(jnp.int32)
    slot = ((jnp.cumsum(first) - 1) % 2).astype(jnp.int32)
    used = counts > 0
    later = jnp.where(used[None, :] & (jnp.arange(N_EXPERTS)[None, :] > jnp.arange(N_EXPERTS)[:, None]),
                      jnp.arange(N_EXPERTS, dtype=jnp.int32)[None, :], N_EXPERTS)
    next_used = jnp.min(later, axis=1)
    next_e = jnp.where(next_used < N_EXPERTS, next_used, -1)[blk_e].astype(jnp.int32)
    ys = _experts(blk_e, n_valid, first, slot, next_e, n_used.reshape(1), xs,
                  p['w_gate'][l], p['w_up'][l], p['w_down'][l])
    ws_gu = jnp.concatenate([p['ws_gate'][l], p['ws_up'][l]], 1).astype(BF16)
    yg = _sc_gather_rows(ys, dest.reshape(TOP_K * t), SC_GATHER_CHUNK).reshape(TOP_K, t, D_PACK)
    out = _combine(h1, wts, ws_gu, p['ws_down'][l].astype(BF16),
                   row(p['ln2_g'][l]), row(p['ln2_b'][l]), yg, tiles['combine'])
    return out.reshape(bsz, seq, D_MODEL)


def _tiles(bsz, seq):
    t = bsz * seq
    return {
        'inproj': min(512, t),
        'gdn_nb': bsz,
        'router': min(1024, t),
        'dest': min(512, t),
        'combine': min(512, t),
    }


def kernel(x, ln_in_g, ln_in_b, w_in, lru_conv_w, lru_conv_b, lru_w_rg, lru_b_rg, lru_w_ig, lru_b_ig,
           lru_lambda, lru_out_g, gdn_conv_w, gdn_a_log, gdn_dt_bias, gdn_norm_w, w_out, ln1_g, ln1_b,
           w_router, router_bias, w_gate, w_up, w_down, ws_gate, ws_up, ws_down, ln2_g, ln2_b):
    assert w_in.shape[0] == DEPTH == 1
    p = dict(ln_g=ln_in_g, ln_b=ln_in_b, w_in=w_in, lru_conv_w=lru_conv_w, lru_conv_b=lru_conv_b,
             lru_w_rg=lru_w_rg, lru_b_rg=lru_b_rg, lru_w_ig=lru_w_ig, lru_b_ig=lru_b_ig,
             lru_lambda=lru_lambda, lru_out_g=lru_out_g, gdn_conv_w=gdn_conv_w, gdn_a_log=gdn_a_log,
             gdn_dt_bias=gdn_dt_bias, gdn_norm_w=gdn_norm_w, w_out=w_out, ln1_g=ln1_g, ln1_b=ln1_b,
             w_router=w_router, router_bias=router_bias, w_gate=w_gate, w_up=w_up, w_down=w_down,
             ws_gate=ws_gate, ws_up=ws_up, ws_down=ws_down, ln2_g=ln2_g, ln2_b=ln2_b)
    bsz, seq, _ = x.shape
    return _layer(x, 0, p, _tiles(bsz, seq))
```

```python
import functools

import jax
import jax.numpy as jnp
from jax import lax
from jax.experimental import pallas as pl
from jax.experimental.pallas import tpu as pltpu
from jax.experimental.pallas import tpu_sc as plsc

F32 = jnp.float32
BF16 = jnp.bfloat16

D_MODEL = 1024
LRU_WIDTH = 512
LRU_C = 8.0
CONV_WIDTH = 4
GDN_HEADS = 4
GDN_DK = 128
GDN_DV = 128
GDN_CHUNK = 64
GDN_GROUP = 2
GDN_QK = GDN_HEADS * GDN_DK
GDN_V = GDN_HEADS * GDN_DV
N_MAIN = 2 * LRU_WIDTH + 2 * GDN_QK + 2 * GDN_V
N_EXPERTS = 256
TOP_K = 8
N_GROUPS = 8
GROUP_SIZE = N_EXPERTS // N_GROUPS
TOPK_GROUPS = 4
D_EXPERT = 256
D_SHARED = 256
ROUTED_SCALE = 2.5
MOE_BLOCK = 640
D_PACK = D_MODEL // 2
LN_EPS = 1e-5
NORM_EPS = 1e-6
DEPTH = 1
DEEPNORM_ALPHA = (2.0 * DEPTH) ** 0.25

SCAN_GROUP = 8
HALO = 8
CONV_GROUP = 512
CONV_GROUPS = (0, 2, 3, 4)
LANES = 128
VMEM_LIMIT = 56 * 1024 * 1024
ROUTER_SUB = 256
EXPERT_BANDS = 5
WEIGHT_DMA_PRIORITY = 1
SC_CORES = 2
SC_WORKERS = 32
SC_GATHER_CHUNK = 64
SC_SCATTER_CHUNK = 128

NN = (((1,), (0,)), ((), ()))
NT = (((1,), (1,)), ((), ()))
TN = (((0,), (0,)), ((), ()))


def _dot(a, b, dims=NN):
    return lax.dot_general(a, b, dims, preferred_element_type=F32)


def _split(a):
    hi = a.astype(BF16)
    lo = (a - hi.astype(F32)).astype(BF16)
    return hi, lo


def _dot3(a, b, dims=NN):
    ah, al = _split(a)
    bh, bl = _split(b)
    return _dot(ah, bh, dims) + (_dot(ah, bl, dims) + _dot(al, bh, dims))


def _layer_norm(x, g, b):
    mu = jnp.mean(x, -1, keepdims=True)
    xc = x - mu
    var = jnp.mean(xc * xc, -1, keepdims=True)
    return xc * lax.rsqrt(var + LN_EPS) * g + b


def _sigmoid(x):
    return 0.5 * jnp.tanh(0.5 * x) + 0.5


def _silu(x):
    return x * _sigmoid(x)


def _softplus(x):
    return jnp.maximum(x, 0.0) + jnp.log1p(jnp.exp(-jnp.abs(x)))


def _gelu_tanh(x):
    c = 0.7978845608028654
    return x * (0.5 * (1.0 + jnp.tanh(c * (x + 0.044715 * (x * x * x)))))


def _pack_rows(x):
    hi = lax.bitcast_convert_type(x[:, :D_PACK].astype(BF16).astype(F32), jnp.uint32)
    lo = lax.bitcast_convert_type(x[:, D_PACK:].astype(BF16).astype(F32), jnp.uint32)
    return (hi & jnp.uint32(0xFFFF0000)) | (lo >> 16)


def _unpack_rows(w):
    hi = lax.bitcast_convert_type(w & jnp.uint32(0xFFFF0000), F32)
    lo = lax.bitcast_convert_type(w << 16, F32)
    return hi, lo


def _params(sem, **kw):
    return pltpu.CompilerParams(dimension_semantics=sem, vmem_limit_bytes=VMEM_LIMIT, **kw)


def _inproj_kernel(x_ref, g_ref, b_ref, w_ref, ws_ref, cw_ref, cb_ref,
                   h_ref, proj_ref, smallt_ref, hist, stage, *, tiles_per_seq):
    i = pl.program_id(0)
    tm = x_ref.shape[0]
    h = _layer_norm(x_ref[...], g_ref[...], b_ref[...])
    h_ref[...] = h
    hb = h.astype(BF16)

    @pl.when(i % tiles_per_seq == 0)
    def _():
        hist[...] = jnp.zeros_like(hist)

    for g in range(N_MAIN // CONV_GROUP):
        cols = slice(g * CONV_GROUP, (g + 1) * CONV_GROUP)
        p = _dot(hb, w_ref[:, cols])
        if g in CONV_GROUPS:
            stage[HALO:, :] = p
            stage[:HALO, :] = hist[:, cols]
            acc = cb_ref[:, cols]
            for j in range(CONV_WIDTH):
                off = HALO - (CONV_WIDTH - 1) + j
                acc = acc + stage[off:off + tm, :] * cw_ref[j:j + 1, cols]
            hist[:, cols] = p[tm - HALO:, :]
            p = acc
        proj_ref[:, cols] = p
    smallt_ref[...] = _dot3(h, ws_ref[...]).T[:smallt_ref.shape[0], :]


def _inproj(x2d, g, b, w_main, w_small, conv_w, conv_b, tm, seq):
    t = x2d.shape[0]
    return pl.pallas_call(
        functools.partial(_inproj_kernel, tiles_per_seq=seq // tm),
        grid=(t // tm,),
        in_specs=[
            pl.BlockSpec((tm, D_MODEL), lambda i: (i, 0)),
            pl.BlockSpec((1, D_MODEL), lambda i: (0, 0)),
            pl.BlockSpec((1, D_MODEL), lambda i: (0, 0)),
            pl.BlockSpec((D_MODEL, N_MAIN), lambda i: (0, 0)),
            pl.BlockSpec((D_MODEL, LANES), lambda i: (0, 0)),
            pl.BlockSpec((CONV_WIDTH, N_MAIN), lambda i: (0, 0)),
            pl.BlockSpec((1, N_MAIN), lambda i: (0, 0)),
        ],
        out_specs=[
            pl.BlockSpec((tm, D_MODEL), lambda i: (i, 0)),
            pl.BlockSpec((tm, N_MAIN), lambda i: (i, 0)),
            pl.BlockSpec((8, tm), lambda i: (0, i)),
        ],
        out_shape=[
            jax.ShapeDtypeStruct((t, D_MODEL), F32),
            jax.ShapeDtypeStruct((t, N_MAIN), F32),
            jax.ShapeDtypeStruct((8, t), F32),
        ],
        scratch_shapes=[pltpu.VMEM((HALO, N_MAIN), F32), pltpu.VMEM((HALO + tm, CONV_GROUP), F32)],
        compiler_params=_params(("arbitrary",)),
        name="ln_inproj",
    )(x2d, g, b, w_main, w_small, conv_w, conv_b)


def _lru_tile(xc, gate, wg, bg, lam, og, carry):
    rows = xc.shape[0]
    gates = _dot(xc.astype(BF16), wg) + bg
    r = _sigmoid(gates[:, :LRU_WIDTH])
    i = _sigmoid(gates[:, LRU_WIDTH:])
    log_a = (-LRU_C) * r * _softplus(-lam)
    a = jnp.exp(log_a)
    one_minus_a2 = -jnp.tanh(log_a) * (a * a + 1.0)
    mult = jnp.where(one_minus_a2 > 0.0, one_minus_a2 * lax.rsqrt(one_minus_a2), 0.0)
    bv = mult * (i * xc)
    a = a.reshape(rows // SCAN_GROUP, SCAN_GROUP, LRU_WIDTH)
    bv = bv.reshape(rows // SCAN_GROUP, SCAN_GROUP, LRU_WIDTH)
    row_in_group = lax.broadcasted_iota(jnp.int32, a.shape, 1)
    d = 1
    while d < SCAN_GROUP:
        a_sh = jnp.where(row_in_group < d, 1.0, pltpu.roll(a, d, 1))
        b_sh = jnp.where(row_in_group < d, 0.0, pltpu.roll(bv, d, 1))
        bv = a * b_sh + bv
        a = a * a_sh
        d *= 2
    a = a.reshape(rows, LRU_WIDTH)
    bv = bv.reshape(rows, LRU_WIDTH)
    parts = []
    for g in range(rows // SCAN_GROUP):
        grp = slice(g * SCAN_GROUP, (g + 1) * SCAN_GROUP)
        hg = a[grp] * carry + bv[grp]
        carry = hg[SCAN_GROUP - 1:, :]
        parts.append(hg)
    h = jnp.concatenate(parts, axis=0)
    y = h * _gelu_tanh(gate)
    ms = jnp.mean(y * y, -1, keepdims=True)
    return y * lax.rsqrt(ms + NORM_EPS) * og, carry


def _bdot(a, b, dims=NN):
    return _dot(a.astype(BF16), b.astype(BF16), dims)


def _gdn_heads(args, norm_w):
    c = GDN_CHUNK
    r = GDN_GROUP * c
    ri = lax.broadcasted_iota(jnp.int32, (r, r), 0)
    ci = lax.broadcasted_iota(jnp.int32, (r, r), 1)
    same = (ri // c) == (ci // c)
    causal = same & (ri >= ci)
    strict = same & (ri > ci)
    upper = same & (ri <= ci)
    chunk_of_row = lax.broadcasted_iota(jnp.int32, (r, 1), 0) // c
    each = lambda f, *ls: [f(*xs) for xs in zip(*ls)]
    q, k, v, z, beta, g_col, g_row, st = [list(x) for x in zip(*args)]
    q = each(lambda x: x * lax.rsqrt(jnp.sum(x * x, -1, keepdims=True) + NORM_EPS) * (GDN_DK ** -0.5), q)
    k = each(lambda x: x * lax.rsqrt(jnp.sum(x * x, -1, keepdims=True) + NORM_EPS), k)
    gc_col = each(lambda g: jnp.sum(jnp.where(causal, g, 0.0), axis=1, keepdims=True), g_row)
    gc_row = each(lambda g: jnp.sum(jnp.where(upper, g, 0.0), axis=0, keepdims=True), g_col)
    decay = each(lambda gc, gr: jnp.exp(jnp.where(causal, gc - gr, -jnp.inf)), gc_col, gc_row)
    kb = each(lambda x, bt: x * bt, k, beta)
    vb = each(lambda x, bt: x * bt, v, beta)
    kk = each(lambda x, y: _bdot(x, y, NT), kb, k)
    a_mat = each(lambda m, d: jnp.where(strict, m * d, 0.0), kk, decay)
    e_col = each(jnp.exp, gc_col)
    rhs = each(lambda x, y, e: jnp.concatenate([x, y * e], axis=1), vb, kb, e_col)
    eye = (ri == ci).astype(F32)
    t_mat = each(lambda a: eye - a, a_mat)
    p = a_mat
    for _ in range(5):
        p = each(lambda x: _bdot(x, x), p)
        t_mat = each(lambda tm_, x: tm_ + _bdot(tm_, x), t_mat, p)
    sol = each(lambda tm_, rr: _bdot(tm_, rr), t_mat, rhs)
    qk = each(lambda x, y: _bdot(x, y, NT), q, k)
    qk = each(lambda m, d: jnp.where(causal, m * d, 0.0), qk, decay)
    q_dec = each(lambda x, e: x * e, q, e_col)
    g_last = [each(lambda gc: gc[(j + 1) * c - 1:(j + 1) * c, :], gc_col) for j in range(GDN_GROUP)]

    def last_of_own_chunk(*gl):
        out = gl[-1]
        for j in range(GDN_GROUP - 2, -1, -1):
            out = jnp.where(chunk_of_row == j, gl[j], out)
        return out

    g_end = each(last_of_own_chunk, *g_last)
    k_dec = each(lambda x, ge, gc: x * jnp.exp(ge - gc), k, g_end, gc_col)
    qs_parts, v_parts = [], []
    for j in range(GDN_GROUP):
        rows = slice(j * c, (j + 1) * c)
        ws = each(lambda x, s: _bdot(x[rows, GDN_DV:], s), sol, st)
        qs_parts.append(each(lambda x, s: _bdot(x[rows], s), q_dec, st))
        v_new = each(lambda x, w: x[rows, :GDN_DV] - w, sol, ws)
        v_parts.append(v_new)
        kv = each(lambda x, vn: _bdot(x[rows], vn, TN), k_dec, v_new)
        st = each(lambda s, gl, d: s * jnp.exp(gl) + d, st, g_last[j], kv)
    qs = each(lambda *parts: jnp.concatenate(parts, axis=0), *qs_parts)
    v_all = each(lambda *parts: jnp.concatenate(parts, axis=0), *v_parts)
    o = each(lambda a, m, vn: a + _bdot(m, vn), qs, qk, v_all)
    o = each(lambda x: x * lax.rsqrt(jnp.mean(x * x, -1, keepdims=True) + NORM_EPS) * norm_w, o)
    o = each(lambda x, zz: x * _silu(zz), o, z)
    return list(zip(o, st))


def _mixer_kernel(xc_ref, gate_ref, q_ref, k_ref, v_ref, z_ref, smt_ref,
                  wg_ref, bg_ref, lam_ref, og_ref, alc_ref, dtc_ref, nw_ref,
                  ylru_ref, y_ref, hcarry, state):
    n = pl.program_id(1)
    c = GDN_GROUP * GDN_CHUNK
    nb = q_ref.shape[0]
    first = n == 0

    @pl.when(first)
    def _():
        state[...] = jnp.zeros_like(state)
        hcarry[...] = jnp.zeros_like(hcarry)

    lru_out = [_lru_tile(xc_ref[b], gate_ref[b], wg_ref[...], bg_ref[...], lam_ref[...], og_ref[...], hcarry[b])
               for b in range(nb)]
    norm_w = nw_ref[...]

    args = []
    for b in range(nb):
        q_all = _silu(q_ref[b])
        k_all = _silu(k_ref[b])
        v_all = _silu(v_ref[b])
        z_all = z_ref[b]
        smt = smt_ref[b]
        beta_rows = _sigmoid(smt)
        g_rows = -jnp.exp(alc_ref[...]) * _softplus(smt + dtc_ref[...])
        head_row = lax.broadcasted_iota(jnp.int32, smt.shape, 0)
        stacked = jnp.where(head_row < GDN_HEADS, beta_rows, g_rows)
        cols = jnp.concatenate([stacked, jnp.zeros((LANES - 8, c), F32)], axis=0).T
        for hd in range(GDN_HEADS):
            sl = slice(hd * GDN_DK, (hd + 1) * GDN_DK)
            args.append((q_all[:, sl], k_all[:, sl], v_all[:, sl], z_all[:, sl],
                         cols[:, hd:hd + 1],
                         cols[:, GDN_HEADS + hd:GDN_HEADS + hd + 1],
                         g_rows[GDN_HEADS + hd:GDN_HEADS + hd + 1, :],
                         state[b, hd]))
    outs = _gdn_heads(args, norm_w)
    for b in range(nb):
        for hd in range(GDN_HEADS):
            o, st_new = outs[b * GDN_HEADS + hd]
            state[b, hd] = st_new
            y_ref[b, :, hd * GDN_DK:(hd + 1) * GDN_DK] = o
    for b, (y_lru, carry) in enumerate(lru_out):
        ylru_ref[b] = y_lru
        hcarry[b] = carry


def _mixers(proj3, smallt3, w_gates, b_gates, lam, out_g, alc, dtc, norm_w, nb):
    bsz, seq, _ = proj3.shape
    c = GDN_GROUP * GDN_CHUNK
    nch = seq // c
    col = lambda j: pl.BlockSpec((nb, c, GDN_QK), lambda b, n: (b, n, j))
    const = lambda shape: pl.BlockSpec(shape, lambda b, n: (0,) * len(shape))
    return pl.pallas_call(
        _mixer_kernel,
        grid=(bsz // nb, nch),
        in_specs=[
            col(0), col(1), col(2), col(3), col(4), col(5),
            pl.BlockSpec((nb, None, 8, c), lambda b, n: (b, n, 0, 0)),
            const((LRU_WIDTH, 2 * LRU_WIDTH)), const((1, 2 * LRU_WIDTH)), const((1, LRU_WIDTH)), const((1, LRU_WIDTH)),
            const((8, 1)), const((8, 1)),
            const((1, GDN_DV)),
        ],
        out_specs=[pl.BlockSpec((nb, c, LRU_WIDTH), lambda b, n: (b, n, 0)),
                   pl.BlockSpec((nb, c, GDN_V), lambda b, n: (b, n, 0))],
        out_shape=[jax.ShapeDtypeStruct((bsz, seq, LRU_WIDTH), F32),
                   jax.ShapeDtypeStruct((bsz, seq, GDN_V), F32)],
        scratch_shapes=[
            pltpu.VMEM((nb, 1, LRU_WIDTH), F32),
            pltpu.VMEM((nb, GDN_HEADS, GDN_DK, GDN_DV), F32),
        ],
        compiler_params=_params(("arbitrary", "arbitrary")),
        name="mixers",
    )(proj3, proj3, proj3, proj3, proj3, proj3, smallt3, w_gates, b_gates, lam, out_g, alc, dtc, norm_w)


def _pick_experts(logits, rbias):
    n = logits.shape[1]
    scores = _sigmoid(logits)
    choice = scores + rbias
    neg = -jnp.inf
    gs_rows = []
    sub = lax.broadcasted_iota(jnp.int32, (GROUP_SIZE, n), 0).astype(F32)
    for g in range(N_GROUPS):
        cg = choice[g * GROUP_SIZE:(g + 1) * GROUP_SIZE, :]
        m1 = jnp.max(cg, axis=0, keepdims=True)
        i1 = jnp.min(jnp.where(cg == m1, sub, float(GROUP_SIZE)), axis=0, keepdims=True)
        m2 = jnp.max(jnp.where(sub == i1, neg, cg), axis=0, keepdims=True)
        gs_rows.append(m1 + m2)
    gs = jnp.concatenate(gs_rows, axis=0)
    gi = lax.broadcasted_iota(jnp.int32, (N_GROUPS, n), 0).astype(F32)
    gsel = jnp.zeros((N_GROUPS, n), jnp.bool_)
    for _ in range(TOPK_GROUPS):
        m = jnp.max(gs, axis=0, keepdims=True)
        idx = jnp.min(jnp.where(gs == m, gi, float(N_GROUPS)), axis=0, keepdims=True)
        hit = gi == idx
        gsel = jnp.logical_or(gsel, hit)
        gs = jnp.where(hit, neg, gs)
    masked = jnp.concatenate(
        [jnp.where(gsel[g:g + 1, :], choice[g * GROUP_SIZE:(g + 1) * GROUP_SIZE, :], neg)
         for g in range(N_GROUPS)], axis=0)
    ei = lax.broadcasted_iota(jnp.int32, (N_EXPERTS, n), 0).astype(F32)
    hits, e_rows, w_rows = [], [], []
    multi = jnp.zeros((N_EXPERTS, n), F32)
    for _ in range(TOP_K):
        m = jnp.max(masked, axis=0, keepdims=True)
        idx = jnp.min(jnp.where(masked == m, ei, float(N_EXPERTS)), axis=0, keepdims=True)
        hit = ei == idx
        hits.append(hit)
        e_rows.append(idx)
        w_rows.append(jnp.sum(jnp.where(hit, scores, 0.0), axis=0, keepdims=True))
        multi = multi + hit.astype(F32)
        masked = jnp.where(hit, neg, masked)
    wts = jnp.concatenate(w_rows, axis=0)
    wts = wts / (jnp.sum(wts, axis=0, keepdims=True) + 1e-20) * ROUTED_SCALE
    return jnp.concatenate(e_rows, axis=0), wts, hits, multi


def _router_kernel(yl_ref, yg_ref, h0_ref, wo1_ref, wo2_ref, g_ref, b_ref, wrt_ref, rb_ref,
                   h1_ref, h1p_ref, e_ref, w_ref, rank_ref, cnt_ref, carry):
    i = pl.program_id(0)
    tm = h0_ref.shape[0]
    n = min(ROUTER_SUB, tm)
    subs = [slice(j * n, (j + 1) * n) for j in range(tm // n)]

    @pl.when(i == 0)
    def _():
        carry[...] = jnp.zeros_like(carry)

    mixes = [_dot(yl_ref[r, :].astype(BF16), wo1_ref[...]) + _dot(yg_ref[r, :].astype(BF16), wo2_ref[...])
             for r in subs]
    h1s = [_layer_norm(DEEPNORM_ALPHA * h0_ref[r, :] + mix, g_ref[...], b_ref[...])
           for r, mix in zip(subs, mixes)]
    for r, h1 in zip(subs, h1s):
        h1_ref[r, :] = h1
        h1p_ref[r, :] = _pack_rows(h1)
    logits = [_dot3(wrt_ref[...], h1, NT) for h1 in h1s]
    picks = [_pick_experts(lg, rb_ref[...]) for lg in logits]
    ti = lax.broadcasted_iota(jnp.int32, (n, n), 0)
    tj = lax.broadcasted_iota(jnp.int32, (n, n), 1)
    before = (ti < tj).astype(BF16)
    cums = [_dot(multi.astype(BF16), before) for _, _, _, multi in picks]
    base = carry[...]
    for r, (e_rows, wts, hits, multi), cum in zip(subs, picks, cums):
        cum = cum + base
        r_rows = [jnp.sum(jnp.where(hit, cum, 0.0), axis=0, keepdims=True) for hit in hits]
        base = base + jnp.sum(multi, axis=1, keepdims=True)
        e_ref[:, r] = e_rows.astype(jnp.int32)
        w_ref[:, r] = wts
        rank_ref[:, r] = jnp.concatenate(r_rows, axis=0).astype(jnp.int32)
    carry[...] = base
    cnt_ref[...] = base.astype(jnp.int32)


def _router(y_lru, y_gdn, h0, wo1, wo2, g, b, w_router_t, rbias, tm):
    t = h0.shape[0]
    const = lambda shape: pl.BlockSpec(shape, lambda i: (0,) * len(shape))
    return pl.pallas_call(
        _router_kernel,
        grid=(t // tm,),
        in_specs=[
            pl.BlockSpec((tm, LRU_WIDTH), lambda i: (i, 0)),
            pl.BlockSpec((tm, GDN_V), lambda i: (i, 0)),
            pl.BlockSpec((tm, D_MODEL), lambda i: (i, 0)),
            const((LRU_WIDTH, D_MODEL)), const((GDN_V, D_MODEL)),
            const((1, D_MODEL)), const((1, D_MODEL)),
            const((N_EXPERTS, D_MODEL)), const((N_EXPERTS, 1)),
        ],
        out_specs=[
            pl.BlockSpec((tm, D_MODEL), lambda i: (i, 0)),
            pl.BlockSpec((tm, D_PACK), lambda i: (i, 0)),
            pl.BlockSpec((TOP_K, tm), lambda i: (0, i)),
            pl.BlockSpec((TOP_K, tm), lambda i: (0, i)),
            pl.BlockSpec((TOP_K, tm), lambda i: (0, i)),
            const((N_EXPERTS, 1)),
        ],
        out_shape=[
            jax.ShapeDtypeStruct((t, D_MODEL), F32),
            jax.ShapeDtypeStruct((t, D_PACK), jnp.uint32),
            jax.ShapeDtypeStruct((TOP_K, t), jnp.int32),
            jax.ShapeDtypeStruct((TOP_K, t), F32),
            jax.ShapeDtypeStruct((TOP_K, t), jnp.int32),
            jax.ShapeDtypeStruct((N_EXPERTS, 1), jnp.int32),
        ],
        scratch_shapes=[pltpu.VMEM((N_EXPERTS, 1), F32)],
        compiler_params=_params(("arbitrary",)),
        name="outproj_router",
    )(y_lru, y_gdn, h0, wo1, wo2, g, b, w_router_t, rbias)


def _dest_kernel(e_ref, r_ref, ps_ref, d_ref):
    tm = e_ref.shape[1]
    ei = lax.broadcasted_iota(jnp.int32, (N_EXPERTS, tm), 0)
    rows = []
    for k in range(TOP_K):
        hit = ei == e_ref[k:k + 1, :]
        rows.append(jnp.sum(jnp.where(hit, ps_ref[...], 0), axis=0, keepdims=True))
    d_ref[...] = jnp.concatenate(rows, axis=0) + r_ref[...]


def _dest(top_e, rank, pad_start, tm):
    t = top_e.shape[1]
    blk = pl.BlockSpec((TOP_K, tm), lambda i: (0, i))
    return pl.pallas_call(
        _dest_kernel,
        grid=(t // tm,),
        in_specs=[blk, blk, pl.BlockSpec((N_EXPERTS, 1), lambda i: (0, 0))],
        out_specs=blk,
        out_shape=jax.ShapeDtypeStruct((TOP_K, t), jnp.int32),
        compiler_params=_params(("arbitrary",)),
        name="moe_dest",
    )(top_e, rank, pad_start)


def _sc_scatter_rows(rows, idx, n_out, chunk):
    n_copies, t = idx.shape
    d = rows.shape[1]
    per_worker = t // SC_WORKERS
    n_chunks = per_worker // chunk
    mesh = plsc.VectorSubcoreMesh(core_axis_name="c", subcore_axis_name="s")
    idx_flat = idx.reshape(n_copies * t)

    @functools.partial(
        pl.kernel, mesh=mesh,
        out_type=jax.ShapeDtypeStruct((n_out, d), rows.dtype),
        scratch_types=[pltpu.VMEM((chunk,), jnp.int32) for _ in range(n_copies)] + [
            pltpu.VMEM((chunk, d), rows.dtype),
            pltpu.SemaphoreType.DMA,
        ],
    )
    def scatter(rows_hbm, idx_hbm, out_hbm, *scratch):
        idx_v = scratch[:n_copies]
        rows_v, sem = scratch[n_copies:]
        wid = lax.axis_index("s") * SC_CORES + lax.axis_index("c")
        base = wid * per_worker

        @pl.loop(0, n_chunks)
        def _(j):
            off = base + j * chunk
            for k in range(n_copies):
                pltpu.sync_copy(idx_hbm.at[pl.ds(k * t + off, chunk)], idx_v[k])
            pltpu.sync_copy(rows_hbm.at[pl.ds(off, chunk)], rows_v)
            copies = [pltpu.async_copy(rows_v, out_hbm.at[idx_v[k]], sem) for k in range(n_copies)]
            for cp in copies:
                cp.wait()

    return scatter(rows, idx_flat)


def _expert_kernel(be_ref, nv_ref, first_ref, slot_ref, next_ref, nu_ref,
                   xs_ref, wg_hbm, wu_hbm, wd_hbm, ys_ref, wg_f, wu_f, wd_f, wgu_b, wd_b, sem):
    i = pl.program_id(0)

    def fetch(e, slot):
        return (pltpu.make_async_copy(wg_hbm.at[e], wg_f.at[slot], sem.at[slot]),
                pltpu.make_async_copy(wu_hbm.at[e], wu_f.at[slot], sem.at[slot]),
                pltpu.make_async_copy(wd_hbm.at[e], wd_f.at[slot], sem.at[slot]))

    @pl.when(i < nu_ref[0])
    def _():
        e = be_ref[i]
        slot = slot_ref[i]

        @pl.when(first_ref[i] == 1)
        def _():
            @pl.when(i == 0)
            def _():
                for cp in fetch(e, slot):
                    cp.start(priority=WEIGHT_DMA_PRIORITY)

            for cp in fetch(e, slot):
                cp.wait()

            @pl.when(next_ref[i] >= 0)
            def _():
                for cp in fetch(next_ref[i], 1 - slot):
                    cp.start(priority=WEIGHT_DMA_PRIORITY)

            wgu_b[:, :D_EXPERT] = wg_f[slot].astype(BF16)
            wgu_b[:, D_EXPERT:] = wu_f[slot].astype(BF16)
            wd_b[...] = wd_f[slot].astype(BF16)

        n = xs_ref.shape[0] // EXPERT_BANDS
        bands = [slice(j * n, (j + 1) * n) for j in range(EXPERT_BANDS)]
        row = lax.broadcasted_iota(jnp.int32, (n, D_PACK), 0)
        xs = [_unpack_rows(jnp.where(row + j * n < nv_ref[i], xs_ref[r, :], jnp.uint32(0)))
              for j, r in enumerate(bands)]
        gus = [_dot(x_hi.astype(BF16), wgu_b[:D_PACK, :]) + _dot(x_lo.astype(BF16), wgu_b[D_PACK:, :])
               for x_hi, x_lo in xs]
        hs = [_silu(gu[:, :D_EXPERT]) * gu[:, D_EXPERT:] for gu in gus]
        ys = [_dot(h.astype(BF16), wd_b[...]) for h in hs]
        for r, y in zip(bands, ys):
            ys_ref[r, :] = _pack_rows(y)


def _experts(blk_e, n_valid, first, slot, next_e, n_used, xs, w_gate, w_up, w_down):
    n_rows = xs.shape[0]
    n_blocks = n_rows // MOE_BLOCK
    blk = lambda i, be, nv, fi, sl, nx, nu: (jnp.minimum(i, nu[0] - 1), 0)
    return pl.pallas_call(
        _expert_kernel,
        grid_spec=pltpu.PrefetchScalarGridSpec(
            num_scalar_prefetch=6,
            grid=(n_used[0],),
            in_specs=[
                pl.BlockSpec((MOE_BLOCK, D_PACK), blk),
                pl.BlockSpec(memory_space=pl.ANY),
                pl.BlockSpec(memory_space=pl.ANY),
                pl.BlockSpec(memory_space=pl.ANY),
            ],
            out_specs=pl.BlockSpec((MOE_BLOCK, D_PACK), blk),
            scratch_shapes=[
                pltpu.VMEM((2, D_MODEL, D_EXPERT), F32),
                pltpu.VMEM((2, D_MODEL, D_EXPERT), F32),
                pltpu.VMEM((2, D_EXPERT, D_MODEL), F32),
                pltpu.VMEM((D_MODEL, 2 * D_EXPERT), BF16),
                pltpu.VMEM((D_EXPERT, D_MODEL), BF16),
                pltpu.SemaphoreType.DMA((2,)),
            ],
        ),
        out_shape=jax.ShapeDtypeStruct((n_rows, D_PACK), jnp.uint32),
        compiler_params=_params(("arbitrary",)),
        name="moe_experts",
    )(blk_e, n_valid, first, slot, next_e, n_used, xs, w_gate, w_up, w_down)


def _sc_gather_rows(table, idx, chunk):
    n_idx = idx.shape[0]
    d = table.shape[1]
    per_worker = n_idx // SC_WORKERS
    n_chunks = per_worker // chunk
    assert n_chunks % 2 == 0 and n_chunks * chunk * SC_WORKERS == n_idx
    mesh = plsc.VectorSubcoreMesh(core_axis_name="c", subcore_axis_name="s")

    @functools.partial(
        pl.kernel, mesh=mesh,
        out_type=jax.ShapeDtypeStruct((n_idx, d), table.dtype),
        scratch_types=[
            pltpu.VMEM((chunk,), jnp.int32), pltpu.VMEM((chunk,), jnp.int32),
            pltpu.VMEM((chunk, d), table.dtype), pltpu.VMEM((chunk, d), table.dtype),
            pltpu.SemaphoreType.DMA, pltpu.SemaphoreType.DMA, pltpu.SemaphoreType.DMA, pltpu.SemaphoreType.DMA,
        ],
    )
    def gather(table_hbm, idx_hbm, out_hbm, idx_v0, idx_v1, rows_v0, rows_v1, gsem0, gsem1, osem0, osem1):
        idx_v, rows_v, gsem, osem = (idx_v0, idx_v1), (rows_v0, rows_v1), (gsem0, gsem1), (osem0, osem1)
        wid = lax.axis_index("s") * SC_CORES + lax.axis_index("c")
        base = wid * per_worker

        def gather_copy(slot):
            return pltpu.make_async_copy(table_hbm.at[idx_v[slot]], rows_v[slot], gsem[slot])

        def out_copy(c, slot):
            return pltpu.make_async_copy(rows_v[slot], out_hbm.at[pl.ds(base + c * chunk, chunk)], osem[slot])

        def start_gather(c, slot):
            pltpu.sync_copy(idx_hbm.at[pl.ds(base + c * chunk, chunk)], idx_v[slot])
            gather_copy(slot).start()

        start_gather(0, 0)

        @pl.loop(0, n_chunks, step=2)
        def _(j):
            for b in range(2):
                c = j + b
                cur, other = b, 1 - b

                @pl.when(c >= 1)
                def _():
                    out_copy(c - 1, other).wait()

                @pl.when(c + 1 < n_chunks)
                def _():
                    start_gather(c + 1, other)

                gather_copy(cur).wait()
                out_copy(c, cur).start()

        out_copy(n_chunks - 1, 1).wait()

    return gather(table, idx)


def _combine_kernel(h1_ref, wts_ref, wsgu_ref, wsd_ref, g_ref, b_ref, yg_ref, out_ref):
    h1 = h1_ref[...]
    gu = _dot(h1.astype(BF16), wsgu_ref[...])
    hs = _silu(gu[:, :D_SHARED]) * gu[:, D_SHARED:]
    acc = DEEPNORM_ALPHA * h1 + _dot(hs.astype(BF16), wsd_ref[...])
    wts = wts_ref[...].T
    acc_hi = acc[:, :D_PACK]
    acc_lo = acc[:, D_PACK:]
    for k in range(TOP_K):
        y_hi, y_lo = _unpack_rows(yg_ref[k])
        acc_hi = acc_hi + y_hi * wts[:, k:k + 1]
        acc_lo = acc_lo + y_lo * wts[:, k:k + 1]
    out_ref[...] = _layer_norm(jnp.concatenate([acc_hi, acc_lo], axis=1), g_ref[...], b_ref[...])


def _combine(h1, wts, ws_gu, ws_down, g, b, yg, tm):
    t = h1.shape[0]
    const = lambda shape: pl.BlockSpec(shape, lambda i: (0,) * len(shape))
    return pl.pallas_call(
        _combine_kernel,
        grid=(t // tm,),
        in_specs=[
            pl.BlockSpec((tm, D_MODEL), lambda i: (i, 0)),
            pl.BlockSpec((TOP_K, tm), lambda i: (0, i)),
            const((D_MODEL, 2 * D_SHARED)), const((D_SHARED, D_MODEL)),
            const((1, D_MODEL)), const((1, D_MODEL)),
            pl.BlockSpec((TOP_K, tm, D_PACK), lambda i: (0, i, 0)),
        ],
        out_specs=pl.BlockSpec((tm, D_MODEL), lambda i: (i, 0)),
        out_shape=jax.ShapeDtypeStruct((t, D_MODEL), F32),
        compiler_params=_params(("arbitrary",)),
        name="moe_combine",
    )(h1, wts, ws_gu, ws_down, g, b, yg)


def _block_diag(w):
    nb, bi, bo = w.shape
    eye = jnp.eye(nb, dtype=w.dtype)
    return (eye[:, None, :, None] * w[:, :, None, :]).reshape(nb * bi, nb * bo)


def _pad_lanes(v, offset, width):
    return jnp.zeros((1, width), F32).at[0, offset:offset + v.shape[0]].set(v)


def _layer(h_in_x, l, p, tiles):
    bsz, seq, _ = h_in_x.shape
    t = bsz * seq
    row = lambda v: v.reshape(1, -1)

    w_in = p['w_in'][l]
    w_main = w_in.astype(BF16)
    w_small = jnp.zeros((D_MODEL, LANES), F32).at[:, :2 * GDN_HEADS].set(w_in[:, N_MAIN:])
    zeros = lambda n: jnp.zeros((CONV_WIDTH, n), F32)
    conv_w = jnp.concatenate([p['lru_conv_w'][l], zeros(LRU_WIDTH), p['gdn_conv_w'][l], zeros(GDN_V)], 1)
    conv_b = jnp.zeros((1, N_MAIN), F32).at[0, :LRU_WIDTH].set(p['lru_conv_b'][l])
    h0, proj, small_t = _inproj(h_in_x.reshape(t, D_MODEL), row(p['ln_g']), row(p['ln_b']),
                                       w_main, w_small, conv_w, conv_b, tiles['inproj'], seq)
    proj3 = proj.reshape(bsz, seq, N_MAIN)

    w_gates = jnp.concatenate([_block_diag(p['lru_w_rg'][l]), _block_diag(p['lru_w_ig'][l])], 1).astype(BF16)
    b_gates = jnp.concatenate([p['lru_b_rg'][l], p['lru_b_ig'][l]]).reshape(1, -1)
    rows = GDN_GROUP * GDN_CHUNK
    smallt3 = small_t.reshape(8, bsz, seq // rows, rows).transpose(1, 2, 0, 3)
    a_log, dt_bias = p['gdn_a_log'][l], p['gdn_dt_bias'][l]
    alc = _pad_lanes(a_log, GDN_HEADS, 8).reshape(8, 1)
    dtc = _pad_lanes(dt_bias, GDN_HEADS, 8).reshape(8, 1)
    y_lru, y_gdn = _mixers(proj3, smallt3, w_gates, b_gates, row(p['lru_lambda'][l]), row(p['lru_out_g'][l]),
                           alc, dtc, row(p['gdn_norm_w'][l]), tiles['gdn_nb'])

    w_out = p['w_out'][l].astype(BF16)
    h1, h1p, top_e, wts, rank, counts = _router(
        y_lru.reshape(t, LRU_WIDTH), y_gdn.reshape(t, GDN_V), h0, w_out[:LRU_WIDTH], w_out[LRU_WIDTH:],
        row(p['ln1_g'][l]), row(p['ln1_b'][l]), p['w_router'][l].T, p['router_bias'][l].reshape(-1, 1),
        tiles['router'])

    counts = counts[:, 0]
    padded = (counts + MOE_BLOCK - 1) // MOE_BLOCK * MOE_BLOCK
    pad_end = jnp.cumsum(padded)
    pad_start = pad_end - padded
    n_blocks = (t * TOP_K + N_EXPERTS * (MOE_BLOCK - 1)) // MOE_BLOCK
    n_rows = n_blocks * MOE_BLOCK
    n_used = (pad_end[-1] // MOE_BLOCK).astype(jnp.int32)
    blk_ids = jnp.minimum(jnp.arange(n_blocks, dtype=jnp.int32), n_used - 1)
    blk_e = jnp.minimum(jnp.sum(pad_end[None, :] <= (blk_ids * MOE_BLOCK)[:, None], axis=1),
                        N_EXPERTS - 1).astype(jnp.int32)

    dest = _dest(top_e, rank, pad_start.reshape(-1, 1), tiles['dest'])
    n_valid = jnp.clip(counts[blk_e] - (blk_ids * MOE_BLOCK - pad_start[blk_e]), 0, MOE_BLOCK).astype(jnp.int32)
    xs = _sc_scatter_rows(h1p, dest, n_rows, SC_SCATTER_CHUNK)
    active = jnp.arange(n_blocks, dtype=jnp.int32) < n_used
    first = (active & jnp.concatenate([jnp.ones((1,), bool), blk_e[1:] != blk_e[:-1]])).astype(jnp.int32)
    slot = ((jnp.cumsum(first) - 1) % 2).astype(jnp.int32)
    used = counts > 0
    later = jnp.where(used[None, :] & (jnp.arange(N_EXPERTS)[None, :] > jnp.arange(N_EXPERTS)[:, None]),
                      jnp.arange(N_EXPERTS, dtype=jnp.int32)[None, :], N_EXPERTS)
    next_used = jnp.min(later, axis=1)
    next_e = jnp.where(next_used < N_EXPERTS, next_used, -1)[blk_e].astype(jnp.int32)
    ys = _experts(blk_e, n_valid, first, slot, next_e, n_used.reshape(1), xs,
                  p['w_gate'][l], p['w_up'][l], p['w_down'][l])
    ws_gu = jnp.concatenate([p['ws_gate'][l], p['ws_up'][l]], 1).astype(BF16)
    yg = _sc_gather_rows(ys, dest.reshape(TOP_K * t), SC_GATHER_CHUNK).reshape(TOP_K, t, D_PACK)
    out = _combine(h1, wts, ws_gu, p['ws_down'][l].astype(BF16),
                   row(p['ln2_g'][l]), row(p['ln2_b'][l]), yg, tiles['combine'])
    return out.reshape(bsz, seq, D_MODEL)


def _tiles(bsz, seq):
    t = bsz * seq
    return {
        'inproj': min(512, t),
        'gdn_nb': bsz,
        'router': min(1024, t),
        'dest': min(512, t),
        'combine': min(512, t),
    }


def kernel(x, ln_in_g, ln_in_b, w_in, lru_conv_w, lru_conv_b, lru_w_rg, lru_b_rg, lru_w_ig, lru_b_ig,
           lru_lambda, lru_out_g, gdn_conv_w, gdn_a_log, gdn_dt_bias, gdn_norm_w, w_out, ln1_g, ln1_b,
           w_router, router_bias, w_gate, w_up, w_down, ws_gate, ws_up, ws_down, ln2_g, ln2_b):
    assert w_in.shape[0] == DEPTH == 1
    p = dict(ln_g=ln_in_g, ln_b=ln_in_b, w_in=w_in, lru_conv_w=lru_conv_w, lru_conv_b=lru_conv_b,
             lru_w_rg=lru_w_rg, lru_b_rg=lru_b_rg, lru_w_ig=lru_w_ig, lru_b_ig=lru_b_ig,
             lru_lambda=lru_lambda, lru_out_g=lru_out_g, gdn_conv_w=gdn_conv_w, gdn_a_log=gdn_a_log,
             gdn_dt_bias=gdn_dt_bias, gdn_norm_w=gdn_norm_w, w_out=w_out, ln1_g=ln1_g, ln1_b=ln1_b,
             w_router=w_router, router_bias=router_bias, w_gate=w_gate, w_up=w_up, w_down=w_down,
             ws_gate=ws_gate, ws_up=ws_up, ws_down=ws_down, ln2_g=ln2_g, ln2_b=ln2_b)
    bsz, seq, _ = x.shape
    return _layer(x, 0, p, _tiles(bsz, seq))
```
